```python
import jax, jax.numpy as jnp
from jax import lax
import numpy as np

D_MODEL = 1024
BATCH = 32
SEQ = 2048
DEPTH = 4

CTX_LEN = 256
GRID_W = 64
N_EVEN = (DEPTH + 1) // 2
N_ODD = DEPTH // 2
RET_HEADS = 4
RET_DK = 128
RET_DV = 128
RET_CHUNK = 128
ATT_HEADS = 4
ATT_KV_HEADS = 2
ATT_HD = 128
Q_BLOCK = 128
ROPE_BASE = 10000.0
CM_WIDTH = 1024
CM_GROUPS = 8
CM_GROUP_DIM = CM_WIDTH // CM_GROUPS
CM_CHUNK = 128
FF_HIDDEN = 4 * D_MODEL
EPS = 1e-6
AB_SIZES = (RET_HEADS * RET_DK, RET_HEADS * RET_DK, RET_HEADS * RET_DV, RET_HEADS * RET_DV,
            ATT_HEADS * ATT_HD, ATT_KV_HEADS * ATT_HD, ATT_KV_HEADS * ATT_HD)
AB_IN_W = sum(AB_SIZES)
AB_OUT_W = RET_HEADS * RET_DV + ATT_HEADS * ATT_HD

kernel_name = 'hybrid_retention_gqa_chunkmlp_dit'


def rms_norm(x, g):
    xf = x.astype(jnp.float32)
    y = xf * lax.rsqrt(jnp.mean(xf * xf, axis=-1, keepdims=True) + EPS)
    return (y * g.astype(jnp.float32)).astype(x.dtype)


def modulate(h, shift, scale):
    return h * (1.0 + scale) + shift


def grid_rope(L, hd):
    rows = L // GRID_W
    row = jnp.repeat(jnp.arange(rows, dtype=jnp.float32), GRID_W)
    col = jnp.tile(jnp.arange(GRID_W, dtype=jnp.float32), rows)
    n_freq = hd // 4
    inv = ROPE_BASE ** (-jnp.arange(n_freq, dtype=jnp.float32) / n_freq)
    ang = jnp.concatenate([row[:, None] * inv[None, :], col[:, None] * inv[None, :]], axis=-1)
    return jnp.cos(ang), jnp.sin(ang)


def apply_rope(x, cos, sin):
    half = x.shape[-1] // 2
    x1, x2 = x[..., :half], x[..., half:]
    cs, sn = cos[None, :, None, :], sin[None, :, None, :]
    out = jnp.concatenate([x1 * cs - x2 * sn, x1 * sn + x2 * cs], axis=-1)
    return out.astype(x.dtype)


def split_heads(t, n, d):
    return t.reshape(t.shape[0], t.shape[1], n, d)


def split_ab(p):
    out, start = [], 0
    for s in AB_SIZES:
        out.append(p[..., start:start + s])
        start += s
    return out


def retention_chunkwise(q, k, v, log_gamma, state0):
    B, L, H, dk = q.shape
    dv = v.shape[-1]
    C = RET_CHUNK
    N = L // C
    qc = q.reshape(B, N, C, H, dk)
    kc = k.reshape(B, N, C, H, dk)
    vc = v.reshape(B, N, C, H, dv)
    pos = jnp.arange(C, dtype=jnp.float32)
    rel = pos[:, None] - pos[None, :]
    decay = jnp.where((rel >= 0)[None], jnp.exp(jnp.maximum(rel, 0.0)[None] * log_gamma[:, None, None]), 0.0)
    scores = jnp.einsum('bnihd,bnjhd->bnhij', qc, kc) * decay
    intra = jnp.einsum('bnhij,bnjhe->bnihe', scores, vc)
    k_decay = jnp.exp((C - 1.0 - pos)[:, None] * log_gamma[None, :])
    q_decay = jnp.exp((pos + 1.0)[:, None] * log_gamma[None, :])
    chunk_decay = jnp.exp(C * log_gamma)[None, :, None, None]
    chunk_kv = jnp.einsum('bnjhd,jh,bnjhe->nbhde', kc, k_decay, vc)

    def step(state, kv_n):
        return chunk_decay * state + kv_n, state

    state_final, state_prev = lax.scan(step, state0, chunk_kv)
    cross = jnp.einsum('bnihd,nbhde->bnihe', qc, state_prev) * q_decay[None, None, :, :, None]
    return (intra + cross).reshape(B, L, H, dv), state_final


def retention_out(o, g):
    B, L, H, dv = o.shape
    of = o.astype(jnp.float32)
    mu = jnp.mean(of, axis=-1, keepdims=True)
    var = jnp.mean(jnp.square(of - mu), axis=-1, keepdims=True)
    y = ((of - mu) * lax.rsqrt(var + EPS)).reshape(B, L, H * dv)
    return y.astype(g.dtype) * jax.nn.silu(g)


def block_attention(q, k, v):
    B, Lq, H, hd = q.shape
    KV = k.shape[2]
    G = H // KV
    nb = Lq // Q_BLOCK
    scale = hd ** -0.5
    qb = q.reshape(B, nb, Q_BLOCK, KV, G, hd).transpose(1, 0, 2, 3, 4, 5)

    def one_block(qi):
        s = jnp.einsum('bqkgd,bskd->bkgqs', qi, k).astype(jnp.float32) * scale
        p = jax.nn.softmax(s, axis=-1)
        return jnp.einsum('bkgqs,bskd->bqkgd', p.astype(v.dtype), v)

    o = lax.map(one_block, qb)
    return o.transpose(1, 0, 2, 3, 4, 5).reshape(B, Lq, H * hd)


def mix_ab(h_lat, h_ctx, w_in, w_out, ret_decay, q_g, k_g):
    B, L, _ = h_lat.shape
    cos_r, sin_r = grid_rope(L, RET_DK)
    cos_a, sin_a = grid_rope(L, ATT_HD)
    lat = split_ab(h_lat @ w_in)
    ctx = split_ab(h_ctx @ w_in)
    flip = lambda t: jnp.flip(t, axis=1)
    rq_l = apply_rope(split_heads(lat[0], RET_HEADS, RET_DK), cos_r, sin_r)
    rk_l = apply_rope(split_heads(lat[1], RET_HEADS, RET_DK), cos_r, sin_r) * (RET_DK ** -0.5)
    rv_l = split_heads(lat[2], RET_HEADS, RET_DV)
    rq_c = split_heads(ctx[0], RET_HEADS, RET_DK)
    rk_c = split_heads(ctx[1], RET_HEADS, RET_DK) * (RET_DK ** -0.5)
    rv_c = split_heads(ctx[2], RET_HEADS, RET_DV)
    log_gamma = jax.nn.log_sigmoid(ret_decay.astype(jnp.float32))
    zero = jnp.zeros((B, RET_HEADS, RET_DK, RET_DV), jnp.float32)
    oc_f, st_f = retention_chunkwise(rq_c, rk_c, rv_c, log_gamma[0], zero)
    ol_f, _ = retention_chunkwise(rq_l, rk_l, rv_l, log_gamma[0], st_f)
    oc_b, st_b = retention_chunkwise(flip(rq_c), flip(rk_c), flip(rv_c), log_gamma[1], zero)
    ol_b, _ = retention_chunkwise(flip(rq_l), flip(rk_l), flip(rv_l), log_gamma[1], st_b)
    ret_l = retention_out(ol_f + flip(ol_b), lat[3])
    ret_c = retention_out(oc_f + flip(oc_b), ctx[3])
    aq_l = apply_rope(rms_norm(split_heads(lat[4], ATT_HEADS, ATT_HD), q_g), cos_a, sin_a)
    ak_l = apply_rope(rms_norm(split_heads(lat[5], ATT_KV_HEADS, ATT_HD), k_g), cos_a, sin_a)
    av_l = split_heads(lat[6], ATT_KV_HEADS, ATT_HD)
    aq_c = rms_norm(split_heads(ctx[4], ATT_HEADS, ATT_HD), q_g)
    ak_c = rms_norm(split_heads(ctx[5], ATT_KV_HEADS, ATT_HD), k_g)
    av_c = split_heads(ctx[6], ATT_KV_HEADS, ATT_HD)
    att_l = block_attention(aq_l, jnp.concatenate([ak_c, ak_l], axis=1), jnp.concatenate([av_c, av_l], axis=1))
    att_c = block_attention(aq_c, ak_c, av_c)
    out_l = jnp.concatenate([ret_l, att_l], axis=-1) @ w_out
    out_c = jnp.concatenate([ret_c, att_c], axis=-1) @ w_out
    return out_l, out_c


def mix_chunk_mlp(h, w_in, v_g, w_s, b_s, w_out):
    B, L, _ = h.shape
    z = jax.nn.gelu(h @ w_in)
    u, v = z[..., :CM_WIDTH], z[..., CM_WIDTH:]
    v = rms_norm(v, v_g)
    vc = v.reshape(B, L // CM_CHUNK, CM_CHUNK, CM_GROUPS, CM_GROUP_DIM)
    sv = jnp.einsum('gpq,bnqgd->bnpgd', w_s, vc) + b_s.T[None, None, :, :, None]
    return (u * sv.reshape(B, L, CM_WIDTH)) @ w_out


def sq_relu_mlp(h, w1, w2):
    return jnp.square(jax.nn.relu(h @ w1)) @ w2


def _fwd_setup_inputs(seed: int = 0) -> dict:
    key = jax.random.key(seed)
    ks = jax.random.split(key, 24)
    f32 = jnp.float32

    def nrm(k, shape, scale):
        return jax.random.normal(k, shape, f32) * scale

    def gain(k, shape):
        return 1.0 + 0.01 * jax.random.normal(k, shape, f32)

    base = 1.0 - 2.0 ** (-5.0 - jnp.arange(RET_HEADS, dtype=f32))
    logit = jnp.log(base) - jnp.log1p(-base)
    ret_decay = logit[None, None, :] + 0.05 * jax.random.normal(ks[10], (N_EVEN, 2, RET_HEADS), f32)
    return {
        'x': nrm(ks[0], (BATCH, SEQ, D_MODEL), 1.0),
        'c': nrm(ks[1], (BATCH, D_MODEL), 1.0),
        'ctx': nrm(ks[2], (BATCH, CTX_LEN, D_MODEL), 1.0),
        'c_ctx': nrm(ks[3], (D_MODEL,), 1.0),
        'mod_w': nrm(ks[4], (DEPTH, D_MODEL, 6 * D_MODEL), D_MODEL ** -0.5),
        'mod_b': nrm(ks[5], (DEPTH, 6 * D_MODEL), 0.01),
        'norm1_g': gain(ks[6], (DEPTH, D_MODEL)),
        'norm2_g': gain(ks[7], (DEPTH, D_MODEL)),
        'ab_w_in': nrm(ks[8], (N_EVEN, D_MODEL, AB_IN_W), D_MODEL ** -0.5),
        'ab_w_out': nrm(ks[9], (N_EVEN, AB_OUT_W, D_MODEL), AB_OUT_W ** -0.5),
        'ret_decay': ret_decay,
        'att_q_norm_g': gain(ks[11], (N_EVEN, ATT_HD)),
        'att_k_norm_g': gain(ks[12], (N_EVEN, ATT_HD)),
        'cm_w_in': nrm(ks[13], (N_ODD, D_MODEL, 2 * CM_WIDTH), D_MODEL ** -0.5),
        'cm_v_norm_g': gain(ks[14], (N_ODD, CM_WIDTH)),
        'cm_w_s': nrm(ks[15], (N_ODD, CM_GROUPS, CM_CHUNK, CM_CHUNK), CM_CHUNK ** -0.5),
        'cm_b_s': gain(ks[16], (N_ODD, CM_GROUPS, CM_CHUNK)),
        'cm_w_out': nrm(ks[17], (N_ODD, CM_WIDTH, D_MODEL), CM_WIDTH ** -0.5),
        'ff_w1': nrm(ks[18], (DEPTH, D_MODEL, FF_HIDDEN), D_MODEL ** -0.5),
        'ff_w2': nrm(ks[19], (DEPTH, FF_HIDDEN, D_MODEL), FF_HIDDEN ** -0.5),
    }


def _fwd_reference(x, c, ctx, c_ctx, mod_w, mod_b, norm1_g, norm2_g, ab_w_in, ab_w_out, ret_decay,
              att_q_norm_g, att_k_norm_g, cm_w_in, cm_v_norm_g, cm_w_s, cm_b_s, cm_w_out, ff_w1, ff_w2):
    silu_c = jax.nn.silu(c)
    silu_cc = jax.nn.silu(c_ctx)
    h_stream = ctx
    for l in range(DEPTH):
        last = l == DEPTH - 1
        is_even = l % 2 == 0
        i = l // 2
        mod_lat = (silu_c @ mod_w[l] + mod_b[l])[:, None, :]
        mod_ctx = silu_cc @ mod_w[l] + mod_b[l]
        sh1, sc1, g1, sh2, sc2, g2 = jnp.split(mod_lat, 6, axis=-1)
        csh1, csc1, cg1, csh2, csc2, cg2 = jnp.split(mod_ctx, 6, axis=-1)
        h_lat = modulate(rms_norm(x, norm1_g[l]), sh1, sc1)
        if is_even or not last:
            h_ctx = modulate(rms_norm(h_stream, norm1_g[l]), csh1, csc1)
        if is_even:
            o_lat, o_ctx = mix_ab(h_lat, h_ctx, ab_w_in[i], ab_w_out[i], ret_decay[i],
                                  att_q_norm_g[i], att_k_norm_g[i])
        else:
            o_lat = mix_chunk_mlp(h_lat, cm_w_in[i], cm_v_norm_g[i], cm_w_s[i], cm_b_s[i], cm_w_out[i])
            if not last:
                o_ctx = mix_chunk_mlp(h_ctx, cm_w_in[i], cm_v_norm_g[i], cm_w_s[i], cm_b_s[i], cm_w_out[i])
        x = x + g1 * o_lat
        x = x + g2 * sq_relu_mlp(modulate(rms_norm(x, norm2_g[l]), sh2, sc2), ff_w1[l], ff_w2[l])
        if not last:
            h_stream = h_stream + cg1 * o_ctx
            h_stream = h_stream + cg2 * sq_relu_mlp(modulate(rms_norm(h_stream, norm2_g[l]), csh2, csc2),
                                                     ff_w1[l], ff_w2[l])
    return x


import jax as _jax
import jax.numpy as _jnp

TWIN_FORMAT = 'train_step'
FWD_PARAMS = ['x', 'c', 'ctx', 'c_ctx', 'mod_w', 'mod_b', 'norm1_g', 'norm2_g', 'ab_w_in', 'ab_w_out', 'ret_decay', 'att_q_norm_g', 'att_k_norm_g', 'cm_w_in', 'cm_v_norm_g', 'cm_w_s', 'cm_b_s', 'cm_w_out', 'ff_w1', 'ff_w2']
TWIN_WEIGHTS = ['c_ctx', 'mod_w', 'mod_b', 'norm1_g', 'norm2_g', 'ab_w_in', 'ab_w_out', 'ret_decay', 'att_q_norm_g', 'att_k_norm_g', 'cm_w_in', 'cm_v_norm_g', 'cm_w_s', 'cm_b_s', 'cm_w_out', 'ff_w1', 'ff_w2']
TWIN_DIFF_INPUT = 'x'
TWIN_INPUTS = ['x', 'c', 'ctx', 'c_ctx', 'mod_w', 'mod_b', 'norm1_g', 'norm2_g', 'ab_w_in', 'ab_w_out', 'ret_decay', 'att_q_norm_g', 'att_k_norm_g', 'cm_w_in', 'cm_v_norm_g', 'cm_w_s', 'cm_b_s', 'cm_w_out', 'ff_w1', 'ff_w2', 'loss_target', 'm_c_ctx', 'm_mod_w', 'm_mod_b', 'm_norm1_g', 'm_norm2_g', 'm_ab_w_in', 'm_ab_w_out', 'm_ret_decay', 'm_att_q_norm_g', 'm_att_k_norm_g', 'm_cm_w_in', 'm_cm_v_norm_g', 'm_cm_w_s', 'm_cm_b_s', 'm_cm_w_out', 'm_ff_w1', 'm_ff_w2', 'v_c_ctx', 'v_mod_w', 'v_mod_b', 'v_norm1_g', 'v_norm2_g', 'v_ab_w_in', 'v_ab_w_out', 'v_ret_decay', 'v_att_q_norm_g', 'v_att_k_norm_g', 'v_cm_w_in', 'v_cm_v_norm_g', 'v_cm_w_s', 'v_cm_b_s', 'v_cm_w_out', 'v_ff_w1', 'v_ff_w2']
TWIN_OUTPUTS = ['loss', 'grad_x', 'grad_c_ctx', 'grad_mod_w', 'grad_mod_b', 'grad_norm1_g', 'grad_norm2_g', 'grad_ab_w_in', 'grad_ab_w_out', 'grad_ret_decay', 'grad_att_q_norm_g', 'grad_att_k_norm_g', 'grad_cm_w_in', 'grad_cm_v_norm_g', 'grad_cm_w_s', 'grad_cm_b_s', 'grad_cm_w_out', 'grad_ff_w1', 'grad_ff_w2', 'delta_c_ctx', 'delta_mod_w', 'delta_mod_b', 'delta_norm1_g', 'delta_norm2_g', 'delta_ab_w_in', 'delta_ab_w_out', 'delta_ret_decay', 'delta_att_q_norm_g', 'delta_att_k_norm_g', 'delta_cm_w_in', 'delta_cm_v_norm_g', 'delta_cm_w_s', 'delta_cm_b_s', 'delta_cm_w_out', 'delta_ff_w1', 'delta_ff_w2', 'new_m_c_ctx', 'new_m_mod_w', 'new_m_mod_b', 'new_m_norm1_g', 'new_m_norm2_g', 'new_m_ab_w_in', 'new_m_ab_w_out', 'new_m_ret_decay', 'new_m_att_q_norm_g', 'new_m_att_k_norm_g', 'new_m_cm_w_in', 'new_m_cm_v_norm_g', 'new_m_cm_w_s', 'new_m_cm_b_s', 'new_m_cm_w_out', 'new_m_ff_w1', 'new_m_ff_w2', 'new_v_c_ctx', 'new_v_mod_w', 'new_v_mod_b', 'new_v_norm1_g', 'new_v_norm2_g', 'new_v_ab_w_in', 'new_v_ab_w_out', 'new_v_ret_decay', 'new_v_att_q_norm_g', 'new_v_att_k_norm_g', 'new_v_cm_w_in', 'new_v_cm_v_norm_g', 'new_v_cm_w_s', 'new_v_cm_b_s', 'new_v_cm_w_out', 'new_v_ff_w1', 'new_v_ff_w2']
TWIN_LEAF_KINDS = {'loss': 'loss', 'grad_x': 'grad_x', 'grad_c_ctx': 'grad_w', 'grad_mod_w': 'grad_w', 'grad_mod_b': 'grad_w', 'grad_norm1_g': 'grad_w', 'grad_norm2_g': 'grad_w', 'grad_ab_w_in': 'grad_w', 'grad_ab_w_out': 'grad_w', 'grad_ret_decay': 'grad_w', 'grad_att_q_norm_g': 'grad_w', 'grad_att_k_norm_g': 'grad_w', 'grad_cm_w_in': 'grad_w', 'grad_cm_v_norm_g': 'grad_w', 'grad_cm_w_s': 'grad_w', 'grad_cm_b_s': 'grad_w', 'grad_cm_w_out': 'grad_w', 'grad_ff_w1': 'grad_w', 'grad_ff_w2': 'grad_w', 'delta_c_ctx': 'delta_w', 'delta_mod_w': 'delta_w', 'delta_mod_b': 'delta_w', 'delta_norm1_g': 'delta_w', 'delta_norm2_g': 'delta_w', 'delta_ab_w_in': 'delta_w', 'delta_ab_w_out': 'delta_w', 'delta_ret_decay': 'delta_w', 'delta_att_q_norm_g': 'delta_w', 'delta_att_k_norm_g': 'delta_w', 'delta_cm_w_in': 'delta_w', 'delta_cm_v_norm_g': 'delta_w', 'delta_cm_w_s': 'delta_w', 'delta_cm_b_s': 'delta_w', 'delta_cm_w_out': 'delta_w', 'delta_ff_w1': 'delta_w', 'delta_ff_w2': 'delta_w', 'new_m_c_ctx': 'new_m', 'new_m_mod_w': 'new_m', 'new_m_mod_b': 'new_m', 'new_m_norm1_g': 'new_m', 'new_m_norm2_g': 'new_m', 'new_m_ab_w_in': 'new_m', 'new_m_ab_w_out': 'new_m', 'new_m_ret_decay': 'new_m', 'new_m_att_q_norm_g': 'new_m', 'new_m_att_k_norm_g': 'new_m', 'new_m_cm_w_in': 'new_m', 'new_m_cm_v_norm_g': 'new_m', 'new_m_cm_w_s': 'new_m', 'new_m_cm_b_s': 'new_m', 'new_m_cm_w_out': 'new_m', 'new_m_ff_w1': 'new_m', 'new_m_ff_w2': 'new_m', 'new_v_c_ctx': 'new_v', 'new_v_mod_w': 'new_v', 'new_v_mod_b': 'new_v', 'new_v_norm1_g': 'new_v', 'new_v_norm2_g': 'new_v', 'new_v_ab_w_in': 'new_v', 'new_v_ab_w_out': 'new_v', 'new_v_ret_decay': 'new_v', 'new_v_att_q_norm_g': 'new_v', 'new_v_att_k_norm_g': 'new_v', 'new_v_cm_w_in': 'new_v', 'new_v_cm_v_norm_g': 'new_v', 'new_v_cm_w_s': 'new_v', 'new_v_cm_b_s': 'new_v', 'new_v_cm_w_out': 'new_v', 'new_v_ff_w1': 'new_v', 'new_v_ff_w2': 'new_v'}


def _forward(args):
    return _fwd_reference(*[args[k] for k in FWD_PARAMS])


def _output_shape():
    out = _jax.eval_shape(lambda: _forward(_fwd_setup_inputs(0)))
    return out.shape, out.dtype

N_MICROBATCH = 1
ADAM_LR = 0.001
ADAM_B1 = 0.9
ADAM_B2 = 0.999
ADAM_EPS = 1e-08
ADAM_WD = 0.01
ADAM_STEP = 10
PER_EXAMPLE_BATCH_AXIS = {'x': 0, 'c': 0, 'ctx': 0, 'loss_target': 0}
SHARED_INPUTS = []
_WEIGHT_DTYPES = {'c_ctx': _jnp.float32, 'mod_w': _jnp.float32, 'mod_b': _jnp.float32, 'norm1_g': _jnp.float32, 'norm2_g': _jnp.float32, 'ab_w_in': _jnp.float32, 'ab_w_out': _jnp.float32, 'ret_decay': _jnp.float32, 'att_q_norm_g': _jnp.float32, 'att_k_norm_g': _jnp.float32, 'cm_w_in': _jnp.float32, 'cm_v_norm_g': _jnp.float32, 'cm_w_s': _jnp.float32, 'cm_b_s': _jnp.float32, 'cm_w_out': _jnp.float32, 'ff_w1': _jnp.float32, 'ff_w2': _jnp.float32}
MOMENT_SCALE = {'c_ctx': 5.064458e+00, 'mod_w': 6.283189e+01, 'mod_b': 1.181695e+02, 'norm1_g': 3.840533e+01, 'norm2_g': 2.333951e+02, 'ab_w_in': 2.733780e+01, 'ab_w_out': 3.366476e+01, 'ret_decay': 1.749762e+01, 'att_q_norm_g': 4.260057e+00, 'att_k_norm_g': 4.232533e+00, 'cm_w_in': 2.441838e+01, 'cm_v_norm_g': 3.221964e+01, 'cm_w_s': 1.315035e+01, 'cm_b_s': 2.047042e+01, 'cm_w_out': 3.796688e+01, 'ff_w1': 3.801018e+01, 'ff_w2': 8.454650e+01}


def _to_microbatches(a, axis):
    t = _jnp.moveaxis(a, axis, 0)
    t = t.reshape((N_MICROBATCH, t.shape[0] // N_MICROBATCH) + t.shape[1:])
    return _jnp.moveaxis(t, 1, axis + 1)


def setup_inputs(seed: int = 0) -> dict:
    inp = _fwd_setup_inputs(seed)
    key = _jax.random.fold_in(_jax.random.key(seed), 7919)
    shape, _ = _output_shape()
    out = dict(inp)
    out["loss_target"] = _jax.random.normal(_jax.random.fold_in(key, 0), shape, _jnp.float32)
    for i, name in enumerate(TWIN_WEIGHTS):
        w = inp[name].astype(_jnp.float32)
        if MOMENT_SCALE is None:
            s = _jnp.sqrt(_jnp.mean(_jnp.square(w)) + 1e-30)
        else:
            s = MOMENT_SCALE[name]
        km, kv = _jax.random.split(_jax.random.fold_in(key, i + 1))
        out[name] = w
        out["m_" + name] = s * _jax.random.normal(km, w.shape, _jnp.float32)
        out["v_" + name] = (s * s) * _jax.random.uniform(kv, w.shape, _jnp.float32, 0.5, 1.5)
    if N_MICROBATCH > 1:
        for name, axis in PER_EXAMPLE_BATCH_AXIS.items():
            out[name] = _to_microbatches(out[name], axis)
    return {'x': out['x'], 'c': out['c'], 'ctx': out['ctx'], 'c_ctx': out['c_ctx'], 'mod_w': out['mod_w'], 'mod_b': out['mod_b'], 'norm1_g': out['norm1_g'], 'norm2_g': out['norm2_g'], 'ab_w_in': out['ab_w_in'], 'ab_w_out': out['ab_w_out'], 'ret_decay': out['ret_decay'], 'att_q_norm_g': out['att_q_norm_g'], 'att_k_norm_g': out['att_k_norm_g'], 'cm_w_in': out['cm_w_in'], 'cm_v_norm_g': out['cm_v_norm_g'], 'cm_w_s': out['cm_w_s'], 'cm_b_s': out['cm_b_s'], 'cm_w_out': out['cm_w_out'], 'ff_w1': out['ff_w1'], 'ff_w2': out['ff_w2'], 'loss_target': out['loss_target'], 'm_c_ctx': out['m_c_ctx'], 'm_mod_w': out['m_mod_w'], 'm_mod_b': out['m_mod_b'], 'm_norm1_g': out['m_norm1_g'], 'm_norm2_g': out['m_norm2_g'], 'm_ab_w_in': out['m_ab_w_in'], 'm_ab_w_out': out['m_ab_w_out'], 'm_ret_decay': out['m_ret_decay'], 'm_att_q_norm_g': out['m_att_q_norm_g'], 'm_att_k_norm_g': out['m_att_k_norm_g'], 'm_cm_w_in': out['m_cm_w_in'], 'm_cm_v_norm_g': out['m_cm_v_norm_g'], 'm_cm_w_s': out['m_cm_w_s'], 'm_cm_b_s': out['m_cm_b_s'], 'm_cm_w_out': out['m_cm_w_out'], 'm_ff_w1': out['m_ff_w1'], 'm_ff_w2': out['m_ff_w2'], 'v_c_ctx': out['v_c_ctx'], 'v_mod_w': out['v_mod_w'], 'v_mod_b': out['v_mod_b'], 'v_norm1_g': out['v_norm1_g'], 'v_norm2_g': out['v_norm2_g'], 'v_ab_w_in': out['v_ab_w_in'], 'v_ab_w_out': out['v_ab_w_out'], 'v_ret_decay': out['v_ret_decay'], 'v_att_q_norm_g': out['v_att_q_norm_g'], 'v_att_k_norm_g': out['v_att_k_norm_g'], 'v_cm_w_in': out['v_cm_w_in'], 'v_cm_v_norm_g': out['v_cm_v_norm_g'], 'v_cm_w_s': out['v_cm_w_s'], 'v_cm_b_s': out['v_cm_b_s'], 'v_cm_w_out': out['v_cm_w_out'], 'v_ff_w1': out['v_ff_w1'], 'v_ff_w2': out['v_ff_w2']}


def _loss(weights, diff, rest, loss_target):
    with _jax.named_scope("forward"):
        args = {**rest, TWIN_DIFF_INPUT: diff, **{k: w.astype(_WEIGHT_DTYPES[k]) for k, w in weights.items()}}
        y = _forward(args)
    with _jax.named_scope("loss_head"):
        err = _jnp.square(y.astype(_jnp.float32) - loss_target)
        return 0.5 * _jnp.sum(_jnp.mean(err, axis=-1)) if err.ndim else 0.5 * err


def _adamw(w, g, m, v):
    m = ADAM_B1 * m + (1.0 - ADAM_B1) * g
    v = ADAM_B2 * v + (1.0 - ADAM_B2) * _jnp.square(g)
    m_hat = m / (1.0 - ADAM_B1 ** ADAM_STEP)
    v_hat = v / (1.0 - ADAM_B2 ** ADAM_STEP)
    delta = -ADAM_LR * (m_hat / (_jnp.sqrt(v_hat) + ADAM_EPS) + ADAM_WD * w)
    return delta, m, v


def reference(x, c, ctx, c_ctx, mod_w, mod_b, norm1_g, norm2_g, ab_w_in, ab_w_out, ret_decay, att_q_norm_g, att_k_norm_g, cm_w_in, cm_v_norm_g, cm_w_s, cm_b_s, cm_w_out, ff_w1, ff_w2, loss_target, m_c_ctx, m_mod_w, m_mod_b, m_norm1_g, m_norm2_g, m_ab_w_in, m_ab_w_out, m_ret_decay, m_att_q_norm_g, m_att_k_norm_g, m_cm_w_in, m_cm_v_norm_g, m_cm_w_s, m_cm_b_s, m_cm_w_out, m_ff_w1, m_ff_w2, v_c_ctx, v_mod_w, v_mod_b, v_norm1_g, v_norm2_g, v_ab_w_in, v_ab_w_out, v_ret_decay, v_att_q_norm_g, v_att_k_norm_g, v_cm_w_in, v_cm_v_norm_g, v_cm_w_s, v_cm_b_s, v_cm_w_out, v_ff_w1, v_ff_w2):
    given = dict(x=x, c=c, ctx=ctx, c_ctx=c_ctx, mod_w=mod_w, mod_b=mod_b, norm1_g=norm1_g, norm2_g=norm2_g, ab_w_in=ab_w_in, ab_w_out=ab_w_out, ret_decay=ret_decay, att_q_norm_g=att_q_norm_g, att_k_norm_g=att_k_norm_g, cm_w_in=cm_w_in, cm_v_norm_g=cm_v_norm_g, cm_w_s=cm_w_s, cm_b_s=cm_b_s, cm_w_out=cm_w_out, ff_w1=ff_w1, ff_w2=ff_w2, loss_target=loss_target, m_c_ctx=m_c_ctx, m_mod_w=m_mod_w, m_mod_b=m_mod_b, m_norm1_g=m_norm1_g, m_norm2_g=m_norm2_g, m_ab_w_in=m_ab_w_in, m_ab_w_out=m_ab_w_out, m_ret_decay=m_ret_decay, m_att_q_norm_g=m_att_q_norm_g, m_att_k_norm_g=m_att_k_norm_g, m_cm_w_in=m_cm_w_in, m_cm_v_norm_g=m_cm_v_norm_g, m_cm_w_s=m_cm_w_s, m_cm_b_s=m_cm_b_s, m_cm_w_out=m_cm_w_out, m_ff_w1=m_ff_w1, m_ff_w2=m_ff_w2, v_c_ctx=v_c_ctx, v_mod_w=v_mod_w, v_mod_b=v_mod_b, v_norm1_g=v_norm1_g, v_norm2_g=v_norm2_g, v_ab_w_in=v_ab_w_in, v_ab_w_out=v_ab_w_out, v_ret_decay=v_ret_decay, v_att_q_norm_g=v_att_q_norm_g, v_att_k_norm_g=v_att_k_norm_g, v_cm_w_in=v_cm_w_in, v_cm_v_norm_g=v_cm_v_norm_g, v_cm_w_s=v_cm_w_s, v_cm_b_s=v_cm_b_s, v_cm_w_out=v_cm_w_out, v_ff_w1=v_ff_w1, v_ff_w2=v_ff_w2)
    weights = {n: given[n] for n in TWIN_WEIGHTS}
    shared = {n: given[n] for n in SHARED_INPUTS}
    per_example = {n: given[n] for n in ['x', 'c', 'ctx']}
    grad_fn = _jax.value_and_grad(_loss, argnums=(0, 1))

    def one_microbatch(ex, loss_target):
        ex = dict(ex)
        diff = ex.pop(TWIN_DIFF_INPUT)
        return grad_fn(weights, diff, {**shared, **ex}, loss_target)

    if N_MICROBATCH == 1:
        loss, (grad_w, grad_x) = one_microbatch(per_example, given["loss_target"])
    else:
        def body(carry, xs):
            loss_sum, grad_sum = carry
            l_k, (gw_k, gx_k) = one_microbatch(xs[0], xs[1])
            with _jax.named_scope("update"):
                return (loss_sum + l_k, _jax.tree.map(_jnp.add, grad_sum, gw_k)), gx_k

        init = (_jnp.zeros((), _jnp.float32), _jax.tree.map(_jnp.zeros_like, weights))
        (loss, grad_w), grad_x = _jax.lax.scan(body, init, (per_example, given["loss_target"]))
    with _jax.named_scope("update"):
        delta_w, new_m, new_v = {}, {}, {}
        for n in TWIN_WEIGHTS:
            delta_w[n], new_m[n], new_v[n] = _adamw(weights[n], grad_w[n], given["m_" + n], given["v_" + n])
    return (loss, grad_x, *[grad_w[n] for n in TWIN_WEIGHTS], *[delta_w[n] for n in TWIN_WEIGHTS],
            *[new_m[n] for n in TWIN_WEIGHTS], *[new_v[n] for n in TWIN_WEIGHTS])
```

```python
import functools
import math
from typing import NamedTuple

import jax
import jax.numpy as jnp
from jax import lax
from jax.experimental import pallas as pl
from jax.experimental.pallas import tpu as pltpu

F32 = jnp.float32
BF16 = jnp.bfloat16
EPS = 1e-6
ROPE_BASE = 10000.0
LANES = 128
CHUNK = 128
N_LAYERS = 4
VMEM_LIMIT = 56 * 1024 * 1024

ADAM_LR = 0.001
ADAM_B1 = 0.9
ADAM_B2 = 0.999
ADAM_EPS = 1e-08
ADAM_WD = 0.01
ADAM_STEP = 10


class Cfg(NamedTuple):
    B: int = 4
    SC: int = 256
    SL: int = 2048
    D: int = 1024
    FF: int = 4096
    GRID_W: int = 64
    H: int = 4
    KV: int = 2
    CMW: int = 1024
    CMG: int = 8

    @property
    def S(self):
        return self.SC + self.SL

    @property
    def T(self):
        return self.B * self.S

    @property
    def TM(self):
        return self.SC

    @property
    def TPE(self):
        return self.S // self.SC

    @property
    def ABW(self):
        return (5 * self.H + 2 * self.KV) * CHUNK


def _tile(dim, pref):
    t = min(dim, pref)
    while dim % t:
        t -= LANES
    return t


def _dot(a, b):
    return lax.dot_general(a, b, (((1,), (0,)), ((), ())), preferred_element_type=F32)


def _dot_nt(a, b):
    return lax.dot_general(a, b, (((1,), (1,)), ((), ())), preferred_element_type=F32)


def _dot_tn(a, b):
    return lax.dot_general(a, b, (((0,), (0,)), ((), ())), preferred_element_type=F32)


def _params(sem, vmem=VMEM_LIMIT):
    return pltpu.CompilerParams(dimension_semantics=sem, vmem_limit_bytes=vmem)


def _mod_index(cfg):
    tpe = cfg.TPE
    return lambda i: (i // tpe, jnp.minimum(i % tpe, 1), 0, 0)


def _mm(name, a, b, *, mode, outs, tm=512, tn=1024, tk=1024, layer=None, epi=None, extras=()):
    bshape = b.shape[1:] if layer is not None else b.shape
    if mode == "nn":
        (M, K), N = a.shape, bshape[1]
    elif mode == "nt":
        (M, K), N = a.shape, bshape[0]
    else:
        (K, M), N = a.shape, bshape[1]
    tm, tn, tk = _tile(M, tm), _tile(N, tn), _tile(K, tk)
    nk = K // tk
    a_spec = (pl.BlockSpec((tk, tm), lambda i, j, k: (k, i)) if mode == "tn"
              else pl.BlockSpec((tm, tk), lambda i, j, k: (i, k)))
    if mode == "nt":
        bblk, bidx = (tn, tk), (lambda i, j, k: (j, k))
    else:
        bblk, bidx = (tk, tn), (lambda i, j, k: (k, j))
    if layer is not None:
        b_spec = pl.BlockSpec((None,) + bblk, lambda i, j, k: (layer,) + bidx(i, j, k))
    else:
        b_spec = pl.BlockSpec(bblk, bidx)
    ne, no = len(extras), len(outs)
    dot = {"nn": _dot, "nt": _dot_nt, "tn": _dot_tn}[mode]

    def body(*refs):
        a_ref, b_ref = refs[0], refs[1]
        ex, out_refs = refs[2:2 + ne], refs[2 + ne:2 + ne + no]

        def finish(acc):
            res = epi(acc, *ex) if epi is not None else (acc,)
            for r, o in zip(res, out_refs):
                o[...] = r.astype(o.dtype)

        part = dot(a_ref[...].astype(BF16), b_ref[...].astype(BF16))
        if nk == 1:
            finish(part)
        else:
            acc_ref = refs[-1]
            k = pl.program_id(2)

            @pl.when(k == 0)
            def _():
                acc_ref[...] = part

            @pl.when(k > 0)
            def _():
                acc_ref[...] += part

            @pl.when(k == nk - 1)
            def _():
                finish(acc_ref[...])

    res = pl.pallas_call(
        body, name=name, grid=(M // tm, N // tn, nk),
        in_specs=[a_spec, b_spec] + [s for _, s in extras],
        out_specs=[pl.BlockSpec((tm, tn), lambda i, j, k: (i, j)) for _ in outs],
        out_shape=[jax.ShapeDtypeStruct((M, N), d) for d in outs],
        scratch_shapes=[pltpu.VMEM((tm, tn), F32)] if nk > 1 else [],
        compiler_params=_params(("parallel", "parallel", "arbitrary")),
    )(a, b, *[x for x, _ in extras])
    return res[0] if no == 1 else res


def _norm_mod_fwd(cfg, name, x, gain, mod, ish, isc):
    T, D, TM = cfg.T, cfg.D, cfg.TM

    def body(x_ref, g_ref, mod_ref, h_ref):
        x = x_ref[...]
        rstd = lax.rsqrt(jnp.mean(x * x, axis=-1, keepdims=True) + EPS)
        n = x * rstd * g_ref[...]
        h = n * (1.0 + mod_ref[pl.ds(isc, 1), :]) + mod_ref[pl.ds(ish, 1), :]
        h_ref[...] = h.astype(BF16)

    return pl.pallas_call(
        body, name=name, grid=(T // TM,),
        in_specs=[pl.BlockSpec((TM, D), lambda i: (i, 0)), pl.BlockSpec((1, D), lambda i: (0, 0)),
                  pl.BlockSpec((None, None, 6, D), _mod_index(cfg))],
        out_specs=pl.BlockSpec((TM, D), lambda i: (i, 0)),
        out_shape=jax.ShapeDtypeStruct((T, D), BF16),
        compiler_params=_params(("parallel",)),
    )(x, gain, mod)


def _norm_mod_bwd(cfg, name, x, gain, mod, ish, isc, dh, dres):
    T, D, TM, TPE = cfg.T, cfg.D, cfg.TM, cfg.TPE

    def body(x_ref, g_ref, mod_ref, dh_ref, dres_ref, dx_ref, dmod_ref, dgain_ref):
        i = pl.program_id(0)
        t = i % TPE
        x = x_ref[...]
        g = g_ref[...]
        dh = dh_ref[...].astype(F32)
        rstd = lax.rsqrt(jnp.mean(x * x, axis=-1, keepdims=True) + EPS)
        xhat = x * rstd
        dn = dh * (1.0 + mod_ref[pl.ds(isc, 1), :])
        dsh = jnp.sum(dh, axis=0, keepdims=True)
        dsc = jnp.sum(dh * (xhat * g), axis=0, keepdims=True)
        dgain = jnp.sum(dn * xhat, axis=0, keepdims=True)
        dxh = dn * g
        dx = rstd * (dxh - xhat * jnp.mean(dxh * xhat, axis=-1, keepdims=True))
        dx_ref[...] = dx + dres_ref[...]

        @pl.when(t <= 1)
        def _():
            dmod_ref[pl.ds(0, 1), :] = dsh
            dmod_ref[pl.ds(1, 1), :] = dsc

        @pl.when(t > 1)
        def _():
            dmod_ref[pl.ds(0, 1), :] += dsh
            dmod_ref[pl.ds(1, 1), :] += dsc

        @pl.when(i == 0)
        def _():
            dgain_ref[...] = dgain

        @pl.when(i > 0)
        def _():
            dgain_ref[...] += dgain

    tok = pl.BlockSpec((TM, D), lambda i: (i, 0))
    return pl.pallas_call(
        body, name=name, grid=(T // TM,),
        in_specs=[tok, pl.BlockSpec((1, D), lambda i: (0, 0)), pl.BlockSpec((None, None, 6, D), _mod_index(cfg)),
                  tok, tok],
        out_specs=[tok, pl.BlockSpec((None, None, 2, D), _mod_index(cfg)), pl.BlockSpec((1, D), lambda i: (0, 0))],
        out_shape=[jax.ShapeDtypeStruct((T, D), F32), jax.ShapeDtypeStruct((cfg.B, 2, 2, D), F32),
                   jax.ShapeDtypeStruct((1, D), F32)],
        compiler_params=_params(("arbitrary",)),
    )(x, gain, mod, dh, dres)


def _gate_bwd(cfg, name, dx, y, mod, igate):
    T, D, TM, TPE = cfg.T, cfg.D, cfg.TM, cfg.TPE

    def body(dx_ref, y_ref, mod_ref, dy_ref, dg_ref):
        t = pl.program_id(0) % TPE
        dx = dx_ref[...]
        dy_ref[...] = (dx * mod_ref[pl.ds(igate, 1), :]).astype(BF16)
        dg = jnp.sum(dx * y_ref[...].astype(F32), axis=0, keepdims=True)

        @pl.when(t <= 1)
        def _():
            dg_ref[...] = dg

        @pl.when(t > 1)
        def _():
            dg_ref[...] += dg

    tok = pl.BlockSpec((TM, D), lambda i: (i, 0))
    return pl.pallas_call(
        body, name=name, grid=(T // TM,),
        in_specs=[tok, tok, pl.BlockSpec((None, None, 6, D), _mod_index(cfg))],
        out_specs=[tok, pl.BlockSpec((None, None, 1, D), _mod_index(cfg))],
        out_shape=[jax.ShapeDtypeStruct((T, D), BF16), jax.ShapeDtypeStruct((cfg.B, 2, 1, D), F32)],
        compiler_params=_params(("arbitrary",)),
    )(dx, y, mod)


def _loss_grad(cfg, x, tgt):
    T, D, TM, TPE = cfg.T, cfg.D, cfg.TM, cfg.TPE

    def body(x_ref, t_ref, dx_ref, loss_ref):
        i = pl.program_id(0)
        t = i % TPE

        @pl.when(i == 0)
        def _():
            loss_ref[...] = jnp.zeros_like(loss_ref)

        @pl.when(t == 0)
        def _():
            dx_ref[...] = jnp.zeros_like(dx_ref)

        @pl.when(t > 0)
        def _():
            err = x_ref[...] - t_ref[...]
            dx_ref[...] = err * (1.0 / D)
            loss_ref[...] += 0.5 * jnp.sum(jnp.mean(err * err, axis=-1, keepdims=True), axis=0, keepdims=True)

    tok = pl.BlockSpec((TM, D), lambda i: (i, 0))
    tgt_spec = pl.BlockSpec((TM, D), lambda i: ((i // TPE) * (TPE - 1) + jnp.maximum(i % TPE - 1, 0), 0))
    dx, loss = pl.pallas_call(
        body, name="loss_grad", grid=(T // TM,),
        in_specs=[tok, tgt_spec], out_specs=[tok, pl.BlockSpec((8, LANES), lambda i: (0, 0))],
        out_shape=[jax.ShapeDtypeStruct((T, D), F32), jax.ShapeDtypeStruct((8, LANES), F32)],
        compiler_params=_params(("arbitrary",)),
    )(x, tgt)
    return loss[0, 0], dx


def _rope_tables(cfg):
    rows = cfg.SL // cfg.GRID_W
    row = jnp.repeat(jnp.arange(rows, dtype=F32), cfg.GRID_W)
    col = jnp.tile(jnp.arange(cfg.GRID_W, dtype=F32), rows)
    n_freq = CHUNK // 4
    inv = ROPE_BASE ** (-jnp.arange(n_freq, dtype=F32) / n_freq)
    ang = jnp.concatenate([row[:, None] * inv[None, :], col[:, None] * inv[None, :]], axis=-1)
    cos, sin = jnp.cos(ang), jnp.sin(ang)
    cosf = jnp.concatenate([jnp.ones((cfg.SC, CHUNK), F32), jnp.concatenate([cos, cos], axis=-1)], axis=0)
    sinf = jnp.concatenate([jnp.zeros((cfg.SC, CHUNK), F32), jnp.concatenate([-sin, sin], axis=-1)], axis=0)
    return cosf, sinf


def _rope(x, cosf, sinf):
    return x * cosf + pltpu.roll(x, CHUNK // 2, 1) * sinf


def _irope(dy, cosf, sinf):
    return dy * cosf - pltpu.roll(dy, CHUNK // 2, 1) * sinf


def _prep_fwd(cfg, name, p, cosf, sinf, qg, kg):
    T, TM, TPE, H, KV = cfg.T, cfg.TM, cfg.TPE, cfg.H, cfg.KV
    HW = H * CHUNK
    kscale = CHUNK ** -0.5

    def body(p_ref, c_ref, s_ref, qg_ref, kg_ref, rq_ref, rk_ref, aq_ref, ak_ref):
        cosf, sinf = c_ref[...], s_ref[...]

        def normed(x, g):
            return x * lax.rsqrt(jnp.mean(x * x, axis=-1, keepdims=True) + EPS) * g

        for h in range(H):
            sl = pl.ds(h * CHUNK, CHUNK)
            rq_ref[:, sl] = _rope(p_ref[:, pl.ds(h * CHUNK, CHUNK)], cosf, sinf)
            rk_ref[:, sl] = _rope(p_ref[:, pl.ds(HW + h * CHUNK, CHUNK)], cosf, sinf) * kscale
            aq_ref[:, sl] = _rope(normed(p_ref[:, pl.ds(4 * HW + h * CHUNK, CHUNK)], qg_ref[...]),
                                  cosf, sinf).astype(BF16)
        for h in range(KV):
            ak_ref[:, pl.ds(h * CHUNK, CHUNK)] = _rope(
                normed(p_ref[:, pl.ds(5 * HW + h * CHUNK, CHUNK)], kg_ref[...]), cosf, sinf).astype(BF16)

    tab = pl.BlockSpec((TM, CHUNK), lambda i: (i % TPE, 0))
    vec = pl.BlockSpec((1, CHUNK), lambda i: (0, 0))
    return pl.pallas_call(
        body, name=name, grid=(T // TM,),
        in_specs=[pl.BlockSpec((TM, cfg.ABW), lambda i: (i, 0)), tab, tab, vec, vec],
        out_specs=[pl.BlockSpec((TM, HW), lambda i: (i, 0))] * 3 + [pl.BlockSpec((TM, KV * CHUNK), lambda i: (i, 0))],
        out_shape=[jax.ShapeDtypeStruct((T, HW), F32), jax.ShapeDtypeStruct((T, HW), F32),
                   jax.ShapeDtypeStruct((T, HW), BF16), jax.ShapeDtypeStruct((T, KV * CHUNK), BF16)],
        compiler_params=_params(("parallel",)),
    )(p, cosf, sinf, qg, kg)


def _prep_bwd(cfg, name, p, cosf, sinf, qg, kg, d_rq, d_rk, d_rv, d_gate, d_aq, d_ak, d_av):
    T, TM, TPE, H, KV = cfg.T, cfg.TM, cfg.TPE, cfg.H, cfg.KV
    HW = H * CHUNK
    kscale = CHUNK ** -0.5

    def body(p_ref, c_ref, s_ref, qg_ref, kg_ref, drq_ref, drk_ref, drv_ref, dgt_ref, daq_ref, dak_ref, dav_ref,
             dp_ref, dqg_ref, dkg_ref):
        i = pl.program_id(0)
        cosf, sinf = c_ref[...], s_ref[...]

        def norm_bwd(x, g, dn):
            rstd = lax.rsqrt(jnp.mean(x * x, axis=-1, keepdims=True) + EPS)
            xhat = x * rstd
            dg = jnp.sum(dn * xhat, axis=0, keepdims=True)
            dxh = dn * g
            return rstd * (dxh - xhat * jnp.mean(dxh * xhat, axis=-1, keepdims=True)), dg

        dqg = jnp.zeros((1, CHUNK), F32)
        dkg = jnp.zeros((1, CHUNK), F32)
        for h in range(H):
            sl = pl.ds(h * CHUNK, CHUNK)
            dp_ref[:, pl.ds(h * CHUNK, CHUNK)] = _irope(drq_ref[:, sl], cosf, sinf).astype(BF16)
            dp_ref[:, pl.ds(HW + h * CHUNK, CHUNK)] = (_irope(drk_ref[:, sl], cosf, sinf) * kscale).astype(BF16)
            dp_ref[:, pl.ds(2 * HW + h * CHUNK, CHUNK)] = drv_ref[:, sl].astype(BF16)
            dp_ref[:, pl.ds(3 * HW + h * CHUNK, CHUNK)] = dgt_ref[:, sl].astype(BF16)
            dx, dg = norm_bwd(p_ref[:, pl.ds(4 * HW + h * CHUNK, CHUNK)], qg_ref[...],
                              _irope(daq_ref[:, sl], cosf, sinf))
            dp_ref[:, pl.ds(4 * HW + h * CHUNK, CHUNK)] = dx.astype(BF16)
            dqg = dqg + dg
        for h in range(KV):
            sl = pl.ds(h * CHUNK, CHUNK)
            dx, dg = norm_bwd(p_ref[:, pl.ds(5 * HW + h * CHUNK, CHUNK)], kg_ref[...],
                              _irope(dak_ref[:, sl], cosf, sinf))
            dp_ref[:, pl.ds(5 * HW + h * CHUNK, CHUNK)] = dx.astype(BF16)
            dp_ref[:, pl.ds(5 * HW + (KV + h) * CHUNK, CHUNK)] = dav_ref[:, sl].astype(BF16)
            dkg = dkg + dg

        @pl.when(i == 0)
        def _():
            dqg_ref[...] = dqg
            dkg_ref[...] = dkg

        @pl.when(i > 0)
        def _():
            dqg_ref[...] += dqg
            dkg_ref[...] += dkg

    tab = pl.BlockSpec((TM, CHUNK), lambda i: (i % TPE, 0))
    vec = pl.BlockSpec((1, CHUNK), lambda i: (0, 0))
    hw = pl.BlockSpec((TM, HW), lambda i: (i, 0))
    kvw = pl.BlockSpec((TM, KV * CHUNK), lambda i: (i, 0))
    return pl.pallas_call(
        body, name=name, grid=(T // TM,),
        in_specs=[pl.BlockSpec((TM, cfg.ABW), lambda i: (i, 0)), tab, tab, vec, vec, hw, hw, hw, hw, hw, kvw, kvw],
        out_specs=[pl.BlockSpec((TM, cfg.ABW), lambda i: (i, 0)), vec, vec],
        out_shape=[jax.ShapeDtypeStruct((T, cfg.ABW), BF16), jax.ShapeDtypeStruct((1, CHUNK), F32),
                   jax.ShapeDtypeStruct((1, CHUNK), F32)],
        compiler_params=_params(("arbitrary",)),
    )(p, cosf, sinf, qg, kg, d_rq, d_rk, d_rv, d_gate, d_aq, d_ak, d_av)


def _ret_consts(direction, lg):
    C = CHUNK
    ii = lax.broadcasted_iota(jnp.int32, (C, C), 0)
    jj = lax.broadcasted_iota(jnp.int32, (C, C), 1)
    col = lax.broadcasted_iota(jnp.int32, (C, 1), 0).astype(F32)
    if direction == 0:
        mask, er, ek, eq = ii >= jj, (ii - jj).astype(F32), (C - 1.0) - col, col + 1.0
    else:
        mask, er, ek, eq = jj >= ii, (jj - ii).astype(F32), col, C - col
    er = jnp.where(mask, er, 0.0)
    dm = jnp.where(mask, jnp.exp(er * lg), 0.0)
    return dm, er, jnp.exp(ek * lg), ek, jnp.exp(eq * lg), eq, jnp.exp(C * lg)


def _ret_chunk(cfg, direction, t):
    n_all, n_ctx = cfg.S // CHUNK, cfg.SC // CHUNK
    if direction == 0:
        return t
    return jnp.where(t < n_ctx, n_ctx - 1 - t, n_all - 1 - (t - n_ctx))


def _head_norm_gate(o, g):
    mu = jnp.mean(o, axis=-1, keepdims=True)
    var = jnp.mean(jnp.square(o - mu), axis=-1, keepdims=True)
    rstd = lax.rsqrt(var + EPS)
    y = (o - mu) * rstd
    sg = jax.nn.sigmoid(g)
    return y, rstd, sg


def _retention_fwd(cfg, name, rq, rk, p, lgb):
    B, H, S, T = cfg.B, cfg.H, cfg.S, cfg.T
    n_all = S // CHUNK

    def body(q_ref, k_ref, v_ref, g_ref, lg_ref, o_ref, ret_ref, st_ref):
        for direction in (0, 1):
            dm, _, kd, _, qd, _, cd = _ret_consts(direction, lg_ref[direction][0:1, 0:1])
            st_ref[...] = jnp.zeros_like(st_ref)

            def step(t, carry):
                n = _ret_chunk(cfg, direction, t)
                sl = pl.ds(pl.multiple_of(n * CHUNK, CHUNK), CHUNK)
                q = q_ref[sl, :].astype(BF16)
                k = k_ref[sl, :]
                v = v_ref[sl, :].astype(BF16)
                st = st_ref[...]
                s = _dot_nt(q, k.astype(BF16)) * dm
                o = _dot(s.astype(BF16), v) + _dot(q, st.astype(BF16)) * qd
                if direction == 0:
                    o_ref[sl, :] = o
                else:
                    o_ref[sl, :] += o
                st_ref[...] = cd * st + _dot_tn((k * kd).astype(BF16), v)
                return carry

            lax.fori_loop(0, n_all, step, 0)

        def gate_step(n, carry):
            sl = pl.ds(pl.multiple_of(n * CHUNK, CHUNK), CHUNK)
            g = g_ref[sl, :]
            y, _, sg = _head_norm_gate(o_ref[sl, :], g)
            ret_ref[sl, :] = (y * (g * sg)).astype(BF16)
            return carry

        lax.fori_loop(0, n_all, gate_step, 0)

    HW = H * CHUNK
    blk = lambda off: pl.BlockSpec((S, CHUNK), lambda b, h: (b, off + h))
    return pl.pallas_call(
        body, name=name, grid=(B, H),
        in_specs=[blk(0), blk(0), blk(2 * H), blk(3 * H),
                  pl.BlockSpec((None, 2, 8, LANES), lambda b, h: (h, 0, 0, 0))],
        out_specs=[blk(0), blk(0)],
        out_shape=[jax.ShapeDtypeStruct((T, HW), F32), jax.ShapeDtypeStruct((T, HW), BF16)],
        scratch_shapes=[pltpu.VMEM((CHUNK, CHUNK), F32)],
        compiler_params=_params(("parallel", "parallel")),
    )(rq, rk, p, p, lgb)


def _retention_bwd(cfg, name, rq, rk, p, o_sum, dcat, lgb):
    B, H, S, T = cfg.B, cfg.H, cfg.S, cfg.T
    n_all = S // CHUNK
    C = CHUNK

    def body(q_ref, k_ref, v_ref, g_ref, o_ref, dr_ref, lg_ref, dq_ref, dk_ref, dv_ref, dg_ref, dlg_ref,
             do_ref, st_ref, ds_ref, acc_ref):
        def gate_step(n, carry):
            sl = pl.ds(pl.multiple_of(n * C, C), C)
            g = g_ref[sl, :]
            dr = dr_ref[sl, :]
            y, rstd, sg = _head_norm_gate(o_ref[sl, :], g)
            dy = dr * (g * sg)
            dg_ref[sl, :] = dr * y * (sg * (1.0 + g * (1.0 - sg)))
            do_ref[sl, :] = rstd * (dy - jnp.mean(dy, axis=-1, keepdims=True)
                                    - y * jnp.mean(dy * y, axis=-1, keepdims=True))
            return carry

        lax.fori_loop(0, n_all, gate_step, 0)

        for direction in (0, 1):
            dm, er, kd, ek, qd, eq, cd = _ret_consts(direction, lg_ref[direction][0:1, 0:1])

            def fwd_step(t, st):
                n = _ret_chunk(cfg, direction, t)
                sl = pl.ds(pl.multiple_of(n * C, C), C)
                st_ref[t] = st
                return cd * st + _dot_tn((k_ref[sl, :] * kd).astype(BF16), v_ref[sl, :].astype(BF16))

            lax.fori_loop(0, n_all, fwd_step, jnp.zeros((C, C), F32))
            ds_ref[...] = jnp.zeros_like(ds_ref)
            acc_ref[...] = jnp.zeros_like(acc_ref)

            def bwd_step(u, carry):
                t = n_all - 1 - u
                n = _ret_chunk(cfg, direction, t)
                sl = pl.ds(pl.multiple_of(n * C, C), C)
                q = q_ref[sl, :].astype(BF16)
                kf = k_ref[sl, :]
                k = kf.astype(BF16)
                v = v_ref[sl, :].astype(BF16)
                do = do_ref[sl, :]
                dob = do.astype(BF16)
                sp = st_ref[t]
                spb = sp.astype(BF16)
                ds = ds_ref[...]
                dsb = ds.astype(BF16)
                dk_state = _dot_nt(v, dsb) * kd
                dv = _dot((kf * kd).astype(BF16), dsb)
                pm = _dot_nt(q, k) * dm
                dpm = _dot_nt(dob, v)
                dsr = (dpm * dm).astype(BF16)
                dq = _dot(dsr, k)
                dk = _dot_tn(dsr, q) + dk_state
                dv = dv + _dot_tn(pm.astype(BF16), dob)
                doq = do * qd
                doqb = doq.astype(BF16)
                qs = _dot(q, spb)
                dq = dq + _dot_nt(doqb, spb)
                acc_ref[...] += (jnp.sum(dpm * pm * er, axis=0, keepdims=True)
                                 + jnp.sum(eq * doq * qs, axis=0, keepdims=True)
                                 + jnp.sum(ek * kf * dk_state, axis=0, keepdims=True)
                                 + (C * cd) * jnp.sum(ds * sp, axis=0, keepdims=True))
                ds_ref[...] = cd * ds + _dot_tn(q, doqb)
                if direction == 0:
                    dq_ref[sl, :] = dq
                    dk_ref[sl, :] = dk
                    dv_ref[sl, :] = dv
                else:
                    dq_ref[sl, :] += dq
                    dk_ref[sl, :] += dk
                    dv_ref[sl, :] += dv
                return carry

            lax.fori_loop(0, n_all, bwd_step, 0)
            dlg_ref[direction] = jnp.broadcast_to(jnp.sum(acc_ref[...], axis=1, keepdims=True), (8, LANES))

    HW = H * CHUNK
    blk = lambda off: pl.BlockSpec((S, CHUNK), lambda b, h: (b, off + h))
    return pl.pallas_call(
        body, name=name, grid=(B, H),
        in_specs=[blk(0), blk(0), blk(2 * H), blk(3 * H), blk(0), blk(0),
                  pl.BlockSpec((None, 2, 8, LANES), lambda b, h: (h, 0, 0, 0))],
        out_specs=[blk(0)] * 4 + [pl.BlockSpec((None, None, 2, 8, LANES), lambda b, h: (b, h, 0, 0, 0))],
        out_shape=[jax.ShapeDtypeStruct((T, HW), F32)] * 4 + [jax.ShapeDtypeStruct((B, H, 2, 8, LANES), F32)],
        scratch_shapes=[pltpu.VMEM((S, CHUNK), F32), pltpu.VMEM((n_all, C, C), F32), pltpu.VMEM((C, C), F32),
                        pltpu.VMEM((1, C), F32)],
        compiler_params=_params(("parallel", "parallel")),
    )(rq, rk, p, p, o_sum, dcat, lgb)


def _attn_scores(cfg, q, k, t):
    s = _dot_nt(q, k) * (CHUNK ** -0.5)
    kcol = lax.broadcasted_iota(jnp.int32, s.shape, 1)
    s = jnp.where(jnp.logical_or(t > 0, kcol < cfg.SC), s, -1e30)
    e = jnp.exp(s - jnp.max(s, axis=-1, keepdims=True))
    return e / jnp.sum(e, axis=-1, keepdims=True)


def _attention_fwd(cfg, name, aq, ak, p):
    B, H, KV, S, T, TM, TPE = cfg.B, cfg.H, cfg.KV, cfg.S, cfg.T, cfg.TM, cfg.TPE
    G = H // KV
    v_off = (5 * H + KV)

    def body(q_ref, k_ref, v_ref, o_ref):
        pr = _attn_scores(cfg, q_ref[...], k_ref[...], pl.program_id(2))
        o_ref[...] = _dot(pr.astype(BF16), v_ref[...].astype(BF16)).astype(BF16)

    return pl.pallas_call(
        body, name=name, grid=(B, H, TPE),
        in_specs=[pl.BlockSpec((TM, CHUNK), lambda b, h, t: (b * TPE + t, h)),
                  pl.BlockSpec((S, CHUNK), lambda b, h, t: (b, h // G)),
                  pl.BlockSpec((S, CHUNK), lambda b, h, t: (b, v_off + h // G))],
        out_specs=pl.BlockSpec((TM, CHUNK), lambda b, h, t: (b * TPE + t, h)),
        out_shape=jax.ShapeDtypeStruct((T, H * CHUNK), BF16),
        compiler_params=_params(("parallel", "parallel", "parallel")),
    )(aq, ak, p)


def _attention_bwd(cfg, name, aq, ak, p, dcat):
    B, H, KV, S, T, TM, TPE = cfg.B, cfg.H, cfg.KV, cfg.S, cfg.T, cfg.TM, cfg.TPE
    G = H // KV
    v_off = (5 * H + KV)

    def body(q_ref, k_ref, v_ref, do_ref, dq_ref, dk_ref, dv_ref):
        g, t = pl.program_id(2), pl.program_id(3)
        q, k = q_ref[...], k_ref[...]
        v = v_ref[...].astype(BF16)
        dob = do_ref[...].astype(BF16)
        pr = _attn_scores(cfg, q, k, t)
        dpr = _dot_nt(dob, v)
        ds = (pr * (dpr - jnp.sum(pr * dpr, axis=-1, keepdims=True)) * (CHUNK ** -0.5)).astype(BF16)
        dq_ref[...] = _dot(ds, k)
        dk = _dot_tn(ds, q)
        dv = _dot_tn(pr.astype(BF16), dob)
        first = jnp.logical_and(g == 0, t == 0)

        @pl.when(first)
        def _():
            dk_ref[...] = dk
            dv_ref[...] = dv

        @pl.when(jnp.logical_not(first))
        def _():
            dk_ref[...] += dk
            dv_ref[...] += dv

    qspec = pl.BlockSpec((TM, CHUNK), lambda b, kv, g, t: (b * TPE + t, kv * G + g))
    kvspec = pl.BlockSpec((S, CHUNK), lambda b, kv, g, t: (b, kv))
    return pl.pallas_call(
        body, name=name, grid=(B, KV, G, TPE),
        in_specs=[qspec, kvspec, pl.BlockSpec((S, CHUNK), lambda b, kv, g, t: (b, v_off + kv)),
                  pl.BlockSpec((TM, CHUNK), lambda b, kv, g, t: (b * TPE + t, H + kv * G + g))],
        out_specs=[qspec, kvspec, kvspec],
        out_shape=[jax.ShapeDtypeStruct((T, H * CHUNK), F32), jax.ShapeDtypeStruct((T, KV * CHUNK), F32),
                   jax.ShapeDtypeStruct((T, KV * CHUNK), F32)],
        compiler_params=_params(("parallel", "parallel", "arbitrary", "arbitrary")),
    )(aq, ak, p, dcat)


_GELU_C = math.sqrt(2.0 / math.pi)


def _gelu(x):
    return 0.5 * x * (1.0 + jnp.tanh(_GELU_C * (x + 0.044715 * x * x * x)))


def _gelu_grad(x):
    th = jnp.tanh(_GELU_C * (x + 0.044715 * x * x * x))
    return 0.5 * (1.0 + th) + 0.5 * x * (1.0 - th * th) * _GELU_C * (1.0 + 3.0 * 0.044715 * x * x)


def _cm_fwd(cfg, name, a, vg, ws, bs):
    T, TM, W, NG = cfg.T, cfg.TM, cfg.CMW, cfg.CMG

    def body(a_ref, vg_ref, ws_ref, bs_ref, m_ref):
        v = _gelu(a_ref[:, pl.ds(W, W)])
        vn = (v * lax.rsqrt(jnp.mean(v * v, axis=-1, keepdims=True) + EPS) * vg_ref[...]).astype(BF16)
        for c in range(TM // CHUNK):
            for g in range(NG):
                rows, cols = slice(c * CHUNK, (c + 1) * CHUNK), slice(g * CHUNK, (g + 1) * CHUNK)
                sv = _dot(ws_ref[g].astype(BF16), vn[rows, cols]) + bs_ref[g]
                u = _gelu(a_ref[pl.ds(c * CHUNK, CHUNK), pl.ds(g * CHUNK, CHUNK)])
                m_ref[pl.ds(c * CHUNK, CHUNK), pl.ds(g * CHUNK, CHUNK)] = (u * sv).astype(BF16)

    return pl.pallas_call(
        body, name=name, grid=(T // TM,),
        in_specs=[pl.BlockSpec((TM, 2 * W), lambda i: (i, 0)), pl.BlockSpec((1, W), lambda i: (0, 0)),
                  pl.BlockSpec((NG, CHUNK, CHUNK), lambda i: (0, 0, 0)),
                  pl.BlockSpec((NG, CHUNK, 1), lambda i: (0, 0, 0))],
        out_specs=pl.BlockSpec((TM, W), lambda i: (i, 0)),
        out_shape=jax.ShapeDtypeStruct((T, W), BF16),
        compiler_params=_params(("parallel",)),
    )(a, vg, ws, bs)


def _cm_bwd(cfg, name, a, vg, ws, bs, dm):
    T, TM, W, NG = cfg.T, cfg.TM, cfg.CMW, cfg.CMG

    def body(a_ref, vg_ref, ws_ref, bs_ref, dm_ref, da_ref, dws_ref, dbs_ref, dvg_ref, dvn_ref):
        i = pl.program_id(0)

        @pl.when(i == 0)
        def _():
            dws_ref[...] = jnp.zeros_like(dws_ref)
            dbs_ref[...] = jnp.zeros_like(dbs_ref)
            dvg_ref[...] = jnp.zeros_like(dvg_ref)

        av = a_ref[:, pl.ds(W, W)]
        v = _gelu(av)
        rstd = lax.rsqrt(jnp.mean(v * v, axis=-1, keepdims=True) + EPS)
        xhat = v * rstd
        vg = vg_ref[...]
        vn = (xhat * vg).astype(BF16)
        for c in range(TM // CHUNK):
            for g in range(NG):
                rows, cols = slice(c * CHUNK, (c + 1) * CHUNK), slice(g * CHUNK, (g + 1) * CHUNK)
                rs, cs = pl.ds(c * CHUNK, CHUNK), pl.ds(g * CHUNK, CHUNK)
                wsb = ws_ref[g].astype(BF16)
                blk = vn[rows, cols]
                sv = _dot(wsb, blk) + bs_ref[g]
                au = a_ref[rs, cs]
                dmb = dm_ref[rs, cs]
                da_ref[rs, cs] = (dmb * sv * _gelu_grad(au)).astype(BF16)
                dsv = dmb * _gelu(au)
                dsvb = dsv.astype(BF16)
                dbs_ref[g] += jnp.sum(dsv, axis=1, keepdims=True)
                dws_ref[g] += _dot_nt(dsvb, blk)
                dvn_ref[rs, cs] = _dot_tn(wsb, dsvb)
        dvn = dvn_ref[...]
        dvg_ref[...] += jnp.sum(dvn * xhat, axis=0, keepdims=True)
        dxh = dvn * vg
        dv = rstd * (dxh - xhat * jnp.mean(dxh * xhat, axis=-1, keepdims=True))
        da_ref[:, pl.ds(W, W)] = (dv * _gelu_grad(av)).astype(BF16)

    return pl.pallas_call(
        body, name=name, grid=(T // TM,),
        in_specs=[pl.BlockSpec((TM, 2 * W), lambda i: (i, 0)), pl.BlockSpec((1, W), lambda i: (0, 0)),
                  pl.BlockSpec((NG, CHUNK, CHUNK), lambda i: (0, 0, 0)),
                  pl.BlockSpec((NG, CHUNK, 1), lambda i: (0, 0, 0)), pl.BlockSpec((TM, W), lambda i: (i, 0))],
        out_specs=[pl.BlockSpec((TM, 2 * W), lambda i: (i, 0)), pl.BlockSpec((NG, CHUNK, CHUNK), lambda i: (0, 0, 0)),
                   pl.BlockSpec((NG, CHUNK, 1), lambda i: (0, 0, 0)), pl.BlockSpec((1, W), lambda i: (0, 0))],
        out_shape=[jax.ShapeDtypeStruct((T, 2 * W), BF16), jax.ShapeDtypeStruct((NG, CHUNK, CHUNK), F32),
                   jax.ShapeDtypeStruct((NG, CHUNK, 1), F32), jax.ShapeDtypeStruct((1, W), F32)],
        scratch_shapes=[pltpu.VMEM((TM, W), F32)],
        compiler_params=_params(("arbitrary",)),
    )(a, vg, ws, bs, dm)


def _local_step(cfg, xcat, tgt, mods, w):
    D, TM, H = cfg.D, cfg.TM, cfg.H
    gate_spec = pl.BlockSpec((None, None, 6, D), lambda i, j, k: _mod_index(cfg)(i))
    cosf, sinf = _rope_tables(cfg)

    def resid_epi(igate):
        def epi(acc, x_ref, mod_ref):
            return x_ref[...] + mod_ref[pl.ds(igate, 1), :] * acc, acc
        return epi

    def gated_out(name, a, wname, l_idx, x, mod, igate):
        return _mm(name, a, w[wname], mode="nn", layer=l_idx, tm=TM, tn=D, outs=[F32, BF16], epi=resid_epi(igate),
                   extras=[(x, pl.BlockSpec((TM, D), lambda i, j, k: (i, j))), (mod, gate_spec)])

    saved = []
    x = xcat
    for l in range(N_LAYERS):
        li = l // 2
        mod = mods[l]
        s = {"x0": x}
        s["h"] = _norm_mod_fwd(cfg, f"norm1_fwd_{l}", x, w["norm1_g"][l][None], mod, 0, 1)
        if l % 2 == 0:
            lgb = jnp.broadcast_to(jax.nn.log_sigmoid(w["ret_decay"][li]).T[:, :, None, None], (H, 2, 8, LANES))
            qg, kg = w["att_q_norm_g"][li][None], w["att_k_norm_g"][li][None]
            s["p"] = _mm(f"ab_in_{l}", s["h"], w["ab_w_in"], mode="nn", layer=li, outs=[F32], tn=768)
            s["rq"], s["rk"], s["aq"], s["ak"] = _prep_fwd(cfg, f"prep_fwd_{l}", s["p"], cosf, sinf, qg, kg)
            s["o"], ret = _retention_fwd(cfg, f"ret_fwd_{l}", s["rq"], s["rk"], s["p"], lgb)
            att = _attention_fwd(cfg, f"att_fwd_{l}", s["aq"], s["ak"], s["p"])
            s["cat"] = jnp.concatenate([ret, att], axis=-1)
            s["lgb"], s["qg"], s["kg"] = lgb, qg, kg
            x, s["y1"] = gated_out(f"ab_out_{l}", s["cat"], "ab_w_out", li, x, mod, 2)
        else:
            s["a"] = _mm(f"cm_in_{l}", s["h"], w["cm_w_in"], mode="nn", layer=li, outs=[F32])
            s["vg"], s["ws"], s["bs"] = w["cm_v_norm_g"][li][None], w["cm_w_s"][li], w["cm_b_s"][li][:, :, None]
            s["m"] = _cm_fwd(cfg, f"cm_fwd_{l}", s["a"], s["vg"], s["ws"], s["bs"])
            x, s["y1"] = gated_out(f"cm_out_{l}", s["m"], "cm_w_out", li, x, mod, 2)
        s["x1"] = x
        s["h2"] = _norm_mod_fwd(cfg, f"norm2_fwd_{l}", x, w["norm2_g"][l][None], mod, 3, 4)
        s["a2"], s["r"] = _mm(f"ff1_{l}", s["h2"], w["ff_w1"], mode="nn", layer=l, outs=[BF16, BF16],
                              epi=lambda acc: (acc, jnp.square(jnp.maximum(acc, 0.0))))
        x, s["y2"] = gated_out(f"ff2_{l}", s["r"], "ff_w2", l, x, mod, 5)
        saved.append(s)

    loss, dx = _loss_grad(cfg, x, tgt)

    big = {k: [None] * (N_LAYERS if k.startswith("ff") else 2) for k in
           ("ab_w_in", "ab_w_out", "cm_w_in", "cm_w_out", "ff_w1", "ff_w2")}
    small = {k: [None] * n for k, n in (("norm1_g", 4), ("norm2_g", 4), ("ret_lg", 2), ("att_q_norm_g", 2),
                                        ("att_k_norm_g", 2), ("cm_v_norm_g", 2), ("cm_w_s", 2), ("cm_b_s", 2))}
    dmods = [None] * N_LAYERS

    def wgrad(name, a, d):
        return _mm(name, a, d, mode="tn", outs=[BF16], tm=512, tn=1024, tk=512)

    for l in reversed(range(N_LAYERS)):
        li = l // 2
        s, mod = saved[l], mods[l]
        dy2, dg2 = _gate_bwd(cfg, f"gate2_bwd_{l}", dx, s["y2"], mod, 5)
        da2 = _mm(f"ff2_dx_{l}", dy2, w["ff_w2"], mode="nt", layer=l, outs=[BF16],
                  epi=lambda acc, a_ref: (acc * (2.0 * jnp.maximum(a_ref[...].astype(F32), 0.0)),),
                  extras=[(s["a2"], pl.BlockSpec((_tile(cfg.T, 512), _tile(cfg.FF, 1024)), lambda i, j, k: (i, j)))])
        big["ff_w2"][l] = wgrad(f"ff2_dw_{l}", s["r"], dy2)
        big["ff_w1"][l] = wgrad(f"ff1_dw_{l}", s["h2"], da2)
        dh2 = _mm(f"ff1_dx_{l}", da2, w["ff_w1"], mode="nt", layer=l, outs=[F32])
        dx, dm2, small["norm2_g"][l] = _norm_mod_bwd(cfg, f"norm2_bwd_{l}", s["x1"], w["norm2_g"][l][None], mod, 3, 4,
                                                     dh2, dx)
        do, dg1 = _gate_bwd(cfg, f"gate1_bwd_{l}", dx, s["y1"], mod, 2)
        if l % 2 == 0:
            big["ab_w_out"][li] = wgrad(f"ab_out_dw_{l}", s["cat"], do)
            dcat = _mm(f"ab_out_dx_{l}", do, w["ab_w_out"], mode="nt", layer=li, outs=[F32])
            d_rq, d_rk, d_rv, d_gt, dlg = _retention_bwd(cfg, f"ret_bwd_{l}", s["rq"], s["rk"], s["p"], s["o"], dcat,
                                                         s["lgb"])
            d_aq, d_ak, d_av = _attention_bwd(cfg, f"att_bwd_{l}", s["aq"], s["ak"], s["p"], dcat)
            dp, dqg, dkg = _prep_bwd(cfg, f"prep_bwd_{l}", s["p"], cosf, sinf, s["qg"], s["kg"],
                                     d_rq, d_rk, d_rv, d_gt, d_aq, d_ak, d_av)
            small["ret_lg"][li] = jnp.sum(dlg[:, :, :, 0, 0], axis=0).T
            small["att_q_norm_g"][li], small["att_k_norm_g"][li] = dqg[0], dkg[0]
            big["ab_w_in"][li] = wgrad(f"ab_in_dw_{l}", s["h"], dp)
            dh = _mm(f"ab_in_dx_{l}", dp, w["ab_w_in"], mode="nt", layer=li, outs=[F32], tk=768)
        else:
            big["cm_w_out"][li] = wgrad(f"cm_out_dw_{l}", s["m"], do)
            dm = _mm(f"cm_out_dx_{l}", do, w["cm_w_out"], mode="nt", layer=li, outs=[F32])
            da, dws, dbs, dvg = _cm_bwd(cfg, f"cm_bwd_{l}", s["a"], s["vg"], s["ws"], s["bs"], dm)
            small["cm_w_s"][li], small["cm_b_s"][li], small["cm_v_norm_g"][li] = dws, dbs[:, :, 0], dvg[0]
            big["cm_w_in"][li] = wgrad(f"cm_in_dw_{l}", s["h"], da)
            dh = _mm(f"cm_in_dx_{l}", da, w["cm_w_in"], mode="nt", layer=li, outs=[F32])
        dx, dm1, small["norm1_g"][l] = _norm_mod_bwd(cfg, f"norm1_bwd_{l}", s["x0"], w["norm1_g"][l][None], mod, 0, 1,
                                                     dh, dx)
        dmods[l] = jnp.concatenate([dm1, dg1, dm2, dg2], axis=2)
    return loss, dx, big, small, dmods


N_DEV = 8
N_CHIP = 4
MESH = pl.DeviceIdType.MESH
ANY = pl.BlockSpec(memory_space=pl.ANY)
BIG = (("ab_w_in", 2), ("ab_w_out", 1), ("cm_w_in", 2), ("cm_w_out", 1), ("ff_w1", 2), ("ff_w2", 1))


def _place():
    x, y, c = lax.axis_index("x"), lax.axis_index("y"), lax.axis_index("c")
    return x, y, c, [(1 - x, y), (x, 1 - y), (1 - x, 1 - y)]


def _shard_of(ref, axis, s, width):
    start = pl.multiple_of(s * width, LANES)
    if axis == 1:
        return ref.at[:, pl.ds(start, width), :]
    return ref.at[:, :, pl.ds(start, width)]


def _allgather8(name, block):
    m_per, n = block.shape

    def body(x_ref, out_ref, send_sems, recv_sems, local_sem):
        x, y, c, chips = _place()
        me, sibling = (x, y, c), (x, y, 1 - c)

        def rows(px, py, pc):
            return out_ref.at[pl.ds((4 * px + 2 * py + pc) * m_per, m_per), :]

        def copy(k, blk, to, src=None):
            return pltpu.make_async_remote_copy(
                src_ref=rows(*blk) if src is None else src, dst_ref=rows(*blk),
                send_sem=send_sems.at[k], recv_sem=recv_sems.at[k], device_id=to, device_id_type=MESH)

        mine = pltpu.make_async_copy(x_ref, rows(*me), local_sem)
        mine.start()
        first = [copy(0, me, sibling, src=x_ref)]
        first += [copy(1 + j, me, (*chip, c), src=x_ref) for j, chip in enumerate(chips)]
        for cp in first:
            cp.start()
        passed = [copy(4 + j, (*chip, c), sibling) for j, chip in enumerate(chips)]
        for j, chip in enumerate(chips):
            copy(1 + j, (*chip, c), me).wait_recv()
            passed[j].start()
        copy(0, sibling, me).wait_recv()
        for j, chip in enumerate(chips):
            copy(4 + j, (*chip, 1 - c), me).wait_recv()
        for cp in first + passed:
            cp.wait_send()
        mine.wait()

    return pl.pallas_call(
        body, name=name, out_shape=jax.ShapeDtypeStruct((N_DEV * m_per, n), block.dtype),
        in_specs=[pl.BlockSpec(memory_space=pltpu.VMEM)], out_specs=pl.BlockSpec(memory_space=pltpu.VMEM),
        scratch_shapes=[pltpu.SemaphoreType.DMA((7,)), pltpu.SemaphoreType.DMA((7,)), pltpu.SemaphoreType.DMA],
        compiler_params=pltpu.CompilerParams(vmem_limit_bytes=VMEM_LIMIT),
    )(block)


def _gather_weights(shards):
    n_t = len(BIG)
    fulls = []
    for (_, axis), sh in zip(BIG, shards):
        shape = list(sh.shape)
        shape[axis] *= N_CHIP
        fulls.append(jax.ShapeDtypeStruct(tuple(shape), sh.dtype))

    def body(*refs):
        srcs, outs = refs[:n_t], refs[n_t:2 * n_t]
        send_sems, recv_sems, local_sems = refs[2 * n_t:]
        x, y, c, chips = _place()
        me = 2 * x + y
        copies = []
        for t, (_, axis) in enumerate(BIG):
            width = srcs[t].shape[axis]
            own = _shard_of(outs[t], axis, me, width)
            loc = pltpu.make_async_copy(srcs[t], own, local_sems.at[t])
            loc.start()
            copies.append(loc)
            for j, (px, py) in enumerate(chips):
                cp = pltpu.make_async_remote_copy(
                    src_ref=srcs[t], dst_ref=own, send_sem=send_sems.at[3 * t + j], recv_sem=recv_sems.at[3 * t + j],
                    device_id=(px, py, c), device_id_type=MESH)
                cp.start()
                copies.append(cp)
        for cp in copies:
            cp.wait()

    return pl.pallas_call(
        body, name="gather_weights", out_shape=fulls, in_specs=[ANY] * n_t, out_specs=[ANY] * n_t,
        scratch_shapes=[pltpu.SemaphoreType.DMA((3 * n_t,)), pltpu.SemaphoreType.DMA((3 * n_t,)),
                        pltpu.SemaphoreType.DMA((n_t,))],
    )(*shards)


def _exchange_grads(grads):
    n_t = len(BIG)
    outs_shape = []
    for (_, axis), g in zip(BIG, grads):
        shape = list(g.shape)
        shape[axis] //= N_CHIP
        outs_shape.append(jax.ShapeDtypeStruct((N_CHIP,) + tuple(shape), g.dtype))

    def body(*refs):
        srcs, outs = refs[:n_t], refs[n_t:2 * n_t]
        send_sems, recv_sems, local_sems = refs[2 * n_t:]
        x, y, c, chips = _place()
        me = 2 * x + y
        copies = []
        for t, (_, axis) in enumerate(BIG):
            width = outs[t].shape[1 + axis]
            loc = pltpu.make_async_copy(_shard_of(srcs[t], axis, me, width), outs[t].at[3], local_sems.at[t])
            loc.start()
            copies.append(loc)
            for j, (px, py) in enumerate(chips):
                cp = pltpu.make_async_remote_copy(
                    src_ref=_shard_of(srcs[t], axis, 2 * px + py, width), dst_ref=outs[t].at[j],
                    send_sem=send_sems.at[3 * t + j], recv_sem=recv_sems.at[3 * t + j],
                    device_id=(px, py, c), device_id_type=MESH)
                cp.start()
                copies.append(cp)
        for cp in copies:
            cp.wait()

    return pl.pallas_call(
        body, name="exchange_grads", out_shape=outs_shape, in_specs=[ANY] * n_t, out_specs=[ANY] * n_t,
        scratch_shapes=[pltpu.SemaphoreType.DMA((3 * n_t,)), pltpu.SemaphoreType.DMA((3 * n_t,)),
                        pltpu.SemaphoreType.DMA((n_t,))],
    )(*grads)


def _swap_sibling(parts):
    n_t = len(parts)

    def body(*refs):
        srcs, outs = refs[:n_t], refs[n_t:2 * n_t]
        send_sems, recv_sems = refs[2 * n_t:]
        x, y, c, _ = _place()
        copies = []
        for t in range(n_t):
            cp = pltpu.make_async_remote_copy(
                src_ref=srcs[t], dst_ref=outs[t], send_sem=send_sems.at[t], recv_sem=recv_sems.at[t],
                device_id=(x, y, 1 - c), device_id_type=MESH)
            cp.start()
            copies.append(cp)
        for cp in copies:
            cp.wait()

    return pl.pallas_call(
        body, name="swap_sibling", out_shape=[jax.ShapeDtypeStruct(p.shape, p.dtype) for p in parts],
        in_specs=[ANY] * n_t, out_specs=[ANY] * n_t,
        scratch_shapes=[pltpu.SemaphoreType.DMA((n_t,)), pltpu.SemaphoreType.DMA((n_t,))],
    )(*parts)


def _rows_view(a):
    if a.ndim == 1:
        return a.reshape(1, a.shape[0])
    return a.reshape(-1, a.shape[-1])


def _row_tile(rows, cols, target_elems=1 << 17):
    tr = rows
    while tr % 16 == 0 and tr * cols > target_elems:
        tr //= 2
    return tr


def _sum_leading(name, a):
    n, rows, cols = a.shape
    tr = _row_tile(rows, cols * n)

    def body(a_ref, o_ref):
        acc = a_ref[0].astype(F32)
        for i in range(1, n):
            acc = acc + a_ref[i].astype(F32)
        o_ref[...] = acc

    return pl.pallas_call(
        body, name=name, grid=(rows // tr,),
        in_specs=[pl.BlockSpec((n, tr, cols), lambda i: (0, i, 0))],
        out_specs=pl.BlockSpec((tr, cols), lambda i: (i, 0)),
        out_shape=jax.ShapeDtypeStruct((rows, cols), F32),
        compiler_params=_params(("parallel",)),
    )(a)


def _silu_rows(name, x):
    def body(x_ref, o_ref):
        v = x_ref[...]
        o_ref[...] = v * jax.nn.sigmoid(v)

    return pl.pallas_call(body, name=name, out_shape=jax.ShapeDtypeStruct(x.shape, F32))(x)


def _silu_bwd_rows(name, x, dy):
    def body(x_ref, dy_ref, o_ref):
        v = x_ref[...]
        sg = jax.nn.sigmoid(v)
        o_ref[...] = dy_ref[...] * (sg * (1.0 + v * (1.0 - sg)))

    return pl.pallas_call(body, name=name, out_shape=jax.ShapeDtypeStruct(x.shape, F32))(x, dy)


def _adamw(name, w, g_parts, m, v):
    shape = w.shape
    w2, m2, v2 = _rows_view(w), _rows_view(m), _rows_view(v)
    gs = [_rows_view(g) for g in g_parts]
    rows, cols = w2.shape
    tr = _row_tile(rows, cols)
    ng = len(gs)

    def body(*refs):
        w_ref, m_ref, v_ref = refs[0], refs[1], refs[2]
        g_refs = refs[3:3 + ng]
        g_out, d_out, m_out, v_out = refs[3 + ng:]
        g = g_refs[0][...]
        for r in g_refs[1:]:
            g = g + r[...]
        m1 = ADAM_B1 * m_ref[...] + (1.0 - ADAM_B1) * g
        v1 = ADAM_B2 * v_ref[...] + (1.0 - ADAM_B2) * jnp.square(g)
        m_hat = m1 / (1.0 - ADAM_B1 ** ADAM_STEP)
        v_hat = v1 / (1.0 - ADAM_B2 ** ADAM_STEP)
        g_out[...] = g
        d_out[...] = -ADAM_LR * (m_hat / (jnp.sqrt(v_hat) + ADAM_EPS) + ADAM_WD * w_ref[...])
        m_out[...] = m1
        v_out[...] = v1

    spec = pl.BlockSpec((tr, cols), lambda i: (i, 0))
    res = pl.pallas_call(
        body, name=name, grid=(rows // tr,), in_specs=[spec] * (3 + ng), out_specs=[spec] * 4,
        out_shape=[jax.ShapeDtypeStruct((rows, cols), F32)] * 4,
        compiler_params=_params(("parallel",)),
    )(w2, m2, v2, *gs)
    return tuple(r.reshape(shape) for r in res)


MOD_ROWS = 48


def kernel(x, c, ctx, c_ctx, mod_w, mod_b, norm1_g, norm2_g, ab_w_in, ab_w_out, ret_decay, att_q_norm_g, att_k_norm_g, cm_w_in, cm_v_norm_g, cm_w_s, cm_b_s, cm_w_out, ff_w1, ff_w2, loss_target, m_c_ctx, m_mod_w, m_mod_b, m_norm1_g, m_norm2_g, m_ab_w_in, m_ab_w_out, m_ret_decay, m_att_q_norm_g, m_att_k_norm_g, m_cm_w_in, m_cm_v_norm_g, m_cm_w_s, m_cm_b_s, m_cm_w_out, m_ff_w1, m_ff_w2, v_c_ctx, v_mod_w, v_mod_b, v_norm1_g, v_norm2_g, v_ab_w_in, v_ab_w_out, v_ret_decay, v_att_q_norm_g, v_att_k_norm_g, v_cm_w_in, v_cm_v_norm_g, v_cm_w_s, v_cm_b_s, v_cm_w_out, v_ff_w1, v_ff_w2):
    B, SL, D = x.shape
    cfg = Cfg(B=B, SC=ctx.shape[1], SL=SL, D=D, FF=ff_w1.shape[2] * N_CHIP)
    L = N_LAYERS
    n_ex = B * N_DEV
    mcols = mod_w.shape[2]
    weights = dict(c_ctx=c_ctx, mod_w=mod_w, mod_b=mod_b, norm1_g=norm1_g, norm2_g=norm2_g, ab_w_in=ab_w_in,
                   ab_w_out=ab_w_out, ret_decay=ret_decay, att_q_norm_g=att_q_norm_g, att_k_norm_g=att_k_norm_g,
                   cm_w_in=cm_w_in, cm_v_norm_g=cm_v_norm_g, cm_w_s=cm_w_s, cm_b_s=cm_b_s, cm_w_out=cm_w_out,
                   ff_w1=ff_w1, ff_w2=ff_w2)
    m_in = dict(c_ctx=m_c_ctx, mod_w=m_mod_w, mod_b=m_mod_b, norm1_g=m_norm1_g, norm2_g=m_norm2_g, ab_w_in=m_ab_w_in,
                ab_w_out=m_ab_w_out, ret_decay=m_ret_decay, att_q_norm_g=m_att_q_norm_g, att_k_norm_g=m_att_k_norm_g,
                cm_w_in=m_cm_w_in, cm_v_norm_g=m_cm_v_norm_g, cm_w_s=m_cm_w_s, cm_b_s=m_cm_b_s, cm_w_out=m_cm_w_out,
                ff_w1=m_ff_w1, ff_w2=m_ff_w2)
    v_in = dict(c_ctx=v_c_ctx, mod_w=v_mod_w, mod_b=v_mod_b, norm1_g=v_norm1_g, norm2_g=v_norm2_g, ab_w_in=v_ab_w_in,
                ab_w_out=v_ab_w_out, ret_decay=v_ret_decay, att_q_norm_g=v_att_q_norm_g, att_k_norm_g=v_att_k_norm_g,
                cm_w_in=v_cm_w_in, cm_v_norm_g=v_cm_v_norm_g, cm_w_s=v_cm_w_s, cm_b_s=v_cm_b_s, cm_w_out=v_cm_w_out,
                ff_w1=v_ff_w1, ff_w2=v_ff_w2)
    xi, yi, ci = lax.axis_index("x"), lax.axis_index("y"), lax.axis_index("c")
    chip = 2 * xi + yi
    dev = 2 * chip + ci

    full = dict(zip([n for n, _ in BIG], _gather_weights([weights[n].astype(BF16) for n, _ in BIG])))
    vgw = cm_v_norm_g.shape[1]
    blk = jnp.zeros((8, D), F32).at[:B].set(c).at[B:B + 2, :vgw].set(cm_v_norm_g)
    g0 = _allgather8("gather_c", blk).reshape(N_DEV, 8, D)
    c_all = g0[:, :B].reshape(n_ex, D)
    vg_full = jnp.concatenate([g0[2 * s, B:B + 2, :vgw] for s in range(N_CHIP)], axis=-1)

    pre = jnp.zeros((MOD_ROWS, D), F32).at[:n_ex].set(c_all).at[n_ex].set(c_ctx)
    act = _silu_rows("silu_c", pre)
    mpart = jnp.stack([_mm(f"mod_fwd_{l}", act, mod_w, mode="nn", layer=l, outs=[F32], tn=mcols) for l in range(L)])
    g1 = _allgather8("gather_mod", mpart.reshape(L * MOD_ROWS, mcols)).reshape(N_DEV, L, MOD_ROWS, mcols)
    mod_all = jnp.concatenate([g1[2 * s] for s in range(N_CHIP)], axis=-1) + mod_b[:, None, :]
    mod_lat = lax.dynamic_slice_in_dim(mod_all, dev * B, B, axis=1)
    mod_ctx = jnp.broadcast_to(mod_all[:, n_ex][:, None], mod_lat.shape)
    mods = jnp.stack([mod_ctx, mod_lat], axis=2).reshape(L, B, 2, 6, D)

    w = dict(full)
    w.update(norm1_g=norm1_g, norm2_g=norm2_g, ret_decay=ret_decay, att_q_norm_g=att_q_norm_g,
             att_k_norm_g=att_k_norm_g, cm_v_norm_g=vg_full, cm_w_s=cm_w_s, cm_b_s=cm_b_s)
    xcat = jnp.concatenate([ctx, x], axis=1).reshape(cfg.T, D)
    loss_local, dxcat, big, small, dmods = _local_step(cfg, xcat, loss_target.reshape(B * SL, D), mods, w)
    loss = lax.psum(loss_local, ("x", "y", "c"))
    grad_x = dxcat.reshape(B, cfg.S, D)[:, cfg.SC:, :]

    recv = _exchange_grads([jnp.stack(big[n]) for n, _ in BIG])
    part = [_sum_leading(f"sum_{n}", r.reshape(N_CHIP, -1, r.shape[-1])).reshape(r.shape[1:])
            for (n, _), r in zip(BIG, recv)]
    other = _swap_sibling(part)
    out = {}
    for (n, _), p_mine, p_other in zip(BIG, part, other):
        out[n] = _adamw(f"adamw_{n}", weights[n], [p_mine, p_other], m_in[n], v_in[n])

    dmod = jnp.stack(dmods).reshape(L, B, 2, 6 * D)
    dmod_lat = dmod[:, :, 1]
    dmod_ctx = jnp.sum(dmod[:, :, 0], axis=1)
    d_ret = jnp.stack(small["ret_lg"]) * jax.nn.sigmoid(-ret_decay)
    summed = [dmod_ctx.reshape(-1), jnp.stack(small["norm1_g"]).reshape(-1), jnp.stack(small["norm2_g"]).reshape(-1),
              jnp.stack(small["cm_v_norm_g"]).reshape(-1), jnp.stack(small["cm_w_s"]).reshape(-1),
              jnp.stack(small["cm_b_s"]).reshape(-1), jnp.stack(small["att_q_norm_g"]).reshape(-1),
              jnp.stack(small["att_k_norm_g"]).reshape(-1), d_ret.reshape(-1)]
    sizes = [int(a.shape[0]) for a in summed]
    flat = jnp.concatenate(summed + [dmod_lat.reshape(-1)])
    n_sum = sum(sizes)
    n_sum_rows = -(-n_sum // D)
    lat_rows = (L * B * 6 * D) // D
    pack_rows = -(-(n_sum_rows + lat_rows) // 8) * 8
    packed = jnp.zeros((pack_rows * D,), F32).at[:n_sum].set(flat[:n_sum])
    packed = packed.at[n_sum_rows * D:(n_sum_rows + lat_rows) * D].set(flat[n_sum:]).reshape(pack_rows, D)
    g2 = _allgather8("gather_small", packed).reshape(N_DEV, pack_rows, D)
    tot = _sum_leading("sum_small", g2[:, :n_sum_rows]).reshape(-1)
    pieces, off = [], 0
    for sz in sizes:
        pieces.append(tot[off:off + sz])
        off += sz
    dmod_ctx_t, g_n1, g_n2, g_vg, g_ws, g_bs, g_qg, g_kg, g_rd = pieces
    dmod_ctx_t = dmod_ctx_t.reshape(L, 6 * D)
    dmod_lat_all = g2[:, n_sum_rows:n_sum_rows + lat_rows].reshape(N_DEV, L, B, 6 * D)
    dmod_rows = jnp.zeros((L, MOD_ROWS, 6 * D), F32)
    dmod_rows = dmod_rows.at[:, :n_ex].set(jnp.transpose(dmod_lat_all, (1, 0, 2, 3)).reshape(L, n_ex, 6 * D))
    dmod_rows = dmod_rows.at[:, n_ex].set(dmod_ctx_t)
    g_mod_b = _sum_leading("sum_mod_b", jnp.transpose(dmod_rows, (1, 0, 2)))
    dmod_mine = lax.dynamic_slice_in_dim(dmod_rows, chip * mcols, mcols, axis=2)
    g_mod_w = jnp.stack([_mm(f"mod_dw_{l}", act, dmod_mine[l], mode="tn", outs=[F32], tn=mcols) for l in range(L)])
    ctx8 = jnp.zeros((L, 8, mcols), F32).at[:, 0].set(dmod_mine[:, n_ex])
    dcc = [_mm(f"mod_dctx_{l}", ctx8[l], mod_w, mode="nt", layer=l, outs=[F32], tk=mcols) for l in range(L)]
    dcc = _sum_leading("sum_dctx_layers", jnp.stack(dcc))
    g3 = _allgather8("gather_dctx", dcc).reshape(N_DEV, 8, D)
    dcc_t = _sum_leading("sum_dctx_chips", g3[0::2])[0:1]
    g_c_ctx = _silu_bwd_rows("silu_bwd_cctx", c_ctx[None], dcc_t)[0]

    vg_mine = lax.dynamic_slice_in_dim(g_vg.reshape(2, -1), chip * vgw, vgw, axis=1)
    small_g = dict(c_ctx=g_c_ctx, mod_w=g_mod_w, mod_b=g_mod_b, norm1_g=g_n1.reshape(norm1_g.shape),
                   norm2_g=g_n2.reshape(norm2_g.shape), ret_decay=g_rd.reshape(ret_decay.shape),
                   att_q_norm_g=g_qg.reshape(att_q_norm_g.shape), att_k_norm_g=g_kg.reshape(att_k_norm_g.shape),
                   cm_v_norm_g=vg_mine, cm_w_s=g_ws.reshape(cm_w_s.shape), cm_b_s=g_bs.reshape(cm_b_s.shape))
    for n, g in small_g.items():
        out[n] = _adamw(f"adamw_{n}", weights[n], [g], m_in[n], v_in[n])

    order = list(weights)
    return (loss, grad_x, *[out[n][0] for n in order], *[out[n][1] for n in order],
            *[out[n][2] for n in order], *[out[n][3] for n in order])
```

```python
import functools
import math
from typing import NamedTuple

import jax
import jax.numpy as jnp
from jax import lax
from jax.experimental import pallas as pl
from jax.experimental.pallas import tpu as pltpu

F32 = jnp.float32
BF16 = jnp.bfloat16
EPS = 1e-6
ROPE_BASE = 10000.0
LANES = 128
CHUNK = 128
N_LAYERS = 4
VMEM_LIMIT = 56 * 1024 * 1024

ADAM_LR = 0.001
ADAM_B1 = 0.9
ADAM_B2 = 0.999
ADAM_EPS = 1e-08
ADAM_WD = 0.01
ADAM_STEP = 10


class Cfg(NamedTuple):
    B: int = 4
    SC: int = 256
    SL: int = 2048
    D: int = 1024
    FF: int = 4096
    GRID_W: int = 64
    H: int = 4
    KV: int = 2
    CMW: int = 1024
    CMG: int = 8

    @property
    def S(self):
        return self.SC + self.SL

    @property
    def T(self):
        return self.B * self.S

    @property
    def TM(self):
        return self.SC

    @property
    def TPE(self):
        return self.S // self.SC

    @property
    def ABW(self):
        return (5 * self.H + 2 * self.KV) * CHUNK


def _tile(dim, pref):
    t = min(dim, pref)
    while dim % t:
        t -= LANES
    return t


def _dot(a, b):
    return lax.dot_general(a, b, (((1,), (0,)), ((), ())), preferred_element_type=F32)


def _dot_nt(a, b):
    return lax.dot_general(a, b, (((1,), (1,)), ((), ())), preferred_element_type=F32)


def _dot_tn(a, b):
    return lax.dot_general(a, b, (((0,), (0,)), ((), ())), preferred_element_type=F32)


def _params(sem, vmem=VMEM_LIMIT):
    return pltpu.CompilerParams(dimension_semantics=sem, vmem_limit_bytes=vmem)


def _mod_index(cfg):
    tpe = cfg.TPE
    return lambda i: (i // tpe, jnp.minimum(i % tpe, 1), 0, 0)


def _mm(name, a, b, *, mode, outs, tm=1024, tn=1024, tk=1024, layer=None, epi=None, extras=()):
    bshape = b.shape[1:] if layer is not None else b.shape
    if mode == "nn":
        (M, K), N = a.shape, bshape[1]
    elif mode == "nt":
        (M, K), N = a.shape, bshape[0]
    else:
        (K, M), N = a.shape, bshape[1]
    tm, tn, tk = _tile(M, tm), _tile(N, tn), _tile(K, tk)
    nk = K // tk
    a_spec = (pl.BlockSpec((tk, tm), lambda i, j, k: (k, i)) if mode == "tn"
              else pl.BlockSpec((tm, tk), lambda i, j, k: (i, k)))
    if mode == "nt":
        bblk, bidx = (tn, tk), (lambda i, j, k: (j, k))
    else:
        bblk, bidx = (tk, tn), (lambda i, j, k: (k, j))
    if layer is not None:
        b_spec = pl.BlockSpec((None,) + bblk, lambda i, j, k: (layer,) + bidx(i, j, k))
    else:
        b_spec = pl.BlockSpec(bblk, bidx)
    ne, no = len(extras), len(outs)
    dot = {"nn": _dot, "nt": _dot_nt, "tn": _dot_tn}[mode]

    def body(*refs):
        a_ref, b_ref = refs[0], refs[1]
        ex, out_refs = refs[2:2 + ne], refs[2 + ne:2 + ne + no]
        row_tile = pl.program_id(0)

        def finish(acc):
            res = epi(acc, row_tile, *ex) if epi is not None else (acc,)
            for r, o in zip(res, out_refs):
                o[...] = r.astype(o.dtype)

        part = dot(a_ref[...].astype(BF16), b_ref[...].astype(BF16))
        if nk == 1:
            finish(part)
        else:
            acc_ref = refs[-1]
            k = pl.program_id(2)

            @pl.when(k == 0)
            def _():
                acc_ref[...] = part

            @pl.when(k > 0)
            def _():
                acc_ref[...] += part

            @pl.when(k == nk - 1)
            def _():
                finish(acc_ref[...])

    res = pl.pallas_call(
        body, name=name, grid=(M // tm, N // tn, nk),
        in_specs=[a_spec, b_spec] + [s for _, s in extras],
        out_specs=[pl.BlockSpec((tm, tn), lambda i, j, k: (i, j)) for _ in outs],
        out_shape=[jax.ShapeDtypeStruct((M, N), d) for d in outs],
        scratch_shapes=[pltpu.VMEM((tm, tn), F32)] if nk > 1 else [],
        compiler_params=_params(("parallel", "parallel", "arbitrary")),
    )(a, b, *[x for x, _ in extras])
    return res[0] if no == 1 else res


def _norm_mod_fwd(cfg, name, x, gain, mod, ish, isc):
    T, D, TM = cfg.T, cfg.D, cfg.TM

    def body(x_ref, g_ref, mod_ref, h_ref):
        x = x_ref[...]
        rstd = lax.rsqrt(jnp.mean(x * x, axis=-1, keepdims=True) + EPS)
        n = x * rstd * g_ref[...]
        h = n * (1.0 + mod_ref[pl.ds(isc, 1), :]) + mod_ref[pl.ds(ish, 1), :]
        h_ref[...] = h.astype(BF16)

    return pl.pallas_call(
        body, name=name, grid=(T // TM,),
        in_specs=[pl.BlockSpec((TM, D), lambda i: (i, 0)), pl.BlockSpec((1, D), lambda i: (0, 0)),
                  pl.BlockSpec((None, None, 6, D), _mod_index(cfg))],
        out_specs=pl.BlockSpec((TM, D), lambda i: (i, 0)),
        out_shape=jax.ShapeDtypeStruct((T, D), BF16),
        compiler_params=_params(("parallel",)),
    )(x, gain, mod)


def _norm_mod_bwd(cfg, name, x, gain, mod, ish, isc, dh, dres):
    T, D, TM, TPE = cfg.T, cfg.D, cfg.TM, cfg.TPE

    def body(x_ref, g_ref, mod_ref, dh_ref, dres_ref, dx_ref, dmod_ref, dgain_ref):
        i = pl.program_id(0)
        t = i % TPE
        x = x_ref[...]
        g = g_ref[...]
        dh = dh_ref[...].astype(F32)
        rstd = lax.rsqrt(jnp.mean(x * x, axis=-1, keepdims=True) + EPS)
        xhat = x * rstd
        dn = dh * (1.0 + mod_ref[pl.ds(isc, 1), :])
        dsh = jnp.sum(dh, axis=0, keepdims=True)
        dsc = jnp.sum(dh * (xhat * g), axis=0, keepdims=True)
        dgain = jnp.sum(dn * xhat, axis=0, keepdims=True)
        dxh = dn * g
        dx = rstd * (dxh - xhat * jnp.mean(dxh * xhat, axis=-1, keepdims=True))
        dx_ref[...] = dx + dres_ref[...]

        @pl.when(t <= 1)
        def _():
            dmod_ref[pl.ds(0, 1), :] = dsh
            dmod_ref[pl.ds(1, 1), :] = dsc

        @pl.when(t > 1)
        def _():
            dmod_ref[pl.ds(0, 1), :] += dsh
            dmod_ref[pl.ds(1, 1), :] += dsc

        @pl.when(i == 0)
        def _():
            dgain_ref[...] = dgain

        @pl.when(i > 0)
        def _():
            dgain_ref[...] += dgain

    tok = pl.BlockSpec((TM, D), lambda i: (i, 0))
    return pl.pallas_call(
        body, name=name, grid=(T // TM,),
        in_specs=[tok, pl.BlockSpec((1, D), lambda i: (0, 0)), pl.BlockSpec((None, None, 6, D), _mod_index(cfg)),
                  tok, tok],
        out_specs=[tok, pl.BlockSpec((None, None, 2, D), _mod_index(cfg)), pl.BlockSpec((1, D), lambda i: (0, 0))],
        out_shape=[jax.ShapeDtypeStruct((T, D), F32), jax.ShapeDtypeStruct((cfg.B, 2, 2, D), F32),
                   jax.ShapeDtypeStruct((1, D), F32)],
        compiler_params=_params(("arbitrary",)),
    )(x, gain, mod, dh, dres)


def _gate_bwd(cfg, name, dx, y, mod, igate):
    T, D, TM, TPE = cfg.T, cfg.D, cfg.TM, cfg.TPE

    def body(dx_ref, y_ref, mod_ref, dy_ref, dg_ref):
        t = pl.program_id(0) % TPE
        dx = dx_ref[...]
        dy_ref[...] = (dx * mod_ref[pl.ds(igate, 1), :]).astype(BF16)
        dg = jnp.sum(dx * y_ref[...].astype(F32), axis=0, keepdims=True)

        @pl.when(t <= 1)
        def _():
            dg_ref[...] = dg

        @pl.when(t > 1)
        def _():
            dg_ref[...] += dg

    tok = pl.BlockSpec((TM, D), lambda i: (i, 0))
    return pl.pallas_call(
        body, name=name, grid=(T // TM,),
        in_specs=[tok, tok, pl.BlockSpec((None, None, 6, D), _mod_index(cfg))],
        out_specs=[tok, pl.BlockSpec((None, None, 1, D), _mod_index(cfg))],
        out_shape=[jax.ShapeDtypeStruct((T, D), BF16), jax.ShapeDtypeStruct((cfg.B, 2, 1, D), F32)],
        compiler_params=_params(("arbitrary",)),
    )(dx, y, mod)


def _loss_grad(cfg, x, tgt):
    T, D, TM, TPE = cfg.T, cfg.D, cfg.TM, cfg.TPE

    def body(x_ref, t_ref, dx_ref, loss_ref):
        i = pl.program_id(0)
        t = i % TPE

        @pl.when(i == 0)
        def _():
            loss_ref[...] = jnp.zeros_like(loss_ref)

        @pl.when(t == 0)
        def _():
            dx_ref[...] = jnp.zeros_like(dx_ref)

        @pl.when(t > 0)
        def _():
            err = x_ref[...] - t_ref[...]
            dx_ref[...] = err * (1.0 / D)
            loss_ref[...] += 0.5 * jnp.sum(jnp.mean(err * err, axis=-1, keepdims=True), axis=0, keepdims=True)

    tok = pl.BlockSpec((TM, D), lambda i: (i, 0))
    tgt_spec = pl.BlockSpec((TM, D), lambda i: ((i // TPE) * (TPE - 1) + jnp.maximum(i % TPE - 1, 0), 0))
    dx, loss = pl.pallas_call(
        body, name="loss_grad", grid=(T // TM,),
        in_specs=[tok, tgt_spec], out_specs=[tok, pl.BlockSpec((8, LANES), lambda i: (0, 0))],
        out_shape=[jax.ShapeDtypeStruct((T, D), F32), jax.ShapeDtypeStruct((8, LANES), F32)],
        compiler_params=_params(("arbitrary",)),
    )(x, tgt)
    return loss[0, 0], dx


def _rope_tables(cfg):
    rows = cfg.SL // cfg.GRID_W
    row = jnp.repeat(jnp.arange(rows, dtype=F32), cfg.GRID_W)
    col = jnp.tile(jnp.arange(cfg.GRID_W, dtype=F32), rows)
    n_freq = CHUNK // 4
    inv = ROPE_BASE ** (-jnp.arange(n_freq, dtype=F32) / n_freq)
    ang = jnp.concatenate([row[:, None] * inv[None, :], col[:, None] * inv[None, :]], axis=-1)
    cos, sin = jnp.cos(ang), jnp.sin(ang)
    cosf = jnp.concatenate([jnp.ones((cfg.SC, CHUNK), F32), jnp.concatenate([cos, cos], axis=-1)], axis=0)
    sinf = jnp.concatenate([jnp.zeros((cfg.SC, CHUNK), F32), jnp.concatenate([-sin, sin], axis=-1)], axis=0)
    return cosf, sinf


def _rope(x, cosf, sinf):
    return x * cosf + pltpu.roll(x, CHUNK // 2, 1) * sinf


def _irope(dy, cosf, sinf):
    return dy * cosf - pltpu.roll(dy, CHUNK // 2, 1) * sinf


def _prep_fwd(cfg, name, p, cosf, sinf, qg, kg):
    T, TM, TPE, H, KV = cfg.T, cfg.TM, cfg.TPE, cfg.H, cfg.KV
    HW = H * CHUNK
    kscale = CHUNK ** -0.5

    def body(p_ref, c_ref, s_ref, qg_ref, kg_ref, rq_ref, rk_ref, aq_ref, ak_ref):
        cosf, sinf = c_ref[...], s_ref[...]

        def normed(x, g):
            return x * lax.rsqrt(jnp.mean(x * x, axis=-1, keepdims=True) + EPS) * g

        for h in range(H):
            sl = pl.ds(h * CHUNK, CHUNK)
            rq_ref[:, sl] = _rope(p_ref[:, pl.ds(h * CHUNK, CHUNK)], cosf, sinf)
            rk_ref[:, sl] = _rope(p_ref[:, pl.ds(HW + h * CHUNK, CHUNK)], cosf, sinf) * kscale
            aq_ref[:, sl] = _rope(normed(p_ref[:, pl.ds(4 * HW + h * CHUNK, CHUNK)], qg_ref[...]),
                                  cosf, sinf).astype(BF16)
        for h in range(KV):
            ak_ref[:, pl.ds(h * CHUNK, CHUNK)] = _rope(
                normed(p_ref[:, pl.ds(5 * HW + h * CHUNK, CHUNK)], kg_ref[...]), cosf, sinf).astype(BF16)

    tab = pl.BlockSpec((TM, CHUNK), lambda i: (i % TPE, 0))
    vec = pl.BlockSpec((1, CHUNK), lambda i: (0, 0))
    return pl.pallas_call(
        body, name=name, grid=(T // TM,),
        in_specs=[pl.BlockSpec((TM, cfg.ABW), lambda i: (i, 0)), tab, tab, vec, vec],
        out_specs=[pl.BlockSpec((TM, HW), lambda i: (i, 0))] * 3 + [pl.BlockSpec((TM, KV * CHUNK), lambda i: (i, 0))],
        out_shape=[jax.ShapeDtypeStruct((T, HW), F32), jax.ShapeDtypeStruct((T, HW), F32),
                   jax.ShapeDtypeStruct((T, HW), BF16), jax.ShapeDtypeStruct((T, KV * CHUNK), BF16)],
        compiler_params=_params(("parallel",)),
    )(p, cosf, sinf, qg, kg)


def _prep_bwd(cfg, name, p, cosf, sinf, qg, kg, d_rq, d_rk, d_rv, d_gate, d_aq, d_ak, d_av):
    T, TM, TPE, H, KV = cfg.T, cfg.TM, cfg.TPE, cfg.H, cfg.KV
    HW = H * CHUNK
    kscale = CHUNK ** -0.5

    def body(p_ref, c_ref, s_ref, qg_ref, kg_ref, drq_ref, drk_ref, drv_ref, dgt_ref, daq_ref, dak_ref, dav_ref,
             dp_ref, dqg_ref, dkg_ref):
        i = pl.program_id(0)
        cosf, sinf = c_ref[...], s_ref[...]

        def norm_bwd(x, g, dn):
            rstd = lax.rsqrt(jnp.mean(x * x, axis=-1, keepdims=True) + EPS)
            xhat = x * rstd
            dg = jnp.sum(dn * xhat, axis=0, keepdims=True)
            dxh = dn * g
            return rstd * (dxh - xhat * jnp.mean(dxh * xhat, axis=-1, keepdims=True)), dg

        dqg = jnp.zeros((1, CHUNK), F32)
        dkg = jnp.zeros((1, CHUNK), F32)
        for h in range(H):
            sl = pl.ds(h * CHUNK, CHUNK)
            dp_ref[:, pl.ds(h * CHUNK, CHUNK)] = _irope(drq_ref[:, sl], cosf, sinf).astype(BF16)
            dp_ref[:, pl.ds(HW + h * CHUNK, CHUNK)] = (_irope(drk_ref[:, sl], cosf, sinf) * kscale).astype(BF16)
            dp_ref[:, pl.ds(2 * HW + h * CHUNK, CHUNK)] = drv_ref[:, sl].astype(BF16)
            dp_ref[:, pl.ds(3 * HW + h * CHUNK, CHUNK)] = dgt_ref[:, sl].astype(BF16)
            dx, dg = norm_bwd(p_ref[:, pl.ds(4 * HW + h * CHUNK, CHUNK)], qg_ref[...],
                              _irope(daq_ref[:, sl], cosf, sinf))
            dp_ref[:, pl.ds(4 * HW + h * CHUNK, CHUNK)] = dx.astype(BF16)
            dqg = dqg + dg
        for h in range(KV):
            sl = pl.ds(h * CHUNK, CHUNK)
            dx, dg = norm_bwd(p_ref[:, pl.ds(5 * HW + h * CHUNK, CHUNK)], kg_ref[...],
                              _irope(dak_ref[:, sl], cosf, sinf))
            dp_ref[:, pl.ds(5 * HW + h * CHUNK, CHUNK)] = dx.astype(BF16)
            dp_ref[:, pl.ds(5 * HW + (KV + h) * CHUNK, CHUNK)] = dav_ref[:, sl].astype(BF16)
            dkg = dkg + dg

        @pl.when(i == 0)
        def _():
            dqg_ref[...] = dqg
            dkg_ref[...] = dkg

        @pl.when(i > 0)
        def _():
            dqg_ref[...] += dqg
            dkg_ref[...] += dkg

    tab = pl.BlockSpec((TM, CHUNK), lambda i: (i % TPE, 0))
    vec = pl.BlockSpec((1, CHUNK), lambda i: (0, 0))
    hw = pl.BlockSpec((TM, HW), lambda i: (i, 0))
    kvw = pl.BlockSpec((TM, KV * CHUNK), lambda i: (i, 0))
    return pl.pallas_call(
        body, name=name, grid=(T // TM,),
        in_specs=[pl.BlockSpec((TM, cfg.ABW), lambda i: (i, 0)), tab, tab, vec, vec, hw, hw, hw, hw, hw, kvw, kvw],
        out_specs=[pl.BlockSpec((TM, cfg.ABW), lambda i: (i, 0)), vec, vec],
        out_shape=[jax.ShapeDtypeStruct((T, cfg.ABW), BF16), jax.ShapeDtypeStruct((1, CHUNK), F32),
                   jax.ShapeDtypeStruct((1, CHUNK), F32)],
        compiler_params=_params(("arbitrary",)),
    )(p, cosf, sinf, qg, kg, d_rq, d_rk, d_rv, d_gate, d_aq, d_ak, d_av)


def _ret_consts(direction, lg):
    C = CHUNK
    ii = lax.broadcasted_iota(jnp.int32, (C, C), 0)
    jj = lax.broadcasted_iota(jnp.int32, (C, C), 1)
    col = lax.broadcasted_iota(jnp.int32, (C, 1), 0).astype(F32)
    if direction == 0:
        mask, er, ek, eq = ii >= jj, (ii - jj).astype(F32), (C - 1.0) - col, col + 1.0
    else:
        mask, er, ek, eq = jj >= ii, (jj - ii).astype(F32), col, C - col
    er = jnp.where(mask, er, 0.0)
    dm = jnp.where(mask, jnp.exp(er * lg), 0.0)
    return dm, er, jnp.exp(ek * lg), ek, jnp.exp(eq * lg), eq, jnp.exp(C * lg)


def _ret_chunk(cfg, direction, t):
    n_all, n_ctx = cfg.S // CHUNK, cfg.SC // CHUNK
    if direction == 0:
        return t
    return jnp.where(t < n_ctx, n_ctx - 1 - t, n_all - 1 - (t - n_ctx))


def _head_norm_gate(o, g):
    mu = jnp.mean(o, axis=-1, keepdims=True)
    var = jnp.mean(jnp.square(o - mu), axis=-1, keepdims=True)
    rstd = lax.rsqrt(var + EPS)
    y = (o - mu) * rstd
    sg = jax.nn.sigmoid(g)
    return y, rstd, sg


def _retention_fwd(cfg, name, rq, rk, p, lgb):
    B, H, S, T = cfg.B, cfg.H, cfg.S, cfg.T
    n_all = S // CHUNK

    def body(q_ref, k_ref, v_ref, g_ref, lg_ref, o_ref, ret_ref, st_ref):
        for direction in (0, 1):
            dm, _, kd, _, qd, _, cd = _ret_consts(direction, lg_ref[direction][0:1, 0:1])
            st_ref[...] = jnp.zeros_like(st_ref)

            def step(t, carry):
                n = _ret_chunk(cfg, direction, t)
                sl = pl.ds(pl.multiple_of(n * CHUNK, CHUNK), CHUNK)
                q = q_ref[sl, :].astype(BF16)
                k = k_ref[sl, :]
                v = v_ref[sl, :].astype(BF16)
                st = st_ref[...]
                s = _dot_nt(q, k.astype(BF16)) * dm
                o = _dot(s.astype(BF16), v) + _dot(q, st.astype(BF16)) * qd
                if direction == 0:
                    o_ref[sl, :] = o
                else:
                    o_ref[sl, :] += o
                st_ref[...] = cd * st + _dot_tn((k * kd).astype(BF16), v)
                return carry

            lax.fori_loop(0, n_all, step, 0)

        def gate_step(n, carry):
            sl = pl.ds(pl.multiple_of(n * CHUNK, CHUNK), CHUNK)
            g = g_ref[sl, :]
            y, _, sg = _head_norm_gate(o_ref[sl, :], g)
            ret_ref[sl, :] = (y * (g * sg)).astype(BF16)
            return carry

        lax.fori_loop(0, n_all, gate_step, 0)

    HW = H * CHUNK
    blk = lambda off: pl.BlockSpec((S, CHUNK), lambda b, h: (b, off + h))
    return pl.pallas_call(
        body, name=name, grid=(B, H),
        in_specs=[blk(0), blk(0), blk(2 * H), blk(3 * H),
                  pl.BlockSpec((None, 2, 8, LANES), lambda b, h: (h, 0, 0, 0))],
        out_specs=[blk(0), blk(0)],
        out_shape=[jax.ShapeDtypeStruct((T, HW), F32), jax.ShapeDtypeStruct((T, HW), BF16)],
        scratch_shapes=[pltpu.VMEM((CHUNK, CHUNK), F32)],
        compiler_params=_params(("parallel", "parallel")),
    )(rq, rk, p, p, lgb)


def _retention_bwd(cfg, name, rq, rk, p, o_sum, dcat, lgb):
    B, H, S, T = cfg.B, cfg.H, cfg.S, cfg.T
    n_all = S // CHUNK
    C = CHUNK

    def body(q_ref, k_ref, v_ref, g_ref, o_ref, dr_ref, lg_ref, dq_ref, dk_ref, dv_ref, dg_ref, dlg_ref,
             do_ref, st_ref, ds_ref, acc_ref):
        def gate_step(n, carry):
            sl = pl.ds(pl.multiple_of(n * C, C), C)
            g = g_ref[sl, :]
            dr = dr_ref[sl, :]
            y, rstd, sg = _head_norm_gate(o_ref[sl, :], g)
            dy = dr * (g * sg)
            dg_ref[sl, :] = dr * y * (sg * (1.0 + g * (1.0 - sg)))
            do_ref[sl, :] = rstd * (dy - jnp.mean(dy, axis=-1, keepdims=True)
                                    - y * jnp.mean(dy * y, axis=-1, keepdims=True))
            return carry

        lax.fori_loop(0, n_all, gate_step, 0)

        for direction in (0, 1):
            dm, er, kd, ek, qd, eq, cd = _ret_consts(direction, lg_ref[direction][0:1, 0:1])

            def fwd_step(t, st):
                n = _ret_chunk(cfg, direction, t)
                sl = pl.ds(pl.multiple_of(n * C, C), C)
                st_ref[t] = st
                return cd * st + _dot_tn((k_ref[sl, :] * kd).astype(BF16), v_ref[sl, :].astype(BF16))

            lax.fori_loop(0, n_all, fwd_step, jnp.zeros((C, C), F32))
            ds_ref[...] = jnp.zeros_like(ds_ref)
            acc_ref[...] = jnp.zeros_like(acc_ref)

            def bwd_step(u, carry):
                t = n_all - 1 - u
                n = _ret_chunk(cfg, direction, t)
                sl = pl.ds(pl.multiple_of(n * C, C), C)
                q = q_ref[sl, :].astype(BF16)
                kf = k_ref[sl, :]
                k = kf.astype(BF16)
                v = v_ref[sl, :].astype(BF16)
                do = do_ref[sl, :]
                dob = do.astype(BF16)
                sp = st_ref[t]
                spb = sp.astype(BF16)
                ds = ds_ref[...]
                dsb = ds.astype(BF16)
                dk_state = _dot_nt(v, dsb) * kd
                dv = _dot((kf * kd).astype(BF16), dsb)
                pm = _dot_nt(q, k) * dm
                dpm = _dot_nt(dob, v)
                dsr = (dpm * dm).astype(BF16)
                dq = _dot(dsr, k)
                dk = _dot_tn(dsr, q) + dk_state
                dv = dv + _dot_tn(pm.astype(BF16), dob)
                doq = do * qd
                doqb = doq.astype(BF16)
                qs = _dot(q, spb)
                dq = dq + _dot_nt(doqb, spb)
                acc_ref[...] += (jnp.sum(dpm * pm * er, axis=0, keepdims=True)
                                 + jnp.sum(eq * doq * qs, axis=0, keepdims=True)
                                 + jnp.sum(ek * kf * dk_state, axis=0, keepdims=True)
                                 + (C * cd) * jnp.sum(ds * sp, axis=0, keepdims=True))
                ds_ref[...] = cd * ds + _dot_tn(q, doqb)
                if direction == 0:
                    dq_ref[sl, :] = dq
                    dk_ref[sl, :] = dk
                    dv_ref[sl, :] = dv
                else:
                    dq_ref[sl, :] += dq
                    dk_ref[sl, :] += dk
                    dv_ref[sl, :] += dv
                return carry

            lax.fori_loop(0, n_all, bwd_step, 0)
            dlg_ref[direction] = jnp.broadcast_to(jnp.sum(acc_ref[...], axis=1, keepdims=True), (8, LANES))

    HW = H * CHUNK
    blk = lambda off: pl.BlockSpec((S, CHUNK), lambda b, h: (b, off + h))
    return pl.pallas_call(
        body, name=name, grid=(B, H),
        in_specs=[blk(0), blk(0), blk(2 * H), blk(3 * H), blk(0), blk(0),
                  pl.BlockSpec((None, 2, 8, LANES), lambda b, h: (h, 0, 0, 0))],
        out_specs=[blk(0)] * 4 + [pl.BlockSpec((None, None, 2, 8, LANES), lambda b, h: (b, h, 0, 0, 0))],
        out_shape=[jax.ShapeDtypeStruct((T, HW), F32)] * 4 + [jax.ShapeDtypeStruct((B, H, 2, 8, LANES), F32)],
        scratch_shapes=[pltpu.VMEM((S, CHUNK), F32), pltpu.VMEM((n_all, C, C), F32), pltpu.VMEM((C, C), F32),
                        pltpu.VMEM((1, C), F32)],
        compiler_params=_params(("parallel", "parallel")),
    )(rq, rk, p, p, o_sum, dcat, lgb)


def _attn_scores(cfg, q, k, t):
    kcol = lax.broadcasted_iota(jnp.int32, (1, cfg.S), 1)
    bias = jnp.where(jnp.logical_or(t > 0, kcol < cfg.SC), 0.0, -1e30)
    s = _dot_nt(q, k) * (CHUNK ** -0.5) + bias
    e = jnp.exp(s - jnp.max(s, axis=-1, keepdims=True))
    return e, 1.0 / jnp.sum(e, axis=-1, keepdims=True)


def _attention_fwd(cfg, name, aq, ak, p):
    B, H, KV, S, T, TM, TPE = cfg.B, cfg.H, cfg.KV, cfg.S, cfg.T, cfg.TM, cfg.TPE
    G = H // KV
    v_off = (5 * H + KV)

    def body(q_ref, k_ref, v_ref, o_ref):
        e, inv = _attn_scores(cfg, q_ref[...], k_ref[...], pl.program_id(2))
        o_ref[...] = (_dot(e.astype(BF16), v_ref[...].astype(BF16)) * inv).astype(BF16)

    return pl.pallas_call(
        body, name=name, grid=(B, H, TPE),
        in_specs=[pl.BlockSpec((TM, CHUNK), lambda b, h, t: (b * TPE + t, h)),
                  pl.BlockSpec((S, CHUNK), lambda b, h, t: (b, h // G)),
                  pl.BlockSpec((S, CHUNK), lambda b, h, t: (b, v_off + h // G))],
        out_specs=pl.BlockSpec((TM, CHUNK), lambda b, h, t: (b * TPE + t, h)),
        out_shape=jax.ShapeDtypeStruct((T, H * CHUNK), BF16),
        compiler_params=_params(("parallel", "parallel", "parallel")),
    )(aq, ak, p)


def _attention_bwd(cfg, name, aq, ak, p, dcat):
    B, H, KV, S, T, TM, TPE = cfg.B, cfg.H, cfg.KV, cfg.S, cfg.T, cfg.TM, cfg.TPE
    G = H // KV
    v_off = (5 * H + KV)

    def body(q_ref, k_ref, v_ref, do_ref, dq_ref, dk_ref, dv_ref):
        g, t = pl.program_id(2), pl.program_id(3)
        q, k = q_ref[...], k_ref[...]
        v = v_ref[...].astype(BF16)
        dob = do_ref[...].astype(BF16)
        e, inv = _attn_scores(cfg, q, k, t)
        pr = e * inv
        dpr = _dot_nt(dob, v)
        ds = (pr * (dpr - jnp.sum(pr * dpr, axis=-1, keepdims=True)) * (CHUNK ** -0.5)).astype(BF16)
        dq_ref[...] = _dot(ds, k)
        dk = _dot_tn(ds, q)
        dv = _dot_tn(pr.astype(BF16), dob)
        first = jnp.logical_and(g == 0, t == 0)

        @pl.when(first)
        def _():
            dk_ref[...] = dk
            dv_ref[...] = dv

        @pl.when(jnp.logical_not(first))
        def _():
            dk_ref[...] += dk
            dv_ref[...] += dv

    qspec = pl.BlockSpec((TM, CHUNK), lambda b, kv, g, t: (b * TPE + t, kv * G + g))
    kvspec = pl.BlockSpec((S, CHUNK), lambda b, kv, g, t: (b, kv))
    return pl.pallas_call(
        body, name=name, grid=(B, KV, G, TPE),
        in_specs=[qspec, kvspec, pl.BlockSpec((S, CHUNK), lambda b, kv, g, t: (b, v_off + kv)),
                  pl.BlockSpec((TM, CHUNK), lambda b, kv, g, t: (b * TPE + t, H + kv * G + g))],
        out_specs=[qspec, kvspec, kvspec],
        out_shape=[jax.ShapeDtypeStruct((T, H * CHUNK), F32), jax.ShapeDtypeStruct((T, KV * CHUNK), F32),
                   jax.ShapeDtypeStruct((T, KV * CHUNK), F32)],
        compiler_params=_params(("parallel", "parallel", "arbitrary", "arbitrary")),
    )(aq, ak, p, dcat)


_GELU_C = math.sqrt(2.0 / math.pi)


def _gelu(x):
    return 0.5 * x * (1.0 + jnp.tanh(_GELU_C * (x + 0.044715 * x * x * x)))


def _gelu_grad(x):
    th = jnp.tanh(_GELU_C * (x + 0.044715 * x * x * x))
    return 0.5 * (1.0 + th) + 0.5 * x * (1.0 - th * th) * _GELU_C * (1.0 + 3.0 * 0.044715 * x * x)


def _cm_fwd(cfg, name, a, vg, ws, bs):
    T, TM, W, NG = cfg.T, cfg.TM, cfg.CMW, cfg.CMG

    def body(a_ref, vg_ref, ws_ref, bs_ref, m_ref):
        v = _gelu(a_ref[:, pl.ds(W, W)])
        vn = (v * lax.rsqrt(jnp.mean(v * v, axis=-1, keepdims=True) + EPS) * vg_ref[...]).astype(BF16)
        for c in range(TM // CHUNK):
            for g in range(NG):
                rows, cols = slice(c * CHUNK, (c + 1) * CHUNK), slice(g * CHUNK, (g + 1) * CHUNK)
                sv = _dot(ws_ref[g].astype(BF16), vn[rows, cols]) + bs_ref[g]
                u = _gelu(a_ref[pl.ds(c * CHUNK, CHUNK), pl.ds(g * CHUNK, CHUNK)])
                m_ref[pl.ds(c * CHUNK, CHUNK), pl.ds(g * CHUNK, CHUNK)] = (u * sv).astype(BF16)

    return pl.pallas_call(
        body, name=name, grid=(T // TM,),
        in_specs=[pl.BlockSpec((TM, 2 * W), lambda i: (i, 0)), pl.BlockSpec((1, W), lambda i: (0, 0)),
                  pl.BlockSpec((NG, CHUNK, CHUNK), lambda i: (0, 0, 0)),
                  pl.BlockSpec((NG, CHUNK, 1), lambda i: (0, 0, 0))],
        out_specs=pl.BlockSpec((TM, W), lambda i: (i, 0)),
        out_shape=jax.ShapeDtypeStruct((T, W), BF16),
        compiler_params=_params(("parallel",)),
    )(a, vg, ws, bs)


def _cm_bwd(cfg, name, a, vg, ws, bs, dm):
    T, TM, W, NG = cfg.T, cfg.TM, cfg.CMW, cfg.CMG

    def body(a_ref, vg_ref, ws_ref, bs_ref, dm_ref, da_ref, dws_ref, dbs_ref, dvg_ref, dvn_ref):
        i = pl.program_id(0)

        @pl.when(i == 0)
        def _():
            dws_ref[...] = jnp.zeros_like(dws_ref)
            dbs_ref[...] = jnp.zeros_like(dbs_ref)
            dvg_ref[...] = jnp.zeros_like(dvg_ref)

        av = a_ref[:, pl.ds(W, W)]
        v = _gelu(av)
        rstd = lax.rsqrt(jnp.mean(v * v, axis=-1, keepdims=True) + EPS)
        xhat = v * rstd
        vg = vg_ref[...]
        vn = (xhat * vg).astype(BF16)
        for c in range(TM // CHUNK):
            for g in range(NG):
                rows, cols = slice(c * CHUNK, (c + 1) * CHUNK), slice(g * CHUNK, (g + 1) * CHUNK)
                rs, cs = pl.ds(c * CHUNK, CHUNK), pl.ds(g * CHUNK, CHUNK)
                wsb = ws_ref[g].astype(BF16)
                blk = vn[rows, cols]
                sv = _dot(wsb, blk) + bs_ref[g]
                au = a_ref[rs, cs]
                dmb = dm_ref[rs, cs]
                da_ref[rs, cs] = (dmb * sv * _gelu_grad(au)).astype(BF16)
                dsv = dmb * _gelu(au)
                dsvb = dsv.astype(BF16)
                dbs_ref[g] += jnp.sum(dsv, axis=1, keepdims=True)
                dws_ref[g] += _dot_nt(dsvb, blk)
                dvn_ref[rs, cs] = _dot_tn(wsb, dsvb)
        dvn = dvn_ref[...]
        dvg_ref[...] += jnp.sum(dvn * xhat, axis=0, keepdims=True)
        dxh = dvn * vg
        dv = rstd * (dxh - xhat * jnp.mean(dxh * xhat, axis=-1, keepdims=True))
        da_ref[:, pl.ds(W, W)] = (dv * _gelu_grad(av)).astype(BF16)

    return pl.pallas_call(
        body, name=name, grid=(T // TM,),
        in_specs=[pl.BlockSpec((TM, 2 * W), lambda i: (i, 0)), pl.BlockSpec((1, W), lambda i: (0, 0)),
                  pl.BlockSpec((NG, CHUNK, CHUNK), lambda i: (0, 0, 0)),
                  pl.BlockSpec((NG, CHUNK, 1), lambda i: (0, 0, 0)), pl.BlockSpec((TM, W), lambda i: (i, 0))],
        out_specs=[pl.BlockSpec((TM, 2 * W), lambda i: (i, 0)), pl.BlockSpec((NG, CHUNK, CHUNK), lambda i: (0, 0, 0)),
                   pl.BlockSpec((NG, CHUNK, 1), lambda i: (0, 0, 0)), pl.BlockSpec((1, W), lambda i: (0, 0))],
        out_shape=[jax.ShapeDtypeStruct((T, 2 * W), BF16), jax.ShapeDtypeStruct((NG, CHUNK, CHUNK), F32),
                   jax.ShapeDtypeStruct((NG, CHUNK, 1), F32), jax.ShapeDtypeStruct((1, W), F32)],
        scratch_shapes=[pltpu.VMEM((TM, W), F32)],
        compiler_params=_params(("arbitrary",)),
    )(a, vg, ws, bs, dm)


def _local_step(cfg, xcat, tgt, mods, w):
    D, TM, H = cfg.D, cfg.TM, cfg.H
    cosf, sinf = _rope_tables(cfg)
    TG = 3 * TM if cfg.TPE % 3 == 0 else TM
    tiles_per_ex = cfg.S // TG
    gate_spec = pl.BlockSpec((None, 2, 6, D), lambda i, j, k: (i // tiles_per_ex, 0, 0, 0))

    def resid_epi(igate):
        def epi(acc, row_tile, x_ref, mod_ref):
            row = lax.broadcasted_iota(jnp.int32, (TG, 1), 0)
            is_ctx = jnp.logical_and(row_tile % tiles_per_ex == 0, row < cfg.SC)
            gate = jnp.where(is_ctx, mod_ref[0, pl.ds(igate, 1), :], mod_ref[1, pl.ds(igate, 1), :])
            return x_ref[...] + gate * acc, acc
        return epi

    def gated_out(name, a, wname, l_idx, x, mod, igate):
        return _mm(name, a, w[wname], mode="nn", layer=l_idx, tm=TG, tn=D, outs=[F32, BF16], epi=resid_epi(igate),
                   extras=[(x, pl.BlockSpec((TG, D), lambda i, j, k: (i, j))), (mod, gate_spec)])

    saved = []
    x = xcat
    for l in range(N_LAYERS):
        li = l // 2
        mod = mods[l]
        s = {"x0": x}
        s["h"] = _norm_mod_fwd(cfg, f"norm1_fwd_{l}", x, w["norm1_g"][l][None], mod, 0, 1)
        if l % 2 == 0:
            lgb = jnp.broadcast_to(jax.nn.log_sigmoid(w["ret_decay"][li]).T[:, :, None, None], (H, 2, 8, LANES))
            qg, kg = w["att_q_norm_g"][li][None], w["att_k_norm_g"][li][None]
            s["p"] = _mm(f"ab_in_{l}", s["h"], w["ab_w_in"], mode="nn", layer=li, outs=[F32], tn=768)
            s["rq"], s["rk"], s["aq"], s["ak"] = _prep_fwd(cfg, f"prep_fwd_{l}", s["p"], cosf, sinf, qg, kg)
            s["o"], ret = _retention_fwd(cfg, f"ret_fwd_{l}", s["rq"], s["rk"], s["p"], lgb)
            att = _attention_fwd(cfg, f"att_fwd_{l}", s["aq"], s["ak"], s["p"])
            s["cat"] = jnp.concatenate([ret, att], axis=-1)
            s["lgb"], s["qg"], s["kg"] = lgb, qg, kg
            x, s["y1"] = gated_out(f"ab_out_{l}", s["cat"], "ab_w_out", li, x, mod, 2)
        else:
            s["a"] = _mm(f"cm_in_{l}", s["h"], w["cm_w_in"], mode="nn", layer=li, outs=[F32])
            s["vg"], s["ws"], s["bs"] = w["cm_v_norm_g"][li][None], w["cm_w_s"][li], w["cm_b_s"][li][:, :, None]
            s["m"] = _cm_fwd(cfg, f"cm_fwd_{l}", s["a"], s["vg"], s["ws"], s["bs"])
            x, s["y1"] = gated_out(f"cm_out_{l}", s["m"], "cm_w_out", li, x, mod, 2)
        s["x1"] = x
        s["h2"] = _norm_mod_fwd(cfg, f"norm2_fwd_{l}", x, w["norm2_g"][l][None], mod, 3, 4)
        s["r"] = _mm(f"ff1_{l}", s["h2"], w["ff_w1"], mode="nn", layer=l, outs=[BF16],
                     epi=lambda acc, row_tile: (jnp.square(jnp.maximum(acc, 0.0)),))
        x, s["y2"] = gated_out(f"ff2_{l}", s["r"], "ff_w2", l, x, mod, 5)
        saved.append(s)

    loss, dx = _loss_grad(cfg, x, tgt)

    big = {k: [None] * (N_LAYERS if k.startswith("ff") else 2) for k in
           ("ab_w_in", "ab_w_out", "cm_w_in", "cm_w_out", "ff_w1", "ff_w2")}
    small = {k: [None] * n for k, n in (("norm1_g", 4), ("norm2_g", 4), ("ret_lg", 2), ("att_q_norm_g", 2),
                                        ("att_k_norm_g", 2), ("cm_v_norm_g", 2), ("cm_w_s", 2), ("cm_b_s", 2))}
    dmods = [None] * N_LAYERS

    def wgrad(name, a, d):
        return _mm(name, a, d, mode="tn", outs=[BF16])

    for l in reversed(range(N_LAYERS)):
        li = l // 2
        s, mod = saved[l], mods[l]
        dy2, dg2 = _gate_bwd(cfg, f"gate2_bwd_{l}", dx, s["y2"], mod, 5)
        da2 = _mm(f"ff2_dx_{l}", dy2, w["ff_w2"], mode="nt", layer=l, outs=[BF16],
                  epi=lambda acc, row_tile, r_ref: (acc * (2.0 * jnp.sqrt(r_ref[...].astype(F32))),),
                  extras=[(s["r"], pl.BlockSpec((_tile(cfg.T, 1024), _tile(cfg.FF, 1024)), lambda i, j, k: (i, j)))])
        big["ff_w2"][l] = wgrad(f"ff2_dw_{l}", s["r"], dy2)
        big["ff_w1"][l] = wgrad(f"ff1_dw_{l}", s["h2"], da2)
        dh2 = _mm(f"ff1_dx_{l}", da2, w["ff_w1"], mode="nt", layer=l, outs=[F32])
        dx, dm2, small["norm2_g"][l] = _norm_mod_bwd(cfg, f"norm2_bwd_{l}", s["x1"], w["norm2_g"][l][None], mod, 3, 4,
                                                     dh2, dx)
        do, dg1 = _gate_bwd(cfg, f"gate1_bwd_{l}", dx, s["y1"], mod, 2)
        if l % 2 == 0:
            big["ab_w_out"][li] = wgrad(f"ab_out_dw_{l}", s["cat"], do)
            dcat = _mm(f"ab_out_dx_{l}", do, w["ab_w_out"], mode="nt", layer=li, outs=[F32])
            d_rq, d_rk, d_rv, d_gt, dlg = _retention_bwd(cfg, f"ret_bwd_{l}", s["rq"], s["rk"], s["p"], s["o"], dcat,
                                                         s["lgb"])
            d_aq, d_ak, d_av = _attention_bwd(cfg, f"att_bwd_{l}", s["aq"], s["ak"], s["p"], dcat)
            dp, dqg, dkg = _prep_bwd(cfg, f"prep_bwd_{l}", s["p"], cosf, sinf, s["qg"], s["kg"],
                                     d_rq, d_rk, d_rv, d_gt, d_aq, d_ak, d_av)
            small["ret_lg"][li] = jnp.sum(dlg[:, :, :, 0, 0], axis=0).T
            small["att_q_norm_g"][li], small["att_k_norm_g"][li] = dqg[0], dkg[0]
            big["ab_w_in"][li] = wgrad(f"ab_in_dw_{l}", s["h"], dp)
            dh = _mm(f"ab_in_dx_{l}", dp, w["ab_w_in"], mode="nt", layer=li, outs=[F32], tk=768)
        else:
            big["cm_w_out"][li] = wgrad(f"cm_out_dw_{l}", s["m"], do)
            dm = _mm(f"cm_out_dx_{l}", do, w["cm_w_out"], mode="nt", layer=li, outs=[F32])
            da, dws, dbs, dvg = _cm_bwd(cfg, f"cm_bwd_{l}", s["a"], s["vg"], s["ws"], s["bs"], dm)
            small["cm_w_s"][li], small["cm_b_s"][li], small["cm_v_norm_g"][li] = dws, dbs[:, :, 0], dvg[0]
            big["cm_w_in"][li] = wgrad(f"cm_in_dw_{l}", s["h"], da)
            dh = _mm(f"cm_in_dx_{l}", da, w["cm_w_in"], mode="nt", layer=li, outs=[F32])
        dx, dm1, small["norm1_g"][l] = _norm_mod_bwd(cfg, f"norm1_bwd_{l}", s["x0"], w["norm1_g"][l][None], mod, 0, 1,
                                                     dh, dx)
        dmods[l] = jnp.concatenate([dm1, dg1, dm2, dg2], axis=2)
    return loss, dx, big, small, dmods


N_DEV = 8
N_CHIP = 4
MESH = pl.DeviceIdType.MESH
ANY = pl.BlockSpec(memory_space=pl.ANY)
BIG = (("ab_w_in", 2), ("ab_w_out", 1), ("cm_w_in", 2), ("cm_w_out", 1), ("ff_w1", 2), ("ff_w2", 1))


def _place():
    x, y, c = lax.axis_index("x"), lax.axis_index("y"), lax.axis_index("c")
    return x, y, c, [(1 - x, y), (x, 1 - y), (1 - x, 1 - y)]


def _shard_of(ref, axis, s, width):
    start = pl.multiple_of(s * width, LANES)
    if axis == 1:
        return ref.at[:, pl.ds(start, width), :]
    return ref.at[:, :, pl.ds(start, width)]


def _allgather8(name, block):
    m_per, n = block.shape

    def body(x_ref, out_ref, send_sems, recv_sems, local_sem):
        x, y, c, chips = _place()
        me, sibling = (x, y, c), (x, y, 1 - c)

        def rows(px, py, pc):
            return out_ref.at[pl.ds((4 * px + 2 * py + pc) * m_per, m_per), :]

        def copy(k, blk, to, src=None):
            return pltpu.make_async_remote_copy(
                src_ref=rows(*blk) if src is None else src, dst_ref=rows(*blk),
                send_sem=send_sems.at[k], recv_sem=recv_sems.at[k], device_id=to, device_id_type=MESH)

        mine = pltpu.make_async_copy(x_ref, rows(*me), local_sem)
        mine.start()
        first = [copy(0, me, sibling, src=x_ref)]
        first += [copy(1 + j, me, (*chip, c), src=x_ref) for j, chip in enumerate(chips)]
        for cp in first:
            cp.start()
        passed = [copy(4 + j, (*chip, c), sibling) for j, chip in enumerate(chips)]
        for j, chip in enumerate(chips):
            copy(1 + j, (*chip, c), me).wait_recv()
            passed[j].start()
        copy(0, sibling, me).wait_recv()
        for j, chip in enumerate(chips):
            copy(4 + j, (*chip, 1 - c), me).wait_recv()
        for cp in first + passed:
            cp.wait_send()
        mine.wait()

    return pl.pallas_call(
        body, name=name, out_shape=jax.ShapeDtypeStruct((N_DEV * m_per, n), block.dtype),
        in_specs=[pl.BlockSpec(memory_space=pltpu.VMEM)], out_specs=pl.BlockSpec(memory_space=pltpu.VMEM),
        scratch_shapes=[pltpu.SemaphoreType.DMA((7,)), pltpu.SemaphoreType.DMA((7,)), pltpu.SemaphoreType.DMA],
        compiler_params=pltpu.CompilerParams(vmem_limit_bytes=VMEM_LIMIT),
    )(block)


def _gather_weights(shards):
    n_t = len(BIG)
    fulls = []
    for (_, axis), sh in zip(BIG, shards):
        shape = list(sh.shape)
        shape[axis] *= N_CHIP
        fulls.append(jax.ShapeDtypeStruct(tuple(shape), sh.dtype))

    def body(*refs):
        srcs, outs = refs[:n_t], refs[n_t:2 * n_t]
        send_sems, recv_sems, local_sems = refs[2 * n_t:]
        x, y, c, chips = _place()
        me = 2 * x + y
        copies = []
        for t, (_, axis) in enumerate(BIG):
            width = srcs[t].shape[axis]
            own = _shard_of(outs[t], axis, me, width)
            loc = pltpu.make_async_copy(srcs[t], own, local_sems.at[t])
            loc.start()
            copies.append(loc)
            for j, (px, py) in enumerate(chips):
                cp = pltpu.make_async_remote_copy(
                    src_ref=srcs[t], dst_ref=own, send_sem=send_sems.at[3 * t + j], recv_sem=recv_sems.at[3 * t + j],
                    device_id=(px, py, c), device_id_type=MESH)
                cp.start()
                copies.append(cp)
        for cp in copies:
            cp.wait()

    return pl.pallas_call(
        body, name="gather_weights", out_shape=fulls, in_specs=[ANY] * n_t, out_specs=[ANY] * n_t,
        scratch_shapes=[pltpu.SemaphoreType.DMA((3 * n_t,)), pltpu.SemaphoreType.DMA((3 * n_t,)),
                        pltpu.SemaphoreType.DMA((n_t,))],
    )(*shards)


def _exchange_grads(grads):
    n_t = len(BIG)
    outs_shape = []
    for (_, axis), g in zip(BIG, grads):
        shape = list(g.shape)
        shape[axis] //= N_CHIP
        outs_shape.append(jax.ShapeDtypeStruct((N_CHIP,) + tuple(shape), g.dtype))

    def body(*refs):
        srcs, outs = refs[:n_t], refs[n_t:2 * n_t]
        send_sems, recv_sems, local_sems = refs[2 * n_t:]
        x, y, c, chips = _place()
        me = 2 * x + y
        copies = []
        for t, (_, axis) in enumerate(BIG):
            width = outs[t].shape[1 + axis]
            loc = pltpu.make_async_copy(_shard_of(srcs[t], axis, me, width), outs[t].at[3], local_sems.at[t])
            loc.start()
            copies.append(loc)
            for j, (px, py) in enumerate(chips):
                cp = pltpu.make_async_remote_copy(
                    src_ref=_shard_of(srcs[t], axis, 2 * px + py, width), dst_ref=outs[t].at[j],
                    send_sem=send_sems.at[3 * t + j], recv_sem=recv_sems.at[3 * t + j],
                    device_id=(px, py, c), device_id_type=MESH)
                cp.start()
                copies.append(cp)
        for cp in copies:
            cp.wait()

    return pl.pallas_call(
        body, name="exchange_grads", out_shape=outs_shape, in_specs=[ANY] * n_t, out_specs=[ANY] * n_t,
        scratch_shapes=[pltpu.SemaphoreType.DMA((3 * n_t,)), pltpu.SemaphoreType.DMA((3 * n_t,)),
                        pltpu.SemaphoreType.DMA((n_t,))],
    )(*grads)


def _swap_sibling(parts):
    n_t = len(parts)

    def body(*refs):
        srcs, outs = refs[:n_t], refs[n_t:2 * n_t]
        send_sems, recv_sems = refs[2 * n_t:]
        x, y, c, _ = _place()
        copies = []
        for t in range(n_t):
            cp = pltpu.make_async_remote_copy(
                src_ref=srcs[t], dst_ref=outs[t], send_sem=send_sems.at[t], recv_sem=recv_sems.at[t],
                device_id=(x, y, 1 - c), device_id_type=MESH)
            cp.start()
            copies.append(cp)
        for cp in copies:
            cp.wait()

    return pl.pallas_call(
        body, name="swap_sibling", out_shape=[jax.ShapeDtypeStruct(p.shape, p.dtype) for p in parts],
        in_specs=[ANY] * n_t, out_specs=[ANY] * n_t,
        scratch_shapes=[pltpu.SemaphoreType.DMA((n_t,)), pltpu.SemaphoreType.DMA((n_t,))],
    )(*parts)


def _rows_view(a):
    if a.ndim == 1:
        return a.reshape(1, a.shape[0])
    return a.reshape(-1, a.shape[-1])


def _row_tile(rows, cols, target_elems=1 << 17):
    tr = rows
    while tr % 16 == 0 and tr * cols > target_elems:
        tr //= 2
    return tr


def _sum_leading(name, a):
    n, rows, cols = a.shape
    tr = _row_tile(rows, cols * n)

    def body(a_ref, o_ref):
        acc = a_ref[0].astype(F32)
        for i in range(1, n):
            acc = acc + a_ref[i].astype(F32)
        o_ref[...] = acc

    return pl.pallas_call(
        body, name=name, grid=(rows // tr,),
        in_specs=[pl.BlockSpec((n, tr, cols), lambda i: (0, i, 0))],
        out_specs=pl.BlockSpec((tr, cols), lambda i: (i, 0)),
        out_shape=jax.ShapeDtypeStruct((rows, cols), F32),
        compiler_params=_params(("parallel",)),
    )(a)


def _silu_rows(name, x):
    def body(x_ref, o_ref):
        v = x_ref[...]
        o_ref[...] = v * jax.nn.sigmoid(v)

    return pl.pallas_call(body, name=name, out_shape=jax.ShapeDtypeStruct(x.shape, F32))(x)


def _silu_bwd_rows(name, x, dy):
    def body(x_ref, dy_ref, o_ref):
        v = x_ref[...]
        sg = jax.nn.sigmoid(v)
        o_ref[...] = dy_ref[...] * (sg * (1.0 + v * (1.0 - sg)))

    return pl.pallas_call(body, name=name, out_shape=jax.ShapeDtypeStruct(x.shape, F32))(x, dy)


def _adamw(name, w, g_parts, m, v):
    shape = w.shape
    w2, m2, v2 = _rows_view(w), _rows_view(m), _rows_view(v)
    gs = [_rows_view(g) for g in g_parts]
    rows, cols = w2.shape
    tr = _row_tile(rows, cols)
    ng = len(gs)

    def body(*refs):
        w_ref, m_ref, v_ref = refs[0], refs[1], refs[2]
        g_refs = refs[3:3 + ng]
        g_out, d_out, m_out, v_out = refs[3 + ng:]
        g = g_refs[0][...]
        for r in g_refs[1:]:
            g = g + r[...]
        m1 = ADAM_B1 * m_ref[...] + (1.0 - ADAM_B1) * g
        v1 = ADAM_B2 * v_ref[...] + (1.0 - ADAM_B2) * jnp.square(g)
        m_hat = m1 / (1.0 - ADAM_B1 ** ADAM_STEP)
        v_hat = v1 / (1.0 - ADAM_B2 ** ADAM_STEP)
        g_out[...] = g
        d_out[...] = -ADAM_LR * (m_hat / (jnp.sqrt(v_hat) + ADAM_EPS) + ADAM_WD * w_ref[...])
        m_out[...] = m1
        v_out[...] = v1

    spec = pl.BlockSpec((tr, cols), lambda i: (i, 0))
    res = pl.pallas_call(
        body, name=name, grid=(rows // tr,), in_specs=[spec] * (3 + ng), out_specs=[spec] * 4,
        out_shape=[jax.ShapeDtypeStruct((rows, cols), F32)] * 4,
        compiler_params=_params(("parallel",)),
    )(w2, m2, v2, *gs)
    return tuple(r.reshape(shape) for r in res)


MOD_ROWS = 48


def kernel(x, c, ctx, c_ctx, mod_w, mod_b, norm1_g, norm2_g, ab_w_in, ab_w_out, ret_decay, att_q_norm_g, att_k_norm_g, cm_w_in, cm_v_norm_g, cm_w_s, cm_b_s, cm_w_out, ff_w1, ff_w2, loss_target, m_c_ctx, m_mod_w, m_mod_b, m_norm1_g, m_norm2_g, m_ab_w_in, m_ab_w_out, m_ret_decay, m_att_q_norm_g, m_att_k_norm_g, m_cm_w_in, m_cm_v_norm_g, m_cm_w_s, m_cm_b_s, m_cm_w_out, m_ff_w1, m_ff_w2, v_c_ctx, v_mod_w, v_mod_b, v_norm1_g, v_norm2_g, v_ab_w_in, v_ab_w_out, v_ret_decay, v_att_q_norm_g, v_att_k_norm_g, v_cm_w_in, v_cm_v_norm_g, v_cm_w_s, v_cm_b_s, v_cm_w_out, v_ff_w1, v_ff_w2):
    B, SL, D = x.shape
    cfg = Cfg(B=B, SC=ctx.shape[1], SL=SL, D=D, FF=ff_w1.shape[2] * N_CHIP)
    L = N_LAYERS
    n_ex = B * N_DEV
    mcols = mod_w.shape[2]
    weights = dict(c_ctx=c_ctx, mod_w=mod_w, mod_b=mod_b, norm1_g=norm1_g, norm2_g=norm2_g, ab_w_in=ab_w_in,
                   ab_w_out=ab_w_out, ret_decay=ret_decay, att_q_norm_g=att_q_norm_g, att_k_norm_g=att_k_norm_g,
                   cm_w_in=cm_w_in, cm_v_norm_g=cm_v_norm_g, cm_w_s=cm_w_s, cm_b_s=cm_b_s, cm_w_out=cm_w_out,
                   ff_w1=ff_w1, ff_w2=ff_w2)
    m_in = dict(c_ctx=m_c_ctx, mod_w=m_mod_w, mod_b=m_mod_b, norm1_g=m_norm1_g, norm2_g=m_norm2_g, ab_w_in=m_ab_w_in,
                ab_w_out=m_ab_w_out, ret_decay=m_ret_decay, att_q_norm_g=m_att_q_norm_g, att_k_norm_g=m_att_k_norm_g,
                cm_w_in=m_cm_w_in, cm_v_norm_g=m_cm_v_norm_g, cm_w_s=m_cm_w_s, cm_b_s=m_cm_b_s, cm_w_out=m_cm_w_out,
                ff_w1=m_ff_w1, ff_w2=m_ff_w2)
    v_in = dict(c_ctx=v_c_ctx, mod_w=v_mod_w, mod_b=v_mod_b, norm1_g=v_norm1_g, norm2_g=v_norm2_g, ab_w_in=v_ab_w_in,
                ab_w_out=v_ab_w_out, ret_decay=v_ret_decay, att_q_norm_g=v_att_q_norm_g, att_k_norm_g=v_att_k_norm_g,
                cm_w_in=v_cm_w_in, cm_v_norm_g=v_cm_v_norm_g, cm_w_s=v_cm_w_s, cm_b_s=v_cm_b_s, cm_w_out=v_cm_w_out,
                ff_w1=v_ff_w1, ff_w2=v_ff_w2)
    xi, yi, ci = lax.axis_index("x"), lax.axis_index("y"), lax.axis_index("c")
    chip = 2 * xi + yi
    dev = 2 * chip + ci

    full = dict(zip([n for n, _ in BIG], _gather_weights([weights[n].astype(BF16) for n, _ in BIG])))
    vgw = cm_v_norm_g.shape[1]
    blk = jnp.zeros((8, D), F32).at[:B].set(c).at[B:B + 2, :vgw].set(cm_v_norm_g)
    g0 = _allgather8("gather_c", blk).reshape(N_DEV, 8, D)
    c_all = g0[:, :B].reshape(n_ex, D)
    vg_full = jnp.concatenate([g0[2 * s, B:B + 2, :vgw] for s in range(N_CHIP)], axis=-1)

    pre = jnp.zeros((MOD_ROWS, D), F32).at[:n_ex].set(c_all).at[n_ex].set(c_ctx)
    act = _silu_rows("silu_c", pre)
    mpart = jnp.stack([_mm(f"mod_fwd_{l}", act, mod_w, mode="nn", layer=l, outs=[F32], tn=mcols) for l in range(L)])
    g1 = _allgather8("gather_mod", mpart.reshape(L * MOD_ROWS, mcols)).reshape(N_DEV, L, MOD_ROWS, mcols)
    mod_all = jnp.concatenate([g1[2 * s] for s in range(N_CHIP)], axis=-1) + mod_b[:, None, :]
    mod_lat = lax.dynamic_slice_in_dim(mod_all, dev * B, B, axis=1)
    mod_ctx = jnp.broadcast_to(mod_all[:, n_ex][:, None], mod_lat.shape)
    mods = jnp.stack([mod_ctx, mod_lat], axis=2).reshape(L, B, 2, 6, D)

    w = dict(full)
    w.update(norm1_g=norm1_g, norm2_g=norm2_g, ret_decay=ret_decay, att_q_norm_g=att_q_norm_g,
             att_k_norm_g=att_k_norm_g, cm_v_norm_g=vg_full, cm_w_s=cm_w_s, cm_b_s=cm_b_s)
    xcat = jnp.concatenate([ctx, x], axis=1).reshape(cfg.T, D)
    loss_local, dxcat, big, small, dmods = _local_step(cfg, xcat, loss_target.reshape(B * SL, D), mods, w)
    loss = lax.psum(loss_local, ("x", "y", "c"))
    grad_x = dxcat.reshape(B, cfg.S, D)[:, cfg.SC:, :]

    recv = _exchange_grads([jnp.stack(big[n]) for n, _ in BIG])
    part = [_sum_leading(f"sum_{n}", r.reshape(N_CHIP, -1, r.shape[-1])).reshape(r.shape[1:])
            for (n, _), r in zip(BIG, recv)]
    other = _swap_sibling(part)
    out = {}
    for (n, _), p_mine, p_other in zip(BIG, part, other):
        out[n] = _adamw(f"adamw_{n}", weights[n], [p_mine, p_other], m_in[n], v_in[n])

    dmod = jnp.stack(dmods).reshape(L, B, 2, 6 * D)
    dmod_lat = dmod[:, :, 1]
    dmod_ctx = jnp.sum(dmod[:, :, 0], axis=1)
    d_ret = jnp.stack(small["ret_lg"]) * jax.nn.sigmoid(-ret_decay)
    summed = [dmod_ctx.reshape(-1), jnp.stack(small["norm1_g"]).reshape(-1), jnp.stack(small["norm2_g"]).reshape(-1),
              jnp.stack(small["cm_v_norm_g"]).reshape(-1), jnp.stack(small["cm_w_s"]).reshape(-1),
              jnp.stack(small["cm_b_s"]).reshape(-1), jnp.stack(small["att_q_norm_g"]).reshape(-1),
              jnp.stack(small["att_k_norm_g"]).reshape(-1), d_ret.reshape(-1)]
    sizes = [int(a.shape[0]) for a in summed]
    flat = jnp.concatenate(summed + [dmod_lat.reshape(-1)])
    n_sum = sum(sizes)
    n_sum_rows = -(-n_sum // D)
    lat_rows = (L * B * 6 * D) // D
    pack_rows = -(-(n_sum_rows + lat_rows) // 8) * 8
    packed = jnp.zeros((pack_rows * D,), F32).at[:n_sum].set(flat[:n_sum])
    packed = packed.at[n_sum_rows * D:(n_sum_rows + lat_rows) * D].set(flat[n_sum:]).reshape(pack_rows, D)
    g2 = _allgather8("gather_small", packed).reshape(N_DEV, pack_rows, D)
    tot = _sum_leading("sum_small", g2[:, :n_sum_rows]).reshape(-1)
    pieces, off = [], 0
    for sz in sizes:
        pieces.append(tot[off:off + sz])
        off += sz
    dmod_ctx_t, g_n1, g_n2, g_vg, g_ws, g_bs, g_qg, g_kg, g_rd = pieces
    dmod_ctx_t = dmod_ctx_t.reshape(L, 6 * D)
    dmod_lat_all = g2[:, n_sum_rows:n_sum_rows + lat_rows].reshape(N_DEV, L, B, 6 * D)
    dmod_rows = jnp.zeros((L, MOD_ROWS, 6 * D), F32)
    dmod_rows = dmod_rows.at[:, :n_ex].set(jnp.transpose(dmod_lat_all, (1, 0, 2, 3)).reshape(L, n_ex, 6 * D))
    dmod_rows = dmod_rows.at[:, n_ex].set(dmod_ctx_t)
    g_mod_b = _sum_leading("sum_mod_b", jnp.transpose(dmod_rows, (1, 0, 2)))
    dmod_mine = lax.dynamic_slice_in_dim(dmod_rows, chip * mcols, mcols, axis=2)
    g_mod_w = jnp.stack([_mm(f"mod_dw_{l}", act, dmod_mine[l], mode="tn", outs=[F32], tn=mcols) for l in range(L)])
    ctx8 = jnp.zeros((L, 8, mcols), F32).at[:, 0].set(dmod_mine[:, n_ex])
    dcc = [_mm(f"mod_dctx_{l}", ctx8[l], mod_w, mode="nt", layer=l, outs=[F32], tk=mcols) for l in range(L)]
    dcc = _sum_leading("sum_dctx_layers", jnp.stack(dcc))
    g3 = _allgather8("gather_dctx", dcc).reshape(N_DEV, 8, D)
    dcc_t = _sum_leading("sum_dctx_chips", g3[0::2])[0:1]
    g_c_ctx = _silu_bwd_rows("silu_bwd_cctx", c_ctx[None], dcc_t)[0]

    vg_mine = lax.dynamic_slice_in_dim(g_vg.reshape(2, -1), chip * vgw, vgw, axis=1)
    small_g = dict(c_ctx=g_c_ctx, mod_w=g_mod_w, mod_b=g_mod_b, norm1_g=g_n1.reshape(norm1_g.shape),
                   norm2_g=g_n2.reshape(norm2_g.shape), ret_decay=g_rd.reshape(ret_decay.shape),
                   att_q_norm_g=g_qg.reshape(att_q_norm_g.shape), att_k_norm_g=g_kg.reshape(att_k_norm_g.shape),
                   cm_v_norm_g=vg_mine, cm_w_s=g_ws.reshape(cm_w_s.shape), cm_b_s=g_bs.reshape(cm_b_s.shape))
    for n, g in small_g.items():
        out[n] = _adamw(f"adamw_{n}", weights[n], [g], m_in[n], v_in[n])

    order = list(weights)
    return (loss, grad_x, *[out[n][0] for n in order], *[out[n][1] for n in order],
            *[out[n][2] for n in order], *[out[n][3] for n in order])
```

```python
import functools
import math
from typing import NamedTuple

import jax
import jax.numpy as jnp
from jax import lax
from jax.experimental import pallas as pl
from jax.experimental.pallas import tpu as pltpu

F32 = jnp.float32
BF16 = jnp.bfloat16
EPS = 1e-6
ROPE_BASE = 10000.0
LANES = 128
CHUNK = 128
N_LAYERS = 4
VMEM_LIMIT = 56 * 1024 * 1024

ADAM_LR = 0.001
ADAM_B1 = 0.9
ADAM_B2 = 0.999
ADAM_EPS = 1e-08
ADAM_WD = 0.01
ADAM_STEP = 10


class Cfg(NamedTuple):
    B: int = 4
    SC: int = 256
    SL: int = 2048
    D: int = 1024
    FF: int = 4096
    GRID_W: int = 64
    H: int = 4
    KV: int = 2
    CMW: int = 1024
    CMG: int = 8

    @property
    def S(self):
        return self.SC + self.SL

    @property
    def T(self):
        return self.B * self.S

    @property
    def TM(self):
        return self.SC

    @property
    def TPE(self):
        return self.S // self.SC

    @property
    def ABW(self):
        return (5 * self.H + 2 * self.KV) * CHUNK


def _tile(dim, pref):
    t = min(dim, pref)
    while dim % t:
        t -= LANES
    return t


def _dot(a, b):
    return lax.dot_general(a, b, (((1,), (0,)), ((), ())), preferred_element_type=F32)


def _dot_nt(a, b):
    return lax.dot_general(a, b, (((1,), (1,)), ((), ())), preferred_element_type=F32)


def _dot_tn(a, b):
    return lax.dot_general(a, b, (((0,), (0,)), ((), ())), preferred_element_type=F32)


def _params(sem, vmem=VMEM_LIMIT):
    return pltpu.CompilerParams(dimension_semantics=sem, vmem_limit_bytes=vmem)


def _mod_index(cfg):
    tpe = cfg.TPE
    return lambda i: (i // tpe, jnp.minimum(i % tpe, 1), 0, 0)


def _mm(name, a, b, *, mode, outs, tm=1024, tn=1024, tk=1024, layer=None, epi=None, extras=(), carry=None):
    bshape = b.shape[1:] if layer is not None else b.shape
    if mode == "nn":
        (M, K), N = a.shape, bshape[1]
    elif mode == "nt":
        (M, K), N = a.shape, bshape[0]
    else:
        (K, M), N = a.shape, bshape[1]
    tm, tn, tk = _tile(M, tm), _tile(N, tn), _tile(K, tk)
    nk = K // tk
    a_spec = (pl.BlockSpec((tk, tm), lambda i, j, k: (k, i)) if mode == "tn"
              else pl.BlockSpec((tm, tk), lambda i, j, k: (i, k)))
    if mode == "nt":
        bblk, bidx = (tn, tk), (lambda i, j, k: (j, k))
    else:
        bblk, bidx = (tk, tn), (lambda i, j, k: (k, j))
    if layer is not None:
        b_spec = pl.BlockSpec((None,) + bblk, lambda i, j, k: (layer,) + bidx(i, j, k))
    else:
        b_spec = pl.BlockSpec(bblk, bidx)
    ne, no = len(extras), len(outs)
    nc = len(carry.srcs) if carry is not None else 0
    dot = {"nn": _dot, "nt": _dot_nt, "tn": _dot_tn}[mode]
    grid = (M // tm, N // tn, nk)

    def body(*refs):
        a_ref, b_ref = refs[0], refs[1]
        ex, out_refs = refs[2:2 + ne], refs[2 + ne + nc:2 + ne + nc + no]
        row_tile = pl.program_id(0)

        if nc:
            step = (pl.program_id(0) * grid[1] + pl.program_id(1)) * grid[2] + pl.program_id(2)
            c_src = refs[2 + ne:2 + ne + nc]
            c_dst = refs[2 + ne + nc + no:2 + ne + 2 * nc + no]
            sems = refs[2 + ne + 2 * nc + no:2 + ne + 2 * nc + no + 3]

            @pl.when(step == 0)
            def _():
                for cp in _carry_copies(carry, c_src, c_dst, *sems):
                    cp.start()

        def finish(acc):
            res = epi(acc, row_tile, *ex) if epi is not None else (acc,)
            for r, o in zip(res, out_refs):
                o[...] = r.astype(o.dtype)

        part = dot(a_ref[...].astype(BF16), b_ref[...].astype(BF16))
        if nk == 1:
            finish(part)
        else:
            acc_ref = refs[-1]
            k = pl.program_id(2)

            @pl.when(k == 0)
            def _():
                acc_ref[...] = part

            @pl.when(k > 0)
            def _():
                acc_ref[...] += part

            @pl.when(k == nk - 1)
            def _():
                finish(acc_ref[...])

        if nc:
            @pl.when(step == grid[0] * grid[1] * grid[2] - 1)
            def _():
                for cp in _carry_copies(carry, c_src, c_dst, *sems):
                    cp.wait()

    scratch = [pltpu.SemaphoreType.DMA((3 * nc,)), pltpu.SemaphoreType.DMA((3 * nc,)),
               pltpu.SemaphoreType.DMA((nc,))] if nc else []
    if nk > 1:
        scratch.append(pltpu.VMEM((tm, tn), F32))
    res = pl.pallas_call(
        body, name=name, grid=grid,
        in_specs=[a_spec, b_spec] + [s for _, s in extras] + [ANY] * nc,
        out_specs=[pl.BlockSpec((tm, tn), lambda i, j, k: (i, j)) for _ in outs] + [ANY] * nc,
        out_shape=[jax.ShapeDtypeStruct((M, N), d) for d in outs] + (_carry_out_shapes(carry) if nc else []),
        scratch_shapes=scratch,
        compiler_params=_params(("arbitrary",) * 3 if nc else ("parallel", "parallel", "arbitrary")),
    )(a, b, *[x for x, _ in extras], *(carry.srcs if nc else ()))
    if nc:
        return (res[0] if no == 1 else res[:no]), res[no:]
    return res[0] if no == 1 else res


def _norm_mod_fwd(cfg, name, x, gain, mod, ish, isc):
    T, D, TM = cfg.T, cfg.D, cfg.TM

    def body(x_ref, g_ref, mod_ref, h_ref):
        x = x_ref[...]
        rstd = lax.rsqrt(jnp.mean(x * x, axis=-1, keepdims=True) + EPS)
        n = x * rstd * g_ref[...]
        h = n * (1.0 + mod_ref[pl.ds(isc, 1), :]) + mod_ref[pl.ds(ish, 1), :]
        h_ref[...] = h.astype(BF16)

    return pl.pallas_call(
        body, name=name, grid=(T // TM,),
        in_specs=[pl.BlockSpec((TM, D), lambda i: (i, 0)), pl.BlockSpec((1, D), lambda i: (0, 0)),
                  pl.BlockSpec((None, None, 6, D), _mod_index(cfg))],
        out_specs=pl.BlockSpec((TM, D), lambda i: (i, 0)),
        out_shape=jax.ShapeDtypeStruct((T, D), BF16),
        compiler_params=_params(("parallel",)),
    )(x, gain, mod)


def _norm_mod_bwd(cfg, name, x, gain, mod, ish, isc, dh, dres):
    T, D, TM, TPE = cfg.T, cfg.D, cfg.TM, cfg.TPE

    def body(x_ref, g_ref, mod_ref, dh_ref, dres_ref, dx_ref, dmod_ref, dgain_ref):
        i = pl.program_id(0)
        t = i % TPE
        x = x_ref[...]
        g = g_ref[...]
        dh = dh_ref[...].astype(F32)
        rstd = lax.rsqrt(jnp.mean(x * x, axis=-1, keepdims=True) + EPS)
        xhat = x * rstd
        dn = dh * (1.0 + mod_ref[pl.ds(isc, 1), :])
        dsh = jnp.sum(dh, axis=0, keepdims=True)
        dsc = jnp.sum(dh * (xhat * g), axis=0, keepdims=True)
        dgain = jnp.sum(dn * xhat, axis=0, keepdims=True)
        dxh = dn * g
        dx = rstd * (dxh - xhat * jnp.mean(dxh * xhat, axis=-1, keepdims=True))
        dx_ref[...] = dx + dres_ref[...]

        @pl.when(t <= 1)
        def _():
            dmod_ref[pl.ds(0, 1), :] = dsh
            dmod_ref[pl.ds(1, 1), :] = dsc

        @pl.when(t > 1)
        def _():
            dmod_ref[pl.ds(0, 1), :] += dsh
            dmod_ref[pl.ds(1, 1), :] += dsc

        @pl.when(i == 0)
        def _():
            dgain_ref[...] = dgain

        @pl.when(i > 0)
        def _():
            dgain_ref[...] += dgain

    tok = pl.BlockSpec((TM, D), lambda i: (i, 0))
    return pl.pallas_call(
        body, name=name, grid=(T // TM,),
        in_specs=[tok, pl.BlockSpec((1, D), lambda i: (0, 0)), pl.BlockSpec((None, None, 6, D), _mod_index(cfg)),
                  tok, tok],
        out_specs=[tok, pl.BlockSpec((None, None, 2, D), _mod_index(cfg)), pl.BlockSpec((1, D), lambda i: (0, 0))],
        out_shape=[jax.ShapeDtypeStruct((T, D), F32), jax.ShapeDtypeStruct((cfg.B, 2, 2, D), F32),
                   jax.ShapeDtypeStruct((1, D), F32)],
        compiler_params=_params(("arbitrary",)),
    )(x, gain, mod, dh, dres)


def _gate_bwd(cfg, name, dx, y, mod, igate):
    T, D, TM, TPE = cfg.T, cfg.D, cfg.TM, cfg.TPE

    def body(dx_ref, y_ref, mod_ref, dy_ref, dg_ref):
        t = pl.program_id(0) % TPE
        dx = dx_ref[...]
        dy_ref[...] = (dx * mod_ref[pl.ds(igate, 1), :]).astype(BF16)
        dg = jnp.sum(dx * y_ref[...].astype(F32), axis=0, keepdims=True)

        @pl.when(t <= 1)
        def _():
            dg_ref[...] = dg

        @pl.when(t > 1)
        def _():
            dg_ref[...] += dg

    tok = pl.BlockSpec((TM, D), lambda i: (i, 0))
    return pl.pallas_call(
        body, name=name, grid=(T // TM,),
        in_specs=[tok, tok, pl.BlockSpec((None, None, 6, D), _mod_index(cfg))],
        out_specs=[tok, pl.BlockSpec((None, None, 1, D), _mod_index(cfg))],
        out_shape=[jax.ShapeDtypeStruct((T, D), BF16), jax.ShapeDtypeStruct((cfg.B, 2, 1, D), F32)],
        compiler_params=_params(("arbitrary",)),
    )(dx, y, mod)


def _loss_grad(cfg, x, tgt):
    T, D, TM, TPE = cfg.T, cfg.D, cfg.TM, cfg.TPE

    def body(x_ref, t_ref, dx_ref, loss_ref):
        i = pl.program_id(0)
        t = i % TPE

        @pl.when(i == 0)
        def _():
            loss_ref[...] = jnp.zeros_like(loss_ref)

        @pl.when(t == 0)
        def _():
            dx_ref[...] = jnp.zeros_like(dx_ref)

        @pl.when(t > 0)
        def _():
            err = x_ref[...] - t_ref[...]
            dx_ref[...] = err * (1.0 / D)
            loss_ref[...] += 0.5 * jnp.sum(jnp.mean(err * err, axis=-1, keepdims=True), axis=0, keepdims=True)

    tok = pl.BlockSpec((TM, D), lambda i: (i, 0))
    tgt_spec = pl.BlockSpec((TM, D), lambda i: ((i // TPE) * (TPE - 1) + jnp.maximum(i % TPE - 1, 0), 0))
    dx, loss = pl.pallas_call(
        body, name="loss_grad", grid=(T // TM,),
        in_specs=[tok, tgt_spec], out_specs=[tok, pl.BlockSpec((8, LANES), lambda i: (0, 0))],
        out_shape=[jax.ShapeDtypeStruct((T, D), F32), jax.ShapeDtypeStruct((8, LANES), F32)],
        compiler_params=_params(("arbitrary",)),
    )(x, tgt)
    return loss[0, 0], dx


def _rope_tables(cfg):
    rows = cfg.SL // cfg.GRID_W
    row = jnp.repeat(jnp.arange(rows, dtype=F32), cfg.GRID_W)
    col = jnp.tile(jnp.arange(cfg.GRID_W, dtype=F32), rows)
    n_freq = CHUNK // 4
    inv = ROPE_BASE ** (-jnp.arange(n_freq, dtype=F32) / n_freq)
    ang = jnp.concatenate([row[:, None] * inv[None, :], col[:, None] * inv[None, :]], axis=-1)
    cos, sin = jnp.cos(ang), jnp.sin(ang)
    cosf = jnp.concatenate([jnp.ones((cfg.SC, CHUNK), F32), jnp.concatenate([cos, cos], axis=-1)], axis=0)
    sinf = jnp.concatenate([jnp.zeros((cfg.SC, CHUNK), F32), jnp.concatenate([-sin, sin], axis=-1)], axis=0)
    return cosf, sinf


def _rope(x, cosf, sinf):
    return x * cosf + pltpu.roll(x, CHUNK // 2, 1) * sinf


def _irope(dy, cosf, sinf):
    return dy * cosf - pltpu.roll(dy, CHUNK // 2, 1) * sinf


def _prep_fwd(cfg, name, p, cosf, sinf, qg, kg):
    T, TM, TPE, H, KV = cfg.T, cfg.TM, cfg.TPE, cfg.H, cfg.KV
    HW = H * CHUNK
    kscale = CHUNK ** -0.5

    def body(p_ref, c_ref, s_ref, qg_ref, kg_ref, rq_ref, rk_ref, aq_ref, ak_ref):
        cosf, sinf = c_ref[...], s_ref[...]

        def normed(x, g):
            return x * lax.rsqrt(jnp.mean(x * x, axis=-1, keepdims=True) + EPS) * g

        for h in range(H):
            sl = pl.ds(h * CHUNK, CHUNK)
            rq_ref[:, sl] = _rope(p_ref[:, pl.ds(h * CHUNK, CHUNK)], cosf, sinf)
            rk_ref[:, sl] = _rope(p_ref[:, pl.ds(HW + h * CHUNK, CHUNK)], cosf, sinf) * kscale
            aq_ref[:, sl] = _rope(normed(p_ref[:, pl.ds(4 * HW + h * CHUNK, CHUNK)], qg_ref[...]),
                                  cosf, sinf).astype(BF16)
        for h in range(KV):
            ak_ref[:, pl.ds(h * CHUNK, CHUNK)] = _rope(
                normed(p_ref[:, pl.ds(5 * HW + h * CHUNK, CHUNK)], kg_ref[...]), cosf, sinf).astype(BF16)

    tab = pl.BlockSpec((TM, CHUNK), lambda i: (i % TPE, 0))
    vec = pl.BlockSpec((1, CHUNK), lambda i: (0, 0))
    return pl.pallas_call(
        body, name=name, grid=(T // TM,),
        in_specs=[pl.BlockSpec((TM, cfg.ABW), lambda i: (i, 0)), tab, tab, vec, vec],
        out_specs=[pl.BlockSpec((TM, HW), lambda i: (i, 0))] * 3 + [pl.BlockSpec((TM, KV * CHUNK), lambda i: (i, 0))],
        out_shape=[jax.ShapeDtypeStruct((T, HW), F32), jax.ShapeDtypeStruct((T, HW), F32),
                   jax.ShapeDtypeStruct((T, HW), BF16), jax.ShapeDtypeStruct((T, KV * CHUNK), BF16)],
        compiler_params=_params(("parallel",)),
    )(p, cosf, sinf, qg, kg)


def _prep_bwd(cfg, name, p, cosf, sinf, qg, kg, d_rq, d_rk, d_rv, d_gate, d_aq, d_ak, d_av):
    T, TM, TPE, H, KV = cfg.T, cfg.TM, cfg.TPE, cfg.H, cfg.KV
    HW = H * CHUNK
    kscale = CHUNK ** -0.5

    def body(p_ref, c_ref, s_ref, qg_ref, kg_ref, drq_ref, drk_ref, drv_ref, dgt_ref, daq_ref, dak_ref, dav_ref,
             dp_ref, dqg_ref, dkg_ref):
        i = pl.program_id(0)
        cosf, sinf = c_ref[...], s_ref[...]

        def norm_bwd(x, g, dn):
            rstd = lax.rsqrt(jnp.mean(x * x, axis=-1, keepdims=True) + EPS)
            xhat = x * rstd
            dg = jnp.sum(dn * xhat, axis=0, keepdims=True)
            dxh = dn * g
            return rstd * (dxh - xhat * jnp.mean(dxh * xhat, axis=-1, keepdims=True)), dg

        dqg = jnp.zeros((1, CHUNK), F32)
        dkg = jnp.zeros((1, CHUNK), F32)
        for h in range(H):
            sl = pl.ds(h * CHUNK, CHUNK)
            dp_ref[:, pl.ds(h * CHUNK, CHUNK)] = _irope(drq_ref[:, sl], cosf, sinf).astype(BF16)
            dp_ref[:, pl.ds(HW + h * CHUNK, CHUNK)] = (_irope(drk_ref[:, sl], cosf, sinf) * kscale).astype(BF16)
            dp_ref[:, pl.ds(2 * HW + h * CHUNK, CHUNK)] = drv_ref[:, sl].astype(BF16)
            dp_ref[:, pl.ds(3 * HW + h * CHUNK, CHUNK)] = dgt_ref[:, sl].astype(BF16)
            dx, dg = norm_bwd(p_ref[:, pl.ds(4 * HW + h * CHUNK, CHUNK)], qg_ref[...],
                              _irope(daq_ref[:, sl], cosf, sinf))
            dp_ref[:, pl.ds(4 * HW + h * CHUNK, CHUNK)] = dx.astype(BF16)
            dqg = dqg + dg
        for h in range(KV):
            sl = pl.ds(h * CHUNK, CHUNK)
            dx, dg = norm_bwd(p_ref[:, pl.ds(5 * HW + h * CHUNK, CHUNK)], kg_ref[...],
                              _irope(dak_ref[:, sl], cosf, sinf))
            dp_ref[:, pl.ds(5 * HW + h * CHUNK, CHUNK)] = dx.astype(BF16)
            dp_ref[:, pl.ds(5 * HW + (KV + h) * CHUNK, CHUNK)] = dav_ref[:, sl].astype(BF16)
            dkg = dkg + dg

        @pl.when(i == 0)
        def _():
            dqg_ref[...] = dqg
            dkg_ref[...] = dkg

        @pl.when(i > 0)
        def _():
            dqg_ref[...] += dqg
            dkg_ref[...] += dkg

    tab = pl.BlockSpec((TM, CHUNK), lambda i: (i % TPE, 0))
    vec = pl.BlockSpec((1, CHUNK), lambda i: (0, 0))
    hw = pl.BlockSpec((TM, HW), lambda i: (i, 0))
    kvw = pl.BlockSpec((TM, KV * CHUNK), lambda i: (i, 0))
    return pl.pallas_call(
        body, name=name, grid=(T // TM,),
        in_specs=[pl.BlockSpec((TM, cfg.ABW), lambda i: (i, 0)), tab, tab, vec, vec, hw, hw, hw, hw, hw, kvw, kvw],
        out_specs=[pl.BlockSpec((TM, cfg.ABW), lambda i: (i, 0)), vec, vec],
        out_shape=[jax.ShapeDtypeStruct((T, cfg.ABW), BF16), jax.ShapeDtypeStruct((1, CHUNK), F32),
                   jax.ShapeDtypeStruct((1, CHUNK), F32)],
        compiler_params=_params(("arbitrary",)),
    )(p, cosf, sinf, qg, kg, d_rq, d_rk, d_rv, d_gate, d_aq, d_ak, d_av)


def _ret_consts(direction, lg):
    C = CHUNK
    ii = lax.broadcasted_iota(jnp.int32, (C, C), 0)
    jj = lax.broadcasted_iota(jnp.int32, (C, C), 1)
    col = lax.broadcasted_iota(jnp.int32, (C, 1), 0).astype(F32)
    if direction == 0:
        mask, er, ek, eq = ii >= jj, (ii - jj).astype(F32), (C - 1.0) - col, col + 1.0
    else:
        mask, er, ek, eq = jj >= ii, (jj - ii).astype(F32), col, C - col
    er = jnp.where(mask, er, 0.0)
    dm = jnp.where(mask, jnp.exp(er * lg), 0.0)
    return dm, er, jnp.exp(ek * lg), ek, jnp.exp(eq * lg), eq, jnp.exp(C * lg)


def _ret_chunk(cfg, direction, t):
    n_all, n_ctx = cfg.S // CHUNK, cfg.SC // CHUNK
    if direction == 0:
        return t
    return jnp.where(t < n_ctx, n_ctx - 1 - t, n_all - 1 - (t - n_ctx))


def _head_norm_gate(o, g):
    mu = jnp.mean(o, axis=-1, keepdims=True)
    var = jnp.mean(jnp.square(o - mu), axis=-1, keepdims=True)
    rstd = lax.rsqrt(var + EPS)
    y = (o - mu) * rstd
    sg = jax.nn.sigmoid(g)
    return y, rstd, sg


def _retention_fwd(cfg, name, rq, rk, p, lgb):
    B, H, S, T = cfg.B, cfg.H, cfg.S, cfg.T
    n_all = S // CHUNK

    def body(q_ref, k_ref, v_ref, g_ref, lg_ref, o_ref, ret_ref, st_ref):
        for direction in (0, 1):
            dm, _, kd, _, qd, _, cd = _ret_consts(direction, lg_ref[direction][0:1, 0:1])
            st_ref[...] = jnp.zeros_like(st_ref)

            def step(t, carry):
                n = _ret_chunk(cfg, direction, t)
                sl = pl.ds(pl.multiple_of(n * CHUNK, CHUNK), CHUNK)
                q = q_ref[sl, :].astype(BF16)
                k = k_ref[sl, :]
                v = v_ref[sl, :].astype(BF16)
                st = st_ref[...]
                s = _dot_nt(q, k.astype(BF16)) * dm
                o = _dot(s.astype(BF16), v) + _dot(q, st.astype(BF16)) * qd
                if direction == 0:
                    o_ref[sl, :] = o
                else:
                    o_ref[sl, :] += o
                st_ref[...] = cd * st + _dot_tn((k * kd).astype(BF16), v)
                return carry

            lax.fori_loop(0, n_all, step, 0)

        def gate_step(n, carry):
            sl = pl.ds(pl.multiple_of(n * CHUNK, CHUNK), CHUNK)
            g = g_ref[sl, :]
            y, _, sg = _head_norm_gate(o_ref[sl, :], g)
            ret_ref[sl, :] = (y * (g * sg)).astype(BF16)
            return carry

        lax.fori_loop(0, n_all, gate_step, 0)

    HW = H * CHUNK
    blk = lambda off: pl.BlockSpec((S, CHUNK), lambda b, h: (b, off + h))
    return pl.pallas_call(
        body, name=name, grid=(B, H),
        in_specs=[blk(0), blk(0), blk(2 * H), blk(3 * H),
                  pl.BlockSpec((None, 2, 8, LANES), lambda b, h: (h, 0, 0, 0))],
        out_specs=[blk(0), blk(0)],
        out_shape=[jax.ShapeDtypeStruct((T, HW), F32), jax.ShapeDtypeStruct((T, HW), BF16)],
        scratch_shapes=[pltpu.VMEM((CHUNK, CHUNK), F32)],
        compiler_params=_params(("parallel", "parallel")),
    )(rq, rk, p, p, lgb)


def _retention_bwd(cfg, name, rq, rk, p, o_sum, dcat, lgb):
    B, H, S, T = cfg.B, cfg.H, cfg.S, cfg.T
    n_all = S // CHUNK
    C = CHUNK

    def body(q_ref, k_ref, v_ref, g_ref, o_ref, dr_ref, lg_ref, dq_ref, dk_ref, dv_ref, dg_ref, dlg_ref,
             do_ref, st_ref, ds_ref, acc_ref):
        def gate_step(n, carry):
            sl = pl.ds(pl.multiple_of(n * C, C), C)
            g = g_ref[sl, :]
            dr = dr_ref[sl, :]
            y, rstd, sg = _head_norm_gate(o_ref[sl, :], g)
            dy = dr * (g * sg)
            dg_ref[sl, :] = dr * y * (sg * (1.0 + g * (1.0 - sg)))
            do_ref[sl, :] = rstd * (dy - jnp.mean(dy, axis=-1, keepdims=True)
                                    - y * jnp.mean(dy * y, axis=-1, keepdims=True))
            return carry

        lax.fori_loop(0, n_all, gate_step, 0)

        for direction in (0, 1):
            dm, er, kd, ek, qd, eq, cd = _ret_consts(direction, lg_ref[direction][0:1, 0:1])

            def fwd_step(t, st):
                n = _ret_chunk(cfg, direction, t)
                sl = pl.ds(pl.multiple_of(n * C, C), C)
                st_ref[t] = st
                return cd * st + _dot_tn((k_ref[sl, :] * kd).astype(BF16), v_ref[sl, :].astype(BF16))

            lax.fori_loop(0, n_all, fwd_step, jnp.zeros((C, C), F32))
            ds_ref[...] = jnp.zeros_like(ds_ref)
            acc_ref[...] = jnp.zeros_like(acc_ref)

            def bwd_step(u, carry):
                t = n_all - 1 - u
                n = _ret_chunk(cfg, direction, t)
                sl = pl.ds(pl.multiple_of(n * C, C), C)
                q = q_ref[sl, :].astype(BF16)
                kf = k_ref[sl, :]
                k = kf.astype(BF16)
                v = v_ref[sl, :].astype(BF16)
                do = do_ref[sl, :]
                dob = do.astype(BF16)
                sp = st_ref[t]
                spb = sp.astype(BF16)
                ds = ds_ref[...]
                dsb = ds.astype(BF16)
                dk_state = _dot_nt(v, dsb) * kd
                dv = _dot((kf * kd).astype(BF16), dsb)
                pm = _dot_nt(q, k) * dm
                dpm = _dot_nt(dob, v)
                dsr = (dpm * dm).astype(BF16)
                dq = _dot(dsr, k)
                dk = _dot_tn(dsr, q) + dk_state
                dv = dv + _dot_tn(pm.astype(BF16), dob)
                doq = do * qd
                doqb = doq.astype(BF16)
                qs = _dot(q, spb)
                dq = dq + _dot_nt(doqb, spb)
                acc_ref[...] += (jnp.sum(dpm * pm * er, axis=0, keepdims=True)
                                 + jnp.sum(eq * doq * qs, axis=0, keepdims=True)
                                 + jnp.sum(ek * kf * dk_state, axis=0, keepdims=True)
                                 + (C * cd) * jnp.sum(ds * sp, axis=0, keepdims=True))
                ds_ref[...] = cd * ds + _dot_tn(q, doqb)
                if direction == 0:
                    dq_ref[sl, :] = dq
                    dk_ref[sl, :] = dk
                    dv_ref[sl, :] = dv
                else:
                    dq_ref[sl, :] += dq
                    dk_ref[sl, :] += dk
                    dv_ref[sl, :] += dv
                return carry

            lax.fori_loop(0, n_all, bwd_step, 0)
            dlg_ref[direction] = jnp.broadcast_to(jnp.sum(acc_ref[...], axis=1, keepdims=True), (8, LANES))

    HW = H * CHUNK
    blk = lambda off: pl.BlockSpec((S, CHUNK), lambda b, h: (b, off + h))
    return pl.pallas_call(
        body, name=name, grid=(B, H),
        in_specs=[blk(0), blk(0), blk(2 * H), blk(3 * H), blk(0), blk(0),
                  pl.BlockSpec((None, 2, 8, LANES), lambda b, h: (h, 0, 0, 0))],
        out_specs=[blk(0)] * 4 + [pl.BlockSpec((None, None, 2, 8, LANES), lambda b, h: (b, h, 0, 0, 0))],
        out_shape=[jax.ShapeDtypeStruct((T, HW), F32)] * 4 + [jax.ShapeDtypeStruct((B, H, 2, 8, LANES), F32)],
        scratch_shapes=[pltpu.VMEM((S, CHUNK), F32), pltpu.VMEM((n_all, C, C), F32), pltpu.VMEM((C, C), F32),
                        pltpu.VMEM((1, C), F32)],
        compiler_params=_params(("parallel", "parallel")),
    )(rq, rk, p, p, o_sum, dcat, lgb)


def _attn_scores(cfg, q, k, t):
    kcol = lax.broadcasted_iota(jnp.int32, (1, cfg.S), 1)
    bias = jnp.where(jnp.logical_or(t > 0, kcol < cfg.SC), 0.0, -1e30)
    s = _dot_nt(q, k) * (CHUNK ** -0.5) + bias
    e = jnp.exp(s - jnp.max(s, axis=-1, keepdims=True))
    return e, 1.0 / jnp.sum(e, axis=-1, keepdims=True)


def _attention_fwd(cfg, name, aq, ak, p):
    B, H, KV, S, T, TM, TPE = cfg.B, cfg.H, cfg.KV, cfg.S, cfg.T, cfg.TM, cfg.TPE
    G = H // KV
    v_off = (5 * H + KV)

    def body(q_ref, k_ref, v_ref, o_ref):
        e, inv = _attn_scores(cfg, q_ref[...], k_ref[...], pl.program_id(2))
        o_ref[...] = (_dot(e.astype(BF16), v_ref[...].astype(BF16)) * inv).astype(BF16)

    return pl.pallas_call(
        body, name=name, grid=(B, H, TPE),
        in_specs=[pl.BlockSpec((TM, CHUNK), lambda b, h, t: (b * TPE + t, h)),
                  pl.BlockSpec((S, CHUNK), lambda b, h, t: (b, h // G)),
                  pl.BlockSpec((S, CHUNK), lambda b, h, t: (b, v_off + h // G))],
        out_specs=pl.BlockSpec((TM, CHUNK), lambda b, h, t: (b * TPE + t, h)),
        out_shape=jax.ShapeDtypeStruct((T, H * CHUNK), BF16),
        compiler_params=_params(("parallel", "parallel", "parallel")),
    )(aq, ak, p)


def _attention_bwd(cfg, name, aq, ak, p, dcat):
    B, H, KV, S, T, TM, TPE = cfg.B, cfg.H, cfg.KV, cfg.S, cfg.T, cfg.TM, cfg.TPE
    G = H // KV
    v_off = (5 * H + KV)

    def body(q_ref, k_ref, v_ref, do_ref, dq_ref, dk_ref, dv_ref):
        g, t = pl.program_id(2), pl.program_id(3)
        q, k = q_ref[...], k_ref[...]
        v = v_ref[...].astype(BF16)
        dob = do_ref[...].astype(BF16)
        e, inv = _attn_scores(cfg, q, k, t)
        pr = e * inv
        dpr = _dot_nt(dob, v)
        ds = (pr * (dpr - jnp.sum(pr * dpr, axis=-1, keepdims=True)) * (CHUNK ** -0.5)).astype(BF16)
        dq_ref[...] = _dot(ds, k)
        dk = _dot_tn(ds, q)
        dv = _dot_tn(pr.astype(BF16), dob)
        first = jnp.logical_and(g == 0, t == 0)

        @pl.when(first)
        def _():
            dk_ref[...] = dk
            dv_ref[...] = dv

        @pl.when(jnp.logical_not(first))
        def _():
            dk_ref[...] += dk
            dv_ref[...] += dv

    qspec = pl.BlockSpec((TM, CHUNK), lambda b, kv, g, t: (b * TPE + t, kv * G + g))
    kvspec = pl.BlockSpec((S, CHUNK), lambda b, kv, g, t: (b, kv))
    return pl.pallas_call(
        body, name=name, grid=(B, KV, G, TPE),
        in_specs=[qspec, kvspec, pl.BlockSpec((S, CHUNK), lambda b, kv, g, t: (b, v_off + kv)),
                  pl.BlockSpec((TM, CHUNK), lambda b, kv, g, t: (b * TPE + t, H + kv * G + g))],
        out_specs=[qspec, kvspec, kvspec],
        out_shape=[jax.ShapeDtypeStruct((T, H * CHUNK), F32), jax.ShapeDtypeStruct((T, KV * CHUNK), F32),
                   jax.ShapeDtypeStruct((T, KV * CHUNK), F32)],
        compiler_params=_params(("parallel", "parallel", "arbitrary", "arbitrary")),
    )(aq, ak, p, dcat)


_GELU_C = math.sqrt(2.0 / math.pi)


def _gelu(x):
    return 0.5 * x * (1.0 + jnp.tanh(_GELU_C * (x + 0.044715 * x * x * x)))


def _gelu_grad(x):
    th = jnp.tanh(_GELU_C * (x + 0.044715 * x * x * x))
    return 0.5 * (1.0 + th) + 0.5 * x * (1.0 - th * th) * _GELU_C * (1.0 + 3.0 * 0.044715 * x * x)


def _cm_fwd(cfg, name, a, vg, ws, bs):
    T, TM, W, NG = cfg.T, cfg.TM, cfg.CMW, cfg.CMG

    def body(a_ref, vg_ref, ws_ref, bs_ref, m_ref):
        v = _gelu(a_ref[:, pl.ds(W, W)])
        vn = (v * lax.rsqrt(jnp.mean(v * v, axis=-1, keepdims=True) + EPS) * vg_ref[...]).astype(BF16)
        for c in range(TM // CHUNK):
            for g in range(NG):
                rows, cols = slice(c * CHUNK, (c + 1) * CHUNK), slice(g * CHUNK, (g + 1) * CHUNK)
                sv = _dot(ws_ref[g].astype(BF16), vn[rows, cols]) + bs_ref[g]
                u = _gelu(a_ref[pl.ds(c * CHUNK, CHUNK), pl.ds(g * CHUNK, CHUNK)])
                m_ref[pl.ds(c * CHUNK, CHUNK), pl.ds(g * CHUNK, CHUNK)] = (u * sv).astype(BF16)

    return pl.pallas_call(
        body, name=name, grid=(T // TM,),
        in_specs=[pl.BlockSpec((TM, 2 * W), lambda i: (i, 0)), pl.BlockSpec((1, W), lambda i: (0, 0)),
                  pl.BlockSpec((NG, CHUNK, CHUNK), lambda i: (0, 0, 0)),
                  pl.BlockSpec((NG, CHUNK, 1), lambda i: (0, 0, 0))],
        out_specs=pl.BlockSpec((TM, W), lambda i: (i, 0)),
        out_shape=jax.ShapeDtypeStruct((T, W), BF16),
        compiler_params=_params(("parallel",)),
    )(a, vg, ws, bs)


def _cm_bwd(cfg, name, a, vg, ws, bs, dm):
    T, TM, W, NG = cfg.T, cfg.TM, cfg.CMW, cfg.CMG

    def body(a_ref, vg_ref, ws_ref, bs_ref, dm_ref, da_ref, dws_ref, dbs_ref, dvg_ref, dvn_ref):
        i = pl.program_id(0)

        @pl.when(i == 0)
        def _():
            dws_ref[...] = jnp.zeros_like(dws_ref)
            dbs_ref[...] = jnp.zeros_like(dbs_ref)
            dvg_ref[...] = jnp.zeros_like(dvg_ref)

        av = a_ref[:, pl.ds(W, W)]
        v = _gelu(av)
        rstd = lax.rsqrt(jnp.mean(v * v, axis=-1, keepdims=True) + EPS)
        xhat = v * rstd
        vg = vg_ref[...]
        vn = (xhat * vg).astype(BF16)
        for c in range(TM // CHUNK):
            for g in range(NG):
                rows, cols = slice(c * CHUNK, (c + 1) * CHUNK), slice(g * CHUNK, (g + 1) * CHUNK)
                rs, cs = pl.ds(c * CHUNK, CHUNK), pl.ds(g * CHUNK, CHUNK)
                wsb = ws_ref[g].astype(BF16)
                blk = vn[rows, cols]
                sv = _dot(wsb, blk) + bs_ref[g]
                au = a_ref[rs, cs]
                dmb = dm_ref[rs, cs]
                da_ref[rs, cs] = (dmb * sv * _gelu_grad(au)).astype(BF16)
                dsv = dmb * _gelu(au)
                dsvb = dsv.astype(BF16)
                dbs_ref[g] += jnp.sum(dsv, axis=1, keepdims=True)
                dws_ref[g] += _dot_nt(dsvb, blk)
                dvn_ref[rs, cs] = _dot_tn(wsb, dsvb)
        dvn = dvn_ref[...]
        dvg_ref[...] += jnp.sum(dvn * xhat, axis=0, keepdims=True)
        dxh = dvn * vg
        dv = rstd * (dxh - xhat * jnp.mean(dxh * xhat, axis=-1, keepdims=True))
        da_ref[:, pl.ds(W, W)] = (dv * _gelu_grad(av)).astype(BF16)

    return pl.pallas_call(
        body, name=name, grid=(T // TM,),
        in_specs=[pl.BlockSpec((TM, 2 * W), lambda i: (i, 0)), pl.BlockSpec((1, W), lambda i: (0, 0)),
                  pl.BlockSpec((NG, CHUNK, CHUNK), lambda i: (0, 0, 0)),
                  pl.BlockSpec((NG, CHUNK, 1), lambda i: (0, 0, 0)), pl.BlockSpec((TM, W), lambda i: (i, 0))],
        out_specs=[pl.BlockSpec((TM, 2 * W), lambda i: (i, 0)), pl.BlockSpec((NG, CHUNK, CHUNK), lambda i: (0, 0, 0)),
                   pl.BlockSpec((NG, CHUNK, 1), lambda i: (0, 0, 0)), pl.BlockSpec((1, W), lambda i: (0, 0))],
        out_shape=[jax.ShapeDtypeStruct((T, 2 * W), BF16), jax.ShapeDtypeStruct((NG, CHUNK, CHUNK), F32),
                   jax.ShapeDtypeStruct((NG, CHUNK, 1), F32), jax.ShapeDtypeStruct((1, W), F32)],
        scratch_shapes=[pltpu.VMEM((TM, W), F32)],
        compiler_params=_params(("arbitrary",)),
    )(a, vg, ws, bs, dm)


def _layer_weights(l):
    mixer = ("ab_w_in", "ab_w_out") if l % 2 == 0 else ("cm_w_in", "cm_w_out")
    return [(mixer[0], l // 2), (mixer[1], l // 2), ("ff_w1", l), ("ff_w2", l)]


def _local_step(cfg, xcat, tgt, mods, shards, w):
    D, TM, H = cfg.D, cfg.TM, cfg.H
    cosf, sinf = _rope_tables(cfg)
    full, big, recv = {}, {}, {}

    def gather_of(keys):
        return _Carry("gather", tuple(shards[n][i] for n, i in keys), tuple(BIG[n] for n, _ in keys))

    def exchange_of(keys):
        return _Carry("exchange", tuple(big[k] for k in keys), tuple(BIG[n] for n, _ in keys))

    def mm(pending, name, a, b, **kw):
        if not pending:
            return _mm(name, a, b, **kw)
        key, carry, sink = pending.pop(0)
        out, (got,) = _mm(name, a, b, carry=carry, **kw)
        sink[key] = got
        return out

    keys0 = _layer_weights(0)
    full.update(zip(keys0, _comm_call("gather_weights_0", gather_of(keys0))))
    TG = 3 * TM if cfg.TPE % 3 == 0 else TM
    tiles_per_ex = cfg.S // TG
    gate_spec = pl.BlockSpec((None, 2, 6, D), lambda i, j, k: (i // tiles_per_ex, 0, 0, 0))

    def resid_epi(igate):
        def epi(acc, row_tile, x_ref, mod_ref):
            row = lax.broadcasted_iota(jnp.int32, (TG, 1), 0)
            is_ctx = jnp.logical_and(row_tile % tiles_per_ex == 0, row < cfg.SC)
            gate = jnp.where(is_ctx, mod_ref[0, pl.ds(igate, 1), :], mod_ref[1, pl.ds(igate, 1), :])
            return x_ref[...] + gate * acc, acc
        return epi

    def gated_out(pending, name, a, key, x, mod, igate):
        return mm(pending, name, a, full[key], mode="nn", tm=TG, tn=D, outs=[F32, BF16], epi=resid_epi(igate),
                  extras=[(x, pl.BlockSpec((TG, D), lambda i, j, k: (i, j))), (mod, gate_spec)])

    saved = []
    x = xcat
    for l in range(N_LAYERS):
        li = l // 2
        mod = mods[l]
        k_in, k_out, k_ff1, k_ff2 = _layer_weights(l)
        pend = [(k, gather_of([k]), full) for k in _layer_weights(l + 1)] if l + 1 < N_LAYERS else []
        s = {"x0": x}
        s["h"] = _norm_mod_fwd(cfg, f"norm1_fwd_{l}", x, w["norm1_g"][l][None], mod, 0, 1)
        if l % 2 == 0:
            lgb = jnp.broadcast_to(jax.nn.log_sigmoid(w["ret_decay"][li]).T[:, :, None, None], (H, 2, 8, LANES))
            qg, kg = w["att_q_norm_g"][li][None], w["att_k_norm_g"][li][None]
            s["p"] = mm(pend, f"ab_in_{l}", s["h"], full[k_in], mode="nn", outs=[F32], tn=768)
            s["rq"], s["rk"], s["aq"], s["ak"] = _prep_fwd(cfg, f"prep_fwd_{l}", s["p"], cosf, sinf, qg, kg)
            s["o"], ret = _retention_fwd(cfg, f"ret_fwd_{l}", s["rq"], s["rk"], s["p"], lgb)
            att = _attention_fwd(cfg, f"att_fwd_{l}", s["aq"], s["ak"], s["p"])
            s["cat"] = jnp.concatenate([ret, att], axis=-1)
            s["lgb"], s["qg"], s["kg"] = lgb, qg, kg
            x, s["y1"] = gated_out(pend, f"ab_out_{l}", s["cat"], k_out, x, mod, 2)
        else:
            s["a"] = mm(pend, f"cm_in_{l}", s["h"], full[k_in], mode="nn", outs=[F32])
            s["vg"], s["ws"], s["bs"] = w["cm_v_norm_g"][li][None], w["cm_w_s"][li], w["cm_b_s"][li][:, :, None]
            s["m"] = _cm_fwd(cfg, f"cm_fwd_{l}", s["a"], s["vg"], s["ws"], s["bs"])
            x, s["y1"] = gated_out(pend, f"cm_out_{l}", s["m"], k_out, x, mod, 2)
        s["x1"] = x
        s["h2"] = _norm_mod_fwd(cfg, f"norm2_fwd_{l}", x, w["norm2_g"][l][None], mod, 3, 4)
        s["r"] = mm(pend, f"ff1_{l}", s["h2"], full[k_ff1], mode="nn", outs=[BF16],
                    epi=lambda acc, row_tile: (jnp.square(jnp.maximum(acc, 0.0)),))
        x, s["y2"] = gated_out(pend, f"ff2_{l}", s["r"], k_ff2, x, mod, 5)
        saved.append(s)

    loss, dx = _loss_grad(cfg, x, tgt)

    small = {k: [None] * n for k, n in (("norm1_g", 4), ("norm2_g", 4), ("ret_lg", 2), ("att_q_norm_g", 2),
                                        ("att_k_norm_g", 2), ("cm_v_norm_g", 2), ("cm_w_s", 2), ("cm_b_s", 2))}
    dmods = [None] * N_LAYERS

    for l in reversed(range(N_LAYERS)):
        li = l // 2
        s, mod = saved[l], mods[l]
        k_in, k_out, k_ff1, k_ff2 = _layer_weights(l)
        pend = [(k, exchange_of([k]), recv) for k in reversed(_layer_weights(l + 1))] if l + 1 < N_LAYERS else []
        dy2, dg2 = _gate_bwd(cfg, f"gate2_bwd_{l}", dx, s["y2"], mod, 5)
        da2 = mm(pend, f"ff2_dx_{l}", dy2, full[k_ff2], mode="nt", outs=[BF16],
                 epi=lambda acc, row_tile, r_ref: (acc * (2.0 * jnp.sqrt(r_ref[...].astype(F32))),),
                 extras=[(s["r"], pl.BlockSpec((_tile(cfg.T, 1024), _tile(cfg.FF, 1024)), lambda i, j, k: (i, j)))])
        big[k_ff2] = mm(pend, f"ff2_dw_{l}", s["r"], dy2, mode="tn", outs=[BF16])
        big[k_ff1] = mm(pend, f"ff1_dw_{l}", s["h2"], da2, mode="tn", outs=[BF16])
        dh2 = mm(pend, f"ff1_dx_{l}", da2, full[k_ff1], mode="nt", outs=[F32])
        dx, dm2, small["norm2_g"][l] = _norm_mod_bwd(cfg, f"norm2_bwd_{l}", s["x1"], w["norm2_g"][l][None], mod, 3, 4,
                                                     dh2, dx)
        do, dg1 = _gate_bwd(cfg, f"gate1_bwd_{l}", dx, s["y1"], mod, 2)
        if l % 2 == 0:
            big[k_out] = _mm(f"ab_out_dw_{l}", s["cat"], do, mode="tn", outs=[BF16])
            dcat = _mm(f"ab_out_dx_{l}", do, full[k_out], mode="nt", outs=[F32])
            d_rq, d_rk, d_rv, d_gt, dlg = _retention_bwd(cfg, f"ret_bwd_{l}", s["rq"], s["rk"], s["p"], s["o"], dcat,
                                                         s["lgb"])
            d_aq, d_ak, d_av = _attention_bwd(cfg, f"att_bwd_{l}", s["aq"], s["ak"], s["p"], dcat)
            dp, dqg, dkg = _prep_bwd(cfg, f"prep_bwd_{l}", s["p"], cosf, sinf, s["qg"], s["kg"],
                                     d_rq, d_rk, d_rv, d_gt, d_aq, d_ak, d_av)
            small["ret_lg"][li] = jnp.sum(dlg[:, :, :, 0, 0], axis=0).T
            small["att_q_norm_g"][li], small["att_k_norm_g"][li] = dqg[0], dkg[0]
            big[k_in] = _mm(f"ab_in_dw_{l}", s["h"], dp, mode="tn", outs=[BF16])
            dh = _mm(f"ab_in_dx_{l}", dp, full[k_in], mode="nt", outs=[F32], tk=768)
        else:
            big[k_out] = _mm(f"cm_out_dw_{l}", s["m"], do, mode="tn", outs=[BF16])
            dm = _mm(f"cm_out_dx_{l}", do, full[k_out], mode="nt", outs=[F32])
            da, dws, dbs, dvg = _cm_bwd(cfg, f"cm_bwd_{l}", s["a"], s["vg"], s["ws"], s["bs"], dm)
            small["cm_w_s"][li], small["cm_b_s"][li], small["cm_v_norm_g"][li] = dws, dbs[:, :, 0], dvg[0]
            big[k_in] = _mm(f"cm_in_dw_{l}", s["h"], da, mode="tn", outs=[BF16])
            dh = _mm(f"cm_in_dx_{l}", da, full[k_in], mode="nt", outs=[F32])
        dx, dm1, small["norm1_g"][l] = _norm_mod_bwd(cfg, f"norm1_bwd_{l}", s["x0"], w["norm1_g"][l][None], mod, 0, 1,
                                                     dh, dx)
        dmods[l] = jnp.concatenate([dm1, dg1, dm2, dg2], axis=2)
    recv.update(zip(keys0, _comm_call("exchange_grads_0", exchange_of(keys0))))
    return loss, dx, recv, small, dmods


N_DEV = 8
N_CHIP = 4
MESH = pl.DeviceIdType.MESH
ANY = pl.BlockSpec(memory_space=pl.ANY)
BIG = {"ab_w_in": 1, "ab_w_out": 0, "cm_w_in": 1, "cm_w_out": 0, "ff_w1": 1, "ff_w2": 0}


class _Carry(NamedTuple):
    kind: str
    srcs: tuple
    axes: tuple


def _place():
    x, y, c = lax.axis_index("x"), lax.axis_index("y"), lax.axis_index("c")
    return x, y, c, [(1 - x, y), (x, 1 - y), (1 - x, 1 - y)]


def _shard_of(ref, axis, s, width):
    start = pl.multiple_of(s * width, LANES)
    if axis == 0:
        return ref.at[pl.ds(start, width), :]
    return ref.at[:, pl.ds(start, width)]


def _carry_out_shapes(carry):
    shapes = []
    for src, axis in zip(carry.srcs, carry.axes):
        shape = list(src.shape)
        if carry.kind == "gather":
            shape[axis] *= N_CHIP
        else:
            shape[axis] //= N_CHIP
            shape = [N_CHIP] + shape
        shapes.append(jax.ShapeDtypeStruct(tuple(shape), src.dtype))
    return shapes


def _carry_copies(carry, srcs, dsts, send_sems, recv_sems, local_sems):
    x, y, c, chips = _place()
    me = 2 * x + y
    copies = []
    for t, axis in enumerate(carry.axes):
        if carry.kind == "gather":
            own = _shard_of(dsts[t], axis, me, srcs[t].shape[axis])
            copies.append(pltpu.make_async_copy(srcs[t], own, local_sems.at[t]))
            parts = [(srcs[t], own)] * 3
        else:
            width = dsts[t].shape[1 + axis]
            copies.append(pltpu.make_async_copy(_shard_of(srcs[t], axis, me, width), dsts[t].at[3], local_sems.at[t]))
            parts = [(_shard_of(srcs[t], axis, 2 * px + py, width), dsts[t].at[j]) for j, (px, py) in enumerate(chips)]
        for j, (px, py) in enumerate(chips):
            copies.append(pltpu.make_async_remote_copy(
                src_ref=parts[j][0], dst_ref=parts[j][1], send_sem=send_sems.at[3 * t + j],
                recv_sem=recv_sems.at[3 * t + j], device_id=(px, py, c), device_id_type=MESH))
    return copies


def _comm_call(name, carry):
    nc = len(carry.srcs)

    def body(*refs):
        copies = _carry_copies(carry, refs[:nc], refs[nc:2 * nc], *refs[2 * nc:])
        for cp in copies:
            cp.start()
        for cp in copies:
            cp.wait()

    return pl.pallas_call(
        body, name=name, out_shape=_carry_out_shapes(carry), in_specs=[ANY] * nc, out_specs=[ANY] * nc,
        scratch_shapes=[pltpu.SemaphoreType.DMA((3 * nc,)), pltpu.SemaphoreType.DMA((3 * nc,)),
                        pltpu.SemaphoreType.DMA((nc,))],
    )(*carry.srcs)


def _allgather8(name, block):
    m_per, n = block.shape

    def body(x_ref, out_ref, send_sems, recv_sems, local_sem):
        x, y, c, chips = _place()
        me, sibling = (x, y, c), (x, y, 1 - c)

        def rows(px, py, pc):
            return out_ref.at[pl.ds((4 * px + 2 * py + pc) * m_per, m_per), :]

        def copy(k, blk, to, src=None):
            return pltpu.make_async_remote_copy(
                src_ref=rows(*blk) if src is None else src, dst_ref=rows(*blk),
                send_sem=send_sems.at[k], recv_sem=recv_sems.at[k], device_id=to, device_id_type=MESH)

        mine = pltpu.make_async_copy(x_ref, rows(*me), local_sem)
        mine.start()
        first = [copy(0, me, sibling, src=x_ref)]
        first += [copy(1 + j, me, (*chip, c), src=x_ref) for j, chip in enumerate(chips)]
        for cp in first:
            cp.start()
        passed = [copy(4 + j, (*chip, c), sibling) for j, chip in enumerate(chips)]
        for j, chip in enumerate(chips):
            copy(1 + j, (*chip, c), me).wait_recv()
            passed[j].start()
        copy(0, sibling, me).wait_recv()
        for j, chip in enumerate(chips):
            copy(4 + j, (*chip, 1 - c), me).wait_recv()
        for cp in first + passed:
            cp.wait_send()
        mine.wait()

    return pl.pallas_call(
        body, name=name, out_shape=jax.ShapeDtypeStruct((N_DEV * m_per, n), block.dtype),
        in_specs=[pl.BlockSpec(memory_space=pltpu.VMEM)], out_specs=pl.BlockSpec(memory_space=pltpu.VMEM),
        scratch_shapes=[pltpu.SemaphoreType.DMA((7,)), pltpu.SemaphoreType.DMA((7,)), pltpu.SemaphoreType.DMA],
        compiler_params=pltpu.CompilerParams(vmem_limit_bytes=VMEM_LIMIT),
    )(block)


def _swap_sibling(parts):
    n_t = len(parts)

    def body(*refs):
        srcs, outs = refs[:n_t], refs[n_t:2 * n_t]
        send_sems, recv_sems = refs[2 * n_t:]
        x, y, c, _ = _place()
        copies = []
        for t in range(n_t):
            cp = pltpu.make_async_remote_copy(
                src_ref=srcs[t], dst_ref=outs[t], send_sem=send_sems.at[t], recv_sem=recv_sems.at[t],
                device_id=(x, y, 1 - c), device_id_type=MESH)
            cp.start()
            copies.append(cp)
        for cp in copies:
            cp.wait()

    return pl.pallas_call(
        body, name="swap_sibling", out_shape=[jax.ShapeDtypeStruct(p.shape, p.dtype) for p in parts],
        in_specs=[ANY] * n_t, out_specs=[ANY] * n_t,
        scratch_shapes=[pltpu.SemaphoreType.DMA((n_t,)), pltpu.SemaphoreType.DMA((n_t,))],
    )(*parts)


def _rows_view(a):
    if a.ndim == 1:
        return a.reshape(1, a.shape[0])
    return a.reshape(-1, a.shape[-1])


def _row_tile(rows, cols, target_elems=1 << 17):
    tr = rows
    while tr % 16 == 0 and tr * cols > target_elems:
        tr //= 2
    return tr


def _sum_leading(name, a):
    n, rows, cols = a.shape
    tr = _row_tile(rows, cols * n)

    def body(a_ref, o_ref):
        acc = a_ref[0].astype(F32)
        for i in range(1, n):
            acc = acc + a_ref[i].astype(F32)
        o_ref[...] = acc

    return pl.pallas_call(
        body, name=name, grid=(rows // tr,),
        in_specs=[pl.BlockSpec((n, tr, cols), lambda i: (0, i, 0))],
        out_specs=pl.BlockSpec((tr, cols), lambda i: (i, 0)),
        out_shape=jax.ShapeDtypeStruct((rows, cols), F32),
        compiler_params=_params(("parallel",)),
    )(a)


def _silu_rows(name, x):
    def body(x_ref, o_ref):
        v = x_ref[...]
        o_ref[...] = v * jax.nn.sigmoid(v)

    return pl.pallas_call(body, name=name, out_shape=jax.ShapeDtypeStruct(x.shape, F32))(x)


def _silu_bwd_rows(name, x, dy):
    def body(x_ref, dy_ref, o_ref):
        v = x_ref[...]
        sg = jax.nn.sigmoid(v)
        o_ref[...] = dy_ref[...] * (sg * (1.0 + v * (1.0 - sg)))

    return pl.pallas_call(body, name=name, out_shape=jax.ShapeDtypeStruct(x.shape, F32))(x, dy)


def _adamw(name, w, g_parts, m, v):
    shape = w.shape
    w2, m2, v2 = _rows_view(w), _rows_view(m), _rows_view(v)
    gs = [_rows_view(g) for g in g_parts]
    rows, cols = w2.shape
    tr = _row_tile(rows, cols)
    ng = len(gs)

    def body(*refs):
        w_ref, m_ref, v_ref = refs[0], refs[1], refs[2]
        g_refs = refs[3:3 + ng]
        g_out, d_out, m_out, v_out = refs[3 + ng:]
        g = g_refs[0][...]
        for r in g_refs[1:]:
            g = g + r[...]
        m1 = ADAM_B1 * m_ref[...] + (1.0 - ADAM_B1) * g
        v1 = ADAM_B2 * v_ref[...] + (1.0 - ADAM_B2) * jnp.square(g)
        m_hat = m1 / (1.0 - ADAM_B1 ** ADAM_STEP)
        v_hat = v1 / (1.0 - ADAM_B2 ** ADAM_STEP)
        g_out[...] = g
        d_out[...] = -ADAM_LR * (m_hat / (jnp.sqrt(v_hat) + ADAM_EPS) + ADAM_WD * w_ref[...])
        m_out[...] = m1
        v_out[...] = v1

    spec = pl.BlockSpec((tr, cols), lambda i: (i, 0))
    res = pl.pallas_call(
        body, name=name, grid=(rows // tr,), in_specs=[spec] * (3 + ng), out_specs=[spec] * 4,
        out_shape=[jax.ShapeDtypeStruct((rows, cols), F32)] * 4,
        compiler_params=_params(("parallel",)),
    )(w2, m2, v2, *gs)
    return tuple(r.reshape(shape) for r in res)


MOD_ROWS = 48


def kernel(x, c, ctx, c_ctx, mod_w, mod_b, norm1_g, norm2_g, ab_w_in, ab_w_out, ret_decay, att_q_norm_g, att_k_norm_g, cm_w_in, cm_v_norm_g, cm_w_s, cm_b_s, cm_w_out, ff_w1, ff_w2, loss_target, m_c_ctx, m_mod_w, m_mod_b, m_norm1_g, m_norm2_g, m_ab_w_in, m_ab_w_out, m_ret_decay, m_att_q_norm_g, m_att_k_norm_g, m_cm_w_in, m_cm_v_norm_g, m_cm_w_s, m_cm_b_s, m_cm_w_out, m_ff_w1, m_ff_w2, v_c_ctx, v_mod_w, v_mod_b, v_norm1_g, v_norm2_g, v_ab_w_in, v_ab_w_out, v_ret_decay, v_att_q_norm_g, v_att_k_norm_g, v_cm_w_in, v_cm_v_norm_g, v_cm_w_s, v_cm_b_s, v_cm_w_out, v_ff_w1, v_ff_w2):
    B, SL, D = x.shape
    cfg = Cfg(B=B, SC=ctx.shape[1], SL=SL, D=D, FF=ff_w1.shape[2] * N_CHIP)
    L = N_LAYERS
    n_ex = B * N_DEV
    mcols = mod_w.shape[2]
    weights = dict(c_ctx=c_ctx, mod_w=mod_w, mod_b=mod_b, norm1_g=norm1_g, norm2_g=norm2_g, ab_w_in=ab_w_in,
                   ab_w_out=ab_w_out, ret_decay=ret_decay, att_q_norm_g=att_q_norm_g, att_k_norm_g=att_k_norm_g,
                   cm_w_in=cm_w_in, cm_v_norm_g=cm_v_norm_g, cm_w_s=cm_w_s, cm_b_s=cm_b_s, cm_w_out=cm_w_out,
                   ff_w1=ff_w1, ff_w2=ff_w2)
    m_in = dict(c_ctx=m_c_ctx, mod_w=m_mod_w, mod_b=m_mod_b, norm1_g=m_norm1_g, norm2_g=m_norm2_g, ab_w_in=m_ab_w_in,
                ab_w_out=m_ab_w_out, ret_decay=m_ret_decay, att_q_norm_g=m_att_q_norm_g, att_k_norm_g=m_att_k_norm_g,
                cm_w_in=m_cm_w_in, cm_v_norm_g=m_cm_v_norm_g, cm_w_s=m_cm_w_s, cm_b_s=m_cm_b_s, cm_w_out=m_cm_w_out,
                ff_w1=m_ff_w1, ff_w2=m_ff_w2)
    v_in = dict(c_ctx=v_c_ctx, mod_w=v_mod_w, mod_b=v_mod_b, norm1_g=v_norm1_g, norm2_g=v_norm2_g, ab_w_in=v_ab_w_in,
                ab_w_out=v_ab_w_out, ret_decay=v_ret_decay, att_q_norm_g=v_att_q_norm_g, att_k_norm_g=v_att_k_norm_g,
                cm_w_in=v_cm_w_in, cm_v_norm_g=v_cm_v_norm_g, cm_w_s=v_cm_w_s, cm_b_s=v_cm_b_s, cm_w_out=v_cm_w_out,
                ff_w1=v_ff_w1, ff_w2=v_ff_w2)
    xi, yi, ci = lax.axis_index("x"), lax.axis_index("y"), lax.axis_index("c")
    chip = 2 * xi + yi
    dev = 2 * chip + ci

    shards = {n: [weights[n][i].astype(BF16) for i in range(weights[n].shape[0])] for n in BIG}
    vgw = cm_v_norm_g.shape[1]
    blk = jnp.zeros((8, D), F32).at[:B].set(c).at[B:B + 2, :vgw].set(cm_v_norm_g)
    g0 = _allgather8("gather_c", blk).reshape(N_DEV, 8, D)
    c_all = g0[:, :B].reshape(n_ex, D)
    vg_full = jnp.concatenate([g0[2 * s, B:B + 2, :vgw] for s in range(N_CHIP)], axis=-1)

    pre = jnp.zeros((MOD_ROWS, D), F32).at[:n_ex].set(c_all).at[n_ex].set(c_ctx)
    act = _silu_rows("silu_c", pre)
    mpart = jnp.stack([_mm(f"mod_fwd_{l}", act, mod_w, mode="nn", layer=l, outs=[F32], tn=mcols) for l in range(L)])
    g1 = _allgather8("gather_mod", mpart.reshape(L * MOD_ROWS, mcols)).reshape(N_DEV, L, MOD_ROWS, mcols)
    mod_all = jnp.concatenate([g1[2 * s] for s in range(N_CHIP)], axis=-1) + mod_b[:, None, :]
    mod_lat = lax.dynamic_slice_in_dim(mod_all, dev * B, B, axis=1)
    mod_ctx = jnp.broadcast_to(mod_all[:, n_ex][:, None], mod_lat.shape)
    mods = jnp.stack([mod_ctx, mod_lat], axis=2).reshape(L, B, 2, 6, D)

    w = dict(norm1_g=norm1_g, norm2_g=norm2_g, ret_decay=ret_decay, att_q_norm_g=att_q_norm_g,
             att_k_norm_g=att_k_norm_g, cm_v_norm_g=vg_full, cm_w_s=cm_w_s, cm_b_s=cm_b_s)
    xcat = jnp.concatenate([ctx, x], axis=1).reshape(cfg.T, D)
    loss_local, dxcat, recv, small, dmods = _local_step(cfg, xcat, loss_target.reshape(B * SL, D), mods, shards, w)
    loss = lax.psum(loss_local, ("x", "y", "c"))
    grad_x = dxcat.reshape(B, cfg.S, D)[:, cfg.SC:, :]

    part = [jnp.stack([_sum_leading(f"sum_{n}_{i}", recv[(n, i)]) for i in range(weights[n].shape[0])]) for n in BIG]
    other = _swap_sibling(part)
    out = {}
    for n, p_mine, p_other in zip(BIG, part, other):
        out[n] = _adamw(f"adamw_{n}", weights[n], [p_mine, p_other], m_in[n], v_in[n])

    dmod = jnp.stack(dmods).reshape(L, B, 2, 6 * D)
    dmod_lat = dmod[:, :, 1]
    dmod_ctx = jnp.sum(dmod[:, :, 0], axis=1)
    d_ret = jnp.stack(small["ret_lg"]) * jax.nn.sigmoid(-ret_decay)
    summed = [dmod_ctx.reshape(-1), jnp.stack(small["norm1_g"]).reshape(-1), jnp.stack(small["norm2_g"]).reshape(-1),
              jnp.stack(small["cm_v_norm_g"]).reshape(-1), jnp.stack(small["cm_w_s"]).reshape(-1),
              jnp.stack(small["cm_b_s"]).reshape(-1), jnp.stack(small["att_q_norm_g"]).reshape(-1),
              jnp.stack(small["att_k_norm_g"]).reshape(-1), d_ret.reshape(-1)]
    sizes = [int(a.shape[0]) for a in summed]
    flat = jnp.concatenate(summed + [dmod_lat.reshape(-1)])
    n_sum = sum(sizes)
    n_sum_rows = -(-n_sum // D)
    lat_rows = (L * B * 6 * D) // D
    pack_rows = -(-(n_sum_rows + lat_rows) // 8) * 8
    packed = jnp.zeros((pack_rows * D,), F32).at[:n_sum].set(flat[:n_sum])
    packed = packed.at[n_sum_rows * D:(n_sum_rows + lat_rows) * D].set(flat[n_sum:]).reshape(pack_rows, D)
    g2 = _allgather8("gather_small", packed).reshape(N_DEV, pack_rows, D)
    tot = _sum_leading("sum_small", g2[:, :n_sum_rows]).reshape(-1)
    pieces, off = [], 0
    for sz in sizes:
        pieces.append(tot[off:off + sz])
        off += sz
    dmod_ctx_t, g_n1, g_n2, g_vg, g_ws, g_bs, g_qg, g_kg, g_rd = pieces
    dmod_ctx_t = dmod_ctx_t.reshape(L, 6 * D)
    dmod_lat_all = g2[:, n_sum_rows:n_sum_rows + lat_rows].reshape(N_DEV, L, B, 6 * D)
    dmod_rows = jnp.zeros((L, MOD_ROWS, 6 * D), F32)
    dmod_rows = dmod_rows.at[:, :n_ex].set(jnp.transpose(dmod_lat_all, (1, 0, 2, 3)).reshape(L, n_ex, 6 * D))
    dmod_rows = dmod_rows.at[:, n_ex].set(dmod_ctx_t)
    g_mod_b = _sum_leading("sum_mod_b", jnp.transpose(dmod_rows, (1, 0, 2)))
    dmod_mine = lax.dynamic_slice_in_dim(dmod_rows, chip * mcols, mcols, axis=2)
    g_mod_w = jnp.stack([_mm(f"mod_dw_{l}", act, dmod_mine[l], mode="tn", outs=[F32], tn=mcols) for l in range(L)])
    ctx8 = jnp.zeros((L, 8, mcols), F32).at[:, 0].set(dmod_mine[:, n_ex])
    dcc = [_mm(f"mod_dctx_{l}", ctx8[l], mod_w, mode="nt", layer=l, outs=[F32], tk=mcols) for l in range(L)]
    dcc = _sum_leading("sum_dctx_layers", jnp.stack(dcc))
    g3 = _allgather8("gather_dctx", dcc).reshape(N_DEV, 8, D)
    dcc_t = _sum_leading("sum_dctx_chips", g3[0::2])[0:1]
    g_c_ctx = _silu_bwd_rows("silu_bwd_cctx", c_ctx[None], dcc_t)[0]

    vg_mine = lax.dynamic_slice_in_dim(g_vg.reshape(2, -1), chip * vgw, vgw, axis=1)
    small_g = dict(c_ctx=g_c_ctx, mod_w=g_mod_w, mod_b=g_mod_b, norm1_g=g_n1.reshape(norm1_g.shape),
                   norm2_g=g_n2.reshape(norm2_g.shape), ret_decay=g_rd.reshape(ret_decay.shape),
                   att_q_norm_g=g_qg.reshape(att_q_norm_g.shape), att_k_norm_g=g_kg.reshape(att_k_norm_g.shape),
                   cm_v_norm_g=vg_mine, cm_w_s=g_ws.reshape(cm_w_s.shape), cm_b_s=g_bs.reshape(cm_b_s.shape))
    for n, g in small_g.items():
        out[n] = _adamw(f"adamw_{n}", weights[n], [g], m_in[n], v_in[n])

    order = list(weights)
    return (loss, grad_x, *[out[n][0] for n in order], *[out[n][1] for n in order],
            *[out[n][2] for n in order], *[out[n][3] for n in order])
```

```python
import functools
import math
from typing import NamedTuple

import jax
import jax.numpy as jnp
from jax import lax
from jax.experimental import pallas as pl
from jax.experimental.pallas import tpu as pltpu

F32 = jnp.float32
BF16 = jnp.bfloat16
EPS = 1e-6
ROPE_BASE = 10000.0
LANES = 128
CHUNK = 128
N_LAYERS = 4
VMEM_LIMIT = 56 * 1024 * 1024

ADAM_LR = 0.001
ADAM_B1 = 0.9
ADAM_B2 = 0.999
ADAM_EPS = 1e-08
ADAM_WD = 0.01
ADAM_STEP = 10


class Cfg(NamedTuple):
    B: int = 4
    SC: int = 256
    SL: int = 2048
    D: int = 1024
    FF: int = 4096
    GRID_W: int = 64
    H: int = 4
    KV: int = 2
    CMW: int = 1024
    CMG: int = 8

    @property
    def S(self):
        return self.SC + self.SL

    @property
    def T(self):
        return self.B * self.S

    @property
    def TM(self):
        return self.SC

    @property
    def TPE(self):
        return self.S // self.SC

    @property
    def ABW(self):
        return (5 * self.H + 2 * self.KV) * CHUNK


def _tile(dim, pref):
    t = min(dim, pref)
    while dim % t:
        t -= LANES
    return t


def _dot(a, b):
    return lax.dot_general(a, b, (((1,), (0,)), ((), ())), preferred_element_type=F32)


def _dot_nt(a, b):
    return lax.dot_general(a, b, (((1,), (1,)), ((), ())), preferred_element_type=F32)


def _dot_tn(a, b):
    return lax.dot_general(a, b, (((0,), (0,)), ((), ())), preferred_element_type=F32)


def _params(sem, vmem=VMEM_LIMIT):
    return pltpu.CompilerParams(dimension_semantics=sem, vmem_limit_bytes=vmem)


def _mod_index(cfg):
    tpe = cfg.TPE
    return lambda i: (i // tpe, jnp.minimum(i % tpe, 1), 0, 0)


def _mm(name, a, b, *, mode, outs, tm=1024, tn=1024, tk=1024, layer=None, epi=None, extras=(), carry=None):
    bshape = b.shape[1:] if layer is not None else b.shape
    if mode == "nn":
        (M, K), N = a.shape, bshape[1]
    elif mode == "nt":
        (M, K), N = a.shape, bshape[0]
    else:
        (K, M), N = a.shape, bshape[1]
    tm, tn, tk = _tile(M, tm), _tile(N, tn), _tile(K, tk)
    nk = K // tk
    a_spec = (pl.BlockSpec((tk, tm), lambda i, j, k: (k, i)) if mode == "tn"
              else pl.BlockSpec((tm, tk), lambda i, j, k: (i, k)))
    if mode == "nt":
        bblk, bidx = (tn, tk), (lambda i, j, k: (j, k))
    else:
        bblk, bidx = (tk, tn), (lambda i, j, k: (k, j))
    if layer is not None:
        b_spec = pl.BlockSpec((None,) + bblk, lambda i, j, k: (layer,) + bidx(i, j, k))
    else:
        b_spec = pl.BlockSpec(bblk, bidx)
    ne, no = len(extras), len(outs)
    nc = len(carry.srcs) if carry is not None else 0
    dot = {"nn": _dot, "nt": _dot_nt, "tn": _dot_tn}[mode]
    grid = (M // tm, N // tn, nk)

    def body(*refs):
        a_ref, b_ref = refs[0], refs[1]
        ex, out_refs = refs[2:2 + ne], refs[2 + ne + nc:2 + ne + nc + no]
        row_tile = pl.program_id(0)

        if nc:
            step = (pl.program_id(0) * grid[1] + pl.program_id(1)) * grid[2] + pl.program_id(2)
            c_src = refs[2 + ne:2 + ne + nc]
            c_dst = refs[2 + ne + nc + no:2 + ne + 2 * nc + no]
            sems = refs[2 + ne + 2 * nc + no:2 + ne + 2 * nc + no + 3]

            @pl.when(step == 0)
            def _():
                for cp in _carry_copies(carry, c_src, c_dst, *sems):
                    cp.start()

        def finish(acc):
            res = epi(acc, row_tile, *ex) if epi is not None else (acc,)
            for r, o in zip(res, out_refs):
                o[...] = r.astype(o.dtype)

        part = dot(a_ref[...].astype(BF16), b_ref[...].astype(BF16))
        if nk == 1:
            finish(part)
        else:
            acc_ref = refs[-1]
            k = pl.program_id(2)

            @pl.when(k == 0)
            def _():
                acc_ref[...] = part

            @pl.when(k > 0)
            def _():
                acc_ref[...] += part

            @pl.when(k == nk - 1)
            def _():
                finish(acc_ref[...])

        if nc:
            @pl.when(step == grid[0] * grid[1] * grid[2] - 1)
            def _():
                for cp in _carry_copies(carry, c_src, c_dst, *sems):
                    cp.wait()

    scratch = [pltpu.SemaphoreType.DMA((3 * nc,)), pltpu.SemaphoreType.DMA((3 * nc,)),
               pltpu.SemaphoreType.DMA((nc,))] if nc else []
    if nk > 1:
        scratch.append(pltpu.VMEM((tm, tn), F32))
    res = pl.pallas_call(
        body, name=name, grid=grid,
        in_specs=[a_spec, b_spec] + [s for _, s in extras] + [ANY] * nc,
        out_specs=[pl.BlockSpec((tm, tn), lambda i, j, k: (i, j)) for _ in outs] + [ANY] * nc,
        out_shape=[jax.ShapeDtypeStruct((M, N), d) for d in outs] + (_carry_out_shapes(carry) if nc else []),
        scratch_shapes=scratch,
        compiler_params=_params(("arbitrary",) * 3 if nc else ("parallel", "parallel", "arbitrary")),
    )(a, b, *[x for x, _ in extras], *(carry.srcs if nc else ()))
    if nc:
        return (res[0] if no == 1 else res[:no]), res[no:]
    return res[0] if no == 1 else res


def _norm_mod_fwd(cfg, name, x, gain, mod, ish, isc):
    T, D, TM = cfg.T, cfg.D, cfg.TM

    def body(x_ref, g_ref, mod_ref, h_ref):
        x = x_ref[...]
        rstd = lax.rsqrt(jnp.mean(x * x, axis=-1, keepdims=True) + EPS)
        n = x * rstd * g_ref[...]
        h = n * (1.0 + mod_ref[pl.ds(isc, 1), :]) + mod_ref[pl.ds(ish, 1), :]
        h_ref[...] = h.astype(BF16)

    return pl.pallas_call(
        body, name=name, grid=(T // TM,),
        in_specs=[pl.BlockSpec((TM, D), lambda i: (i, 0)), pl.BlockSpec((1, D), lambda i: (0, 0)),
                  pl.BlockSpec((None, None, 6, D), _mod_index(cfg))],
        out_specs=pl.BlockSpec((TM, D), lambda i: (i, 0)),
        out_shape=jax.ShapeDtypeStruct((T, D), BF16),
        compiler_params=_params(("parallel",)),
    )(x, gain, mod)


def _norm_mod_bwd(cfg, name, x, gain, mod, ish, isc, dh, dres):
    T, D, TM, TPE = cfg.T, cfg.D, cfg.TM, cfg.TPE

    def body(x_ref, g_ref, mod_ref, dh_ref, dres_ref, dx_ref, dmod_ref, dgain_ref):
        i = pl.program_id(0)
        t = i % TPE
        x = x_ref[...]
        g = g_ref[...]
        dh = dh_ref[...].astype(F32)
        rstd = lax.rsqrt(jnp.mean(x * x, axis=-1, keepdims=True) + EPS)
        xhat = x * rstd
        dn = dh * (1.0 + mod_ref[pl.ds(isc, 1), :])
        dsh = jnp.sum(dh, axis=0, keepdims=True)
        dsc = jnp.sum(dh * (xhat * g), axis=0, keepdims=True)
        dgain = jnp.sum(dn * xhat, axis=0, keepdims=True)
        dxh = dn * g
        dx = rstd * (dxh - xhat * jnp.mean(dxh * xhat, axis=-1, keepdims=True))
        dx_ref[...] = dx + dres_ref[...]

        @pl.when(t <= 1)
        def _():
            dmod_ref[pl.ds(0, 1), :] = dsh
            dmod_ref[pl.ds(1, 1), :] = dsc

        @pl.when(t > 1)
        def _():
            dmod_ref[pl.ds(0, 1), :] += dsh
            dmod_ref[pl.ds(1, 1), :] += dsc

        @pl.when(i == 0)
        def _():
            dgain_ref[...] = dgain

        @pl.when(i > 0)
        def _():
            dgain_ref[...] += dgain

    tok = pl.BlockSpec((TM, D), lambda i: (i, 0))
    return pl.pallas_call(
        body, name=name, grid=(T // TM,),
        in_specs=[tok, pl.BlockSpec((1, D), lambda i: (0, 0)), pl.BlockSpec((None, None, 6, D), _mod_index(cfg)),
                  tok, tok],
        out_specs=[tok, pl.BlockSpec((None, None, 2, D), _mod_index(cfg)), pl.BlockSpec((1, D), lambda i: (0, 0))],
        out_shape=[jax.ShapeDtypeStruct((T, D), F32), jax.ShapeDtypeStruct((cfg.B, 2, 2, D), F32),
                   jax.ShapeDtypeStruct((1, D), F32)],
        compiler_params=_params(("arbitrary",)),
    )(x, gain, mod, dh, dres)


def _gate_bwd(cfg, name, dx, y, mod, igate):
    T, D, TM, TPE = cfg.T, cfg.D, cfg.TM, cfg.TPE

    def body(dx_ref, y_ref, mod_ref, dy_ref, dg_ref):
        t = pl.program_id(0) % TPE
        dx = dx_ref[...]
        dy_ref[...] = (dx * mod_ref[pl.ds(igate, 1), :]).astype(BF16)
        dg = jnp.sum(dx * y_ref[...].astype(F32), axis=0, keepdims=True)

        @pl.when(t <= 1)
        def _():
            dg_ref[...] = dg

        @pl.when(t > 1)
        def _():
            dg_ref[...] += dg

    tok = pl.BlockSpec((TM, D), lambda i: (i, 0))
    return pl.pallas_call(
        body, name=name, grid=(T // TM,),
        in_specs=[tok, tok, pl.BlockSpec((None, None, 6, D), _mod_index(cfg))],
        out_specs=[tok, pl.BlockSpec((None, None, 1, D), _mod_index(cfg))],
        out_shape=[jax.ShapeDtypeStruct((T, D), BF16), jax.ShapeDtypeStruct((cfg.B, 2, 1, D), F32)],
        compiler_params=_params(("arbitrary",)),
    )(dx, y, mod)


def _loss_grad(cfg, x, tgt):
    T, D, TM, TPE = cfg.T, cfg.D, cfg.TM, cfg.TPE

    def body(x_ref, t_ref, dx_ref, loss_ref):
        i = pl.program_id(0)
        t = i % TPE

        @pl.when(i == 0)
        def _():
            loss_ref[...] = jnp.zeros_like(loss_ref)

        @pl.when(t == 0)
        def _():
            dx_ref[...] = jnp.zeros_like(dx_ref)

        @pl.when(t > 0)
        def _():
            err = x_ref[...] - t_ref[...]
            dx_ref[...] = err * (1.0 / D)
            loss_ref[...] += 0.5 * jnp.sum(jnp.mean(err * err, axis=-1, keepdims=True), axis=0, keepdims=True)

    tok = pl.BlockSpec((TM, D), lambda i: (i, 0))
    tgt_spec = pl.BlockSpec((TM, D), lambda i: ((i // TPE) * (TPE - 1) + jnp.maximum(i % TPE - 1, 0), 0))
    dx, loss = pl.pallas_call(
        body, name="loss_grad", grid=(T // TM,),
        in_specs=[tok, tgt_spec], out_specs=[tok, pl.BlockSpec((8, LANES), lambda i: (0, 0))],
        out_shape=[jax.ShapeDtypeStruct((T, D), F32), jax.ShapeDtypeStruct((8, LANES), F32)],
        compiler_params=_params(("arbitrary",)),
    )(x, tgt)
    return loss[0, 0], dx


def _rope_tables(cfg):
    rows = cfg.SL // cfg.GRID_W
    row = jnp.repeat(jnp.arange(rows, dtype=F32), cfg.GRID_W)
    col = jnp.tile(jnp.arange(cfg.GRID_W, dtype=F32), rows)
    n_freq = CHUNK // 4
    inv = ROPE_BASE ** (-jnp.arange(n_freq, dtype=F32) / n_freq)
    ang = jnp.concatenate([row[:, None] * inv[None, :], col[:, None] * inv[None, :]], axis=-1)
    cos, sin = jnp.cos(ang), jnp.sin(ang)
    cosf = jnp.concatenate([jnp.ones((cfg.SC, CHUNK), F32), jnp.concatenate([cos, cos], axis=-1)], axis=0)
    sinf = jnp.concatenate([jnp.zeros((cfg.SC, CHUNK), F32), jnp.concatenate([-sin, sin], axis=-1)], axis=0)
    return cosf, sinf


def _rope(x, cosf, sinf):
    return x * cosf + pltpu.roll(x, CHUNK // 2, 1) * sinf


def _irope(dy, cosf, sinf):
    return dy * cosf - pltpu.roll(dy, CHUNK // 2, 1) * sinf


def _prep_fwd(cfg, name, p, cosf, sinf, qg, kg):
    T, TM, TPE, H, KV = cfg.T, cfg.TM, cfg.TPE, cfg.H, cfg.KV
    HW = H * CHUNK
    kscale = CHUNK ** -0.5

    def body(p_ref, c_ref, s_ref, qg_ref, kg_ref, rq_ref, rk_ref, aq_ref, ak_ref):
        cosf, sinf = c_ref[...], s_ref[...]

        def normed(x, g):
            return x * lax.rsqrt(jnp.mean(x * x, axis=-1, keepdims=True) + EPS) * g

        for h in range(H):
            sl = pl.ds(h * CHUNK, CHUNK)
            rq_ref[:, sl] = _rope(p_ref[:, pl.ds(h * CHUNK, CHUNK)], cosf, sinf)
            rk_ref[:, sl] = _rope(p_ref[:, pl.ds(HW + h * CHUNK, CHUNK)], cosf, sinf) * kscale
            aq_ref[:, sl] = _rope(normed(p_ref[:, pl.ds(4 * HW + h * CHUNK, CHUNK)], qg_ref[...]),
                                  cosf, sinf).astype(BF16)
        for h in range(KV):
            ak_ref[:, pl.ds(h * CHUNK, CHUNK)] = _rope(
                normed(p_ref[:, pl.ds(5 * HW + h * CHUNK, CHUNK)], kg_ref[...]), cosf, sinf).astype(BF16)

    tab = pl.BlockSpec((TM, CHUNK), lambda i: (i % TPE, 0))
    vec = pl.BlockSpec((1, CHUNK), lambda i: (0, 0))
    return pl.pallas_call(
        body, name=name, grid=(T // TM,),
        in_specs=[pl.BlockSpec((TM, cfg.ABW), lambda i: (i, 0)), tab, tab, vec, vec],
        out_specs=[pl.BlockSpec((TM, HW), lambda i: (i, 0))] * 3 + [pl.BlockSpec((TM, KV * CHUNK), lambda i: (i, 0))],
        out_shape=[jax.ShapeDtypeStruct((T, HW), F32), jax.ShapeDtypeStruct((T, HW), F32),
                   jax.ShapeDtypeStruct((T, HW), BF16), jax.ShapeDtypeStruct((T, KV * CHUNK), BF16)],
        compiler_params=_params(("parallel",)),
    )(p, cosf, sinf, qg, kg)


def _prep_bwd(cfg, name, p, cosf, sinf, qg, kg, d_rq, d_rk, d_rv, d_gate, d_aq, d_ak, d_av):
    T, TM, TPE, H, KV = cfg.T, cfg.TM, cfg.TPE, cfg.H, cfg.KV
    HW = H * CHUNK
    kscale = CHUNK ** -0.5

    def body(p_ref, c_ref, s_ref, qg_ref, kg_ref, drq_ref, drk_ref, drv_ref, dgt_ref, daq_ref, dak_ref, dav_ref,
             dp_ref, dqg_ref, dkg_ref):
        i = pl.program_id(0)
        cosf, sinf = c_ref[...], s_ref[...]

        def norm_bwd(x, g, dn):
            rstd = lax.rsqrt(jnp.mean(x * x, axis=-1, keepdims=True) + EPS)
            xhat = x * rstd
            dg = jnp.sum(dn * xhat, axis=0, keepdims=True)
            dxh = dn * g
            return rstd * (dxh - xhat * jnp.mean(dxh * xhat, axis=-1, keepdims=True)), dg

        dqg = jnp.zeros((1, CHUNK), F32)
        dkg = jnp.zeros((1, CHUNK), F32)
        for h in range(H):
            sl = pl.ds(h * CHUNK, CHUNK)
            dp_ref[:, pl.ds(h * CHUNK, CHUNK)] = _irope(drq_ref[:, sl], cosf, sinf).astype(BF16)
            dp_ref[:, pl.ds(HW + h * CHUNK, CHUNK)] = (_irope(drk_ref[:, sl], cosf, sinf) * kscale).astype(BF16)
            dp_ref[:, pl.ds(2 * HW + h * CHUNK, CHUNK)] = drv_ref[:, sl].astype(BF16)
            dp_ref[:, pl.ds(3 * HW + h * CHUNK, CHUNK)] = dgt_ref[:, sl].astype(BF16)
            dx, dg = norm_bwd(p_ref[:, pl.ds(4 * HW + h * CHUNK, CHUNK)], qg_ref[...],
                              _irope(daq_ref[:, sl], cosf, sinf))
            dp_ref[:, pl.ds(4 * HW + h * CHUNK, CHUNK)] = dx.astype(BF16)
            dqg = dqg + dg
        for h in range(KV):
            sl = pl.ds(h * CHUNK, CHUNK)
            dx, dg = norm_bwd(p_ref[:, pl.ds(5 * HW + h * CHUNK, CHUNK)], kg_ref[...],
                              _irope(dak_ref[:, sl], cosf, sinf))
            dp_ref[:, pl.ds(5 * HW + h * CHUNK, CHUNK)] = dx.astype(BF16)
            dp_ref[:, pl.ds(5 * HW + (KV + h) * CHUNK, CHUNK)] = dav_ref[:, sl].astype(BF16)
            dkg = dkg + dg

        @pl.when(i == 0)
        def _():
            dqg_ref[...] = dqg
            dkg_ref[...] = dkg

        @pl.when(i > 0)
        def _():
            dqg_ref[...] += dqg
            dkg_ref[...] += dkg

    tab = pl.BlockSpec((TM, CHUNK), lambda i: (i % TPE, 0))
    vec = pl.BlockSpec((1, CHUNK), lambda i: (0, 0))
    hw = pl.BlockSpec((TM, HW), lambda i: (i, 0))
    kvw = pl.BlockSpec((TM, KV * CHUNK), lambda i: (i, 0))
    return pl.pallas_call(
        body, name=name, grid=(T // TM,),
        in_specs=[pl.BlockSpec((TM, cfg.ABW), lambda i: (i, 0)), tab, tab, vec, vec, hw, hw, hw, hw, hw, kvw, kvw],
        out_specs=[pl.BlockSpec((TM, cfg.ABW), lambda i: (i, 0)), vec, vec],
        out_shape=[jax.ShapeDtypeStruct((T, cfg.ABW), BF16), jax.ShapeDtypeStruct((1, CHUNK), F32),
                   jax.ShapeDtypeStruct((1, CHUNK), F32)],
        compiler_params=_params(("arbitrary",)),
    )(p, cosf, sinf, qg, kg, d_rq, d_rk, d_rv, d_gate, d_aq, d_ak, d_av)


def _ret_consts(direction, lg):
    C = CHUNK
    ii = lax.broadcasted_iota(jnp.int32, (C, C), 0)
    jj = lax.broadcasted_iota(jnp.int32, (C, C), 1)
    col = lax.broadcasted_iota(jnp.int32, (C, 1), 0).astype(F32)
    if direction == 0:
        mask, er, ek, eq = ii >= jj, (ii - jj).astype(F32), (C - 1.0) - col, col + 1.0
    else:
        mask, er, ek, eq = jj >= ii, (jj - ii).astype(F32), col, C - col
    er = jnp.where(mask, er, 0.0)
    dm = jnp.where(mask, jnp.exp(er * lg), 0.0)
    return dm, er, jnp.exp(ek * lg), ek, jnp.exp(eq * lg), eq, jnp.exp(C * lg)


def _ret_order(cfg, direction):
    n_all, n_ctx = cfg.S // CHUNK, cfg.SC // CHUNK
    if direction == 0:
        return list(range(n_all))
    return list(range(n_ctx - 1, -1, -1)) + list(range(n_all - 1, n_ctx - 1, -1))


def _carry_begin(carry, c_src, c_dst, sems, step):
    @pl.when(step == 0)
    def _():
        for cp in _carry_copies(carry, c_src, c_dst, *sems):
            cp.start()


def _carry_end(carry, c_src, c_dst, sems, step, n_steps):
    @pl.when(step == n_steps - 1)
    def _():
        for cp in _carry_copies(carry, c_src, c_dst, *sems):
            cp.wait()


def _carry_scratch(nc):
    return [pltpu.SemaphoreType.DMA((3 * nc,)), pltpu.SemaphoreType.DMA((3 * nc,)),
            pltpu.SemaphoreType.DMA((nc,))] if nc else []


def _head_norm_gate(o, g):
    mu = jnp.mean(o, axis=-1, keepdims=True)
    var = jnp.mean(jnp.square(o - mu), axis=-1, keepdims=True)
    rstd = lax.rsqrt(var + EPS)
    y = (o - mu) * rstd
    sg = jax.nn.sigmoid(g)
    return y, rstd, sg


RET_UNROLL = 2


def _retention_fwd(cfg, name, rq, rk, p, lgb, carry=None):
    B, H, S, T = cfg.B, cfg.H, cfg.S, cfg.T
    n_all = S // CHUNK
    nc = len(carry.srcs) if carry is not None else 0

    def body(*refs):
        q_ref, k_ref, v_ref, g_ref, lg_ref = refs[:5]
        c_src = refs[5:5 + nc]
        o_ref, ret_ref, st_ref = refs[5 + nc:8 + nc]
        c_dst = refs[8 + nc:8 + 2 * nc]
        sems = refs[8 + 2 * nc:8 + 2 * nc + 3] if nc else ()
        kv_ref = refs[-1]
        step = pl.program_id(0) * H + pl.program_id(1)
        if nc:
            _carry_begin(carry, c_src, c_dst, sems, step)

        def rows(n):
            return pl.ds(pl.multiple_of(n * CHUNK, CHUNK), CHUNK)

        for direction in (0, 1):
            dm, _, kd, _, qd, _, cd = _ret_consts(direction, lg_ref[direction, 0:1, 0:1])

            def kv_step(n, c):
                kv_ref[n] = _dot_tn((k_ref[rows(n), :] * kd).astype(BF16), v_ref[rows(n), :].astype(BF16))
                return c

            lax.fori_loop(0, n_all, kv_step, 0, unroll=RET_UNROLL)
            order = _ret_order(cfg, direction)
            st = jnp.zeros((CHUNK, CHUNK), F32)
            for t, n in enumerate(order):
                st_ref[direction, n] = st
                if t + 1 < n_all:
                    st = cd * st + kv_ref[n]

            def out_step(n, c):
                q = q_ref[rows(n), :].astype(BF16)
                v = v_ref[rows(n), :].astype(BF16)
                s = _dot_nt(q, k_ref[rows(n), :].astype(BF16)) * dm
                o = _dot(s.astype(BF16), v) + _dot(q, st_ref[direction, n].astype(BF16)) * qd
                if direction == 0:
                    o_ref[rows(n), :] = o
                else:
                    o_ref[rows(n), :] += o
                return c

            lax.fori_loop(0, n_all, out_step, 0, unroll=RET_UNROLL)

        def gate_step(n, c):
            g = g_ref[rows(n), :]
            y, _, sg = _head_norm_gate(o_ref[rows(n), :], g)
            ret_ref[rows(n), :] = (y * (g * sg)).astype(BF16)
            return c

        lax.fori_loop(0, n_all, gate_step, 0, unroll=RET_UNROLL)
        if nc:
            _carry_end(carry, c_src, c_dst, sems, step, B * H)

    HW = H * CHUNK
    blk = lambda off: pl.BlockSpec((S, CHUNK), lambda b, h: (b, off + h))
    st_spec = pl.BlockSpec((None, None, 2, n_all, CHUNK, CHUNK), lambda b, h: (b, h, 0, 0, 0, 0))
    res = pl.pallas_call(
        body, name=name, grid=(B, H),
        in_specs=[blk(0), blk(0), blk(2 * H), blk(3 * H),
                  pl.BlockSpec((None, 2, 8, LANES), lambda b, h: (h, 0, 0, 0))] + [ANY] * nc,
        out_specs=[blk(0), blk(0), st_spec] + [ANY] * nc,
        out_shape=[jax.ShapeDtypeStruct((T, HW), F32), jax.ShapeDtypeStruct((T, HW), BF16),
                   jax.ShapeDtypeStruct((B, H, 2, n_all, CHUNK, CHUNK), F32)] + (_carry_out_shapes(carry) if nc else []),
        scratch_shapes=_carry_scratch(nc) + [pltpu.VMEM((n_all, CHUNK, CHUNK), F32)],
        compiler_params=_params(("arbitrary", "arbitrary") if nc else ("parallel", "parallel")),
    )(rq, rk, p, p, lgb, *(carry.srcs if nc else ()))
    return res[:3], res[3:]


def _retention_bwd(cfg, name, rq, rk, p, o_sum, states, dcat, lgb, carry=None):
    B, H, S, T = cfg.B, cfg.H, cfg.S, cfg.T
    n_all = S // CHUNK
    C = CHUNK
    nc = len(carry.srcs) if carry is not None else 0

    def body(*refs):
        q_ref, k_ref, v_ref, g_ref, o_ref, st_ref, dr_ref, lg_ref = refs[:8]
        c_src = refs[8:8 + nc]
        dq_ref, dk_ref, dv_ref, dg_ref, dlg_ref = refs[8 + nc:13 + nc]
        c_dst = refs[13 + nc:13 + 2 * nc]
        sems = refs[13 + 2 * nc:13 + 2 * nc + 3] if nc else ()
        do_ref, gq_ref, ds_ref, acc_ref = refs[-4:]
        step = pl.program_id(0) * H + pl.program_id(1)
        if nc:
            _carry_begin(carry, c_src, c_dst, sems, step)

        def rows(n):
            return pl.ds(pl.multiple_of(n * C, C), C)

        def gate_step(n, c):
            g = g_ref[rows(n), :]
            dr = dr_ref[rows(n), :]
            y, rstd, sg = _head_norm_gate(o_ref[rows(n), :], g)
            dy = dr * (g * sg)
            dg_ref[rows(n), :] = dr * y * (sg * (1.0 + g * (1.0 - sg)))
            do_ref[rows(n), :] = rstd * (dy - jnp.mean(dy, axis=-1, keepdims=True)
                                         - y * jnp.mean(dy * y, axis=-1, keepdims=True))
            return c

        lax.fori_loop(0, n_all, gate_step, 0, unroll=RET_UNROLL)

        for direction in (0, 1):
            dm, er, kd, ek, qd, eq, cd = _ret_consts(direction, lg_ref[direction, 0:1, 0:1])

            def gq_step(n, c):
                gq_ref[n] = _dot_tn(q_ref[rows(n), :].astype(BF16), (do_ref[rows(n), :] * qd).astype(BF16))
                return c

            lax.fori_loop(0, n_all, gq_step, 0, unroll=RET_UNROLL)
            order = _ret_order(cfg, direction)
            ds = jnp.zeros((C, C), F32)
            for t in reversed(range(n_all)):
                ds_ref[order[t]] = ds
                if t > 0:
                    ds = cd * ds + gq_ref[order[t]]
            acc_ref[...] = jnp.zeros_like(acc_ref)

            def chunk_step(n, c):
                q = q_ref[rows(n), :].astype(BF16)
                kf = k_ref[rows(n), :]
                k = kf.astype(BF16)
                v = v_ref[rows(n), :].astype(BF16)
                do = do_ref[rows(n), :]
                dob = do.astype(BF16)
                sp = st_ref[direction, n]
                spb = sp.astype(BF16)
                ds = ds_ref[n]
                dsb = ds.astype(BF16)
                dk_state = _dot_nt(v, dsb) * kd
                dv = _dot((kf * kd).astype(BF16), dsb)
                pm = _dot_nt(q, k) * dm
                dpm = _dot_nt(dob, v)
                dsr = (dpm * dm).astype(BF16)
                dq = _dot(dsr, k)
                dk = _dot_tn(dsr, q) + dk_state
                dv = dv + _dot_tn(pm.astype(BF16), dob)
                doq = do * qd
                qs = _dot(q, spb)
                dq = dq + _dot_nt(doq.astype(BF16), spb)
                acc_ref[...] += (jnp.sum(dpm * pm * er, axis=0, keepdims=True)
                                 + jnp.sum(eq * doq * qs, axis=0, keepdims=True)
                                 + jnp.sum(ek * kf * dk_state, axis=0, keepdims=True)
                                 + (C * cd) * jnp.sum(ds * sp, axis=0, keepdims=True))
                if direction == 0:
                    dq_ref[rows(n), :] = dq
                    dk_ref[rows(n), :] = dk
                    dv_ref[rows(n), :] = dv
                else:
                    dq_ref[rows(n), :] += dq
                    dk_ref[rows(n), :] += dk
                    dv_ref[rows(n), :] += dv
                return c

            lax.fori_loop(0, n_all, chunk_step, 0, unroll=RET_UNROLL)
            dlg_ref[direction] = jnp.broadcast_to(jnp.sum(acc_ref[...], axis=1, keepdims=True), (8, LANES))
        if nc:
            _carry_end(carry, c_src, c_dst, sems, step, B * H)

    HW = H * CHUNK
    blk = lambda off: pl.BlockSpec((S, CHUNK), lambda b, h: (b, off + h))
    st_spec = pl.BlockSpec((None, None, 2, n_all, C, C), lambda b, h: (b, h, 0, 0, 0, 0))
    res = pl.pallas_call(
        body, name=name, grid=(B, H),
        in_specs=[blk(0), blk(0), blk(2 * H), blk(3 * H), blk(0), st_spec, blk(0),
                  pl.BlockSpec((None, 2, 8, LANES), lambda b, h: (h, 0, 0, 0))] + [ANY] * nc,
        out_specs=[blk(0)] * 4 + [pl.BlockSpec((None, None, 2, 8, LANES), lambda b, h: (b, h, 0, 0, 0))] + [ANY] * nc,
        out_shape=[jax.ShapeDtypeStruct((T, HW), F32)] * 4 + [jax.ShapeDtypeStruct((B, H, 2, 8, LANES), F32)]
        + (_carry_out_shapes(carry) if nc else []),
        scratch_shapes=_carry_scratch(nc) + [pltpu.VMEM((S, CHUNK), F32), pltpu.VMEM((n_all, C, C), F32),
                                             pltpu.VMEM((n_all, C, C), F32), pltpu.VMEM((1, C), F32)],
        compiler_params=_params(("arbitrary", "arbitrary") if nc else ("parallel", "parallel")),
    )(rq, rk, p, p, o_sum, states, dcat, lgb, *(carry.srcs if nc else ()))
    return res[:5], res[5:]


def _attn_scores(cfg, q, k, t):
    kcol = lax.broadcasted_iota(jnp.int32, (1, cfg.S), 1)
    bias = jnp.where(jnp.logical_or(t > 0, kcol < cfg.SC), 0.0, -1e30)
    s = _dot_nt(q, k) * (CHUNK ** -0.5) + bias
    e = jnp.exp(s - jnp.max(s, axis=-1, keepdims=True))
    return e, 1.0 / jnp.sum(e, axis=-1, keepdims=True)


def _attention_fwd(cfg, name, aq, ak, p, carry=None):
    B, H, KV, S, T, TM, TPE = cfg.B, cfg.H, cfg.KV, cfg.S, cfg.T, cfg.TM, cfg.TPE
    G = H // KV
    v_off = (5 * H + KV)
    nc = len(carry.srcs) if carry is not None else 0

    def body(*refs):
        q_ref, k_ref, v_ref = refs[:3]
        o_ref = refs[3 + nc]
        c_src, c_dst, sems = refs[3:3 + nc], refs[4 + nc:4 + 2 * nc], refs[4 + 2 * nc:]
        step = (pl.program_id(0) * H + pl.program_id(1)) * TPE + pl.program_id(2)
        if nc:
            _carry_begin(carry, c_src, c_dst, sems, step)
        e, inv = _attn_scores(cfg, q_ref[...], k_ref[...], pl.program_id(2))
        o_ref[...] = (_dot(e.astype(BF16), v_ref[...].astype(BF16)) * inv).astype(BF16)
        if nc:
            _carry_end(carry, c_src, c_dst, sems, step, B * H * TPE)

    res = pl.pallas_call(
        body, name=name, grid=(B, H, TPE),
        in_specs=[pl.BlockSpec((TM, CHUNK), lambda b, h, t: (b * TPE + t, h)),
                  pl.BlockSpec((S, CHUNK), lambda b, h, t: (b, h // G)),
                  pl.BlockSpec((S, CHUNK), lambda b, h, t: (b, v_off + h // G))] + [ANY] * nc,
        out_specs=[pl.BlockSpec((TM, CHUNK), lambda b, h, t: (b * TPE + t, h))] + [ANY] * nc,
        out_shape=[jax.ShapeDtypeStruct((T, H * CHUNK), BF16)] + (_carry_out_shapes(carry) if nc else []),
        scratch_shapes=_carry_scratch(nc),
        compiler_params=_params(("arbitrary",) * 3 if nc else ("parallel",) * 3),
    )(aq, ak, p, *(carry.srcs if nc else ()))
    return res[0], res[1:]


def _attention_bwd(cfg, name, aq, ak, p, dcat, carry=None):
    B, H, KV, S, T, TM, TPE = cfg.B, cfg.H, cfg.KV, cfg.S, cfg.T, cfg.TM, cfg.TPE
    G = H // KV
    v_off = (5 * H + KV)
    nc = len(carry.srcs) if carry is not None else 0

    def body(*refs):
        q_ref, k_ref, v_ref, do_ref = refs[:4]
        dq_ref, dk_ref, dv_ref = refs[4 + nc:7 + nc]
        c_src, c_dst, sems = refs[4:4 + nc], refs[7 + nc:7 + 2 * nc], refs[7 + 2 * nc:]
        g, t = pl.program_id(2), pl.program_id(3)
        step = ((pl.program_id(0) * KV + pl.program_id(1)) * G + g) * TPE + t
        if nc:
            _carry_begin(carry, c_src, c_dst, sems, step)
        q, k = q_ref[...], k_ref[...]
        v = v_ref[...].astype(BF16)
        dob = do_ref[...].astype(BF16)
        e, inv = _attn_scores(cfg, q, k, t)
        pr = e * inv
        dpr = _dot_nt(dob, v)
        ds = (pr * (dpr - jnp.sum(pr * dpr, axis=-1, keepdims=True)) * (CHUNK ** -0.5)).astype(BF16)
        dq_ref[...] = _dot(ds, k)
        dk = _dot_tn(ds, q)
        dv = _dot_tn(pr.astype(BF16), dob)
        first = jnp.logical_and(g == 0, t == 0)

        @pl.when(first)
        def _():
            dk_ref[...] = dk
            dv_ref[...] = dv

        @pl.when(jnp.logical_not(first))
        def _():
            dk_ref[...] += dk
            dv_ref[...] += dv

        if nc:
            _carry_end(carry, c_src, c_dst, sems, step, B * KV * G * TPE)

    qspec = pl.BlockSpec((TM, CHUNK), lambda b, kv, g, t: (b * TPE + t, kv * G + g))
    kvspec = pl.BlockSpec((S, CHUNK), lambda b, kv, g, t: (b, kv))
    res = pl.pallas_call(
        body, name=name, grid=(B, KV, G, TPE),
        in_specs=[qspec, kvspec, pl.BlockSpec((S, CHUNK), lambda b, kv, g, t: (b, v_off + kv)),
                  pl.BlockSpec((TM, CHUNK), lambda b, kv, g, t: (b * TPE + t, H + kv * G + g))] + [ANY] * nc,
        out_specs=[qspec, kvspec, kvspec] + [ANY] * nc,
        out_shape=[jax.ShapeDtypeStruct((T, H * CHUNK), F32), jax.ShapeDtypeStruct((T, KV * CHUNK), F32),
                   jax.ShapeDtypeStruct((T, KV * CHUNK), F32)] + (_carry_out_shapes(carry) if nc else []),
        scratch_shapes=_carry_scratch(nc),
        compiler_params=_params(("arbitrary",) * 4 if nc else ("parallel", "parallel", "arbitrary", "arbitrary")),
    )(aq, ak, p, dcat, *(carry.srcs if nc else ()))
    return res[:3], res[3:]


_GELU_C = math.sqrt(2.0 / math.pi)


def _gelu(x):
    return 0.5 * x * (1.0 + jnp.tanh(_GELU_C * (x + 0.044715 * x * x * x)))


def _gelu_grad(x):
    th = jnp.tanh(_GELU_C * (x + 0.044715 * x * x * x))
    return 0.5 * (1.0 + th) + 0.5 * x * (1.0 - th * th) * _GELU_C * (1.0 + 3.0 * 0.044715 * x * x)


def _cm_fwd(cfg, name, a, vg, ws, bs):
    T, TM, W, NG = cfg.T, cfg.TM, cfg.CMW, cfg.CMG

    def body(a_ref, vg_ref, ws_ref, bs_ref, m_ref):
        v = _gelu(a_ref[:, pl.ds(W, W)])
        vn = (v * lax.rsqrt(jnp.mean(v * v, axis=-1, keepdims=True) + EPS) * vg_ref[...]).astype(BF16)
        for c in range(TM // CHUNK):
            for g in range(NG):
                rows, cols = slice(c * CHUNK, (c + 1) * CHUNK), slice(g * CHUNK, (g + 1) * CHUNK)
                sv = _dot(ws_ref[g].astype(BF16), vn[rows, cols]) + bs_ref[g]
                u = _gelu(a_ref[pl.ds(c * CHUNK, CHUNK), pl.ds(g * CHUNK, CHUNK)])
                m_ref[pl.ds(c * CHUNK, CHUNK), pl.ds(g * CHUNK, CHUNK)] = (u * sv).astype(BF16)

    return pl.pallas_call(
        body, name=name, grid=(T // TM,),
        in_specs=[pl.BlockSpec((TM, 2 * W), lambda i: (i, 0)), pl.BlockSpec((1, W), lambda i: (0, 0)),
                  pl.BlockSpec((NG, CHUNK, CHUNK), lambda i: (0, 0, 0)),
                  pl.BlockSpec((NG, CHUNK, 1), lambda i: (0, 0, 0))],
        out_specs=pl.BlockSpec((TM, W), lambda i: (i, 0)),
        out_shape=jax.ShapeDtypeStruct((T, W), BF16),
        compiler_params=_params(("parallel",)),
    )(a, vg, ws, bs)


def _cm_bwd(cfg, name, a, vg, ws, bs, dm):
    T, TM, W, NG = cfg.T, cfg.TM, cfg.CMW, cfg.CMG

    def body(a_ref, vg_ref, ws_ref, bs_ref, dm_ref, da_ref, dws_ref, dbs_ref, dvg_ref, dvn_ref):
        i = pl.program_id(0)

        @pl.when(i == 0)
        def _():
            dws_ref[...] = jnp.zeros_like(dws_ref)
            dbs_ref[...] = jnp.zeros_like(dbs_ref)
            dvg_ref[...] = jnp.zeros_like(dvg_ref)

        av = a_ref[:, pl.ds(W, W)]
        v = _gelu(av)
        rstd = lax.rsqrt(jnp.mean(v * v, axis=-1, keepdims=True) + EPS)
        xhat = v * rstd
        vg = vg_ref[...]
        vn = (xhat * vg).astype(BF16)
        for c in range(TM // CHUNK):
            for g in range(NG):
                rows, cols = slice(c * CHUNK, (c + 1) * CHUNK), slice(g * CHUNK, (g + 1) * CHUNK)
                rs, cs = pl.ds(c * CHUNK, CHUNK), pl.ds(g * CHUNK, CHUNK)
                wsb = ws_ref[g].astype(BF16)
                blk = vn[rows, cols]
                sv = _dot(wsb, blk) + bs_ref[g]
                au = a_ref[rs, cs]
                dmb = dm_ref[rs, cs]
                da_ref[rs, cs] = (dmb * sv * _gelu_grad(au)).astype(BF16)
                dsv = dmb * _gelu(au)
                dsvb = dsv.astype(BF16)
                dbs_ref[g] += jnp.sum(dsv, axis=1, keepdims=True)
                dws_ref[g] += _dot_nt(dsvb, blk)
                dvn_ref[rs, cs] = _dot_tn(wsb, dsvb)
        dvn = dvn_ref[...]
        dvg_ref[...] += jnp.sum(dvn * xhat, axis=0, keepdims=True)
        dxh = dvn * vg
        dv = rstd * (dxh - xhat * jnp.mean(dxh * xhat, axis=-1, keepdims=True))
        da_ref[:, pl.ds(W, W)] = (dv * _gelu_grad(av)).astype(BF16)

    return pl.pallas_call(
        body, name=name, grid=(T // TM,),
        in_specs=[pl.BlockSpec((TM, 2 * W), lambda i: (i, 0)), pl.BlockSpec((1, W), lambda i: (0, 0)),
                  pl.BlockSpec((NG, CHUNK, CHUNK), lambda i: (0, 0, 0)),
                  pl.BlockSpec((NG, CHUNK, 1), lambda i: (0, 0, 0)), pl.BlockSpec((TM, W), lambda i: (i, 0))],
        out_specs=[pl.BlockSpec((TM, 2 * W), lambda i: (i, 0)), pl.BlockSpec((NG, CHUNK, CHUNK), lambda i: (0, 0, 0)),
                   pl.BlockSpec((NG, CHUNK, 1), lambda i: (0, 0, 0)), pl.BlockSpec((1, W), lambda i: (0, 0))],
        out_shape=[jax.ShapeDtypeStruct((T, 2 * W), BF16), jax.ShapeDtypeStruct((NG, CHUNK, CHUNK), F32),
                   jax.ShapeDtypeStruct((NG, CHUNK, 1), F32), jax.ShapeDtypeStruct((1, W), F32)],
        scratch_shapes=[pltpu.VMEM((TM, W), F32)],
        compiler_params=_params(("arbitrary",)),
    )(a, vg, ws, bs, dm)


def _layer_weights(l):
    mixer = ("ab_w_in", "ab_w_out") if l % 2 == 0 else ("cm_w_in", "cm_w_out")
    return [(mixer[0], l // 2), (mixer[1], l // 2), ("ff_w1", l), ("ff_w2", l)]


def _local_step(cfg, xcat, tgt, mods, shards, w):
    D, TM, H = cfg.D, cfg.TM, cfg.H
    cosf, sinf = _rope_tables(cfg)
    full, big, recv = {}, {}, {}

    def gather_of(keys):
        return _Carry("gather", tuple(shards[n][i] for n, i in keys), tuple(BIG[n] for n, _ in keys))

    def exchange_of(keys):
        return _Carry("exchange", tuple(big[k] for k in keys), tuple(BIG[n] for n, _ in keys))

    def mm(pending, name, a, b, **kw):
        if not pending:
            return _mm(name, a, b, **kw)
        key, carry, sink = pending.pop(0)
        out, (got,) = _mm(name, a, b, carry=carry, **kw)
        sink[key] = got
        return out

    def with_carry(call, keys, make, sink):
        out, got = call(carry=make(keys) if keys else None)
        sink.update(zip(keys, got))
        return out

    keys0 = _layer_weights(0)
    full[keys0[0]], = _comm_call("gather_weights_0", gather_of(keys0[:1]))
    TG = 3 * TM if cfg.TPE % 3 == 0 else TM
    tiles_per_ex = cfg.S // TG
    gate_spec = pl.BlockSpec((None, 2, 6, D), lambda i, j, k: (i // tiles_per_ex, 0, 0, 0))

    def resid_epi(igate):
        def epi(acc, row_tile, x_ref, mod_ref):
            row = lax.broadcasted_iota(jnp.int32, (TG, 1), 0)
            is_ctx = jnp.logical_and(row_tile % tiles_per_ex == 0, row < cfg.SC)
            gate = jnp.where(is_ctx, mod_ref[0, pl.ds(igate, 1), :], mod_ref[1, pl.ds(igate, 1), :])
            return x_ref[...] + gate * acc, acc
        return epi

    def gated_out(pending, name, a, key, x, mod, igate):
        return mm(pending, name, a, full[key], mode="nn", tm=TG, tn=D, outs=[F32, BF16], epi=resid_epi(igate),
                  extras=[(x, pl.BlockSpec((TG, D), lambda i, j, k: (i, j))), (mod, gate_spec)])

    saved = []
    x = xcat
    for l in range(N_LAYERS):
        li = l // 2
        mod = mods[l]
        k_in, k_out, k_ff1, k_ff2 = _layer_weights(l)
        pend = [(k, gather_of([k]), full) for k in _layer_weights(l + 1)] if l + 1 < N_LAYERS else []
        s = {"x0": x}
        s["h"] = _norm_mod_fwd(cfg, f"norm1_fwd_{l}", x, w["norm1_g"][l][None], mod, 0, 1)
        if l % 2 == 0:
            lgb = jnp.broadcast_to(jax.nn.log_sigmoid(w["ret_decay"][li]).T[:, :, None, None], (H, 2, 8, LANES))
            qg, kg = w["att_q_norm_g"][li][None], w["att_k_norm_g"][li][None]
            s["p"] = mm(pend, f"ab_in_{l}", s["h"], full[k_in], mode="nn", outs=[F32], tn=768)
            s["rq"], s["rk"], s["aq"], s["ak"] = _prep_fwd(cfg, f"prep_fwd_{l}", s["p"], cosf, sinf, qg, kg)
            s["o"], ret, s["st"] = with_carry(
                functools.partial(_retention_fwd, cfg, f"ret_fwd_{l}", s["rq"], s["rk"], s["p"], lgb),
                keys0[1:3] if l == 0 else [], gather_of, full)
            att = with_carry(functools.partial(_attention_fwd, cfg, f"att_fwd_{l}", s["aq"], s["ak"], s["p"]),
                             keys0[3:] if l == 0 else [], gather_of, full)
            s["cat"] = jnp.concatenate([ret, att], axis=-1)
            s["lgb"], s["qg"], s["kg"] = lgb, qg, kg
            x, s["y1"] = gated_out(pend, f"ab_out_{l}", s["cat"], k_out, x, mod, 2)
        else:
            s["a"] = mm(pend, f"cm_in_{l}", s["h"], full[k_in], mode="nn", outs=[F32])
            s["vg"], s["ws"], s["bs"] = w["cm_v_norm_g"][li][None], w["cm_w_s"][li], w["cm_b_s"][li][:, :, None]
            s["m"] = _cm_fwd(cfg, f"cm_fwd_{l}", s["a"], s["vg"], s["ws"], s["bs"])
            x, s["y1"] = gated_out(pend, f"cm_out_{l}", s["m"], k_out, x, mod, 2)
        s["x1"] = x
        s["h2"] = _norm_mod_fwd(cfg, f"norm2_fwd_{l}", x, w["norm2_g"][l][None], mod, 3, 4)
        s["r"] = mm(pend, f"ff1_{l}", s["h2"], full[k_ff1], mode="nn", outs=[BF16],
                    epi=lambda acc, row_tile: (jnp.square(jnp.maximum(acc, 0.0)),))
        x, s["y2"] = gated_out(pend, f"ff2_{l}", s["r"], k_ff2, x, mod, 5)
        saved.append(s)

    loss, dx = _loss_grad(cfg, x, tgt)

    small = {k: [None] * n for k, n in (("norm1_g", 4), ("norm2_g", 4), ("ret_lg", 2), ("att_q_norm_g", 2),
                                        ("att_k_norm_g", 2), ("cm_v_norm_g", 2), ("cm_w_s", 2), ("cm_b_s", 2))}
    dmods = [None] * N_LAYERS

    for l in reversed(range(N_LAYERS)):
        li = l // 2
        s, mod = saved[l], mods[l]
        k_in, k_out, k_ff1, k_ff2 = _layer_weights(l)
        pend = [(k, exchange_of([k]), recv) for k in reversed(_layer_weights(l + 1))] if l + 1 < N_LAYERS else []
        dy2, dg2 = _gate_bwd(cfg, f"gate2_bwd_{l}", dx, s["y2"], mod, 5)
        da2 = mm(pend, f"ff2_dx_{l}", dy2, full[k_ff2], mode="nt", outs=[BF16],
                 epi=lambda acc, row_tile, r_ref: (acc * (2.0 * jnp.sqrt(r_ref[...].astype(F32))),),
                 extras=[(s["r"], pl.BlockSpec((_tile(cfg.T, 1024), _tile(cfg.FF, 1024)), lambda i, j, k: (i, j)))])
        big[k_ff2] = mm(pend, f"ff2_dw_{l}", s["r"], dy2, mode="tn", outs=[BF16])
        big[k_ff1] = mm(pend, f"ff1_dw_{l}", s["h2"], da2, mode="tn", outs=[BF16])
        dh2 = mm(pend, f"ff1_dx_{l}", da2, full[k_ff1], mode="nt", outs=[F32])
        dx, dm2, small["norm2_g"][l] = _norm_mod_bwd(cfg, f"norm2_bwd_{l}", s["x1"], w["norm2_g"][l][None], mod, 3, 4,
                                                     dh2, dx)
        do, dg1 = _gate_bwd(cfg, f"gate1_bwd_{l}", dx, s["y1"], mod, 2)
        if l % 2 == 0:
            big[k_out] = _mm(f"ab_out_dw_{l}", s["cat"], do, mode="tn", outs=[BF16])
            dcat = _mm(f"ab_out_dx_{l}", do, full[k_out], mode="nt", outs=[F32])
            d_rq, d_rk, d_rv, d_gt, dlg = with_carry(
                functools.partial(_retention_bwd, cfg, f"ret_bwd_{l}", s["rq"], s["rk"], s["p"], s["o"], s["st"], dcat,
                                  s["lgb"]), [k_ff2, k_ff1] if l == 0 else [], exchange_of, recv)
            d_aq, d_ak, d_av = with_carry(
                functools.partial(_attention_bwd, cfg, f"att_bwd_{l}", s["aq"], s["ak"], s["p"], dcat),
                [k_out] if l == 0 else [], exchange_of, recv)
            dp, dqg, dkg = _prep_bwd(cfg, f"prep_bwd_{l}", s["p"], cosf, sinf, s["qg"], s["kg"],
                                     d_rq, d_rk, d_rv, d_gt, d_aq, d_ak, d_av)
            small["ret_lg"][li] = jnp.sum(dlg[:, :, :, 0, 0], axis=0).T
            small["att_q_norm_g"][li], small["att_k_norm_g"][li] = dqg[0], dkg[0]
            big[k_in] = _mm(f"ab_in_dw_{l}", s["h"], dp, mode="tn", outs=[BF16])
            last = [(k_in, exchange_of([k_in]), recv)] if l == 0 else []
            dh = mm(last, f"ab_in_dx_{l}", dp, full[k_in], mode="nt", outs=[F32], tk=768)
        else:
            big[k_out] = _mm(f"cm_out_dw_{l}", s["m"], do, mode="tn", outs=[BF16])
            dm = _mm(f"cm_out_dx_{l}", do, full[k_out], mode="nt", outs=[F32])
            da, dws, dbs, dvg = _cm_bwd(cfg, f"cm_bwd_{l}", s["a"], s["vg"], s["ws"], s["bs"], dm)
            small["cm_w_s"][li], small["cm_b_s"][li], small["cm_v_norm_g"][li] = dws, dbs[:, :, 0], dvg[0]
            big[k_in] = _mm(f"cm_in_dw_{l}", s["h"], da, mode="tn", outs=[BF16])
            dh = _mm(f"cm_in_dx_{l}", da, full[k_in], mode="nt", outs=[F32])
        dx, dm1, small["norm1_g"][l] = _norm_mod_bwd(cfg, f"norm1_bwd_{l}", s["x0"], w["norm1_g"][l][None], mod, 0, 1,
                                                     dh, dx)
        dmods[l] = jnp.concatenate([dm1, dg1, dm2, dg2], axis=2)
    return loss, dx, recv, small, dmods


N_DEV = 8
N_CHIP = 4
MESH = pl.DeviceIdType.MESH
ANY = pl.BlockSpec(memory_space=pl.ANY)
BIG = {"ab_w_in": 1, "ab_w_out": 0, "cm_w_in": 1, "cm_w_out": 0, "ff_w1": 1, "ff_w2": 0}


class _Carry(NamedTuple):
    kind: str
    srcs: tuple
    axes: tuple


def _place():
    x, y, c = lax.axis_index("x"), lax.axis_index("y"), lax.axis_index("c")
    return x, y, c, [(1 - x, y), (x, 1 - y), (1 - x, 1 - y)]


def _shard_of(ref, axis, s, width):
    start = pl.multiple_of(s * width, LANES)
    if axis == 0:
        return ref.at[pl.ds(start, width), :]
    return ref.at[:, pl.ds(start, width)]


def _carry_out_shapes(carry):
    shapes = []
    for src, axis in zip(carry.srcs, carry.axes):
        shape = list(src.shape)
        if carry.kind == "gather":
            shape[axis] *= N_CHIP
        else:
            shape[axis] //= N_CHIP
            shape = [N_CHIP] + shape
        shapes.append(jax.ShapeDtypeStruct(tuple(shape), src.dtype))
    return shapes


def _carry_copies(carry, srcs, dsts, send_sems, recv_sems, local_sems):
    x, y, c, chips = _place()
    me = 2 * x + y
    copies = []
    for t, axis in enumerate(carry.axes):
        if carry.kind == "gather":
            own = _shard_of(dsts[t], axis, me, srcs[t].shape[axis])
            copies.append(pltpu.make_async_copy(srcs[t], own, local_sems.at[t]))
            parts = [(srcs[t], own)] * 3
        else:
            width = dsts[t].shape[1 + axis]
            copies.append(pltpu.make_async_copy(_shard_of(srcs[t], axis, me, width), dsts[t].at[3], local_sems.at[t]))
            parts = [(_shard_of(srcs[t], axis, 2 * px + py, width), dsts[t].at[j]) for j, (px, py) in enumerate(chips)]
        for j, (px, py) in enumerate(chips):
            copies.append(pltpu.make_async_remote_copy(
                src_ref=parts[j][0], dst_ref=parts[j][1], send_sem=send_sems.at[3 * t + j],
                recv_sem=recv_sems.at[3 * t + j], device_id=(px, py, c), device_id_type=MESH))
    return copies


def _comm_call(name, carry):
    nc = len(carry.srcs)

    def body(*refs):
        copies = _carry_copies(carry, refs[:nc], refs[nc:2 * nc], *refs[2 * nc:])
        for cp in copies:
            cp.start()
        for cp in copies:
            cp.wait()

    return pl.pallas_call(
        body, name=name, out_shape=_carry_out_shapes(carry), in_specs=[ANY] * nc, out_specs=[ANY] * nc,
        scratch_shapes=[pltpu.SemaphoreType.DMA((3 * nc,)), pltpu.SemaphoreType.DMA((3 * nc,)),
                        pltpu.SemaphoreType.DMA((nc,))],
    )(*carry.srcs)


def _allgather8(name, block):
    m_per, n = block.shape

    def body(x_ref, out_ref, send_sems, recv_sems, local_sem):
        x, y, c, chips = _place()
        me, sibling = (x, y, c), (x, y, 1 - c)

        def rows(px, py, pc):
            return out_ref.at[pl.ds((4 * px + 2 * py + pc) * m_per, m_per), :]

        def copy(k, blk, to, src=None):
            return pltpu.make_async_remote_copy(
                src_ref=rows(*blk) if src is None else src, dst_ref=rows(*blk),
                send_sem=send_sems.at[k], recv_sem=recv_sems.at[k], device_id=to, device_id_type=MESH)

        mine = pltpu.make_async_copy(x_ref, rows(*me), local_sem)
        mine.start()
        first = [copy(0, me, sibling, src=x_ref)]
        first += [copy(1 + j, me, (*chip, c), src=x_ref) for j, chip in enumerate(chips)]
        for cp in first:
            cp.start()
        passed = [copy(4 + j, (*chip, c), sibling) for j, chip in enumerate(chips)]
        for j, chip in enumerate(chips):
            copy(1 + j, (*chip, c), me).wait_recv()
            passed[j].start()
        copy(0, sibling, me).wait_recv()
        for j, chip in enumerate(chips):
            copy(4 + j, (*chip, 1 - c), me).wait_recv()
        for cp in first + passed:
            cp.wait_send()
        mine.wait()

    return pl.pallas_call(
        body, name=name, out_shape=jax.ShapeDtypeStruct((N_DEV * m_per, n), block.dtype),
        in_specs=[pl.BlockSpec(memory_space=pltpu.VMEM)], out_specs=pl.BlockSpec(memory_space=pltpu.VMEM),
        scratch_shapes=[pltpu.SemaphoreType.DMA((7,)), pltpu.SemaphoreType.DMA((7,)), pltpu.SemaphoreType.DMA],
        compiler_params=pltpu.CompilerParams(vmem_limit_bytes=VMEM_LIMIT),
    )(block)


def _swap_sibling(parts):
    n_t = len(parts)

    def body(*refs):
        srcs, outs = refs[:n_t], refs[n_t:2 * n_t]
        send_sems, recv_sems = refs[2 * n_t:]
        x, y, c, _ = _place()
        copies = []
        for t in range(n_t):
            cp = pltpu.make_async_remote_copy(
                src_ref=srcs[t], dst_ref=outs[t], send_sem=send_sems.at[t], recv_sem=recv_sems.at[t],
                device_id=(x, y, 1 - c), device_id_type=MESH)
            cp.start()
            copies.append(cp)
        for cp in copies:
            cp.wait()

    return pl.pallas_call(
        body, name="swap_sibling", out_shape=[jax.ShapeDtypeStruct(p.shape, p.dtype) for p in parts],
        in_specs=[ANY] * n_t, out_specs=[ANY] * n_t,
        scratch_shapes=[pltpu.SemaphoreType.DMA((n_t,)), pltpu.SemaphoreType.DMA((n_t,))],
    )(*parts)


def _rows_view(a):
    if a.ndim == 1:
        return a.reshape(1, a.shape[0])
    return a.reshape(-1, a.shape[-1])


def _row_tile(rows, cols, target_elems=1 << 17):
    tr = rows
    while tr % 16 == 0 and tr * cols > target_elems:
        tr //= 2
    return tr


def _sum_leading(name, a):
    n, rows, cols = a.shape
    tr = _row_tile(rows, cols * n)

    def body(a_ref, o_ref):
        acc = a_ref[0].astype(F32)
        for i in range(1, n):
            acc = acc + a_ref[i].astype(F32)
        o_ref[...] = acc

    return pl.pallas_call(
        body, name=name, grid=(rows // tr,),
        in_specs=[pl.BlockSpec((n, tr, cols), lambda i: (0, i, 0))],
        out_specs=pl.BlockSpec((tr, cols), lambda i: (i, 0)),
        out_shape=jax.ShapeDtypeStruct((rows, cols), F32),
        compiler_params=_params(("parallel",)),
    )(a)


def _silu_rows(name, x):
    def body(x_ref, o_ref):
        v = x_ref[...]
        o_ref[...] = v * jax.nn.sigmoid(v)

    return pl.pallas_call(body, name=name, out_shape=jax.ShapeDtypeStruct(x.shape, F32))(x)


def _silu_bwd_rows(name, x, dy):
    def body(x_ref, dy_ref, o_ref):
        v = x_ref[...]
        sg = jax.nn.sigmoid(v)
        o_ref[...] = dy_ref[...] * (sg * (1.0 + v * (1.0 - sg)))

    return pl.pallas_call(body, name=name, out_shape=jax.ShapeDtypeStruct(x.shape, F32))(x, dy)


def _adamw(name, w, g_parts, m, v):
    shape = w.shape
    w2, m2, v2 = _rows_view(w), _rows_view(m), _rows_view(v)
    gs = [_rows_view(g) for g in g_parts]
    rows, cols = w2.shape
    tr = _row_tile(rows, cols)
    ng = len(gs)

    def body(*refs):
        w_ref, m_ref, v_ref = refs[0], refs[1], refs[2]
        g_refs = refs[3:3 + ng]
        g_out, d_out, m_out, v_out = refs[3 + ng:]
        g = g_refs[0][...]
        for r in g_refs[1:]:
            g = g + r[...]
        m1 = ADAM_B1 * m_ref[...] + (1.0 - ADAM_B1) * g
        v1 = ADAM_B2 * v_ref[...] + (1.0 - ADAM_B2) * jnp.square(g)
        m_hat = m1 / (1.0 - ADAM_B1 ** ADAM_STEP)
        v_hat = v1 / (1.0 - ADAM_B2 ** ADAM_STEP)
        g_out[...] = g
        d_out[...] = -ADAM_LR * (m_hat / (jnp.sqrt(v_hat) + ADAM_EPS) + ADAM_WD * w_ref[...])
        m_out[...] = m1
        v_out[...] = v1

    spec = pl.BlockSpec((tr, cols), lambda i: (i, 0))
    res = pl.pallas_call(
        body, name=name, grid=(rows // tr,), in_specs=[spec] * (3 + ng), out_specs=[spec] * 4,
        out_shape=[jax.ShapeDtypeStruct((rows, cols), F32)] * 4,
        compiler_params=_params(("parallel",)),
    )(w2, m2, v2, *gs)
    return tuple(r.reshape(shape) for r in res)


MOD_ROWS = 48


def kernel(x, c, ctx, c_ctx, mod_w, mod_b, norm1_g, norm2_g, ab_w_in, ab_w_out, ret_decay, att_q_norm_g, att_k_norm_g, cm_w_in, cm_v_norm_g, cm_w_s, cm_b_s, cm_w_out, ff_w1, ff_w2, loss_target, m_c_ctx, m_mod_w, m_mod_b, m_norm1_g, m_norm2_g, m_ab_w_in, m_ab_w_out, m_ret_decay, m_att_q_norm_g, m_att_k_norm_g, m_cm_w_in, m_cm_v_norm_g, m_cm_w_s, m_cm_b_s, m_cm_w_out, m_ff_w1, m_ff_w2, v_c_ctx, v_mod_w, v_mod_b, v_norm1_g, v_norm2_g, v_ab_w_in, v_ab_w_out, v_ret_decay, v_att_q_norm_g, v_att_k_norm_g, v_cm_w_in, v_cm_v_norm_g, v_cm_w_s, v_cm_b_s, v_cm_w_out, v_ff_w1, v_ff_w2):
    B, SL, D = x.shape
    cfg = Cfg(B=B, SC=ctx.shape[1], SL=SL, D=D, FF=ff_w1.shape[2] * N_CHIP)
    L = N_LAYERS
    n_ex = B * N_DEV
    mcols = mod_w.shape[2]
    weights = dict(c_ctx=c_ctx, mod_w=mod_w, mod_b=mod_b, norm1_g=norm1_g, norm2_g=norm2_g, ab_w_in=ab_w_in,
                   ab_w_out=ab_w_out, ret_decay=ret_decay, att_q_norm_g=att_q_norm_g, att_k_norm_g=att_k_norm_g,
                   cm_w_in=cm_w_in, cm_v_norm_g=cm_v_norm_g, cm_w_s=cm_w_s, cm_b_s=cm_b_s, cm_w_out=cm_w_out,
                   ff_w1=ff_w1, ff_w2=ff_w2)
    m_in = dict(c_ctx=m_c_ctx, mod_w=m_mod_w, mod_b=m_mod_b, norm1_g=m_norm1_g, norm2_g=m_norm2_g, ab_w_in=m_ab_w_in,
                ab_w_out=m_ab_w_out, ret_decay=m_ret_decay, att_q_norm_g=m_att_q_norm_g, att_k_norm_g=m_att_k_norm_g,
                cm_w_in=m_cm_w_in, cm_v_norm_g=m_cm_v_norm_g, cm_w_s=m_cm_w_s, cm_b_s=m_cm_b_s, cm_w_out=m_cm_w_out,
                ff_w1=m_ff_w1, ff_w2=m_ff_w2)
    v_in = dict(c_ctx=v_c_ctx, mod_w=v_mod_w, mod_b=v_mod_b, norm1_g=v_norm1_g, norm2_g=v_norm2_g, ab_w_in=v_ab_w_in,
                ab_w_out=v_ab_w_out, ret_decay=v_ret_decay, att_q_norm_g=v_att_q_norm_g, att_k_norm_g=v_att_k_norm_g,
                cm_w_in=v_cm_w_in, cm_v_norm_g=v_cm_v_norm_g, cm_w_s=v_cm_w_s, cm_b_s=v_cm_b_s, cm_w_out=v_cm_w_out,
                ff_w1=v_ff_w1, ff_w2=v_ff_w2)
    xi, yi, ci = lax.axis_index("x"), lax.axis_index("y"), lax.axis_index("c")
    chip = 2 * xi + yi
    dev = 2 * chip + ci

    shards = {n: [weights[n][i].astype(BF16) for i in range(weights[n].shape[0])] for n in BIG}
    vgw = cm_v_norm_g.shape[1]
    blk = jnp.zeros((8, D), F32).at[:B].set(c).at[B:B + 2, :vgw].set(cm_v_norm_g)
    g0 = _allgather8("gather_c", blk).reshape(N_DEV, 8, D)
    c_all = g0[:, :B].reshape(n_ex, D)
    vg_full = jnp.concatenate([g0[2 * s, B:B + 2, :vgw] for s in range(N_CHIP)], axis=-1)

    pre = jnp.zeros((MOD_ROWS, D), F32).at[:n_ex].set(c_all).at[n_ex].set(c_ctx)
    act = _silu_rows("silu_c", pre)
    mpart = jnp.stack([_mm(f"mod_fwd_{l}", act, mod_w, mode="nn", layer=l, outs=[F32], tn=mcols) for l in range(L)])
    g1 = _allgather8("gather_mod", mpart.reshape(L * MOD_ROWS, mcols)).reshape(N_DEV, L, MOD_ROWS, mcols)
    mod_all = jnp.concatenate([g1[2 * s] for s in range(N_CHIP)], axis=-1) + mod_b[:, None, :]
    mod_lat = lax.dynamic_slice_in_dim(mod_all, dev * B, B, axis=1)
    mod_ctx = jnp.broadcast_to(mod_all[:, n_ex][:, None], mod_lat.shape)
    mods = jnp.stack([mod_ctx, mod_lat], axis=2).reshape(L, B, 2, 6, D)

    w = dict(norm1_g=norm1_g, norm2_g=norm2_g, ret_decay=ret_decay, att_q_norm_g=att_q_norm_g,
             att_k_norm_g=att_k_norm_g, cm_v_norm_g=vg_full, cm_w_s=cm_w_s, cm_b_s=cm_b_s)
    xcat = jnp.concatenate([ctx, x], axis=1).reshape(cfg.T, D)
    loss_local, dxcat, recv, small, dmods = _local_step(cfg, xcat, loss_target.reshape(B * SL, D), mods, shards, w)
    loss = lax.psum(loss_local, ("x", "y", "c"))
    grad_x = dxcat.reshape(B, cfg.S, D)[:, cfg.SC:, :]

    part = [jnp.stack([_sum_leading(f"sum_{n}_{i}", recv[(n, i)]) for i in range(weights[n].shape[0])]) for n in BIG]
    other = _swap_sibling(part)
    out = {}
    for n, p_mine, p_other in zip(BIG, part, other):
        out[n] = _adamw(f"adamw_{n}", weights[n], [p_mine, p_other], m_in[n], v_in[n])

    dmod = jnp.stack(dmods).reshape(L, B, 2, 6 * D)
    dmod_lat = dmod[:, :, 1]
    dmod_ctx = jnp.sum(dmod[:, :, 0], axis=1)
    d_ret = jnp.stack(small["ret_lg"]) * jax.nn.sigmoid(-ret_decay)
    summed = [dmod_ctx.reshape(-1), jnp.stack(small["norm1_g"]).reshape(-1), jnp.stack(small["norm2_g"]).reshape(-1),
              jnp.stack(small["cm_v_norm_g"]).reshape(-1), jnp.stack(small["cm_w_s"]).reshape(-1),
              jnp.stack(small["cm_b_s"]).reshape(-1), jnp.stack(small["att_q_norm_g"]).reshape(-1),
              jnp.stack(small["att_k_norm_g"]).reshape(-1), d_ret.reshape(-1)]
    sizes = [int(a.shape[0]) for a in summed]
    flat = jnp.concatenate(summed + [dmod_lat.reshape(-1)])
    n_sum = sum(sizes)
    n_sum_rows = -(-n_sum // D)
    lat_rows = (L * B * 6 * D) // D
    pack_rows = -(-(n_sum_rows + lat_rows) // 8) * 8
    packed = jnp.zeros((pack_rows * D,), F32).at[:n_sum].set(flat[:n_sum])
    packed = packed.at[n_sum_rows * D:(n_sum_rows + lat_rows) * D].set(flat[n_sum:]).reshape(pack_rows, D)
    g2 = _allgather8("gather_small", packed).reshape(N_DEV, pack_rows, D)
    tot = _sum_leading("sum_small", g2[:, :n_sum_rows]).reshape(-1)
    pieces, off = [], 0
    for sz in sizes:
        pieces.append(tot[off:off + sz])
        off += sz
    dmod_ctx_t, g_n1, g_n2, g_vg, g_ws, g_bs, g_qg, g_kg, g_rd = pieces
    dmod_ctx_t = dmod_ctx_t.reshape(L, 6 * D)
    dmod_lat_all = g2[:, n_sum_rows:n_sum_rows + lat_rows].reshape(N_DEV, L, B, 6 * D)
    dmod_rows = jnp.zeros((L, MOD_ROWS, 6 * D), F32)
    dmod_rows = dmod_rows.at[:, :n_ex].set(jnp.transpose(dmod_lat_all, (1, 0, 2, 3)).reshape(L, n_ex, 6 * D))
    dmod_rows = dmod_rows.at[:, n_ex].set(dmod_ctx_t)
    g_mod_b = _sum_leading("sum_mod_b", jnp.transpose(dmod_rows, (1, 0, 2)))
    dmod_mine = lax.dynamic_slice_in_dim(dmod_rows, chip * mcols, mcols, axis=2)
    g_mod_w = jnp.stack([_mm(f"mod_dw_{l}", act, dmod_mine[l], mode="tn", outs=[F32], tn=mcols) for l in range(L)])
    ctx8 = jnp.zeros((L, 8, mcols), F32).at[:, 0].set(dmod_mine[:, n_ex])
    dcc = [_mm(f"mod_dctx_{l}", ctx8[l], mod_w, mode="nt", layer=l, outs=[F32], tk=mcols) for l in range(L)]
    dcc = _sum_leading("sum_dctx_layers", jnp.stack(dcc))
    g3 = _allgather8("gather_dctx", dcc).reshape(N_DEV, 8, D)
    dcc_t = _sum_leading("sum_dctx_chips", g3[0::2])[0:1]
    g_c_ctx = _silu_bwd_rows("silu_bwd_cctx", c_ctx[None], dcc_t)[0]

    vg_mine = lax.dynamic_slice_in_dim(g_vg.reshape(2, -1), chip * vgw, vgw, axis=1)
    small_g = dict(c_ctx=g_c_ctx, mod_w=g_mod_w, mod_b=g_mod_b, norm1_g=g_n1.reshape(norm1_g.shape),
                   norm2_g=g_n2.reshape(norm2_g.shape), ret_decay=g_rd.reshape(ret_decay.shape),
                   att_q_norm_g=g_qg.reshape(att_q_norm_g.shape), att_k_norm_g=g_kg.reshape(att_k_norm_g.shape),
                   cm_v_norm_g=vg_mine, cm_w_s=g_ws.reshape(cm_w_s.shape), cm_b_s=g_bs.reshape(cm_b_s.shape))
    for n, g in small_g.items():
        out[n] = _adamw(f"adamw_{n}", weights[n], [g], m_in[n], v_in[n])

    order = list(weights)
    return (loss, grad_x, *[out[n][0] for n in order], *[out[n][1] for n in order],
            *[out[n][2] for n in order], *[out[n][3] for n in order])
```

```python
import functools
import math
from typing import NamedTuple

import jax
import jax.numpy as jnp
from jax import lax
from jax.experimental import pallas as pl
from jax.experimental.pallas import tpu as pltpu

F32 = jnp.float32
BF16 = jnp.bfloat16
EPS = 1e-6
ROPE_BASE = 10000.0
LANES = 128
CHUNK = 128
N_LAYERS = 4
VMEM_LIMIT = 56 * 1024 * 1024

ADAM_LR = 0.001
ADAM_B1 = 0.9
ADAM_B2 = 0.999
ADAM_EPS = 1e-08
ADAM_WD = 0.01
ADAM_STEP = 10


class Cfg(NamedTuple):
    B: int = 4
    SC: int = 256
    SL: int = 2048
    D: int = 1024
    FF: int = 4096
    GRID_W: int = 64
    H: int = 4
    KV: int = 2
    CMW: int = 1024
    CMG: int = 8

    @property
    def S(self):
        return self.SC + self.SL

    @property
    def T(self):
        return self.B * self.S

    @property
    def TM(self):
        return self.SC

    @property
    def TPE(self):
        return self.S // self.SC

    @property
    def ABW(self):
        return (5 * self.H + 2 * self.KV) * CHUNK


def _tile(dim, pref):
    t = min(dim, pref)
    while dim % t:
        t -= LANES
    return t


def _dot(a, b):
    return lax.dot_general(a, b, (((1,), (0,)), ((), ())), preferred_element_type=F32)


def _dot_nt(a, b):
    return lax.dot_general(a, b, (((1,), (1,)), ((), ())), preferred_element_type=F32)


def _dot_tn(a, b):
    return lax.dot_general(a, b, (((0,), (0,)), ((), ())), preferred_element_type=F32)


def _params(sem, vmem=VMEM_LIMIT):
    return pltpu.CompilerParams(dimension_semantics=sem, vmem_limit_bytes=vmem)


def _mod_index(cfg):
    tpe = cfg.TPE
    return lambda i: (i // tpe, jnp.minimum(i % tpe, 1), 0, 0)


def _mm(name, a, b, *, mode, outs, tm=1024, tn=1024, tk=1024, layer=None, epi=None, extras=(), carry=None):
    bshape = b.shape[1:] if layer is not None else b.shape
    if mode == "nn":
        (M, K), N = a.shape, bshape[1]
    elif mode == "nt":
        (M, K), N = a.shape, bshape[0]
    else:
        (K, M), N = a.shape, bshape[1]
    tm, tn, tk = _tile(M, tm), _tile(N, tn), _tile(K, tk)
    nk = K // tk
    a_spec = (pl.BlockSpec((tk, tm), lambda i, j, k: (k, i)) if mode == "tn"
              else pl.BlockSpec((tm, tk), lambda i, j, k: (i, k)))
    if mode == "nt":
        bblk, bidx = (tn, tk), (lambda i, j, k: (j, k))
    else:
        bblk, bidx = (tk, tn), (lambda i, j, k: (k, j))
    if layer is not None:
        b_spec = pl.BlockSpec((None,) + bblk, lambda i, j, k: (layer,) + bidx(i, j, k))
    else:
        b_spec = pl.BlockSpec(bblk, bidx)
    ne, no = len(extras), len(outs)
    nc = len(carry.srcs) if carry is not None else 0
    dot = {"nn": _dot, "nt": _dot_nt, "tn": _dot_tn}[mode]
    grid = (M // tm, N // tn, nk)

    def body(*refs):
        a_ref, b_ref = refs[0], refs[1]
        ex, out_refs = refs[2:2 + ne], refs[2 + ne + nc:2 + ne + nc + no]
        row_tile = pl.program_id(0)

        if nc:
            step = (pl.program_id(0) * grid[1] + pl.program_id(1)) * grid[2] + pl.program_id(2)
            c_src = refs[2 + ne:2 + ne + nc]
            c_dst = refs[2 + ne + nc + no:2 + ne + 2 * nc + no]
            sems = refs[2 + ne + 2 * nc + no:2 + ne + 2 * nc + no + 3]

            @pl.when(step == 0)
            def _():
                for cp in _carry_copies(carry, c_src, c_dst, *sems):
                    cp.start()

        def finish(acc):
            res = epi(acc, row_tile, *ex) if epi is not None else (acc,)
            for r, o in zip(res, out_refs):
                o[...] = r.astype(o.dtype)

        part = dot(a_ref[...].astype(BF16), b_ref[...].astype(BF16))
        if nk == 1:
            finish(part)
        else:
            acc_ref = refs[-1]
            k = pl.program_id(2)

            @pl.when(k == 0)
            def _():
                acc_ref[...] = part

            @pl.when(k > 0)
            def _():
                acc_ref[...] += part

            @pl.when(k == nk - 1)
            def _():
                finish(acc_ref[...])

        if nc:
            @pl.when(step == grid[0] * grid[1] * grid[2] - 1)
            def _():
                for cp in _carry_copies(carry, c_src, c_dst, *sems):
                    cp.wait()

    scratch = [pltpu.SemaphoreType.DMA((3 * nc,)), pltpu.SemaphoreType.DMA((3 * nc,)),
               pltpu.SemaphoreType.DMA((nc,))] if nc else []
    if nk > 1:
        scratch.append(pltpu.VMEM((tm, tn), F32))
    res = pl.pallas_call(
        body, name=name, grid=grid,
        in_specs=[a_spec, b_spec] + [s for _, s in extras] + [ANY] * nc,
        out_specs=[pl.BlockSpec((tm, tn), lambda i, j, k: (i, j)) for _ in outs] + [ANY] * nc,
        out_shape=[jax.ShapeDtypeStruct((M, N), d) for d in outs] + (_carry_out_shapes(carry) if nc else []),
        scratch_shapes=scratch,
        compiler_params=_params(("arbitrary",) * 3 if nc else ("parallel", "parallel", "arbitrary")),
    )(a, b, *[x for x, _ in extras], *(carry.srcs if nc else ()))
    if nc:
        return (res[0] if no == 1 else res[:no]), res[no:]
    return res[0] if no == 1 else res


def _norm_mod_fwd(cfg, name, x, gain, mod, ish, isc):
    T, D, TM = cfg.T, cfg.D, cfg.TM

    def body(x_ref, g_ref, mod_ref, h_ref):
        x = x_ref[...]
        rstd = lax.rsqrt(jnp.mean(x * x, axis=-1, keepdims=True) + EPS)
        n = x * rstd * g_ref[...]
        h = n * (1.0 + mod_ref[pl.ds(isc, 1), :]) + mod_ref[pl.ds(ish, 1), :]
        h_ref[...] = h.astype(BF16)

    return pl.pallas_call(
        body, name=name, grid=(T // TM,),
        in_specs=[pl.BlockSpec((TM, D), lambda i: (i, 0)), pl.BlockSpec((1, D), lambda i: (0, 0)),
                  pl.BlockSpec((None, None, 6, D), _mod_index(cfg))],
        out_specs=pl.BlockSpec((TM, D), lambda i: (i, 0)),
        out_shape=jax.ShapeDtypeStruct((T, D), BF16),
        compiler_params=_params(("parallel",)),
    )(x, gain, mod)


def _norm_mod_bwd(cfg, name, x, gain, mod, ish, isc, dh, dres, gate=None):
    T, D, TM, TPE = cfg.T, cfg.D, cfg.TM, cfg.TPE
    ng = 2 if gate is not None else 0

    def body(*refs):
        x_ref, g_ref, mod_ref, dh_ref, dres_ref = refs[:5]
        dx_ref, dmod_ref, dgain_ref = refs[5 + ng:8 + ng]
        i = pl.program_id(0)
        t = i % TPE
        x = x_ref[...]
        g = g_ref[...]
        dh = dh_ref[...].astype(F32)
        rstd = lax.rsqrt(jnp.mean(x * x, axis=-1, keepdims=True) + EPS)
        xhat = x * rstd
        dn = dh * (1.0 + mod_ref[pl.ds(isc, 1), :])
        dsh = jnp.sum(dh, axis=0, keepdims=True)
        dsc = jnp.sum(dh * (xhat * g), axis=0, keepdims=True)
        dgain = jnp.sum(dn * xhat, axis=0, keepdims=True)
        dxh = dn * g
        dx = rstd * (dxh - xhat * jnp.mean(dxh * xhat, axis=-1, keepdims=True)) + dres_ref[...]
        dx_ref[...] = dx
        sums = [(dmod_ref.at[pl.ds(0, 1), :], dsh), (dmod_ref.at[pl.ds(1, 1), :], dsc)]
        if ng:
            y_ref, gmod_ref = refs[5:7]
            dy_ref, dgate_ref = refs[8 + ng:]
            dy_ref[...] = (dx * gmod_ref[pl.ds(gate[2], 1), :]).astype(BF16)
            sums.append((dgate_ref, jnp.sum(dx * y_ref[...].astype(F32), axis=0, keepdims=True)))

        @pl.when(t <= 1)
        def _():
            for ref, val in sums:
                ref[...] = val

        @pl.when(t > 1)
        def _():
            for ref, val in sums:
                ref[...] += val

        @pl.when(i == 0)
        def _():
            dgain_ref[...] = dgain

        @pl.when(i > 0)
        def _():
            dgain_ref[...] += dgain

    tok = pl.BlockSpec((TM, D), lambda i: (i, 0))
    mod_spec = pl.BlockSpec((None, None, 6, D), _mod_index(cfg))
    res = pl.pallas_call(
        body, name=name, grid=(T // TM,),
        in_specs=[tok, pl.BlockSpec((1, D), lambda i: (0, 0)), mod_spec, tok, tok] + ([tok, mod_spec] if ng else []),
        out_specs=[tok, pl.BlockSpec((None, None, 2, D), _mod_index(cfg)), pl.BlockSpec((1, D), lambda i: (0, 0))]
        + ([tok, pl.BlockSpec((None, None, 1, D), _mod_index(cfg))] if ng else []),
        out_shape=[jax.ShapeDtypeStruct((T, D), F32), jax.ShapeDtypeStruct((cfg.B, 2, 2, D), F32),
                   jax.ShapeDtypeStruct((1, D), F32)]
        + ([jax.ShapeDtypeStruct((T, D), BF16), jax.ShapeDtypeStruct((cfg.B, 2, 1, D), F32)] if ng else []),
        compiler_params=_params(("arbitrary",)),
    )(x, gain, mod, dh, dres, *(gate[:2] if ng else ()))
    return res


def _gate_bwd(cfg, name, dx, y, mod, igate):
    T, D, TM, TPE = cfg.T, cfg.D, cfg.TM, cfg.TPE

    def body(dx_ref, y_ref, mod_ref, dy_ref, dg_ref):
        t = pl.program_id(0) % TPE
        dx = dx_ref[...]
        dy_ref[...] = (dx * mod_ref[pl.ds(igate, 1), :]).astype(BF16)
        dg = jnp.sum(dx * y_ref[...].astype(F32), axis=0, keepdims=True)

        @pl.when(t <= 1)
        def _():
            dg_ref[...] = dg

        @pl.when(t > 1)
        def _():
            dg_ref[...] += dg

    tok = pl.BlockSpec((TM, D), lambda i: (i, 0))
    return pl.pallas_call(
        body, name=name, grid=(T // TM,),
        in_specs=[tok, tok, pl.BlockSpec((None, None, 6, D), _mod_index(cfg))],
        out_specs=[tok, pl.BlockSpec((None, None, 1, D), _mod_index(cfg))],
        out_shape=[jax.ShapeDtypeStruct((T, D), BF16), jax.ShapeDtypeStruct((cfg.B, 2, 1, D), F32)],
        compiler_params=_params(("arbitrary",)),
    )(dx, y, mod)


def _loss_grad(cfg, x, tgt):
    T, D, TM, TPE = cfg.T, cfg.D, cfg.TM, cfg.TPE

    def body(x_ref, t_ref, dx_ref, loss_ref):
        i = pl.program_id(0)
        t = i % TPE

        @pl.when(i == 0)
        def _():
            loss_ref[...] = jnp.zeros_like(loss_ref)

        @pl.when(t == 0)
        def _():
            dx_ref[...] = jnp.zeros_like(dx_ref)

        @pl.when(t > 0)
        def _():
            err = x_ref[...] - t_ref[...]
            dx_ref[...] = err * (1.0 / D)
            loss_ref[...] += 0.5 * jnp.sum(jnp.mean(err * err, axis=-1, keepdims=True), axis=0, keepdims=True)

    tok = pl.BlockSpec((TM, D), lambda i: (i, 0))
    tgt_spec = pl.BlockSpec((TM, D), lambda i: ((i // TPE) * (TPE - 1) + jnp.maximum(i % TPE - 1, 0), 0))
    dx, loss = pl.pallas_call(
        body, name="loss_grad", grid=(T // TM,),
        in_specs=[tok, tgt_spec], out_specs=[tok, pl.BlockSpec((8, LANES), lambda i: (0, 0))],
        out_shape=[jax.ShapeDtypeStruct((T, D), F32), jax.ShapeDtypeStruct((8, LANES), F32)],
        compiler_params=_params(("arbitrary",)),
    )(x, tgt)
    return loss[0, 0], dx


def _rope_tables(cfg):
    rows = cfg.SL // cfg.GRID_W
    row = jnp.repeat(jnp.arange(rows, dtype=F32), cfg.GRID_W)
    col = jnp.tile(jnp.arange(cfg.GRID_W, dtype=F32), rows)
    n_freq = CHUNK // 4
    inv = ROPE_BASE ** (-jnp.arange(n_freq, dtype=F32) / n_freq)
    ang = jnp.concatenate([row[:, None] * inv[None, :], col[:, None] * inv[None, :]], axis=-1)
    cos, sin = jnp.cos(ang), jnp.sin(ang)
    cosf = jnp.concatenate([jnp.ones((cfg.SC, CHUNK), F32), jnp.concatenate([cos, cos], axis=-1)], axis=0)
    sinf = jnp.concatenate([jnp.zeros((cfg.SC, CHUNK), F32), jnp.concatenate([-sin, sin], axis=-1)], axis=0)
    return cosf, sinf


def _rope(x, cosf, sinf):
    return x * cosf + pltpu.roll(x, CHUNK // 2, 1) * sinf


def _irope(dy, cosf, sinf):
    return dy * cosf - pltpu.roll(dy, CHUNK // 2, 1) * sinf


def _prep_fwd(cfg, name, p, cosf, sinf, qg, kg):
    T, TM, TPE, H, KV = cfg.T, cfg.TM, cfg.TPE, cfg.H, cfg.KV
    HW = H * CHUNK
    kscale = CHUNK ** -0.5

    def body(p_ref, c_ref, s_ref, qg_ref, kg_ref, rq_ref, rk_ref, aq_ref, ak_ref):
        cosf, sinf = c_ref[...], s_ref[...]

        def normed(x, g):
            return x * lax.rsqrt(jnp.mean(x * x, axis=-1, keepdims=True) + EPS) * g

        for h in range(H):
            sl = pl.ds(h * CHUNK, CHUNK)
            rq_ref[:, sl] = _rope(p_ref[:, pl.ds(h * CHUNK, CHUNK)], cosf, sinf)
            rk_ref[:, sl] = _rope(p_ref[:, pl.ds(HW + h * CHUNK, CHUNK)], cosf, sinf) * kscale
            aq_ref[:, sl] = _rope(normed(p_ref[:, pl.ds(4 * HW + h * CHUNK, CHUNK)], qg_ref[...]),
                                  cosf, sinf).astype(BF16)
        for h in range(KV):
            ak_ref[:, pl.ds(h * CHUNK, CHUNK)] = _rope(
                normed(p_ref[:, pl.ds(5 * HW + h * CHUNK, CHUNK)], kg_ref[...]), cosf, sinf).astype(BF16)

    tab = pl.BlockSpec((TM, CHUNK), lambda i: (i % TPE, 0))
    vec = pl.BlockSpec((1, CHUNK), lambda i: (0, 0))
    return pl.pallas_call(
        body, name=name, grid=(T // TM,),
        in_specs=[pl.BlockSpec((TM, cfg.ABW), lambda i: (i, 0)), tab, tab, vec, vec],
        out_specs=[pl.BlockSpec((TM, HW), lambda i: (i, 0))] * 3 + [pl.BlockSpec((TM, KV * CHUNK), lambda i: (i, 0))],
        out_shape=[jax.ShapeDtypeStruct((T, HW), F32), jax.ShapeDtypeStruct((T, HW), F32),
                   jax.ShapeDtypeStruct((T, HW), BF16), jax.ShapeDtypeStruct((T, KV * CHUNK), BF16)],
        compiler_params=_params(("parallel",)),
    )(p, cosf, sinf, qg, kg)


def _prep_bwd(cfg, name, p, cosf, sinf, qg, kg, d_rq, d_rk, d_rv, d_gate, d_aq, d_ak, d_av):
    T, TM, TPE, H, KV = cfg.T, cfg.TM, cfg.TPE, cfg.H, cfg.KV
    HW = H * CHUNK
    kscale = CHUNK ** -0.5

    def body(p_ref, c_ref, s_ref, qg_ref, kg_ref, drq_ref, drk_ref, drv_ref, dgt_ref, daq_ref, dak_ref, dav_ref,
             dp_ref, dqg_ref, dkg_ref):
        i = pl.program_id(0)
        cosf, sinf = c_ref[...], s_ref[...]

        def norm_bwd(x, g, dn):
            rstd = lax.rsqrt(jnp.mean(x * x, axis=-1, keepdims=True) + EPS)
            xhat = x * rstd
            dg = jnp.sum(dn * xhat, axis=0, keepdims=True)
            dxh = dn * g
            return rstd * (dxh - xhat * jnp.mean(dxh * xhat, axis=-1, keepdims=True)), dg

        dqg = jnp.zeros((1, CHUNK), F32)
        dkg = jnp.zeros((1, CHUNK), F32)
        for h in range(H):
            sl = pl.ds(h * CHUNK, CHUNK)
            dp_ref[:, pl.ds(h * CHUNK, CHUNK)] = _irope(drq_ref[:, sl], cosf, sinf).astype(BF16)
            dp_ref[:, pl.ds(HW + h * CHUNK, CHUNK)] = (_irope(drk_ref[:, sl], cosf, sinf) * kscale).astype(BF16)
            dp_ref[:, pl.ds(2 * HW + h * CHUNK, CHUNK)] = drv_ref[:, sl].astype(BF16)
            dp_ref[:, pl.ds(3 * HW + h * CHUNK, CHUNK)] = dgt_ref[:, sl].astype(BF16)
            dx, dg = norm_bwd(p_ref[:, pl.ds(4 * HW + h * CHUNK, CHUNK)], qg_ref[...],
                              _irope(daq_ref[:, sl], cosf, sinf))
            dp_ref[:, pl.ds(4 * HW + h * CHUNK, CHUNK)] = dx.astype(BF16)
            dqg = dqg + dg
        for h in range(KV):
            sl = pl.ds(h * CHUNK, CHUNK)
            dx, dg = norm_bwd(p_ref[:, pl.ds(5 * HW + h * CHUNK, CHUNK)], kg_ref[...],
                              _irope(dak_ref[:, sl], cosf, sinf))
            dp_ref[:, pl.ds(5 * HW + h * CHUNK, CHUNK)] = dx.astype(BF16)
            dp_ref[:, pl.ds(5 * HW + (KV + h) * CHUNK, CHUNK)] = dav_ref[:, sl].astype(BF16)
            dkg = dkg + dg

        @pl.when(i == 0)
        def _():
            dqg_ref[...] = dqg
            dkg_ref[...] = dkg

        @pl.when(i > 0)
        def _():
            dqg_ref[...] += dqg
            dkg_ref[...] += dkg

    tab = pl.BlockSpec((TM, CHUNK), lambda i: (i % TPE, 0))
    vec = pl.BlockSpec((1, CHUNK), lambda i: (0, 0))
    hw = pl.BlockSpec((TM, HW), lambda i: (i, 0))
    kvw = pl.BlockSpec((TM, KV * CHUNK), lambda i: (i, 0))
    return pl.pallas_call(
        body, name=name, grid=(T // TM,),
        in_specs=[pl.BlockSpec((TM, cfg.ABW), lambda i: (i, 0)), tab, tab, vec, vec, hw, hw, hw, hw, hw, kvw, kvw],
        out_specs=[pl.BlockSpec((TM, cfg.ABW), lambda i: (i, 0)), vec, vec],
        out_shape=[jax.ShapeDtypeStruct((T, cfg.ABW), BF16), jax.ShapeDtypeStruct((1, CHUNK), F32),
                   jax.ShapeDtypeStruct((1, CHUNK), F32)],
        compiler_params=_params(("arbitrary",)),
    )(p, cosf, sinf, qg, kg, d_rq, d_rk, d_rv, d_gate, d_aq, d_ak, d_av)


def _ret_consts(direction, lg):
    C = CHUNK
    ii = lax.broadcasted_iota(jnp.int32, (C, C), 0)
    jj = lax.broadcasted_iota(jnp.int32, (C, C), 1)
    col = lax.broadcasted_iota(jnp.int32, (C, 1), 0).astype(F32)
    if direction == 0:
        mask, er, ek, eq = ii >= jj, (ii - jj).astype(F32), (C - 1.0) - col, col + 1.0
    else:
        mask, er, ek, eq = jj >= ii, (jj - ii).astype(F32), col, C - col
    er = jnp.where(mask, er, 0.0)
    dm = jnp.where(mask, jnp.exp(er * lg), 0.0)
    return dm, er, jnp.exp(ek * lg), ek, jnp.exp(eq * lg), eq, jnp.exp(C * lg)


def _ret_order(cfg, direction):
    n_all, n_ctx = cfg.S // CHUNK, cfg.SC // CHUNK
    if direction == 0:
        return list(range(n_all))
    return list(range(n_ctx - 1, -1, -1)) + list(range(n_all - 1, n_ctx - 1, -1))


def _carry_begin(carry, c_src, c_dst, sems, step):
    @pl.when(step == 0)
    def _():
        for cp in _carry_copies(carry, c_src, c_dst, *sems):
            cp.start()


def _carry_end(carry, c_src, c_dst, sems, step, n_steps):
    @pl.when(step == n_steps - 1)
    def _():
        for cp in _carry_copies(carry, c_src, c_dst, *sems):
            cp.wait()


def _carry_scratch(nc):
    return [pltpu.SemaphoreType.DMA((3 * nc,)), pltpu.SemaphoreType.DMA((3 * nc,)),
            pltpu.SemaphoreType.DMA((nc,))] if nc else []


def _head_norm_gate(o, g):
    mu = jnp.mean(o, axis=-1, keepdims=True)
    var = jnp.mean(jnp.square(o - mu), axis=-1, keepdims=True)
    rstd = lax.rsqrt(var + EPS)
    y = (o - mu) * rstd
    sg = jax.nn.sigmoid(g)
    return y, rstd, sg


RET_UNROLL = 2


def _retention_fwd(cfg, name, rq, rk, p, lgb, carry=None):
    B, H, S, T = cfg.B, cfg.H, cfg.S, cfg.T
    n_all = S // CHUNK
    nc = len(carry.srcs) if carry is not None else 0

    def body(*refs):
        q_ref, k_ref, v_ref, g_ref, lg_ref = refs[:5]
        c_src = refs[5:5 + nc]
        o_ref, ret_ref, st_ref = refs[5 + nc:8 + nc]
        c_dst = refs[8 + nc:8 + 2 * nc]
        sems = refs[8 + 2 * nc:8 + 2 * nc + 3] if nc else ()
        kv_ref = refs[-1]
        step = pl.program_id(0) * H + pl.program_id(1)
        if nc:
            _carry_begin(carry, c_src, c_dst, sems, step)

        def rows(n):
            return pl.ds(pl.multiple_of(n * CHUNK, CHUNK), CHUNK)

        for direction in (0, 1):
            dm, _, kd, _, qd, _, cd = _ret_consts(direction, lg_ref[direction, 0:1, 0:1])

            def kv_step(n, c):
                kv_ref[n] = _dot_tn((k_ref[rows(n), :] * kd).astype(BF16), v_ref[rows(n), :].astype(BF16))
                return c

            lax.fori_loop(0, n_all, kv_step, 0, unroll=RET_UNROLL)
            order = _ret_order(cfg, direction)
            st = jnp.zeros((CHUNK, CHUNK), F32)
            for t, n in enumerate(order):
                st_ref[direction, n] = st
                if t + 1 < n_all:
                    st = cd * st + kv_ref[n]

            def out_step(n, c):
                q = q_ref[rows(n), :].astype(BF16)
                v = v_ref[rows(n), :].astype(BF16)
                s = _dot_nt(q, k_ref[rows(n), :].astype(BF16)) * dm
                o = _dot(s.astype(BF16), v) + _dot(q, st_ref[direction, n].astype(BF16)) * qd
                if direction == 0:
                    o_ref[rows(n), :] = o
                else:
                    o_ref[rows(n), :] += o
                return c

            lax.fori_loop(0, n_all, out_step, 0, unroll=RET_UNROLL)

        def gate_step(n, c):
            g = g_ref[rows(n), :]
            y, _, sg = _head_norm_gate(o_ref[rows(n), :], g)
            ret_ref[rows(n), :] = (y * (g * sg)).astype(BF16)
            return c

        lax.fori_loop(0, n_all, gate_step, 0, unroll=RET_UNROLL)
        if nc:
            _carry_end(carry, c_src, c_dst, sems, step, B * H)

    HW = H * CHUNK
    blk = lambda off: pl.BlockSpec((S, CHUNK), lambda b, h: (b, off + h))
    st_spec = pl.BlockSpec((None, None, 2, n_all, CHUNK, CHUNK), lambda b, h: (b, h, 0, 0, 0, 0))
    res = pl.pallas_call(
        body, name=name, grid=(B, H),
        in_specs=[blk(0), blk(0), blk(2 * H), blk(3 * H),
                  pl.BlockSpec((None, 2, 8, LANES), lambda b, h: (h, 0, 0, 0))] + [ANY] * nc,
        out_specs=[blk(0), blk(0), st_spec] + [ANY] * nc,
        out_shape=[jax.ShapeDtypeStruct((T, HW), F32), jax.ShapeDtypeStruct((T, 2 * HW), BF16),
                   jax.ShapeDtypeStruct((B, H, 2, n_all, CHUNK, CHUNK), F32)] + (_carry_out_shapes(carry) if nc else []),
        scratch_shapes=_carry_scratch(nc) + [pltpu.VMEM((n_all, CHUNK, CHUNK), F32)],
        compiler_params=_params(("arbitrary", "arbitrary") if nc else ("parallel", "parallel")),
    )(rq, rk, p, p, lgb, *(carry.srcs if nc else ()))
    return res[:3], res[3:]


def _retention_bwd(cfg, name, rq, rk, p, o_sum, states, dcat, lgb, carry=None):
    B, H, S, T = cfg.B, cfg.H, cfg.S, cfg.T
    n_all = S // CHUNK
    C = CHUNK
    nc = len(carry.srcs) if carry is not None else 0

    def body(*refs):
        q_ref, k_ref, v_ref, g_ref, o_ref, st_ref, dr_ref, lg_ref = refs[:8]
        c_src = refs[8:8 + nc]
        dq_ref, dk_ref, dv_ref, dg_ref, dlg_ref = refs[8 + nc:13 + nc]
        c_dst = refs[13 + nc:13 + 2 * nc]
        sems = refs[13 + 2 * nc:13 + 2 * nc + 3] if nc else ()
        do_ref, gq_ref, ds_ref, acc_ref = refs[-4:]
        step = pl.program_id(0) * H + pl.program_id(1)
        if nc:
            _carry_begin(carry, c_src, c_dst, sems, step)

        def rows(n):
            return pl.ds(pl.multiple_of(n * C, C), C)

        def gate_step(n, c):
            g = g_ref[rows(n), :]
            dr = dr_ref[rows(n), :]
            y, rstd, sg = _head_norm_gate(o_ref[rows(n), :], g)
            dy = dr * (g * sg)
            dg_ref[rows(n), :] = dr * y * (sg * (1.0 + g * (1.0 - sg)))
            do_ref[rows(n), :] = rstd * (dy - jnp.mean(dy, axis=-1, keepdims=True)
                                         - y * jnp.mean(dy * y, axis=-1, keepdims=True))
            return c

        lax.fori_loop(0, n_all, gate_step, 0, unroll=RET_UNROLL)

        for direction in (0, 1):
            dm, er, kd, ek, qd, eq, cd = _ret_consts(direction, lg_ref[direction, 0:1, 0:1])

            def gq_step(n, c):
                gq_ref[n] = _dot_tn(q_ref[rows(n), :].astype(BF16), (do_ref[rows(n), :] * qd).astype(BF16))
                return c

            lax.fori_loop(0, n_all, gq_step, 0, unroll=RET_UNROLL)
            order = _ret_order(cfg, direction)
            ds = jnp.zeros((C, C), F32)
            for t in reversed(range(n_all)):
                ds_ref[order[t]] = ds
                if t > 0:
                    ds = cd * ds + gq_ref[order[t]]
            acc_ref[...] = jnp.zeros_like(acc_ref)

            def chunk_step(n, c):
                q = q_ref[rows(n), :].astype(BF16)
                kf = k_ref[rows(n), :]
                k = kf.astype(BF16)
                v = v_ref[rows(n), :].astype(BF16)
                do = do_ref[rows(n), :]
                dob = do.astype(BF16)
                sp = st_ref[direction, n]
                spb = sp.astype(BF16)
                ds = ds_ref[n]
                dsb = ds.astype(BF16)
                dk_state = _dot_nt(v, dsb) * kd
                dv = _dot((kf * kd).astype(BF16), dsb)
                pm = _dot_nt(q, k) * dm
                dpm = _dot_nt(dob, v)
                dsr = (dpm * dm).astype(BF16)
                dq = _dot(dsr, k)
                dk = _dot_tn(dsr, q) + dk_state
                dv = dv + _dot_tn(pm.astype(BF16), dob)
                doq = do * qd
                qs = _dot(q, spb)
                dq = dq + _dot_nt(doq.astype(BF16), spb)
                acc_ref[...] += (jnp.sum(dpm * pm * er, axis=0, keepdims=True)
                                 + jnp.sum(eq * doq * qs, axis=0, keepdims=True)
                                 + jnp.sum(ek * kf * dk_state, axis=0, keepdims=True)
                                 + (C * cd) * jnp.sum(ds * sp, axis=0, keepdims=True))
                if direction == 0:
                    dq_ref[rows(n), :] = dq
                    dk_ref[rows(n), :] = dk
                    dv_ref[rows(n), :] = dv
                else:
                    dq_ref[rows(n), :] += dq
                    dk_ref[rows(n), :] += dk
                    dv_ref[rows(n), :] += dv
                return c

            lax.fori_loop(0, n_all, chunk_step, 0, unroll=RET_UNROLL)
            dlg_ref[direction] = jnp.broadcast_to(jnp.sum(acc_ref[...], axis=1, keepdims=True), (8, LANES))
        if nc:
            _carry_end(carry, c_src, c_dst, sems, step, B * H)

    HW = H * CHUNK
    blk = lambda off: pl.BlockSpec((S, CHUNK), lambda b, h: (b, off + h))
    st_spec = pl.BlockSpec((None, None, 2, n_all, C, C), lambda b, h: (b, h, 0, 0, 0, 0))
    res = pl.pallas_call(
        body, name=name, grid=(B, H),
        in_specs=[blk(0), blk(0), blk(2 * H), blk(3 * H), blk(0), st_spec, blk(0),
                  pl.BlockSpec((None, 2, 8, LANES), lambda b, h: (h, 0, 0, 0))] + [ANY] * nc,
        out_specs=[blk(0)] * 4 + [pl.BlockSpec((None, None, 2, 8, LANES), lambda b, h: (b, h, 0, 0, 0))] + [ANY] * nc,
        out_shape=[jax.ShapeDtypeStruct((T, HW), F32)] * 4 + [jax.ShapeDtypeStruct((B, H, 2, 8, LANES), F32)]
        + (_carry_out_shapes(carry) if nc else []),
        scratch_shapes=_carry_scratch(nc) + [pltpu.VMEM((S, CHUNK), F32), pltpu.VMEM((n_all, C, C), F32),
                                             pltpu.VMEM((n_all, C, C), F32), pltpu.VMEM((1, C), F32)],
        compiler_params=_params(("arbitrary", "arbitrary") if nc else ("parallel", "parallel")),
    )(rq, rk, p, p, o_sum, states, dcat, lgb, *(carry.srcs if nc else ()))
    return res[:5], res[5:]


def _attn_scores(cfg, q, k, t):
    kcol = lax.broadcasted_iota(jnp.int32, (1, cfg.S), 1)
    bias = jnp.where(jnp.logical_or(t > 0, kcol < cfg.SC), 0.0, -1e30)
    s = _dot_nt(q, k) * (CHUNK ** -0.5) + bias
    e = jnp.exp(s - jnp.max(s, axis=-1, keepdims=True))
    return e, 1.0 / jnp.sum(e, axis=-1, keepdims=True)


def _attention_fwd(cfg, name, aq, ak, p, cat, carry=None):
    B, H, KV, S, T, TM, TPE = cfg.B, cfg.H, cfg.KV, cfg.S, cfg.T, cfg.TM, cfg.TPE
    G = H // KV
    v_off = (5 * H + KV)
    nc = len(carry.srcs) if carry is not None else 0

    def body(*refs):
        q_ref, k_ref, v_ref = refs[:3]
        o_ref = refs[4 + nc]
        c_src, c_dst, sems = refs[4:4 + nc], refs[5 + nc:5 + 2 * nc], refs[5 + 2 * nc:]
        step = (pl.program_id(0) * H + pl.program_id(1)) * TPE + pl.program_id(2)
        if nc:
            _carry_begin(carry, c_src, c_dst, sems, step)
        e, inv = _attn_scores(cfg, q_ref[...], k_ref[...], pl.program_id(2))
        o_ref[...] = (_dot(e.astype(BF16), v_ref[...].astype(BF16)) * inv).astype(BF16)
        if nc:
            _carry_end(carry, c_src, c_dst, sems, step, B * H * TPE)

    res = pl.pallas_call(
        body, name=name, grid=(B, H, TPE),
        in_specs=[pl.BlockSpec((TM, CHUNK), lambda b, h, t: (b * TPE + t, h)),
                  pl.BlockSpec((S, CHUNK), lambda b, h, t: (b, h // G)),
                  pl.BlockSpec((S, CHUNK), lambda b, h, t: (b, v_off + h // G)), ANY] + [ANY] * nc,
        out_specs=[pl.BlockSpec((TM, CHUNK), lambda b, h, t: (b * TPE + t, H + h))] + [ANY] * nc,
        out_shape=[jax.ShapeDtypeStruct(cat.shape, cat.dtype)] + (_carry_out_shapes(carry) if nc else []),
        input_output_aliases={3: 0},
        scratch_shapes=_carry_scratch(nc),
        compiler_params=_params(("arbitrary",) * 3 if nc else ("parallel",) * 3),
    )(aq, ak, p, cat, *(carry.srcs if nc else ()))
    return res[0], res[1:]


def _attention_bwd(cfg, name, aq, ak, p, dcat, carry=None):
    B, H, KV, S, T, TM, TPE = cfg.B, cfg.H, cfg.KV, cfg.S, cfg.T, cfg.TM, cfg.TPE
    G = H // KV
    v_off = (5 * H + KV)
    nc = len(carry.srcs) if carry is not None else 0

    def body(*refs):
        q_ref, k_ref, v_ref, do_ref = refs[:4]
        dq_ref, dk_ref, dv_ref = refs[4 + nc:7 + nc]
        c_src, c_dst, sems = refs[4:4 + nc], refs[7 + nc:7 + 2 * nc], refs[7 + 2 * nc:]
        g, t = pl.program_id(2), pl.program_id(3)
        step = ((pl.program_id(0) * KV + pl.program_id(1)) * G + g) * TPE + t
        if nc:
            _carry_begin(carry, c_src, c_dst, sems, step)
        q, k = q_ref[...], k_ref[...]
        v = v_ref[...].astype(BF16)
        dob = do_ref[...].astype(BF16)
        e, inv = _attn_scores(cfg, q, k, t)
        pr = e * inv
        dpr = _dot_nt(dob, v)
        ds = (pr * (dpr - jnp.sum(pr * dpr, axis=-1, keepdims=True)) * (CHUNK ** -0.5)).astype(BF16)
        dq_ref[...] = _dot(ds, k)
        dk = _dot_tn(ds, q)
        dv = _dot_tn(pr.astype(BF16), dob)
        first = jnp.logical_and(g == 0, t == 0)

        @pl.when(first)
        def _():
            dk_ref[...] = dk
            dv_ref[...] = dv

        @pl.when(jnp.logical_not(first))
        def _():
            dk_ref[...] += dk
            dv_ref[...] += dv

        if nc:
            _carry_end(carry, c_src, c_dst, sems, step, B * KV * G * TPE)

    qspec = pl.BlockSpec((TM, CHUNK), lambda b, kv, g, t: (b * TPE + t, kv * G + g))
    kvspec = pl.BlockSpec((S, CHUNK), lambda b, kv, g, t: (b, kv))
    res = pl.pallas_call(
        body, name=name, grid=(B, KV, G, TPE),
        in_specs=[qspec, kvspec, pl.BlockSpec((S, CHUNK), lambda b, kv, g, t: (b, v_off + kv)),
                  pl.BlockSpec((TM, CHUNK), lambda b, kv, g, t: (b * TPE + t, H + kv * G + g))] + [ANY] * nc,
        out_specs=[qspec, kvspec, kvspec] + [ANY] * nc,
        out_shape=[jax.ShapeDtypeStruct((T, H * CHUNK), F32), jax.ShapeDtypeStruct((T, KV * CHUNK), F32),
                   jax.ShapeDtypeStruct((T, KV * CHUNK), F32)] + (_carry_out_shapes(carry) if nc else []),
        scratch_shapes=_carry_scratch(nc),
        compiler_params=_params(("arbitrary",) * 4 if nc else ("parallel", "parallel", "arbitrary", "arbitrary")),
    )(aq, ak, p, dcat, *(carry.srcs if nc else ()))
    return res[:3], res[3:]


_GELU_C = math.sqrt(2.0 / math.pi)


def _gelu(x):
    return 0.5 * x * (1.0 + jnp.tanh(_GELU_C * (x + 0.044715 * x * x * x)))


def _gelu_grad(x):
    th = jnp.tanh(_GELU_C * (x + 0.044715 * x * x * x))
    return 0.5 * (1.0 + th) + 0.5 * x * (1.0 - th * th) * _GELU_C * (1.0 + 3.0 * 0.044715 * x * x)


def _cm_fwd(cfg, name, a, vg, ws, bs):
    T, TM, W, NG = cfg.T, cfg.TM, cfg.CMW, cfg.CMG

    def body(a_ref, vg_ref, ws_ref, bs_ref, m_ref):
        v = _gelu(a_ref[:, pl.ds(W, W)])
        vn = (v * lax.rsqrt(jnp.mean(v * v, axis=-1, keepdims=True) + EPS) * vg_ref[...]).astype(BF16)
        for c in range(TM // CHUNK):
            for g in range(NG):
                rows, cols = slice(c * CHUNK, (c + 1) * CHUNK), slice(g * CHUNK, (g + 1) * CHUNK)
                sv = _dot(ws_ref[g].astype(BF16), vn[rows, cols]) + bs_ref[g]
                u = _gelu(a_ref[pl.ds(c * CHUNK, CHUNK), pl.ds(g * CHUNK, CHUNK)])
                m_ref[pl.ds(c * CHUNK, CHUNK), pl.ds(g * CHUNK, CHUNK)] = (u * sv).astype(BF16)

    return pl.pallas_call(
        body, name=name, grid=(T // TM,),
        in_specs=[pl.BlockSpec((TM, 2 * W), lambda i: (i, 0)), pl.BlockSpec((1, W), lambda i: (0, 0)),
                  pl.BlockSpec((NG, CHUNK, CHUNK), lambda i: (0, 0, 0)),
                  pl.BlockSpec((NG, CHUNK, 1), lambda i: (0, 0, 0))],
        out_specs=pl.BlockSpec((TM, W), lambda i: (i, 0)),
        out_shape=jax.ShapeDtypeStruct((T, W), BF16),
        compiler_params=_params(("parallel",)),
    )(a, vg, ws, bs)


def _cm_bwd(cfg, name, a, vg, ws, bs, dm):
    T, TM, W, NG = cfg.T, cfg.TM, cfg.CMW, cfg.CMG

    def body(a_ref, vg_ref, ws_ref, bs_ref, dm_ref, da_ref, dws_ref, dbs_ref, dvg_ref, dvn_ref):
        i = pl.program_id(0)

        @pl.when(i == 0)
        def _():
            dws_ref[...] = jnp.zeros_like(dws_ref)
            dbs_ref[...] = jnp.zeros_like(dbs_ref)
            dvg_ref[...] = jnp.zeros_like(dvg_ref)

        av = a_ref[:, pl.ds(W, W)]
        v = _gelu(av)
        rstd = lax.rsqrt(jnp.mean(v * v, axis=-1, keepdims=True) + EPS)
        xhat = v * rstd
        vg = vg_ref[...]
        vn = (xhat * vg).astype(BF16)
        for c in range(TM // CHUNK):
            for g in range(NG):
                rows, cols = slice(c * CHUNK, (c + 1) * CHUNK), slice(g * CHUNK, (g + 1) * CHUNK)
                rs, cs = pl.ds(c * CHUNK, CHUNK), pl.ds(g * CHUNK, CHUNK)
                wsb = ws_ref[g].astype(BF16)
                blk = vn[rows, cols]
                sv = _dot(wsb, blk) + bs_ref[g]
                au = a_ref[rs, cs]
                dmb = dm_ref[rs, cs]
                da_ref[rs, cs] = (dmb * sv * _gelu_grad(au)).astype(BF16)
                dsv = dmb * _gelu(au)
                dsvb = dsv.astype(BF16)
                dbs_ref[g] += jnp.sum(dsv, axis=1, keepdims=True)
                dws_ref[g] += _dot_nt(dsvb, blk)
                dvn_ref[rs, cs] = _dot_tn(wsb, dsvb)
        dvn = dvn_ref[...]
        dvg_ref[...] += jnp.sum(dvn * xhat, axis=0, keepdims=True)
        dxh = dvn * vg
        dv = rstd * (dxh - xhat * jnp.mean(dxh * xhat, axis=-1, keepdims=True))
        da_ref[:, pl.ds(W, W)] = (dv * _gelu_grad(av)).astype(BF16)

    return pl.pallas_call(
        body, name=name, grid=(T // TM,),
        in_specs=[pl.BlockSpec((TM, 2 * W), lambda i: (i, 0)), pl.BlockSpec((1, W), lambda i: (0, 0)),
                  pl.BlockSpec((NG, CHUNK, CHUNK), lambda i: (0, 0, 0)),
                  pl.BlockSpec((NG, CHUNK, 1), lambda i: (0, 0, 0)), pl.BlockSpec((TM, W), lambda i: (i, 0))],
        out_specs=[pl.BlockSpec((TM, 2 * W), lambda i: (i, 0)), pl.BlockSpec((NG, CHUNK, CHUNK), lambda i: (0, 0, 0)),
                   pl.BlockSpec((NG, CHUNK, 1), lambda i: (0, 0, 0)), pl.BlockSpec((1, W), lambda i: (0, 0))],
        out_shape=[jax.ShapeDtypeStruct((T, 2 * W), BF16), jax.ShapeDtypeStruct((NG, CHUNK, CHUNK), F32),
                   jax.ShapeDtypeStruct((NG, CHUNK, 1), F32), jax.ShapeDtypeStruct((1, W), F32)],
        scratch_shapes=[pltpu.VMEM((TM, W), F32)],
        compiler_params=_params(("arbitrary",)),
    )(a, vg, ws, bs, dm)


def _layer_weights(l):
    mixer = ("ab_w_in", "ab_w_out") if l % 2 == 0 else ("cm_w_in", "cm_w_out")
    return [(mixer[0], l // 2), (mixer[1], l // 2), ("ff_w1", l), ("ff_w2", l)]


def _local_step(cfg, xcat, tgt, mods, shards, w):
    D, TM, H = cfg.D, cfg.TM, cfg.H
    cosf, sinf = _rope_tables(cfg)
    full, big, recv = {}, {}, {}

    def gather_of(keys):
        return _Carry("gather", tuple(shards[n][i] for n, i in keys), tuple(BIG[n] for n, _ in keys))

    def exchange_of(keys):
        return _Carry("exchange", tuple(big[k] for k in keys), tuple(BIG[n] for n, _ in keys))

    def mm(pending, name, a, b, **kw):
        if not pending:
            return _mm(name, a, b, **kw)
        key, carry, sink = pending.pop(0)
        out, (got,) = _mm(name, a, b, carry=carry, **kw)
        sink[key] = got
        return out

    def with_carry(call, keys, make, sink):
        out, got = call(carry=make(keys) if keys else None)
        sink.update(zip(keys, got))
        return out

    keys0 = _layer_weights(0)
    full[keys0[0]], = _comm_call("gather_weights_0", gather_of(keys0[:1]))
    TG = 3 * TM if cfg.TPE % 3 == 0 else TM
    tiles_per_ex = cfg.S // TG
    gate_spec = pl.BlockSpec((None, 2, 6, D), lambda i, j, k: (i // tiles_per_ex, 0, 0, 0))

    def resid_epi(igate, nxt):
        def epi(acc, row_tile, x_ref, mod_ref, *nxt_refs):
            row = lax.broadcasted_iota(jnp.int32, (TG, 1), 0)
            is_ctx = jnp.logical_and(row_tile % tiles_per_ex == 0, row < cfg.SC)

            def pick(ref, idx):
                return jnp.where(is_ctx, ref[0, pl.ds(idx, 1), :], ref[1, pl.ds(idx, 1), :])

            x = x_ref[...] + pick(mod_ref, igate) * acc
            if nxt is None:
                return x, acc
            gain_ref, modn_ref = nxt_refs
            n = x * lax.rsqrt(jnp.mean(x * x, axis=-1, keepdims=True) + EPS) * gain_ref[...]
            return x, acc, n * (1.0 + pick(modn_ref, nxt[3])) + pick(modn_ref, nxt[2])
        return epi

    def gated_out(pending, name, a, key, x, mod, igate, nxt=None):
        extras = [(x, pl.BlockSpec((TG, D), lambda i, j, k: (i, j))), (mod, gate_spec)]
        if nxt is not None:
            extras += [(nxt[0], pl.BlockSpec((1, D), lambda i, j, k: (0, 0))), (nxt[1], gate_spec)]
        return mm(pending, name, a, full[key], mode="nn", tm=TG, tn=D, outs=[F32, BF16] + [BF16] * (nxt is not None),
                  epi=resid_epi(igate, nxt), extras=extras)

    saved = []
    x = xcat
    h = _norm_mod_fwd(cfg, "norm1_fwd_0", x, w["norm1_g"][0][None], mods[0], 0, 1)
    for l in range(N_LAYERS):
        li = l // 2
        mod = mods[l]
        k_in, k_out, k_ff1, k_ff2 = _layer_weights(l)
        pend = [(k, gather_of([k]), full) for k in _layer_weights(l + 1)] if l + 1 < N_LAYERS else []
        norm2 = (w["norm2_g"][l][None], mod, 3, 4)
        s = {"x0": x, "h": h}
        if l % 2 == 0:
            lgb = jnp.broadcast_to(jax.nn.log_sigmoid(w["ret_decay"][li]).T[:, :, None, None], (H, 2, 8, LANES))
            qg, kg = w["att_q_norm_g"][li][None], w["att_k_norm_g"][li][None]
            s["p"] = mm(pend, f"ab_in_{l}", s["h"], full[k_in], mode="nn", outs=[F32], tn=768)
            s["rq"], s["rk"], s["aq"], s["ak"] = _prep_fwd(cfg, f"prep_fwd_{l}", s["p"], cosf, sinf, qg, kg)
            s["o"], ret, s["st"] = with_carry(
                functools.partial(_retention_fwd, cfg, f"ret_fwd_{l}", s["rq"], s["rk"], s["p"], lgb),
                keys0[1:3] if l == 0 else [], gather_of, full)
            s["cat"] = with_carry(functools.partial(_attention_fwd, cfg, f"att_fwd_{l}", s["aq"], s["ak"], s["p"], ret),
                                  keys0[3:] if l == 0 else [], gather_of, full)
            s["lgb"], s["qg"], s["kg"] = lgb, qg, kg
            x, s["y1"], s["h2"] = gated_out(pend, f"ab_out_{l}", s["cat"], k_out, x, mod, 2, norm2)
        else:
            s["a"] = mm(pend, f"cm_in_{l}", s["h"], full[k_in], mode="nn", outs=[F32])
            s["vg"], s["ws"], s["bs"] = w["cm_v_norm_g"][li][None], w["cm_w_s"][li], w["cm_b_s"][li][:, :, None]
            s["m"] = _cm_fwd(cfg, f"cm_fwd_{l}", s["a"], s["vg"], s["ws"], s["bs"])
            x, s["y1"], s["h2"] = gated_out(pend, f"cm_out_{l}", s["m"], k_out, x, mod, 2, norm2)
        s["x1"] = x
        s["r"] = mm(pend, f"ff1_{l}", s["h2"], full[k_ff1], mode="nn", outs=[BF16],
                    epi=lambda acc, row_tile: (jnp.square(jnp.maximum(acc, 0.0)),))
        if l + 1 < N_LAYERS:
            x, s["y2"], h = gated_out(pend, f"ff2_{l}", s["r"], k_ff2, x, mod, 5,
                                      (w["norm1_g"][l + 1][None], mods[l + 1], 0, 1))
        else:
            x, s["y2"] = gated_out(pend, f"ff2_{l}", s["r"], k_ff2, x, mod, 5)
        saved.append(s)

    loss, dx = _loss_grad(cfg, x, tgt)

    small = {k: [None] * n for k, n in (("norm1_g", 4), ("norm2_g", 4), ("ret_lg", 2), ("att_q_norm_g", 2),
                                        ("att_k_norm_g", 2), ("cm_v_norm_g", 2), ("cm_w_s", 2), ("cm_b_s", 2))}
    dmods = [None] * N_LAYERS

    for l in reversed(range(N_LAYERS)):
        li = l // 2
        s, mod = saved[l], mods[l]
        k_in, k_out, k_ff1, k_ff2 = _layer_weights(l)
        pend = [(k, exchange_of([k]), recv) for k in reversed(_layer_weights(l + 1))] if l + 1 < N_LAYERS else []
        if l == N_LAYERS - 1:
            dy2, dg2 = _gate_bwd(cfg, f"gate2_bwd_{l}", dx, s["y2"], mod, 5)
        da2 = mm(pend, f"ff2_dx_{l}", dy2, full[k_ff2], mode="nt", outs=[BF16],
                 epi=lambda acc, row_tile, r_ref: (acc * (2.0 * jnp.sqrt(r_ref[...].astype(F32))),),
                 extras=[(s["r"], pl.BlockSpec((_tile(cfg.T, 1024), _tile(cfg.FF, 1024)), lambda i, j, k: (i, j)))])
        big[k_ff2] = mm(pend, f"ff2_dw_{l}", s["r"], dy2, mode="tn", outs=[BF16])
        big[k_ff1] = mm(pend, f"ff1_dw_{l}", s["h2"], da2, mode="tn", outs=[BF16])
        dh2 = mm(pend, f"ff1_dx_{l}", da2, full[k_ff1], mode="nt", outs=[F32])
        dx, dm2, small["norm2_g"][l], do, dg1 = _norm_mod_bwd(
            cfg, f"norm2_bwd_{l}", s["x1"], w["norm2_g"][l][None], mod, 3, 4, dh2, dx, gate=(s["y1"], mod, 2))
        if l % 2 == 0:
            big[k_out] = _mm(f"ab_out_dw_{l}", s["cat"], do, mode="tn", outs=[BF16])
            dcat = _mm(f"ab_out_dx_{l}", do, full[k_out], mode="nt", outs=[F32])
            d_rq, d_rk, d_rv, d_gt, dlg = with_carry(
                functools.partial(_retention_bwd, cfg, f"ret_bwd_{l}", s["rq"], s["rk"], s["p"], s["o"], s["st"], dcat,
                                  s["lgb"]), [k_ff2, k_ff1] if l == 0 else [], exchange_of, recv)
            d_aq, d_ak, d_av = with_carry(
                functools.partial(_attention_bwd, cfg, f"att_bwd_{l}", s["aq"], s["ak"], s["p"], dcat),
                [k_out] if l == 0 else [], exchange_of, recv)
            dp, dqg, dkg = _prep_bwd(cfg, f"prep_bwd_{l}", s["p"], cosf, sinf, s["qg"], s["kg"],
                                     d_rq, d_rk, d_rv, d_gt, d_aq, d_ak, d_av)
            small["ret_lg"][li] = jnp.sum(dlg[:, :, :, 0, 0], axis=0).T
            small["att_q_norm_g"][li], small["att_k_norm_g"][li] = dqg[0], dkg[0]
            big[k_in] = _mm(f"ab_in_dw_{l}", s["h"], dp, mode="tn", outs=[BF16])
            last = [(k_in, exchange_of([k_in]), recv)] if l == 0 else []
            dh = mm(last, f"ab_in_dx_{l}", dp, full[k_in], mode="nt", outs=[F32], tk=768)
        else:
            big[k_out] = _mm(f"cm_out_dw_{l}", s["m"], do, mode="tn", outs=[BF16])
            dm = _mm(f"cm_out_dx_{l}", do, full[k_out], mode="nt", outs=[F32])
            da, dws, dbs, dvg = _cm_bwd(cfg, f"cm_bwd_{l}", s["a"], s["vg"], s["ws"], s["bs"], dm)
            small["cm_w_s"][li], small["cm_b_s"][li], small["cm_v_norm_g"][li] = dws, dbs[:, :, 0], dvg[0]
            big[k_in] = _mm(f"cm_in_dw_{l}", s["h"], da, mode="tn", outs=[BF16])
            dh = _mm(f"cm_in_dx_{l}", da, full[k_in], mode="nt", outs=[F32])
        below = (saved[l - 1]["y2"], mods[l - 1], 5) if l > 0 else None
        dx, dm1, small["norm1_g"][l], *rest = _norm_mod_bwd(
            cfg, f"norm1_bwd_{l}", s["x0"], w["norm1_g"][l][None], mod, 0, 1, dh, dx, gate=below)
        dmods[l] = jnp.concatenate([dm1, dg1, dm2, dg2], axis=2)
        if l > 0:
            dy2, dg2 = rest
    return loss, dx, recv, small, dmods


N_DEV = 8
N_CHIP = 4
MESH = pl.DeviceIdType.MESH
ANY = pl.BlockSpec(memory_space=pl.ANY)
BIG = {"ab_w_in": 1, "ab_w_out": 0, "cm_w_in": 1, "cm_w_out": 0, "ff_w1": 1, "ff_w2": 0}


class _Carry(NamedTuple):
    kind: str
    srcs: tuple
    axes: tuple


def _place():
    x, y, c = lax.axis_index("x"), lax.axis_index("y"), lax.axis_index("c")
    return x, y, c, [(1 - x, y), (x, 1 - y), (1 - x, 1 - y)]


def _shard_of(ref, axis, s, width):
    start = pl.multiple_of(s * width, LANES)
    if axis == 0:
        return ref.at[pl.ds(start, width), :]
    return ref.at[:, pl.ds(start, width)]


def _carry_out_shapes(carry):
    shapes = []
    for src, axis in zip(carry.srcs, carry.axes):
        shape = list(src.shape)
        if carry.kind == "gather":
            shape[axis] *= N_CHIP
        else:
            shape[axis] //= N_CHIP
            shape = [N_CHIP] + shape
        shapes.append(jax.ShapeDtypeStruct(tuple(shape), src.dtype))
    return shapes


def _carry_copies(carry, srcs, dsts, send_sems, recv_sems, local_sems):
    x, y, c, chips = _place()
    me = 2 * x + y
    copies = []
    for t, axis in enumerate(carry.axes):
        if carry.kind == "gather":
            own = _shard_of(dsts[t], axis, me, srcs[t].shape[axis])
            copies.append(pltpu.make_async_copy(srcs[t], own, local_sems.at[t]))
            parts = [(srcs[t], own)] * 3
        else:
            width = dsts[t].shape[1 + axis]
            copies.append(pltpu.make_async_copy(_shard_of(srcs[t], axis, me, width), dsts[t].at[3], local_sems.at[t]))
            parts = [(_shard_of(srcs[t], axis, 2 * px + py, width), dsts[t].at[j]) for j, (px, py) in enumerate(chips)]
        for j, (px, py) in enumerate(chips):
            copies.append(pltpu.make_async_remote_copy(
                src_ref=parts[j][0], dst_ref=parts[j][1], send_sem=send_sems.at[3 * t + j],
                recv_sem=recv_sems.at[3 * t + j], device_id=(px, py, c), device_id_type=MESH))
    return copies


def _comm_call(name, carry):
    nc = len(carry.srcs)

    def body(*refs):
        copies = _carry_copies(carry, refs[:nc], refs[nc:2 * nc], *refs[2 * nc:])
        for cp in copies:
            cp.start()
        for cp in copies:
            cp.wait()

    return pl.pallas_call(
        body, name=name, out_shape=_carry_out_shapes(carry), in_specs=[ANY] * nc, out_specs=[ANY] * nc,
        scratch_shapes=[pltpu.SemaphoreType.DMA((3 * nc,)), pltpu.SemaphoreType.DMA((3 * nc,)),
                        pltpu.SemaphoreType.DMA((nc,))],
    )(*carry.srcs)


def _allgather8(name, block):
    m_per, n = block.shape

    def body(x_ref, out_ref, send_sems, recv_sems, local_sem):
        x, y, c, chips = _place()
        me, sibling = (x, y, c), (x, y, 1 - c)

        def rows(px, py, pc):
            return out_ref.at[pl.ds((4 * px + 2 * py + pc) * m_per, m_per), :]

        def copy(k, blk, to, src=None):
            return pltpu.make_async_remote_copy(
                src_ref=rows(*blk) if src is None else src, dst_ref=rows(*blk),
                send_sem=send_sems.at[k], recv_sem=recv_sems.at[k], device_id=to, device_id_type=MESH)

        mine = pltpu.make_async_copy(x_ref, rows(*me), local_sem)
        mine.start()
        first = [copy(0, me, sibling, src=x_ref)]
        first += [copy(1 + j, me, (*chip, c), src=x_ref) for j, chip in enumerate(chips)]
        for cp in first:
            cp.start()
        passed = [copy(4 + j, (*chip, c), sibling) for j, chip in enumerate(chips)]
        for j, chip in enumerate(chips):
            copy(1 + j, (*chip, c), me).wait_recv()
            passed[j].start()
        copy(0, sibling, me).wait_recv()
        for j, chip in enumerate(chips):
            copy(4 + j, (*chip, 1 - c), me).wait_recv()
        for cp in first + passed:
            cp.wait_send()
        mine.wait()

    return pl.pallas_call(
        body, name=name, out_shape=jax.ShapeDtypeStruct((N_DEV * m_per, n), block.dtype),
        in_specs=[pl.BlockSpec(memory_space=pltpu.VMEM)], out_specs=pl.BlockSpec(memory_space=pltpu.VMEM),
        scratch_shapes=[pltpu.SemaphoreType.DMA((7,)), pltpu.SemaphoreType.DMA((7,)), pltpu.SemaphoreType.DMA],
        compiler_params=pltpu.CompilerParams(vmem_limit_bytes=VMEM_LIMIT),
    )(block)


def _swap_sibling(parts):
    n_t = len(parts)

    def body(*refs):
        srcs, outs = refs[:n_t], refs[n_t:2 * n_t]
        send_sems, recv_sems = refs[2 * n_t:]
        x, y, c, _ = _place()
        copies = []
        for t in range(n_t):
            cp = pltpu.make_async_remote_copy(
                src_ref=srcs[t], dst_ref=outs[t], send_sem=send_sems.at[t], recv_sem=recv_sems.at[t],
                device_id=(x, y, 1 - c), device_id_type=MESH)
            cp.start()
            copies.append(cp)
        for cp in copies:
            cp.wait()

    return pl.pallas_call(
        body, name="swap_sibling", out_shape=[jax.ShapeDtypeStruct(p.shape, p.dtype) for p in parts],
        in_specs=[ANY] * n_t, out_specs=[ANY] * n_t,
        scratch_shapes=[pltpu.SemaphoreType.DMA((n_t,)), pltpu.SemaphoreType.DMA((n_t,))],
    )(*parts)


def _rows_view(a):
    if a.ndim == 1:
        return a.reshape(1, a.shape[0])
    return a.reshape(-1, a.shape[-1])


def _row_tile(rows, cols, target_elems=1 << 17):
    tr = rows
    while tr % 16 == 0 and tr * cols > target_elems:
        tr //= 2
    return tr


def _sum_leading(name, a):
    n, rows, cols = a.shape
    tr = _row_tile(rows, cols * n)

    def body(a_ref, o_ref):
        acc = a_ref[0].astype(F32)
        for i in range(1, n):
            acc = acc + a_ref[i].astype(F32)
        o_ref[...] = acc

    return pl.pallas_call(
        body, name=name, grid=(rows // tr,),
        in_specs=[pl.BlockSpec((n, tr, cols), lambda i: (0, i, 0))],
        out_specs=pl.BlockSpec((tr, cols), lambda i: (i, 0)),
        out_shape=jax.ShapeDtypeStruct((rows, cols), F32),
        compiler_params=_params(("parallel",)),
    )(a)


def _silu_rows(name, x):
    def body(x_ref, o_ref):
        v = x_ref[...]
        o_ref[...] = v * jax.nn.sigmoid(v)

    return pl.pallas_call(body, name=name, out_shape=jax.ShapeDtypeStruct(x.shape, F32))(x)


def _silu_bwd_rows(name, x, dy):
    def body(x_ref, dy_ref, o_ref):
        v = x_ref[...]
        sg = jax.nn.sigmoid(v)
        o_ref[...] = dy_ref[...] * (sg * (1.0 + v * (1.0 - sg)))

    return pl.pallas_call(body, name=name, out_shape=jax.ShapeDtypeStruct(x.shape, F32))(x, dy)


def _adamw(name, w, g_parts, m, v):
    shape = w.shape
    w2, m2, v2 = _rows_view(w), _rows_view(m), _rows_view(v)
    gs = [_rows_view(g) for g in g_parts]
    rows, cols = w2.shape
    tr = _row_tile(rows, cols)
    ng = len(gs)

    def body(*refs):
        w_ref, m_ref, v_ref = refs[0], refs[1], refs[2]
        g_refs = refs[3:3 + ng]
        g_out, d_out, m_out, v_out = refs[3 + ng:]
        g = g_refs[0][...]
        for r in g_refs[1:]:
            g = g + r[...]
        m1 = ADAM_B1 * m_ref[...] + (1.0 - ADAM_B1) * g
        v1 = ADAM_B2 * v_ref[...] + (1.0 - ADAM_B2) * jnp.square(g)
        m_hat = m1 / (1.0 - ADAM_B1 ** ADAM_STEP)
        v_hat = v1 / (1.0 - ADAM_B2 ** ADAM_STEP)
        g_out[...] = g
        d_out[...] = -ADAM_LR * (m_hat / (jnp.sqrt(v_hat) + ADAM_EPS) + ADAM_WD * w_ref[...])
        m_out[...] = m1
        v_out[...] = v1

    spec = pl.BlockSpec((tr, cols), lambda i: (i, 0))
    res = pl.pallas_call(
        body, name=name, grid=(rows // tr,), in_specs=[spec] * (3 + ng), out_specs=[spec] * 4,
        out_shape=[jax.ShapeDtypeStruct((rows, cols), F32)] * 4,
        compiler_params=_params(("parallel",)),
    )(w2, m2, v2, *gs)
    return tuple(r.reshape(shape) for r in res)


MOD_ROWS = 48


def kernel(x, c, ctx, c_ctx, mod_w, mod_b, norm1_g, norm2_g, ab_w_in, ab_w_out, ret_decay, att_q_norm_g, att_k_norm_g, cm_w_in, cm_v_norm_g, cm_w_s, cm_b_s, cm_w_out, ff_w1, ff_w2, loss_target, m_c_ctx, m_mod_w, m_mod_b, m_norm1_g, m_norm2_g, m_ab_w_in, m_ab_w_out, m_ret_decay, m_att_q_norm_g, m_att_k_norm_g, m_cm_w_in, m_cm_v_norm_g, m_cm_w_s, m_cm_b_s, m_cm_w_out, m_ff_w1, m_ff_w2, v_c_ctx, v_mod_w, v_mod_b, v_norm1_g, v_norm2_g, v_ab_w_in, v_ab_w_out, v_ret_decay, v_att_q_norm_g, v_att_k_norm_g, v_cm_w_in, v_cm_v_norm_g, v_cm_w_s, v_cm_b_s, v_cm_w_out, v_ff_w1, v_ff_w2):
    B, SL, D = x.shape
    cfg = Cfg(B=B, SC=ctx.shape[1], SL=SL, D=D, FF=ff_w1.shape[2] * N_CHIP)
    L = N_LAYERS
    n_ex = B * N_DEV
    mcols = mod_w.shape[2]
    weights = dict(c_ctx=c_ctx, mod_w=mod_w, mod_b=mod_b, norm1_g=norm1_g, norm2_g=norm2_g, ab_w_in=ab_w_in,
                   ab_w_out=ab_w_out, ret_decay=ret_decay, att_q_norm_g=att_q_norm_g, att_k_norm_g=att_k_norm_g,
                   cm_w_in=cm_w_in, cm_v_norm_g=cm_v_norm_g, cm_w_s=cm_w_s, cm_b_s=cm_b_s, cm_w_out=cm_w_out,
                   ff_w1=ff_w1, ff_w2=ff_w2)
    m_in = dict(c_ctx=m_c_ctx, mod_w=m_mod_w, mod_b=m_mod_b, norm1_g=m_norm1_g, norm2_g=m_norm2_g, ab_w_in=m_ab_w_in,
                ab_w_out=m_ab_w_out, ret_decay=m_ret_decay, att_q_norm_g=m_att_q_norm_g, att_k_norm_g=m_att_k_norm_g,
                cm_w_in=m_cm_w_in, cm_v_norm_g=m_cm_v_norm_g, cm_w_s=m_cm_w_s, cm_b_s=m_cm_b_s, cm_w_out=m_cm_w_out,
                ff_w1=m_ff_w1, ff_w2=m_ff_w2)
    v_in = dict(c_ctx=v_c_ctx, mod_w=v_mod_w, mod_b=v_mod_b, norm1_g=v_norm1_g, norm2_g=v_norm2_g, ab_w_in=v_ab_w_in,
                ab_w_out=v_ab_w_out, ret_decay=v_ret_decay, att_q_norm_g=v_att_q_norm_g, att_k_norm_g=v_att_k_norm_g,
                cm_w_in=v_cm_w_in, cm_v_norm_g=v_cm_v_norm_g, cm_w_s=v_cm_w_s, cm_b_s=v_cm_b_s, cm_w_out=v_cm_w_out,
                ff_w1=v_ff_w1, ff_w2=v_ff_w2)
    xi, yi, ci = lax.axis_index("x"), lax.axis_index("y"), lax.axis_index("c")
    chip = 2 * xi + yi
    dev = 2 * chip + ci

    shards = {n: [weights[n][i].astype(BF16) for i in range(weights[n].shape[0])] for n in BIG}
    vgw = cm_v_norm_g.shape[1]
    blk = jnp.zeros((8, D), F32).at[:B].set(c).at[B:B + 2, :vgw].set(cm_v_norm_g)
    g0 = _allgather8("gather_c", blk).reshape(N_DEV, 8, D)
    c_all = g0[:, :B].reshape(n_ex, D)
    vg_full = jnp.concatenate([g0[2 * s, B:B + 2, :vgw] for s in range(N_CHIP)], axis=-1)

    pre = jnp.zeros((MOD_ROWS, D), F32).at[:n_ex].set(c_all).at[n_ex].set(c_ctx)
    act = _silu_rows("silu_c", pre)
    mpart = jnp.stack([_mm(f"mod_fwd_{l}", act, mod_w, mode="nn", layer=l, outs=[F32], tn=mcols) for l in range(L)])
    g1 = _allgather8("gather_mod", mpart.reshape(L * MOD_ROWS, mcols)).reshape(N_DEV, L, MOD_ROWS, mcols)
    mod_all = jnp.concatenate([g1[2 * s] for s in range(N_CHIP)], axis=-1) + mod_b[:, None, :]
    mod_lat = lax.dynamic_slice_in_dim(mod_all, dev * B, B, axis=1)
    mod_ctx = jnp.broadcast_to(mod_all[:, n_ex][:, None], mod_lat.shape)
    mods = jnp.stack([mod_ctx, mod_lat], axis=2).reshape(L, B, 2, 6, D)

    w = dict(norm1_g=norm1_g, norm2_g=norm2_g, ret_decay=ret_decay, att_q_norm_g=att_q_norm_g,
             att_k_norm_g=att_k_norm_g, cm_v_norm_g=vg_full, cm_w_s=cm_w_s, cm_b_s=cm_b_s)
    xcat = jnp.concatenate([ctx, x], axis=1).reshape(cfg.T, D)
    loss_local, dxcat, recv, small, dmods = _local_step(cfg, xcat, loss_target.reshape(B * SL, D), mods, shards, w)
    loss = lax.psum(loss_local, ("x", "y", "c"))
    grad_x = dxcat.reshape(B, cfg.S, D)[:, cfg.SC:, :]

    part = [jnp.stack([_sum_leading(f"sum_{n}_{i}", recv[(n, i)]) for i in range(weights[n].shape[0])]) for n in BIG]
    other = _swap_sibling(part)
    out = {}
    for n, p_mine, p_other in zip(BIG, part, other):
        out[n] = _adamw(f"adamw_{n}", weights[n], [p_mine, p_other], m_in[n], v_in[n])

    dmod = jnp.stack(dmods).reshape(L, B, 2, 6 * D)
    dmod_lat = dmod[:, :, 1]
    dmod_ctx = jnp.sum(dmod[:, :, 0], axis=1)
    d_ret = jnp.stack(small["ret_lg"]) * jax.nn.sigmoid(-ret_decay)
    summed = [dmod_ctx.reshape(-1), jnp.stack(small["norm1_g"]).reshape(-1), jnp.stack(small["norm2_g"]).reshape(-1),
              jnp.stack(small["cm_v_norm_g"]).reshape(-1), jnp.stack(small["cm_w_s"]).reshape(-1),
              jnp.stack(small["cm_b_s"]).reshape(-1), jnp.stack(small["att_q_norm_g"]).reshape(-1),
              jnp.stack(small["att_k_norm_g"]).reshape(-1), d_ret.reshape(-1)]
    sizes = [int(a.shape[0]) for a in summed]
    flat = jnp.concatenate(summed + [dmod_lat.reshape(-1)])
    n_sum = sum(sizes)
    n_sum_rows = -(-n_sum // D)
    lat_rows = (L * B * 6 * D) // D
    pack_rows = -(-(n_sum_rows + lat_rows) // 8) * 8
    packed = jnp.zeros((pack_rows * D,), F32).at[:n_sum].set(flat[:n_sum])
    packed = packed.at[n_sum_rows * D:(n_sum_rows + lat_rows) * D].set(flat[n_sum:]).reshape(pack_rows, D)
    g2 = _allgather8("gather_small", packed).reshape(N_DEV, pack_rows, D)
    tot = _sum_leading("sum_small", g2[:, :n_sum_rows]).reshape(-1)
    pieces, off = [], 0
    for sz in sizes:
        pieces.append(tot[off:off + sz])
        off += sz
    dmod_ctx_t, g_n1, g_n2, g_vg, g_ws, g_bs, g_qg, g_kg, g_rd = pieces
    dmod_ctx_t = dmod_ctx_t.reshape(L, 6 * D)
    dmod_lat_all = g2[:, n_sum_rows:n_sum_rows + lat_rows].reshape(N_DEV, L, B, 6 * D)
    dmod_rows = jnp.zeros((L, MOD_ROWS, 6 * D), F32)
    dmod_rows = dmod_rows.at[:, :n_ex].set(jnp.transpose(dmod_lat_all, (1, 0, 2, 3)).reshape(L, n_ex, 6 * D))
    dmod_rows = dmod_rows.at[:, n_ex].set(dmod_ctx_t)
    g_mod_b = _sum_leading("sum_mod_b", jnp.transpose(dmod_rows, (1, 0, 2)))
    dmod_mine = lax.dynamic_slice_in_dim(dmod_rows, chip * mcols, mcols, axis=2)
    g_mod_w = jnp.stack([_mm(f"mod_dw_{l}", act, dmod_mine[l], mode="tn", outs=[F32], tn=mcols) for l in range(L)])
    ctx8 = jnp.zeros((L, 8, mcols), F32).at[:, 0].set(dmod_mine[:, n_ex])
    dcc = [_mm(f"mod_dctx_{l}", ctx8[l], mod_w, mode="nt", layer=l, outs=[F32], tk=mcols) for l in range(L)]
    dcc = _sum_leading("sum_dctx_layers", jnp.stack(dcc))
    g3 = _allgather8("gather_dctx", dcc).reshape(N_DEV, 8, D)
    dcc_t = _sum_leading("sum_dctx_chips", g3[0::2])[0:1]
    g_c_ctx = _silu_bwd_rows("silu_bwd_cctx", c_ctx[None], dcc_t)[0]

    vg_mine = lax.dynamic_slice_in_dim(g_vg.reshape(2, -1), chip * vgw, vgw, axis=1)
    small_g = dict(c_ctx=g_c_ctx, mod_w=g_mod_w, mod_b=g_mod_b, norm1_g=g_n1.reshape(norm1_g.shape),
                   norm2_g=g_n2.reshape(norm2_g.shape), ret_decay=g_rd.reshape(ret_decay.shape),
                   att_q_norm_g=g_qg.reshape(att_q_norm_g.shape), att_k_norm_g=g_kg.reshape(att_k_norm_g.shape),
                   cm_v_norm_g=vg_mine, cm_w_s=g_ws.reshape(cm_w_s.shape), cm_b_s=g_bs.reshape(cm_b_s.shape))
    for n, g in small_g.items():
        out[n] = _adamw(f"adamw_{n}", weights[n], [g], m_in[n], v_in[n])

    order = list(weights)
    return (loss, grad_x, *[out[n][0] for n in order], *[out[n][1] for n in order],
            *[out[n][2] for n in order], *[out[n][3] for n in order])
```

```python
import functools
import math
from typing import NamedTuple

import jax
import jax.numpy as jnp
from jax import lax
from jax.experimental import pallas as pl
from jax.experimental.pallas import tpu as pltpu

F32 = jnp.float32
BF16 = jnp.bfloat16
EPS = 1e-6
ROPE_BASE = 10000.0
LANES = 128
CHUNK = 128
N_LAYERS = 4
VMEM_LIMIT = 56 * 1024 * 1024

ADAM_LR = 0.001
ADAM_B1 = 0.9
ADAM_B2 = 0.999
ADAM_EPS = 1e-08
ADAM_WD = 0.01
ADAM_STEP = 10


class Cfg(NamedTuple):
    B: int = 4
    SC: int = 256
    SL: int = 2048
    D: int = 1024
    FF: int = 4096
    GRID_W: int = 64
    H: int = 4
    KV: int = 2
    CMW: int = 1024
    CMG: int = 8

    @property
    def S(self):
        return self.SC + self.SL

    @property
    def T(self):
        return self.B * self.S

    @property
    def TM(self):
        return self.SC

    @property
    def TPE(self):
        return self.S // self.SC

    @property
    def ABW(self):
        return (5 * self.H + 2 * self.KV) * CHUNK


def _tile(dim, pref):
    t = min(dim, pref)
    while dim % t:
        t -= LANES
    return t


def _dot(a, b):
    return lax.dot_general(a, b, (((1,), (0,)), ((), ())), preferred_element_type=F32)


def _dot_nt(a, b):
    return lax.dot_general(a, b, (((1,), (1,)), ((), ())), preferred_element_type=F32)


def _dot_tn(a, b):
    return lax.dot_general(a, b, (((0,), (0,)), ((), ())), preferred_element_type=F32)


def _params(sem, vmem=VMEM_LIMIT):
    return pltpu.CompilerParams(dimension_semantics=sem, vmem_limit_bytes=vmem)


def _mod_index(cfg):
    tpe = cfg.TPE
    return lambda i: (i // tpe, jnp.minimum(i % tpe, 1), 0, 0)


def _mm(name, a, b, *, mode, outs, tm=1024, tn=1024, tk=1024, layer=None, epi=None, extras=(), carry=None):
    bshape = b.shape[1:] if layer is not None else b.shape
    if mode == "nn":
        (M, K), N = a.shape, bshape[1]
    elif mode == "nt":
        (M, K), N = a.shape, bshape[0]
    else:
        (K, M), N = a.shape, bshape[1]
    tm, tn, tk = _tile(M, tm), _tile(N, tn), _tile(K, tk)
    nk = K // tk
    a_spec = (pl.BlockSpec((tk, tm), lambda i, j, k: (k, i)) if mode == "tn"
              else pl.BlockSpec((tm, tk), lambda i, j, k: (i, k)))
    if mode == "nt":
        bblk, bidx = (tn, tk), (lambda i, j, k: (j, k))
    else:
        bblk, bidx = (tk, tn), (lambda i, j, k: (k, j))
    if layer is not None:
        b_spec = pl.BlockSpec((None,) + bblk, lambda i, j, k: (layer,) + bidx(i, j, k))
    else:
        b_spec = pl.BlockSpec(bblk, bidx)
    ne, no = len(extras), len(outs)
    nc = len(carry.srcs) if carry is not None else 0
    dot = {"nn": _dot, "nt": _dot_nt, "tn": _dot_tn}[mode]
    grid = (M // tm, N // tn, nk)

    def body(*refs):
        a_ref, b_ref = refs[0], refs[1]
        ex, out_refs = refs[2:2 + ne], refs[2 + ne + nc:2 + ne + nc + no]
        row_tile = pl.program_id(0)

        if nc:
            step = (pl.program_id(0) * grid[1] + pl.program_id(1)) * grid[2] + pl.program_id(2)
            c_src = refs[2 + ne:2 + ne + nc]
            c_dst = refs[2 + ne + nc + no:2 + ne + 2 * nc + no]
            sems = refs[2 + ne + 2 * nc + no:2 + ne + 2 * nc + no + 3]

            @pl.when(step == 0)
            def _():
                for cp in _carry_copies(carry, c_src, c_dst, *sems):
                    cp.start()

        def finish(acc):
            res = epi(acc, row_tile, *ex) if epi is not None else (acc,)
            for r, o in zip(res, out_refs):
                o[...] = r.astype(o.dtype)

        part = dot(a_ref[...].astype(BF16), b_ref[...].astype(BF16))
        if nk == 1:
            finish(part)
        else:
            acc_ref = refs[-1]
            k = pl.program_id(2)

            @pl.when(k == 0)
            def _():
                acc_ref[...] = part

            @pl.when(k > 0)
            def _():
                acc_ref[...] += part

            @pl.when(k == nk - 1)
            def _():
                finish(acc_ref[...])

        if nc:
            @pl.when(step == grid[0] * grid[1] * grid[2] - 1)
            def _():
                for cp in _carry_copies(carry, c_src, c_dst, *sems):
                    cp.wait()

    scratch = [pltpu.SemaphoreType.DMA((3 * nc,)), pltpu.SemaphoreType.DMA((3 * nc,)),
               pltpu.SemaphoreType.DMA((nc,))] if nc else []
    if nk > 1:
        scratch.append(pltpu.VMEM((tm, tn), F32))
    res = pl.pallas_call(
        body, name=name, grid=grid,
        in_specs=[a_spec, b_spec] + [s for _, s in extras] + [ANY] * nc,
        out_specs=[pl.BlockSpec((tm, tn), lambda i, j, k: (i, j)) for _ in outs] + [ANY] * nc,
        out_shape=[jax.ShapeDtypeStruct((M, N), d) for d in outs] + (_carry_out_shapes(carry) if nc else []),
        scratch_shapes=scratch,
        compiler_params=_params(("arbitrary",) * 3 if nc else ("parallel", "parallel", "arbitrary")),
    )(a, b, *[x for x, _ in extras], *(carry.srcs if nc else ()))
    if nc:
        return (res[0] if no == 1 else res[:no]), res[no:]
    return res[0] if no == 1 else res


def _norm_mod_fwd(cfg, name, x, gain, mod, ish, isc):
    T, D, TM = cfg.T, cfg.D, cfg.TM

    def body(x_ref, g_ref, mod_ref, h_ref):
        x = x_ref[...]
        rstd = lax.rsqrt(jnp.mean(x * x, axis=-1, keepdims=True) + EPS)
        n = x * rstd * g_ref[...]
        h = n * (1.0 + mod_ref[pl.ds(isc, 1), :]) + mod_ref[pl.ds(ish, 1), :]
        h_ref[...] = h.astype(BF16)

    return pl.pallas_call(
        body, name=name, grid=(T // TM,),
        in_specs=[pl.BlockSpec((TM, D), lambda i: (i, 0)), pl.BlockSpec((1, D), lambda i: (0, 0)),
                  pl.BlockSpec((None, None, 6, D), _mod_index(cfg))],
        out_specs=pl.BlockSpec((TM, D), lambda i: (i, 0)),
        out_shape=jax.ShapeDtypeStruct((T, D), BF16),
        compiler_params=_params(("parallel",)),
    )(x, gain, mod)


def _norm_mod_bwd(cfg, name, x, gain, mod, ish, isc, dh, dres, gate=None, lat_only=False):
    T, D, TM, TPE = cfg.T, cfg.D, cfg.TM, cfg.TPE
    ng = 2 if gate is not None else 0
    dx_spec = (pl.BlockSpec((TM, D), lambda i: ((i // TPE) * (TPE - 1) + jnp.maximum(i % TPE - 1, 0), 0)) if lat_only
               else pl.BlockSpec((TM, D), lambda i: (i, 0)))
    dx_rows = cfg.B * cfg.SL if lat_only else T

    def body(*refs):
        x_ref, g_ref, mod_ref, dh_ref, dres_ref = refs[:5]
        dx_ref, dmod_ref, dgain_ref = refs[5 + ng:8 + ng]
        i = pl.program_id(0)
        t = i % TPE
        x = x_ref[...]
        g = g_ref[...]
        dh = dh_ref[...].astype(F32)
        rstd = lax.rsqrt(jnp.mean(x * x, axis=-1, keepdims=True) + EPS)
        xhat = x * rstd
        dn = dh * (1.0 + mod_ref[pl.ds(isc, 1), :])
        dsh = jnp.sum(dh, axis=0, keepdims=True)
        dsc = jnp.sum(dh * (xhat * g), axis=0, keepdims=True)
        dgain = jnp.sum(dn * xhat, axis=0, keepdims=True)
        dxh = dn * g
        dx = rstd * (dxh - xhat * jnp.mean(dxh * xhat, axis=-1, keepdims=True)) + dres_ref[...]
        dx_ref[...] = dx
        sums = [(dmod_ref.at[pl.ds(0, 1), :], dsh), (dmod_ref.at[pl.ds(1, 1), :], dsc)]
        if ng:
            y_ref, gmod_ref = refs[5:7]
            dy_ref, dgate_ref = refs[8 + ng:]
            dy_ref[...] = (dx * gmod_ref[pl.ds(gate[2], 1), :]).astype(BF16)
            sums.append((dgate_ref, jnp.sum(dx * y_ref[...].astype(F32), axis=0, keepdims=True)))

        @pl.when(t <= 1)
        def _():
            for ref, val in sums:
                ref[...] = val

        @pl.when(t > 1)
        def _():
            for ref, val in sums:
                ref[...] += val

        @pl.when(i == 0)
        def _():
            dgain_ref[...] = dgain

        @pl.when(i > 0)
        def _():
            dgain_ref[...] += dgain

    tok = pl.BlockSpec((TM, D), lambda i: (i, 0))
    mod_spec = pl.BlockSpec((None, None, 6, D), _mod_index(cfg))
    res = pl.pallas_call(
        body, name=name, grid=(T // TM,),
        in_specs=[tok, pl.BlockSpec((1, D), lambda i: (0, 0)), mod_spec, tok, tok] + ([tok, mod_spec] if ng else []),
        out_specs=[dx_spec, pl.BlockSpec((None, None, 2, D), _mod_index(cfg)), pl.BlockSpec((1, D), lambda i: (0, 0))]
        + ([tok, pl.BlockSpec((None, None, 1, D), _mod_index(cfg))] if ng else []),
        out_shape=[jax.ShapeDtypeStruct((dx_rows, D), F32), jax.ShapeDtypeStruct((cfg.B, 2, 2, D), F32),
                   jax.ShapeDtypeStruct((1, D), F32)]
        + ([jax.ShapeDtypeStruct((T, D), BF16), jax.ShapeDtypeStruct((cfg.B, 2, 1, D), F32)] if ng else []),
        compiler_params=_params(("arbitrary",)),
    )(x, gain, mod, dh, dres, *(gate[:2] if ng else ()))
    return res


def _gate_bwd(cfg, name, dx, y, mod, igate):
    T, D, TM, TPE = cfg.T, cfg.D, cfg.TM, cfg.TPE

    def body(dx_ref, y_ref, mod_ref, dy_ref, dg_ref):
        t = pl.program_id(0) % TPE
        dx = dx_ref[...]
        dy_ref[...] = (dx * mod_ref[pl.ds(igate, 1), :]).astype(BF16)
        dg = jnp.sum(dx * y_ref[...].astype(F32), axis=0, keepdims=True)

        @pl.when(t <= 1)
        def _():
            dg_ref[...] = dg

        @pl.when(t > 1)
        def _():
            dg_ref[...] += dg

    tok = pl.BlockSpec((TM, D), lambda i: (i, 0))
    return pl.pallas_call(
        body, name=name, grid=(T // TM,),
        in_specs=[tok, tok, pl.BlockSpec((None, None, 6, D), _mod_index(cfg))],
        out_specs=[tok, pl.BlockSpec((None, None, 1, D), _mod_index(cfg))],
        out_shape=[jax.ShapeDtypeStruct((T, D), BF16), jax.ShapeDtypeStruct((cfg.B, 2, 1, D), F32)],
        compiler_params=_params(("arbitrary",)),
    )(dx, y, mod)


def _loss_grad(cfg, x, tgt):
    T, D, TM, TPE = cfg.T, cfg.D, cfg.TM, cfg.TPE

    def body(x_ref, t_ref, dx_ref, loss_ref):
        i = pl.program_id(0)
        t = i % TPE

        @pl.when(i == 0)
        def _():
            loss_ref[...] = jnp.zeros_like(loss_ref)

        @pl.when(t == 0)
        def _():
            dx_ref[...] = jnp.zeros_like(dx_ref)

        @pl.when(t > 0)
        def _():
            err = x_ref[...] - t_ref[...]
            dx_ref[...] = err * (1.0 / D)
            loss_ref[...] += 0.5 * jnp.sum(jnp.mean(err * err, axis=-1, keepdims=True), axis=0, keepdims=True)

    tok = pl.BlockSpec((TM, D), lambda i: (i, 0))
    tgt_spec = pl.BlockSpec((TM, D), lambda i: ((i // TPE) * (TPE - 1) + jnp.maximum(i % TPE - 1, 0), 0))
    dx, loss = pl.pallas_call(
        body, name="loss_grad", grid=(T // TM,),
        in_specs=[tok, tgt_spec], out_specs=[tok, pl.BlockSpec((8, LANES), lambda i: (0, 0))],
        out_shape=[jax.ShapeDtypeStruct((T, D), F32), jax.ShapeDtypeStruct((8, LANES), F32)],
        compiler_params=_params(("arbitrary",)),
    )(x, tgt)
    return loss[0, 0], dx


def _rope_tables(cfg):
    rows = cfg.SL // cfg.GRID_W
    row = jnp.repeat(jnp.arange(rows, dtype=F32), cfg.GRID_W)
    col = jnp.tile(jnp.arange(cfg.GRID_W, dtype=F32), rows)
    n_freq = CHUNK // 4
    inv = ROPE_BASE ** (-jnp.arange(n_freq, dtype=F32) / n_freq)
    ang = jnp.concatenate([row[:, None] * inv[None, :], col[:, None] * inv[None, :]], axis=-1)
    cos, sin = jnp.cos(ang), jnp.sin(ang)
    cosf = jnp.concatenate([jnp.ones((cfg.SC, CHUNK), F32), jnp.concatenate([cos, cos], axis=-1)], axis=0)
    sinf = jnp.concatenate([jnp.zeros((cfg.SC, CHUNK), F32), jnp.concatenate([-sin, sin], axis=-1)], axis=0)
    return cosf, sinf


def _rope(x, cosf, sinf):
    return x * cosf + pltpu.roll(x, CHUNK // 2, 1) * sinf


def _irope(dy, cosf, sinf):
    return dy * cosf - pltpu.roll(dy, CHUNK // 2, 1) * sinf


def _prep_fwd(cfg, name, p, cosf, sinf, qg, kg):
    T, TM, TPE, H, KV = cfg.T, cfg.TM, cfg.TPE, cfg.H, cfg.KV
    HW = H * CHUNK
    kscale = CHUNK ** -0.5

    def body(p_ref, c_ref, s_ref, qg_ref, kg_ref, rq_ref, rk_ref, aq_ref, ak_ref):
        cosf, sinf = c_ref[...], s_ref[...]

        def normed(x, g):
            return x * lax.rsqrt(jnp.mean(x * x, axis=-1, keepdims=True) + EPS) * g

        def seg(col):
            return p_ref[:, pl.ds(col, CHUNK)].astype(F32)

        for h in range(H):
            sl = pl.ds(h * CHUNK, CHUNK)
            rq_ref[:, sl] = _rope(seg(h * CHUNK), cosf, sinf)
            rk_ref[:, sl] = _rope(seg(HW + h * CHUNK), cosf, sinf) * kscale
            aq_ref[:, sl] = _rope(normed(seg(4 * HW + h * CHUNK), qg_ref[...]), cosf, sinf).astype(BF16)
        for h in range(KV):
            ak_ref[:, pl.ds(h * CHUNK, CHUNK)] = _rope(
                normed(seg(5 * HW + h * CHUNK), kg_ref[...]), cosf, sinf).astype(BF16)

    tab = pl.BlockSpec((TM, CHUNK), lambda i: (i % TPE, 0))
    vec = pl.BlockSpec((1, CHUNK), lambda i: (0, 0))
    return pl.pallas_call(
        body, name=name, grid=(T // TM,),
        in_specs=[pl.BlockSpec((TM, cfg.ABW), lambda i: (i, 0)), tab, tab, vec, vec],
        out_specs=[pl.BlockSpec((TM, HW), lambda i: (i, 0))] * 3 + [pl.BlockSpec((TM, KV * CHUNK), lambda i: (i, 0))],
        out_shape=[jax.ShapeDtypeStruct((T, HW), F32), jax.ShapeDtypeStruct((T, HW), F32),
                   jax.ShapeDtypeStruct((T, HW), BF16), jax.ShapeDtypeStruct((T, KV * CHUNK), BF16)],
        compiler_params=_params(("parallel",)),
    )(p, cosf, sinf, qg, kg)


def _prep_bwd(cfg, name, p, cosf, sinf, qg, kg, d_rq, d_rk, d_rv, d_gate, d_aq, d_ak, d_av):
    T, TM, TPE, H, KV = cfg.T, cfg.TM, cfg.TPE, cfg.H, cfg.KV
    HW = H * CHUNK
    kscale = CHUNK ** -0.5

    def body(p_ref, c_ref, s_ref, qg_ref, kg_ref, drq_ref, drk_ref, drv_ref, dgt_ref, daq_ref, dak_ref, dav_ref,
             dp_ref, dqg_ref, dkg_ref):
        i = pl.program_id(0)
        cosf, sinf = c_ref[...], s_ref[...]

        def norm_bwd(x, g, dn):
            rstd = lax.rsqrt(jnp.mean(x * x, axis=-1, keepdims=True) + EPS)
            xhat = x * rstd
            dg = jnp.sum(dn * xhat, axis=0, keepdims=True)
            dxh = dn * g
            return rstd * (dxh - xhat * jnp.mean(dxh * xhat, axis=-1, keepdims=True)), dg

        dqg = jnp.zeros((1, CHUNK), F32)
        dkg = jnp.zeros((1, CHUNK), F32)
        for h in range(H):
            sl = pl.ds(h * CHUNK, CHUNK)
            dp_ref[:, pl.ds(h * CHUNK, CHUNK)] = _irope(drq_ref[:, sl].astype(F32), cosf, sinf).astype(BF16)
            dp_ref[:, pl.ds(HW + h * CHUNK, CHUNK)] = (_irope(drk_ref[:, sl].astype(F32), cosf, sinf)
                                                       * kscale).astype(BF16)
            dp_ref[:, pl.ds(2 * HW + h * CHUNK, CHUNK)] = drv_ref[:, sl].astype(BF16)
            dp_ref[:, pl.ds(3 * HW + h * CHUNK, CHUNK)] = dgt_ref[:, sl].astype(BF16)
            dx, dg = norm_bwd(p_ref[:, pl.ds(4 * HW + h * CHUNK, CHUNK)].astype(F32), qg_ref[...],
                              _irope(daq_ref[:, sl].astype(F32), cosf, sinf))
            dp_ref[:, pl.ds(4 * HW + h * CHUNK, CHUNK)] = dx.astype(BF16)
            dqg = dqg + dg
        for h in range(KV):
            sl = pl.ds(h * CHUNK, CHUNK)
            dx, dg = norm_bwd(p_ref[:, pl.ds(5 * HW + h * CHUNK, CHUNK)].astype(F32), kg_ref[...],
                              _irope(dak_ref[:, sl], cosf, sinf))
            dp_ref[:, pl.ds(5 * HW + h * CHUNK, CHUNK)] = dx.astype(BF16)
            dp_ref[:, pl.ds(5 * HW + (KV + h) * CHUNK, CHUNK)] = dav_ref[:, sl].astype(BF16)
            dkg = dkg + dg

        @pl.when(i == 0)
        def _():
            dqg_ref[...] = dqg
            dkg_ref[...] = dkg

        @pl.when(i > 0)
        def _():
            dqg_ref[...] += dqg
            dkg_ref[...] += dkg

    tab = pl.BlockSpec((TM, CHUNK), lambda i: (i % TPE, 0))
    vec = pl.BlockSpec((1, CHUNK), lambda i: (0, 0))
    hw = pl.BlockSpec((TM, HW), lambda i: (i, 0))
    kvw = pl.BlockSpec((TM, KV * CHUNK), lambda i: (i, 0))
    return pl.pallas_call(
        body, name=name, grid=(T // TM,),
        in_specs=[pl.BlockSpec((TM, cfg.ABW), lambda i: (i, 0)), tab, tab, vec, vec, hw, hw, hw, hw, hw, kvw, kvw],
        out_specs=[pl.BlockSpec((TM, cfg.ABW), lambda i: (i, 0)), vec, vec],
        out_shape=[jax.ShapeDtypeStruct((T, cfg.ABW), BF16), jax.ShapeDtypeStruct((1, CHUNK), F32),
                   jax.ShapeDtypeStruct((1, CHUNK), F32)],
        compiler_params=_params(("arbitrary",)),
    )(p, cosf, sinf, qg, kg, d_rq, d_rk, d_rv, d_gate, d_aq, d_ak, d_av)


def _ret_consts(direction, lg):
    C = CHUNK
    ii = lax.broadcasted_iota(jnp.int32, (C, C), 0)
    jj = lax.broadcasted_iota(jnp.int32, (C, C), 1)
    col = lax.broadcasted_iota(jnp.int32, (C, 1), 0).astype(F32)
    if direction == 0:
        mask, er, ek, eq = ii >= jj, (ii - jj).astype(F32), (C - 1.0) - col, col + 1.0
    else:
        mask, er, ek, eq = jj >= ii, (jj - ii).astype(F32), col, C - col
    er = jnp.where(mask, er, 0.0)
    dm = jnp.where(mask, jnp.exp(er * lg), 0.0)
    return dm, er, jnp.exp(ek * lg), ek, jnp.exp(eq * lg), eq, jnp.exp(C * lg)


def _ret_order(cfg, direction):
    n_all, n_ctx = cfg.S // CHUNK, cfg.SC // CHUNK
    if direction == 0:
        return list(range(n_all))
    return list(range(n_ctx - 1, -1, -1)) + list(range(n_all - 1, n_ctx - 1, -1))


def _carry_begin(carry, c_src, c_dst, sems, step):
    @pl.when(step == 0)
    def _():
        for cp in _carry_copies(carry, c_src, c_dst, *sems):
            cp.start()


def _carry_end(carry, c_src, c_dst, sems, step, n_steps):
    @pl.when(step == n_steps - 1)
    def _():
        for cp in _carry_copies(carry, c_src, c_dst, *sems):
            cp.wait()


def _carry_scratch(nc):
    return [pltpu.SemaphoreType.DMA((3 * nc,)), pltpu.SemaphoreType.DMA((3 * nc,)),
            pltpu.SemaphoreType.DMA((nc,))] if nc else []


def _head_norm_gate(o, g):
    mu = jnp.mean(o, axis=-1, keepdims=True)
    var = jnp.mean(jnp.square(o - mu), axis=-1, keepdims=True)
    rstd = lax.rsqrt(var + EPS)
    y = (o - mu) * rstd
    sg = jax.nn.sigmoid(g)
    return y, rstd, sg


RET_UNROLL = 2


def _retention_fwd(cfg, name, rq, rk, p, lgb, carry=None):
    B, H, S, T = cfg.B, cfg.H, cfg.S, cfg.T
    n_all = S // CHUNK
    nc = len(carry.srcs) if carry is not None else 0

    def body(*refs):
        q_ref, k_ref, v_ref, g_ref, lg_ref = refs[:5]
        c_src = refs[5:5 + nc]
        o_ref, ret_ref, st_ref = refs[5 + nc:8 + nc]
        c_dst = refs[8 + nc:8 + 2 * nc]
        sems = refs[8 + 2 * nc:8 + 2 * nc + 3] if nc else ()
        kv_ref = refs[-1]
        step = pl.program_id(0) * H + pl.program_id(1)
        if nc:
            _carry_begin(carry, c_src, c_dst, sems, step)

        def rows(n):
            return pl.ds(pl.multiple_of(n * CHUNK, CHUNK), CHUNK)

        (dm0, _, kd0, _, qd0, _, cd0), (dm1, _, kd1, _, qd1, _, cd1) = (
            _ret_consts(d, lg_ref[d, 0:1, 0:1]) for d in (0, 1))
        dm_both = dm0 + dm1

        def kv_step(n, c):
            k = k_ref[rows(n), :]
            v = v_ref[rows(n), :].astype(BF16)
            kv_ref[0, n] = _dot_tn((k * kd0).astype(BF16), v)
            kv_ref[1, n] = _dot_tn((k * kd1).astype(BF16), v)
            return c

        lax.fori_loop(0, n_all, kv_step, 0, unroll=RET_UNROLL)
        for direction, cd in ((0, cd0), (1, cd1)):
            st = jnp.zeros((CHUNK, CHUNK), F32)
            for t, n in enumerate(_ret_order(cfg, direction)):
                st_ref[direction, n] = st
                if t + 1 < n_all:
                    st = cd * st + kv_ref[direction, n]

        def out_step(n, c):
            q = q_ref[rows(n), :].astype(BF16)
            v = v_ref[rows(n), :].astype(BF16)
            s = _dot_nt(q, k_ref[rows(n), :].astype(BF16)) * dm_both
            states = jnp.concatenate([st_ref[0, n].astype(BF16), st_ref[1, n].astype(BF16)], axis=1)
            cross = _dot(q, states)
            o = _dot(s.astype(BF16), v) + cross[:, :CHUNK] * qd0 + cross[:, CHUNK:] * qd1
            o_ref[rows(n), :] = o
            g = g_ref[rows(n), :].astype(F32)
            y, _, sg = _head_norm_gate(o, g)
            ret_ref[rows(n), :] = (y * (g * sg)).astype(BF16)
            return c

        lax.fori_loop(0, n_all, out_step, 0, unroll=RET_UNROLL)
        if nc:
            _carry_end(carry, c_src, c_dst, sems, step, B * H)

    HW = H * CHUNK
    blk = lambda off: pl.BlockSpec((S, CHUNK), lambda b, h: (b, off + h))
    st_spec = pl.BlockSpec((None, None, 2, n_all, CHUNK, CHUNK), lambda b, h: (b, h, 0, 0, 0, 0))
    res = pl.pallas_call(
        body, name=name, grid=(B, H),
        in_specs=[blk(0), blk(0), blk(2 * H), blk(3 * H),
                  pl.BlockSpec((None, 2, 8, LANES), lambda b, h: (h, 0, 0, 0))] + [ANY] * nc,
        out_specs=[blk(0), blk(0), st_spec] + [ANY] * nc,
        out_shape=[jax.ShapeDtypeStruct((T, HW), F32), jax.ShapeDtypeStruct((T, 2 * HW), BF16),
                   jax.ShapeDtypeStruct((B, H, 2, n_all, CHUNK, CHUNK), F32)] + (_carry_out_shapes(carry) if nc else []),
        scratch_shapes=_carry_scratch(nc) + [pltpu.VMEM((2, n_all, CHUNK, CHUNK), F32)],
        compiler_params=_params(("arbitrary", "arbitrary") if nc else ("parallel", "parallel")),
    )(rq, rk, p, p, lgb, *(carry.srcs if nc else ()))
    return res[:3], res[3:]


def _retention_bwd(cfg, name, rq, rk, p, o_sum, states, dcat, lgb, carry=None):
    B, H, S, T = cfg.B, cfg.H, cfg.S, cfg.T
    n_all = S // CHUNK
    C = CHUNK
    nc = len(carry.srcs) if carry is not None else 0

    def body(*refs):
        q_ref, k_ref, v_ref, g_ref, o_ref, st_ref, dr_ref, lg_ref = refs[:8]
        c_src = refs[8:8 + nc]
        dq_ref, dk_ref, dv_ref, dg_ref, dlg_ref = refs[8 + nc:13 + nc]
        c_dst = refs[13 + nc:13 + 2 * nc]
        sems = refs[13 + 2 * nc:13 + 2 * nc + 3] if nc else ()
        do_ref, gq_ref, ds_ref, acc_ref = refs[-4:]
        step = pl.program_id(0) * H + pl.program_id(1)
        if nc:
            _carry_begin(carry, c_src, c_dst, sems, step)

        def rows(n):
            return pl.ds(pl.multiple_of(n * C, C), C)

        def gate_step(n, c):
            g = g_ref[rows(n), :].astype(F32)
            dr = dr_ref[rows(n), :].astype(F32)
            y, rstd, sg = _head_norm_gate(o_ref[rows(n), :], g)
            dy = dr * (g * sg)
            dg_ref[rows(n), :] = (dr * y * (sg * (1.0 + g * (1.0 - sg)))).astype(BF16)
            do_ref[rows(n), :] = rstd * (dy - jnp.mean(dy, axis=-1, keepdims=True)
                                         - y * jnp.mean(dy * y, axis=-1, keepdims=True))
            return c

        lax.fori_loop(0, n_all, gate_step, 0, unroll=RET_UNROLL)

        (dm0, er0, kd0, ek0, qd0, eq0, cd0), (dm1, er1, kd1, ek1, qd1, eq1, cd1) = (
            _ret_consts(d, lg_ref[d, 0:1, 0:1]) for d in (0, 1))
        dm_both = dm0 + dm1
        wdm0, wdm1 = dm0 * er0, dm1 * er1

        def side(a, b):
            return jnp.concatenate([a.astype(BF16), b.astype(BF16)], axis=1)

        def gq_step(n, c):
            do = do_ref[rows(n), :]
            gq = _dot_tn(q_ref[rows(n), :].astype(BF16), side(do * qd0, do * qd1))
            gq_ref[0, n] = gq[:, :C]
            gq_ref[1, n] = gq[:, C:]
            return c

        lax.fori_loop(0, n_all, gq_step, 0, unroll=RET_UNROLL)
        for direction, cd in ((0, cd0), (1, cd1)):
            order = _ret_order(cfg, direction)
            ds = jnp.zeros((C, C), F32)
            for t in reversed(range(n_all)):
                ds_ref[direction, order[t]] = ds
                if t > 0:
                    ds = cd * ds + gq_ref[direction, order[t]]
        acc_ref[...] = jnp.zeros_like(acc_ref)

        def chunk_step(n, c):
            q = q_ref[rows(n), :].astype(BF16)
            kf = k_ref[rows(n), :]
            k = kf.astype(BF16)
            v = v_ref[rows(n), :].astype(BF16)
            do = do_ref[rows(n), :]
            dob = do.astype(BF16)
            sp0, sp1 = st_ref[0, n], st_ref[1, n]
            ds0, ds1 = ds_ref[0, n], ds_ref[1, n]
            states = side(sp0, sp1)
            dstates = jnp.concatenate([ds0.astype(BF16), ds1.astype(BF16)], axis=0)
            doq0, doq1 = do * qd0, do * qd1
            doq = side(doq0, doq1)
            s_raw = _dot_nt(q, k)
            dpm = _dot_nt(dob, v)
            dsr = (dpm * dm_both).astype(BF16)
            dks = _dot_nt(v, dstates)
            dks0, dks1 = dks[:, :C] * kd0, dks[:, C:] * kd1
            qs = _dot(q, states)
            dq_ref[rows(n), :] = (_dot(dsr, k) + _dot_nt(doq, states)).astype(BF16)
            dk_ref[rows(n), :] = (_dot_tn(dsr, q) + dks0 + dks1).astype(BF16)
            dv_ref[rows(n), :] = (_dot_tn((s_raw * dm_both).astype(BF16), dob)
                                  + _dot(side(kf * kd0, kf * kd1), dstates)).astype(BF16)
            inner = dpm * s_raw
            acc_ref[0] += (jnp.sum(inner * wdm0, axis=0, keepdims=True)
                           + jnp.sum(eq0 * doq0 * qs[:, :C], axis=0, keepdims=True)
                           + jnp.sum(ek0 * kf * dks0, axis=0, keepdims=True)
                           + (C * cd0) * jnp.sum(ds0 * sp0, axis=0, keepdims=True))
            acc_ref[1] += (jnp.sum(inner * wdm1, axis=0, keepdims=True)
                           + jnp.sum(eq1 * doq1 * qs[:, C:], axis=0, keepdims=True)
                           + jnp.sum(ek1 * kf * dks1, axis=0, keepdims=True)
                           + (C * cd1) * jnp.sum(ds1 * sp1, axis=0, keepdims=True))
            return c

        lax.fori_loop(0, n_all, chunk_step, 0, unroll=RET_UNROLL)
        for direction in (0, 1):
            dlg_ref[direction] = jnp.broadcast_to(jnp.sum(acc_ref[direction], axis=1, keepdims=True), (8, LANES))
        if nc:
            _carry_end(carry, c_src, c_dst, sems, step, B * H)

    HW = H * CHUNK
    blk = lambda off: pl.BlockSpec((S, CHUNK), lambda b, h: (b, off + h))
    st_spec = pl.BlockSpec((None, None, 2, n_all, C, C), lambda b, h: (b, h, 0, 0, 0, 0))
    res = pl.pallas_call(
        body, name=name, grid=(B, H),
        in_specs=[blk(0), blk(0), blk(2 * H), blk(3 * H), blk(0), st_spec, blk(0),
                  pl.BlockSpec((None, 2, 8, LANES), lambda b, h: (h, 0, 0, 0))] + [ANY] * nc,
        out_specs=[blk(0)] * 4 + [pl.BlockSpec((None, None, 2, 8, LANES), lambda b, h: (b, h, 0, 0, 0))] + [ANY] * nc,
        out_shape=[jax.ShapeDtypeStruct((T, HW), BF16)] * 4 + [jax.ShapeDtypeStruct((B, H, 2, 8, LANES), F32)]
        + (_carry_out_shapes(carry) if nc else []),
        scratch_shapes=_carry_scratch(nc) + [pltpu.VMEM((S, CHUNK), F32), pltpu.VMEM((2, n_all, C, C), F32),
                                             pltpu.VMEM((2, n_all, C, C), F32), pltpu.VMEM((2, 1, C), F32)],
        compiler_params=_params(("arbitrary", "arbitrary") if nc else ("parallel", "parallel")),
    )(rq, rk, p, p, o_sum, states, dcat, lgb, *(carry.srcs if nc else ()))
    return res[:5], res[5:]


def _attn_scores(cfg, q, k, t):
    kcol = lax.broadcasted_iota(jnp.int32, (1, cfg.S), 1)
    bias = jnp.where(jnp.logical_or(t > 0, kcol < cfg.SC), 0.0, -1e30)
    s = _dot_nt(q, k) * (CHUNK ** -0.5) + bias
    e = jnp.exp(s - jnp.max(s, axis=-1, keepdims=True))
    return e, 1.0 / jnp.sum(e, axis=-1, keepdims=True)


def _attention_fwd(cfg, name, aq, ak, p, cat, carry=None):
    B, H, KV, S, T, TM, TPE = cfg.B, cfg.H, cfg.KV, cfg.S, cfg.T, cfg.TM, cfg.TPE
    G = H // KV
    v_off = (5 * H + KV)
    nc = len(carry.srcs) if carry is not None else 0

    def body(*refs):
        q_ref, k_ref, v_ref = refs[:3]
        o_ref = refs[4 + nc]
        c_src, c_dst, sems = refs[4:4 + nc], refs[5 + nc:5 + 2 * nc], refs[5 + 2 * nc:]
        step = (pl.program_id(0) * H + pl.program_id(1)) * TPE + pl.program_id(2)
        if nc:
            _carry_begin(carry, c_src, c_dst, sems, step)
        e, inv = _attn_scores(cfg, q_ref[...], k_ref[...], pl.program_id(2))
        o_ref[...] = (_dot(e.astype(BF16), v_ref[...].astype(BF16)) * inv).astype(BF16)
        if nc:
            _carry_end(carry, c_src, c_dst, sems, step, B * H * TPE)

    res = pl.pallas_call(
        body, name=name, grid=(B, H, TPE),
        in_specs=[pl.BlockSpec((TM, CHUNK), lambda b, h, t: (b * TPE + t, h)),
                  pl.BlockSpec((S, CHUNK), lambda b, h, t: (b, h // G)),
                  pl.BlockSpec((S, CHUNK), lambda b, h, t: (b, v_off + h // G)), ANY] + [ANY] * nc,
        out_specs=[pl.BlockSpec((TM, CHUNK), lambda b, h, t: (b * TPE + t, H + h))] + [ANY] * nc,
        out_shape=[jax.ShapeDtypeStruct(cat.shape, cat.dtype)] + (_carry_out_shapes(carry) if nc else []),
        input_output_aliases={3: 0},
        scratch_shapes=_carry_scratch(nc),
        compiler_params=_params(("arbitrary",) * 3 if nc else ("parallel",) * 3),
    )(aq, ak, p, cat, *(carry.srcs if nc else ()))
    return res[0], res[1:]


def _attention_bwd(cfg, name, aq, ak, p, dcat, carry=None):
    B, H, KV, S, T, TM, TPE = cfg.B, cfg.H, cfg.KV, cfg.S, cfg.T, cfg.TM, cfg.TPE
    G = H // KV
    v_off = (5 * H + KV)
    nc = len(carry.srcs) if carry is not None else 0

    def body(*refs):
        q_ref, k_ref, v_ref, do_ref = refs[:4]
        dq_ref, dk_ref, dv_ref = refs[4 + nc:7 + nc]
        c_src, c_dst, sems = refs[4:4 + nc], refs[7 + nc:7 + 2 * nc], refs[7 + 2 * nc:]
        g, t = pl.program_id(2), pl.program_id(3)
        step = ((pl.program_id(0) * KV + pl.program_id(1)) * G + g) * TPE + t
        if nc:
            _carry_begin(carry, c_src, c_dst, sems, step)
        q, k = q_ref[...], k_ref[...]
        v = v_ref[...].astype(BF16)
        dob = do_ref[...].astype(BF16)
        e, inv = _attn_scores(cfg, q, k, t)
        pr = e * inv
        dpr = _dot_nt(dob, v)
        ds = (pr * (dpr - jnp.sum(pr * dpr, axis=-1, keepdims=True)) * (CHUNK ** -0.5)).astype(BF16)
        dq_ref[...] = _dot(ds, k).astype(BF16)
        dk = _dot_tn(ds, q)
        dv = _dot_tn(pr.astype(BF16), dob)
        first = jnp.logical_and(g == 0, t == 0)

        @pl.when(first)
        def _():
            dk_ref[...] = dk
            dv_ref[...] = dv

        @pl.when(jnp.logical_not(first))
        def _():
            dk_ref[...] += dk
            dv_ref[...] += dv

        if nc:
            _carry_end(carry, c_src, c_dst, sems, step, B * KV * G * TPE)

    qspec = pl.BlockSpec((TM, CHUNK), lambda b, kv, g, t: (b * TPE + t, kv * G + g))
    kvspec = pl.BlockSpec((S, CHUNK), lambda b, kv, g, t: (b, kv))
    res = pl.pallas_call(
        body, name=name, grid=(B, KV, G, TPE),
        in_specs=[qspec, kvspec, pl.BlockSpec((S, CHUNK), lambda b, kv, g, t: (b, v_off + kv)),
                  pl.BlockSpec((TM, CHUNK), lambda b, kv, g, t: (b * TPE + t, H + kv * G + g))] + [ANY] * nc,
        out_specs=[qspec, kvspec, kvspec] + [ANY] * nc,
        out_shape=[jax.ShapeDtypeStruct((T, H * CHUNK), BF16), jax.ShapeDtypeStruct((T, KV * CHUNK), F32),
                   jax.ShapeDtypeStruct((T, KV * CHUNK), F32)] + (_carry_out_shapes(carry) if nc else []),
        scratch_shapes=_carry_scratch(nc),
        compiler_params=_params(("arbitrary",) * 4 if nc else ("parallel", "parallel", "arbitrary", "arbitrary")),
    )(aq, ak, p, dcat, *(carry.srcs if nc else ()))
    return res[:3], res[3:]


_GELU_C = math.sqrt(2.0 / math.pi)


def _gelu(x):
    return 0.5 * x * (1.0 + jnp.tanh(_GELU_C * (x + 0.044715 * x * x * x)))


def _gelu_grad(x):
    th = jnp.tanh(_GELU_C * (x + 0.044715 * x * x * x))
    return 0.5 * (1.0 + th) + 0.5 * x * (1.0 - th * th) * _GELU_C * (1.0 + 3.0 * 0.044715 * x * x)


def _cm_fwd(cfg, name, a, vg, ws, bs):
    T, TM, W, NG = cfg.T, cfg.TM, cfg.CMW, cfg.CMG

    def body(a_ref, vg_ref, ws_ref, bs_ref, m_ref):
        v = _gelu(a_ref[:, pl.ds(W, W)].astype(F32))
        vn = (v * lax.rsqrt(jnp.mean(v * v, axis=-1, keepdims=True) + EPS) * vg_ref[...]).astype(BF16)
        for c in range(TM // CHUNK):
            for g in range(NG):
                rows, cols = slice(c * CHUNK, (c + 1) * CHUNK), slice(g * CHUNK, (g + 1) * CHUNK)
                sv = _dot(ws_ref[g].astype(BF16), vn[rows, cols]) + bs_ref[g]
                u = _gelu(a_ref[pl.ds(c * CHUNK, CHUNK), pl.ds(g * CHUNK, CHUNK)].astype(F32))
                m_ref[pl.ds(c * CHUNK, CHUNK), pl.ds(g * CHUNK, CHUNK)] = (u * sv).astype(BF16)

    return pl.pallas_call(
        body, name=name, grid=(T // TM,),
        in_specs=[pl.BlockSpec((TM, 2 * W), lambda i: (i, 0)), pl.BlockSpec((1, W), lambda i: (0, 0)),
                  pl.BlockSpec((NG, CHUNK, CHUNK), lambda i: (0, 0, 0)),
                  pl.BlockSpec((NG, CHUNK, 1), lambda i: (0, 0, 0))],
        out_specs=pl.BlockSpec((TM, W), lambda i: (i, 0)),
        out_shape=jax.ShapeDtypeStruct((T, W), BF16),
        compiler_params=_params(("parallel",)),
    )(a, vg, ws, bs)


def _cm_bwd(cfg, name, a, vg, ws, bs, dm):
    T, TM, W, NG = cfg.T, cfg.TM, cfg.CMW, cfg.CMG

    def body(a_ref, vg_ref, ws_ref, bs_ref, dm_ref, da_ref, dws_ref, dbs_ref, dvg_ref, dvn_ref):
        i = pl.program_id(0)

        @pl.when(i == 0)
        def _():
            dws_ref[...] = jnp.zeros_like(dws_ref)
            dbs_ref[...] = jnp.zeros_like(dbs_ref)
            dvg_ref[...] = jnp.zeros_like(dvg_ref)

        av = a_ref[:, pl.ds(W, W)].astype(F32)
        v = _gelu(av)
        rstd = lax.rsqrt(jnp.mean(v * v, axis=-1, keepdims=True) + EPS)
        xhat = v * rstd
        vg = vg_ref[...]
        vn = (xhat * vg).astype(BF16)
        for c in range(TM // CHUNK):
            for g in range(NG):
                rows, cols = slice(c * CHUNK, (c + 1) * CHUNK), slice(g * CHUNK, (g + 1) * CHUNK)
                rs, cs = pl.ds(c * CHUNK, CHUNK), pl.ds(g * CHUNK, CHUNK)
                wsb = ws_ref[g].astype(BF16)
                blk = vn[rows, cols]
                sv = _dot(wsb, blk) + bs_ref[g]
                au = a_ref[rs, cs].astype(F32)
                dmb = dm_ref[rs, cs].astype(F32)
                da_ref[rs, cs] = (dmb * sv * _gelu_grad(au)).astype(BF16)
                dsv = dmb * _gelu(au)
                dsvb = dsv.astype(BF16)
                dbs_ref[g] += jnp.sum(dsv, axis=1, keepdims=True)
                dws_ref[g] += _dot_nt(dsvb, blk)
                dvn_ref[rs, cs] = _dot_tn(wsb, dsvb)
        dvn = dvn_ref[...]
        dvg_ref[...] += jnp.sum(dvn * xhat, axis=0, keepdims=True)
        dxh = dvn * vg
        dv = rstd * (dxh - xhat * jnp.mean(dxh * xhat, axis=-1, keepdims=True))
        da_ref[:, pl.ds(W, W)] = (dv * _gelu_grad(av)).astype(BF16)

    return pl.pallas_call(
        body, name=name, grid=(T // TM,),
        in_specs=[pl.BlockSpec((TM, 2 * W), lambda i: (i, 0)), pl.BlockSpec((1, W), lambda i: (0, 0)),
                  pl.BlockSpec((NG, CHUNK, CHUNK), lambda i: (0, 0, 0)),
                  pl.BlockSpec((NG, CHUNK, 1), lambda i: (0, 0, 0)), pl.BlockSpec((TM, W), lambda i: (i, 0))],
        out_specs=[pl.BlockSpec((TM, 2 * W), lambda i: (i, 0)), pl.BlockSpec((NG, CHUNK, CHUNK), lambda i: (0, 0, 0)),
                   pl.BlockSpec((NG, CHUNK, 1), lambda i: (0, 0, 0)), pl.BlockSpec((1, W), lambda i: (0, 0))],
        out_shape=[jax.ShapeDtypeStruct((T, 2 * W), BF16), jax.ShapeDtypeStruct((NG, CHUNK, CHUNK), F32),
                   jax.ShapeDtypeStruct((NG, CHUNK, 1), F32), jax.ShapeDtypeStruct((1, W), F32)],
        scratch_shapes=[pltpu.VMEM((TM, W), F32)],
        compiler_params=_params(("arbitrary",)),
    )(a, vg, ws, bs, dm)


def _layer_weights(l):
    mixer = ("ab_w_in", "ab_w_out") if l % 2 == 0 else ("cm_w_in", "cm_w_out")
    return [(mixer[0], l // 2), (mixer[1], l // 2), ("ff_w1", l), ("ff_w2", l)]


def _local_step(cfg, xcat, tgt, mods, shards, w):
    D, TM, H = cfg.D, cfg.TM, cfg.H
    cosf, sinf = _rope_tables(cfg)
    full, big, recv = {}, {}, {}

    def gather_of(keys):
        return _Carry("gather", tuple(shards[n][i] for n, i in keys), tuple(BIG[n] for n, _ in keys))

    def exchange_of(keys):
        return _Carry("exchange", tuple(big[k] for k in keys), tuple(BIG[n] for n, _ in keys))

    def mm(pending, name, a, b, **kw):
        if not pending:
            return _mm(name, a, b, **kw)
        key, carry, sink = pending.pop(0)
        out, (got,) = _mm(name, a, b, carry=carry, **kw)
        sink[key] = got
        return out

    def with_carry(call, keys, make, sink):
        out, got = call(carry=make(keys) if keys else None)
        sink.update(zip(keys, got))
        return out

    keys0 = _layer_weights(0)
    full[keys0[0]], = _comm_call("gather_weights_0", gather_of(keys0[:1]))
    TG = 3 * TM if cfg.TPE % 3 == 0 else TM
    tiles_per_ex = cfg.S // TG
    gate_spec = pl.BlockSpec((None, 2, 6, D), lambda i, j, k: (i // tiles_per_ex, 0, 0, 0))

    def resid_epi(igate, nxt):
        def epi(acc, row_tile, x_ref, mod_ref, *nxt_refs):
            row = lax.broadcasted_iota(jnp.int32, (TG, 1), 0)
            is_ctx = jnp.logical_and(row_tile % tiles_per_ex == 0, row < cfg.SC)

            def pick(ref, idx):
                return jnp.where(is_ctx, ref[0, pl.ds(idx, 1), :], ref[1, pl.ds(idx, 1), :])

            x = x_ref[...] + pick(mod_ref, igate) * acc
            if nxt is None:
                return x, acc
            gain_ref, modn_ref = nxt_refs
            n = x * lax.rsqrt(jnp.mean(x * x, axis=-1, keepdims=True) + EPS) * gain_ref[...]
            return x, acc, n * (1.0 + pick(modn_ref, nxt[3])) + pick(modn_ref, nxt[2])
        return epi

    def gated_out(pending, name, a, key, x, mod, igate, nxt=None):
        extras = [(x, pl.BlockSpec((TG, D), lambda i, j, k: (i, j))), (mod, gate_spec)]
        if nxt is not None:
            extras += [(nxt[0], pl.BlockSpec((1, D), lambda i, j, k: (0, 0))), (nxt[1], gate_spec)]
        return mm(pending, name, a, full[key], mode="nn", tm=TG, tn=D, outs=[F32, BF16] + [BF16] * (nxt is not None),
                  epi=resid_epi(igate, nxt), extras=extras)

    saved = []
    x = xcat
    h = _norm_mod_fwd(cfg, "norm1_fwd_0", x, w["norm1_g"][0][None], mods[0], 0, 1)
    for l in range(N_LAYERS):
        li = l // 2
        mod = mods[l]
        k_in, k_out, k_ff1, k_ff2 = _layer_weights(l)
        pend = [(k, gather_of([k]), full) for k in _layer_weights(l + 1)] if l + 1 < N_LAYERS else []
        norm2 = (w["norm2_g"][l][None], mod, 3, 4)
        s = {"x0": x, "h": h}
        if l % 2 == 0:
            lgb = jnp.broadcast_to(jax.nn.log_sigmoid(w["ret_decay"][li]).T[:, :, None, None], (H, 2, 8, LANES))
            qg, kg = w["att_q_norm_g"][li][None], w["att_k_norm_g"][li][None]
            s["p"] = mm(pend, f"ab_in_{l}", s["h"], full[k_in], mode="nn", outs=[BF16], tn=768)
            s["rq"], s["rk"], s["aq"], s["ak"] = _prep_fwd(cfg, f"prep_fwd_{l}", s["p"], cosf, sinf, qg, kg)
            s["o"], ret, s["st"] = with_carry(
                functools.partial(_retention_fwd, cfg, f"ret_fwd_{l}", s["rq"], s["rk"], s["p"], lgb),
                keys0[1:3] if l == 0 else [], gather_of, full)
            s["cat"] = with_carry(functools.partial(_attention_fwd, cfg, f"att_fwd_{l}", s["aq"], s["ak"], s["p"], ret),
                                  keys0[3:] if l == 0 else [], gather_of, full)
            s["lgb"], s["qg"], s["kg"] = lgb, qg, kg
            x, s["y1"], s["h2"] = gated_out(pend, f"ab_out_{l}", s["cat"], k_out, x, mod, 2, norm2)
        else:
            s["a"] = mm(pend, f"cm_in_{l}", s["h"], full[k_in], mode="nn", outs=[BF16])
            s["vg"], s["ws"], s["bs"] = w["cm_v_norm_g"][li][None], w["cm_w_s"][li], w["cm_b_s"][li][:, :, None]
            s["m"] = _cm_fwd(cfg, f"cm_fwd_{l}", s["a"], s["vg"], s["ws"], s["bs"])
            x, s["y1"], s["h2"] = gated_out(pend, f"cm_out_{l}", s["m"], k_out, x, mod, 2, norm2)
        s["x1"] = x
        s["r"] = mm(pend, f"ff1_{l}", s["h2"], full[k_ff1], mode="nn", outs=[BF16],
                    epi=lambda acc, row_tile: (jnp.square(jnp.maximum(acc, 0.0)),))
        if l + 1 < N_LAYERS:
            x, s["y2"], h = gated_out(pend, f"ff2_{l}", s["r"], k_ff2, x, mod, 5,
                                      (w["norm1_g"][l + 1][None], mods[l + 1], 0, 1))
        else:
            x, s["y2"] = gated_out(pend, f"ff2_{l}", s["r"], k_ff2, x, mod, 5)
        saved.append(s)

    loss, dx = _loss_grad(cfg, x, tgt)

    small = {k: [None] * n for k, n in (("norm1_g", 4), ("norm2_g", 4), ("ret_lg", 2), ("att_q_norm_g", 2),
                                        ("att_k_norm_g", 2), ("cm_v_norm_g", 2), ("cm_w_s", 2), ("cm_b_s", 2))}
    dmods = [None] * N_LAYERS

    for l in reversed(range(N_LAYERS)):
        li = l // 2
        s, mod = saved[l], mods[l]
        k_in, k_out, k_ff1, k_ff2 = _layer_weights(l)
        pend = [(k, exchange_of([k]), recv) for k in reversed(_layer_weights(l + 1))] if l + 1 < N_LAYERS else []
        if l == N_LAYERS - 1:
            dy2, dg2 = _gate_bwd(cfg, f"gate2_bwd_{l}", dx, s["y2"], mod, 5)
        da2 = mm(pend, f"ff2_dx_{l}", dy2, full[k_ff2], mode="nt", outs=[BF16],
                 epi=lambda acc, row_tile, r_ref: (acc * (2.0 * jnp.sqrt(r_ref[...].astype(F32))),),
                 extras=[(s["r"], pl.BlockSpec((_tile(cfg.T, 1024), _tile(cfg.FF, 1024)), lambda i, j, k: (i, j)))])
        big[k_ff2] = mm(pend, f"ff2_dw_{l}", s["r"], dy2, mode="tn", outs=[BF16])
        big[k_ff1] = mm(pend, f"ff1_dw_{l}", s["h2"], da2, mode="tn", outs=[BF16])
        dh2 = mm(pend, f"ff1_dx_{l}", da2, full[k_ff1], mode="nt", outs=[BF16])
        dx, dm2, small["norm2_g"][l], do, dg1 = _norm_mod_bwd(
            cfg, f"norm2_bwd_{l}", s["x1"], w["norm2_g"][l][None], mod, 3, 4, dh2, dx, gate=(s["y1"], mod, 2))
        if l % 2 == 0:
            big[k_out] = _mm(f"ab_out_dw_{l}", s["cat"], do, mode="tn", outs=[BF16])
            dcat = _mm(f"ab_out_dx_{l}", do, full[k_out], mode="nt", outs=[BF16])
            d_rq, d_rk, d_rv, d_gt, dlg = with_carry(
                functools.partial(_retention_bwd, cfg, f"ret_bwd_{l}", s["rq"], s["rk"], s["p"], s["o"], s["st"], dcat,
                                  s["lgb"]), [k_ff2, k_ff1] if l == 0 else [], exchange_of, recv)
            d_aq, d_ak, d_av = with_carry(
                functools.partial(_attention_bwd, cfg, f"att_bwd_{l}", s["aq"], s["ak"], s["p"], dcat),
                [k_out] if l == 0 else [], exchange_of, recv)
            dp, dqg, dkg = _prep_bwd(cfg, f"prep_bwd_{l}", s["p"], cosf, sinf, s["qg"], s["kg"],
                                     d_rq, d_rk, d_rv, d_gt, d_aq, d_ak, d_av)
            small["ret_lg"][li] = jnp.sum(dlg[:, :, :, 0, 0], axis=0).T
            small["att_q_norm_g"][li], small["att_k_norm_g"][li] = dqg[0], dkg[0]
            big[k_in] = _mm(f"ab_in_dw_{l}", s["h"], dp, mode="tn", outs=[BF16])
            last = [(k_in, exchange_of([k_in]), recv)] if l == 0 else []
            dh = mm(last, f"ab_in_dx_{l}", dp, full[k_in], mode="nt", outs=[BF16], tk=768)
        else:
            big[k_out] = _mm(f"cm_out_dw_{l}", s["m"], do, mode="tn", outs=[BF16])
            dm = _mm(f"cm_out_dx_{l}", do, full[k_out], mode="nt", outs=[BF16])
            da, dws, dbs, dvg = _cm_bwd(cfg, f"cm_bwd_{l}", s["a"], s["vg"], s["ws"], s["bs"], dm)
            small["cm_w_s"][li], small["cm_b_s"][li], small["cm_v_norm_g"][li] = dws, dbs[:, :, 0], dvg[0]
            big[k_in] = _mm(f"cm_in_dw_{l}", s["h"], da, mode="tn", outs=[BF16])
            dh = _mm(f"cm_in_dx_{l}", da, full[k_in], mode="nt", outs=[BF16])
        below = (saved[l - 1]["y2"], mods[l - 1], 5) if l > 0 else None
        dx, dm1, small["norm1_g"][l], *rest = _norm_mod_bwd(
            cfg, f"norm1_bwd_{l}", s["x0"], w["norm1_g"][l][None], mod, 0, 1, dh, dx, gate=below, lat_only=l == 0)
        dmods[l] = jnp.concatenate([dm1, dg1, dm2, dg2], axis=2)
        if l > 0:
            dy2, dg2 = rest
    return loss, dx, recv, small, dmods


N_DEV = 8
N_CHIP = 4
MESH = pl.DeviceIdType.MESH
ANY = pl.BlockSpec(memory_space=pl.ANY)
BIG = {"ab_w_in": 1, "ab_w_out": 0, "cm_w_in": 1, "cm_w_out": 0, "ff_w1": 1, "ff_w2": 0}


class _Carry(NamedTuple):
    kind: str
    srcs: tuple
    axes: tuple


def _place():
    x, y, c = lax.axis_index("x"), lax.axis_index("y"), lax.axis_index("c")
    return x, y, c, [(1 - x, y), (x, 1 - y), (1 - x, 1 - y)]


def _shard_of(ref, axis, s, width):
    start = pl.multiple_of(s * width, LANES)
    if axis == 0:
        return ref.at[pl.ds(start, width), :]
    return ref.at[:, pl.ds(start, width)]


def _carry_out_shapes(carry):
    shapes = []
    for src, axis in zip(carry.srcs, carry.axes):
        shape = list(src.shape)
        if carry.kind == "gather":
            shape[axis] *= N_CHIP
        else:
            shape[axis] //= N_CHIP
            shape = [N_CHIP] + shape
        shapes.append(jax.ShapeDtypeStruct(tuple(shape), src.dtype))
    return shapes


def _carry_copies(carry, srcs, dsts, send_sems, recv_sems, local_sems):
    x, y, c, chips = _place()
    me = 2 * x + y
    copies = []
    for t, axis in enumerate(carry.axes):
        if carry.kind == "gather":
            own = _shard_of(dsts[t], axis, me, srcs[t].shape[axis])
            copies.append(pltpu.make_async_copy(srcs[t], own, local_sems.at[t]))
            parts = [(srcs[t], own)] * 3
        else:
            width = dsts[t].shape[1 + axis]
            copies.append(pltpu.make_async_copy(_shard_of(srcs[t], axis, me, width), dsts[t].at[3], local_sems.at[t]))
            parts = [(_shard_of(srcs[t], axis, 2 * px + py, width), dsts[t].at[j]) for j, (px, py) in enumerate(chips)]
        for j, (px, py) in enumerate(chips):
            copies.append(pltpu.make_async_remote_copy(
                src_ref=parts[j][0], dst_ref=parts[j][1], send_sem=send_sems.at[3 * t + j],
                recv_sem=recv_sems.at[3 * t + j], device_id=(px, py, c), device_id_type=MESH))
    return copies


def _comm_call(name, carry):
    nc = len(carry.srcs)

    def body(*refs):
        copies = _carry_copies(carry, refs[:nc], refs[nc:2 * nc], *refs[2 * nc:])
        for cp in copies:
            cp.start()
        for cp in copies:
            cp.wait()

    return pl.pallas_call(
        body, name=name, out_shape=_carry_out_shapes(carry), in_specs=[ANY] * nc, out_specs=[ANY] * nc,
        scratch_shapes=[pltpu.SemaphoreType.DMA((3 * nc,)), pltpu.SemaphoreType.DMA((3 * nc,)),
                        pltpu.SemaphoreType.DMA((nc,))],
    )(*carry.srcs)


def _allgather8(name, block):
    m_per, n = block.shape

    def body(x_ref, out_ref, send_sems, recv_sems, local_sem):
        x, y, c, chips = _place()
        me, sibling = (x, y, c), (x, y, 1 - c)

        def rows(px, py, pc):
            return out_ref.at[pl.ds((4 * px + 2 * py + pc) * m_per, m_per), :]

        def copy(k, blk, to, src=None):
            return pltpu.make_async_remote_copy(
                src_ref=rows(*blk) if src is None else src, dst_ref=rows(*blk),
                send_sem=send_sems.at[k], recv_sem=recv_sems.at[k], device_id=to, device_id_type=MESH)

        mine = pltpu.make_async_copy(x_ref, rows(*me), local_sem)
        mine.start()
        first = [copy(0, me, sibling, src=x_ref)]
        first += [copy(1 + j, me, (*chip, c), src=x_ref) for j, chip in enumerate(chips)]
        for cp in first:
            cp.start()
        passed = [copy(4 + j, (*chip, c), sibling) for j, chip in enumerate(chips)]
        for j, chip in enumerate(chips):
            copy(1 + j, (*chip, c), me).wait_recv()
            passed[j].start()
        copy(0, sibling, me).wait_recv()
        for j, chip in enumerate(chips):
            copy(4 + j, (*chip, 1 - c), me).wait_recv()
        for cp in first + passed:
            cp.wait_send()
        mine.wait()

    return pl.pallas_call(
        body, name=name, out_shape=jax.ShapeDtypeStruct((N_DEV * m_per, n), block.dtype),
        in_specs=[pl.BlockSpec(memory_space=pltpu.VMEM)], out_specs=pl.BlockSpec(memory_space=pltpu.VMEM),
        scratch_shapes=[pltpu.SemaphoreType.DMA((7,)), pltpu.SemaphoreType.DMA((7,)), pltpu.SemaphoreType.DMA],
        compiler_params=pltpu.CompilerParams(vmem_limit_bytes=VMEM_LIMIT),
    )(block)


def _swap_sibling(parts):
    n_t = len(parts)

    def body(*refs):
        srcs, outs = refs[:n_t], refs[n_t:2 * n_t]
        send_sems, recv_sems = refs[2 * n_t:]
        x, y, c, _ = _place()
        copies = []
        for t in range(n_t):
            cp = pltpu.make_async_remote_copy(
                src_ref=srcs[t], dst_ref=outs[t], send_sem=send_sems.at[t], recv_sem=recv_sems.at[t],
                device_id=(x, y, 1 - c), device_id_type=MESH)
            cp.start()
            copies.append(cp)
        for cp in copies:
            cp.wait()

    return pl.pallas_call(
        body, name="swap_sibling", out_shape=[jax.ShapeDtypeStruct(p.shape, p.dtype) for p in parts],
        in_specs=[ANY] * n_t, out_specs=[ANY] * n_t,
        scratch_shapes=[pltpu.SemaphoreType.DMA((n_t,)), pltpu.SemaphoreType.DMA((n_t,))],
    )(*parts)


def _rows_view(a):
    if a.ndim == 1:
        return a.reshape(1, a.shape[0])
    return a.reshape(-1, a.shape[-1])


def _row_tile(rows, cols, target_elems=1 << 17):
    tr = rows
    while tr % 16 == 0 and tr * cols > target_elems:
        tr //= 2
    return tr


def _sum_leading(name, a):
    n, rows, cols = a.shape
    tr = _row_tile(rows, cols * n)

    def body(a_ref, o_ref):
        acc = a_ref[0].astype(F32)
        for i in range(1, n):
            acc = acc + a_ref[i].astype(F32)
        o_ref[...] = acc

    return pl.pallas_call(
        body, name=name, grid=(rows // tr,),
        in_specs=[pl.BlockSpec((n, tr, cols), lambda i: (0, i, 0))],
        out_specs=pl.BlockSpec((tr, cols), lambda i: (i, 0)),
        out_shape=jax.ShapeDtypeStruct((rows, cols), F32),
        compiler_params=_params(("parallel",)),
    )(a)


def _silu_rows(name, x):
    def body(x_ref, o_ref):
        v = x_ref[...]
        o_ref[...] = v * jax.nn.sigmoid(v)

    return pl.pallas_call(body, name=name, out_shape=jax.ShapeDtypeStruct(x.shape, F32))(x)


def _silu_bwd_rows(name, x, dy):
    def body(x_ref, dy_ref, o_ref):
        v = x_ref[...]
        sg = jax.nn.sigmoid(v)
        o_ref[...] = dy_ref[...] * (sg * (1.0 + v * (1.0 - sg)))

    return pl.pallas_call(body, name=name, out_shape=jax.ShapeDtypeStruct(x.shape, F32))(x, dy)


def _adamw(name, w, g_parts, m, v):
    shape = w.shape
    w2, m2, v2 = _rows_view(w), _rows_view(m), _rows_view(v)
    gs = [_rows_view(g) for g in g_parts]
    rows, cols = w2.shape
    tr = _row_tile(rows, cols)
    ng = len(gs)

    def body(*refs):
        w_ref, m_ref, v_ref = refs[0], refs[1], refs[2]
        g_refs = refs[3:3 + ng]
        g_out, d_out, m_out, v_out = refs[3 + ng:]
        g = g_refs[0][...]
        for r in g_refs[1:]:
            g = g + r[...]
        m1 = ADAM_B1 * m_ref[...] + (1.0 - ADAM_B1) * g
        v1 = ADAM_B2 * v_ref[...] + (1.0 - ADAM_B2) * jnp.square(g)
        m_hat = m1 / (1.0 - ADAM_B1 ** ADAM_STEP)
        v_hat = v1 / (1.0 - ADAM_B2 ** ADAM_STEP)
        g_out[...] = g
        d_out[...] = -ADAM_LR * (m_hat / (jnp.sqrt(v_hat) + ADAM_EPS) + ADAM_WD * w_ref[...])
        m_out[...] = m1
        v_out[...] = v1

    spec = pl.BlockSpec((tr, cols), lambda i: (i, 0))
    res = pl.pallas_call(
        body, name=name, grid=(rows // tr,), in_specs=[spec] * (3 + ng), out_specs=[spec] * 4,
        out_shape=[jax.ShapeDtypeStruct((rows, cols), F32)] * 4,
        compiler_params=_params(("parallel",)),
    )(w2, m2, v2, *gs)
    return tuple(r.reshape(shape) for r in res)


MOD_ROWS = 48


def kernel(x, c, ctx, c_ctx, mod_w, mod_b, norm1_g, norm2_g, ab_w_in, ab_w_out, ret_decay, att_q_norm_g, att_k_norm_g, cm_w_in, cm_v_norm_g, cm_w_s, cm_b_s, cm_w_out, ff_w1, ff_w2, loss_target, m_c_ctx, m_mod_w, m_mod_b, m_norm1_g, m_norm2_g, m_ab_w_in, m_ab_w_out, m_ret_decay, m_att_q_norm_g, m_att_k_norm_g, m_cm_w_in, m_cm_v_norm_g, m_cm_w_s, m_cm_b_s, m_cm_w_out, m_ff_w1, m_ff_w2, v_c_ctx, v_mod_w, v_mod_b, v_norm1_g, v_norm2_g, v_ab_w_in, v_ab_w_out, v_ret_decay, v_att_q_norm_g, v_att_k_norm_g, v_cm_w_in, v_cm_v_norm_g, v_cm_w_s, v_cm_b_s, v_cm_w_out, v_ff_w1, v_ff_w2):
    B, SL, D = x.shape
    cfg = Cfg(B=B, SC=ctx.shape[1], SL=SL, D=D, FF=ff_w1.shape[2] * N_CHIP)
    L = N_LAYERS
    n_ex = B * N_DEV
    mcols = mod_w.shape[2]
    weights = dict(c_ctx=c_ctx, mod_w=mod_w, mod_b=mod_b, norm1_g=norm1_g, norm2_g=norm2_g, ab_w_in=ab_w_in,
                   ab_w_out=ab_w_out, ret_decay=ret_decay, att_q_norm_g=att_q_norm_g, att_k_norm_g=att_k_norm_g,
                   cm_w_in=cm_w_in, cm_v_norm_g=cm_v_norm_g, cm_w_s=cm_w_s, cm_b_s=cm_b_s, cm_w_out=cm_w_out,
                   ff_w1=ff_w1, ff_w2=ff_w2)
    m_in = dict(c_ctx=m_c_ctx, mod_w=m_mod_w, mod_b=m_mod_b, norm1_g=m_norm1_g, norm2_g=m_norm2_g, ab_w_in=m_ab_w_in,
                ab_w_out=m_ab_w_out, ret_decay=m_ret_decay, att_q_norm_g=m_att_q_norm_g, att_k_norm_g=m_att_k_norm_g,
                cm_w_in=m_cm_w_in, cm_v_norm_g=m_cm_v_norm_g, cm_w_s=m_cm_w_s, cm_b_s=m_cm_b_s, cm_w_out=m_cm_w_out,
                ff_w1=m_ff_w1, ff_w2=m_ff_w2)
    v_in = dict(c_ctx=v_c_ctx, mod_w=v_mod_w, mod_b=v_mod_b, norm1_g=v_norm1_g, norm2_g=v_norm2_g, ab_w_in=v_ab_w_in,
                ab_w_out=v_ab_w_out, ret_decay=v_ret_decay, att_q_norm_g=v_att_q_norm_g, att_k_norm_g=v_att_k_norm_g,
                cm_w_in=v_cm_w_in, cm_v_norm_g=v_cm_v_norm_g, cm_w_s=v_cm_w_s, cm_b_s=v_cm_b_s, cm_w_out=v_cm_w_out,
                ff_w1=v_ff_w1, ff_w2=v_ff_w2)
    xi, yi, ci = lax.axis_index("x"), lax.axis_index("y"), lax.axis_index("c")
    chip = 2 * xi + yi
    dev = 2 * chip + ci

    shards = {n: [weights[n][i].astype(BF16) for i in range(weights[n].shape[0])] for n in BIG}
    vgw = cm_v_norm_g.shape[1]
    blk = jnp.zeros((8, D), F32).at[:B].set(c).at[B:B + 2, :vgw].set(cm_v_norm_g)
    g0 = _allgather8("gather_c", blk).reshape(N_DEV, 8, D)
    c_all = g0[:, :B].reshape(n_ex, D)
    vg_full = jnp.concatenate([g0[2 * s, B:B + 2, :vgw] for s in range(N_CHIP)], axis=-1)

    pre = jnp.zeros((MOD_ROWS, D), F32).at[:n_ex].set(c_all).at[n_ex].set(c_ctx)
    act = _silu_rows("silu_c", pre)
    mpart = jnp.stack([_mm(f"mod_fwd_{l}", act, mod_w, mode="nn", layer=l, outs=[F32], tn=mcols) for l in range(L)])
    g1 = _allgather8("gather_mod", mpart.reshape(L * MOD_ROWS, mcols)).reshape(N_DEV, L, MOD_ROWS, mcols)
    mod_all = jnp.concatenate([g1[2 * s] for s in range(N_CHIP)], axis=-1) + mod_b[:, None, :]
    mod_lat = lax.dynamic_slice_in_dim(mod_all, dev * B, B, axis=1)
    mod_ctx = jnp.broadcast_to(mod_all[:, n_ex][:, None], mod_lat.shape)
    mods = jnp.stack([mod_ctx, mod_lat], axis=2).reshape(L, B, 2, 6, D)

    w = dict(norm1_g=norm1_g, norm2_g=norm2_g, ret_decay=ret_decay, att_q_norm_g=att_q_norm_g,
             att_k_norm_g=att_k_norm_g, cm_v_norm_g=vg_full, cm_w_s=cm_w_s, cm_b_s=cm_b_s)
    xcat = jnp.concatenate([ctx, x], axis=1).reshape(cfg.T, D)
    loss_local, dx_lat, recv, small, dmods = _local_step(cfg, xcat, loss_target.reshape(B * SL, D), mods, shards, w)
    loss = lax.psum(loss_local, ("x", "y", "c"))
    grad_x = dx_lat.reshape(B, SL, D)

    part = [jnp.stack([_sum_leading(f"sum_{n}_{i}", recv[(n, i)]) for i in range(weights[n].shape[0])]) for n in BIG]
    other = _swap_sibling(part)
    out = {}
    for n, p_mine, p_other in zip(BIG, part, other):
        out[n] = _adamw(f"adamw_{n}", weights[n], [p_mine, p_other], m_in[n], v_in[n])

    dmod = jnp.stack(dmods).reshape(L, B, 2, 6 * D)
    dmod_lat = dmod[:, :, 1]
    dmod_ctx = jnp.sum(dmod[:, :, 0], axis=1)
    d_ret = jnp.stack(small["ret_lg"]) * jax.nn.sigmoid(-ret_decay)
    summed = [dmod_ctx.reshape(-1), jnp.stack(small["norm1_g"]).reshape(-1), jnp.stack(small["norm2_g"]).reshape(-1),
              jnp.stack(small["cm_v_norm_g"]).reshape(-1), jnp.stack(small["cm_w_s"]).reshape(-1),
              jnp.stack(small["cm_b_s"]).reshape(-1), jnp.stack(small["att_q_norm_g"]).reshape(-1),
              jnp.stack(small["att_k_norm_g"]).reshape(-1), d_ret.reshape(-1)]
    sizes = [int(a.shape[0]) for a in summed]
    flat = jnp.concatenate(summed + [dmod_lat.reshape(-1)])
    n_sum = sum(sizes)
    n_sum_rows = -(-n_sum // D)
    lat_rows = (L * B * 6 * D) // D
    pack_rows = -(-(n_sum_rows + lat_rows) // 8) * 8
    packed = jnp.zeros((pack_rows * D,), F32).at[:n_sum].set(flat[:n_sum])
    packed = packed.at[n_sum_rows * D:(n_sum_rows + lat_rows) * D].set(flat[n_sum:]).reshape(pack_rows, D)
    g2 = _allgather8("gather_small", packed).reshape(N_DEV, pack_rows, D)
    tot = _sum_leading("sum_small", g2[:, :n_sum_rows]).reshape(-1)
    pieces, off = [], 0
    for sz in sizes:
        pieces.append(tot[off:off + sz])
        off += sz
    dmod_ctx_t, g_n1, g_n2, g_vg, g_ws, g_bs, g_qg, g_kg, g_rd = pieces
    dmod_ctx_t = dmod_ctx_t.reshape(L, 6 * D)
    dmod_lat_all = g2[:, n_sum_rows:n_sum_rows + lat_rows].reshape(N_DEV, L, B, 6 * D)
    dmod_rows = jnp.zeros((L, MOD_ROWS, 6 * D), F32)
    dmod_rows = dmod_rows.at[:, :n_ex].set(jnp.transpose(dmod_lat_all, (1, 0, 2, 3)).reshape(L, n_ex, 6 * D))
    dmod_rows = dmod_rows.at[:, n_ex].set(dmod_ctx_t)
    g_mod_b = _sum_leading("sum_mod_b", jnp.transpose(dmod_rows, (1, 0, 2)))
    dmod_mine = lax.dynamic_slice_in_dim(dmod_rows, chip * mcols, mcols, axis=2)
    g_mod_w = jnp.stack([_mm(f"mod_dw_{l}", act, dmod_mine[l], mode="tn", outs=[F32], tn=mcols) for l in range(L)])
    ctx8 = jnp.zeros((L, 8, mcols), F32).at[:, 0].set(dmod_mine[:, n_ex])
    dcc = [_mm(f"mod_dctx_{l}", ctx8[l], mod_w, mode="nt", layer=l, outs=[F32], tk=mcols) for l in range(L)]
    dcc = _sum_leading("sum_dctx_layers", jnp.stack(dcc))
    g3 = _allgather8("gather_dctx", dcc).reshape(N_DEV, 8, D)
    dcc_t = _sum_leading("sum_dctx_chips", g3[0::2])[0:1]
    g_c_ctx = _silu_bwd_rows("silu_bwd_cctx", c_ctx[None], dcc_t)[0]

    vg_mine = lax.dynamic_slice_in_dim(g_vg.reshape(2, -1), chip * vgw, vgw, axis=1)
    small_g = dict(c_ctx=g_c_ctx, mod_w=g_mod_w, mod_b=g_mod_b, norm1_g=g_n1.reshape(norm1_g.shape),
                   norm2_g=g_n2.reshape(norm2_g.shape), ret_decay=g_rd.reshape(ret_decay.shape),
                   att_q_norm_g=g_qg.reshape(att_q_norm_g.shape), att_k_norm_g=g_kg.reshape(att_k_norm_g.shape),
                   cm_v_norm_g=vg_mine, cm_w_s=g_ws.reshape(cm_w_s.shape), cm_b_s=g_bs.reshape(cm_b_s.shape))
    for n, g in small_g.items():
        out[n] = _adamw(f"adamw_{n}", weights[n], [g], m_in[n], v_in[n])

    order = list(weights)
    return (loss, grad_x, *[out[n][0] for n in order], *[out[n][1] for n in order],
            *[out[n][2] for n in order], *[out[n][3] for n in order])
```

```python
import functools
import math
from typing import NamedTuple

import jax
import jax.numpy as jnp
from jax import lax
from jax.experimental import pallas as pl
from jax.experimental.pallas import tpu as pltpu

F32 = jnp.float32
BF16 = jnp.bfloat16
EPS = 1e-6
ROPE_BASE = 10000.0
LANES = 128
CHUNK = 128
N_LAYERS = 4
VMEM_LIMIT = 56 * 1024 * 1024

ADAM_LR = 0.001
ADAM_B1 = 0.9
ADAM_B2 = 0.999
ADAM_EPS = 1e-08
ADAM_WD = 0.01
ADAM_STEP = 10


class Cfg(NamedTuple):
    B: int = 4
    SC: int = 256
    SL: int = 2048
    D: int = 1024
    FF: int = 4096
    GRID_W: int = 64
    H: int = 4
    KV: int = 2
    CMW: int = 1024
    CMG: int = 8

    @property
    def S(self):
        return self.SC + self.SL

    @property
    def T(self):
        return self.B * self.S

    @property
    def TM(self):
        return self.SC

    @property
    def TPE(self):
        return self.S // self.SC

    @property
    def ABW(self):
        return (5 * self.H + 2 * self.KV) * CHUNK


def _tile(dim, pref):
    t = min(dim, pref)
    while dim % t:
        t -= LANES
    return t


def _dot(a, b):
    return lax.dot_general(a, b, (((1,), (0,)), ((), ())), preferred_element_type=F32)


def _dot_nt(a, b):
    return lax.dot_general(a, b, (((1,), (1,)), ((), ())), preferred_element_type=F32)


def _dot_tn(a, b):
    return lax.dot_general(a, b, (((0,), (0,)), ((), ())), preferred_element_type=F32)


def _params(sem, vmem=VMEM_LIMIT):
    return pltpu.CompilerParams(dimension_semantics=sem, vmem_limit_bytes=vmem)


def _mod_index(cfg):
    tpe = cfg.TPE
    return lambda i: (i // tpe, jnp.minimum(i % tpe, 1), 0, 0)


def _mm(name, a, b, *, mode, outs, tm=1024, tn=1024, tk=1024, layer=None, epi=None, extras=(), carry=None):
    bshape = b.shape[1:] if layer is not None else b.shape
    if mode == "nn":
        (M, K), N = a.shape, bshape[1]
    elif mode == "nt":
        (M, K), N = a.shape, bshape[0]
    else:
        (K, M), N = a.shape, bshape[1]
    tm, tn, tk = _tile(M, tm), _tile(N, tn), _tile(K, tk)
    nk = K // tk
    a_spec = (pl.BlockSpec((tk, tm), lambda i, j, k: (k, i)) if mode == "tn"
              else pl.BlockSpec((tm, tk), lambda i, j, k: (i, k)))
    if mode == "nt":
        bblk, bidx = (tn, tk), (lambda i, j, k: (j, k))
    else:
        bblk, bidx = (tk, tn), (lambda i, j, k: (k, j))
    if layer is not None:
        b_spec = pl.BlockSpec((None,) + bblk, lambda i, j, k: (layer,) + bidx(i, j, k))
    else:
        b_spec = pl.BlockSpec(bblk, bidx)
    ne, no = len(extras), len(outs)
    nc = len(carry.srcs) if carry is not None else 0
    dot = {"nn": _dot, "nt": _dot_nt, "tn": _dot_tn}[mode]
    grid = (M // tm, N // tn, nk)

    def body(*refs):
        a_ref, b_ref = refs[0], refs[1]
        ex, out_refs = refs[2:2 + ne], refs[2 + ne + nc:2 + ne + nc + no]
        row_tile = pl.program_id(0)

        if nc:
            step = (pl.program_id(0) * grid[1] + pl.program_id(1)) * grid[2] + pl.program_id(2)
            c_src = refs[2 + ne:2 + ne + nc]
            c_dst = refs[2 + ne + nc + no:2 + ne + 2 * nc + no]
            sems = refs[2 + ne + 2 * nc + no:2 + ne + 2 * nc + no + 3]

            @pl.when(step == 0)
            def _():
                for cp in _carry_copies(carry, c_src, c_dst, *sems):
                    cp.start()

        def finish(acc):
            res = epi(acc, row_tile, *ex) if epi is not None else (acc,)
            for r, o in zip(res, out_refs):
                o[...] = r.astype(o.dtype)

        part = dot(a_ref[...].astype(BF16), b_ref[...].astype(BF16))
        if nk == 1:
            finish(part)
        else:
            acc_ref = refs[-1]
            k = pl.program_id(2)

            @pl.when(k == 0)
            def _():
                acc_ref[...] = part

            @pl.when(k > 0)
            def _():
                acc_ref[...] += part

            @pl.when(k == nk - 1)
            def _():
                finish(acc_ref[...])

        if nc:
            @pl.when(step == grid[0] * grid[1] * grid[2] - 1)
            def _():
                for cp in _carry_copies(carry, c_src, c_dst, *sems):
                    cp.wait()

    scratch = [pltpu.SemaphoreType.DMA((3 * nc,)), pltpu.SemaphoreType.DMA((3 * nc,)),
               pltpu.SemaphoreType.DMA((nc,))] if nc else []
    if nk > 1:
        scratch.append(pltpu.VMEM((tm, tn), F32))
    res = pl.pallas_call(
        body, name=name, grid=grid,
        in_specs=[a_spec, b_spec] + [s for _, s in extras] + [ANY] * nc,
        out_specs=[pl.BlockSpec((tm, tn), lambda i, j, k: (i, j)) for _ in outs] + [ANY] * nc,
        out_shape=[jax.ShapeDtypeStruct((M, N), d) for d in outs] + (_carry_out_shapes(carry) if nc else []),
        scratch_shapes=scratch,
        compiler_params=_params(("arbitrary",) * 3 if nc else ("parallel", "parallel", "arbitrary")),
    )(a, b, *[x for x, _ in extras], *(carry.srcs if nc else ()))
    if nc:
        return (res[0] if no == 1 else res[:no]), res[no:]
    return res[0] if no == 1 else res


def _norm_mod_fwd(cfg, name, x, gain, mod, ish, isc):
    T, D, TM = cfg.T, cfg.D, cfg.TM

    def body(x_ref, g_ref, mod_ref, h_ref):
        x = x_ref[...]
        rstd = lax.rsqrt(jnp.mean(x * x, axis=-1, keepdims=True) + EPS)
        n = x * rstd * g_ref[...]
        h = n * (1.0 + mod_ref[pl.ds(isc, 1), :]) + mod_ref[pl.ds(ish, 1), :]
        h_ref[...] = h.astype(BF16)

    return pl.pallas_call(
        body, name=name, grid=(T // TM,),
        in_specs=[pl.BlockSpec((TM, D), lambda i: (i, 0)), pl.BlockSpec((1, D), lambda i: (0, 0)),
                  pl.BlockSpec((None, None, 6, D), _mod_index(cfg))],
        out_specs=pl.BlockSpec((TM, D), lambda i: (i, 0)),
        out_shape=jax.ShapeDtypeStruct((T, D), BF16),
        compiler_params=_params(("parallel",)),
    )(x, gain, mod)


def _norm_mod_bwd(cfg, name, x, gain, mod, ish, isc, dh, dres, gate=None, lat_only=False):
    T, D, TM, TPE = cfg.T, cfg.D, cfg.TM, cfg.TPE
    ng = 2 if gate is not None else 0
    dx_spec = (pl.BlockSpec((TM, D), lambda i: ((i // TPE) * (TPE - 1) + jnp.maximum(i % TPE - 1, 0), 0)) if lat_only
               else pl.BlockSpec((TM, D), lambda i: (i, 0)))
    dx_rows = cfg.B * cfg.SL if lat_only else T

    def body(*refs):
        x_ref, g_ref, mod_ref, dh_ref, dres_ref = refs[:5]
        dx_ref, dmod_ref, dgain_ref = refs[5 + ng:8 + ng]
        i = pl.program_id(0)
        t = i % TPE
        x = x_ref[...]
        g = g_ref[...]
        dh = dh_ref[...].astype(F32)
        rstd = lax.rsqrt(jnp.mean(x * x, axis=-1, keepdims=True) + EPS)
        xhat = x * rstd
        dn = dh * (1.0 + mod_ref[pl.ds(isc, 1), :])
        dsh = jnp.sum(dh, axis=0, keepdims=True)
        dsc = jnp.sum(dh * (xhat * g), axis=0, keepdims=True)
        dgain = jnp.sum(dn * xhat, axis=0, keepdims=True)
        dxh = dn * g
        dx = rstd * (dxh - xhat * jnp.mean(dxh * xhat, axis=-1, keepdims=True)) + dres_ref[...]
        dx_ref[...] = dx
        sums = [(dmod_ref.at[pl.ds(0, 1), :], dsh), (dmod_ref.at[pl.ds(1, 1), :], dsc)]
        if ng:
            y_ref, gmod_ref = refs[5:7]
            dy_ref, dgate_ref = refs[8 + ng:]
            dy_ref[...] = (dx * gmod_ref[pl.ds(gate[2], 1), :]).astype(BF16)
            sums.append((dgate_ref, jnp.sum(dx * y_ref[...].astype(F32), axis=0, keepdims=True)))

        @pl.when(t <= 1)
        def _():
            for ref, val in sums:
                ref[...] = val

        @pl.when(t > 1)
        def _():
            for ref, val in sums:
                ref[...] += val

        @pl.when(i == 0)
        def _():
            dgain_ref[...] = dgain

        @pl.when(i > 0)
        def _():
            dgain_ref[...] += dgain

    tok = pl.BlockSpec((TM, D), lambda i: (i, 0))
    mod_spec = pl.BlockSpec((None, None, 6, D), _mod_index(cfg))
    res = pl.pallas_call(
        body, name=name, grid=(T // TM,),
        in_specs=[tok, pl.BlockSpec((1, D), lambda i: (0, 0)), mod_spec, tok, tok] + ([tok, mod_spec] if ng else []),
        out_specs=[dx_spec, pl.BlockSpec((None, None, 2, D), _mod_index(cfg)), pl.BlockSpec((1, D), lambda i: (0, 0))]
        + ([tok, pl.BlockSpec((None, None, 1, D), _mod_index(cfg))] if ng else []),
        out_shape=[jax.ShapeDtypeStruct((dx_rows, D), F32), jax.ShapeDtypeStruct((cfg.B, 2, 2, D), F32),
                   jax.ShapeDtypeStruct((1, D), F32)]
        + ([jax.ShapeDtypeStruct((T, D), BF16), jax.ShapeDtypeStruct((cfg.B, 2, 1, D), F32)] if ng else []),
        compiler_params=_params(("arbitrary",)),
    )(x, gain, mod, dh, dres, *(gate[:2] if ng else ()))
    return res


def _gate_bwd(cfg, name, dx, y, mod, igate):
    T, D, TM, TPE = cfg.T, cfg.D, cfg.TM, cfg.TPE

    def body(dx_ref, y_ref, mod_ref, dy_ref, dg_ref):
        t = pl.program_id(0) % TPE
        dx = dx_ref[...]
        dy_ref[...] = (dx * mod_ref[pl.ds(igate, 1), :]).astype(BF16)
        dg = jnp.sum(dx * y_ref[...].astype(F32), axis=0, keepdims=True)

        @pl.when(t <= 1)
        def _():
            dg_ref[...] = dg

        @pl.when(t > 1)
        def _():
            dg_ref[...] += dg

    tok = pl.BlockSpec((TM, D), lambda i: (i, 0))
    return pl.pallas_call(
        body, name=name, grid=(T // TM,),
        in_specs=[tok, tok, pl.BlockSpec((None, None, 6, D), _mod_index(cfg))],
        out_specs=[tok, pl.BlockSpec((None, None, 1, D), _mod_index(cfg))],
        out_shape=[jax.ShapeDtypeStruct((T, D), BF16), jax.ShapeDtypeStruct((cfg.B, 2, 1, D), F32)],
        compiler_params=_params(("arbitrary",)),
    )(dx, y, mod)


def _loss_grad(cfg, x, tgt):
    T, D, TM, TPE = cfg.T, cfg.D, cfg.TM, cfg.TPE

    def body(x_ref, t_ref, dx_ref, loss_ref):
        i = pl.program_id(0)
        t = i % TPE

        @pl.when(i == 0)
        def _():
            loss_ref[...] = jnp.zeros_like(loss_ref)

        @pl.when(t == 0)
        def _():
            dx_ref[...] = jnp.zeros_like(dx_ref)

        @pl.when(t > 0)
        def _():
            err = x_ref[...] - t_ref[...]
            dx_ref[...] = err * (1.0 / D)
            loss_ref[...] += 0.5 * jnp.sum(jnp.mean(err * err, axis=-1, keepdims=True), axis=0, keepdims=True)

    tok = pl.BlockSpec((TM, D), lambda i: (i, 0))
    tgt_spec = pl.BlockSpec((TM, D), lambda i: ((i // TPE) * (TPE - 1) + jnp.maximum(i % TPE - 1, 0), 0))
    dx, loss = pl.pallas_call(
        body, name="loss_grad", grid=(T // TM,),
        in_specs=[tok, tgt_spec], out_specs=[tok, pl.BlockSpec((8, LANES), lambda i: (0, 0))],
        out_shape=[jax.ShapeDtypeStruct((T, D), F32), jax.ShapeDtypeStruct((8, LANES), F32)],
        compiler_params=_params(("arbitrary",)),
    )(x, tgt)
    return loss[0, 0], dx


def _rope_tables(cfg):
    rows = cfg.SL // cfg.GRID_W
    row = jnp.repeat(jnp.arange(rows, dtype=F32), cfg.GRID_W)
    col = jnp.tile(jnp.arange(cfg.GRID_W, dtype=F32), rows)
    n_freq = CHUNK // 4
    inv = ROPE_BASE ** (-jnp.arange(n_freq, dtype=F32) / n_freq)
    ang = jnp.concatenate([row[:, None] * inv[None, :], col[:, None] * inv[None, :]], axis=-1)
    cos, sin = jnp.cos(ang), jnp.sin(ang)
    cosf = jnp.concatenate([jnp.ones((cfg.SC, CHUNK), F32), jnp.concatenate([cos, cos], axis=-1)], axis=0)
    sinf = jnp.concatenate([jnp.zeros((cfg.SC, CHUNK), F32), jnp.concatenate([-sin, sin], axis=-1)], axis=0)
    return cosf, sinf


def _rope(x, cosf, sinf):
    return x * cosf + pltpu.roll(x, CHUNK // 2, 1) * sinf


def _irope(dy, cosf, sinf):
    return dy * cosf - pltpu.roll(dy, CHUNK // 2, 1) * sinf


def _prep_fwd(cfg, name, p, cosf, sinf, qg, kg):
    T, TM, TPE, H, KV = cfg.T, cfg.TM, cfg.TPE, cfg.H, cfg.KV
    HW = H * CHUNK
    kscale = CHUNK ** -0.5

    def body(p_ref, c_ref, s_ref, qg_ref, kg_ref, rq_ref, rk_ref, aq_ref, ak_ref):
        cosf, sinf = c_ref[...], s_ref[...]

        def normed(x, g):
            return x * lax.rsqrt(jnp.mean(x * x, axis=-1, keepdims=True) + EPS) * g

        def seg(col):
            return p_ref[:, pl.ds(col, CHUNK)].astype(F32)

        for h in range(H):
            sl = pl.ds(h * CHUNK, CHUNK)
            rq_ref[:, sl] = _rope(seg(h * CHUNK), cosf, sinf)
            rk_ref[:, sl] = _rope(seg(HW + h * CHUNK), cosf, sinf) * kscale
            aq_ref[:, sl] = (_rope(normed(seg(4 * HW + h * CHUNK), qg_ref[...]), cosf, sinf) * ATT_SCALE).astype(BF16)
        for h in range(KV):
            ak_ref[:, pl.ds(h * CHUNK, CHUNK)] = _rope(
                normed(seg(5 * HW + h * CHUNK), kg_ref[...]), cosf, sinf).astype(BF16)

    tab = pl.BlockSpec((TM, CHUNK), lambda i: (i % TPE, 0))
    vec = pl.BlockSpec((1, CHUNK), lambda i: (0, 0))
    return pl.pallas_call(
        body, name=name, grid=(T // TM,),
        in_specs=[pl.BlockSpec((TM, cfg.ABW), lambda i: (i, 0)), tab, tab, vec, vec],
        out_specs=[pl.BlockSpec((TM, HW), lambda i: (i, 0))] * 3 + [pl.BlockSpec((TM, KV * CHUNK), lambda i: (i, 0))],
        out_shape=[jax.ShapeDtypeStruct((T, HW), F32), jax.ShapeDtypeStruct((T, HW), F32),
                   jax.ShapeDtypeStruct((T, HW), BF16), jax.ShapeDtypeStruct((T, KV * CHUNK), BF16)],
        compiler_params=_params(("parallel",)),
    )(p, cosf, sinf, qg, kg)


def _prep_bwd(cfg, name, p, cosf, sinf, qg, kg, d_rq, d_rk, d_rv, d_gate, d_aq, d_ak, d_av):
    T, TM, TPE, H, KV = cfg.T, cfg.TM, cfg.TPE, cfg.H, cfg.KV
    HW = H * CHUNK
    kscale = CHUNK ** -0.5

    def body(p_ref, c_ref, s_ref, qg_ref, kg_ref, drq_ref, drk_ref, drv_ref, dgt_ref, daq_ref, dak_ref, dav_ref,
             dp_ref, dqg_ref, dkg_ref):
        i = pl.program_id(0)
        cosf, sinf = c_ref[...], s_ref[...]

        def norm_bwd(x, g, dn):
            rstd = lax.rsqrt(jnp.mean(x * x, axis=-1, keepdims=True) + EPS)
            xhat = x * rstd
            dg = jnp.sum(dn * xhat, axis=0, keepdims=True)
            dxh = dn * g
            return rstd * (dxh - xhat * jnp.mean(dxh * xhat, axis=-1, keepdims=True)), dg

        dqg = jnp.zeros((1, CHUNK), F32)
        dkg = jnp.zeros((1, CHUNK), F32)
        for h in range(H):
            sl = pl.ds(h * CHUNK, CHUNK)
            dp_ref[:, pl.ds(h * CHUNK, CHUNK)] = _irope(drq_ref[:, sl].astype(F32), cosf, sinf).astype(BF16)
            dp_ref[:, pl.ds(HW + h * CHUNK, CHUNK)] = (_irope(drk_ref[:, sl].astype(F32), cosf, sinf)
                                                       * kscale).astype(BF16)
            dp_ref[:, pl.ds(2 * HW + h * CHUNK, CHUNK)] = drv_ref[:, sl].astype(BF16)
            dp_ref[:, pl.ds(3 * HW + h * CHUNK, CHUNK)] = dgt_ref[:, sl].astype(BF16)
            dx, dg = norm_bwd(p_ref[:, pl.ds(4 * HW + h * CHUNK, CHUNK)].astype(F32), qg_ref[...],
                              _irope(daq_ref[:, sl].astype(F32), cosf, sinf))
            dp_ref[:, pl.ds(4 * HW + h * CHUNK, CHUNK)] = dx.astype(BF16)
            dqg = dqg + dg
        for h in range(KV):
            sl = pl.ds(h * CHUNK, CHUNK)
            dx, dg = norm_bwd(p_ref[:, pl.ds(5 * HW + h * CHUNK, CHUNK)].astype(F32), kg_ref[...],
                              _irope(dak_ref[:, sl], cosf, sinf))
            dp_ref[:, pl.ds(5 * HW + h * CHUNK, CHUNK)] = dx.astype(BF16)
            dp_ref[:, pl.ds(5 * HW + (KV + h) * CHUNK, CHUNK)] = dav_ref[:, sl].astype(BF16)
            dkg = dkg + dg

        @pl.when(i == 0)
        def _():
            dqg_ref[...] = dqg
            dkg_ref[...] = dkg

        @pl.when(i > 0)
        def _():
            dqg_ref[...] += dqg
            dkg_ref[...] += dkg

    tab = pl.BlockSpec((TM, CHUNK), lambda i: (i % TPE, 0))
    vec = pl.BlockSpec((1, CHUNK), lambda i: (0, 0))
    hw = pl.BlockSpec((TM, HW), lambda i: (i, 0))
    kvw = pl.BlockSpec((TM, KV * CHUNK), lambda i: (i, 0))
    return pl.pallas_call(
        body, name=name, grid=(T // TM,),
        in_specs=[pl.BlockSpec((TM, cfg.ABW), lambda i: (i, 0)), tab, tab, vec, vec, hw, hw, hw, hw, hw, kvw, kvw],
        out_specs=[pl.BlockSpec((TM, cfg.ABW), lambda i: (i, 0)), vec, vec],
        out_shape=[jax.ShapeDtypeStruct((T, cfg.ABW), BF16), jax.ShapeDtypeStruct((1, CHUNK), F32),
                   jax.ShapeDtypeStruct((1, CHUNK), F32)],
        compiler_params=_params(("arbitrary",)),
    )(p, cosf, sinf, qg, kg, d_rq, d_rk, d_rv, d_gate, d_aq, d_ak, d_av)


def _ret_consts(direction, lg):
    C = CHUNK
    ii = lax.broadcasted_iota(jnp.int32, (C, C), 0)
    jj = lax.broadcasted_iota(jnp.int32, (C, C), 1)
    col = lax.broadcasted_iota(jnp.int32, (C, 1), 0).astype(F32)
    if direction == 0:
        mask, er, ek, eq = ii >= jj, (ii - jj).astype(F32), (C - 1.0) - col, col + 1.0
    else:
        mask, er, ek, eq = jj >= ii, (jj - ii).astype(F32), col, C - col
    er = jnp.where(mask, er, 0.0)
    dm = jnp.where(mask, jnp.exp(er * lg), 0.0)
    return dm, er, jnp.exp(ek * lg), ek, jnp.exp(eq * lg), eq, jnp.exp(C * lg)


def _ret_order(cfg, direction):
    n_all, n_ctx = cfg.S // CHUNK, cfg.SC // CHUNK
    if direction == 0:
        return list(range(n_all))
    return list(range(n_ctx - 1, -1, -1)) + list(range(n_all - 1, n_ctx - 1, -1))


def _carry_begin(carry, c_src, c_dst, sems, step):
    @pl.when(step == 0)
    def _():
        for cp in _carry_copies(carry, c_src, c_dst, *sems):
            cp.start()


def _carry_end(carry, c_src, c_dst, sems, step, n_steps):
    @pl.when(step == n_steps - 1)
    def _():
        for cp in _carry_copies(carry, c_src, c_dst, *sems):
            cp.wait()


def _carry_scratch(nc):
    return [pltpu.SemaphoreType.DMA((3 * nc,)), pltpu.SemaphoreType.DMA((3 * nc,)),
            pltpu.SemaphoreType.DMA((nc,))] if nc else []


def _head_norm_gate(o, g):
    mu = jnp.mean(o, axis=-1, keepdims=True)
    var = jnp.mean(jnp.square(o - mu), axis=-1, keepdims=True)
    rstd = lax.rsqrt(var + EPS)
    y = (o - mu) * rstd
    sg = jax.nn.sigmoid(g)
    return y, rstd, sg


RET_UNROLL = 2


def _retention_fwd(cfg, name, rq, rk, p, lgb, carry=None):
    B, H, S, T = cfg.B, cfg.H, cfg.S, cfg.T
    n_all = S // CHUNK
    nc = len(carry.srcs) if carry is not None else 0

    def body(*refs):
        q_ref, k_ref, v_ref, g_ref, lg_ref = refs[:5]
        c_src = refs[5:5 + nc]
        o_ref, ret_ref, st_ref = refs[5 + nc:8 + nc]
        c_dst = refs[8 + nc:8 + 2 * nc]
        sems = refs[8 + 2 * nc:8 + 2 * nc + 3] if nc else ()
        kv_ref = refs[-1]
        step = pl.program_id(0) * H + pl.program_id(1)
        if nc:
            _carry_begin(carry, c_src, c_dst, sems, step)

        def rows(n):
            return pl.ds(pl.multiple_of(n * CHUNK, CHUNK), CHUNK)

        (dm0, _, kd0, _, qd0, _, cd0), (dm1, _, kd1, _, qd1, _, cd1) = (
            _ret_consts(d, lg_ref[d, 0:1, 0:1]) for d in (0, 1))
        dm_both = dm0 + dm1

        def kv_step(n, c):
            k = k_ref[rows(n), :]
            v = v_ref[rows(n), :].astype(BF16)
            kv_ref[0, n] = _dot_tn((k * kd0).astype(BF16), v)
            kv_ref[1, n] = _dot_tn((k * kd1).astype(BF16), v)
            return c

        lax.fori_loop(0, n_all, kv_step, 0, unroll=RET_UNROLL)
        for direction, cd in ((0, cd0), (1, cd1)):
            st = jnp.zeros((CHUNK, CHUNK), F32)
            for t, n in enumerate(_ret_order(cfg, direction)):
                st_ref[direction, n] = st
                if t + 1 < n_all:
                    st = cd * st + kv_ref[direction, n]

        def out_step(n, c):
            q = q_ref[rows(n), :].astype(BF16)
            v = v_ref[rows(n), :].astype(BF16)
            s = _dot_nt(q, k_ref[rows(n), :].astype(BF16)) * dm_both
            states = jnp.concatenate([st_ref[0, n].astype(BF16), st_ref[1, n].astype(BF16)], axis=1)
            cross = _dot(q, states)
            o = _dot(s.astype(BF16), v) + cross[:, :CHUNK] * qd0 + cross[:, CHUNK:] * qd1
            o_ref[rows(n), :] = o
            g = g_ref[rows(n), :].astype(F32)
            y, _, sg = _head_norm_gate(o, g)
            ret_ref[rows(n), :] = (y * (g * sg)).astype(BF16)
            return c

        lax.fori_loop(0, n_all, out_step, 0, unroll=RET_UNROLL)
        if nc:
            _carry_end(carry, c_src, c_dst, sems, step, B * H)

    HW = H * CHUNK
    blk = lambda off: pl.BlockSpec((S, CHUNK), lambda b, h: (b, off + h))
    st_spec = pl.BlockSpec((None, None, 2, n_all, CHUNK, CHUNK), lambda b, h: (b, h, 0, 0, 0, 0))
    res = pl.pallas_call(
        body, name=name, grid=(B, H),
        in_specs=[blk(0), blk(0), blk(2 * H), blk(3 * H),
                  pl.BlockSpec((None, 2, 8, LANES), lambda b, h: (h, 0, 0, 0))] + [ANY] * nc,
        out_specs=[blk(0), blk(0), st_spec] + [ANY] * nc,
        out_shape=[jax.ShapeDtypeStruct((T, HW), F32), jax.ShapeDtypeStruct((T, 2 * HW), BF16),
                   jax.ShapeDtypeStruct((B, H, 2, n_all, CHUNK, CHUNK), F32)] + (_carry_out_shapes(carry) if nc else []),
        scratch_shapes=_carry_scratch(nc) + [pltpu.VMEM((2, n_all, CHUNK, CHUNK), F32)],
        compiler_params=_params(("arbitrary", "arbitrary") if nc else ("parallel", "parallel")),
    )(rq, rk, p, p, lgb, *(carry.srcs if nc else ()))
    return res[:3], res[3:]


def _retention_bwd(cfg, name, rq, rk, p, o_sum, states, dcat, lgb, carry=None):
    B, H, S, T = cfg.B, cfg.H, cfg.S, cfg.T
    n_all = S // CHUNK
    C = CHUNK
    nc = len(carry.srcs) if carry is not None else 0

    def body(*refs):
        q_ref, k_ref, v_ref, g_ref, o_ref, st_ref, dr_ref, lg_ref = refs[:8]
        c_src = refs[8:8 + nc]
        dq_ref, dk_ref, dv_ref, dg_ref, dlg_ref = refs[8 + nc:13 + nc]
        c_dst = refs[13 + nc:13 + 2 * nc]
        sems = refs[13 + 2 * nc:13 + 2 * nc + 3] if nc else ()
        do_ref, gq_ref, ds_ref, acc_ref = refs[-4:]
        step = pl.program_id(0) * H + pl.program_id(1)
        if nc:
            _carry_begin(carry, c_src, c_dst, sems, step)

        def rows(n):
            return pl.ds(pl.multiple_of(n * C, C), C)

        def gate_step(n, c):
            g = g_ref[rows(n), :].astype(F32)
            dr = dr_ref[rows(n), :].astype(F32)
            y, rstd, sg = _head_norm_gate(o_ref[rows(n), :], g)
            dy = dr * (g * sg)
            dg_ref[rows(n), :] = (dr * y * (sg * (1.0 + g * (1.0 - sg)))).astype(BF16)
            do_ref[rows(n), :] = rstd * (dy - jnp.mean(dy, axis=-1, keepdims=True)
                                         - y * jnp.mean(dy * y, axis=-1, keepdims=True))
            return c

        lax.fori_loop(0, n_all, gate_step, 0, unroll=RET_UNROLL)

        (dm0, er0, kd0, ek0, qd0, eq0, cd0), (dm1, er1, kd1, ek1, qd1, eq1, cd1) = (
            _ret_consts(d, lg_ref[d, 0:1, 0:1]) for d in (0, 1))
        dm_both = dm0 + dm1
        wdm0, wdm1 = dm0 * er0, dm1 * er1

        def side(a, b):
            return jnp.concatenate([a.astype(BF16), b.astype(BF16)], axis=1)

        def gq_step(n, c):
            do = do_ref[rows(n), :]
            gq = _dot_tn(q_ref[rows(n), :].astype(BF16), side(do * qd0, do * qd1))
            gq_ref[0, n] = gq[:, :C]
            gq_ref[1, n] = gq[:, C:]
            return c

        lax.fori_loop(0, n_all, gq_step, 0, unroll=RET_UNROLL)
        for direction, cd in ((0, cd0), (1, cd1)):
            order = _ret_order(cfg, direction)
            ds = jnp.zeros((C, C), F32)
            for t in reversed(range(n_all)):
                ds_ref[direction, order[t]] = ds
                if t > 0:
                    ds = cd * ds + gq_ref[direction, order[t]]
        acc_ref[...] = jnp.zeros_like(acc_ref)

        def chunk_step(n, c):
            q = q_ref[rows(n), :].astype(BF16)
            kf = k_ref[rows(n), :]
            k = kf.astype(BF16)
            v = v_ref[rows(n), :].astype(BF16)
            do = do_ref[rows(n), :]
            dob = do.astype(BF16)
            sp0, sp1 = st_ref[0, n], st_ref[1, n]
            ds0, ds1 = ds_ref[0, n], ds_ref[1, n]
            states = side(sp0, sp1)
            dstates = jnp.concatenate([ds0.astype(BF16), ds1.astype(BF16)], axis=0)
            doq0, doq1 = do * qd0, do * qd1
            doq = side(doq0, doq1)
            s_raw = _dot_nt(q, k)
            dpm = _dot_nt(dob, v)
            dsr = (dpm * dm_both).astype(BF16)
            dks = _dot_nt(v, dstates)
            dks0, dks1 = dks[:, :C] * kd0, dks[:, C:] * kd1
            qs = _dot(q, states)
            dq_ref[rows(n), :] = (_dot(dsr, k) + _dot_nt(doq, states)).astype(BF16)
            dk_ref[rows(n), :] = (_dot_tn(dsr, q) + dks0 + dks1).astype(BF16)
            dv_ref[rows(n), :] = (_dot_tn((s_raw * dm_both).astype(BF16), dob)
                                  + _dot(side(kf * kd0, kf * kd1), dstates)).astype(BF16)
            inner = dpm * s_raw
            acc_ref[0] += (jnp.sum(inner * wdm0, axis=0, keepdims=True)
                           + jnp.sum(eq0 * doq0 * qs[:, :C], axis=0, keepdims=True)
                           + jnp.sum(ek0 * kf * dks0, axis=0, keepdims=True)
                           + (C * cd0) * jnp.sum(ds0 * sp0, axis=0, keepdims=True))
            acc_ref[1] += (jnp.sum(inner * wdm1, axis=0, keepdims=True)
                           + jnp.sum(eq1 * doq1 * qs[:, C:], axis=0, keepdims=True)
                           + jnp.sum(ek1 * kf * dks1, axis=0, keepdims=True)
                           + (C * cd1) * jnp.sum(ds1 * sp1, axis=0, keepdims=True))
            return c

        lax.fori_loop(0, n_all, chunk_step, 0, unroll=RET_UNROLL)
        for direction in (0, 1):
            dlg_ref[direction] = jnp.broadcast_to(jnp.sum(acc_ref[direction], axis=1, keepdims=True), (8, LANES))
        if nc:
            _carry_end(carry, c_src, c_dst, sems, step, B * H)

    HW = H * CHUNK
    blk = lambda off: pl.BlockSpec((S, CHUNK), lambda b, h: (b, off + h))
    st_spec = pl.BlockSpec((None, None, 2, n_all, C, C), lambda b, h: (b, h, 0, 0, 0, 0))
    res = pl.pallas_call(
        body, name=name, grid=(B, H),
        in_specs=[blk(0), blk(0), blk(2 * H), blk(3 * H), blk(0), st_spec, blk(0),
                  pl.BlockSpec((None, 2, 8, LANES), lambda b, h: (h, 0, 0, 0))] + [ANY] * nc,
        out_specs=[blk(0)] * 4 + [pl.BlockSpec((None, None, 2, 8, LANES), lambda b, h: (b, h, 0, 0, 0))] + [ANY] * nc,
        out_shape=[jax.ShapeDtypeStruct((T, HW), BF16)] * 4 + [jax.ShapeDtypeStruct((B, H, 2, 8, LANES), F32)]
        + (_carry_out_shapes(carry) if nc else []),
        scratch_shapes=_carry_scratch(nc) + [pltpu.VMEM((S, CHUNK), F32), pltpu.VMEM((2, n_all, C, C), F32),
                                             pltpu.VMEM((2, n_all, C, C), F32), pltpu.VMEM((2, 1, C), F32)],
        compiler_params=_params(("arbitrary", "arbitrary") if nc else ("parallel", "parallel")),
    )(rq, rk, p, p, o_sum, states, dcat, lgb, *(carry.srcs if nc else ()))
    return res[:5], res[5:]


ATT_SCALE = CHUNK ** -0.5


def _attn_scores(cfg, q, k, t):
    kcol = lax.broadcasted_iota(jnp.int32, (1, cfg.S), 1)
    bias = jnp.where(jnp.logical_or(t > 0, kcol < cfg.SC), 0.0, -1e30)
    return _dot_nt(q, k) + bias


def _attention_fwd(cfg, name, aq, ak, p, cat, carry=None):
    B, H, KV, S, T, TM, TPE = cfg.B, cfg.H, cfg.KV, cfg.S, cfg.T, cfg.TM, cfg.TPE
    G = H // KV
    v_off = (5 * H + KV)
    nc = len(carry.srcs) if carry is not None else 0

    def body(*refs):
        q_ref, k_ref, v_ref = refs[:3]
        o_ref, lse_ref = refs[4 + nc:6 + nc]
        c_src, c_dst, sems = refs[4:4 + nc], refs[6 + nc:6 + 2 * nc], refs[6 + 2 * nc:]
        step = (pl.program_id(0) * H + pl.program_id(1)) * TPE + pl.program_id(2)
        if nc:
            _carry_begin(carry, c_src, c_dst, sems, step)
        s = _attn_scores(cfg, q_ref[...], k_ref[...], pl.program_id(2))
        m = jnp.max(s, axis=-1, keepdims=True)
        e = jnp.exp(s - m)
        total = jnp.sum(e, axis=-1, keepdims=True)
        o_ref[...] = (_dot(e.astype(BF16), v_ref[...].astype(BF16)) * (1.0 / total)).astype(BF16)
        lse_ref[...] = m + jnp.log(total)
        if nc:
            _carry_end(carry, c_src, c_dst, sems, step, B * H * TPE)

    res = pl.pallas_call(
        body, name=name, grid=(B, H, TPE),
        in_specs=[pl.BlockSpec((TM, CHUNK), lambda b, h, t: (b * TPE + t, h)),
                  pl.BlockSpec((S, CHUNK), lambda b, h, t: (b, h // G)),
                  pl.BlockSpec((S, CHUNK), lambda b, h, t: (b, v_off + h // G)), ANY] + [ANY] * nc,
        out_specs=[pl.BlockSpec((TM, CHUNK), lambda b, h, t: (b * TPE + t, H + h)),
                   pl.BlockSpec((None, TM, 1), lambda b, h, t: (h, b * TPE + t, 0))] + [ANY] * nc,
        out_shape=[jax.ShapeDtypeStruct(cat.shape, cat.dtype), jax.ShapeDtypeStruct((H, T, 1), F32)]
        + (_carry_out_shapes(carry) if nc else []),
        input_output_aliases={3: 0},
        scratch_shapes=_carry_scratch(nc),
        compiler_params=_params(("arbitrary",) * 3 if nc else ("parallel",) * 3),
    )(aq, ak, p, cat, *(carry.srcs if nc else ()))
    return res[:2], res[2:]


def _attention_bwd(cfg, name, aq, ak, p, cat, lse, dcat, carry=None):
    B, H, KV, S, T, TM, TPE = cfg.B, cfg.H, cfg.KV, cfg.S, cfg.T, cfg.TM, cfg.TPE
    G = H // KV
    v_off = (5 * H + KV)
    nc = len(carry.srcs) if carry is not None else 0

    def body(*refs):
        q_ref, k_ref, v_ref, o_ref, lse_ref, do_ref = refs[:6]
        dq_ref, dk_ref, dv_ref = refs[6 + nc:9 + nc]
        c_src, c_dst, sems = refs[6:6 + nc], refs[9 + nc:9 + 2 * nc], refs[9 + 2 * nc:]
        g, t = pl.program_id(2), pl.program_id(3)
        step = ((pl.program_id(0) * KV + pl.program_id(1)) * G + g) * TPE + t
        if nc:
            _carry_begin(carry, c_src, c_dst, sems, step)
        q, k = q_ref[...], k_ref[...]
        v = v_ref[...].astype(BF16)
        do = do_ref[...]
        dob = do.astype(BF16)
        pr = jnp.exp(_attn_scores(cfg, q, k, t) - lse_ref[...])
        delta = jnp.sum(do.astype(F32) * o_ref[...].astype(F32), axis=-1, keepdims=True)
        ds = (pr * (_dot_nt(dob, v) - delta)).astype(BF16)
        dq_ref[...] = (_dot(ds, k) * ATT_SCALE).astype(BF16)
        dk = _dot_tn(ds, q)
        dv = _dot_tn(pr.astype(BF16), dob)
        first = jnp.logical_and(g == 0, t == 0)

        @pl.when(first)
        def _():
            dk_ref[...] = dk
            dv_ref[...] = dv

        @pl.when(jnp.logical_not(first))
        def _():
            dk_ref[...] += dk
            dv_ref[...] += dv

        if nc:
            _carry_end(carry, c_src, c_dst, sems, step, B * KV * G * TPE)

    qspec = pl.BlockSpec((TM, CHUNK), lambda b, kv, g, t: (b * TPE + t, kv * G + g))
    kvspec = pl.BlockSpec((S, CHUNK), lambda b, kv, g, t: (b, kv))
    right = pl.BlockSpec((TM, CHUNK), lambda b, kv, g, t: (b * TPE + t, H + kv * G + g))
    res = pl.pallas_call(
        body, name=name, grid=(B, KV, G, TPE),
        in_specs=[qspec, kvspec, pl.BlockSpec((S, CHUNK), lambda b, kv, g, t: (b, v_off + kv)), right,
                  pl.BlockSpec((None, TM, 1), lambda b, kv, g, t: (kv * G + g, b * TPE + t, 0)), right] + [ANY] * nc,
        out_specs=[qspec, kvspec, kvspec] + [ANY] * nc,
        out_shape=[jax.ShapeDtypeStruct((T, H * CHUNK), BF16), jax.ShapeDtypeStruct((T, KV * CHUNK), F32),
                   jax.ShapeDtypeStruct((T, KV * CHUNK), F32)] + (_carry_out_shapes(carry) if nc else []),
        scratch_shapes=_carry_scratch(nc),
        compiler_params=_params(("arbitrary",) * 4 if nc else ("parallel", "parallel", "arbitrary", "arbitrary")),
    )(aq, ak, p, cat, lse, dcat, *(carry.srcs if nc else ()))
    return res[:3], res[3:]


_GELU_C = math.sqrt(2.0 / math.pi)


def _gelu(x):
    return 0.5 * x * (1.0 + jnp.tanh(_GELU_C * (x + 0.044715 * x * x * x)))


def _gelu_and_grad(x):
    x2 = x * x
    th = jnp.tanh(_GELU_C * (x + 0.044715 * x * x2))
    half = 0.5 * (1.0 + th)
    return x * half, half + 0.5 * x * (1.0 - th * th) * _GELU_C * (1.0 + 3.0 * 0.044715 * x2)


def _cm_fwd(cfg, name, a, vg, ws, bs):
    T, TM, W, NG = cfg.T, cfg.TM, cfg.CMW, cfg.CMG

    def body(a_ref, vg_ref, ws_ref, bs_ref, m_ref):
        v = _gelu(a_ref[:, pl.ds(W, W)].astype(F32))
        vn = (v * lax.rsqrt(jnp.mean(v * v, axis=-1, keepdims=True) + EPS) * vg_ref[...]).astype(BF16)
        for c in range(TM // CHUNK):
            for g in range(NG):
                rows, cols = slice(c * CHUNK, (c + 1) * CHUNK), slice(g * CHUNK, (g + 1) * CHUNK)
                sv = _dot(ws_ref[g].astype(BF16), vn[rows, cols]) + bs_ref[g]
                u = _gelu(a_ref[pl.ds(c * CHUNK, CHUNK), pl.ds(g * CHUNK, CHUNK)].astype(F32))
                m_ref[pl.ds(c * CHUNK, CHUNK), pl.ds(g * CHUNK, CHUNK)] = (u * sv).astype(BF16)

    return pl.pallas_call(
        body, name=name, grid=(T // TM,),
        in_specs=[pl.BlockSpec((TM, 2 * W), lambda i: (i, 0)), pl.BlockSpec((1, W), lambda i: (0, 0)),
                  pl.BlockSpec((NG, CHUNK, CHUNK), lambda i: (0, 0, 0)),
                  pl.BlockSpec((NG, CHUNK, 1), lambda i: (0, 0, 0))],
        out_specs=pl.BlockSpec((TM, W), lambda i: (i, 0)),
        out_shape=jax.ShapeDtypeStruct((T, W), BF16),
        compiler_params=_params(("parallel",)),
    )(a, vg, ws, bs)


def _cm_bwd(cfg, name, a, vg, ws, bs, dm):
    T, TM, W, NG = cfg.T, cfg.TM, cfg.CMW, cfg.CMG

    def body(a_ref, vg_ref, ws_ref, bs_ref, dm_ref, da_ref, dws_ref, dbs_ref, dvg_ref, dvn_ref):
        i = pl.program_id(0)

        @pl.when(i == 0)
        def _():
            dws_ref[...] = jnp.zeros_like(dws_ref)
            dbs_ref[...] = jnp.zeros_like(dbs_ref)
            dvg_ref[...] = jnp.zeros_like(dvg_ref)

        v, v_grad = _gelu_and_grad(a_ref[:, pl.ds(W, W)].astype(F32))
        rstd = lax.rsqrt(jnp.mean(v * v, axis=-1, keepdims=True) + EPS)
        xhat = v * rstd
        vg = vg_ref[...]
        vn = (xhat * vg).astype(BF16)
        for c in range(TM // CHUNK):
            for g in range(NG):
                rows, cols = slice(c * CHUNK, (c + 1) * CHUNK), slice(g * CHUNK, (g + 1) * CHUNK)
                rs, cs = pl.ds(c * CHUNK, CHUNK), pl.ds(g * CHUNK, CHUNK)
                wsb = ws_ref[g].astype(BF16)
                blk = vn[rows, cols]
                sv = _dot(wsb, blk) + bs_ref[g]
                u, u_grad = _gelu_and_grad(a_ref[rs, cs].astype(F32))
                dmb = dm_ref[rs, cs].astype(F32)
                da_ref[rs, cs] = (dmb * sv * u_grad).astype(BF16)
                dsv = dmb * u
                dsvb = dsv.astype(BF16)
                dbs_ref[g] += jnp.sum(dsv, axis=1, keepdims=True)
                dws_ref[g] += _dot_nt(dsvb, blk)
                dvn_ref[rs, cs] = _dot_tn(wsb, dsvb)
        dvn = dvn_ref[...]
        dvg_ref[...] += jnp.sum(dvn * xhat, axis=0, keepdims=True)
        dxh = dvn * vg
        dv = rstd * (dxh - xhat * jnp.mean(dxh * xhat, axis=-1, keepdims=True))
        da_ref[:, pl.ds(W, W)] = (dv * v_grad).astype(BF16)

    return pl.pallas_call(
        body, name=name, grid=(T // TM,),
        in_specs=[pl.BlockSpec((TM, 2 * W), lambda i: (i, 0)), pl.BlockSpec((1, W), lambda i: (0, 0)),
                  pl.BlockSpec((NG, CHUNK, CHUNK), lambda i: (0, 0, 0)),
                  pl.BlockSpec((NG, CHUNK, 1), lambda i: (0, 0, 0)), pl.BlockSpec((TM, W), lambda i: (i, 0))],
        out_specs=[pl.BlockSpec((TM, 2 * W), lambda i: (i, 0)), pl.BlockSpec((NG, CHUNK, CHUNK), lambda i: (0, 0, 0)),
                   pl.BlockSpec((NG, CHUNK, 1), lambda i: (0, 0, 0)), pl.BlockSpec((1, W), lambda i: (0, 0))],
        out_shape=[jax.ShapeDtypeStruct((T, 2 * W), BF16), jax.ShapeDtypeStruct((NG, CHUNK, CHUNK), F32),
                   jax.ShapeDtypeStruct((NG, CHUNK, 1), F32), jax.ShapeDtypeStruct((1, W), F32)],
        scratch_shapes=[pltpu.VMEM((TM, W), F32)],
        compiler_params=_params(("arbitrary",)),
    )(a, vg, ws, bs, dm)


def _layer_weights(l):
    mixer = ("ab_w_in", "ab_w_out") if l % 2 == 0 else ("cm_w_in", "cm_w_out")
    return [(mixer[0], l // 2), (mixer[1], l // 2), ("ff_w1", l), ("ff_w2", l)]


def _local_step(cfg, xcat, tgt, mods, shards, w):
    D, TM, H = cfg.D, cfg.TM, cfg.H
    cosf, sinf = _rope_tables(cfg)
    full, big, recv = {}, {}, {}

    def gather_of(keys):
        return _Carry("gather", tuple(shards[n][i] for n, i in keys), tuple(BIG[n] for n, _ in keys))

    def exchange_of(keys):
        return _Carry("exchange", tuple(big[k] for k in keys), tuple(BIG[n] for n, _ in keys))

    def mm(pending, name, a, b, **kw):
        if not pending:
            return _mm(name, a, b, **kw)
        key, carry, sink = pending.pop(0)
        out, (got,) = _mm(name, a, b, carry=carry, **kw)
        sink[key] = got
        return out

    def with_carry(call, keys, make, sink):
        out, got = call(carry=make(keys) if keys else None)
        sink.update(zip(keys, got))
        return out

    keys0 = _layer_weights(0)
    full[keys0[0]], = _comm_call("gather_weights_0", gather_of(keys0[:1]))
    TG = 3 * TM if cfg.TPE % 3 == 0 else TM
    tiles_per_ex = cfg.S // TG
    gate_spec = pl.BlockSpec((None, 2, 6, D), lambda i, j, k: (i // tiles_per_ex, 0, 0, 0))

    def resid_epi(igate, nxt):
        def epi(acc, row_tile, x_ref, mod_ref, *nxt_refs):
            row = lax.broadcasted_iota(jnp.int32, (TG, 1), 0)
            is_ctx = jnp.logical_and(row_tile % tiles_per_ex == 0, row < cfg.SC)

            def pick(ref, idx):
                return jnp.where(is_ctx, ref[0, pl.ds(idx, 1), :], ref[1, pl.ds(idx, 1), :])

            x = x_ref[...] + pick(mod_ref, igate) * acc
            if nxt is None:
                return x, acc
            gain_ref, modn_ref = nxt_refs
            n = x * lax.rsqrt(jnp.mean(x * x, axis=-1, keepdims=True) + EPS) * gain_ref[...]
            return x, acc, n * (1.0 + pick(modn_ref, nxt[3])) + pick(modn_ref, nxt[2])
        return epi

    def gated_out(pending, name, a, key, x, mod, igate, nxt=None):
        extras = [(x, pl.BlockSpec((TG, D), lambda i, j, k: (i, j))), (mod, gate_spec)]
        if nxt is not None:
            extras += [(nxt[0], pl.BlockSpec((1, D), lambda i, j, k: (0, 0))), (nxt[1], gate_spec)]
        return mm(pending, name, a, full[key], mode="nn", tm=TG, tn=D, outs=[F32, BF16] + [BF16] * (nxt is not None),
                  epi=resid_epi(igate, nxt), extras=extras)

    saved = []
    x = xcat
    h = _norm_mod_fwd(cfg, "norm1_fwd_0", x, w["norm1_g"][0][None], mods[0], 0, 1)
    for l in range(N_LAYERS):
        li = l // 2
        mod = mods[l]
        k_in, k_out, k_ff1, k_ff2 = _layer_weights(l)
        pend = [(k, gather_of([k]), full) for k in _layer_weights(l + 1)] if l + 1 < N_LAYERS else []
        norm2 = (w["norm2_g"][l][None], mod, 3, 4)
        s = {"x0": x, "h": h}
        if l % 2 == 0:
            lgb = jnp.broadcast_to(jax.nn.log_sigmoid(w["ret_decay"][li]).T[:, :, None, None], (H, 2, 8, LANES))
            qg, kg = w["att_q_norm_g"][li][None], w["att_k_norm_g"][li][None]
            s["p"] = mm(pend, f"ab_in_{l}", s["h"], full[k_in], mode="nn", outs=[BF16], tn=768)
            s["rq"], s["rk"], s["aq"], s["ak"] = _prep_fwd(cfg, f"prep_fwd_{l}", s["p"], cosf, sinf, qg, kg)
            s["o"], ret, s["st"] = with_carry(
                functools.partial(_retention_fwd, cfg, f"ret_fwd_{l}", s["rq"], s["rk"], s["p"], lgb),
                keys0[1:3] if l == 0 else [], gather_of, full)
            s["cat"], s["lse"] = with_carry(
                functools.partial(_attention_fwd, cfg, f"att_fwd_{l}", s["aq"], s["ak"], s["p"], ret),
                keys0[3:] if l == 0 else [], gather_of, full)
            s["lgb"], s["qg"], s["kg"] = lgb, qg, kg
            x, s["y1"], s["h2"] = gated_out(pend, f"ab_out_{l}", s["cat"], k_out, x, mod, 2, norm2)
        else:
            s["a"] = mm(pend, f"cm_in_{l}", s["h"], full[k_in], mode="nn", outs=[BF16])
            s["vg"], s["ws"], s["bs"] = w["cm_v_norm_g"][li][None], w["cm_w_s"][li], w["cm_b_s"][li][:, :, None]
            s["m"] = _cm_fwd(cfg, f"cm_fwd_{l}", s["a"], s["vg"], s["ws"], s["bs"])
            x, s["y1"], s["h2"] = gated_out(pend, f"cm_out_{l}", s["m"], k_out, x, mod, 2, norm2)
        s["x1"] = x
        s["r"] = mm(pend, f"ff1_{l}", s["h2"], full[k_ff1], mode="nn", outs=[BF16],
                    epi=lambda acc, row_tile: (jnp.square(jnp.maximum(acc, 0.0)),))
        if l + 1 < N_LAYERS:
            x, s["y2"], h = gated_out(pend, f"ff2_{l}", s["r"], k_ff2, x, mod, 5,
                                      (w["norm1_g"][l + 1][None], mods[l + 1], 0, 1))
        else:
            x, s["y2"] = gated_out(pend, f"ff2_{l}", s["r"], k_ff2, x, mod, 5)
        saved.append(s)

    loss, dx = _loss_grad(cfg, x, tgt)

    small = {k: [None] * n for k, n in (("norm1_g", 4), ("norm2_g", 4), ("ret_lg", 2), ("att_q_norm_g", 2),
                                        ("att_k_norm_g", 2), ("cm_v_norm_g", 2), ("cm_w_s", 2), ("cm_b_s", 2))}
    dmods = [None] * N_LAYERS

    for l in reversed(range(N_LAYERS)):
        li = l // 2
        s, mod = saved[l], mods[l]
        k_in, k_out, k_ff1, k_ff2 = _layer_weights(l)
        pend = [(k, exchange_of([k]), recv) for k in reversed(_layer_weights(l + 1))] if l + 1 < N_LAYERS else []
        if l == N_LAYERS - 1:
            dy2, dg2 = _gate_bwd(cfg, f"gate2_bwd_{l}", dx, s["y2"], mod, 5)
        da2 = mm(pend, f"ff2_dx_{l}", dy2, full[k_ff2], mode="nt", outs=[BF16],
                 epi=lambda acc, row_tile, r_ref: (acc * (2.0 * jnp.sqrt(r_ref[...].astype(F32))),),
                 extras=[(s["r"], pl.BlockSpec((_tile(cfg.T, 1024), _tile(cfg.FF, 1024)), lambda i, j, k: (i, j)))])
        big[k_ff2] = mm(pend, f"ff2_dw_{l}", s["r"], dy2, mode="tn", outs=[BF16])
        big[k_ff1] = mm(pend, f"ff1_dw_{l}", s["h2"], da2, mode="tn", outs=[BF16])
        dh2 = mm(pend, f"ff1_dx_{l}", da2, full[k_ff1], mode="nt", outs=[BF16])
        dx, dm2, small["norm2_g"][l], do, dg1 = _norm_mod_bwd(
            cfg, f"norm2_bwd_{l}", s["x1"], w["norm2_g"][l][None], mod, 3, 4, dh2, dx, gate=(s["y1"], mod, 2))
        if l % 2 == 0:
            big[k_out] = _mm(f"ab_out_dw_{l}", s["cat"], do, mode="tn", outs=[BF16])
            dcat = _mm(f"ab_out_dx_{l}", do, full[k_out], mode="nt", outs=[BF16])
            d_rq, d_rk, d_rv, d_gt, dlg = with_carry(
                functools.partial(_retention_bwd, cfg, f"ret_bwd_{l}", s["rq"], s["rk"], s["p"], s["o"], s["st"], dcat,
                                  s["lgb"]), [k_ff2, k_ff1] if l == 0 else [], exchange_of, recv)
            d_aq, d_ak, d_av = with_carry(
                functools.partial(_attention_bwd, cfg, f"att_bwd_{l}", s["aq"], s["ak"], s["p"], s["cat"], s["lse"], dcat),
                [k_out] if l == 0 else [], exchange_of, recv)
            dp, dqg, dkg = _prep_bwd(cfg, f"prep_bwd_{l}", s["p"], cosf, sinf, s["qg"], s["kg"],
                                     d_rq, d_rk, d_rv, d_gt, d_aq, d_ak, d_av)
            small["ret_lg"][li] = jnp.sum(dlg[:, :, :, 0, 0], axis=0).T
            small["att_q_norm_g"][li], small["att_k_norm_g"][li] = dqg[0], dkg[0]
            big[k_in] = _mm(f"ab_in_dw_{l}", s["h"], dp, mode="tn", outs=[BF16])
            last = [(k_in, exchange_of([k_in]), recv)] if l == 0 else []
            dh = mm(last, f"ab_in_dx_{l}", dp, full[k_in], mode="nt", outs=[BF16], tk=768)
        else:
            big[k_out] = _mm(f"cm_out_dw_{l}", s["m"], do, mode="tn", outs=[BF16])
            dm = _mm(f"cm_out_dx_{l}", do, full[k_out], mode="nt", outs=[BF16])
            da, dws, dbs, dvg = _cm_bwd(cfg, f"cm_bwd_{l}", s["a"], s["vg"], s["ws"], s["bs"], dm)
            small["cm_w_s"][li], small["cm_b_s"][li], small["cm_v_norm_g"][li] = dws, dbs[:, :, 0], dvg[0]
            big[k_in] = _mm(f"cm_in_dw_{l}", s["h"], da, mode="tn", outs=[BF16])
            dh = _mm(f"cm_in_dx_{l}", da, full[k_in], mode="nt", outs=[BF16])
        below = (saved[l - 1]["y2"], mods[l - 1], 5) if l > 0 else None
        dx, dm1, small["norm1_g"][l], *rest = _norm_mod_bwd(
            cfg, f"norm1_bwd_{l}", s["x0"], w["norm1_g"][l][None], mod, 0, 1, dh, dx, gate=below, lat_only=l == 0)
        dmods[l] = jnp.concatenate([dm1, dg1, dm2, dg2], axis=2)
        if l > 0:
            dy2, dg2 = rest
    return loss, dx, recv, small, dmods


N_DEV = 8
N_CHIP = 4
MESH = pl.DeviceIdType.MESH
ANY = pl.BlockSpec(memory_space=pl.ANY)
BIG = {"ab_w_in": 1, "ab_w_out": 0, "cm_w_in": 1, "cm_w_out": 0, "ff_w1": 1, "ff_w2": 0}


class _Carry(NamedTuple):
    kind: str
    srcs: tuple
    axes: tuple


def _place():
    x, y, c = lax.axis_index("x"), lax.axis_index("y"), lax.axis_index("c")
    return x, y, c, [(1 - x, y), (x, 1 - y), (1 - x, 1 - y)]


def _shard_of(ref, axis, s, width):
    start = pl.multiple_of(s * width, LANES)
    if axis == 0:
        return ref.at[pl.ds(start, width), :]
    return ref.at[:, pl.ds(start, width)]


def _carry_out_shapes(carry):
    shapes = []
    for src, axis in zip(carry.srcs, carry.axes):
        shape = list(src.shape)
        if carry.kind == "gather":
            shape[axis] *= N_CHIP
        else:
            shape[axis] //= N_CHIP
            shape = [N_CHIP] + shape
        shapes.append(jax.ShapeDtypeStruct(tuple(shape), src.dtype))
    return shapes


def _carry_copies(carry, srcs, dsts, send_sems, recv_sems, local_sems):
    x, y, c, chips = _place()
    me = 2 * x + y
    copies = []
    for t, axis in enumerate(carry.axes):
        if carry.kind == "gather":
            own = _shard_of(dsts[t], axis, me, srcs[t].shape[axis])
            copies.append(pltpu.make_async_copy(srcs[t], own, local_sems.at[t]))
            parts = [(srcs[t], own)] * 3
        else:
            width = dsts[t].shape[1 + axis]
            copies.append(pltpu.make_async_copy(_shard_of(srcs[t], axis, me, width), dsts[t].at[3], local_sems.at[t]))
            parts = [(_shard_of(srcs[t], axis, 2 * px + py, width), dsts[t].at[j]) for j, (px, py) in enumerate(chips)]
        for j, (px, py) in enumerate(chips):
            copies.append(pltpu.make_async_remote_copy(
                src_ref=parts[j][0], dst_ref=parts[j][1], send_sem=send_sems.at[3 * t + j],
                recv_sem=recv_sems.at[3 * t + j], device_id=(px, py, c), device_id_type=MESH))
    return copies


def _comm_call(name, carry):
    nc = len(carry.srcs)

    def body(*refs):
        copies = _carry_copies(carry, refs[:nc], refs[nc:2 * nc], *refs[2 * nc:])
        for cp in copies:
            cp.start()
        for cp in copies:
            cp.wait()

    return pl.pallas_call(
        body, name=name, out_shape=_carry_out_shapes(carry), in_specs=[ANY] * nc, out_specs=[ANY] * nc,
        scratch_shapes=[pltpu.SemaphoreType.DMA((3 * nc,)), pltpu.SemaphoreType.DMA((3 * nc,)),
                        pltpu.SemaphoreType.DMA((nc,))],
    )(*carry.srcs)


def _allgather8(name, block):
    m_per, n = block.shape

    def body(x_ref, out_ref, send_sems, recv_sems, local_sem):
        x, y, c, chips = _place()
        me, sibling = (x, y, c), (x, y, 1 - c)

        def rows(px, py, pc):
            return out_ref.at[pl.ds((4 * px + 2 * py + pc) * m_per, m_per), :]

        def copy(k, blk, to, src=None):
            return pltpu.make_async_remote_copy(
                src_ref=rows(*blk) if src is None else src, dst_ref=rows(*blk),
                send_sem=send_sems.at[k], recv_sem=recv_sems.at[k], device_id=to, device_id_type=MESH)

        mine = pltpu.make_async_copy(x_ref, rows(*me), local_sem)
        mine.start()
        first = [copy(0, me, sibling, src=x_ref)]
        first += [copy(1 + j, me, (*chip, c), src=x_ref) for j, chip in enumerate(chips)]
        for cp in first:
            cp.start()
        passed = [copy(4 + j, (*chip, c), sibling) for j, chip in enumerate(chips)]
        for j, chip in enumerate(chips):
            copy(1 + j, (*chip, c), me).wait_recv()
            passed[j].start()
        copy(0, sibling, me).wait_recv()
        for j, chip in enumerate(chips):
            copy(4 + j, (*chip, 1 - c), me).wait_recv()
        for cp in first + passed:
            cp.wait_send()
        mine.wait()

    return pl.pallas_call(
        body, name=name, out_shape=jax.ShapeDtypeStruct((N_DEV * m_per, n), block.dtype),
        in_specs=[pl.BlockSpec(memory_space=pltpu.VMEM)], out_specs=pl.BlockSpec(memory_space=pltpu.VMEM),
        scratch_shapes=[pltpu.SemaphoreType.DMA((7,)), pltpu.SemaphoreType.DMA((7,)), pltpu.SemaphoreType.DMA],
        compiler_params=pltpu.CompilerParams(vmem_limit_bytes=VMEM_LIMIT),
    )(block)


def _swap_sibling(parts):
    n_t = len(parts)

    def body(*refs):
        srcs, outs = refs[:n_t], refs[n_t:2 * n_t]
        send_sems, recv_sems = refs[2 * n_t:]
        x, y, c, _ = _place()
        copies = []
        for t in range(n_t):
            cp = pltpu.make_async_remote_copy(
                src_ref=srcs[t], dst_ref=outs[t], send_sem=send_sems.at[t], recv_sem=recv_sems.at[t],
                device_id=(x, y, 1 - c), device_id_type=MESH)
            cp.start()
            copies.append(cp)
        for cp in copies:
            cp.wait()

    return pl.pallas_call(
        body, name="swap_sibling", out_shape=[jax.ShapeDtypeStruct(p.shape, p.dtype) for p in parts],
        in_specs=[ANY] * n_t, out_specs=[ANY] * n_t,
        scratch_shapes=[pltpu.SemaphoreType.DMA((n_t,)), pltpu.SemaphoreType.DMA((n_t,))],
    )(*parts)


def _rows_view(a):
    if a.ndim == 1:
        return a.reshape(1, a.shape[0])
    return a.reshape(-1, a.shape[-1])


def _row_tile(rows, cols, target_elems=1 << 17):
    tr = rows
    while tr % 16 == 0 and tr * cols > target_elems:
        tr //= 2
    return tr


def _sum_leading(name, a):
    n, rows, cols = a.shape
    tr = _row_tile(rows, cols * n, target_elems=1 << 20)

    def body(a_ref, o_ref):
        acc = a_ref[0].astype(F32)
        for i in range(1, n):
            acc = acc + a_ref[i].astype(F32)
        o_ref[...] = acc

    return pl.pallas_call(
        body, name=name, grid=(rows // tr,),
        in_specs=[pl.BlockSpec((n, tr, cols), lambda i: (0, i, 0))],
        out_specs=pl.BlockSpec((tr, cols), lambda i: (i, 0)),
        out_shape=jax.ShapeDtypeStruct((rows, cols), F32),
        compiler_params=_params(("parallel",)),
    )(a)


def _silu_rows(name, x):
    def body(x_ref, o_ref):
        v = x_ref[...]
        o_ref[...] = v * jax.nn.sigmoid(v)

    return pl.pallas_call(body, name=name, out_shape=jax.ShapeDtypeStruct(x.shape, F32))(x)


def _silu_bwd_rows(name, x, dy):
    def body(x_ref, dy_ref, o_ref):
        v = x_ref[...]
        sg = jax.nn.sigmoid(v)
        o_ref[...] = dy_ref[...] * (sg * (1.0 + v * (1.0 - sg)))

    return pl.pallas_call(body, name=name, out_shape=jax.ShapeDtypeStruct(x.shape, F32))(x, dy)


def _adamw(name, w, g_parts, m, v):
    shape = w.shape
    w2, m2, v2 = _rows_view(w), _rows_view(m), _rows_view(v)
    gs = [_rows_view(g) for g in g_parts]
    rows, cols = w2.shape
    tr = _row_tile(rows, cols)
    ng = len(gs)

    def body(*refs):
        w_ref, m_ref, v_ref = refs[0], refs[1], refs[2]
        g_refs = refs[3:3 + ng]
        g_out, d_out, m_out, v_out = refs[3 + ng:]
        g = g_refs[0][...]
        for r in g_refs[1:]:
            g = g + r[...]
        m1 = ADAM_B1 * m_ref[...] + (1.0 - ADAM_B1) * g
        v1 = ADAM_B2 * v_ref[...] + (1.0 - ADAM_B2) * jnp.square(g)
        m_hat = m1 / (1.0 - ADAM_B1 ** ADAM_STEP)
        v_hat = v1 / (1.0 - ADAM_B2 ** ADAM_STEP)
        g_out[...] = g
        d_out[...] = -ADAM_LR * (m_hat / (jnp.sqrt(v_hat) + ADAM_EPS) + ADAM_WD * w_ref[...])
        m_out[...] = m1
        v_out[...] = v1

    spec = pl.BlockSpec((tr, cols), lambda i: (i, 0))
    res = pl.pallas_call(
        body, name=name, grid=(rows // tr,), in_specs=[spec] * (3 + ng), out_specs=[spec] * 4,
        out_shape=[jax.ShapeDtypeStruct((rows, cols), F32)] * 4,
        compiler_params=_params(("parallel",)),
    )(w2, m2, v2, *gs)
    return tuple(r.reshape(shape) for r in res)


MOD_ROWS = 48


def kernel(x, c, ctx, c_ctx, mod_w, mod_b, norm1_g, norm2_g, ab_w_in, ab_w_out, ret_decay, att_q_norm_g, att_k_norm_g, cm_w_in, cm_v_norm_g, cm_w_s, cm_b_s, cm_w_out, ff_w1, ff_w2, loss_target, m_c_ctx, m_mod_w, m_mod_b, m_norm1_g, m_norm2_g, m_ab_w_in, m_ab_w_out, m_ret_decay, m_att_q_norm_g, m_att_k_norm_g, m_cm_w_in, m_cm_v_norm_g, m_cm_w_s, m_cm_b_s, m_cm_w_out, m_ff_w1, m_ff_w2, v_c_ctx, v_mod_w, v_mod_b, v_norm1_g, v_norm2_g, v_ab_w_in, v_ab_w_out, v_ret_decay, v_att_q_norm_g, v_att_k_norm_g, v_cm_w_in, v_cm_v_norm_g, v_cm_w_s, v_cm_b_s, v_cm_w_out, v_ff_w1, v_ff_w2):
    B, SL, D = x.shape
    cfg = Cfg(B=B, SC=ctx.shape[1], SL=SL, D=D, FF=ff_w1.shape[2] * N_CHIP)
    L = N_LAYERS
    n_ex = B * N_DEV
    mcols = mod_w.shape[2]
    weights = dict(c_ctx=c_ctx, mod_w=mod_w, mod_b=mod_b, norm1_g=norm1_g, norm2_g=norm2_g, ab_w_in=ab_w_in,
                   ab_w_out=ab_w_out, ret_decay=ret_decay, att_q_norm_g=att_q_norm_g, att_k_norm_g=att_k_norm_g,
                   cm_w_in=cm_w_in, cm_v_norm_g=cm_v_norm_g, cm_w_s=cm_w_s, cm_b_s=cm_b_s, cm_w_out=cm_w_out,
                   ff_w1=ff_w1, ff_w2=ff_w2)
    m_in = dict(c_ctx=m_c_ctx, mod_w=m_mod_w, mod_b=m_mod_b, norm1_g=m_norm1_g, norm2_g=m_norm2_g, ab_w_in=m_ab_w_in,
                ab_w_out=m_ab_w_out, ret_decay=m_ret_decay, att_q_norm_g=m_att_q_norm_g, att_k_norm_g=m_att_k_norm_g,
                cm_w_in=m_cm_w_in, cm_v_norm_g=m_cm_v_norm_g, cm_w_s=m_cm_w_s, cm_b_s=m_cm_b_s, cm_w_out=m_cm_w_out,
                ff_w1=m_ff_w1, ff_w2=m_ff_w2)
    v_in = dict(c_ctx=v_c_ctx, mod_w=v_mod_w, mod_b=v_mod_b, norm1_g=v_norm1_g, norm2_g=v_norm2_g, ab_w_in=v_ab_w_in,
                ab_w_out=v_ab_w_out, ret_decay=v_ret_decay, att_q_norm_g=v_att_q_norm_g, att_k_norm_g=v_att_k_norm_g,
                cm_w_in=v_cm_w_in, cm_v_norm_g=v_cm_v_norm_g, cm_w_s=v_cm_w_s, cm_b_s=v_cm_b_s, cm_w_out=v_cm_w_out,
                ff_w1=v_ff_w1, ff_w2=v_ff_w2)
    xi, yi, ci = lax.axis_index("x"), lax.axis_index("y"), lax.axis_index("c")
    chip = 2 * xi + yi
    dev = 2 * chip + ci

    shards = {n: [weights[n][i].astype(BF16) for i in range(weights[n].shape[0])] for n in BIG}
    vgw = cm_v_norm_g.shape[1]
    blk = jnp.zeros((8, D), F32).at[:B].set(c).at[B:B + 2, :vgw].set(cm_v_norm_g)
    g0 = _allgather8("gather_c", blk).reshape(N_DEV, 8, D)
    c_all = g0[:, :B].reshape(n_ex, D)
    vg_full = jnp.concatenate([g0[2 * s, B:B + 2, :vgw] for s in range(N_CHIP)], axis=-1)

    pre = jnp.zeros((MOD_ROWS, D), F32).at[:n_ex].set(c_all).at[n_ex].set(c_ctx)
    act = _silu_rows("silu_c", pre)
    mpart = jnp.stack([_mm(f"mod_fwd_{l}", act, mod_w, mode="nn", layer=l, outs=[F32], tn=mcols) for l in range(L)])
    g1 = _allgather8("gather_mod", mpart.reshape(L * MOD_ROWS, mcols)).reshape(N_DEV, L, MOD_ROWS, mcols)
    mod_all = jnp.concatenate([g1[2 * s] for s in range(N_CHIP)], axis=-1) + mod_b[:, None, :]
    mod_lat = lax.dynamic_slice_in_dim(mod_all, dev * B, B, axis=1)
    mod_ctx = jnp.broadcast_to(mod_all[:, n_ex][:, None], mod_lat.shape)
    mods = jnp.stack([mod_ctx, mod_lat], axis=2).reshape(L, B, 2, 6, D)

    w = dict(norm1_g=norm1_g, norm2_g=norm2_g, ret_decay=ret_decay, att_q_norm_g=att_q_norm_g,
             att_k_norm_g=att_k_norm_g, cm_v_norm_g=vg_full, cm_w_s=cm_w_s, cm_b_s=cm_b_s)
    xcat = jnp.concatenate([ctx, x], axis=1).reshape(cfg.T, D)
    loss_local, dx_lat, recv, small, dmods = _local_step(cfg, xcat, loss_target.reshape(B * SL, D), mods, shards, w)
    loss = lax.psum(loss_local, ("x", "y", "c"))
    grad_x = dx_lat.reshape(B, SL, D)

    part = [jnp.stack([_sum_leading(f"sum_{n}_{i}", recv[(n, i)]) for i in range(weights[n].shape[0])]) for n in BIG]
    other = _swap_sibling(part)
    out = {}
    for n, p_mine, p_other in zip(BIG, part, other):
        out[n] = _adamw(f"adamw_{n}", weights[n], [p_mine, p_other], m_in[n], v_in[n])

    dmod = jnp.stack(dmods).reshape(L, B, 2, 6 * D)
    dmod_lat = dmod[:, :, 1]
    dmod_ctx = jnp.sum(dmod[:, :, 0], axis=1)
    d_ret = jnp.stack(small["ret_lg"]) * jax.nn.sigmoid(-ret_decay)
    summed = [dmod_ctx.reshape(-1), jnp.stack(small["norm1_g"]).reshape(-1), jnp.stack(small["norm2_g"]).reshape(-1),
              jnp.stack(small["cm_v_norm_g"]).reshape(-1), jnp.stack(small["cm_w_s"]).reshape(-1),
              jnp.stack(small["cm_b_s"]).reshape(-1), jnp.stack(small["att_q_norm_g"]).reshape(-1),
              jnp.stack(small["att_k_norm_g"]).reshape(-1), d_ret.reshape(-1)]
    sizes = [int(a.shape[0]) for a in summed]
    flat = jnp.concatenate(summed + [dmod_lat.reshape(-1)])
    n_sum = sum(sizes)
    n_sum_rows = -(-n_sum // D)
    lat_rows = (L * B * 6 * D) // D
    pack_rows = -(-(n_sum_rows + lat_rows) // 8) * 8
    packed = jnp.zeros((pack_rows * D,), F32).at[:n_sum].set(flat[:n_sum])
    packed = packed.at[n_sum_rows * D:(n_sum_rows + lat_rows) * D].set(flat[n_sum:]).reshape(pack_rows, D)
    g2 = _allgather8("gather_small", packed).reshape(N_DEV, pack_rows, D)
    tot = _sum_leading("sum_small", g2[:, :n_sum_rows]).reshape(-1)
    pieces, off = [], 0
    for sz in sizes:
        pieces.append(tot[off:off + sz])
        off += sz
    dmod_ctx_t, g_n1, g_n2, g_vg, g_ws, g_bs, g_qg, g_kg, g_rd = pieces
    dmod_ctx_t = dmod_ctx_t.reshape(L, 6 * D)
    dmod_lat_all = g2[:, n_sum_rows:n_sum_rows + lat_rows].reshape(N_DEV, L, B, 6 * D)
    dmod_rows = jnp.zeros((L, MOD_ROWS, 6 * D), F32)
    dmod_rows = dmod_rows.at[:, :n_ex].set(jnp.transpose(dmod_lat_all, (1, 0, 2, 3)).reshape(L, n_ex, 6 * D))
    dmod_rows = dmod_rows.at[:, n_ex].set(dmod_ctx_t)
    g_mod_b = _sum_leading("sum_mod_b", jnp.transpose(dmod_rows, (1, 0, 2)))
    dmod_mine = lax.dynamic_slice_in_dim(dmod_rows, chip * mcols, mcols, axis=2)
    g_mod_w = jnp.stack([_mm(f"mod_dw_{l}", act, dmod_mine[l], mode="tn", outs=[F32], tn=mcols) for l in range(L)])
    ctx8 = jnp.zeros((L, 8, mcols), F32).at[:, 0].set(dmod_mine[:, n_ex])
    dcc = [_mm(f"mod_dctx_{l}", ctx8[l], mod_w, mode="nt", layer=l, outs=[F32], tk=mcols) for l in range(L)]
    dcc = _sum_leading("sum_dctx_layers", jnp.stack(dcc))
    g3 = _allgather8("gather_dctx", dcc).reshape(N_DEV, 8, D)
    dcc_t = _sum_leading("sum_dctx_chips", g3[0::2])[0:1]
    g_c_ctx = _silu_bwd_rows("silu_bwd_cctx", c_ctx[None], dcc_t)[0]

    vg_mine = lax.dynamic_slice_in_dim(g_vg.reshape(2, -1), chip * vgw, vgw, axis=1)
    small_g = dict(c_ctx=g_c_ctx, mod_w=g_mod_w, mod_b=g_mod_b, norm1_g=g_n1.reshape(norm1_g.shape),
                   norm2_g=g_n2.reshape(norm2_g.shape), ret_decay=g_rd.reshape(ret_decay.shape),
                   att_q_norm_g=g_qg.reshape(att_q_norm_g.shape), att_k_norm_g=g_kg.reshape(att_k_norm_g.shape),
                   cm_v_norm_g=vg_mine, cm_w_s=g_ws.reshape(cm_w_s.shape), cm_b_s=g_bs.reshape(cm_b_s.shape))
    for n, g in small_g.items():
        out[n] = _adamw(f"adamw_{n}", weights[n], [g], m_in[n], v_in[n])

    order = list(weights)
    return (loss, grad_x, *[out[n][0] for n in order], *[out[n][1] for n in order],
            *[out[n][2] for n in order], *[out[n][3] for n in order])
```

```python
import functools
import math
from typing import NamedTuple

import jax
import jax.numpy as jnp
from jax import lax
from jax.experimental import pallas as pl
from jax.experimental.pallas import tpu as pltpu

F32 = jnp.float32
BF16 = jnp.bfloat16
EPS = 1e-6
ROPE_BASE = 10000.0
LANES = 128
CHUNK = 128
N_LAYERS = 4
VMEM_LIMIT = 56 * 1024 * 1024

ADAM_LR = 0.001
ADAM_B1 = 0.9
ADAM_B2 = 0.999
ADAM_EPS = 1e-08
ADAM_WD = 0.01
ADAM_STEP = 10


class Cfg(NamedTuple):
    B: int = 4
    SC: int = 256
    SL: int = 2048
    D: int = 1024
    FF: int = 4096
    GRID_W: int = 64
    H: int = 4
    KV: int = 2
    CMW: int = 1024
    CMG: int = 8

    @property
    def S(self):
        return self.SC + self.SL

    @property
    def T(self):
        return self.B * self.S

    @property
    def TM(self):
        return self.SC

    @property
    def TPE(self):
        return self.S // self.SC

    @property
    def ABW(self):
        return (5 * self.H + 2 * self.KV) * CHUNK


def _tile(dim, pref):
    t = min(dim, pref)
    while dim % t:
        t -= LANES
    return t


def _dot(a, b):
    return lax.dot_general(a, b, (((1,), (0,)), ((), ())), preferred_element_type=F32)


def _dot_nt(a, b):
    return lax.dot_general(a, b, (((1,), (1,)), ((), ())), preferred_element_type=F32)


def _dot_tn(a, b):
    return lax.dot_general(a, b, (((0,), (0,)), ((), ())), preferred_element_type=F32)


def _params(sem, vmem=VMEM_LIMIT):
    return pltpu.CompilerParams(dimension_semantics=sem, vmem_limit_bytes=vmem)


def _mod_index(cfg):
    tpe = cfg.TPE
    return lambda i: (i // tpe, jnp.minimum(i % tpe, 1), 0, 0)


def _mm(name, a, b, *, mode, outs, tm=1024, tn=1024, tk=1024, layer=None, epi=None, extras=(), carry=None):
    bshape = b.shape[1:] if layer is not None else b.shape
    if mode == "nn":
        (M, K), N = a.shape, bshape[1]
    elif mode == "nt":
        (M, K), N = a.shape, bshape[0]
    else:
        (K, M), N = a.shape, bshape[1]
    tm, tn, tk = _tile(M, tm), _tile(N, tn), _tile(K, tk)
    nk = K // tk
    a_spec = (pl.BlockSpec((tk, tm), lambda i, j, k: (k, i)) if mode == "tn"
              else pl.BlockSpec((tm, tk), lambda i, j, k: (i, k)))
    if mode == "nt":
        bblk, bidx = (tn, tk), (lambda i, j, k: (j, k))
    else:
        bblk, bidx = (tk, tn), (lambda i, j, k: (k, j))
    if layer is not None:
        b_spec = pl.BlockSpec((None,) + bblk, lambda i, j, k: (layer,) + bidx(i, j, k))
    else:
        b_spec = pl.BlockSpec(bblk, bidx)
    ne, no = len(extras), len(outs)
    nc = len(carry.srcs) if carry is not None else 0
    dot = {"nn": _dot, "nt": _dot_nt, "tn": _dot_tn}[mode]
    grid = (M // tm, N // tn, nk)

    def body(*refs):
        a_ref, b_ref = refs[0], refs[1]
        ex, out_refs = refs[2:2 + ne], refs[2 + ne + nc:2 + ne + nc + no]
        row_tile = pl.program_id(0)

        if nc:
            step = (pl.program_id(0) * grid[1] + pl.program_id(1)) * grid[2] + pl.program_id(2)
            c_src = refs[2 + ne:2 + ne + nc]
            c_dst = refs[2 + ne + nc + no:2 + ne + 2 * nc + no]
            sems = refs[2 + ne + 2 * nc + no:2 + ne + 2 * nc + no + 3]

            @pl.when(step == 0)
            def _():
                for cp in _carry_copies(carry, c_src, c_dst, *sems):
                    cp.start()

        def finish(acc):
            res = epi(acc, row_tile, *ex) if epi is not None else (acc,)
            for r, o in zip(res, out_refs):
                o[...] = r.astype(o.dtype)

        part = dot(a_ref[...].astype(BF16), b_ref[...].astype(BF16))
        if nk == 1:
            finish(part)
        else:
            acc_ref = refs[-1]
            k = pl.program_id(2)

            @pl.when(k == 0)
            def _():
                acc_ref[...] = part

            @pl.when(k > 0)
            def _():
                acc_ref[...] += part

            @pl.when(k == nk - 1)
            def _():
                finish(acc_ref[...])

        if nc:
            @pl.when(step == grid[0] * grid[1] * grid[2] - 1)
            def _():
                for cp in _carry_copies(carry, c_src, c_dst, *sems):
                    cp.wait()

    scratch = [pltpu.SemaphoreType.DMA((3 * nc,)), pltpu.SemaphoreType.DMA((3 * nc,)),
               pltpu.SemaphoreType.DMA((nc,))] if nc else []
    if nk > 1:
        scratch.append(pltpu.VMEM((tm, tn), F32))
    res = pl.pallas_call(
        body, name=name, grid=grid,
        in_specs=[a_spec, b_spec] + [s for _, s in extras] + [ANY] * nc,
        out_specs=[pl.BlockSpec((tm, tn), lambda i, j, k: (i, j)) for _ in outs] + [ANY] * nc,
        out_shape=[jax.ShapeDtypeStruct((M, N), d) for d in outs] + (_carry_out_shapes(carry) if nc else []),
        scratch_shapes=scratch,
        compiler_params=_params(("arbitrary",) * 3 if nc else ("parallel", "parallel", "arbitrary")),
    )(a, b, *[x for x, _ in extras], *(carry.srcs if nc else ()))
    if nc:
        return (res[0] if no == 1 else res[:no]), res[no:]
    return res[0] if no == 1 else res


def _ff_bwd(name, first, second, weight, kind, carry=None):
    (T, D), FF = first.shape, second.shape[1]
    tm = _tile(T, 512)
    cw = _tile(FF, 1024)
    n_steps = T // tm
    nc = len(carry.srcs) if carry is not None else 0

    def body(*refs):
        a_ref, b_ref, w_ref = refs[:3]
        c_src = refs[3:3 + nc]
        x_ref, dw_ref = refs[3 + nc:5 + nc]
        c_dst, sems = refs[5 + nc:5 + 2 * nc], refs[5 + 2 * nc:5 + 2 * nc + 3] if nc else ()
        acc_ref = refs[-1]
        i = pl.program_id(0)
        if nc:
            _carry_begin(carry, c_src, c_dst, sems, i)
        a = a_ref[...]
        dh = None
        for c in range(FF // cw):
            cols = pl.ds(c * cw, cw)
            if kind == "w2":
                r = b_ref[:, cols]
                x_ref[:, cols] = (_dot_nt(a, w_ref[cols, :]) * (2.0 * jnp.sqrt(r.astype(F32)))).astype(BF16)
                part, dst = _dot_tn(r, a), acc_ref.at[cols, :]
            else:
                da = b_ref[:, cols]
                term = _dot_nt(da, w_ref[:, cols])
                dh = term if dh is None else dh + term
                part, dst = _dot_tn(a, da), acc_ref.at[:, cols]

            @pl.when(i == 0)
            def _():
                dst[...] = part

            @pl.when(i > 0)
            def _():
                dst[...] += part

        if kind == "w1":
            x_ref[...] = dh.astype(BF16)

        @pl.when(i == n_steps - 1)
        def _():
            dw_ref[...] = acc_ref[...].astype(BF16)

        if nc:
            _carry_end(carry, c_src, c_dst, sems, i, n_steps)

    resident = lambda shape: pl.BlockSpec(shape, lambda i: (0, 0), pipeline_mode=pl.Buffered(1))
    x_cols = FF if kind == "w2" else D
    res = pl.pallas_call(
        body, name=name, grid=(n_steps,),
        in_specs=[pl.BlockSpec((tm, D), lambda i: (i, 0)), pl.BlockSpec((tm, FF), lambda i: (i, 0)),
                  resident(weight.shape)] + [ANY] * nc,
        out_specs=[pl.BlockSpec((tm, x_cols), lambda i: (i, 0)), resident(weight.shape)] + [ANY] * nc,
        out_shape=[jax.ShapeDtypeStruct((T, x_cols), BF16), jax.ShapeDtypeStruct(weight.shape, BF16)]
        + (_carry_out_shapes(carry) if nc else []),
        scratch_shapes=_carry_scratch(nc) + [pltpu.VMEM(weight.shape, F32)],
        compiler_params=_params(("arbitrary",)),
    )(first, second, weight, *(carry.srcs if nc else ()))
    return res[:2], res[2:]


def _norm_mod_fwd(cfg, name, x, gain, mod, ish, isc):
    T, D, TM = cfg.T, cfg.D, cfg.TM

    def body(x_ref, g_ref, mod_ref, h_ref):
        x = x_ref[...]
        rstd = lax.rsqrt(jnp.mean(x * x, axis=-1, keepdims=True) + EPS)
        n = x * rstd * g_ref[...]
        h = n * (1.0 + mod_ref[pl.ds(isc, 1), :]) + mod_ref[pl.ds(ish, 1), :]
        h_ref[...] = h.astype(BF16)

    return pl.pallas_call(
        body, name=name, grid=(T // TM,),
        in_specs=[pl.BlockSpec((TM, D), lambda i: (i, 0)), pl.BlockSpec((1, D), lambda i: (0, 0)),
                  pl.BlockSpec((None, None, 6, D), _mod_index(cfg))],
        out_specs=pl.BlockSpec((TM, D), lambda i: (i, 0)),
        out_shape=jax.ShapeDtypeStruct((T, D), BF16),
        compiler_params=_params(("parallel",)),
    )(x, gain, mod)


def _norm_mod_bwd(cfg, name, x, gain, mod, ish, isc, dh, dres, gate=None, lat_only=False):
    T, D, TM, TPE = cfg.T, cfg.D, cfg.TM, cfg.TPE
    ng = 2 if gate is not None else 0
    dx_spec = (pl.BlockSpec((TM, D), lambda i: ((i // TPE) * (TPE - 1) + jnp.maximum(i % TPE - 1, 0), 0)) if lat_only
               else pl.BlockSpec((TM, D), lambda i: (i, 0)))
    dx_rows = cfg.B * cfg.SL if lat_only else T

    def body(*refs):
        x_ref, g_ref, mod_ref, dh_ref, dres_ref = refs[:5]
        dx_ref, dmod_ref, dgain_ref = refs[5 + ng:8 + ng]
        i = pl.program_id(0)
        t = i % TPE
        x = x_ref[...]
        g = g_ref[...]
        dh = dh_ref[...].astype(F32)
        rstd = lax.rsqrt(jnp.mean(x * x, axis=-1, keepdims=True) + EPS)
        xhat = x * rstd
        dn = dh * (1.0 + mod_ref[pl.ds(isc, 1), :])
        dsh = jnp.sum(dh, axis=0, keepdims=True)
        dsc = jnp.sum(dh * (xhat * g), axis=0, keepdims=True)
        dgain = jnp.sum(dn * xhat, axis=0, keepdims=True)
        dxh = dn * g
        dx = rstd * (dxh - xhat * jnp.mean(dxh * xhat, axis=-1, keepdims=True)) + dres_ref[...]
        dx_ref[...] = dx
        sums = [(dmod_ref.at[pl.ds(0, 1), :], dsh), (dmod_ref.at[pl.ds(1, 1), :], dsc)]
        if ng:
            y_ref, gmod_ref = refs[5:7]
            dy_ref, dgate_ref = refs[8 + ng:]
            dy_ref[...] = (dx * gmod_ref[pl.ds(gate[2], 1), :]).astype(BF16)
            sums.append((dgate_ref, jnp.sum(dx * y_ref[...].astype(F32), axis=0, keepdims=True)))

        @pl.when(t <= 1)
        def _():
            for ref, val in sums:
                ref[...] = val

        @pl.when(t > 1)
        def _():
            for ref, val in sums:
                ref[...] += val

        @pl.when(i == 0)
        def _():
            dgain_ref[...] = dgain

        @pl.when(i > 0)
        def _():
            dgain_ref[...] += dgain

    tok = pl.BlockSpec((TM, D), lambda i: (i, 0))
    mod_spec = pl.BlockSpec((None, None, 6, D), _mod_index(cfg))
    res = pl.pallas_call(
        body, name=name, grid=(T // TM,),
        in_specs=[tok, pl.BlockSpec((1, D), lambda i: (0, 0)), mod_spec, tok, tok] + ([tok, mod_spec] if ng else []),
        out_specs=[dx_spec, pl.BlockSpec((None, None, 2, D), _mod_index(cfg)), pl.BlockSpec((1, D), lambda i: (0, 0))]
        + ([tok, pl.BlockSpec((None, None, 1, D), _mod_index(cfg))] if ng else []),
        out_shape=[jax.ShapeDtypeStruct((dx_rows, D), F32), jax.ShapeDtypeStruct((cfg.B, 2, 2, D), F32),
                   jax.ShapeDtypeStruct((1, D), F32)]
        + ([jax.ShapeDtypeStruct((T, D), BF16), jax.ShapeDtypeStruct((cfg.B, 2, 1, D), F32)] if ng else []),
        compiler_params=_params(("arbitrary",)),
    )(x, gain, mod, dh, dres, *(gate[:2] if ng else ()))
    return res


def _gate_bwd(cfg, name, dx, y, mod, igate):
    T, D, TM, TPE = cfg.T, cfg.D, cfg.TM, cfg.TPE

    def body(dx_ref, y_ref, mod_ref, dy_ref, dg_ref):
        t = pl.program_id(0) % TPE
        dx = dx_ref[...]
        dy_ref[...] = (dx * mod_ref[pl.ds(igate, 1), :]).astype(BF16)
        dg = jnp.sum(dx * y_ref[...].astype(F32), axis=0, keepdims=True)

        @pl.when(t <= 1)
        def _():
            dg_ref[...] = dg

        @pl.when(t > 1)
        def _():
            dg_ref[...] += dg

    tok = pl.BlockSpec((TM, D), lambda i: (i, 0))
    return pl.pallas_call(
        body, name=name, grid=(T // TM,),
        in_specs=[tok, tok, pl.BlockSpec((None, None, 6, D), _mod_index(cfg))],
        out_specs=[tok, pl.BlockSpec((None, None, 1, D), _mod_index(cfg))],
        out_shape=[jax.ShapeDtypeStruct((T, D), BF16), jax.ShapeDtypeStruct((cfg.B, 2, 1, D), F32)],
        compiler_params=_params(("arbitrary",)),
    )(dx, y, mod)


def _loss_grad(cfg, x, tgt):
    T, D, TM, TPE = cfg.T, cfg.D, cfg.TM, cfg.TPE

    def body(x_ref, t_ref, dx_ref, loss_ref):
        i = pl.program_id(0)
        t = i % TPE

        @pl.when(i == 0)
        def _():
            loss_ref[...] = jnp.zeros_like(loss_ref)

        @pl.when(t == 0)
        def _():
            dx_ref[...] = jnp.zeros_like(dx_ref)

        @pl.when(t > 0)
        def _():
            err = x_ref[...] - t_ref[...]
            dx_ref[...] = err * (1.0 / D)
            loss_ref[...] += 0.5 * jnp.sum(jnp.mean(err * err, axis=-1, keepdims=True), axis=0, keepdims=True)

    tok = pl.BlockSpec((TM, D), lambda i: (i, 0))
    tgt_spec = pl.BlockSpec((TM, D), lambda i: ((i // TPE) * (TPE - 1) + jnp.maximum(i % TPE - 1, 0), 0))
    dx, loss = pl.pallas_call(
        body, name="loss_grad", grid=(T // TM,),
        in_specs=[tok, tgt_spec], out_specs=[tok, pl.BlockSpec((8, LANES), lambda i: (0, 0))],
        out_shape=[jax.ShapeDtypeStruct((T, D), F32), jax.ShapeDtypeStruct((8, LANES), F32)],
        compiler_params=_params(("arbitrary",)),
    )(x, tgt)
    return loss[0, 0], dx


def _rope_tables(cfg):
    rows = cfg.SL // cfg.GRID_W
    row = jnp.repeat(jnp.arange(rows, dtype=F32), cfg.GRID_W)
    col = jnp.tile(jnp.arange(cfg.GRID_W, dtype=F32), rows)
    n_freq = CHUNK // 4
    inv = ROPE_BASE ** (-jnp.arange(n_freq, dtype=F32) / n_freq)
    ang = jnp.concatenate([row[:, None] * inv[None, :], col[:, None] * inv[None, :]], axis=-1)
    cos, sin = jnp.cos(ang), jnp.sin(ang)
    cosf = jnp.concatenate([jnp.ones((cfg.SC, CHUNK), F32), jnp.concatenate([cos, cos], axis=-1)], axis=0)
    sinf = jnp.concatenate([jnp.zeros((cfg.SC, CHUNK), F32), jnp.concatenate([-sin, sin], axis=-1)], axis=0)
    return cosf, sinf


def _rope(x, cosf, sinf):
    return x * cosf + pltpu.roll(x, CHUNK // 2, 1) * sinf


def _irope(dy, cosf, sinf):
    return dy * cosf - pltpu.roll(dy, CHUNK // 2, 1) * sinf


def _prep_fwd(cfg, name, p, cosf, sinf, qg, kg):
    T, TM, TPE, H, KV = cfg.T, cfg.TM, cfg.TPE, cfg.H, cfg.KV
    HW = H * CHUNK
    kscale = CHUNK ** -0.5

    def body(p_ref, c_ref, s_ref, qg_ref, kg_ref, rq_ref, rk_ref, aq_ref, ak_ref):
        cosf, sinf = c_ref[...], s_ref[...]

        def normed(x, g):
            return x * lax.rsqrt(jnp.mean(x * x, axis=-1, keepdims=True) + EPS) * g

        def seg(col):
            return p_ref[:, pl.ds(col, CHUNK)].astype(F32)

        for h in range(H):
            sl = pl.ds(h * CHUNK, CHUNK)
            rq_ref[:, sl] = _rope(seg(h * CHUNK), cosf, sinf)
            rk_ref[:, sl] = _rope(seg(HW + h * CHUNK), cosf, sinf) * kscale
            aq_ref[:, sl] = (_rope(normed(seg(4 * HW + h * CHUNK), qg_ref[...]), cosf, sinf) * ATT_SCALE).astype(BF16)
        for h in range(KV):
            ak_ref[:, pl.ds(h * CHUNK, CHUNK)] = _rope(
                normed(seg(5 * HW + h * CHUNK), kg_ref[...]), cosf, sinf).astype(BF16)

    tab = pl.BlockSpec((TM, CHUNK), lambda i: (i % TPE, 0))
    vec = pl.BlockSpec((1, CHUNK), lambda i: (0, 0))
    return pl.pallas_call(
        body, name=name, grid=(T // TM,),
        in_specs=[pl.BlockSpec((TM, cfg.ABW), lambda i: (i, 0)), tab, tab, vec, vec],
        out_specs=[pl.BlockSpec((TM, HW), lambda i: (i, 0))] * 3 + [pl.BlockSpec((TM, KV * CHUNK), lambda i: (i, 0))],
        out_shape=[jax.ShapeDtypeStruct((T, HW), F32), jax.ShapeDtypeStruct((T, HW), F32),
                   jax.ShapeDtypeStruct((T, HW), BF16), jax.ShapeDtypeStruct((T, KV * CHUNK), BF16)],
        compiler_params=_params(("parallel",)),
    )(p, cosf, sinf, qg, kg)


def _prep_bwd(cfg, name, p, cosf, sinf, qg, kg, d_rq, d_rk, d_rv, d_gate, d_aq, d_ak, d_av):
    T, TM, TPE, H, KV = cfg.T, cfg.TM, cfg.TPE, cfg.H, cfg.KV
    HW = H * CHUNK
    kscale = CHUNK ** -0.5

    def body(p_ref, c_ref, s_ref, qg_ref, kg_ref, drq_ref, drk_ref, drv_ref, dgt_ref, daq_ref, dak_ref, dav_ref,
             dp_ref, dqg_ref, dkg_ref):
        i = pl.program_id(0)
        cosf, sinf = c_ref[...], s_ref[...]

        def norm_bwd(x, g, dn):
            rstd = lax.rsqrt(jnp.mean(x * x, axis=-1, keepdims=True) + EPS)
            xhat = x * rstd
            dg = jnp.sum(dn * xhat, axis=0, keepdims=True)
            dxh = dn * g
            return rstd * (dxh - xhat * jnp.mean(dxh * xhat, axis=-1, keepdims=True)), dg

        dqg = jnp.zeros((1, CHUNK), F32)
        dkg = jnp.zeros((1, CHUNK), F32)
        for h in range(H):
            sl = pl.ds(h * CHUNK, CHUNK)
            dp_ref[:, pl.ds(h * CHUNK, CHUNK)] = _irope(drq_ref[:, sl].astype(F32), cosf, sinf).astype(BF16)
            dp_ref[:, pl.ds(HW + h * CHUNK, CHUNK)] = (_irope(drk_ref[:, sl].astype(F32), cosf, sinf)
                                                       * kscale).astype(BF16)
            dp_ref[:, pl.ds(2 * HW + h * CHUNK, CHUNK)] = drv_ref[:, sl].astype(BF16)
            dp_ref[:, pl.ds(3 * HW + h * CHUNK, CHUNK)] = dgt_ref[:, sl].astype(BF16)
            dx, dg = norm_bwd(p_ref[:, pl.ds(4 * HW + h * CHUNK, CHUNK)].astype(F32), qg_ref[...],
                              _irope(daq_ref[:, sl].astype(F32), cosf, sinf))
            dp_ref[:, pl.ds(4 * HW + h * CHUNK, CHUNK)] = dx.astype(BF16)
            dqg = dqg + dg
        for h in range(KV):
            sl = pl.ds(h * CHUNK, CHUNK)
            dx, dg = norm_bwd(p_ref[:, pl.ds(5 * HW + h * CHUNK, CHUNK)].astype(F32), kg_ref[...],
                              _irope(dak_ref[:, sl], cosf, sinf))
            dp_ref[:, pl.ds(5 * HW + h * CHUNK, CHUNK)] = dx.astype(BF16)
            dp_ref[:, pl.ds(5 * HW + (KV + h) * CHUNK, CHUNK)] = dav_ref[:, sl].astype(BF16)
            dkg = dkg + dg

        @pl.when(i == 0)
        def _():
            dqg_ref[...] = dqg
            dkg_ref[...] = dkg

        @pl.when(i > 0)
        def _():
            dqg_ref[...] += dqg
            dkg_ref[...] += dkg

    tab = pl.BlockSpec((TM, CHUNK), lambda i: (i % TPE, 0))
    vec = pl.BlockSpec((1, CHUNK), lambda i: (0, 0))
    hw = pl.BlockSpec((TM, HW), lambda i: (i, 0))
    kvw = pl.BlockSpec((TM, KV * CHUNK), lambda i: (i, 0))
    return pl.pallas_call(
        body, name=name, grid=(T // TM,),
        in_specs=[pl.BlockSpec((TM, cfg.ABW), lambda i: (i, 0)), tab, tab, vec, vec, hw, hw, hw, hw, hw, kvw, kvw],
        out_specs=[pl.BlockSpec((TM, cfg.ABW), lambda i: (i, 0)), vec, vec],
        out_shape=[jax.ShapeDtypeStruct((T, cfg.ABW), BF16), jax.ShapeDtypeStruct((1, CHUNK), F32),
                   jax.ShapeDtypeStruct((1, CHUNK), F32)],
        compiler_params=_params(("arbitrary",)),
    )(p, cosf, sinf, qg, kg, d_rq, d_rk, d_rv, d_gate, d_aq, d_ak, d_av)


def _ret_consts(direction, lg):
    C = CHUNK
    ii = lax.broadcasted_iota(jnp.int32, (C, C), 0)
    jj = lax.broadcasted_iota(jnp.int32, (C, C), 1)
    col = lax.broadcasted_iota(jnp.int32, (C, 1), 0).astype(F32)
    if direction == 0:
        mask, er, ek, eq = ii >= jj, (ii - jj).astype(F32), (C - 1.0) - col, col + 1.0
    else:
        mask, er, ek, eq = jj >= ii, (jj - ii).astype(F32), col, C - col
    er = jnp.where(mask, er, 0.0)
    dm = jnp.where(mask, jnp.exp(er * lg), 0.0)
    return dm, er, jnp.exp(ek * lg), ek, jnp.exp(eq * lg), eq, jnp.exp(C * lg)


def _ret_order(cfg, direction):
    n_all, n_ctx = cfg.S // CHUNK, cfg.SC // CHUNK
    if direction == 0:
        return list(range(n_all))
    return list(range(n_ctx - 1, -1, -1)) + list(range(n_all - 1, n_ctx - 1, -1))


def _carry_begin(carry, c_src, c_dst, sems, step):
    @pl.when(step == 0)
    def _():
        for cp in _carry_copies(carry, c_src, c_dst, *sems):
            cp.start()


def _carry_end(carry, c_src, c_dst, sems, step, n_steps):
    @pl.when(step == n_steps - 1)
    def _():
        for cp in _carry_copies(carry, c_src, c_dst, *sems):
            cp.wait()


def _carry_scratch(nc):
    return [pltpu.SemaphoreType.DMA((3 * nc,)), pltpu.SemaphoreType.DMA((3 * nc,)),
            pltpu.SemaphoreType.DMA((nc,))] if nc else []


def _head_norm_gate(o, g):
    mu = jnp.mean(o, axis=-1, keepdims=True)
    var = jnp.mean(jnp.square(o - mu), axis=-1, keepdims=True)
    rstd = lax.rsqrt(var + EPS)
    y = (o - mu) * rstd
    sg = jax.nn.sigmoid(g)
    return y, rstd, sg


RET_UNROLL = 2


def _retention_fwd(cfg, name, rq, rk, p, lgb, carry=None):
    B, H, S, T = cfg.B, cfg.H, cfg.S, cfg.T
    n_all = S // CHUNK
    nc = len(carry.srcs) if carry is not None else 0

    def body(*refs):
        q_ref, k_ref, v_ref, g_ref, lg_ref = refs[:5]
        c_src = refs[5:5 + nc]
        o_ref, ret_ref, st_ref = refs[5 + nc:8 + nc]
        c_dst = refs[8 + nc:8 + 2 * nc]
        sems = refs[8 + 2 * nc:8 + 2 * nc + 3] if nc else ()
        kv_ref = refs[-1]
        step = pl.program_id(0) * H + pl.program_id(1)
        if nc:
            _carry_begin(carry, c_src, c_dst, sems, step)

        def rows(n):
            return pl.ds(pl.multiple_of(n * CHUNK, CHUNK), CHUNK)

        (dm0, _, kd0, _, qd0, _, cd0), (dm1, _, kd1, _, qd1, _, cd1) = (
            _ret_consts(d, lg_ref[d, 0:1, 0:1]) for d in (0, 1))
        dm_both = dm0 + dm1

        def kv_step(n, c):
            k = k_ref[rows(n), :]
            v = v_ref[rows(n), :].astype(BF16)
            kv_ref[0, n] = _dot_tn((k * kd0).astype(BF16), v)
            kv_ref[1, n] = _dot_tn((k * kd1).astype(BF16), v)
            return c

        lax.fori_loop(0, n_all, kv_step, 0, unroll=RET_UNROLL)
        for direction, cd in ((0, cd0), (1, cd1)):
            st = jnp.zeros((CHUNK, CHUNK), F32)
            for t, n in enumerate(_ret_order(cfg, direction)):
                st_ref[direction, n] = st
                if t + 1 < n_all:
                    st = cd * st + kv_ref[direction, n]

        def out_step(n, c):
            q = q_ref[rows(n), :].astype(BF16)
            v = v_ref[rows(n), :].astype(BF16)
            s = _dot_nt(q, k_ref[rows(n), :].astype(BF16)) * dm_both
            states = jnp.concatenate([st_ref[0, n].astype(BF16), st_ref[1, n].astype(BF16)], axis=1)
            cross = _dot(q, states)
            o = _dot(s.astype(BF16), v) + cross[:, :CHUNK] * qd0 + cross[:, CHUNK:] * qd1
            o_ref[rows(n), :] = o
            g = g_ref[rows(n), :].astype(F32)
            y, _, sg = _head_norm_gate(o, g)
            ret_ref[rows(n), :] = (y * (g * sg)).astype(BF16)
            return c

        lax.fori_loop(0, n_all, out_step, 0, unroll=RET_UNROLL)
        if nc:
            _carry_end(carry, c_src, c_dst, sems, step, B * H)

    HW = H * CHUNK
    blk = lambda off: pl.BlockSpec((S, CHUNK), lambda b, h: (b, off + h))
    st_spec = pl.BlockSpec((None, None, 2, n_all, CHUNK, CHUNK), lambda b, h: (b, h, 0, 0, 0, 0))
    res = pl.pallas_call(
        body, name=name, grid=(B, H),
        in_specs=[blk(0), blk(0), blk(2 * H), blk(3 * H),
                  pl.BlockSpec((None, 2, 8, LANES), lambda b, h: (h, 0, 0, 0))] + [ANY] * nc,
        out_specs=[blk(0), blk(0), st_spec] + [ANY] * nc,
        out_shape=[jax.ShapeDtypeStruct((T, HW), F32), jax.ShapeDtypeStruct((T, 2 * HW), BF16),
                   jax.ShapeDtypeStruct((B, H, 2, n_all, CHUNK, CHUNK), F32)] + (_carry_out_shapes(carry) if nc else []),
        scratch_shapes=_carry_scratch(nc) + [pltpu.VMEM((2, n_all, CHUNK, CHUNK), F32)],
        compiler_params=_params(("arbitrary", "arbitrary") if nc else ("parallel", "parallel")),
    )(rq, rk, p, p, lgb, *(carry.srcs if nc else ()))
    return res[:3], res[3:]


def _retention_bwd(cfg, name, rq, rk, p, o_sum, states, dcat, lgb, carry=None):
    B, H, S, T = cfg.B, cfg.H, cfg.S, cfg.T
    n_all = S // CHUNK
    C = CHUNK
    nc = len(carry.srcs) if carry is not None else 0

    def body(*refs):
        q_ref, k_ref, v_ref, g_ref, o_ref, st_ref, dr_ref, lg_ref = refs[:8]
        c_src = refs[8:8 + nc]
        dq_ref, dk_ref, dv_ref, dg_ref, dlg_ref = refs[8 + nc:13 + nc]
        c_dst = refs[13 + nc:13 + 2 * nc]
        sems = refs[13 + 2 * nc:13 + 2 * nc + 3] if nc else ()
        do_ref, gq_ref, ds_ref, acc_ref = refs[-4:]
        step = pl.program_id(0) * H + pl.program_id(1)
        if nc:
            _carry_begin(carry, c_src, c_dst, sems, step)

        def rows(n):
            return pl.ds(pl.multiple_of(n * C, C), C)

        def gate_step(n, c):
            g = g_ref[rows(n), :].astype(F32)
            dr = dr_ref[rows(n), :].astype(F32)
            y, rstd, sg = _head_norm_gate(o_ref[rows(n), :], g)
            dy = dr * (g * sg)
            dg_ref[rows(n), :] = (dr * y * (sg * (1.0 + g * (1.0 - sg)))).astype(BF16)
            do_ref[rows(n), :] = rstd * (dy - jnp.mean(dy, axis=-1, keepdims=True)
                                         - y * jnp.mean(dy * y, axis=-1, keepdims=True))
            return c

        lax.fori_loop(0, n_all, gate_step, 0, unroll=RET_UNROLL)

        (dm0, er0, kd0, ek0, qd0, eq0, cd0), (dm1, er1, kd1, ek1, qd1, eq1, cd1) = (
            _ret_consts(d, lg_ref[d, 0:1, 0:1]) for d in (0, 1))
        dm_both = dm0 + dm1
        wdm0, wdm1 = dm0 * er0, dm1 * er1

        def side(a, b):
            return jnp.concatenate([a.astype(BF16), b.astype(BF16)], axis=1)

        def gq_step(n, c):
            do = do_ref[rows(n), :]
            gq = _dot_tn(q_ref[rows(n), :].astype(BF16), side(do * qd0, do * qd1))
            gq_ref[0, n] = gq[:, :C]
            gq_ref[1, n] = gq[:, C:]
            return c

        lax.fori_loop(0, n_all, gq_step, 0, unroll=RET_UNROLL)
        for direction, cd in ((0, cd0), (1, cd1)):
            order = _ret_order(cfg, direction)
            ds = jnp.zeros((C, C), F32)
            for t in reversed(range(n_all)):
                ds_ref[direction, order[t]] = ds
                if t > 0:
                    ds = cd * ds + gq_ref[direction, order[t]]
        acc_ref[...] = jnp.zeros_like(acc_ref)

        def chunk_step(n, c):
            q = q_ref[rows(n), :].astype(BF16)
            kf = k_ref[rows(n), :]
            k = kf.astype(BF16)
            v = v_ref[rows(n), :].astype(BF16)
            do = do_ref[rows(n), :]
            dob = do.astype(BF16)
            sp0, sp1 = st_ref[0, n], st_ref[1, n]
            ds0, ds1 = ds_ref[0, n], ds_ref[1, n]
            states = side(sp0, sp1)
            dstates = jnp.concatenate([ds0.astype(BF16), ds1.astype(BF16)], axis=0)
            doq0, doq1 = do * qd0, do * qd1
            doq = side(doq0, doq1)
            s_raw = _dot_nt(q, k)
            dpm = _dot_nt(dob, v)
            dsr = (dpm * dm_both).astype(BF16)
            dks = _dot_nt(v, dstates)
            dks0, dks1 = dks[:, :C] * kd0, dks[:, C:] * kd1
            qs = _dot(q, states)
            dq_ref[rows(n), :] = (_dot(dsr, k) + _dot_nt(doq, states)).astype(BF16)
            dk_ref[rows(n), :] = (_dot_tn(dsr, q) + dks0 + dks1).astype(BF16)
            dv_ref[rows(n), :] = (_dot_tn((s_raw * dm_both).astype(BF16), dob)
                                  + _dot(side(kf * kd0, kf * kd1), dstates)).astype(BF16)
            inner = dpm * s_raw
            acc_ref[0] += (jnp.sum(inner * wdm0, axis=0, keepdims=True)
                           + jnp.sum(eq0 * doq0 * qs[:, :C], axis=0, keepdims=True)
                           + jnp.sum(ek0 * kf * dks0, axis=0, keepdims=True)
                           + (C * cd0) * jnp.sum(ds0 * sp0, axis=0, keepdims=True))
            acc_ref[1] += (jnp.sum(inner * wdm1, axis=0, keepdims=True)
                           + jnp.sum(eq1 * doq1 * qs[:, C:], axis=0, keepdims=True)
                           + jnp.sum(ek1 * kf * dks1, axis=0, keepdims=True)
                           + (C * cd1) * jnp.sum(ds1 * sp1, axis=0, keepdims=True))
            return c

        lax.fori_loop(0, n_all, chunk_step, 0, unroll=RET_UNROLL)
        for direction in (0, 1):
            dlg_ref[direction] = jnp.broadcast_to(jnp.sum(acc_ref[direction], axis=1, keepdims=True), (8, LANES))
        if nc:
            _carry_end(carry, c_src, c_dst, sems, step, B * H)

    HW = H * CHUNK
    blk = lambda off: pl.BlockSpec((S, CHUNK), lambda b, h: (b, off + h))
    st_spec = pl.BlockSpec((None, None, 2, n_all, C, C), lambda b, h: (b, h, 0, 0, 0, 0))
    res = pl.pallas_call(
        body, name=name, grid=(B, H),
        in_specs=[blk(0), blk(0), blk(2 * H), blk(3 * H), blk(0), st_spec, blk(0),
                  pl.BlockSpec((None, 2, 8, LANES), lambda b, h: (h, 0, 0, 0))] + [ANY] * nc,
        out_specs=[blk(0)] * 4 + [pl.BlockSpec((None, None, 2, 8, LANES), lambda b, h: (b, h, 0, 0, 0))] + [ANY] * nc,
        out_shape=[jax.ShapeDtypeStruct((T, HW), BF16)] * 4 + [jax.ShapeDtypeStruct((B, H, 2, 8, LANES), F32)]
        + (_carry_out_shapes(carry) if nc else []),
        scratch_shapes=_carry_scratch(nc) + [pltpu.VMEM((S, CHUNK), F32), pltpu.VMEM((2, n_all, C, C), F32),
                                             pltpu.VMEM((2, n_all, C, C), F32), pltpu.VMEM((2, 1, C), F32)],
        compiler_params=_params(("arbitrary", "arbitrary") if nc else ("parallel", "parallel")),
    )(rq, rk, p, p, o_sum, states, dcat, lgb, *(carry.srcs if nc else ()))
    return res[:5], res[5:]


ATT_SCALE = CHUNK ** -0.5


def _attn_scores(cfg, q, k, t):
    kcol = lax.broadcasted_iota(jnp.int32, (1, cfg.S), 1)
    bias = jnp.where(jnp.logical_or(t > 0, kcol < cfg.SC), 0.0, -1e30)
    return _dot_nt(q, k) + bias


def _attention_fwd(cfg, name, aq, ak, p, cat, carry=None):
    B, H, KV, S, T, TM, TPE = cfg.B, cfg.H, cfg.KV, cfg.S, cfg.T, cfg.TM, cfg.TPE
    G = H // KV
    v_off = (5 * H + KV)
    nc = len(carry.srcs) if carry is not None else 0

    def body(*refs):
        q_ref, k_ref, v_ref = refs[:3]
        o_ref, lse_ref = refs[4 + nc:6 + nc]
        c_src, c_dst, sems = refs[4:4 + nc], refs[6 + nc:6 + 2 * nc], refs[6 + 2 * nc:]
        step = (pl.program_id(0) * H + pl.program_id(1)) * TPE + pl.program_id(2)
        if nc:
            _carry_begin(carry, c_src, c_dst, sems, step)
        s = _attn_scores(cfg, q_ref[...], k_ref[...], pl.program_id(2))
        m = jnp.max(s, axis=-1, keepdims=True)
        e = jnp.exp(s - m)
        total = jnp.sum(e, axis=-1, keepdims=True)
        o_ref[...] = (_dot(e.astype(BF16), v_ref[...].astype(BF16)) * (1.0 / total)).astype(BF16)
        lse_ref[...] = m + jnp.log(total)
        if nc:
            _carry_end(carry, c_src, c_dst, sems, step, B * H * TPE)

    res = pl.pallas_call(
        body, name=name, grid=(B, H, TPE),
        in_specs=[pl.BlockSpec((TM, CHUNK), lambda b, h, t: (b * TPE + t, h)),
                  pl.BlockSpec((S, CHUNK), lambda b, h, t: (b, h // G)),
                  pl.BlockSpec((S, CHUNK), lambda b, h, t: (b, v_off + h // G)), ANY] + [ANY] * nc,
        out_specs=[pl.BlockSpec((TM, CHUNK), lambda b, h, t: (b * TPE + t, H + h)),
                   pl.BlockSpec((None, TM, 1), lambda b, h, t: (h, b * TPE + t, 0))] + [ANY] * nc,
        out_shape=[jax.ShapeDtypeStruct(cat.shape, cat.dtype), jax.ShapeDtypeStruct((H, T, 1), F32)]
        + (_carry_out_shapes(carry) if nc else []),
        input_output_aliases={3: 0},
        scratch_shapes=_carry_scratch(nc),
        compiler_params=_params(("arbitrary",) * 3 if nc else ("parallel",) * 3),
    )(aq, ak, p, cat, *(carry.srcs if nc else ()))
    return res[:2], res[2:]


def _attention_bwd(cfg, name, aq, ak, p, cat, lse, dcat, carry=None):
    B, H, KV, S, T, TM, TPE = cfg.B, cfg.H, cfg.KV, cfg.S, cfg.T, cfg.TM, cfg.TPE
    G = H // KV
    v_off = (5 * H + KV)
    nc = len(carry.srcs) if carry is not None else 0

    def body(*refs):
        q_ref, k_ref, v_ref, o_ref, lse_ref, do_ref = refs[:6]
        dq_ref, dk_ref, dv_ref = refs[6 + nc:9 + nc]
        c_src, c_dst, sems = refs[6:6 + nc], refs[9 + nc:9 + 2 * nc], refs[9 + 2 * nc:]
        g, t = pl.program_id(2), pl.program_id(3)
        step = ((pl.program_id(0) * KV + pl.program_id(1)) * G + g) * TPE + t
        if nc:
            _carry_begin(carry, c_src, c_dst, sems, step)
        q, k = q_ref[...], k_ref[...]
        v = v_ref[...].astype(BF16)
        do = do_ref[...]
        dob = do.astype(BF16)
        pr = jnp.exp(_attn_scores(cfg, q, k, t) - lse_ref[...])
        delta = jnp.sum(do.astype(F32) * o_ref[...].astype(F32), axis=-1, keepdims=True)
        ds = (pr * (_dot_nt(dob, v) - delta)).astype(BF16)
        dq_ref[...] = (_dot(ds, k) * ATT_SCALE).astype(BF16)
        dk = _dot_tn(ds, q)
        dv = _dot_tn(pr.astype(BF16), dob)
        first = jnp.logical_and(g == 0, t == 0)

        @pl.when(first)
        def _():
            dk_ref[...] = dk
            dv_ref[...] = dv

        @pl.when(jnp.logical_not(first))
        def _():
            dk_ref[...] += dk
            dv_ref[...] += dv

        if nc:
            _carry_end(carry, c_src, c_dst, sems, step, B * KV * G * TPE)

    qspec = pl.BlockSpec((TM, CHUNK), lambda b, kv, g, t: (b * TPE + t, kv * G + g))
    kvspec = pl.BlockSpec((S, CHUNK), lambda b, kv, g, t: (b, kv))
    right = pl.BlockSpec((TM, CHUNK), lambda b, kv, g, t: (b * TPE + t, H + kv * G + g))
    res = pl.pallas_call(
        body, name=name, grid=(B, KV, G, TPE),
        in_specs=[qspec, kvspec, pl.BlockSpec((S, CHUNK), lambda b, kv, g, t: (b, v_off + kv)), right,
                  pl.BlockSpec((None, TM, 1), lambda b, kv, g, t: (kv * G + g, b * TPE + t, 0)), right] + [ANY] * nc,
        out_specs=[qspec, kvspec, kvspec] + [ANY] * nc,
        out_shape=[jax.ShapeDtypeStruct((T, H * CHUNK), BF16), jax.ShapeDtypeStruct((T, KV * CHUNK), F32),
                   jax.ShapeDtypeStruct((T, KV * CHUNK), F32)] + (_carry_out_shapes(carry) if nc else []),
        scratch_shapes=_carry_scratch(nc),
        compiler_params=_params(("arbitrary",) * 4 if nc else ("parallel", "parallel", "arbitrary", "arbitrary")),
    )(aq, ak, p, cat, lse, dcat, *(carry.srcs if nc else ()))
    return res[:3], res[3:]


_GELU_C = math.sqrt(2.0 / math.pi)


def _gelu(x):
    return 0.5 * x * (1.0 + jnp.tanh(_GELU_C * (x + 0.044715 * x * x * x)))


def _gelu_and_grad(x):
    x2 = x * x
    th = jnp.tanh(_GELU_C * (x + 0.044715 * x * x2))
    half = 0.5 * (1.0 + th)
    return x * half, half + 0.5 * x * (1.0 - th * th) * _GELU_C * (1.0 + 3.0 * 0.044715 * x2)


def _cm_fwd(cfg, name, a, vg, ws, bs):
    T, TM, W, NG = cfg.T, cfg.TM, cfg.CMW, cfg.CMG

    def body(a_ref, vg_ref, ws_ref, bs_ref, m_ref):
        v = _gelu(a_ref[:, pl.ds(W, W)].astype(F32))
        vn = (v * lax.rsqrt(jnp.mean(v * v, axis=-1, keepdims=True) + EPS) * vg_ref[...]).astype(BF16)
        for c in range(TM // CHUNK):
            for g in range(NG):
                rows, cols = slice(c * CHUNK, (c + 1) * CHUNK), slice(g * CHUNK, (g + 1) * CHUNK)
                sv = _dot(ws_ref[g].astype(BF16), vn[rows, cols]) + bs_ref[g]
                u = _gelu(a_ref[pl.ds(c * CHUNK, CHUNK), pl.ds(g * CHUNK, CHUNK)].astype(F32))
                m_ref[pl.ds(c * CHUNK, CHUNK), pl.ds(g * CHUNK, CHUNK)] = (u * sv).astype(BF16)

    return pl.pallas_call(
        body, name=name, grid=(T // TM,),
        in_specs=[pl.BlockSpec((TM, 2 * W), lambda i: (i, 0)), pl.BlockSpec((1, W), lambda i: (0, 0)),
                  pl.BlockSpec((NG, CHUNK, CHUNK), lambda i: (0, 0, 0)),
                  pl.BlockSpec((NG, CHUNK, 1), lambda i: (0, 0, 0))],
        out_specs=pl.BlockSpec((TM, W), lambda i: (i, 0)),
        out_shape=jax.ShapeDtypeStruct((T, W), BF16),
        compiler_params=_params(("parallel",)),
    )(a, vg, ws, bs)


def _cm_bwd(cfg, name, a, vg, ws, bs, dm):
    T, TM, W, NG = cfg.T, cfg.TM, cfg.CMW, cfg.CMG

    def body(a_ref, vg_ref, ws_ref, bs_ref, dm_ref, da_ref, dws_ref, dbs_ref, dvg_ref, dvn_ref):
        i = pl.program_id(0)

        @pl.when(i == 0)
        def _():
            dws_ref[...] = jnp.zeros_like(dws_ref)
            dbs_ref[...] = jnp.zeros_like(dbs_ref)
            dvg_ref[...] = jnp.zeros_like(dvg_ref)

        v, v_grad = _gelu_and_grad(a_ref[:, pl.ds(W, W)].astype(F32))
        rstd = lax.rsqrt(jnp.mean(v * v, axis=-1, keepdims=True) + EPS)
        xhat = v * rstd
        vg = vg_ref[...]
        vn = (xhat * vg).astype(BF16)
        for c in range(TM // CHUNK):
            for g in range(NG):
                rows, cols = slice(c * CHUNK, (c + 1) * CHUNK), slice(g * CHUNK, (g + 1) * CHUNK)
                rs, cs = pl.ds(c * CHUNK, CHUNK), pl.ds(g * CHUNK, CHUNK)
                wsb = ws_ref[g].astype(BF16)
                blk = vn[rows, cols]
                sv = _dot(wsb, blk) + bs_ref[g]
                u, u_grad = _gelu_and_grad(a_ref[rs, cs].astype(F32))
                dmb = dm_ref[rs, cs].astype(F32)
                da_ref[rs, cs] = (dmb * sv * u_grad).astype(BF16)
                dsv = dmb * u
                dsvb = dsv.astype(BF16)
                dbs_ref[g] += jnp.sum(dsv, axis=1, keepdims=True)
                dws_ref[g] += _dot_nt(dsvb, blk)
                dvn_ref[rs, cs] = _dot_tn(wsb, dsvb)
        dvn = dvn_ref[...]
        dvg_ref[...] += jnp.sum(dvn * xhat, axis=0, keepdims=True)
        dxh = dvn * vg
        dv = rstd * (dxh - xhat * jnp.mean(dxh * xhat, axis=-1, keepdims=True))
        da_ref[:, pl.ds(W, W)] = (dv * v_grad).astype(BF16)

    return pl.pallas_call(
        body, name=name, grid=(T // TM,),
        in_specs=[pl.BlockSpec((TM, 2 * W), lambda i: (i, 0)), pl.BlockSpec((1, W), lambda i: (0, 0)),
                  pl.BlockSpec((NG, CHUNK, CHUNK), lambda i: (0, 0, 0)),
                  pl.BlockSpec((NG, CHUNK, 1), lambda i: (0, 0, 0)), pl.BlockSpec((TM, W), lambda i: (i, 0))],
        out_specs=[pl.BlockSpec((TM, 2 * W), lambda i: (i, 0)), pl.BlockSpec((NG, CHUNK, CHUNK), lambda i: (0, 0, 0)),
                   pl.BlockSpec((NG, CHUNK, 1), lambda i: (0, 0, 0)), pl.BlockSpec((1, W), lambda i: (0, 0))],
        out_shape=[jax.ShapeDtypeStruct((T, 2 * W), BF16), jax.ShapeDtypeStruct((NG, CHUNK, CHUNK), F32),
                   jax.ShapeDtypeStruct((NG, CHUNK, 1), F32), jax.ShapeDtypeStruct((1, W), F32)],
        scratch_shapes=[pltpu.VMEM((TM, W), F32)],
        compiler_params=_params(("arbitrary",)),
    )(a, vg, ws, bs, dm)


def _layer_weights(l):
    mixer = ("ab_w_in", "ab_w_out") if l % 2 == 0 else ("cm_w_in", "cm_w_out")
    return [(mixer[0], l // 2), (mixer[1], l // 2), ("ff_w1", l), ("ff_w2", l)]


def _local_step(cfg, xcat, tgt, mods, shards, w):
    D, TM, H = cfg.D, cfg.TM, cfg.H
    cosf, sinf = _rope_tables(cfg)
    full, big, recv = {}, {}, {}

    def gather_of(keys):
        return _Carry("gather", tuple(shards[n][i] for n, i in keys), tuple(BIG[n] for n, _ in keys))

    def exchange_of(keys):
        return _Carry("exchange", tuple(big[k] for k in keys), tuple(BIG[n] for n, _ in keys))

    def mm(pending, name, a, b, **kw):
        if not pending:
            return _mm(name, a, b, **kw)
        key, carry, sink = pending.pop(0)
        out, (got,) = _mm(name, a, b, carry=carry, **kw)
        sink[key] = got
        return out

    def with_carry(call, keys, make, sink):
        out, got = call(carry=make(keys) if keys else None)
        sink.update(zip(keys, got))
        return out

    keys0 = _layer_weights(0)
    full[keys0[0]], = _comm_call("gather_weights_0", gather_of(keys0[:1]))
    TG = 3 * TM if cfg.TPE % 3 == 0 else TM
    tiles_per_ex = cfg.S // TG
    gate_spec = pl.BlockSpec((None, 2, 6, D), lambda i, j, k: (i // tiles_per_ex, 0, 0, 0))

    def resid_epi(igate, nxt):
        def epi(acc, row_tile, x_ref, mod_ref, *nxt_refs):
            row = lax.broadcasted_iota(jnp.int32, (TG, 1), 0)
            is_ctx = jnp.logical_and(row_tile % tiles_per_ex == 0, row < cfg.SC)

            def pick(ref, idx):
                return jnp.where(is_ctx, ref[0, pl.ds(idx, 1), :], ref[1, pl.ds(idx, 1), :])

            x = x_ref[...] + pick(mod_ref, igate) * acc
            if nxt is None:
                return x, acc
            gain_ref, modn_ref = nxt_refs
            n = x * lax.rsqrt(jnp.mean(x * x, axis=-1, keepdims=True) + EPS) * gain_ref[...]
            return x, acc, n * (1.0 + pick(modn_ref, nxt[3])) + pick(modn_ref, nxt[2])
        return epi

    def gated_out(pending, name, a, key, x, mod, igate, nxt=None):
        extras = [(x, pl.BlockSpec((TG, D), lambda i, j, k: (i, j))), (mod, gate_spec)]
        if nxt is not None:
            extras += [(nxt[0], pl.BlockSpec((1, D), lambda i, j, k: (0, 0))), (nxt[1], gate_spec)]
        return mm(pending, name, a, full[key], mode="nn", tm=TG, tn=D, outs=[F32, BF16] + [BF16] * (nxt is not None),
                  epi=resid_epi(igate, nxt), extras=extras)

    saved = []
    x = xcat
    h = _norm_mod_fwd(cfg, "norm1_fwd_0", x, w["norm1_g"][0][None], mods[0], 0, 1)
    for l in range(N_LAYERS):
        li = l // 2
        mod = mods[l]
        k_in, k_out, k_ff1, k_ff2 = _layer_weights(l)
        pend = [(k, gather_of([k]), full) for k in _layer_weights(l + 1)] if l + 1 < N_LAYERS else []
        norm2 = (w["norm2_g"][l][None], mod, 3, 4)
        s = {"x0": x, "h": h}
        if l % 2 == 0:
            lgb = jnp.broadcast_to(jax.nn.log_sigmoid(w["ret_decay"][li]).T[:, :, None, None], (H, 2, 8, LANES))
            qg, kg = w["att_q_norm_g"][li][None], w["att_k_norm_g"][li][None]
            s["p"] = mm(pend, f"ab_in_{l}", s["h"], full[k_in], mode="nn", outs=[BF16], tn=768)
            s["rq"], s["rk"], s["aq"], s["ak"] = _prep_fwd(cfg, f"prep_fwd_{l}", s["p"], cosf, sinf, qg, kg)
            s["o"], ret, s["st"] = with_carry(
                functools.partial(_retention_fwd, cfg, f"ret_fwd_{l}", s["rq"], s["rk"], s["p"], lgb),
                keys0[1:3] if l == 0 else [], gather_of, full)
            s["cat"], s["lse"] = with_carry(
                functools.partial(_attention_fwd, cfg, f"att_fwd_{l}", s["aq"], s["ak"], s["p"], ret),
                keys0[3:] if l == 0 else [], gather_of, full)
            s["lgb"], s["qg"], s["kg"] = lgb, qg, kg
            x, s["y1"], s["h2"] = gated_out(pend, f"ab_out_{l}", s["cat"], k_out, x, mod, 2, norm2)
        else:
            s["a"] = mm(pend, f"cm_in_{l}", s["h"], full[k_in], mode="nn", outs=[BF16])
            s["vg"], s["ws"], s["bs"] = w["cm_v_norm_g"][li][None], w["cm_w_s"][li], w["cm_b_s"][li][:, :, None]
            s["m"] = _cm_fwd(cfg, f"cm_fwd_{l}", s["a"], s["vg"], s["ws"], s["bs"])
            x, s["y1"], s["h2"] = gated_out(pend, f"cm_out_{l}", s["m"], k_out, x, mod, 2, norm2)
        s["x1"] = x
        s["r"] = mm(pend, f"ff1_{l}", s["h2"], full[k_ff1], mode="nn", outs=[BF16],
                    epi=lambda acc, row_tile: (jnp.square(jnp.maximum(acc, 0.0)),))
        if l + 1 < N_LAYERS:
            x, s["y2"], h = gated_out(pend, f"ff2_{l}", s["r"], k_ff2, x, mod, 5,
                                      (w["norm1_g"][l + 1][None], mods[l + 1], 0, 1))
        else:
            x, s["y2"] = gated_out(pend, f"ff2_{l}", s["r"], k_ff2, x, mod, 5)
        saved.append(s)

    loss, dx = _loss_grad(cfg, x, tgt)

    small = {k: [None] * n for k, n in (("norm1_g", 4), ("norm2_g", 4), ("ret_lg", 2), ("att_q_norm_g", 2),
                                        ("att_k_norm_g", 2), ("cm_v_norm_g", 2), ("cm_w_s", 2), ("cm_b_s", 2))}
    dmods = [None] * N_LAYERS

    for l in reversed(range(N_LAYERS)):
        li = l // 2
        s, mod = saved[l], mods[l]
        k_in, k_out, k_ff1, k_ff2 = _layer_weights(l)
        above = _layer_weights(l + 1) if l + 1 < N_LAYERS else [None] * 4
        if l == N_LAYERS - 1:
            dy2, dg2 = _gate_bwd(cfg, f"gate2_bwd_{l}", dx, s["y2"], mod, 5)
        da2, big[k_ff2] = with_carry(functools.partial(_ff_bwd, f"ff2_bwd_{l}", dy2, s["r"], full[k_ff2], "w2"),
                                     [above[3], above[1]] if above[0] else [], exchange_of, recv)
        dh2, big[k_ff1] = with_carry(functools.partial(_ff_bwd, f"ff1_bwd_{l}", s["h2"], da2, full[k_ff1], "w1"),
                                     [above[2], above[0]] if above[0] else [], exchange_of, recv)
        dx, dm2, small["norm2_g"][l], do, dg1 = _norm_mod_bwd(
            cfg, f"norm2_bwd_{l}", s["x1"], w["norm2_g"][l][None], mod, 3, 4, dh2, dx, gate=(s["y1"], mod, 2))
        if l % 2 == 0:
            big[k_out] = _mm(f"ab_out_dw_{l}", s["cat"], do, mode="tn", outs=[BF16])
            dcat = _mm(f"ab_out_dx_{l}", do, full[k_out], mode="nt", outs=[BF16])
            d_rq, d_rk, d_rv, d_gt, dlg = with_carry(
                functools.partial(_retention_bwd, cfg, f"ret_bwd_{l}", s["rq"], s["rk"], s["p"], s["o"], s["st"], dcat,
                                  s["lgb"]), [k_ff2, k_ff1] if l == 0 else [], exchange_of, recv)
            d_aq, d_ak, d_av = with_carry(
                functools.partial(_attention_bwd, cfg, f"att_bwd_{l}", s["aq"], s["ak"], s["p"], s["cat"], s["lse"], dcat),
                [k_out] if l == 0 else [], exchange_of, recv)
            dp, dqg, dkg = _prep_bwd(cfg, f"prep_bwd_{l}", s["p"], cosf, sinf, s["qg"], s["kg"],
                                     d_rq, d_rk, d_rv, d_gt, d_aq, d_ak, d_av)
            small["ret_lg"][li] = jnp.sum(dlg[:, :, :, 0, 0], axis=0).T
            small["att_q_norm_g"][li], small["att_k_norm_g"][li] = dqg[0], dkg[0]
            big[k_in] = _mm(f"ab_in_dw_{l}", s["h"], dp, mode="tn", outs=[BF16])
            last = [(k_in, exchange_of([k_in]), recv)] if l == 0 else []
            dh = mm(last, f"ab_in_dx_{l}", dp, full[k_in], mode="nt", outs=[BF16], tk=768)
        else:
            big[k_out] = _mm(f"cm_out_dw_{l}", s["m"], do, mode="tn", outs=[BF16])
            dm = _mm(f"cm_out_dx_{l}", do, full[k_out], mode="nt", outs=[BF16])
            da, dws, dbs, dvg = _cm_bwd(cfg, f"cm_bwd_{l}", s["a"], s["vg"], s["ws"], s["bs"], dm)
            small["cm_w_s"][li], small["cm_b_s"][li], small["cm_v_norm_g"][li] = dws, dbs[:, :, 0], dvg[0]
            big[k_in] = _mm(f"cm_in_dw_{l}", s["h"], da, mode="tn", outs=[BF16])
            dh = _mm(f"cm_in_dx_{l}", da, full[k_in], mode="nt", outs=[BF16])
        below = (saved[l - 1]["y2"], mods[l - 1], 5) if l > 0 else None
        dx, dm1, small["norm1_g"][l], *rest = _norm_mod_bwd(
            cfg, f"norm1_bwd_{l}", s["x0"], w["norm1_g"][l][None], mod, 0, 1, dh, dx, gate=below, lat_only=l == 0)
        dmods[l] = jnp.concatenate([dm1, dg1, dm2, dg2], axis=2)
        if l > 0:
            dy2, dg2 = rest
    return loss, dx, recv, small, dmods


N_DEV = 8
N_CHIP = 4
MESH = pl.DeviceIdType.MESH
ANY = pl.BlockSpec(memory_space=pl.ANY)
BIG = {"ab_w_in": 1, "ab_w_out": 0, "cm_w_in": 1, "cm_w_out": 0, "ff_w1": 1, "ff_w2": 0}


class _Carry(NamedTuple):
    kind: str
    srcs: tuple
    axes: tuple


def _place():
    x, y, c = lax.axis_index("x"), lax.axis_index("y"), lax.axis_index("c")
    return x, y, c, [(1 - x, y), (x, 1 - y), (1 - x, 1 - y)]


def _shard_of(ref, axis, s, width):
    start = pl.multiple_of(s * width, LANES)
    if axis == 0:
        return ref.at[pl.ds(start, width), :]
    return ref.at[:, pl.ds(start, width)]


def _carry_out_shapes(carry):
    shapes = []
    for src, axis in zip(carry.srcs, carry.axes):
        shape = list(src.shape)
        if carry.kind == "gather":
            shape[axis] *= N_CHIP
        else:
            shape[axis] //= N_CHIP
            shape = [N_CHIP] + shape
        shapes.append(jax.ShapeDtypeStruct(tuple(shape), src.dtype))
    return shapes


def _carry_copies(carry, srcs, dsts, send_sems, recv_sems, local_sems):
    x, y, c, chips = _place()
    me = 2 * x + y
    copies = []
    for t, axis in enumerate(carry.axes):
        if carry.kind == "gather":
            own = _shard_of(dsts[t], axis, me, srcs[t].shape[axis])
            copies.append(pltpu.make_async_copy(srcs[t], own, local_sems.at[t]))
            parts = [(srcs[t], own)] * 3
        else:
            width = dsts[t].shape[1 + axis]
            copies.append(pltpu.make_async_copy(_shard_of(srcs[t], axis, me, width), dsts[t].at[3], local_sems.at[t]))
            parts = [(_shard_of(srcs[t], axis, 2 * px + py, width), dsts[t].at[j]) for j, (px, py) in enumerate(chips)]
        for j, (px, py) in enumerate(chips):
            copies.append(pltpu.make_async_remote_copy(
                src_ref=parts[j][0], dst_ref=parts[j][1], send_sem=send_sems.at[3 * t + j],
                recv_sem=recv_sems.at[3 * t + j], device_id=(px, py, c), device_id_type=MESH))
    return copies


def _comm_call(name, carry):
    nc = len(carry.srcs)

    def body(*refs):
        copies = _carry_copies(carry, refs[:nc], refs[nc:2 * nc], *refs[2 * nc:])
        for cp in copies:
            cp.start()
        for cp in copies:
            cp.wait()

    return pl.pallas_call(
        body, name=name, out_shape=_carry_out_shapes(carry), in_specs=[ANY] * nc, out_specs=[ANY] * nc,
        scratch_shapes=[pltpu.SemaphoreType.DMA((3 * nc,)), pltpu.SemaphoreType.DMA((3 * nc,)),
                        pltpu.SemaphoreType.DMA((nc,))],
    )(*carry.srcs)


def _allgather8(name, block):
    m_per, n = block.shape

    def body(x_ref, out_ref, send_sems, recv_sems, local_sem):
        x, y, c, chips = _place()
        me, sibling = (x, y, c), (x, y, 1 - c)

        def rows(px, py, pc):
            return out_ref.at[pl.ds((4 * px + 2 * py + pc) * m_per, m_per), :]

        def copy(k, blk, to, src=None):
            return pltpu.make_async_remote_copy(
                src_ref=rows(*blk) if src is None else src, dst_ref=rows(*blk),
                send_sem=send_sems.at[k], recv_sem=recv_sems.at[k], device_id=to, device_id_type=MESH)

        mine = pltpu.make_async_copy(x_ref, rows(*me), local_sem)
        mine.start()
        first = [copy(0, me, sibling, src=x_ref)]
        first += [copy(1 + j, me, (*chip, c), src=x_ref) for j, chip in enumerate(chips)]
        for cp in first:
            cp.start()
        passed = [copy(4 + j, (*chip, c), sibling) for j, chip in enumerate(chips)]
        for j, chip in enumerate(chips):
            copy(1 + j, (*chip, c), me).wait_recv()
            passed[j].start()
        copy(0, sibling, me).wait_recv()
        for j, chip in enumerate(chips):
            copy(4 + j, (*chip, 1 - c), me).wait_recv()
        for cp in first + passed:
            cp.wait_send()
        mine.wait()

    return pl.pallas_call(
        body, name=name, out_shape=jax.ShapeDtypeStruct((N_DEV * m_per, n), block.dtype),
        in_specs=[pl.BlockSpec(memory_space=pltpu.VMEM)], out_specs=pl.BlockSpec(memory_space=pltpu.VMEM),
        scratch_shapes=[pltpu.SemaphoreType.DMA((7,)), pltpu.SemaphoreType.DMA((7,)), pltpu.SemaphoreType.DMA],
        compiler_params=pltpu.CompilerParams(vmem_limit_bytes=VMEM_LIMIT),
    )(block)


def _swap_sibling(parts):
    n_t = len(parts)

    def body(*refs):
        srcs, outs = refs[:n_t], refs[n_t:2 * n_t]
        send_sems, recv_sems = refs[2 * n_t:]
        x, y, c, _ = _place()
        copies = []
        for t in range(n_t):
            cp = pltpu.make_async_remote_copy(
                src_ref=srcs[t], dst_ref=outs[t], send_sem=send_sems.at[t], recv_sem=recv_sems.at[t],
                device_id=(x, y, 1 - c), device_id_type=MESH)
            cp.start()
            copies.append(cp)
        for cp in copies:
            cp.wait()

    return pl.pallas_call(
        body, name="swap_sibling", out_shape=[jax.ShapeDtypeStruct(p.shape, p.dtype) for p in parts],
        in_specs=[ANY] * n_t, out_specs=[ANY] * n_t,
        scratch_shapes=[pltpu.SemaphoreType.DMA((n_t,)), pltpu.SemaphoreType.DMA((n_t,))],
    )(*parts)


def _rows_view(a):
    if a.ndim == 1:
        return a.reshape(1, a.shape[0])
    return a.reshape(-1, a.shape[-1])


def _row_tile(rows, cols, target_elems=1 << 17):
    tr = rows
    while tr % 16 == 0 and tr * cols > target_elems:
        tr //= 2
    return tr


def _sum_leading(name, a):
    n, rows, cols = a.shape
    tr = _row_tile(rows, cols * n, target_elems=1 << 20)

    def body(a_ref, o_ref):
        acc = a_ref[0].astype(F32)
        for i in range(1, n):
            acc = acc + a_ref[i].astype(F32)
        o_ref[...] = acc

    return pl.pallas_call(
        body, name=name, grid=(rows // tr,),
        in_specs=[pl.BlockSpec((n, tr, cols), lambda i: (0, i, 0))],
        out_specs=pl.BlockSpec((tr, cols), lambda i: (i, 0)),
        out_shape=jax.ShapeDtypeStruct((rows, cols), F32),
        compiler_params=_params(("parallel",)),
    )(a)


def _silu_rows(name, x):
    def body(x_ref, o_ref):
        v = x_ref[...]
        o_ref[...] = v * jax.nn.sigmoid(v)

    return pl.pallas_call(body, name=name, out_shape=jax.ShapeDtypeStruct(x.shape, F32))(x)


def _silu_bwd_rows(name, x, dy):
    def body(x_ref, dy_ref, o_ref):
        v = x_ref[...]
        sg = jax.nn.sigmoid(v)
        o_ref[...] = dy_ref[...] * (sg * (1.0 + v * (1.0 - sg)))

    return pl.pallas_call(body, name=name, out_shape=jax.ShapeDtypeStruct(x.shape, F32))(x, dy)


def _adamw(name, w, g_parts, m, v):
    shape = w.shape
    w2, m2, v2 = _rows_view(w), _rows_view(m), _rows_view(v)
    gs = [_rows_view(g) for g in g_parts]
    rows, cols = w2.shape
    tr = _row_tile(rows, cols)
    ng = len(gs)

    def body(*refs):
        w_ref, m_ref, v_ref = refs[0], refs[1], refs[2]
        g_refs = refs[3:3 + ng]
        g_out, d_out, m_out, v_out = refs[3 + ng:]
        g = g_refs[0][...]
        for r in g_refs[1:]:
            g = g + r[...]
        m1 = ADAM_B1 * m_ref[...] + (1.0 - ADAM_B1) * g
        v1 = ADAM_B2 * v_ref[...] + (1.0 - ADAM_B2) * jnp.square(g)
        m_hat = m1 / (1.0 - ADAM_B1 ** ADAM_STEP)
        v_hat = v1 / (1.0 - ADAM_B2 ** ADAM_STEP)
        g_out[...] = g
        d_out[...] = -ADAM_LR * (m_hat / (jnp.sqrt(v_hat) + ADAM_EPS) + ADAM_WD * w_ref[...])
        m_out[...] = m1
        v_out[...] = v1

    spec = pl.BlockSpec((tr, cols), lambda i: (i, 0))
    res = pl.pallas_call(
        body, name=name, grid=(rows // tr,), in_specs=[spec] * (3 + ng), out_specs=[spec] * 4,
        out_shape=[jax.ShapeDtypeStruct((rows, cols), F32)] * 4,
        compiler_params=_params(("parallel",)),
    )(w2, m2, v2, *gs)
    return tuple(r.reshape(shape) for r in res)


MOD_ROWS = 48


def kernel(x, c, ctx, c_ctx, mod_w, mod_b, norm1_g, norm2_g, ab_w_in, ab_w_out, ret_decay, att_q_norm_g, att_k_norm_g, cm_w_in, cm_v_norm_g, cm_w_s, cm_b_s, cm_w_out, ff_w1, ff_w2, loss_target, m_c_ctx, m_mod_w, m_mod_b, m_norm1_g, m_norm2_g, m_ab_w_in, m_ab_w_out, m_ret_decay, m_att_q_norm_g, m_att_k_norm_g, m_cm_w_in, m_cm_v_norm_g, m_cm_w_s, m_cm_b_s, m_cm_w_out, m_ff_w1, m_ff_w2, v_c_ctx, v_mod_w, v_mod_b, v_norm1_g, v_norm2_g, v_ab_w_in, v_ab_w_out, v_ret_decay, v_att_q_norm_g, v_att_k_norm_g, v_cm_w_in, v_cm_v_norm_g, v_cm_w_s, v_cm_b_s, v_cm_w_out, v_ff_w1, v_ff_w2):
    B, SL, D = x.shape
    cfg = Cfg(B=B, SC=ctx.shape[1], SL=SL, D=D, FF=ff_w1.shape[2] * N_CHIP)
    L = N_LAYERS
    n_ex = B * N_DEV
    mcols = mod_w.shape[2]
    weights = dict(c_ctx=c_ctx, mod_w=mod_w, mod_b=mod_b, norm1_g=norm1_g, norm2_g=norm2_g, ab_w_in=ab_w_in,
                   ab_w_out=ab_w_out, ret_decay=ret_decay, att_q_norm_g=att_q_norm_g, att_k_norm_g=att_k_norm_g,
                   cm_w_in=cm_w_in, cm_v_norm_g=cm_v_norm_g, cm_w_s=cm_w_s, cm_b_s=cm_b_s, cm_w_out=cm_w_out,
                   ff_w1=ff_w1, ff_w2=ff_w2)
    m_in = dict(c_ctx=m_c_ctx, mod_w=m_mod_w, mod_b=m_mod_b, norm1_g=m_norm1_g, norm2_g=m_norm2_g, ab_w_in=m_ab_w_in,
                ab_w_out=m_ab_w_out, ret_decay=m_ret_decay, att_q_norm_g=m_att_q_norm_g, att_k_norm_g=m_att_k_norm_g,
                cm_w_in=m_cm_w_in, cm_v_norm_g=m_cm_v_norm_g, cm_w_s=m_cm_w_s, cm_b_s=m_cm_b_s, cm_w_out=m_cm_w_out,
                ff_w1=m_ff_w1, ff_w2=m_ff_w2)
    v_in = dict(c_ctx=v_c_ctx, mod_w=v_mod_w, mod_b=v_mod_b, norm1_g=v_norm1_g, norm2_g=v_norm2_g, ab_w_in=v_ab_w_in,
                ab_w_out=v_ab_w_out, ret_decay=v_ret_decay, att_q_norm_g=v_att_q_norm_g, att_k_norm_g=v_att_k_norm_g,
                cm_w_in=v_cm_w_in, cm_v_norm_g=v_cm_v_norm_g, cm_w_s=v_cm_w_s, cm_b_s=v_cm_b_s, cm_w_out=v_cm_w_out,
                ff_w1=v_ff_w1, ff_w2=v_ff_w2)
    xi, yi, ci = lax.axis_index("x"), lax.axis_index("y"), lax.axis_index("c")
    chip = 2 * xi + yi
    dev = 2 * chip + ci

    shards = {n: [weights[n][i].astype(BF16) for i in range(weights[n].shape[0])] for n in BIG}
    vgw = cm_v_norm_g.shape[1]
    blk = jnp.zeros((8, D), F32).at[:B].set(c).at[B:B + 2, :vgw].set(cm_v_norm_g)
    g0 = _allgather8("gather_c", blk).reshape(N_DEV, 8, D)
    c_all = g0[:, :B].reshape(n_ex, D)
    vg_full = jnp.concatenate([g0[2 * s, B:B + 2, :vgw] for s in range(N_CHIP)], axis=-1)

    pre = jnp.zeros((MOD_ROWS, D), F32).at[:n_ex].set(c_all).at[n_ex].set(c_ctx)
    act = _silu_rows("silu_c", pre)
    mpart = jnp.stack([_mm(f"mod_fwd_{l}", act, mod_w, mode="nn", layer=l, outs=[F32], tn=mcols) for l in range(L)])
    g1 = _allgather8("gather_mod", mpart.reshape(L * MOD_ROWS, mcols)).reshape(N_DEV, L, MOD_ROWS, mcols)
    mod_all = jnp.concatenate([g1[2 * s] for s in range(N_CHIP)], axis=-1) + mod_b[:, None, :]
    mod_lat = lax.dynamic_slice_in_dim(mod_all, dev * B, B, axis=1)
    mod_ctx = jnp.broadcast_to(mod_all[:, n_ex][:, None], mod_lat.shape)
    mods = jnp.stack([mod_ctx, mod_lat], axis=2).reshape(L, B, 2, 6, D)

    w = dict(norm1_g=norm1_g, norm2_g=norm2_g, ret_decay=ret_decay, att_q_norm_g=att_q_norm_g,
             att_k_norm_g=att_k_norm_g, cm_v_norm_g=vg_full, cm_w_s=cm_w_s, cm_b_s=cm_b_s)
    xcat = jnp.concatenate([ctx, x], axis=1).reshape(cfg.T, D)
    loss_local, dx_lat, recv, small, dmods = _local_step(cfg, xcat, loss_target.reshape(B * SL, D), mods, shards, w)
    loss = lax.psum(loss_local, ("x", "y", "c"))
    grad_x = dx_lat.reshape(B, SL, D)

    part = [jnp.stack([_sum_leading(f"sum_{n}_{i}", recv[(n, i)]) for i in range(weights[n].shape[0])]) for n in BIG]
    other = _swap_sibling(part)
    out = {}
    for n, p_mine, p_other in zip(BIG, part, other):
        out[n] = _adamw(f"adamw_{n}", weights[n], [p_mine, p_other], m_in[n], v_in[n])

    dmod = jnp.stack(dmods).reshape(L, B, 2, 6 * D)
    dmod_lat = dmod[:, :, 1]
    dmod_ctx = jnp.sum(dmod[:, :, 0], axis=1)
    d_ret = jnp.stack(small["ret_lg"]) * jax.nn.sigmoid(-ret_decay)
    summed = [dmod_ctx.reshape(-1), jnp.stack(small["norm1_g"]).reshape(-1), jnp.stack(small["norm2_g"]).reshape(-1),
              jnp.stack(small["cm_v_norm_g"]).reshape(-1), jnp.stack(small["cm_w_s"]).reshape(-1),
              jnp.stack(small["cm_b_s"]).reshape(-1), jnp.stack(small["att_q_norm_g"]).reshape(-1),
              jnp.stack(small["att_k_norm_g"]).reshape(-1), d_ret.reshape(-1)]
    sizes = [int(a.shape[0]) for a in summed]
    flat = jnp.concatenate(summed + [dmod_lat.reshape(-1)])
    n_sum = sum(sizes)
    n_sum_rows = -(-n_sum // D)
    lat_rows = (L * B * 6 * D) // D
    pack_rows = -(-(n_sum_rows + lat_rows) // 8) * 8
    packed = jnp.zeros((pack_rows * D,), F32).at[:n_sum].set(flat[:n_sum])
    packed = packed.at[n_sum_rows * D:(n_sum_rows + lat_rows) * D].set(flat[n_sum:]).reshape(pack_rows, D)
    g2 = _allgather8("gather_small", packed).reshape(N_DEV, pack_rows, D)
    tot = _sum_leading("sum_small", g2[:, :n_sum_rows]).reshape(-1)
    pieces, off = [], 0
    for sz in sizes:
        pieces.append(tot[off:off + sz])
        off += sz
    dmod_ctx_t, g_n1, g_n2, g_vg, g_ws, g_bs, g_qg, g_kg, g_rd = pieces
    dmod_ctx_t = dmod_ctx_t.reshape(L, 6 * D)
    dmod_lat_all = g2[:, n_sum_rows:n_sum_rows + lat_rows].reshape(N_DEV, L, B, 6 * D)
    dmod_rows = jnp.zeros((L, MOD_ROWS, 6 * D), F32)
    dmod_rows = dmod_rows.at[:, :n_ex].set(jnp.transpose(dmod_lat_all, (1, 0, 2, 3)).reshape(L, n_ex, 6 * D))
    dmod_rows = dmod_rows.at[:, n_ex].set(dmod_ctx_t)
    g_mod_b = _sum_leading("sum_mod_b", jnp.transpose(dmod_rows, (1, 0, 2)))
    dmod_mine = lax.dynamic_slice_in_dim(dmod_rows, chip * mcols, mcols, axis=2)
    g_mod_w = jnp.stack([_mm(f"mod_dw_{l}", act, dmod_mine[l], mode="tn", outs=[F32], tn=mcols) for l in range(L)])
    ctx8 = jnp.zeros((L, 8, mcols), F32).at[:, 0].set(dmod_mine[:, n_ex])
    dcc = [_mm(f"mod_dctx_{l}", ctx8[l], mod_w, mode="nt", layer=l, outs=[F32], tk=mcols) for l in range(L)]
    dcc = _sum_leading("sum_dctx_layers", jnp.stack(dcc))
    g3 = _allgather8("gather_dctx", dcc).reshape(N_DEV, 8, D)
    dcc_t = _sum_leading("sum_dctx_chips", g3[0::2])[0:1]
    g_c_ctx = _silu_bwd_rows("silu_bwd_cctx", c_ctx[None], dcc_t)[0]

    vg_mine = lax.dynamic_slice_in_dim(g_vg.reshape(2, -1), chip * vgw, vgw, axis=1)
    small_g = dict(c_ctx=g_c_ctx, mod_w=g_mod_w, mod_b=g_mod_b, norm1_g=g_n1.reshape(norm1_g.shape),
                   norm2_g=g_n2.reshape(norm2_g.shape), ret_decay=g_rd.reshape(ret_decay.shape),
                   att_q_norm_g=g_qg.reshape(att_q_norm_g.shape), att_k_norm_g=g_kg.reshape(att_k_norm_g.shape),
                   cm_v_norm_g=vg_mine, cm_w_s=g_ws.reshape(cm_w_s.shape), cm_b_s=g_bs.reshape(cm_b_s.shape))
    for n, g in small_g.items():
        out[n] = _adamw(f"adamw_{n}", weights[n], [g], m_in[n], v_in[n])

    order = list(weights)
    return (loss, grad_x, *[out[n][0] for n in order], *[out[n][1] for n in order],
            *[out[n][2] for n in order], *[out[n][3] for n in order])
```

```python
import functools
import math
from typing import NamedTuple

import jax
import jax.numpy as jnp
from jax import lax
from jax.experimental import pallas as pl
from jax.experimental.pallas import tpu as pltpu

F32 = jnp.float32
BF16 = jnp.bfloat16
EPS = 1e-6
ROPE_BASE = 10000.0
LANES = 128
CHUNK = 128
N_LAYERS = 4
VMEM_LIMIT = 56 * 1024 * 1024

ADAM_LR = 0.001
ADAM_B1 = 0.9
ADAM_B2 = 0.999
ADAM_EPS = 1e-08
ADAM_WD = 0.01
ADAM_STEP = 10


class Cfg(NamedTuple):
    B: int = 4
    SC: int = 256
    SL: int = 2048
    D: int = 1024
    FF: int = 4096
    GRID_W: int = 64
    H: int = 4
    KV: int = 2
    CMW: int = 1024
    CMG: int = 8

    @property
    def S(self):
        return self.SC + self.SL

    @property
    def T(self):
        return self.B * self.S

    @property
    def TM(self):
        return self.SC

    @property
    def TPE(self):
        return self.S // self.SC

    @property
    def ABW(self):
        return (5 * self.H + 2 * self.KV) * CHUNK


def _tile(dim, pref):
    t = min(dim, pref)
    while dim % t:
        t -= LANES
    return t


def _dot(a, b):
    return lax.dot_general(a, b, (((1,), (0,)), ((), ())), preferred_element_type=F32)


def _dot_nt(a, b):
    return lax.dot_general(a, b, (((1,), (1,)), ((), ())), preferred_element_type=F32)


def _dot_tn(a, b):
    return lax.dot_general(a, b, (((0,), (0,)), ((), ())), preferred_element_type=F32)


def _params(sem, vmem=VMEM_LIMIT):
    return pltpu.CompilerParams(dimension_semantics=sem, vmem_limit_bytes=vmem)


def _mod_index(cfg):
    tpe = cfg.TPE
    return lambda i: (i // tpe, jnp.minimum(i % tpe, 1), 0, 0)


def _mm(name, a, b, *, mode, outs, tm=1024, tn=1024, tk=1024, layer=None, epi=None, extras=(), carry=None):
    bshape = b.shape[1:] if layer is not None else b.shape
    if mode == "nn":
        (M, K), N = a.shape, bshape[1]
    elif mode == "nt":
        (M, K), N = a.shape, bshape[0]
    else:
        (K, M), N = a.shape, bshape[1]
    tm, tn, tk = _tile(M, tm), _tile(N, tn), _tile(K, tk)
    nk = K // tk
    a_spec = (pl.BlockSpec((tk, tm), lambda i, j, k: (k, i)) if mode == "tn"
              else pl.BlockSpec((tm, tk), lambda i, j, k: (i, k)))
    if mode == "nt":
        bblk, bidx = (tn, tk), (lambda i, j, k: (j, k))
    else:
        bblk, bidx = (tk, tn), (lambda i, j, k: (k, j))
    if layer is not None:
        b_spec = pl.BlockSpec((None,) + bblk, lambda i, j, k: (layer,) + bidx(i, j, k))
    else:
        b_spec = pl.BlockSpec(bblk, bidx)
    ne, no = len(extras), len(outs)
    nc = len(carry.srcs) if carry is not None else 0
    dot = {"nn": _dot, "nt": _dot_nt, "tn": _dot_tn}[mode]
    grid = (M // tm, N // tn, nk)

    def body(*refs):
        a_ref, b_ref = refs[0], refs[1]
        ex, out_refs = refs[2:2 + ne], refs[2 + ne + nc:2 + ne + nc + no]
        row_tile = pl.program_id(0)

        if nc:
            step = (pl.program_id(0) * grid[1] + pl.program_id(1)) * grid[2] + pl.program_id(2)
            c_src = refs[2 + ne:2 + ne + nc]
            c_dst = refs[2 + ne + nc + no:2 + ne + 2 * nc + no]
            sems = refs[2 + ne + 2 * nc + no:2 + ne + 2 * nc + no + 3]

            @pl.when(step == 0)
            def _():
                for cp in _carry_copies(carry, c_src, c_dst, *sems):
                    cp.start()

        def finish(acc):
            res = epi(acc, row_tile, *ex) if epi is not None else (acc,)
            for r, o in zip(res, out_refs):
                o[...] = r.astype(o.dtype)

        part = dot(a_ref[...].astype(BF16), b_ref[...].astype(BF16))
        if nk == 1:
            finish(part)
        else:
            acc_ref = refs[-1]
            k = pl.program_id(2)

            @pl.when(k == 0)
            def _():
                acc_ref[...] = part

            @pl.when(k > 0)
            def _():
                acc_ref[...] += part

            @pl.when(k == nk - 1)
            def _():
                finish(acc_ref[...])

        if nc:
            @pl.when(step == grid[0] * grid[1] * grid[2] - 1)
            def _():
                for cp in _carry_copies(carry, c_src, c_dst, *sems):
                    cp.wait()

    scratch = [pltpu.SemaphoreType.DMA((3 * nc,)), pltpu.SemaphoreType.DMA((3 * nc,)),
               pltpu.SemaphoreType.DMA((nc,))] if nc else []
    if nk > 1:
        scratch.append(pltpu.VMEM((tm, tn), F32))
    res = pl.pallas_call(
        body, name=name, grid=grid,
        in_specs=[a_spec, b_spec] + [s for _, s in extras] + [ANY] * nc,
        out_specs=[pl.BlockSpec((tm, tn), lambda i, j, k: (i, j)) for _ in outs] + [ANY] * nc,
        out_shape=[jax.ShapeDtypeStruct((M, N), d) for d in outs] + (_carry_out_shapes(carry) if nc else []),
        scratch_shapes=scratch,
        compiler_params=_params(("arbitrary",) * 3 if nc else ("parallel", "parallel", "arbitrary")),
    )(a, b, *[x for x, _ in extras], *(carry.srcs if nc else ()))
    if nc:
        return (res[0] if no == 1 else res[:no]), res[no:]
    return res[0] if no == 1 else res


def _ff_bwd(name, first, second, weight, kind, carry=None):
    (T, D), FF = first.shape, second.shape[1]
    halves = 2
    tm = _tile(T, 1024)
    ffh = FF // halves if kind == "w2" else FF
    dh_cols = D if kind == "w2" else D // halves
    cw = _tile(ffh, 1024)
    n_steps = T // tm
    nc = len(carry.srcs) if carry is not None else 0

    def body(*refs):
        a_ref, b_ref, w_ref = refs[:3]
        c_src = refs[3:3 + nc]
        x_ref, dw_ref = refs[3 + nc:5 + nc]
        c_dst, sems = refs[5 + nc:5 + 2 * nc], refs[5 + 2 * nc:5 + 2 * nc + 3] if nc else ()
        acc_ref = refs[-1]
        i = pl.program_id(1)
        step = pl.program_id(0) * n_steps + i
        if nc:
            _carry_begin(carry, c_src, c_dst, sems, step)
        a = a_ref[...]
        dh = None
        for c in range(ffh // cw):
            cols = pl.ds(c * cw, cw)
            if kind == "w2":
                r = b_ref[:, cols]
                x_ref[:, cols] = (_dot_nt(a, w_ref[cols, :]) * (2.0 * jnp.sqrt(r.astype(F32)))).astype(BF16)
                part, dst = _dot_tn(r, a), acc_ref.at[cols, :]
            else:
                da = b_ref[:, cols]
                term = _dot_nt(da, w_ref[:, cols])
                dh = term if dh is None else dh + term
                part, dst = _dot_tn(a, da), acc_ref.at[:, cols]

            @pl.when(i == 0)
            def _():
                dst[...] = part

            @pl.when(i > 0)
            def _():
                dst[...] += part

        if kind == "w1":
            x_ref[...] = dh.astype(BF16)

        @pl.when(i == n_steps - 1)
        def _():
            dw_ref[...] = acc_ref[...].astype(BF16)

        if nc:
            _carry_end(carry, c_src, c_dst, sems, step, halves * n_steps)

    if kind == "w2":
        wshape = (ffh, D)
        a_spec = pl.BlockSpec((tm, D), lambda j, i: (i, 0))
        b_spec = pl.BlockSpec((tm, ffh), lambda j, i: (i, j))
        x_spec, x_cols = pl.BlockSpec((tm, ffh), lambda j, i: (i, j)), FF
    else:
        wshape = (dh_cols, FF)
        a_spec = pl.BlockSpec((tm, dh_cols), lambda j, i: (i, j))
        b_spec = pl.BlockSpec((tm, FF), lambda j, i: (i, 0))
        x_spec, x_cols = pl.BlockSpec((tm, dh_cols), lambda j, i: (i, j)), D
    wspec = pl.BlockSpec(wshape, lambda j, i: (j, 0), pipeline_mode=pl.Buffered(1))
    res = pl.pallas_call(
        body, name=name, grid=(halves, n_steps),
        in_specs=[a_spec, b_spec, wspec] + [ANY] * nc,
        out_specs=[x_spec, wspec] + [ANY] * nc,
        out_shape=[jax.ShapeDtypeStruct((T, x_cols), BF16), jax.ShapeDtypeStruct(weight.shape, BF16)]
        + (_carry_out_shapes(carry) if nc else []),
        scratch_shapes=_carry_scratch(nc) + [pltpu.VMEM(wshape, F32)],
        compiler_params=_params(("arbitrary", "arbitrary")),
    )(first, second, weight, *(carry.srcs if nc else ()))
    return res[:2], res[2:]


def _norm_mod_fwd(cfg, name, x, gain, mod, ish, isc):
    T, D, TM = cfg.T, cfg.D, cfg.TM

    def body(x_ref, g_ref, mod_ref, h_ref):
        x = x_ref[...]
        rstd = lax.rsqrt(jnp.mean(x * x, axis=-1, keepdims=True) + EPS)
        n = x * rstd * g_ref[...]
        h = n * (1.0 + mod_ref[pl.ds(isc, 1), :]) + mod_ref[pl.ds(ish, 1), :]
        h_ref[...] = h.astype(BF16)

    return pl.pallas_call(
        body, name=name, grid=(T // TM,),
        in_specs=[pl.BlockSpec((TM, D), lambda i: (i, 0)), pl.BlockSpec((1, D), lambda i: (0, 0)),
                  pl.BlockSpec((None, None, 6, D), _mod_index(cfg))],
        out_specs=pl.BlockSpec((TM, D), lambda i: (i, 0)),
        out_shape=jax.ShapeDtypeStruct((T, D), BF16),
        compiler_params=_params(("parallel",)),
    )(x, gain, mod)


def _norm_mod_bwd(cfg, name, x, gain, mod, ish, isc, dh, dres, gate=None, lat_only=False):
    T, D, TM, TPE = cfg.T, cfg.D, cfg.TM, cfg.TPE
    ng = 2 if gate is not None else 0
    dx_spec = (pl.BlockSpec((TM, D), lambda i: ((i // TPE) * (TPE - 1) + jnp.maximum(i % TPE - 1, 0), 0)) if lat_only
               else pl.BlockSpec((TM, D), lambda i: (i, 0)))
    dx_rows = cfg.B * cfg.SL if lat_only else T

    def body(*refs):
        x_ref, g_ref, mod_ref, dh_ref, dres_ref = refs[:5]
        dx_ref, dmod_ref, dgain_ref = refs[5 + ng:8 + ng]
        i = pl.program_id(0)
        t = i % TPE
        x = x_ref[...]
        g = g_ref[...]
        dh = dh_ref[...].astype(F32)
        rstd = lax.rsqrt(jnp.mean(x * x, axis=-1, keepdims=True) + EPS)
        xhat = x * rstd
        dn = dh * (1.0 + mod_ref[pl.ds(isc, 1), :])
        dsh = jnp.sum(dh, axis=0, keepdims=True)
        dsc = jnp.sum(dh * (xhat * g), axis=0, keepdims=True)
        dgain = jnp.sum(dn * xhat, axis=0, keepdims=True)
        dxh = dn * g
        dx = rstd * (dxh - xhat * jnp.mean(dxh * xhat, axis=-1, keepdims=True)) + dres_ref[...]
        dx_ref[...] = dx
        sums = [(dmod_ref.at[pl.ds(0, 1), :], dsh), (dmod_ref.at[pl.ds(1, 1), :], dsc)]
        if ng:
            y_ref, gmod_ref = refs[5:7]
            dy_ref, dgate_ref = refs[8 + ng:]
            dy_ref[...] = (dx * gmod_ref[pl.ds(gate[2], 1), :]).astype(BF16)
            sums.append((dgate_ref, jnp.sum(dx * y_ref[...].astype(F32), axis=0, keepdims=True)))

        @pl.when(t <= 1)
        def _():
            for ref, val in sums:
                ref[...] = val

        @pl.when(t > 1)
        def _():
            for ref, val in sums:
                ref[...] += val

        @pl.when(i == 0)
        def _():
            dgain_ref[...] = dgain

        @pl.when(i > 0)
        def _():
            dgain_ref[...] += dgain

    tok = pl.BlockSpec((TM, D), lambda i: (i, 0))
    mod_spec = pl.BlockSpec((None, None, 6, D), _mod_index(cfg))
    res = pl.pallas_call(
        body, name=name, grid=(T // TM,),
        in_specs=[tok, pl.BlockSpec((1, D), lambda i: (0, 0)), mod_spec, tok, tok] + ([tok, mod_spec] if ng else []),
        out_specs=[dx_spec, pl.BlockSpec((None, None, 2, D), _mod_index(cfg)), pl.BlockSpec((1, D), lambda i: (0, 0))]
        + ([tok, pl.BlockSpec((None, None, 1, D), _mod_index(cfg))] if ng else []),
        out_shape=[jax.ShapeDtypeStruct((dx_rows, D), F32), jax.ShapeDtypeStruct((cfg.B, 2, 2, D), F32),
                   jax.ShapeDtypeStruct((1, D), F32)]
        + ([jax.ShapeDtypeStruct((T, D), BF16), jax.ShapeDtypeStruct((cfg.B, 2, 1, D), F32)] if ng else []),
        compiler_params=_params(("arbitrary",)),
    )(x, gain, mod, dh, dres, *(gate[:2] if ng else ()))
    return res


def _gate_bwd(cfg, name, dx, y, mod, igate):
    T, D, TM, TPE = cfg.T, cfg.D, cfg.TM, cfg.TPE

    def body(dx_ref, y_ref, mod_ref, dy_ref, dg_ref):
        t = pl.program_id(0) % TPE
        dx = dx_ref[...]
        dy_ref[...] = (dx * mod_ref[pl.ds(igate, 1), :]).astype(BF16)
        dg = jnp.sum(dx * y_ref[...].astype(F32), axis=0, keepdims=True)

        @pl.when(t <= 1)
        def _():
            dg_ref[...] = dg

        @pl.when(t > 1)
        def _():
            dg_ref[...] += dg

    tok = pl.BlockSpec((TM, D), lambda i: (i, 0))
    return pl.pallas_call(
        body, name=name, grid=(T // TM,),
        in_specs=[tok, tok, pl.BlockSpec((None, None, 6, D), _mod_index(cfg))],
        out_specs=[tok, pl.BlockSpec((None, None, 1, D), _mod_index(cfg))],
        out_shape=[jax.ShapeDtypeStruct((T, D), BF16), jax.ShapeDtypeStruct((cfg.B, 2, 1, D), F32)],
        compiler_params=_params(("arbitrary",)),
    )(dx, y, mod)


def _loss_grad(cfg, x, tgt):
    T, D, TM, TPE = cfg.T, cfg.D, cfg.TM, cfg.TPE

    def body(x_ref, t_ref, dx_ref, loss_ref):
        i = pl.program_id(0)
        t = i % TPE

        @pl.when(i == 0)
        def _():
            loss_ref[...] = jnp.zeros_like(loss_ref)

        @pl.when(t == 0)
        def _():
            dx_ref[...] = jnp.zeros_like(dx_ref)

        @pl.when(t > 0)
        def _():
            err = x_ref[...] - t_ref[...]
            dx_ref[...] = err * (1.0 / D)
            loss_ref[...] += 0.5 * jnp.sum(jnp.mean(err * err, axis=-1, keepdims=True), axis=0, keepdims=True)

    tok = pl.BlockSpec((TM, D), lambda i: (i, 0))
    tgt_spec = pl.BlockSpec((TM, D), lambda i: ((i // TPE) * (TPE - 1) + jnp.maximum(i % TPE - 1, 0), 0))
    dx, loss = pl.pallas_call(
        body, name="loss_grad", grid=(T // TM,),
        in_specs=[tok, tgt_spec], out_specs=[tok, pl.BlockSpec((8, LANES), lambda i: (0, 0))],
        out_shape=[jax.ShapeDtypeStruct((T, D), F32), jax.ShapeDtypeStruct((8, LANES), F32)],
        compiler_params=_params(("arbitrary",)),
    )(x, tgt)
    return loss[0, 0], dx


def _rope_tables(cfg):
    rows = cfg.SL // cfg.GRID_W
    row = jnp.repeat(jnp.arange(rows, dtype=F32), cfg.GRID_W)
    col = jnp.tile(jnp.arange(cfg.GRID_W, dtype=F32), rows)
    n_freq = CHUNK // 4
    inv = ROPE_BASE ** (-jnp.arange(n_freq, dtype=F32) / n_freq)
    ang = jnp.concatenate([row[:, None] * inv[None, :], col[:, None] * inv[None, :]], axis=-1)
    cos, sin = jnp.cos(ang), jnp.sin(ang)
    cosf = jnp.concatenate([jnp.ones((cfg.SC, CHUNK), F32), jnp.concatenate([cos, cos], axis=-1)], axis=0)
    sinf = jnp.concatenate([jnp.zeros((cfg.SC, CHUNK), F32), jnp.concatenate([-sin, sin], axis=-1)], axis=0)
    return cosf, sinf


def _rope(x, cosf, sinf):
    return x * cosf + pltpu.roll(x, CHUNK // 2, 1) * sinf


def _irope(dy, cosf, sinf):
    return dy * cosf - pltpu.roll(dy, CHUNK // 2, 1) * sinf


def _prep_fwd(cfg, name, p, cosf, sinf, qg, kg):
    T, TM, TPE, H, KV = cfg.T, cfg.TM, cfg.TPE, cfg.H, cfg.KV
    HW = H * CHUNK
    kscale = CHUNK ** -0.5

    def body(p_ref, c_ref, s_ref, qg_ref, kg_ref, rq_ref, rk_ref, aq_ref, ak_ref):
        cosf, sinf = c_ref[...], s_ref[...]

        def normed(x, g):
            return x * lax.rsqrt(jnp.mean(x * x, axis=-1, keepdims=True) + EPS) * g

        def seg(col):
            return p_ref[:, pl.ds(col, CHUNK)].astype(F32)

        for h in range(H):
            sl = pl.ds(h * CHUNK, CHUNK)
            rq_ref[:, sl] = _rope(seg(h * CHUNK), cosf, sinf)
            rk_ref[:, sl] = _rope(seg(HW + h * CHUNK), cosf, sinf) * kscale
            aq_ref[:, sl] = (_rope(normed(seg(4 * HW + h * CHUNK), qg_ref[...]), cosf, sinf) * ATT_SCALE).astype(BF16)
        for h in range(KV):
            ak_ref[:, pl.ds(h * CHUNK, CHUNK)] = _rope(
                normed(seg(5 * HW + h * CHUNK), kg_ref[...]), cosf, sinf).astype(BF16)

    tab = pl.BlockSpec((TM, CHUNK), lambda i: (i % TPE, 0))
    vec = pl.BlockSpec((1, CHUNK), lambda i: (0, 0))
    return pl.pallas_call(
        body, name=name, grid=(T // TM,),
        in_specs=[pl.BlockSpec((TM, cfg.ABW), lambda i: (i, 0)), tab, tab, vec, vec],
        out_specs=[pl.BlockSpec((TM, HW), lambda i: (i, 0))] * 3 + [pl.BlockSpec((TM, KV * CHUNK), lambda i: (i, 0))],
        out_shape=[jax.ShapeDtypeStruct((T, HW), F32), jax.ShapeDtypeStruct((T, HW), F32),
                   jax.ShapeDtypeStruct((T, HW), BF16), jax.ShapeDtypeStruct((T, KV * CHUNK), BF16)],
        compiler_params=_params(("parallel",)),
    )(p, cosf, sinf, qg, kg)


def _prep_bwd(cfg, name, p, cosf, sinf, qg, kg, d_rq, d_rk, d_rv, d_gate, d_aq, d_ak, d_av):
    T, TM, TPE, H, KV = cfg.T, cfg.TM, cfg.TPE, cfg.H, cfg.KV
    HW = H * CHUNK
    kscale = CHUNK ** -0.5

    def body(p_ref, c_ref, s_ref, qg_ref, kg_ref, drq_ref, drk_ref, drv_ref, dgt_ref, daq_ref, dak_ref, dav_ref,
             dp_ref, dqg_ref, dkg_ref):
        i = pl.program_id(0)
        cosf, sinf = c_ref[...], s_ref[...]

        def norm_bwd(x, g, dn):
            rstd = lax.rsqrt(jnp.mean(x * x, axis=-1, keepdims=True) + EPS)
            xhat = x * rstd
            dg = jnp.sum(dn * xhat, axis=0, keepdims=True)
            dxh = dn * g
            return rstd * (dxh - xhat * jnp.mean(dxh * xhat, axis=-1, keepdims=True)), dg

        dqg = jnp.zeros((1, CHUNK), F32)
        dkg = jnp.zeros((1, CHUNK), F32)
        for h in range(H):
            sl = pl.ds(h * CHUNK, CHUNK)
            dp_ref[:, pl.ds(h * CHUNK, CHUNK)] = _irope(drq_ref[:, sl].astype(F32), cosf, sinf).astype(BF16)
            dp_ref[:, pl.ds(HW + h * CHUNK, CHUNK)] = (_irope(drk_ref[:, sl].astype(F32), cosf, sinf)
                                                       * kscale).astype(BF16)
            dp_ref[:, pl.ds(2 * HW + h * CHUNK, CHUNK)] = drv_ref[:, sl].astype(BF16)
            dp_ref[:, pl.ds(3 * HW + h * CHUNK, CHUNK)] = dgt_ref[:, sl].astype(BF16)
            dx, dg = norm_bwd(p_ref[:, pl.ds(4 * HW + h * CHUNK, CHUNK)].astype(F32), qg_ref[...],
                              _irope(daq_ref[:, sl].astype(F32), cosf, sinf))
            dp_ref[:, pl.ds(4 * HW + h * CHUNK, CHUNK)] = dx.astype(BF16)
            dqg = dqg + dg
        for h in range(KV):
            sl = pl.ds(h * CHUNK, CHUNK)
            dx, dg = norm_bwd(p_ref[:, pl.ds(5 * HW + h * CHUNK, CHUNK)].astype(F32), kg_ref[...],
                              _irope(dak_ref[:, sl], cosf, sinf))
            dp_ref[:, pl.ds(5 * HW + h * CHUNK, CHUNK)] = dx.astype(BF16)
            dp_ref[:, pl.ds(5 * HW + (KV + h) * CHUNK, CHUNK)] = dav_ref[:, sl].astype(BF16)
            dkg = dkg + dg

        @pl.when(i == 0)
        def _():
            dqg_ref[...] = dqg
            dkg_ref[...] = dkg

        @pl.when(i > 0)
        def _():
            dqg_ref[...] += dqg
            dkg_ref[...] += dkg

    tab = pl.BlockSpec((TM, CHUNK), lambda i: (i % TPE, 0))
    vec = pl.BlockSpec((1, CHUNK), lambda i: (0, 0))
    hw = pl.BlockSpec((TM, HW), lambda i: (i, 0))
    kvw = pl.BlockSpec((TM, KV * CHUNK), lambda i: (i, 0))
    return pl.pallas_call(
        body, name=name, grid=(T // TM,),
        in_specs=[pl.BlockSpec((TM, cfg.ABW), lambda i: (i, 0)), tab, tab, vec, vec, hw, hw, hw, hw, hw, kvw, kvw],
        out_specs=[pl.BlockSpec((TM, cfg.ABW), lambda i: (i, 0)), vec, vec],
        out_shape=[jax.ShapeDtypeStruct((T, cfg.ABW), BF16), jax.ShapeDtypeStruct((1, CHUNK), F32),
                   jax.ShapeDtypeStruct((1, CHUNK), F32)],
        compiler_params=_params(("arbitrary",)),
    )(p, cosf, sinf, qg, kg, d_rq, d_rk, d_rv, d_gate, d_aq, d_ak, d_av)


def _ret_consts(direction, lg):
    C = CHUNK
    ii = lax.broadcasted_iota(jnp.int32, (C, C), 0)
    jj = lax.broadcasted_iota(jnp.int32, (C, C), 1)
    col = lax.broadcasted_iota(jnp.int32, (C, 1), 0).astype(F32)
    if direction == 0:
        mask, er, ek, eq = ii >= jj, (ii - jj).astype(F32), (C - 1.0) - col, col + 1.0
    else:
        mask, er, ek, eq = jj >= ii, (jj - ii).astype(F32), col, C - col
    er = jnp.where(mask, er, 0.0)
    dm = jnp.where(mask, jnp.exp(er * lg), 0.0)
    return dm, er, jnp.exp(ek * lg), ek, jnp.exp(eq * lg), eq, jnp.exp(C * lg)


def _ret_order(cfg, direction):
    n_all, n_ctx = cfg.S // CHUNK, cfg.SC // CHUNK
    if direction == 0:
        return list(range(n_all))
    return list(range(n_ctx - 1, -1, -1)) + list(range(n_all - 1, n_ctx - 1, -1))


def _carry_begin(carry, c_src, c_dst, sems, step):
    @pl.when(step == 0)
    def _():
        for cp in _carry_copies(carry, c_src, c_dst, *sems):
            cp.start()


def _carry_end(carry, c_src, c_dst, sems, step, n_steps):
    @pl.when(step == n_steps - 1)
    def _():
        for cp in _carry_copies(carry, c_src, c_dst, *sems):
            cp.wait()


def _carry_scratch(nc):
    return [pltpu.SemaphoreType.DMA((3 * nc,)), pltpu.SemaphoreType.DMA((3 * nc,)),
            pltpu.SemaphoreType.DMA((nc,))] if nc else []


def _head_norm_gate(o, g):
    mu = jnp.mean(o, axis=-1, keepdims=True)
    var = jnp.mean(jnp.square(o - mu), axis=-1, keepdims=True)
    rstd = lax.rsqrt(var + EPS)
    y = (o - mu) * rstd
    sg = jax.nn.sigmoid(g)
    return y, rstd, sg


RET_UNROLL = 2


def _retention_fwd(cfg, name, rq, rk, p, lgb, carry=None):
    B, H, S, T = cfg.B, cfg.H, cfg.S, cfg.T
    n_all = S // CHUNK
    nc = len(carry.srcs) if carry is not None else 0

    def body(*refs):
        q_ref, k_ref, v_ref, g_ref, lg_ref = refs[:5]
        c_src = refs[5:5 + nc]
        o_ref, ret_ref, st_ref = refs[5 + nc:8 + nc]
        c_dst = refs[8 + nc:8 + 2 * nc]
        sems = refs[8 + 2 * nc:8 + 2 * nc + 3] if nc else ()
        kv_ref = refs[-1]
        step = pl.program_id(0) * H + pl.program_id(1)
        if nc:
            _carry_begin(carry, c_src, c_dst, sems, step)

        def rows(n):
            return pl.ds(pl.multiple_of(n * CHUNK, CHUNK), CHUNK)

        (dm0, _, kd0, _, qd0, _, cd0), (dm1, _, kd1, _, qd1, _, cd1) = (
            _ret_consts(d, lg_ref[d, 0:1, 0:1]) for d in (0, 1))
        dm_both = dm0 + dm1

        def kv_step(n, c):
            k = k_ref[rows(n), :]
            v = v_ref[rows(n), :].astype(BF16)
            kv_ref[0, n] = _dot_tn((k * kd0).astype(BF16), v)
            kv_ref[1, n] = _dot_tn((k * kd1).astype(BF16), v)
            return c

        lax.fori_loop(0, n_all, kv_step, 0, unroll=RET_UNROLL)
        for direction, cd in ((0, cd0), (1, cd1)):
            st = jnp.zeros((CHUNK, CHUNK), F32)
            for t, n in enumerate(_ret_order(cfg, direction)):
                st_ref[direction, n] = st
                if t + 1 < n_all:
                    st = cd * st + kv_ref[direction, n]

        def out_step(n, c):
            q = q_ref[rows(n), :].astype(BF16)
            v = v_ref[rows(n), :].astype(BF16)
            s = _dot_nt(q, k_ref[rows(n), :].astype(BF16)) * dm_both
            states = jnp.concatenate([st_ref[0, n].astype(BF16), st_ref[1, n].astype(BF16)], axis=1)
            cross = _dot(q, states)
            o = _dot(s.astype(BF16), v) + cross[:, :CHUNK] * qd0 + cross[:, CHUNK:] * qd1
            o_ref[rows(n), :] = o
            g = g_ref[rows(n), :].astype(F32)
            y, _, sg = _head_norm_gate(o, g)
            ret_ref[rows(n), :] = (y * (g * sg)).astype(BF16)
            return c

        lax.fori_loop(0, n_all, out_step, 0, unroll=RET_UNROLL)
        if nc:
            _carry_end(carry, c_src, c_dst, sems, step, B * H)

    HW = H * CHUNK
    blk = lambda off: pl.BlockSpec((S, CHUNK), lambda b, h: (b, off + h))
    st_spec = pl.BlockSpec((None, None, 2, n_all, CHUNK, CHUNK), lambda b, h: (b, h, 0, 0, 0, 0))
    res = pl.pallas_call(
        body, name=name, grid=(B, H),
        in_specs=[blk(0), blk(0), blk(2 * H), blk(3 * H),
                  pl.BlockSpec((None, 2, 8, LANES), lambda b, h: (h, 0, 0, 0))] + [ANY] * nc,
        out_specs=[blk(0), blk(0), st_spec] + [ANY] * nc,
        out_shape=[jax.ShapeDtypeStruct((T, HW), F32), jax.ShapeDtypeStruct((T, 2 * HW), BF16),
                   jax.ShapeDtypeStruct((B, H, 2, n_all, CHUNK, CHUNK), F32)] + (_carry_out_shapes(carry) if nc else []),
        scratch_shapes=_carry_scratch(nc) + [pltpu.VMEM((2, n_all, CHUNK, CHUNK), F32)],
        compiler_params=_params(("arbitrary", "arbitrary") if nc else ("parallel", "parallel")),
    )(rq, rk, p, p, lgb, *(carry.srcs if nc else ()))
    return res[:3], res[3:]


def _retention_bwd(cfg, name, rq, rk, p, o_sum, states, dcat, lgb, carry=None):
    B, H, S, T = cfg.B, cfg.H, cfg.S, cfg.T
    n_all = S // CHUNK
    C = CHUNK
    nc = len(carry.srcs) if carry is not None else 0

    def body(*refs):
        q_ref, k_ref, v_ref, g_ref, o_ref, st_ref, dr_ref, lg_ref = refs[:8]
        c_src = refs[8:8 + nc]
        dq_ref, dk_ref, dv_ref, dg_ref, dlg_ref = refs[8 + nc:13 + nc]
        c_dst = refs[13 + nc:13 + 2 * nc]
        sems = refs[13 + 2 * nc:13 + 2 * nc + 3] if nc else ()
        do_ref, gq_ref, ds_ref, acc_ref = refs[-4:]
        step = pl.program_id(0) * H + pl.program_id(1)
        if nc:
            _carry_begin(carry, c_src, c_dst, sems, step)

        def rows(n):
            return pl.ds(pl.multiple_of(n * C, C), C)

        def gate_step(n, c):
            g = g_ref[rows(n), :].astype(F32)
            dr = dr_ref[rows(n), :].astype(F32)
            y, rstd, sg = _head_norm_gate(o_ref[rows(n), :], g)
            dy = dr * (g * sg)
            dg_ref[rows(n), :] = (dr * y * (sg * (1.0 + g * (1.0 - sg)))).astype(BF16)
            do_ref[rows(n), :] = rstd * (dy - jnp.mean(dy, axis=-1, keepdims=True)
                                         - y * jnp.mean(dy * y, axis=-1, keepdims=True))
            return c

        lax.fori_loop(0, n_all, gate_step, 0, unroll=RET_UNROLL)

        (dm0, er0, kd0, ek0, qd0, eq0, cd0), (dm1, er1, kd1, ek1, qd1, eq1, cd1) = (
            _ret_consts(d, lg_ref[d, 0:1, 0:1]) for d in (0, 1))
        dm_both = dm0 + dm1
        wdm0, wdm1 = dm0 * er0, dm1 * er1

        def side(a, b):
            return jnp.concatenate([a.astype(BF16), b.astype(BF16)], axis=1)

        def gq_step(n, c):
            do = do_ref[rows(n), :]
            gq = _dot_tn(q_ref[rows(n), :].astype(BF16), side(do * qd0, do * qd1))
            gq_ref[0, n] = gq[:, :C]
            gq_ref[1, n] = gq[:, C:]
            return c

        lax.fori_loop(0, n_all, gq_step, 0, unroll=RET_UNROLL)
        for direction, cd in ((0, cd0), (1, cd1)):
            order = _ret_order(cfg, direction)
            ds = jnp.zeros((C, C), F32)
            for t in reversed(range(n_all)):
                ds_ref[direction, order[t]] = ds
                if t > 0:
                    ds = cd * ds + gq_ref[direction, order[t]]
        acc_ref[...] = jnp.zeros_like(acc_ref)

        def chunk_step(n, c):
            q = q_ref[rows(n), :].astype(BF16)
            kf = k_ref[rows(n), :]
            k = kf.astype(BF16)
            v = v_ref[rows(n), :].astype(BF16)
            do = do_ref[rows(n), :]
            dob = do.astype(BF16)
            sp0, sp1 = st_ref[0, n], st_ref[1, n]
            ds0, ds1 = ds_ref[0, n], ds_ref[1, n]
            states = side(sp0, sp1)
            dstates = jnp.concatenate([ds0.astype(BF16), ds1.astype(BF16)], axis=0)
            doq0, doq1 = do * qd0, do * qd1
            doq = side(doq0, doq1)
            s_raw = _dot_nt(q, k)
            dpm = _dot_nt(dob, v)
            dsr = (dpm * dm_both).astype(BF16)
            dks = _dot_nt(v, dstates)
            dks0, dks1 = dks[:, :C] * kd0, dks[:, C:] * kd1
            qs = _dot(q, states)
            dq_ref[rows(n), :] = (_dot(dsr, k) + _dot_nt(doq, states)).astype(BF16)
            dk_ref[rows(n), :] = (_dot_tn(dsr, q) + dks0 + dks1).astype(BF16)
            dv_ref[rows(n), :] = (_dot_tn((s_raw * dm_both).astype(BF16), dob)
                                  + _dot(side(kf * kd0, kf * kd1), dstates)).astype(BF16)
            inner = dpm * s_raw
            acc_ref[0] += (jnp.sum(inner * wdm0, axis=0, keepdims=True)
                           + jnp.sum(eq0 * doq0 * qs[:, :C], axis=0, keepdims=True)
                           + jnp.sum(ek0 * kf * dks0, axis=0, keepdims=True)
                           + (C * cd0) * jnp.sum(ds0 * sp0, axis=0, keepdims=True))
            acc_ref[1] += (jnp.sum(inner * wdm1, axis=0, keepdims=True)
                           + jnp.sum(eq1 * doq1 * qs[:, C:], axis=0, keepdims=True)
                           + jnp.sum(ek1 * kf * dks1, axis=0, keepdims=True)
                           + (C * cd1) * jnp.sum(ds1 * sp1, axis=0, keepdims=True))
            return c

        lax.fori_loop(0, n_all, chunk_step, 0, unroll=RET_UNROLL)
        for direction in (0, 1):
            dlg_ref[direction] = jnp.broadcast_to(jnp.sum(acc_ref[direction], axis=1, keepdims=True), (8, LANES))
        if nc:
            _carry_end(carry, c_src, c_dst, sems, step, B * H)

    HW = H * CHUNK
    blk = lambda off: pl.BlockSpec((S, CHUNK), lambda b, h: (b, off + h))
    st_spec = pl.BlockSpec((None, None, 2, n_all, C, C), lambda b, h: (b, h, 0, 0, 0, 0))
    res = pl.pallas_call(
        body, name=name, grid=(B, H),
        in_specs=[blk(0), blk(0), blk(2 * H), blk(3 * H), blk(0), st_spec, blk(0),
                  pl.BlockSpec((None, 2, 8, LANES), lambda b, h: (h, 0, 0, 0))] + [ANY] * nc,
        out_specs=[blk(0)] * 4 + [pl.BlockSpec((None, None, 2, 8, LANES), lambda b, h: (b, h, 0, 0, 0))] + [ANY] * nc,
        out_shape=[jax.ShapeDtypeStruct((T, HW), BF16)] * 4 + [jax.ShapeDtypeStruct((B, H, 2, 8, LANES), F32)]
        + (_carry_out_shapes(carry) if nc else []),
        scratch_shapes=_carry_scratch(nc) + [pltpu.VMEM((S, CHUNK), F32), pltpu.VMEM((2, n_all, C, C), F32),
                                             pltpu.VMEM((2, n_all, C, C), F32), pltpu.VMEM((2, 1, C), F32)],
        compiler_params=_params(("arbitrary", "arbitrary") if nc else ("parallel", "parallel")),
    )(rq, rk, p, p, o_sum, states, dcat, lgb, *(carry.srcs if nc else ()))
    return res[:5], res[5:]


ATT_SCALE = CHUNK ** -0.5


def _attn_scores(cfg, q, k, t):
    kcol = lax.broadcasted_iota(jnp.int32, (1, cfg.S), 1)
    bias = jnp.where(jnp.logical_or(t > 0, kcol < cfg.SC), 0.0, -1e30)
    return _dot_nt(q, k) + bias


def _attention_fwd(cfg, name, aq, ak, p, cat, carry=None):
    B, H, KV, S, T, TM, TPE = cfg.B, cfg.H, cfg.KV, cfg.S, cfg.T, cfg.TM, cfg.TPE
    G = H // KV
    v_off = (5 * H + KV)
    nc = len(carry.srcs) if carry is not None else 0

    def body(*refs):
        q_ref, k_ref, v_ref = refs[:3]
        o_ref, lse_ref = refs[4 + nc:6 + nc]
        c_src, c_dst, sems = refs[4:4 + nc], refs[6 + nc:6 + 2 * nc], refs[6 + 2 * nc:]
        step = (pl.program_id(0) * H + pl.program_id(1)) * TPE + pl.program_id(2)
        if nc:
            _carry_begin(carry, c_src, c_dst, sems, step)
        s = _attn_scores(cfg, q_ref[...], k_ref[...], pl.program_id(2))
        m = jnp.max(s, axis=-1, keepdims=True)
        e = jnp.exp(s - m)
        total = jnp.sum(e, axis=-1, keepdims=True)
        o_ref[...] = (_dot(e.astype(BF16), v_ref[...].astype(BF16)) * (1.0 / total)).astype(BF16)
        lse_ref[...] = m + jnp.log(total)
        if nc:
            _carry_end(carry, c_src, c_dst, sems, step, B * H * TPE)

    res = pl.pallas_call(
        body, name=name, grid=(B, H, TPE),
        in_specs=[pl.BlockSpec((TM, CHUNK), lambda b, h, t: (b * TPE + t, h)),
                  pl.BlockSpec((S, CHUNK), lambda b, h, t: (b, h // G)),
                  pl.BlockSpec((S, CHUNK), lambda b, h, t: (b, v_off + h // G)), ANY] + [ANY] * nc,
        out_specs=[pl.BlockSpec((TM, CHUNK), lambda b, h, t: (b * TPE + t, H + h)),
                   pl.BlockSpec((None, TM, 1), lambda b, h, t: (h, b * TPE + t, 0))] + [ANY] * nc,
        out_shape=[jax.ShapeDtypeStruct(cat.shape, cat.dtype), jax.ShapeDtypeStruct((H, T, 1), F32)]
        + (_carry_out_shapes(carry) if nc else []),
        input_output_aliases={3: 0},
        scratch_shapes=_carry_scratch(nc),
        compiler_params=_params(("arbitrary",) * 3 if nc else ("parallel",) * 3),
    )(aq, ak, p, cat, *(carry.srcs if nc else ()))
    return res[:2], res[2:]


def _attention_bwd(cfg, name, aq, ak, p, cat, lse, dcat, carry=None):
    B, H, KV, S, T, TM, TPE = cfg.B, cfg.H, cfg.KV, cfg.S, cfg.T, cfg.TM, cfg.TPE
    G = H // KV
    v_off = (5 * H + KV)
    nc = len(carry.srcs) if carry is not None else 0

    def body(*refs):
        q_ref, k_ref, v_ref, o_ref, lse_ref, do_ref = refs[:6]
        dq_ref, dk_ref, dv_ref = refs[6 + nc:9 + nc]
        c_src, c_dst, sems = refs[6:6 + nc], refs[9 + nc:9 + 2 * nc], refs[9 + 2 * nc:]
        g, t = pl.program_id(2), pl.program_id(3)
        step = ((pl.program_id(0) * KV + pl.program_id(1)) * G + g) * TPE + t
        if nc:
            _carry_begin(carry, c_src, c_dst, sems, step)
        q, k = q_ref[...], k_ref[...]
        v = v_ref[...].astype(BF16)
        do = do_ref[...]
        dob = do.astype(BF16)
        pr = jnp.exp(_attn_scores(cfg, q, k, t) - lse_ref[...])
        delta = jnp.sum(do.astype(F32) * o_ref[...].astype(F32), axis=-1, keepdims=True)
        ds = (pr * (_dot_nt(dob, v) - delta)).astype(BF16)
        dq_ref[...] = (_dot(ds, k) * ATT_SCALE).astype(BF16)
        dk = _dot_tn(ds, q)
        dv = _dot_tn(pr.astype(BF16), dob)
        first = jnp.logical_and(g == 0, t == 0)

        @pl.when(first)
        def _():
            dk_ref[...] = dk
            dv_ref[...] = dv

        @pl.when(jnp.logical_not(first))
        def _():
            dk_ref[...] += dk
            dv_ref[...] += dv

        if nc:
            _carry_end(carry, c_src, c_dst, sems, step, B * KV * G * TPE)

    qspec = pl.BlockSpec((TM, CHUNK), lambda b, kv, g, t: (b * TPE + t, kv * G + g))
    kvspec = pl.BlockSpec((S, CHUNK), lambda b, kv, g, t: (b, kv))
    right = pl.BlockSpec((TM, CHUNK), lambda b, kv, g, t: (b * TPE + t, H + kv * G + g))
    res = pl.pallas_call(
        body, name=name, grid=(B, KV, G, TPE),
        in_specs=[qspec, kvspec, pl.BlockSpec((S, CHUNK), lambda b, kv, g, t: (b, v_off + kv)), right,
                  pl.BlockSpec((None, TM, 1), lambda b, kv, g, t: (kv * G + g, b * TPE + t, 0)), right] + [ANY] * nc,
        out_specs=[qspec, kvspec, kvspec] + [ANY] * nc,
        out_shape=[jax.ShapeDtypeStruct((T, H * CHUNK), BF16), jax.ShapeDtypeStruct((T, KV * CHUNK), F32),
                   jax.ShapeDtypeStruct((T, KV * CHUNK), F32)] + (_carry_out_shapes(carry) if nc else []),
        scratch_shapes=_carry_scratch(nc),
        compiler_params=_params(("arbitrary",) * 4 if nc else ("parallel", "parallel", "arbitrary", "arbitrary")),
    )(aq, ak, p, cat, lse, dcat, *(carry.srcs if nc else ()))
    return res[:3], res[3:]


_GELU_C = math.sqrt(2.0 / math.pi)


def _gelu(x):
    return 0.5 * x * (1.0 + jnp.tanh(_GELU_C * (x + 0.044715 * x * x * x)))


def _gelu_and_grad(x):
    x2 = x * x
    th = jnp.tanh(_GELU_C * (x + 0.044715 * x * x2))
    half = 0.5 * (1.0 + th)
    return x * half, half + 0.5 * x * (1.0 - th * th) * _GELU_C * (1.0 + 3.0 * 0.044715 * x2)


def _cm_fwd(cfg, name, a, vg, ws, bs):
    T, TM, W, NG = cfg.T, cfg.TM, cfg.CMW, cfg.CMG

    def body(a_ref, vg_ref, ws_ref, bs_ref, m_ref):
        v = _gelu(a_ref[:, pl.ds(W, W)].astype(F32))
        vn = (v * lax.rsqrt(jnp.mean(v * v, axis=-1, keepdims=True) + EPS) * vg_ref[...]).astype(BF16)
        for c in range(TM // CHUNK):
            for g in range(NG):
                rows, cols = slice(c * CHUNK, (c + 1) * CHUNK), slice(g * CHUNK, (g + 1) * CHUNK)
                sv = _dot(ws_ref[g].astype(BF16), vn[rows, cols]) + bs_ref[g]
                u = _gelu(a_ref[pl.ds(c * CHUNK, CHUNK), pl.ds(g * CHUNK, CHUNK)].astype(F32))
                m_ref[pl.ds(c * CHUNK, CHUNK), pl.ds(g * CHUNK, CHUNK)] = (u * sv).astype(BF16)

    return pl.pallas_call(
        body, name=name, grid=(T // TM,),
        in_specs=[pl.BlockSpec((TM, 2 * W), lambda i: (i, 0)), pl.BlockSpec((1, W), lambda i: (0, 0)),
                  pl.BlockSpec((NG, CHUNK, CHUNK), lambda i: (0, 0, 0)),
                  pl.BlockSpec((NG, CHUNK, 1), lambda i: (0, 0, 0))],
        out_specs=pl.BlockSpec((TM, W), lambda i: (i, 0)),
        out_shape=jax.ShapeDtypeStruct((T, W), BF16),
        compiler_params=_params(("parallel",)),
    )(a, vg, ws, bs)


def _cm_bwd(cfg, name, a, vg, ws, bs, dm):
    T, TM, W, NG = cfg.T, cfg.TM, cfg.CMW, cfg.CMG

    def body(a_ref, vg_ref, ws_ref, bs_ref, dm_ref, da_ref, dws_ref, dbs_ref, dvg_ref, dvn_ref):
        i = pl.program_id(0)

        @pl.when(i == 0)
        def _():
            dws_ref[...] = jnp.zeros_like(dws_ref)
            dbs_ref[...] = jnp.zeros_like(dbs_ref)
            dvg_ref[...] = jnp.zeros_like(dvg_ref)

        v, v_grad = _gelu_and_grad(a_ref[:, pl.ds(W, W)].astype(F32))
        rstd = lax.rsqrt(jnp.mean(v * v, axis=-1, keepdims=True) + EPS)
        xhat = v * rstd
        vg = vg_ref[...]
        vn = (xhat * vg).astype(BF16)
        for c in range(TM // CHUNK):
            for g in range(NG):
                rows, cols = slice(c * CHUNK, (c + 1) * CHUNK), slice(g * CHUNK, (g + 1) * CHUNK)
                rs, cs = pl.ds(c * CHUNK, CHUNK), pl.ds(g * CHUNK, CHUNK)
                wsb = ws_ref[g].astype(BF16)
                blk = vn[rows, cols]
                sv = _dot(wsb, blk) + bs_ref[g]
                u, u_grad = _gelu_and_grad(a_ref[rs, cs].astype(F32))
                dmb = dm_ref[rs, cs].astype(F32)
                da_ref[rs, cs] = (dmb * sv * u_grad).astype(BF16)
                dsv = dmb * u
                dsvb = dsv.astype(BF16)
                dbs_ref[g] += jnp.sum(dsv, axis=1, keepdims=True)
                dws_ref[g] += _dot_nt(dsvb, blk)
                dvn_ref[rs, cs] = _dot_tn(wsb, dsvb)
        dvn = dvn_ref[...]
        dvg_ref[...] += jnp.sum(dvn * xhat, axis=0, keepdims=True)
        dxh = dvn * vg
        dv = rstd * (dxh - xhat * jnp.mean(dxh * xhat, axis=-1, keepdims=True))
        da_ref[:, pl.ds(W, W)] = (dv * v_grad).astype(BF16)

    return pl.pallas_call(
        body, name=name, grid=(T // TM,),
        in_specs=[pl.BlockSpec((TM, 2 * W), lambda i: (i, 0)), pl.BlockSpec((1, W), lambda i: (0, 0)),
                  pl.BlockSpec((NG, CHUNK, CHUNK), lambda i: (0, 0, 0)),
                  pl.BlockSpec((NG, CHUNK, 1), lambda i: (0, 0, 0)), pl.BlockSpec((TM, W), lambda i: (i, 0))],
        out_specs=[pl.BlockSpec((TM, 2 * W), lambda i: (i, 0)), pl.BlockSpec((NG, CHUNK, CHUNK), lambda i: (0, 0, 0)),
                   pl.BlockSpec((NG, CHUNK, 1), lambda i: (0, 0, 0)), pl.BlockSpec((1, W), lambda i: (0, 0))],
        out_shape=[jax.ShapeDtypeStruct((T, 2 * W), BF16), jax.ShapeDtypeStruct((NG, CHUNK, CHUNK), F32),
                   jax.ShapeDtypeStruct((NG, CHUNK, 1), F32), jax.ShapeDtypeStruct((1, W), F32)],
        scratch_shapes=[pltpu.VMEM((TM, W), F32)],
        compiler_params=_params(("arbitrary",)),
    )(a, vg, ws, bs, dm)


def _layer_weights(l):
    mixer = ("ab_w_in", "ab_w_out") if l % 2 == 0 else ("cm_w_in", "cm_w_out")
    return [(mixer[0], l // 2), (mixer[1], l // 2), ("ff_w1", l), ("ff_w2", l)]


def _local_step(cfg, xcat, tgt, mods, shards, w):
    D, TM, H = cfg.D, cfg.TM, cfg.H
    cosf, sinf = _rope_tables(cfg)
    full, big, recv = {}, {}, {}

    def gather_of(keys):
        return _Carry("gather", tuple(shards[n][i] for n, i in keys), tuple(BIG[n] for n, _ in keys))

    def exchange_of(keys):
        return _Carry("exchange", tuple(big[k] for k in keys), tuple(BIG[n] for n, _ in keys))

    def mm(pending, name, a, b, **kw):
        if not pending:
            return _mm(name, a, b, **kw)
        key, carry, sink = pending.pop(0)
        out, (got,) = _mm(name, a, b, carry=carry, **kw)
        sink[key] = got
        return out

    def with_carry(call, keys, make, sink):
        out, got = call(carry=make(keys) if keys else None)
        sink.update(zip(keys, got))
        return out

    keys0 = _layer_weights(0)
    full[keys0[0]], = _comm_call("gather_weights_0", gather_of(keys0[:1]))
    TG = 3 * TM if cfg.TPE % 3 == 0 else TM
    tiles_per_ex = cfg.S // TG
    gate_spec = pl.BlockSpec((None, 2, 6, D), lambda i, j, k: (i // tiles_per_ex, 0, 0, 0))

    def resid_epi(igate, nxt):
        def epi(acc, row_tile, x_ref, mod_ref, *nxt_refs):
            row = lax.broadcasted_iota(jnp.int32, (TG, 1), 0)
            is_ctx = jnp.logical_and(row_tile % tiles_per_ex == 0, row < cfg.SC)

            def pick(ref, idx):
                return jnp.where(is_ctx, ref[0, pl.ds(idx, 1), :], ref[1, pl.ds(idx, 1), :])

            x = x_ref[...] + pick(mod_ref, igate) * acc
            if nxt is None:
                return x, acc
            gain_ref, modn_ref = nxt_refs
            n = x * lax.rsqrt(jnp.mean(x * x, axis=-1, keepdims=True) + EPS) * gain_ref[...]
            return x, acc, n * (1.0 + pick(modn_ref, nxt[3])) + pick(modn_ref, nxt[2])
        return epi

    def gated_out(pending, name, a, key, x, mod, igate, nxt=None):
        extras = [(x, pl.BlockSpec((TG, D), lambda i, j, k: (i, j))), (mod, gate_spec)]
        if nxt is not None:
            extras += [(nxt[0], pl.BlockSpec((1, D), lambda i, j, k: (0, 0))), (nxt[1], gate_spec)]
        return mm(pending, name, a, full[key], mode="nn", tm=TG, tn=D, outs=[F32, BF16] + [BF16] * (nxt is not None),
                  epi=resid_epi(igate, nxt), extras=extras)

    saved = []
    x = xcat
    h = _norm_mod_fwd(cfg, "norm1_fwd_0", x, w["norm1_g"][0][None], mods[0], 0, 1)
    for l in range(N_LAYERS):
        li = l // 2
        mod = mods[l]
        k_in, k_out, k_ff1, k_ff2 = _layer_weights(l)
        pend = [(k, gather_of([k]), full) for k in _layer_weights(l + 1)] if l + 1 < N_LAYERS else []
        norm2 = (w["norm2_g"][l][None], mod, 3, 4)
        s = {"x0": x, "h": h}
        if l % 2 == 0:
            lgb = jnp.broadcast_to(jax.nn.log_sigmoid(w["ret_decay"][li]).T[:, :, None, None], (H, 2, 8, LANES))
            qg, kg = w["att_q_norm_g"][li][None], w["att_k_norm_g"][li][None]
            s["p"] = mm(pend, f"ab_in_{l}", s["h"], full[k_in], mode="nn", outs=[BF16], tn=768)
            s["rq"], s["rk"], s["aq"], s["ak"] = _prep_fwd(cfg, f"prep_fwd_{l}", s["p"], cosf, sinf, qg, kg)
            s["o"], ret, s["st"] = with_carry(
                functools.partial(_retention_fwd, cfg, f"ret_fwd_{l}", s["rq"], s["rk"], s["p"], lgb),
                keys0[1:3] if l == 0 else [], gather_of, full)
            s["cat"], s["lse"] = with_carry(
                functools.partial(_attention_fwd, cfg, f"att_fwd_{l}", s["aq"], s["ak"], s["p"], ret),
                keys0[3:] if l == 0 else [], gather_of, full)
            s["lgb"], s["qg"], s["kg"] = lgb, qg, kg
            x, s["y1"], s["h2"] = gated_out(pend, f"ab_out_{l}", s["cat"], k_out, x, mod, 2, norm2)
        else:
            s["a"] = mm(pend, f"cm_in_{l}", s["h"], full[k_in], mode="nn", outs=[BF16])
            s["vg"], s["ws"], s["bs"] = w["cm_v_norm_g"][li][None], w["cm_w_s"][li], w["cm_b_s"][li][:, :, None]
            s["m"] = _cm_fwd(cfg, f"cm_fwd_{l}", s["a"], s["vg"], s["ws"], s["bs"])
            x, s["y1"], s["h2"] = gated_out(pend, f"cm_out_{l}", s["m"], k_out, x, mod, 2, norm2)
        s["x1"] = x
        s["r"] = mm(pend, f"ff1_{l}", s["h2"], full[k_ff1], mode="nn", outs=[BF16],
                    epi=lambda acc, row_tile: (jnp.square(jnp.maximum(acc, 0.0)),))
        if l + 1 < N_LAYERS:
            x, s["y2"], h = gated_out(pend, f"ff2_{l}", s["r"], k_ff2, x, mod, 5,
                                      (w["norm1_g"][l + 1][None], mods[l + 1], 0, 1))
        else:
            x, s["y2"] = gated_out(pend, f"ff2_{l}", s["r"], k_ff2, x, mod, 5)
        saved.append(s)

    loss, dx = _loss_grad(cfg, x, tgt)

    small = {k: [None] * n for k, n in (("norm1_g", 4), ("norm2_g", 4), ("ret_lg", 2), ("att_q_norm_g", 2),
                                        ("att_k_norm_g", 2), ("cm_v_norm_g", 2), ("cm_w_s", 2), ("cm_b_s", 2))}
    dmods = [None] * N_LAYERS

    for l in reversed(range(N_LAYERS)):
        li = l // 2
        s, mod = saved[l], mods[l]
        k_in, k_out, k_ff1, k_ff2 = _layer_weights(l)
        above = _layer_weights(l + 1) if l + 1 < N_LAYERS else [None] * 4
        if l == N_LAYERS - 1:
            dy2, dg2 = _gate_bwd(cfg, f"gate2_bwd_{l}", dx, s["y2"], mod, 5)
        da2, big[k_ff2] = with_carry(functools.partial(_ff_bwd, f"ff2_bwd_{l}", dy2, s["r"], full[k_ff2], "w2"),
                                     [above[3], above[1]] if above[0] else [], exchange_of, recv)
        dh2, big[k_ff1] = with_carry(functools.partial(_ff_bwd, f"ff1_bwd_{l}", s["h2"], da2, full[k_ff1], "w1"),
                                     [above[2], above[0]] if above[0] else [], exchange_of, recv)
        dx, dm2, small["norm2_g"][l], do, dg1 = _norm_mod_bwd(
            cfg, f"norm2_bwd_{l}", s["x1"], w["norm2_g"][l][None], mod, 3, 4, dh2, dx, gate=(s["y1"], mod, 2))
        if l % 2 == 0:
            big[k_out] = _mm(f"ab_out_dw_{l}", s["cat"], do, mode="tn", outs=[BF16])
            dcat = _mm(f"ab_out_dx_{l}", do, full[k_out], mode="nt", outs=[BF16])
            d_rq, d_rk, d_rv, d_gt, dlg = with_carry(
                functools.partial(_retention_bwd, cfg, f"ret_bwd_{l}", s["rq"], s["rk"], s["p"], s["o"], s["st"], dcat,
                                  s["lgb"]), [k_ff2, k_ff1] if l == 0 else [], exchange_of, recv)
            d_aq, d_ak, d_av = with_carry(
                functools.partial(_attention_bwd, cfg, f"att_bwd_{l}", s["aq"], s["ak"], s["p"], s["cat"], s["lse"], dcat),
                [k_out] if l == 0 else [], exchange_of, recv)
            dp, dqg, dkg = _prep_bwd(cfg, f"prep_bwd_{l}", s["p"], cosf, sinf, s["qg"], s["kg"],
                                     d_rq, d_rk, d_rv, d_gt, d_aq, d_ak, d_av)
            small["ret_lg"][li] = jnp.sum(dlg[:, :, :, 0, 0], axis=0).T
            small["att_q_norm_g"][li], small["att_k_norm_g"][li] = dqg[0], dkg[0]
            big[k_in] = _mm(f"ab_in_dw_{l}", s["h"], dp, mode="tn", outs=[BF16])
            last = [(k_in, exchange_of([k_in]), recv)] if l == 0 else []
            dh = mm(last, f"ab_in_dx_{l}", dp, full[k_in], mode="nt", outs=[BF16], tk=768)
        else:
            big[k_out] = _mm(f"cm_out_dw_{l}", s["m"], do, mode="tn", outs=[BF16])
            dm = _mm(f"cm_out_dx_{l}", do, full[k_out], mode="nt", outs=[BF16])
            da, dws, dbs, dvg = _cm_bwd(cfg, f"cm_bwd_{l}", s["a"], s["vg"], s["ws"], s["bs"], dm)
            small["cm_w_s"][li], small["cm_b_s"][li], small["cm_v_norm_g"][li] = dws, dbs[:, :, 0], dvg[0]
            big[k_in] = _mm(f"cm_in_dw_{l}", s["h"], da, mode="tn", outs=[BF16])
            dh = _mm(f"cm_in_dx_{l}", da, full[k_in], mode="nt", outs=[BF16])
        below = (saved[l - 1]["y2"], mods[l - 1], 5) if l > 0 else None
        dx, dm1, small["norm1_g"][l], *rest = _norm_mod_bwd(
            cfg, f"norm1_bwd_{l}", s["x0"], w["norm1_g"][l][None], mod, 0, 1, dh, dx, gate=below, lat_only=l == 0)
        dmods[l] = jnp.concatenate([dm1, dg1, dm2, dg2], axis=2)
        if l > 0:
            dy2, dg2 = rest
    return loss, dx, recv, small, dmods


N_DEV = 8
N_CHIP = 4
MESH = pl.DeviceIdType.MESH
ANY = pl.BlockSpec(memory_space=pl.ANY)
BIG = {"ab_w_in": 1, "ab_w_out": 0, "cm_w_in": 1, "cm_w_out": 0, "ff_w1": 1, "ff_w2": 0}


class _Carry(NamedTuple):
    kind: str
    srcs: tuple
    axes: tuple


def _place():
    x, y, c = lax.axis_index("x"), lax.axis_index("y"), lax.axis_index("c")
    return x, y, c, [(1 - x, y), (x, 1 - y), (1 - x, 1 - y)]


def _shard_of(ref, axis, s, width):
    start = pl.multiple_of(s * width, LANES)
    if axis == 0:
        return ref.at[pl.ds(start, width), :]
    return ref.at[:, pl.ds(start, width)]


def _carry_out_shapes(carry):
    shapes = []
    for src, axis in zip(carry.srcs, carry.axes):
        shape = list(src.shape)
        if carry.kind == "gather":
            shape[axis] *= N_CHIP
        else:
            shape[axis] //= N_CHIP
            shape = [N_CHIP] + shape
        shapes.append(jax.ShapeDtypeStruct(tuple(shape), src.dtype))
    return shapes


def _carry_copies(carry, srcs, dsts, send_sems, recv_sems, local_sems):
    x, y, c, chips = _place()
    me = 2 * x + y
    copies = []
    for t, axis in enumerate(carry.axes):
        if carry.kind == "gather":
            own = _shard_of(dsts[t], axis, me, srcs[t].shape[axis])
            copies.append(pltpu.make_async_copy(srcs[t], own, local_sems.at[t]))
            parts = [(srcs[t], own)] * 3
        else:
            width = dsts[t].shape[1 + axis]
            copies.append(pltpu.make_async_copy(_shard_of(srcs[t], axis, me, width), dsts[t].at[3], local_sems.at[t]))
            parts = [(_shard_of(srcs[t], axis, 2 * px + py, width), dsts[t].at[j]) for j, (px, py) in enumerate(chips)]
        for j, (px, py) in enumerate(chips):
            copies.append(pltpu.make_async_remote_copy(
                src_ref=parts[j][0], dst_ref=parts[j][1], send_sem=send_sems.at[3 * t + j],
                recv_sem=recv_sems.at[3 * t + j], device_id=(px, py, c), device_id_type=MESH))
    return copies


def _comm_call(name, carry):
    nc = len(carry.srcs)

    def body(*refs):
        copies = _carry_copies(carry, refs[:nc], refs[nc:2 * nc], *refs[2 * nc:])
        for cp in copies:
            cp.start()
        for cp in copies:
            cp.wait()

    return pl.pallas_call(
        body, name=name, out_shape=_carry_out_shapes(carry), in_specs=[ANY] * nc, out_specs=[ANY] * nc,
        scratch_shapes=[pltpu.SemaphoreType.DMA((3 * nc,)), pltpu.SemaphoreType.DMA((3 * nc,)),
                        pltpu.SemaphoreType.DMA((nc,))],
    )(*carry.srcs)


def _allgather8(name, block):
    m_per, n = block.shape

    def body(x_ref, out_ref, send_sems, recv_sems, local_sem):
        x, y, c, chips = _place()
        me, sibling = (x, y, c), (x, y, 1 - c)

        def rows(px, py, pc):
            return out_ref.at[pl.ds((4 * px + 2 * py + pc) * m_per, m_per), :]

        def copy(k, blk, to, src=None):
            return pltpu.make_async_remote_copy(
                src_ref=rows(*blk) if src is None else src, dst_ref=rows(*blk),
                send_sem=send_sems.at[k], recv_sem=recv_sems.at[k], device_id=to, device_id_type=MESH)

        mine = pltpu.make_async_copy(x_ref, rows(*me), local_sem)
        mine.start()
        first = [copy(0, me, sibling, src=x_ref)]
        first += [copy(1 + j, me, (*chip, c), src=x_ref) for j, chip in enumerate(chips)]
        for cp in first:
            cp.start()
        passed = [copy(4 + j, (*chip, c), sibling) for j, chip in enumerate(chips)]
        for j, chip in enumerate(chips):
            copy(1 + j, (*chip, c), me).wait_recv()
            passed[j].start()
        copy(0, sibling, me).wait_recv()
        for j, chip in enumerate(chips):
            copy(4 + j, (*chip, 1 - c), me).wait_recv()
        for cp in first + passed:
            cp.wait_send()
        mine.wait()

    return pl.pallas_call(
        body, name=name, out_shape=jax.ShapeDtypeStruct((N_DEV * m_per, n), block.dtype),
        in_specs=[pl.BlockSpec(memory_space=pltpu.VMEM)], out_specs=pl.BlockSpec(memory_space=pltpu.VMEM),
        scratch_shapes=[pltpu.SemaphoreType.DMA((7,)), pltpu.SemaphoreType.DMA((7,)), pltpu.SemaphoreType.DMA],
        compiler_params=pltpu.CompilerParams(vmem_limit_bytes=VMEM_LIMIT),
    )(block)


def _swap_sibling(parts):
    n_t = len(parts)

    def body(*refs):
        srcs, outs = refs[:n_t], refs[n_t:2 * n_t]
        send_sems, recv_sems = refs[2 * n_t:]
        x, y, c, _ = _place()
        copies = []
        for t in range(n_t):
            cp = pltpu.make_async_remote_copy(
                src_ref=srcs[t], dst_ref=outs[t], send_sem=send_sems.at[t], recv_sem=recv_sems.at[t],
                device_id=(x, y, 1 - c), device_id_type=MESH)
            cp.start()
            copies.append(cp)
        for cp in copies:
            cp.wait()

    return pl.pallas_call(
        body, name="swap_sibling", out_shape=[jax.ShapeDtypeStruct(p.shape, p.dtype) for p in parts],
        in_specs=[ANY] * n_t, out_specs=[ANY] * n_t,
        scratch_shapes=[pltpu.SemaphoreType.DMA((n_t,)), pltpu.SemaphoreType.DMA((n_t,))],
    )(*parts)


def _rows_view(a):
    if a.ndim == 1:
        return a.reshape(1, a.shape[0])
    return a.reshape(-1, a.shape[-1])


def _row_tile(rows, cols, target_elems=1 << 17):
    tr = rows
    while tr % 16 == 0 and tr * cols > target_elems:
        tr //= 2
    return tr


def _sum_leading(name, a):
    n, rows, cols = a.shape
    tr = _row_tile(rows, cols * n, target_elems=1 << 20)

    def body(a_ref, o_ref):
        acc = a_ref[0].astype(F32)
        for i in range(1, n):
            acc = acc + a_ref[i].astype(F32)
        o_ref[...] = acc

    return pl.pallas_call(
        body, name=name, grid=(rows // tr,),
        in_specs=[pl.BlockSpec((n, tr, cols), lambda i: (0, i, 0))],
        out_specs=pl.BlockSpec((tr, cols), lambda i: (i, 0)),
        out_shape=jax.ShapeDtypeStruct((rows, cols), F32),
        compiler_params=_params(("parallel",)),
    )(a)


def _silu_rows(name, x):
    def body(x_ref, o_ref):
        v = x_ref[...]
        o_ref[...] = v * jax.nn.sigmoid(v)

    return pl.pallas_call(body, name=name, out_shape=jax.ShapeDtypeStruct(x.shape, F32))(x)


def _silu_bwd_rows(name, x, dy):
    def body(x_ref, dy_ref, o_ref):
        v = x_ref[...]
        sg = jax.nn.sigmoid(v)
        o_ref[...] = dy_ref[...] * (sg * (1.0 + v * (1.0 - sg)))

    return pl.pallas_call(body, name=name, out_shape=jax.ShapeDtypeStruct(x.shape, F32))(x, dy)


def _adamw(name, w, g_parts, m, v):
    shape = w.shape
    w2, m2, v2 = _rows_view(w), _rows_view(m), _rows_view(v)
    gs = [_rows_view(g) for g in g_parts]
    rows, cols = w2.shape
    tr = _row_tile(rows, cols)
    ng = len(gs)

    def body(*refs):
        w_ref, m_ref, v_ref = refs[0], refs[1], refs[2]
        g_refs = refs[3:3 + ng]
        g_out, d_out, m_out, v_out = refs[3 + ng:]
        g = g_refs[0][...]
        for r in g_refs[1:]:
            g = g + r[...]
        m1 = ADAM_B1 * m_ref[...] + (1.0 - ADAM_B1) * g
        v1 = ADAM_B2 * v_ref[...] + (1.0 - ADAM_B2) * jnp.square(g)
        m_hat = m1 / (1.0 - ADAM_B1 ** ADAM_STEP)
        v_hat = v1 / (1.0 - ADAM_B2 ** ADAM_STEP)
        g_out[...] = g
        d_out[...] = -ADAM_LR * (m_hat / (jnp.sqrt(v_hat) + ADAM_EPS) + ADAM_WD * w_ref[...])
        m_out[...] = m1
        v_out[...] = v1

    spec = pl.BlockSpec((tr, cols), lambda i: (i, 0))
    res = pl.pallas_call(
        body, name=name, grid=(rows // tr,), in_specs=[spec] * (3 + ng), out_specs=[spec] * 4,
        out_shape=[jax.ShapeDtypeStruct((rows, cols), F32)] * 4,
        compiler_params=_params(("parallel",)),
    )(w2, m2, v2, *gs)
    return tuple(r.reshape(shape) for r in res)


MOD_ROWS = 48


def kernel(x, c, ctx, c_ctx, mod_w, mod_b, norm1_g, norm2_g, ab_w_in, ab_w_out, ret_decay, att_q_norm_g, att_k_norm_g, cm_w_in, cm_v_norm_g, cm_w_s, cm_b_s, cm_w_out, ff_w1, ff_w2, loss_target, m_c_ctx, m_mod_w, m_mod_b, m_norm1_g, m_norm2_g, m_ab_w_in, m_ab_w_out, m_ret_decay, m_att_q_norm_g, m_att_k_norm_g, m_cm_w_in, m_cm_v_norm_g, m_cm_w_s, m_cm_b_s, m_cm_w_out, m_ff_w1, m_ff_w2, v_c_ctx, v_mod_w, v_mod_b, v_norm1_g, v_norm2_g, v_ab_w_in, v_ab_w_out, v_ret_decay, v_att_q_norm_g, v_att_k_norm_g, v_cm_w_in, v_cm_v_norm_g, v_cm_w_s, v_cm_b_s, v_cm_w_out, v_ff_w1, v_ff_w2):
    B, SL, D = x.shape
    cfg = Cfg(B=B, SC=ctx.shape[1], SL=SL, D=D, FF=ff_w1.shape[2] * N_CHIP)
    L = N_LAYERS
    n_ex = B * N_DEV
    mcols = mod_w.shape[2]
    weights = dict(c_ctx=c_ctx, mod_w=mod_w, mod_b=mod_b, norm1_g=norm1_g, norm2_g=norm2_g, ab_w_in=ab_w_in,
                   ab_w_out=ab_w_out, ret_decay=ret_decay, att_q_norm_g=att_q_norm_g, att_k_norm_g=att_k_norm_g,
                   cm_w_in=cm_w_in, cm_v_norm_g=cm_v_norm_g, cm_w_s=cm_w_s, cm_b_s=cm_b_s, cm_w_out=cm_w_out,
                   ff_w1=ff_w1, ff_w2=ff_w2)
    m_in = dict(c_ctx=m_c_ctx, mod_w=m_mod_w, mod_b=m_mod_b, norm1_g=m_norm1_g, norm2_g=m_norm2_g, ab_w_in=m_ab_w_in,
                ab_w_out=m_ab_w_out, ret_decay=m_ret_decay, att_q_norm_g=m_att_q_norm_g, att_k_norm_g=m_att_k_norm_g,
                cm_w_in=m_cm_w_in, cm_v_norm_g=m_cm_v_norm_g, cm_w_s=m_cm_w_s, cm_b_s=m_cm_b_s, cm_w_out=m_cm_w_out,
                ff_w1=m_ff_w1, ff_w2=m_ff_w2)
    v_in = dict(c_ctx=v_c_ctx, mod_w=v_mod_w, mod_b=v_mod_b, norm1_g=v_norm1_g, norm2_g=v_norm2_g, ab_w_in=v_ab_w_in,
                ab_w_out=v_ab_w_out, ret_decay=v_ret_decay, att_q_norm_g=v_att_q_norm_g, att_k_norm_g=v_att_k_norm_g,
                cm_w_in=v_cm_w_in, cm_v_norm_g=v_cm_v_norm_g, cm_w_s=v_cm_w_s, cm_b_s=v_cm_b_s, cm_w_out=v_cm_w_out,
                ff_w1=v_ff_w1, ff_w2=v_ff_w2)
    xi, yi, ci = lax.axis_index("x"), lax.axis_index("y"), lax.axis_index("c")
    chip = 2 * xi + yi
    dev = 2 * chip + ci

    shards = {n: [weights[n][i].astype(BF16) for i in range(weights[n].shape[0])] for n in BIG}
    vgw = cm_v_norm_g.shape[1]
    blk = jnp.zeros((8, D), F32).at[:B].set(c).at[B:B + 2, :vgw].set(cm_v_norm_g)
    g0 = _allgather8("gather_c", blk).reshape(N_DEV, 8, D)
    c_all = g0[:, :B].reshape(n_ex, D)
    vg_full = jnp.concatenate([g0[2 * s, B:B + 2, :vgw] for s in range(N_CHIP)], axis=-1)

    pre = jnp.zeros((MOD_ROWS, D), F32).at[:n_ex].set(c_all).at[n_ex].set(c_ctx)
    act = _silu_rows("silu_c", pre)
    mpart = jnp.stack([_mm(f"mod_fwd_{l}", act, mod_w, mode="nn", layer=l, outs=[F32], tn=mcols) for l in range(L)])
    g1 = _allgather8("gather_mod", mpart.reshape(L * MOD_ROWS, mcols)).reshape(N_DEV, L, MOD_ROWS, mcols)
    mod_all = jnp.concatenate([g1[2 * s] for s in range(N_CHIP)], axis=-1) + mod_b[:, None, :]
    mod_lat = lax.dynamic_slice_in_dim(mod_all, dev * B, B, axis=1)
    mod_ctx = jnp.broadcast_to(mod_all[:, n_ex][:, None], mod_lat.shape)
    mods = jnp.stack([mod_ctx, mod_lat], axis=2).reshape(L, B, 2, 6, D)

    w = dict(norm1_g=norm1_g, norm2_g=norm2_g, ret_decay=ret_decay, att_q_norm_g=att_q_norm_g,
             att_k_norm_g=att_k_norm_g, cm_v_norm_g=vg_full, cm_w_s=cm_w_s, cm_b_s=cm_b_s)
    xcat = jnp.concatenate([ctx, x], axis=1).reshape(cfg.T, D)
    loss_local, dx_lat, recv, small, dmods = _local_step(cfg, xcat, loss_target.reshape(B * SL, D), mods, shards, w)
    loss = lax.psum(loss_local, ("x", "y", "c"))
    grad_x = dx_lat.reshape(B, SL, D)

    part = [jnp.stack([_sum_leading(f"sum_{n}_{i}", recv[(n, i)]) for i in range(weights[n].shape[0])]) for n in BIG]
    other = _swap_sibling(part)
    out = {}
    for n, p_mine, p_other in zip(BIG, part, other):
        out[n] = _adamw(f"adamw_{n}", weights[n], [p_mine, p_other], m_in[n], v_in[n])

    dmod = jnp.stack(dmods).reshape(L, B, 2, 6 * D)
    dmod_lat = dmod[:, :, 1]
    dmod_ctx = jnp.sum(dmod[:, :, 0], axis=1)
    d_ret = jnp.stack(small["ret_lg"]) * jax.nn.sigmoid(-ret_decay)
    summed = [dmod_ctx.reshape(-1), jnp.stack(small["norm1_g"]).reshape(-1), jnp.stack(small["norm2_g"]).reshape(-1),
              jnp.stack(small["cm_v_norm_g"]).reshape(-1), jnp.stack(small["cm_w_s"]).reshape(-1),
              jnp.stack(small["cm_b_s"]).reshape(-1), jnp.stack(small["att_q_norm_g"]).reshape(-1),
              jnp.stack(small["att_k_norm_g"]).reshape(-1), d_ret.reshape(-1)]
    sizes = [int(a.shape[0]) for a in summed]
    flat = jnp.concatenate(summed + [dmod_lat.reshape(-1)])
    n_sum = sum(sizes)
    n_sum_rows = -(-n_sum // D)
    lat_rows = (L * B * 6 * D) // D
    pack_rows = -(-(n_sum_rows + lat_rows) // 8) * 8
    packed = jnp.zeros((pack_rows * D,), F32).at[:n_sum].set(flat[:n_sum])
    packed = packed.at[n_sum_rows * D:(n_sum_rows + lat_rows) * D].set(flat[n_sum:]).reshape(pack_rows, D)
    g2 = _allgather8("gather_small", packed).reshape(N_DEV, pack_rows, D)
    tot = _sum_leading("sum_small", g2[:, :n_sum_rows]).reshape(-1)
    pieces, off = [], 0
    for sz in sizes:
        pieces.append(tot[off:off + sz])
        off += sz
    dmod_ctx_t, g_n1, g_n2, g_vg, g_ws, g_bs, g_qg, g_kg, g_rd = pieces
    dmod_ctx_t = dmod_ctx_t.reshape(L, 6 * D)
    dmod_lat_all = g2[:, n_sum_rows:n_sum_rows + lat_rows].reshape(N_DEV, L, B, 6 * D)
    dmod_rows = jnp.zeros((L, MOD_ROWS, 6 * D), F32)
    dmod_rows = dmod_rows.at[:, :n_ex].set(jnp.transpose(dmod_lat_all, (1, 0, 2, 3)).reshape(L, n_ex, 6 * D))
    dmod_rows = dmod_rows.at[:, n_ex].set(dmod_ctx_t)
    g_mod_b = _sum_leading("sum_mod_b", jnp.transpose(dmod_rows, (1, 0, 2)))
    dmod_mine = lax.dynamic_slice_in_dim(dmod_rows, chip * mcols, mcols, axis=2)
    g_mod_w = jnp.stack([_mm(f"mod_dw_{l}", act, dmod_mine[l], mode="tn", outs=[F32], tn=mcols) for l in range(L)])
    ctx8 = jnp.zeros((L, 8, mcols), F32).at[:, 0].set(dmod_mine[:, n_ex])
    dcc = [_mm(f"mod_dctx_{l}", ctx8[l], mod_w, mode="nt", layer=l, outs=[F32], tk=mcols) for l in range(L)]
    dcc = _sum_leading("sum_dctx_layers", jnp.stack(dcc))
    g3 = _allgather8("gather_dctx", dcc).reshape(N_DEV, 8, D)
    dcc_t = _sum_leading("sum_dctx_chips", g3[0::2])[0:1]
    g_c_ctx = _silu_bwd_rows("silu_bwd_cctx", c_ctx[None], dcc_t)[0]

    vg_mine = lax.dynamic_slice_in_dim(g_vg.reshape(2, -1), chip * vgw, vgw, axis=1)
    small_g = dict(c_ctx=g_c_ctx, mod_w=g_mod_w, mod_b=g_mod_b, norm1_g=g_n1.reshape(norm1_g.shape),
                   norm2_g=g_n2.reshape(norm2_g.shape), ret_decay=g_rd.reshape(ret_decay.shape),
                   att_q_norm_g=g_qg.reshape(att_q_norm_g.shape), att_k_norm_g=g_kg.reshape(att_k_norm_g.shape),
                   cm_v_norm_g=vg_mine, cm_w_s=g_ws.reshape(cm_w_s.shape), cm_b_s=g_bs.reshape(cm_b_s.shape))
    for n, g in small_g.items():
        out[n] = _adamw(f"adamw_{n}", weights[n], [g], m_in[n], v_in[n])

    order = list(weights)
    return (loss, grad_x, *[out[n][0] for n in order], *[out[n][1] for n in order],
            *[out[n][2] for n in order], *[out[n][3] for n in order])
```

```python
import functools
import math
from typing import NamedTuple

import jax
import jax.numpy as jnp
from jax import lax
from jax.experimental import pallas as pl
from jax.experimental.pallas import tpu as pltpu

F32 = jnp.float32
BF16 = jnp.bfloat16
EPS = 1e-6
ROPE_BASE = 10000.0
LANES = 128
CHUNK = 128
N_LAYERS = 4
VMEM_LIMIT = 56 * 1024 * 1024

ADAM_LR = 0.001
ADAM_B1 = 0.9
ADAM_B2 = 0.999
ADAM_EPS = 1e-08
ADAM_WD = 0.01
ADAM_STEP = 10


class Cfg(NamedTuple):
    B: int = 4
    SC: int = 256
    SL: int = 2048
    D: int = 1024
    FF: int = 4096
    GRID_W: int = 64
    H: int = 4
    KV: int = 2
    CMW: int = 1024
    CMG: int = 8

    @property
    def S(self):
        return self.SC + self.SL

    @property
    def T(self):
        return self.B * self.S

    @property
    def TM(self):
        return self.SC

    @property
    def TPE(self):
        return self.S // self.SC

    @property
    def ABW(self):
        return (5 * self.H + 2 * self.KV) * CHUNK


def _tile(dim, pref):
    t = min(dim, pref)
    while dim % t:
        t -= LANES
    return t


def _dot(a, b):
    return lax.dot_general(a, b, (((1,), (0,)), ((), ())), preferred_element_type=F32)


def _dot_nt(a, b):
    return lax.dot_general(a, b, (((1,), (1,)), ((), ())), preferred_element_type=F32)


def _dot_tn(a, b):
    return lax.dot_general(a, b, (((0,), (0,)), ((), ())), preferred_element_type=F32)


def _params(sem, vmem=VMEM_LIMIT):
    return pltpu.CompilerParams(dimension_semantics=sem, vmem_limit_bytes=vmem)


def _mod_index(cfg):
    tpe = cfg.TPE
    return lambda i: (i // tpe, jnp.minimum(i % tpe, 1), 0, 0)


def _mm(name, a, b, *, mode, outs, tm=1024, tn=1024, tk=1024, layer=None, epi=None, extras=(), carry=None):
    bshape = b.shape[1:] if layer is not None else b.shape
    if mode == "nn":
        (M, K), N = a.shape, bshape[1]
    elif mode == "nt":
        (M, K), N = a.shape, bshape[0]
    else:
        (K, M), N = a.shape, bshape[1]
    tm, tn, tk = _tile(M, tm), _tile(N, tn), _tile(K, tk)
    nk = K // tk
    a_spec = (pl.BlockSpec((tk, tm), lambda i, j, k: (k, i)) if mode == "tn"
              else pl.BlockSpec((tm, tk), lambda i, j, k: (i, k)))
    if mode == "nt":
        bblk, bidx = (tn, tk), (lambda i, j, k: (j, k))
    else:
        bblk, bidx = (tk, tn), (lambda i, j, k: (k, j))
    if layer is not None:
        b_spec = pl.BlockSpec((None,) + bblk, lambda i, j, k: (layer,) + bidx(i, j, k))
    else:
        b_spec = pl.BlockSpec(bblk, bidx)
    ne, no = len(extras), len(outs)
    nc = len(carry.srcs) if carry is not None else 0
    dot = {"nn": _dot, "nt": _dot_nt, "tn": _dot_tn}[mode]
    grid = (M // tm, N // tn, nk)

    def body(*refs):
        a_ref, b_ref = refs[0], refs[1]
        ex, out_refs = refs[2:2 + ne], refs[2 + ne + nc:2 + ne + nc + no]
        row_tile = pl.program_id(0)

        if nc:
            step = (pl.program_id(0) * grid[1] + pl.program_id(1)) * grid[2] + pl.program_id(2)
            c_src = refs[2 + ne:2 + ne + nc]
            c_dst = refs[2 + ne + nc + no:2 + ne + 2 * nc + no]
            sems = refs[2 + ne + 2 * nc + no:2 + ne + 2 * nc + no + 3]

            @pl.when(step == 0)
            def _():
                for cp in _carry_copies(carry, c_src, c_dst, *sems):
                    cp.start()

        def finish(acc):
            res = epi(acc, row_tile, *ex) if epi is not None else (acc,)
            for r, o in zip(res, out_refs):
                o[...] = r.astype(o.dtype)

        part = dot(a_ref[...].astype(BF16), b_ref[...].astype(BF16))
        if nk == 1:
            finish(part)
        else:
            acc_ref = refs[-1]
            k = pl.program_id(2)

            @pl.when(k == 0)
            def _():
                acc_ref[...] = part

            @pl.when(k > 0)
            def _():
                acc_ref[...] += part

            @pl.when(k == nk - 1)
            def _():
                finish(acc_ref[...])

        if nc:
            @pl.when(step == grid[0] * grid[1] * grid[2] - 1)
            def _():
                for cp in _carry_copies(carry, c_src, c_dst, *sems):
                    cp.wait()

    scratch = [pltpu.SemaphoreType.DMA((3 * nc,)), pltpu.SemaphoreType.DMA((3 * nc,)),
               pltpu.SemaphoreType.DMA((nc,))] if nc else []
    if nk > 1:
        scratch.append(pltpu.VMEM((tm, tn), F32))
    res = pl.pallas_call(
        body, name=name, grid=grid,
        in_specs=[a_spec, b_spec] + [s for _, s in extras] + [ANY] * nc,
        out_specs=[pl.BlockSpec((tm, tn), lambda i, j, k: (i, j)) for _ in outs] + [ANY] * nc,
        out_shape=[jax.ShapeDtypeStruct((M, N), d) for d in outs] + (_carry_out_shapes(carry) if nc else []),
        scratch_shapes=scratch,
        compiler_params=_params(("arbitrary",) * 3 if nc else ("parallel", "parallel", "arbitrary")),
    )(a, b, *[x for x, _ in extras], *(carry.srcs if nc else ()))
    if nc:
        return (res[0] if no == 1 else res[:no]), res[no:]
    return res[0] if no == 1 else res


def _ff_bwd(name, first, second, weight, kind, carry=None):
    (T, D), FF = first.shape, second.shape[1]
    halves = 2
    tm = _tile(T, 1024)
    ffh = FF // halves if kind == "w2" else FF
    dh_cols = D if kind == "w2" else D // halves
    cw = _tile(ffh, 1024)
    n_steps = T // tm
    nc = len(carry.srcs) if carry is not None else 0

    def body(*refs):
        a_ref, b_ref, w_ref = refs[:3]
        c_src = refs[3:3 + nc]
        x_ref, dw_ref = refs[3 + nc:5 + nc]
        c_dst, sems = refs[5 + nc:5 + 2 * nc], refs[5 + 2 * nc:5 + 2 * nc + 3] if nc else ()
        acc_ref = refs[-1]
        i = pl.program_id(1)
        step = pl.program_id(0) * n_steps + i
        if nc:
            _carry_begin(carry, c_src, c_dst, sems, step)
        a = a_ref[...]
        dh = None
        for c in range(ffh // cw):
            cols = pl.ds(c * cw, cw)
            if kind == "w2":
                r = b_ref[:, cols]
                x_ref[:, cols] = (_dot_nt(a, w_ref[cols, :]) * (2.0 * jnp.sqrt(r.astype(F32)))).astype(BF16)
                part, dst = _dot_tn(r, a), acc_ref.at[cols, :]
            else:
                da = b_ref[:, cols]
                term = _dot_nt(da, w_ref[:, cols])
                dh = term if dh is None else dh + term
                part, dst = _dot_tn(a, da), acc_ref.at[:, cols]

            @pl.when(i == 0)
            def _():
                dst[...] = part

            @pl.when(i > 0)
            def _():
                dst[...] += part

        if kind == "w1":
            x_ref[...] = dh.astype(BF16)

        @pl.when(i == n_steps - 1)
        def _():
            dw_ref[...] = acc_ref[...].astype(BF16)

        if nc:
            _carry_end(carry, c_src, c_dst, sems, step, halves * n_steps)

    if kind == "w2":
        wshape = (ffh, D)
        a_spec = pl.BlockSpec((tm, D), lambda j, i: (i, 0))
        b_spec = pl.BlockSpec((tm, ffh), lambda j, i: (i, j))
        x_spec, x_cols = pl.BlockSpec((tm, ffh), lambda j, i: (i, j)), FF
    else:
        wshape = (dh_cols, FF)
        a_spec = pl.BlockSpec((tm, dh_cols), lambda j, i: (i, j))
        b_spec = pl.BlockSpec((tm, FF), lambda j, i: (i, 0))
        x_spec, x_cols = pl.BlockSpec((tm, dh_cols), lambda j, i: (i, j)), D
    wspec = pl.BlockSpec(wshape, lambda j, i: (j, 0), pipeline_mode=pl.Buffered(1))
    res = pl.pallas_call(
        body, name=name, grid=(halves, n_steps),
        in_specs=[a_spec, b_spec, wspec] + [ANY] * nc,
        out_specs=[x_spec, wspec] + [ANY] * nc,
        out_shape=[jax.ShapeDtypeStruct((T, x_cols), BF16), jax.ShapeDtypeStruct(weight.shape, BF16)]
        + (_carry_out_shapes(carry) if nc else []),
        scratch_shapes=_carry_scratch(nc) + [pltpu.VMEM(wshape, F32)],
        compiler_params=_params(("arbitrary", "arbitrary")),
    )(first, second, weight, *(carry.srcs if nc else ()))
    return res[:2], res[2:]


def _norm_mod_fwd(cfg, name, x, gain, mod, ish, isc):
    T, D, TM = cfg.T, cfg.D, cfg.TM

    def body(x_ref, g_ref, mod_ref, h_ref):
        x = x_ref[...]
        rstd = lax.rsqrt(jnp.mean(x * x, axis=-1, keepdims=True) + EPS)
        n = x * rstd * g_ref[...]
        h = n * (1.0 + mod_ref[pl.ds(isc, 1), :]) + mod_ref[pl.ds(ish, 1), :]
        h_ref[...] = h.astype(BF16)

    return pl.pallas_call(
        body, name=name, grid=(T // TM,),
        in_specs=[pl.BlockSpec((TM, D), lambda i: (i, 0)), pl.BlockSpec((1, D), lambda i: (0, 0)),
                  pl.BlockSpec((None, None, 6, D), _mod_index(cfg))],
        out_specs=pl.BlockSpec((TM, D), lambda i: (i, 0)),
        out_shape=jax.ShapeDtypeStruct((T, D), BF16),
        compiler_params=_params(("parallel",)),
    )(x, gain, mod)


def _norm_mod_bwd(cfg, name, x, gain, mod, ish, isc, dh, dres, gate=None, lat_only=False):
    T, D, TM, TPE = cfg.T, cfg.D, cfg.TM, cfg.TPE
    ng = 2 if gate is not None else 0
    dx_spec = (pl.BlockSpec((TM, D), lambda i: ((i // TPE) * (TPE - 1) + jnp.maximum(i % TPE - 1, 0), 0)) if lat_only
               else pl.BlockSpec((TM, D), lambda i: (i, 0)))
    dx_rows = cfg.B * cfg.SL if lat_only else T

    def body(*refs):
        x_ref, g_ref, mod_ref, dh_ref, dres_ref = refs[:5]
        dx_ref, dmod_ref, dgain_ref = refs[5 + ng:8 + ng]
        i = pl.program_id(0)
        t = i % TPE
        x = x_ref[...]
        g = g_ref[...]
        dh = dh_ref[...].astype(F32)
        rstd = lax.rsqrt(jnp.mean(x * x, axis=-1, keepdims=True) + EPS)
        xhat = x * rstd
        dn = dh * (1.0 + mod_ref[pl.ds(isc, 1), :])
        dsh = jnp.sum(dh, axis=0, keepdims=True)
        dsc = jnp.sum(dh * (xhat * g), axis=0, keepdims=True)
        dgain = jnp.sum(dn * xhat, axis=0, keepdims=True)
        dxh = dn * g
        dx = rstd * (dxh - xhat * jnp.mean(dxh * xhat, axis=-1, keepdims=True)) + dres_ref[...]
        dx_ref[...] = dx
        sums = [(dmod_ref.at[pl.ds(0, 1), :], dsh), (dmod_ref.at[pl.ds(1, 1), :], dsc)]
        if ng:
            y_ref, gmod_ref = refs[5:7]
            dy_ref, dgate_ref = refs[8 + ng:]
            dy_ref[...] = (dx * gmod_ref[pl.ds(gate[2], 1), :]).astype(BF16)
            sums.append((dgate_ref, jnp.sum(dx * y_ref[...].astype(F32), axis=0, keepdims=True)))

        @pl.when(t <= 1)
        def _():
            for ref, val in sums:
                ref[...] = val

        @pl.when(t > 1)
        def _():
            for ref, val in sums:
                ref[...] += val

        @pl.when(i == 0)
        def _():
            dgain_ref[...] = dgain

        @pl.when(i > 0)
        def _():
            dgain_ref[...] += dgain

    tok = pl.BlockSpec((TM, D), lambda i: (i, 0))
    mod_spec = pl.BlockSpec((None, None, 6, D), _mod_index(cfg))
    res = pl.pallas_call(
        body, name=name, grid=(T // TM,),
        in_specs=[tok, pl.BlockSpec((1, D), lambda i: (0, 0)), mod_spec, tok, tok] + ([tok, mod_spec] if ng else []),
        out_specs=[dx_spec, pl.BlockSpec((None, None, 2, D), _mod_index(cfg)), pl.BlockSpec((1, D), lambda i: (0, 0))]
        + ([tok, pl.BlockSpec((None, None, 1, D), _mod_index(cfg))] if ng else []),
        out_shape=[jax.ShapeDtypeStruct((dx_rows, D), F32), jax.ShapeDtypeStruct((cfg.B, 2, 2, D), F32),
                   jax.ShapeDtypeStruct((1, D), F32)]
        + ([jax.ShapeDtypeStruct((T, D), BF16), jax.ShapeDtypeStruct((cfg.B, 2, 1, D), F32)] if ng else []),
        compiler_params=_params(("arbitrary",)),
    )(x, gain, mod, dh, dres, *(gate[:2] if ng else ()))
    return res


def _gate_bwd(cfg, name, dx, y, mod, igate):
    T, D, TM, TPE = cfg.T, cfg.D, cfg.TM, cfg.TPE

    def body(dx_ref, y_ref, mod_ref, dy_ref, dg_ref):
        t = pl.program_id(0) % TPE
        dx = dx_ref[...]
        dy_ref[...] = (dx * mod_ref[pl.ds(igate, 1), :]).astype(BF16)
        dg = jnp.sum(dx * y_ref[...].astype(F32), axis=0, keepdims=True)

        @pl.when(t <= 1)
        def _():
            dg_ref[...] = dg

        @pl.when(t > 1)
        def _():
            dg_ref[...] += dg

    tok = pl.BlockSpec((TM, D), lambda i: (i, 0))
    return pl.pallas_call(
        body, name=name, grid=(T // TM,),
        in_specs=[tok, tok, pl.BlockSpec((None, None, 6, D), _mod_index(cfg))],
        out_specs=[tok, pl.BlockSpec((None, None, 1, D), _mod_index(cfg))],
        out_shape=[jax.ShapeDtypeStruct((T, D), BF16), jax.ShapeDtypeStruct((cfg.B, 2, 1, D), F32)],
        compiler_params=_params(("arbitrary",)),
    )(dx, y, mod)


def _loss_grad(cfg, x, tgt):
    T, D, TM, TPE = cfg.T, cfg.D, cfg.TM, cfg.TPE

    def body(x_ref, t_ref, dx_ref, loss_ref):
        i = pl.program_id(0)
        t = i % TPE

        @pl.when(i == 0)
        def _():
            loss_ref[...] = jnp.zeros_like(loss_ref)

        @pl.when(t == 0)
        def _():
            dx_ref[...] = jnp.zeros_like(dx_ref)

        @pl.when(t > 0)
        def _():
            err = x_ref[...] - t_ref[...]
            dx_ref[...] = err * (1.0 / D)
            loss_ref[...] += 0.5 * jnp.sum(jnp.mean(err * err, axis=-1, keepdims=True), axis=0, keepdims=True)

    tok = pl.BlockSpec((TM, D), lambda i: (i, 0))
    tgt_spec = pl.BlockSpec((TM, D), lambda i: ((i // TPE) * (TPE - 1) + jnp.maximum(i % TPE - 1, 0), 0))
    dx, loss = pl.pallas_call(
        body, name="loss_grad", grid=(T // TM,),
        in_specs=[tok, tgt_spec], out_specs=[tok, pl.BlockSpec((8, LANES), lambda i: (0, 0))],
        out_shape=[jax.ShapeDtypeStruct((T, D), F32), jax.ShapeDtypeStruct((8, LANES), F32)],
        compiler_params=_params(("arbitrary",)),
    )(x, tgt)
    return loss[0, 0], dx


def _rope_tables(cfg):
    rows = cfg.SL // cfg.GRID_W
    row = jnp.repeat(jnp.arange(rows, dtype=F32), cfg.GRID_W)
    col = jnp.tile(jnp.arange(cfg.GRID_W, dtype=F32), rows)
    n_freq = CHUNK // 4
    inv = ROPE_BASE ** (-jnp.arange(n_freq, dtype=F32) / n_freq)
    ang = jnp.concatenate([row[:, None] * inv[None, :], col[:, None] * inv[None, :]], axis=-1)
    cos, sin = jnp.cos(ang), jnp.sin(ang)
    cosf = jnp.concatenate([jnp.ones((cfg.SC, CHUNK), F32), jnp.concatenate([cos, cos], axis=-1)], axis=0)
    sinf = jnp.concatenate([jnp.zeros((cfg.SC, CHUNK), F32), jnp.concatenate([-sin, sin], axis=-1)], axis=0)
    return cosf, sinf


def _rope(x, cosf, sinf):
    return x * cosf + pltpu.roll(x, CHUNK // 2, 1) * sinf


def _irope(dy, cosf, sinf):
    return dy * cosf - pltpu.roll(dy, CHUNK // 2, 1) * sinf


def _prep_fwd(cfg, name, p, cosf, sinf, qg, kg):
    T, TM, TPE, H, KV = cfg.T, cfg.TM, cfg.TPE, cfg.H, cfg.KV
    HW = H * CHUNK
    kscale = CHUNK ** -0.5

    def body(p_ref, c_ref, s_ref, qg_ref, kg_ref, rq_ref, rk_ref, aq_ref, ak_ref):
        cosf, sinf = c_ref[...], s_ref[...]

        def normed(x, g):
            return x * lax.rsqrt(jnp.mean(x * x, axis=-1, keepdims=True) + EPS) * g

        def seg(col):
            return p_ref[:, pl.ds(col, CHUNK)].astype(F32)

        for h in range(H):
            sl = pl.ds(h * CHUNK, CHUNK)
            rq_ref[:, sl] = _rope(seg(h * CHUNK), cosf, sinf)
            rk_ref[:, sl] = _rope(seg(HW + h * CHUNK), cosf, sinf) * kscale
            aq_ref[:, sl] = (_rope(normed(seg(4 * HW + h * CHUNK), qg_ref[...]), cosf, sinf) * ATT_SCALE).astype(BF16)
        for h in range(KV):
            ak_ref[:, pl.ds(h * CHUNK, CHUNK)] = _rope(
                normed(seg(5 * HW + h * CHUNK), kg_ref[...]), cosf, sinf).astype(BF16)

    tab = pl.BlockSpec((TM, CHUNK), lambda i: (i % TPE, 0))
    vec = pl.BlockSpec((1, CHUNK), lambda i: (0, 0))
    return pl.pallas_call(
        body, name=name, grid=(T // TM,),
        in_specs=[pl.BlockSpec((TM, cfg.ABW), lambda i: (i, 0)), tab, tab, vec, vec],
        out_specs=[pl.BlockSpec((TM, HW), lambda i: (i, 0))] * 3 + [pl.BlockSpec((TM, KV * CHUNK), lambda i: (i, 0))],
        out_shape=[jax.ShapeDtypeStruct((T, HW), F32), jax.ShapeDtypeStruct((T, HW), F32),
                   jax.ShapeDtypeStruct((T, HW), BF16), jax.ShapeDtypeStruct((T, KV * CHUNK), BF16)],
        compiler_params=_params(("parallel",)),
    )(p, cosf, sinf, qg, kg)


def _prep_bwd(cfg, name, p, cosf, sinf, qg, kg, d_rq, d_rk, d_rv, d_gate, d_aq, d_ak, d_av):
    T, TM, TPE, H, KV = cfg.T, cfg.TM, cfg.TPE, cfg.H, cfg.KV
    HW = H * CHUNK
    kscale = CHUNK ** -0.5

    def body(p_ref, c_ref, s_ref, qg_ref, kg_ref, drq_ref, drk_ref, drv_ref, dgt_ref, daq_ref, dak_ref, dav_ref,
             dp_ref, dqg_ref, dkg_ref):
        i = pl.program_id(0)
        cosf, sinf = c_ref[...], s_ref[...]

        def norm_bwd(x, g, dn):
            rstd = lax.rsqrt(jnp.mean(x * x, axis=-1, keepdims=True) + EPS)
            xhat = x * rstd
            dg = jnp.sum(dn * xhat, axis=0, keepdims=True)
            dxh = dn * g
            return rstd * (dxh - xhat * jnp.mean(dxh * xhat, axis=-1, keepdims=True)), dg

        dqg = jnp.zeros((1, CHUNK), F32)
        dkg = jnp.zeros((1, CHUNK), F32)
        for h in range(H):
            sl = pl.ds(h * CHUNK, CHUNK)
            dp_ref[:, pl.ds(h * CHUNK, CHUNK)] = _irope(drq_ref[:, sl].astype(F32), cosf, sinf).astype(BF16)
            dp_ref[:, pl.ds(HW + h * CHUNK, CHUNK)] = (_irope(drk_ref[:, sl].astype(F32), cosf, sinf)
                                                       * kscale).astype(BF16)
            dp_ref[:, pl.ds(2 * HW + h * CHUNK, CHUNK)] = drv_ref[:, sl].astype(BF16)
            dp_ref[:, pl.ds(3 * HW + h * CHUNK, CHUNK)] = dgt_ref[:, sl].astype(BF16)
            dx, dg = norm_bwd(p_ref[:, pl.ds(4 * HW + h * CHUNK, CHUNK)].astype(F32), qg_ref[...],
                              _irope(daq_ref[:, sl].astype(F32), cosf, sinf))
            dp_ref[:, pl.ds(4 * HW + h * CHUNK, CHUNK)] = dx.astype(BF16)
            dqg = dqg + dg
        for h in range(KV):
            sl = pl.ds(h * CHUNK, CHUNK)
            dx, dg = norm_bwd(p_ref[:, pl.ds(5 * HW + h * CHUNK, CHUNK)].astype(F32), kg_ref[...],
                              _irope(dak_ref[:, sl], cosf, sinf))
            dp_ref[:, pl.ds(5 * HW + h * CHUNK, CHUNK)] = dx.astype(BF16)
            dp_ref[:, pl.ds(5 * HW + (KV + h) * CHUNK, CHUNK)] = dav_ref[:, sl].astype(BF16)
            dkg = dkg + dg

        @pl.when(i == 0)
        def _():
            dqg_ref[...] = dqg
            dkg_ref[...] = dkg

        @pl.when(i > 0)
        def _():
            dqg_ref[...] += dqg
            dkg_ref[...] += dkg

    tab = pl.BlockSpec((TM, CHUNK), lambda i: (i % TPE, 0))
    vec = pl.BlockSpec((1, CHUNK), lambda i: (0, 0))
    hw = pl.BlockSpec((TM, HW), lambda i: (i, 0))
    kvw = pl.BlockSpec((TM, KV * CHUNK), lambda i: (i, 0))
    return pl.pallas_call(
        body, name=name, grid=(T // TM,),
        in_specs=[pl.BlockSpec((TM, cfg.ABW), lambda i: (i, 0)), tab, tab, vec, vec, hw, hw, hw, hw, hw, kvw, kvw],
        out_specs=[pl.BlockSpec((TM, cfg.ABW), lambda i: (i, 0)), vec, vec],
        out_shape=[jax.ShapeDtypeStruct((T, cfg.ABW), BF16), jax.ShapeDtypeStruct((1, CHUNK), F32),
                   jax.ShapeDtypeStruct((1, CHUNK), F32)],
        compiler_params=_params(("arbitrary",)),
    )(p, cosf, sinf, qg, kg, d_rq, d_rk, d_rv, d_gate, d_aq, d_ak, d_av)


def _ret_consts(direction, lg):
    C = CHUNK
    ii = lax.broadcasted_iota(jnp.int32, (C, C), 0)
    jj = lax.broadcasted_iota(jnp.int32, (C, C), 1)
    col = lax.broadcasted_iota(jnp.int32, (C, 1), 0).astype(F32)
    if direction == 0:
        mask, er, ek, eq = ii >= jj, (ii - jj).astype(F32), (C - 1.0) - col, col + 1.0
    else:
        mask, er, ek, eq = jj >= ii, (jj - ii).astype(F32), col, C - col
    er = jnp.where(mask, er, 0.0)
    dm = jnp.where(mask, jnp.exp(er * lg), 0.0)
    return dm, er, jnp.exp(ek * lg), ek, jnp.exp(eq * lg), eq, jnp.exp(C * lg)


def _ret_order(cfg, direction):
    n_all, n_ctx = cfg.S // CHUNK, cfg.SC // CHUNK
    if direction == 0:
        return list(range(n_all))
    return list(range(n_ctx - 1, -1, -1)) + list(range(n_all - 1, n_ctx - 1, -1))


def _carry_begin(carry, c_src, c_dst, sems, step):
    @pl.when(step == 0)
    def _():
        for cp in _carry_copies(carry, c_src, c_dst, *sems):
            cp.start()


def _carry_end(carry, c_src, c_dst, sems, step, n_steps):
    @pl.when(step == n_steps - 1)
    def _():
        for cp in _carry_copies(carry, c_src, c_dst, *sems):
            cp.wait()


def _carry_scratch(nc):
    return [pltpu.SemaphoreType.DMA((3 * nc,)), pltpu.SemaphoreType.DMA((3 * nc,)),
            pltpu.SemaphoreType.DMA((nc,))] if nc else []


def _head_norm_gate(o, g):
    mu = jnp.mean(o, axis=-1, keepdims=True)
    var = jnp.mean(jnp.square(o - mu), axis=-1, keepdims=True)
    rstd = lax.rsqrt(var + EPS)
    y = (o - mu) * rstd
    sg = jax.nn.sigmoid(g)
    return y, rstd, sg


RET_UNROLL = 3


def _retention_fwd(cfg, name, rq, rk, p, lgb, carry=None):
    B, H, S, T = cfg.B, cfg.H, cfg.S, cfg.T
    n_all = S // CHUNK
    nc = len(carry.srcs) if carry is not None else 0

    def body(*refs):
        q_ref, k_ref, v_ref, g_ref, lg_ref = refs[:5]
        c_src = refs[5:5 + nc]
        o_ref, ret_ref, st_ref = refs[5 + nc:8 + nc]
        c_dst = refs[8 + nc:8 + 2 * nc]
        sems = refs[8 + 2 * nc:8 + 2 * nc + 3] if nc else ()
        kv_ref = refs[-1]
        step = pl.program_id(0) * H + pl.program_id(1)
        if nc:
            _carry_begin(carry, c_src, c_dst, sems, step)

        def rows(n):
            return pl.ds(pl.multiple_of(n * CHUNK, CHUNK), CHUNK)

        (dm0, _, kd0, _, qd0, _, cd0), (dm1, _, kd1, _, qd1, _, cd1) = (
            _ret_consts(d, lg_ref[d, 0:1, 0:1]) for d in (0, 1))
        dm_both = dm0 + dm1

        def kv_step(n, c):
            k = k_ref[rows(n), :]
            v = v_ref[rows(n), :].astype(BF16)
            kv_ref[0, n] = _dot_tn((k * kd0).astype(BF16), v)
            kv_ref[1, n] = _dot_tn((k * kd1).astype(BF16), v)
            return c

        lax.fori_loop(0, n_all, kv_step, 0, unroll=RET_UNROLL)
        for direction, cd in ((0, cd0), (1, cd1)):
            st = jnp.zeros((CHUNK, CHUNK), F32)
            for t, n in enumerate(_ret_order(cfg, direction)):
                st_ref[direction, n] = st
                if t + 1 < n_all:
                    st = cd * st + kv_ref[direction, n]

        def out_step(n, c):
            q = q_ref[rows(n), :].astype(BF16)
            v = v_ref[rows(n), :].astype(BF16)
            s = _dot_nt(q, k_ref[rows(n), :].astype(BF16)) * dm_both
            states = jnp.concatenate([st_ref[0, n].astype(BF16), st_ref[1, n].astype(BF16)], axis=1)
            cross = _dot(q, states)
            o = _dot(s.astype(BF16), v) + cross[:, :CHUNK] * qd0 + cross[:, CHUNK:] * qd1
            o_ref[rows(n), :] = o
            g = g_ref[rows(n), :].astype(F32)
            y, _, sg = _head_norm_gate(o, g)
            ret_ref[rows(n), :] = (y * (g * sg)).astype(BF16)
            return c

        lax.fori_loop(0, n_all, out_step, 0, unroll=RET_UNROLL)
        if nc:
            _carry_end(carry, c_src, c_dst, sems, step, B * H)

    HW = H * CHUNK
    blk = lambda off: pl.BlockSpec((S, CHUNK), lambda b, h: (b, off + h))
    st_spec = pl.BlockSpec((None, None, 2, n_all, CHUNK, CHUNK), lambda b, h: (b, h, 0, 0, 0, 0))
    res = pl.pallas_call(
        body, name=name, grid=(B, H),
        in_specs=[blk(0), blk(0), blk(2 * H), blk(3 * H),
                  pl.BlockSpec((None, 2, 8, LANES), lambda b, h: (h, 0, 0, 0))] + [ANY] * nc,
        out_specs=[blk(0), blk(0), st_spec] + [ANY] * nc,
        out_shape=[jax.ShapeDtypeStruct((T, HW), F32), jax.ShapeDtypeStruct((T, 2 * HW), BF16),
                   jax.ShapeDtypeStruct((B, H, 2, n_all, CHUNK, CHUNK), F32)] + (_carry_out_shapes(carry) if nc else []),
        scratch_shapes=_carry_scratch(nc) + [pltpu.VMEM((2, n_all, CHUNK, CHUNK), F32)],
        compiler_params=_params(("arbitrary", "arbitrary") if nc else ("parallel", "parallel")),
    )(rq, rk, p, p, lgb, *(carry.srcs if nc else ()))
    return res[:3], res[3:]


def _retention_bwd(cfg, name, rq, rk, p, o_sum, states, dcat, lgb, carry=None):
    B, H, S, T = cfg.B, cfg.H, cfg.S, cfg.T
    n_all = S // CHUNK
    C = CHUNK
    nc = len(carry.srcs) if carry is not None else 0

    def body(*refs):
        q_ref, k_ref, v_ref, g_ref, o_ref, st_ref, dr_ref, lg_ref = refs[:8]
        c_src = refs[8:8 + nc]
        dq_ref, dk_ref, dv_ref, dg_ref, dlg_ref = refs[8 + nc:13 + nc]
        c_dst = refs[13 + nc:13 + 2 * nc]
        sems = refs[13 + 2 * nc:13 + 2 * nc + 3] if nc else ()
        do_ref, gq_ref, ds_ref, acc_ref = refs[-4:]
        step = pl.program_id(0) * H + pl.program_id(1)
        if nc:
            _carry_begin(carry, c_src, c_dst, sems, step)

        def rows(n):
            return pl.ds(pl.multiple_of(n * C, C), C)

        (dm0, er0, kd0, ek0, qd0, eq0, cd0), (dm1, er1, kd1, ek1, qd1, eq1, cd1) = (
            _ret_consts(d, lg_ref[d, 0:1, 0:1]) for d in (0, 1))
        dm_both = dm0 + dm1
        wdm0, wdm1 = dm0 * er0, dm1 * er1

        def side(a, b):
            return jnp.concatenate([a.astype(BF16), b.astype(BF16)], axis=1)

        def gq_step(n, c):
            g = g_ref[rows(n), :].astype(F32)
            dr = dr_ref[rows(n), :].astype(F32)
            y, rstd, sg = _head_norm_gate(o_ref[rows(n), :], g)
            dy = dr * (g * sg)
            dg_ref[rows(n), :] = (dr * y * (sg * (1.0 + g * (1.0 - sg)))).astype(BF16)
            do = rstd * (dy - jnp.mean(dy, axis=-1, keepdims=True) - y * jnp.mean(dy * y, axis=-1, keepdims=True))
            do_ref[rows(n), :] = do
            gq = _dot_tn(q_ref[rows(n), :].astype(BF16), side(do * qd0, do * qd1))
            gq_ref[0, n] = gq[:, :C]
            gq_ref[1, n] = gq[:, C:]
            return c

        lax.fori_loop(0, n_all, gq_step, 0, unroll=RET_UNROLL)
        for direction, cd in ((0, cd0), (1, cd1)):
            order = _ret_order(cfg, direction)
            ds = jnp.zeros((C, C), F32)
            for t in reversed(range(n_all)):
                ds_ref[direction, order[t]] = ds
                if t > 0:
                    ds = cd * ds + gq_ref[direction, order[t]]
        acc_ref[...] = jnp.zeros_like(acc_ref)

        def chunk_step(n, c):
            q = q_ref[rows(n), :].astype(BF16)
            kf = k_ref[rows(n), :]
            k = kf.astype(BF16)
            v = v_ref[rows(n), :].astype(BF16)
            do = do_ref[rows(n), :]
            dob = do.astype(BF16)
            sp0, sp1 = st_ref[0, n], st_ref[1, n]
            ds0, ds1 = ds_ref[0, n], ds_ref[1, n]
            states = side(sp0, sp1)
            dstates = jnp.concatenate([ds0.astype(BF16), ds1.astype(BF16)], axis=0)
            doq0, doq1 = do * qd0, do * qd1
            doq = side(doq0, doq1)
            s_raw = _dot_nt(q, k)
            dpm = _dot_nt(dob, v)
            dsr = (dpm * dm_both).astype(BF16)
            dks = _dot_nt(v, dstates)
            dks0, dks1 = dks[:, :C] * kd0, dks[:, C:] * kd1
            qs = _dot(q, states)
            dq_ref[rows(n), :] = (_dot(dsr, k) + _dot_nt(doq, states)).astype(BF16)
            dk_ref[rows(n), :] = (_dot_tn(dsr, q) + dks0 + dks1).astype(BF16)
            dv_ref[rows(n), :] = (_dot_tn((s_raw * dm_both).astype(BF16), dob)
                                  + _dot(side(kf * kd0, kf * kd1), dstates)).astype(BF16)
            inner = dpm * s_raw
            acc_ref[0] += (jnp.sum(inner * wdm0, axis=0, keepdims=True)
                           + jnp.sum(eq0 * doq0 * qs[:, :C], axis=0, keepdims=True)
                           + jnp.sum(ek0 * kf * dks0, axis=0, keepdims=True)
                           + (C * cd0) * jnp.sum(ds0 * sp0, axis=0, keepdims=True))
            acc_ref[1] += (jnp.sum(inner * wdm1, axis=0, keepdims=True)
                           + jnp.sum(eq1 * doq1 * qs[:, C:], axis=0, keepdims=True)
                           + jnp.sum(ek1 * kf * dks1, axis=0, keepdims=True)
                           + (C * cd1) * jnp.sum(ds1 * sp1, axis=0, keepdims=True))
            return c

        lax.fori_loop(0, n_all, chunk_step, 0, unroll=RET_UNROLL)
        for direction in (0, 1):
            dlg_ref[direction] = jnp.broadcast_to(jnp.sum(acc_ref[direction], axis=1, keepdims=True), (8, LANES))
        if nc:
            _carry_end(carry, c_src, c_dst, sems, step, B * H)

    HW = H * CHUNK
    blk = lambda off: pl.BlockSpec((S, CHUNK), lambda b, h: (b, off + h))
    st_spec = pl.BlockSpec((None, None, 2, n_all, C, C), lambda b, h: (b, h, 0, 0, 0, 0))
    res = pl.pallas_call(
        body, name=name, grid=(B, H),
        in_specs=[blk(0), blk(0), blk(2 * H), blk(3 * H), blk(0), st_spec, blk(0),
                  pl.BlockSpec((None, 2, 8, LANES), lambda b, h: (h, 0, 0, 0))] + [ANY] * nc,
        out_specs=[blk(0)] * 4 + [pl.BlockSpec((None, None, 2, 8, LANES), lambda b, h: (b, h, 0, 0, 0))] + [ANY] * nc,
        out_shape=[jax.ShapeDtypeStruct((T, HW), BF16)] * 4 + [jax.ShapeDtypeStruct((B, H, 2, 8, LANES), F32)]
        + (_carry_out_shapes(carry) if nc else []),
        scratch_shapes=_carry_scratch(nc) + [pltpu.VMEM((S, CHUNK), F32), pltpu.VMEM((2, n_all, C, C), F32),
                                             pltpu.VMEM((2, n_all, C, C), F32), pltpu.VMEM((2, 1, C), F32)],
        compiler_params=_params(("arbitrary", "arbitrary") if nc else ("parallel", "parallel")),
    )(rq, rk, p, p, o_sum, states, dcat, lgb, *(carry.srcs if nc else ()))
    return res[:5], res[5:]


ATT_SCALE = CHUNK ** -0.5


def _attn_scores(cfg, q, k, t):
    kcol = lax.broadcasted_iota(jnp.int32, (1, cfg.S), 1)
    bias = jnp.where(jnp.logical_or(t > 0, kcol < cfg.SC), 0.0, -1e30)
    return _dot_nt(q, k) + bias


def _attention_fwd(cfg, name, aq, ak, p, cat, carry=None):
    B, H, KV, S, T, TM, TPE = cfg.B, cfg.H, cfg.KV, cfg.S, cfg.T, cfg.TM, cfg.TPE
    G = H // KV
    v_off = (5 * H + KV)
    nc = len(carry.srcs) if carry is not None else 0

    def body(*refs):
        q_ref, k_ref, v_ref = refs[:3]
        o_ref, lse_ref = refs[4 + nc:6 + nc]
        c_src, c_dst, sems = refs[4:4 + nc], refs[6 + nc:6 + 2 * nc], refs[6 + 2 * nc:]
        step = (pl.program_id(0) * H + pl.program_id(1)) * TPE + pl.program_id(2)
        if nc:
            _carry_begin(carry, c_src, c_dst, sems, step)
        s = _attn_scores(cfg, q_ref[...], k_ref[...], pl.program_id(2))
        m = jnp.max(s, axis=-1, keepdims=True)
        e = jnp.exp(s - m)
        total = jnp.sum(e, axis=-1, keepdims=True)
        o_ref[...] = (_dot(e.astype(BF16), v_ref[...].astype(BF16)) * (1.0 / total)).astype(BF16)
        lse_ref[...] = m + jnp.log(total)
        if nc:
            _carry_end(carry, c_src, c_dst, sems, step, B * H * TPE)

    res = pl.pallas_call(
        body, name=name, grid=(B, H, TPE),
        in_specs=[pl.BlockSpec((TM, CHUNK), lambda b, h, t: (b * TPE + t, h)),
                  pl.BlockSpec((S, CHUNK), lambda b, h, t: (b, h // G)),
                  pl.BlockSpec((S, CHUNK), lambda b, h, t: (b, v_off + h // G)), ANY] + [ANY] * nc,
        out_specs=[pl.BlockSpec((TM, CHUNK), lambda b, h, t: (b * TPE + t, H + h)),
                   pl.BlockSpec((None, TM, 1), lambda b, h, t: (h, b * TPE + t, 0))] + [ANY] * nc,
        out_shape=[jax.ShapeDtypeStruct(cat.shape, cat.dtype), jax.ShapeDtypeStruct((H, T, 1), F32)]
        + (_carry_out_shapes(carry) if nc else []),
        input_output_aliases={3: 0},
        scratch_shapes=_carry_scratch(nc),
        compiler_params=_params(("arbitrary",) * 3 if nc else ("parallel",) * 3),
    )(aq, ak, p, cat, *(carry.srcs if nc else ()))
    return res[:2], res[2:]


def _attention_bwd(cfg, name, aq, ak, p, cat, lse, dcat, carry=None):
    B, H, KV, S, T, TM, TPE = cfg.B, cfg.H, cfg.KV, cfg.S, cfg.T, cfg.TM, cfg.TPE
    G = H // KV
    v_off = (5 * H + KV)
    nc = len(carry.srcs) if carry is not None else 0

    def body(*refs):
        q_ref, k_ref, v_ref, o_ref, lse_ref, do_ref = refs[:6]
        dq_ref, dk_ref, dv_ref = refs[6 + nc:9 + nc]
        c_src, c_dst, sems = refs[6:6 + nc], refs[9 + nc:9 + 2 * nc], refs[9 + 2 * nc:]
        g, t = pl.program_id(2), pl.program_id(3)
        step = ((pl.program_id(0) * KV + pl.program_id(1)) * G + g) * TPE + t
        if nc:
            _carry_begin(carry, c_src, c_dst, sems, step)
        q, k = q_ref[...], k_ref[...]
        v = v_ref[...].astype(BF16)
        do = do_ref[...]
        dob = do.astype(BF16)
        pr = jnp.exp(_attn_scores(cfg, q, k, t) - lse_ref[...])
        delta = jnp.sum(do.astype(F32) * o_ref[...].astype(F32), axis=-1, keepdims=True)
        ds = (pr * (_dot_nt(dob, v) - delta)).astype(BF16)
        dq_ref[...] = (_dot(ds, k) * ATT_SCALE).astype(BF16)
        dk = _dot_tn(ds, q)
        dv = _dot_tn(pr.astype(BF16), dob)
        first = jnp.logical_and(g == 0, t == 0)

        @pl.when(first)
        def _():
            dk_ref[...] = dk
            dv_ref[...] = dv

        @pl.when(jnp.logical_not(first))
        def _():
            dk_ref[...] += dk
            dv_ref[...] += dv

        if nc:
            _carry_end(carry, c_src, c_dst, sems, step, B * KV * G * TPE)

    qspec = pl.BlockSpec((TM, CHUNK), lambda b, kv, g, t: (b * TPE + t, kv * G + g))
    kvspec = pl.BlockSpec((S, CHUNK), lambda b, kv, g, t: (b, kv))
    right = pl.BlockSpec((TM, CHUNK), lambda b, kv, g, t: (b * TPE + t, H + kv * G + g))
    res = pl.pallas_call(
        body, name=name, grid=(B, KV, G, TPE),
        in_specs=[qspec, kvspec, pl.BlockSpec((S, CHUNK), lambda b, kv, g, t: (b, v_off + kv)), right,
                  pl.BlockSpec((None, TM, 1), lambda b, kv, g, t: (kv * G + g, b * TPE + t, 0)), right] + [ANY] * nc,
        out_specs=[qspec, kvspec, kvspec] + [ANY] * nc,
        out_shape=[jax.ShapeDtypeStruct((T, H * CHUNK), BF16), jax.ShapeDtypeStruct((T, KV * CHUNK), F32),
                   jax.ShapeDtypeStruct((T, KV * CHUNK), F32)] + (_carry_out_shapes(carry) if nc else []),
        scratch_shapes=_carry_scratch(nc),
        compiler_params=_params(("arbitrary",) * 4 if nc else ("parallel", "parallel", "arbitrary", "arbitrary")),
    )(aq, ak, p, cat, lse, dcat, *(carry.srcs if nc else ()))
    return res[:3], res[3:]


_GELU_C = math.sqrt(2.0 / math.pi)


def _gelu(x):
    return 0.5 * x * (1.0 + jnp.tanh(_GELU_C * (x + 0.044715 * x * x * x)))


def _gelu_and_grad(x):
    x2 = x * x
    th = jnp.tanh(_GELU_C * (x + 0.044715 * x * x2))
    half = 0.5 * (1.0 + th)
    return x * half, half + 0.5 * x * (1.0 - th * th) * _GELU_C * (1.0 + 3.0 * 0.044715 * x2)


def _cm_fwd(cfg, name, a, vg, ws, bs):
    T, TM, W, NG = cfg.T, cfg.TM, cfg.CMW, cfg.CMG

    def body(a_ref, vg_ref, ws_ref, bs_ref, m_ref):
        v = _gelu(a_ref[:, pl.ds(W, W)].astype(F32))
        vn = (v * lax.rsqrt(jnp.mean(v * v, axis=-1, keepdims=True) + EPS) * vg_ref[...]).astype(BF16)
        for c in range(TM // CHUNK):
            for g in range(NG):
                rows, cols = slice(c * CHUNK, (c + 1) * CHUNK), slice(g * CHUNK, (g + 1) * CHUNK)
                sv = _dot(ws_ref[g].astype(BF16), vn[rows, cols]) + bs_ref[g]
                u = _gelu(a_ref[pl.ds(c * CHUNK, CHUNK), pl.ds(g * CHUNK, CHUNK)].astype(F32))
                m_ref[pl.ds(c * CHUNK, CHUNK), pl.ds(g * CHUNK, CHUNK)] = (u * sv).astype(BF16)

    return pl.pallas_call(
        body, name=name, grid=(T // TM,),
        in_specs=[pl.BlockSpec((TM, 2 * W), lambda i: (i, 0)), pl.BlockSpec((1, W), lambda i: (0, 0)),
                  pl.BlockSpec((NG, CHUNK, CHUNK), lambda i: (0, 0, 0)),
                  pl.BlockSpec((NG, CHUNK, 1), lambda i: (0, 0, 0))],
        out_specs=pl.BlockSpec((TM, W), lambda i: (i, 0)),
        out_shape=jax.ShapeDtypeStruct((T, W), BF16),
        compiler_params=_params(("parallel",)),
    )(a, vg, ws, bs)


def _cm_bwd(cfg, name, a, vg, ws, bs, dm):
    T, TM, W, NG = cfg.T, cfg.TM, cfg.CMW, cfg.CMG

    def body(a_ref, vg_ref, ws_ref, bs_ref, dm_ref, da_ref, dws_ref, dbs_ref, dvg_ref, dvn_ref):
        i = pl.program_id(0)

        @pl.when(i == 0)
        def _():
            dws_ref[...] = jnp.zeros_like(dws_ref)
            dbs_ref[...] = jnp.zeros_like(dbs_ref)
            dvg_ref[...] = jnp.zeros_like(dvg_ref)

        v, v_grad = _gelu_and_grad(a_ref[:, pl.ds(W, W)].astype(F32))
        rstd = lax.rsqrt(jnp.mean(v * v, axis=-1, keepdims=True) + EPS)
        xhat = v * rstd
        vg = vg_ref[...]
        vn = (xhat * vg).astype(BF16)
        for c in range(TM // CHUNK):
            for g in range(NG):
                rows, cols = slice(c * CHUNK, (c + 1) * CHUNK), slice(g * CHUNK, (g + 1) * CHUNK)
                rs, cs = pl.ds(c * CHUNK, CHUNK), pl.ds(g * CHUNK, CHUNK)
                wsb = ws_ref[g].astype(BF16)
                blk = vn[rows, cols]
                sv = _dot(wsb, blk) + bs_ref[g]
                u, u_grad = _gelu_and_grad(a_ref[rs, cs].astype(F32))
                dmb = dm_ref[rs, cs].astype(F32)
                da_ref[rs, cs] = (dmb * sv * u_grad).astype(BF16)
                dsv = dmb * u
                dsvb = dsv.astype(BF16)
                dbs_ref[g] += jnp.sum(dsv, axis=1, keepdims=True)
                dws_ref[g] += _dot_nt(dsvb, blk)
                dvn_ref[rs, cs] = _dot_tn(wsb, dsvb)
        dvn = dvn_ref[...]
        dvg_ref[...] += jnp.sum(dvn * xhat, axis=0, keepdims=True)
        dxh = dvn * vg
        dv = rstd * (dxh - xhat * jnp.mean(dxh * xhat, axis=-1, keepdims=True))
        da_ref[:, pl.ds(W, W)] = (dv * v_grad).astype(BF16)

    return pl.pallas_call(
        body, name=name, grid=(T // TM,),
        in_specs=[pl.BlockSpec((TM, 2 * W), lambda i: (i, 0)), pl.BlockSpec((1, W), lambda i: (0, 0)),
                  pl.BlockSpec((NG, CHUNK, CHUNK), lambda i: (0, 0, 0)),
                  pl.BlockSpec((NG, CHUNK, 1), lambda i: (0, 0, 0)), pl.BlockSpec((TM, W), lambda i: (i, 0))],
        out_specs=[pl.BlockSpec((TM, 2 * W), lambda i: (i, 0)), pl.BlockSpec((NG, CHUNK, CHUNK), lambda i: (0, 0, 0)),
                   pl.BlockSpec((NG, CHUNK, 1), lambda i: (0, 0, 0)), pl.BlockSpec((1, W), lambda i: (0, 0))],
        out_shape=[jax.ShapeDtypeStruct((T, 2 * W), BF16), jax.ShapeDtypeStruct((NG, CHUNK, CHUNK), F32),
                   jax.ShapeDtypeStruct((NG, CHUNK, 1), F32), jax.ShapeDtypeStruct((1, W), F32)],
        scratch_shapes=[pltpu.VMEM((TM, W), F32)],
        compiler_params=_params(("arbitrary",)),
    )(a, vg, ws, bs, dm)


def _layer_weights(l):
    mixer = ("ab_w_in", "ab_w_out") if l % 2 == 0 else ("cm_w_in", "cm_w_out")
    return [(mixer[0], l // 2), (mixer[1], l // 2), ("ff_w1", l), ("ff_w2", l)]


def _local_step(cfg, xcat, tgt, mods, shards, w):
    D, TM, H = cfg.D, cfg.TM, cfg.H
    cosf, sinf = _rope_tables(cfg)
    full, big, recv = {}, {}, {}

    def gather_of(keys):
        return _Carry("gather", tuple(shards[n][i] for n, i in keys), tuple(BIG[n] for n, _ in keys))

    def exchange_of(keys):
        return _Carry("exchange", tuple(big[k] for k in keys), tuple(BIG[n] for n, _ in keys))

    def mm(pending, name, a, b, **kw):
        if not pending:
            return _mm(name, a, b, **kw)
        key, carry, sink = pending.pop(0)
        out, (got,) = _mm(name, a, b, carry=carry, **kw)
        sink[key] = got
        return out

    def with_carry(call, keys, make, sink):
        out, got = call(carry=make(keys) if keys else None)
        sink.update(zip(keys, got))
        return out

    keys0 = _layer_weights(0)
    full[keys0[0]], = _comm_call("gather_weights_0", gather_of(keys0[:1]))
    TG = 3 * TM if cfg.TPE % 3 == 0 else TM
    tiles_per_ex = cfg.S // TG
    gate_spec = pl.BlockSpec((None, 2, 6, D), lambda i, j, k: (i // tiles_per_ex, 0, 0, 0))

    def resid_epi(igate, nxt):
        def epi(acc, row_tile, x_ref, mod_ref, *nxt_refs):
            row = lax.broadcasted_iota(jnp.int32, (TG, 1), 0)
            is_ctx = jnp.logical_and(row_tile % tiles_per_ex == 0, row < cfg.SC)

            def pick(ref, idx):
                return jnp.where(is_ctx, ref[0, pl.ds(idx, 1), :], ref[1, pl.ds(idx, 1), :])

            x = x_ref[...] + pick(mod_ref, igate) * acc
            if nxt is None:
                return x, acc
            gain_ref, modn_ref = nxt_refs
            n = x * lax.rsqrt(jnp.mean(x * x, axis=-1, keepdims=True) + EPS) * gain_ref[...]
            return x, acc, n * (1.0 + pick(modn_ref, nxt[3])) + pick(modn_ref, nxt[2])
        return epi

    def gated_out(pending, name, a, key, x, mod, igate, nxt=None):
        extras = [(x, pl.BlockSpec((TG, D), lambda i, j, k: (i, j))), (mod, gate_spec)]
        if nxt is not None:
            extras += [(nxt[0], pl.BlockSpec((1, D), lambda i, j, k: (0, 0))), (nxt[1], gate_spec)]
        return mm(pending, name, a, full[key], mode="nn", tm=TG, tn=D, outs=[F32, BF16] + [BF16] * (nxt is not None),
                  epi=resid_epi(igate, nxt), extras=extras)

    saved = []
    x = xcat
    h = _norm_mod_fwd(cfg, "norm1_fwd_0", x, w["norm1_g"][0][None], mods[0], 0, 1)
    for l in range(N_LAYERS):
        li = l // 2
        mod = mods[l]
        k_in, k_out, k_ff1, k_ff2 = _layer_weights(l)
        pend = [(k, gather_of([k]), full) for k in _layer_weights(l + 1)] if l + 1 < N_LAYERS else []
        norm2 = (w["norm2_g"][l][None], mod, 3, 4)
        s = {"x0": x, "h": h}
        if l % 2 == 0:
            lgb = jnp.broadcast_to(jax.nn.log_sigmoid(w["ret_decay"][li]).T[:, :, None, None], (H, 2, 8, LANES))
            qg, kg = w["att_q_norm_g"][li][None], w["att_k_norm_g"][li][None]
            s["p"] = mm(pend, f"ab_in_{l}", s["h"], full[k_in], mode="nn", outs=[BF16], tn=768)
            s["rq"], s["rk"], s["aq"], s["ak"] = _prep_fwd(cfg, f"prep_fwd_{l}", s["p"], cosf, sinf, qg, kg)
            s["o"], ret, s["st"] = with_carry(
                functools.partial(_retention_fwd, cfg, f"ret_fwd_{l}", s["rq"], s["rk"], s["p"], lgb),
                keys0[1:3] if l == 0 else [], gather_of, full)
            s["cat"], s["lse"] = with_carry(
                functools.partial(_attention_fwd, cfg, f"att_fwd_{l}", s["aq"], s["ak"], s["p"], ret),
                keys0[3:] if l == 0 else [], gather_of, full)
            s["lgb"], s["qg"], s["kg"] = lgb, qg, kg
            x, s["y1"], s["h2"] = gated_out(pend, f"ab_out_{l}", s["cat"], k_out, x, mod, 2, norm2)
        else:
            s["a"] = mm(pend, f"cm_in_{l}", s["h"], full[k_in], mode="nn", outs=[BF16])
            s["vg"], s["ws"], s["bs"] = w["cm_v_norm_g"][li][None], w["cm_w_s"][li], w["cm_b_s"][li][:, :, None]
            s["m"] = _cm_fwd(cfg, f"cm_fwd_{l}", s["a"], s["vg"], s["ws"], s["bs"])
            x, s["y1"], s["h2"] = gated_out(pend, f"cm_out_{l}", s["m"], k_out, x, mod, 2, norm2)
        s["x1"] = x
        s["r"] = mm(pend, f"ff1_{l}", s["h2"], full[k_ff1], mode="nn", outs=[BF16],
                    epi=lambda acc, row_tile: (jnp.square(jnp.maximum(acc, 0.0)),))
        if l + 1 < N_LAYERS:
            x, s["y2"], h = gated_out(pend, f"ff2_{l}", s["r"], k_ff2, x, mod, 5,
                                      (w["norm1_g"][l + 1][None], mods[l + 1], 0, 1))
        else:
            x, s["y2"] = gated_out(pend, f"ff2_{l}", s["r"], k_ff2, x, mod, 5)
        saved.append(s)

    loss, dx = _loss_grad(cfg, x, tgt)

    small = {k: [None] * n for k, n in (("norm1_g", 4), ("norm2_g", 4), ("ret_lg", 2), ("att_q_norm_g", 2),
                                        ("att_k_norm_g", 2), ("cm_v_norm_g", 2), ("cm_w_s", 2), ("cm_b_s", 2))}
    dmods = [None] * N_LAYERS

    for l in reversed(range(N_LAYERS)):
        li = l // 2
        s, mod = saved[l], mods[l]
        k_in, k_out, k_ff1, k_ff2 = _layer_weights(l)
        above = _layer_weights(l + 1) if l + 1 < N_LAYERS else [None] * 4
        if l == N_LAYERS - 1:
            dy2, dg2 = _gate_bwd(cfg, f"gate2_bwd_{l}", dx, s["y2"], mod, 5)
        da2, big[k_ff2] = with_carry(functools.partial(_ff_bwd, f"ff2_bwd_{l}", dy2, s["r"], full[k_ff2], "w2"),
                                     [above[3], above[1]] if above[0] else [], exchange_of, recv)
        dh2, big[k_ff1] = with_carry(functools.partial(_ff_bwd, f"ff1_bwd_{l}", s["h2"], da2, full[k_ff1], "w1"),
                                     [above[2], above[0]] if above[0] else [], exchange_of, recv)
        dx, dm2, small["norm2_g"][l], do, dg1 = _norm_mod_bwd(
            cfg, f"norm2_bwd_{l}", s["x1"], w["norm2_g"][l][None], mod, 3, 4, dh2, dx, gate=(s["y1"], mod, 2))
        if l % 2 == 0:
            big[k_out] = _mm(f"ab_out_dw_{l}", s["cat"], do, mode="tn", outs=[BF16])
            dcat = _mm(f"ab_out_dx_{l}", do, full[k_out], mode="nt", outs=[BF16])
            d_rq, d_rk, d_rv, d_gt, dlg = with_carry(
                functools.partial(_retention_bwd, cfg, f"ret_bwd_{l}", s["rq"], s["rk"], s["p"], s["o"], s["st"], dcat,
                                  s["lgb"]), [k_ff2, k_ff1] if l == 0 else [], exchange_of, recv)
            d_aq, d_ak, d_av = with_carry(
                functools.partial(_attention_bwd, cfg, f"att_bwd_{l}", s["aq"], s["ak"], s["p"], s["cat"], s["lse"], dcat),
                [k_out] if l == 0 else [], exchange_of, recv)
            dp, dqg, dkg = _prep_bwd(cfg, f"prep_bwd_{l}", s["p"], cosf, sinf, s["qg"], s["kg"],
                                     d_rq, d_rk, d_rv, d_gt, d_aq, d_ak, d_av)
            small["ret_lg"][li] = jnp.sum(dlg[:, :, :, 0, 0], axis=0).T
            small["att_q_norm_g"][li], small["att_k_norm_g"][li] = dqg[0], dkg[0]
            big[k_in] = _mm(f"ab_in_dw_{l}", s["h"], dp, mode="tn", outs=[BF16])
            last = [(k_in, exchange_of([k_in]), recv)] if l == 0 else []
            dh = mm(last, f"ab_in_dx_{l}", dp, full[k_in], mode="nt", outs=[BF16], tk=768)
        else:
            big[k_out] = _mm(f"cm_out_dw_{l}", s["m"], do, mode="tn", outs=[BF16])
            dm = _mm(f"cm_out_dx_{l}", do, full[k_out], mode="nt", outs=[BF16])
            da, dws, dbs, dvg = _cm_bwd(cfg, f"cm_bwd_{l}", s["a"], s["vg"], s["ws"], s["bs"], dm)
            small["cm_w_s"][li], small["cm_b_s"][li], small["cm_v_norm_g"][li] = dws, dbs[:, :, 0], dvg[0]
            big[k_in] = _mm(f"cm_in_dw_{l}", s["h"], da, mode="tn", outs=[BF16])
            dh = _mm(f"cm_in_dx_{l}", da, full[k_in], mode="nt", outs=[BF16])
        below = (saved[l - 1]["y2"], mods[l - 1], 5) if l > 0 else None
        dx, dm1, small["norm1_g"][l], *rest = _norm_mod_bwd(
            cfg, f"norm1_bwd_{l}", s["x0"], w["norm1_g"][l][None], mod, 0, 1, dh, dx, gate=below, lat_only=l == 0)
        dmods[l] = jnp.concatenate([dm1, dg1, dm2, dg2], axis=2)
        if l > 0:
            dy2, dg2 = rest
    return loss, dx, recv, small, dmods


N_DEV = 8
N_CHIP = 4
MESH = pl.DeviceIdType.MESH
ANY = pl.BlockSpec(memory_space=pl.ANY)
BIG = {"ab_w_in": 1, "ab_w_out": 0, "cm_w_in": 1, "cm_w_out": 0, "ff_w1": 1, "ff_w2": 0}


class _Carry(NamedTuple):
    kind: str
    srcs: tuple
    axes: tuple


def _place():
    x, y, c = lax.axis_index("x"), lax.axis_index("y"), lax.axis_index("c")
    return x, y, c, [(1 - x, y), (x, 1 - y), (1 - x, 1 - y)]


def _shard_of(ref, axis, s, width):
    start = pl.multiple_of(s * width, LANES)
    if axis == 0:
        return ref.at[pl.ds(start, width), :]
    return ref.at[:, pl.ds(start, width)]


def _carry_out_shapes(carry):
    shapes = []
    for src, axis in zip(carry.srcs, carry.axes):
        shape = list(src.shape)
        if carry.kind == "swap":
            pass
        elif carry.kind == "gather":
            shape[axis] *= N_CHIP
        else:
            shape[axis] //= N_CHIP
            shape = [N_CHIP] + shape
        shapes.append(jax.ShapeDtypeStruct(tuple(shape), src.dtype))
    return shapes


def _carry_copies(carry, srcs, dsts, send_sems, recv_sems, local_sems):
    x, y, c, chips = _place()
    me = 2 * x + y
    copies = []
    if carry.kind == "swap":
        return [pltpu.make_async_remote_copy(
            src_ref=srcs[t], dst_ref=dsts[t], send_sem=send_sems.at[3 * t], recv_sem=recv_sems.at[3 * t],
            device_id=(x, y, 1 - c), device_id_type=MESH) for t in range(len(srcs))]
    for t, axis in enumerate(carry.axes):
        if carry.kind == "gather":
            own = _shard_of(dsts[t], axis, me, srcs[t].shape[axis])
            copies.append(pltpu.make_async_copy(srcs[t], own, local_sems.at[t]))
            parts = [(srcs[t], own)] * 3
        else:
            width = dsts[t].shape[1 + axis]
            copies.append(pltpu.make_async_copy(_shard_of(srcs[t], axis, me, width), dsts[t].at[3], local_sems.at[t]))
            parts = [(_shard_of(srcs[t], axis, 2 * px + py, width), dsts[t].at[j]) for j, (px, py) in enumerate(chips)]
        for j, (px, py) in enumerate(chips):
            copies.append(pltpu.make_async_remote_copy(
                src_ref=parts[j][0], dst_ref=parts[j][1], send_sem=send_sems.at[3 * t + j],
                recv_sem=recv_sems.at[3 * t + j], device_id=(px, py, c), device_id_type=MESH))
    return copies


def _comm_call(name, carry):
    nc = len(carry.srcs)

    def body(*refs):
        copies = _carry_copies(carry, refs[:nc], refs[nc:2 * nc], *refs[2 * nc:])
        for cp in copies:
            cp.start()
        for cp in copies:
            cp.wait()

    return pl.pallas_call(
        body, name=name, out_shape=_carry_out_shapes(carry), in_specs=[ANY] * nc, out_specs=[ANY] * nc,
        scratch_shapes=[pltpu.SemaphoreType.DMA((3 * nc,)), pltpu.SemaphoreType.DMA((3 * nc,)),
                        pltpu.SemaphoreType.DMA((nc,))],
    )(*carry.srcs)


def _allgather8(name, block, carry=None):
    m_per, n = block.shape
    nc = len(carry.srcs) if carry is not None else 0

    def body(*refs):
        x_ref, out_ref = refs[0], refs[1 + nc]
        send_sems, recv_sems, local_sem = refs[2 + 2 * nc:5 + 2 * nc]
        carried = _carry_copies(carry, refs[1:1 + nc], refs[2 + nc:2 + 2 * nc], *refs[5 + 2 * nc:]) if nc else []
        for cp in carried:
            cp.start()
        x, y, c, chips = _place()
        me, sibling = (x, y, c), (x, y, 1 - c)

        def rows(px, py, pc):
            return out_ref.at[pl.ds((4 * px + 2 * py + pc) * m_per, m_per), :]

        def copy(k, blk, to, src=None):
            return pltpu.make_async_remote_copy(
                src_ref=rows(*blk) if src is None else src, dst_ref=rows(*blk),
                send_sem=send_sems.at[k], recv_sem=recv_sems.at[k], device_id=to, device_id_type=MESH)

        mine = pltpu.make_async_copy(x_ref, rows(*me), local_sem)
        mine.start()
        first = [copy(0, me, sibling, src=x_ref)]
        first += [copy(1 + j, me, (*chip, c), src=x_ref) for j, chip in enumerate(chips)]
        for cp in first:
            cp.start()
        passed = [copy(4 + j, (*chip, c), sibling) for j, chip in enumerate(chips)]
        for j, chip in enumerate(chips):
            copy(1 + j, (*chip, c), me).wait_recv()
            passed[j].start()
        copy(0, sibling, me).wait_recv()
        for j, chip in enumerate(chips):
            copy(4 + j, (*chip, 1 - c), me).wait_recv()
        for cp in first + passed:
            cp.wait_send()
        mine.wait()
        for cp in carried:
            cp.wait()

    res = pl.pallas_call(
        body, name=name,
        out_shape=[jax.ShapeDtypeStruct((N_DEV * m_per, n), block.dtype)] + (_carry_out_shapes(carry) if nc else []),
        in_specs=[pl.BlockSpec(memory_space=pltpu.VMEM)] + [ANY] * nc,
        out_specs=[pl.BlockSpec(memory_space=pltpu.VMEM)] + [ANY] * nc,
        scratch_shapes=[pltpu.SemaphoreType.DMA((7,)), pltpu.SemaphoreType.DMA((7,)), pltpu.SemaphoreType.DMA]
        + _carry_scratch(nc),
        compiler_params=pltpu.CompilerParams(vmem_limit_bytes=VMEM_LIMIT),
    )(block, *(carry.srcs if nc else ()))
    return (res[0], res[1:]) if nc else res[0]


def _rows_view(a):
    if a.ndim == 1:
        return a.reshape(1, a.shape[0])
    return a.reshape(-1, a.shape[-1])


def _row_tile(rows, cols, target_elems=1 << 17):
    tr = rows
    while tr % 16 == 0 and tr * cols > target_elems:
        tr //= 2
    return tr


def _sum_leading(name, a):
    n, rows, cols = a.shape
    tr = _row_tile(rows, cols * n, target_elems=1 << 20)

    def body(a_ref, o_ref):
        acc = a_ref[0].astype(F32)
        for i in range(1, n):
            acc = acc + a_ref[i].astype(F32)
        o_ref[...] = acc

    return pl.pallas_call(
        body, name=name, grid=(rows // tr,),
        in_specs=[pl.BlockSpec((n, tr, cols), lambda i: (0, i, 0))],
        out_specs=pl.BlockSpec((tr, cols), lambda i: (i, 0)),
        out_shape=jax.ShapeDtypeStruct((rows, cols), F32),
        compiler_params=_params(("parallel",)),
    )(a)


def _sum_parts_layers(name, parts):
    n_layers = len(parts)
    n, rows, cols = parts[0].shape
    tr = _row_tile(rows, cols * n, target_elems=1 << 20)

    def body(*refs):
        o_ref = refs[n_layers]
        layer = pl.program_id(0)
        for k in range(n_layers):
            @pl.when(layer == k)
            def _():
                acc = refs[k][0].astype(F32)
                for i in range(1, n):
                    acc = acc + refs[k][i].astype(F32)
                o_ref[...] = acc

    def in_spec(k):
        return pl.BlockSpec((n, tr, cols), lambda l, i: (0, jnp.where(l == k, i, 0), 0))

    return pl.pallas_call(
        body, name=name, grid=(n_layers, rows // tr),
        in_specs=[in_spec(k) for k in range(n_layers)],
        out_specs=pl.BlockSpec((None, tr, cols), lambda l, i: (l, i, 0)),
        out_shape=jax.ShapeDtypeStruct((n_layers, rows, cols), F32),
        compiler_params=_params(("arbitrary", "arbitrary")),
    )(*parts)


def _silu_rows(name, x):
    def body(x_ref, o_ref):
        v = x_ref[...]
        o_ref[...] = v * jax.nn.sigmoid(v)

    return pl.pallas_call(body, name=name, out_shape=jax.ShapeDtypeStruct(x.shape, F32))(x)


def _silu_bwd_rows(name, x, dy):
    def body(x_ref, dy_ref, o_ref):
        v = x_ref[...]
        sg = jax.nn.sigmoid(v)
        o_ref[...] = dy_ref[...] * (sg * (1.0 + v * (1.0 - sg)))

    return pl.pallas_call(body, name=name, out_shape=jax.ShapeDtypeStruct(x.shape, F32))(x, dy)


def _adamw(name, w, g_parts, m, v):
    shape = w.shape
    w2, m2, v2 = _rows_view(w), _rows_view(m), _rows_view(v)
    gs = [_rows_view(g) for g in g_parts]
    rows, cols = w2.shape
    tr = _row_tile(rows, cols)
    ng = len(gs)

    def body(*refs):
        w_ref, m_ref, v_ref = refs[0], refs[1], refs[2]
        g_refs = refs[3:3 + ng]
        g_out, d_out, m_out, v_out = refs[3 + ng:]
        g = g_refs[0][...]
        for r in g_refs[1:]:
            g = g + r[...]
        m1 = ADAM_B1 * m_ref[...] + (1.0 - ADAM_B1) * g
        v1 = ADAM_B2 * v_ref[...] + (1.0 - ADAM_B2) * jnp.square(g)
        m_hat = m1 / (1.0 - ADAM_B1 ** ADAM_STEP)
        v_hat = v1 / (1.0 - ADAM_B2 ** ADAM_STEP)
        g_out[...] = g
        d_out[...] = -ADAM_LR * (m_hat / (jnp.sqrt(v_hat) + ADAM_EPS) + ADAM_WD * w_ref[...])
        m_out[...] = m1
        v_out[...] = v1

    spec = pl.BlockSpec((tr, cols), lambda i: (i, 0))
    res = pl.pallas_call(
        body, name=name, grid=(rows // tr,), in_specs=[spec] * (3 + ng), out_specs=[spec] * 4,
        out_shape=[jax.ShapeDtypeStruct((rows, cols), F32)] * 4,
        compiler_params=_params(("parallel",)),
    )(w2, m2, v2, *gs)
    return tuple(r.reshape(shape) for r in res)


MOD_ROWS = 48


def kernel(x, c, ctx, c_ctx, mod_w, mod_b, norm1_g, norm2_g, ab_w_in, ab_w_out, ret_decay, att_q_norm_g, att_k_norm_g, cm_w_in, cm_v_norm_g, cm_w_s, cm_b_s, cm_w_out, ff_w1, ff_w2, loss_target, m_c_ctx, m_mod_w, m_mod_b, m_norm1_g, m_norm2_g, m_ab_w_in, m_ab_w_out, m_ret_decay, m_att_q_norm_g, m_att_k_norm_g, m_cm_w_in, m_cm_v_norm_g, m_cm_w_s, m_cm_b_s, m_cm_w_out, m_ff_w1, m_ff_w2, v_c_ctx, v_mod_w, v_mod_b, v_norm1_g, v_norm2_g, v_ab_w_in, v_ab_w_out, v_ret_decay, v_att_q_norm_g, v_att_k_norm_g, v_cm_w_in, v_cm_v_norm_g, v_cm_w_s, v_cm_b_s, v_cm_w_out, v_ff_w1, v_ff_w2):
    B, SL, D = x.shape
    cfg = Cfg(B=B, SC=ctx.shape[1], SL=SL, D=D, FF=ff_w1.shape[2] * N_CHIP)
    L = N_LAYERS
    n_ex = B * N_DEV
    mcols = mod_w.shape[2]
    weights = dict(c_ctx=c_ctx, mod_w=mod_w, mod_b=mod_b, norm1_g=norm1_g, norm2_g=norm2_g, ab_w_in=ab_w_in,
                   ab_w_out=ab_w_out, ret_decay=ret_decay, att_q_norm_g=att_q_norm_g, att_k_norm_g=att_k_norm_g,
                   cm_w_in=cm_w_in, cm_v_norm_g=cm_v_norm_g, cm_w_s=cm_w_s, cm_b_s=cm_b_s, cm_w_out=cm_w_out,
                   ff_w1=ff_w1, ff_w2=ff_w2)
    m_in = dict(c_ctx=m_c_ctx, mod_w=m_mod_w, mod_b=m_mod_b, norm1_g=m_norm1_g, norm2_g=m_norm2_g, ab_w_in=m_ab_w_in,
                ab_w_out=m_ab_w_out, ret_decay=m_ret_decay, att_q_norm_g=m_att_q_norm_g, att_k_norm_g=m_att_k_norm_g,
                cm_w_in=m_cm_w_in, cm_v_norm_g=m_cm_v_norm_g, cm_w_s=m_cm_w_s, cm_b_s=m_cm_b_s, cm_w_out=m_cm_w_out,
                ff_w1=m_ff_w1, ff_w2=m_ff_w2)
    v_in = dict(c_ctx=v_c_ctx, mod_w=v_mod_w, mod_b=v_mod_b, norm1_g=v_norm1_g, norm2_g=v_norm2_g, ab_w_in=v_ab_w_in,
                ab_w_out=v_ab_w_out, ret_decay=v_ret_decay, att_q_norm_g=v_att_q_norm_g, att_k_norm_g=v_att_k_norm_g,
                cm_w_in=v_cm_w_in, cm_v_norm_g=v_cm_v_norm_g, cm_w_s=v_cm_w_s, cm_b_s=v_cm_b_s, cm_w_out=v_cm_w_out,
                ff_w1=v_ff_w1, ff_w2=v_ff_w2)
    xi, yi, ci = lax.axis_index("x"), lax.axis_index("y"), lax.axis_index("c")
    chip = 2 * xi + yi
    dev = 2 * chip + ci

    shards = {n: [weights[n][i].astype(BF16) for i in range(weights[n].shape[0])] for n in BIG}
    vgw = cm_v_norm_g.shape[1]
    blk = jnp.zeros((8, D), F32).at[:B].set(c).at[B:B + 2, :vgw].set(cm_v_norm_g)
    g0 = _allgather8("gather_c", blk).reshape(N_DEV, 8, D)
    c_all = g0[:, :B].reshape(n_ex, D)
    vg_full = jnp.concatenate([g0[2 * s, B:B + 2, :vgw] for s in range(N_CHIP)], axis=-1)

    pre = jnp.zeros((MOD_ROWS, D), F32).at[:n_ex].set(c_all).at[n_ex].set(c_ctx)
    act = _silu_rows("silu_c", pre)
    mpart = jnp.stack([_mm(f"mod_fwd_{l}", act, mod_w, mode="nn", layer=l, outs=[F32], tn=mcols) for l in range(L)])
    g1 = _allgather8("gather_mod", mpart.reshape(L * MOD_ROWS, mcols)).reshape(N_DEV, L, MOD_ROWS, mcols)
    mod_all = jnp.concatenate([g1[2 * s] for s in range(N_CHIP)], axis=-1) + mod_b[:, None, :]
    mod_lat = lax.dynamic_slice_in_dim(mod_all, dev * B, B, axis=1)
    mod_ctx = jnp.broadcast_to(mod_all[:, n_ex][:, None], mod_lat.shape)
    mods = jnp.stack([mod_ctx, mod_lat], axis=2).reshape(L, B, 2, 6, D)

    w = dict(norm1_g=norm1_g, norm2_g=norm2_g, ret_decay=ret_decay, att_q_norm_g=att_q_norm_g,
             att_k_norm_g=att_k_norm_g, cm_v_norm_g=vg_full, cm_w_s=cm_w_s, cm_b_s=cm_b_s)
    xcat = jnp.concatenate([ctx, x], axis=1).reshape(cfg.T, D)
    loss_local, dx_lat, recv, small, dmods = _local_step(cfg, xcat, loss_target.reshape(B * SL, D), mods, shards, w)
    loss = lax.psum(loss_local, ("x", "y", "c"))
    grad_x = dx_lat.reshape(B, SL, D)

    part = [_sum_parts_layers(f"sum_{n}", [recv[(n, i)] for i in range(weights[n].shape[0])]) for n in BIG]

    dmod = jnp.stack(dmods).reshape(L, B, 2, 6 * D)
    dmod_lat = dmod[:, :, 1]
    dmod_ctx = jnp.sum(dmod[:, :, 0], axis=1)
    d_ret = jnp.stack(small["ret_lg"]) * jax.nn.sigmoid(-ret_decay)
    summed = [dmod_ctx.reshape(-1), jnp.stack(small["norm1_g"]).reshape(-1), jnp.stack(small["norm2_g"]).reshape(-1),
              jnp.stack(small["cm_v_norm_g"]).reshape(-1), jnp.stack(small["cm_w_s"]).reshape(-1),
              jnp.stack(small["cm_b_s"]).reshape(-1), jnp.stack(small["att_q_norm_g"]).reshape(-1),
              jnp.stack(small["att_k_norm_g"]).reshape(-1), d_ret.reshape(-1)]
    sizes = [int(a.shape[0]) for a in summed]
    flat = jnp.concatenate(summed + [dmod_lat.reshape(-1)])
    n_sum = sum(sizes)
    n_sum_rows = -(-n_sum // D)
    lat_rows = (L * B * 6 * D) // D
    pack_rows = -(-(n_sum_rows + lat_rows) // 8) * 8
    packed = jnp.zeros((pack_rows * D,), F32).at[:n_sum].set(flat[:n_sum])
    packed = packed.at[n_sum_rows * D:(n_sum_rows + lat_rows) * D].set(flat[n_sum:]).reshape(pack_rows, D)
    g2, other = _allgather8("gather_small", packed, carry=_Carry("swap", tuple(part), (0,) * len(part)))
    g2 = g2.reshape(N_DEV, pack_rows, D)
    tot = _sum_leading("sum_small", g2[:, :n_sum_rows]).reshape(-1)
    pieces, off = [], 0
    for sz in sizes:
        pieces.append(tot[off:off + sz])
        off += sz
    dmod_ctx_t, g_n1, g_n2, g_vg, g_ws, g_bs, g_qg, g_kg, g_rd = pieces
    dmod_ctx_t = dmod_ctx_t.reshape(L, 6 * D)
    dmod_lat_all = g2[:, n_sum_rows:n_sum_rows + lat_rows].reshape(N_DEV, L, B, 6 * D)
    dmod_rows = jnp.zeros((L, MOD_ROWS, 6 * D), F32)
    dmod_rows = dmod_rows.at[:, :n_ex].set(jnp.transpose(dmod_lat_all, (1, 0, 2, 3)).reshape(L, n_ex, 6 * D))
    dmod_rows = dmod_rows.at[:, n_ex].set(dmod_ctx_t)
    g_mod_b = _sum_leading("sum_mod_b", jnp.transpose(dmod_rows, (1, 0, 2)))
    dmod_mine = lax.dynamic_slice_in_dim(dmod_rows, chip * mcols, mcols, axis=2)
    g_mod_w = jnp.stack([_mm(f"mod_dw_{l}", act, dmod_mine[l], mode="tn", outs=[F32], tn=mcols) for l in range(L)])
    ctx8 = jnp.zeros((L, 8, mcols), F32).at[:, 0].set(dmod_mine[:, n_ex])
    dcc = [_mm(f"mod_dctx_{l}", ctx8[l], mod_w, mode="nt", layer=l, outs=[F32], tk=mcols) for l in range(L)]
    dcc = _sum_leading("sum_dctx_layers", jnp.stack(dcc))
    g3 = _allgather8("gather_dctx", dcc).reshape(N_DEV, 8, D)
    dcc_t = _sum_leading("sum_dctx_chips", g3[0::2])[0:1]
    g_c_ctx = _silu_bwd_rows("silu_bwd_cctx", c_ctx[None], dcc_t)[0]

    vg_mine = lax.dynamic_slice_in_dim(g_vg.reshape(2, -1), chip * vgw, vgw, axis=1)
    small_g = dict(c_ctx=g_c_ctx, mod_w=g_mod_w, mod_b=g_mod_b, norm1_g=g_n1.reshape(norm1_g.shape),
                   norm2_g=g_n2.reshape(norm2_g.shape), ret_decay=g_rd.reshape(ret_decay.shape),
                   att_q_norm_g=g_qg.reshape(att_q_norm_g.shape), att_k_norm_g=g_kg.reshape(att_k_norm_g.shape),
                   cm_v_norm_g=vg_mine, cm_w_s=g_ws.reshape(cm_w_s.shape), cm_b_s=g_bs.reshape(cm_b_s.shape))
    out = {}
    for n, g in small_g.items():
        out[n] = _adamw(f"adamw_{n}", weights[n], [g], m_in[n], v_in[n])
    for n, p_mine, p_other in zip(BIG, part, other):
        out[n] = _adamw(f"adamw_{n}", weights[n], [p_mine, p_other], m_in[n], v_in[n])

    order = list(weights)
    return (loss, grad_x, *[out[n][0] for n in order], *[out[n][1] for n in order],
            *[out[n][2] for n in order], *[out[n][3] for n in order])
```

```python
import functools
import math
from typing import NamedTuple

import jax
import jax.numpy as jnp
from jax import lax
from jax.experimental import pallas as pl
from jax.experimental.pallas import tpu as pltpu

F32 = jnp.float32
BF16 = jnp.bfloat16
EPS = 1e-6
ROPE_BASE = 10000.0
LANES = 128
CHUNK = 128
N_LAYERS = 4
VMEM_LIMIT = 56 * 1024 * 1024

ADAM_LR = 0.001
ADAM_B1 = 0.9
ADAM_B2 = 0.999
ADAM_EPS = 1e-08
ADAM_WD = 0.01
ADAM_STEP = 10


class Cfg(NamedTuple):
    B: int = 4
    SC: int = 256
    SL: int = 2048
    D: int = 1024
    FF: int = 4096
    GRID_W: int = 64
    H: int = 4
    KV: int = 2
    CMW: int = 1024
    CMG: int = 8

    @property
    def S(self):
        return self.SC + self.SL

    @property
    def T(self):
        return self.B * self.S

    @property
    def TM(self):
        return self.SC

    @property
    def TPE(self):
        return self.S // self.SC

    @property
    def ABW(self):
        return (5 * self.H + 2 * self.KV) * CHUNK


def _tile(dim, pref):
    t = min(dim, pref)
    while dim % t:
        t -= LANES
    return t


def _dot(a, b):
    return lax.dot_general(a, b, (((1,), (0,)), ((), ())), preferred_element_type=F32)


def _dot_nt(a, b):
    return lax.dot_general(a, b, (((1,), (1,)), ((), ())), preferred_element_type=F32)


def _dot_tn(a, b):
    return lax.dot_general(a, b, (((0,), (0,)), ((), ())), preferred_element_type=F32)


def _params(sem, vmem=VMEM_LIMIT):
    return pltpu.CompilerParams(dimension_semantics=sem, vmem_limit_bytes=vmem)


def _mod_index(cfg):
    tpe = cfg.TPE
    return lambda i: (i // tpe, jnp.minimum(i % tpe, 1), 0, 0)


def _mm(name, a, b, *, mode, outs, tm=1024, tn=1024, tk=1024, layer=None, epi=None, extras=(), carry=None):
    bshape = b.shape[1:] if layer is not None else b.shape
    if mode == "nn":
        (M, K), N = a.shape, bshape[1]
    elif mode == "nt":
        (M, K), N = a.shape, bshape[0]
    else:
        (K, M), N = a.shape, bshape[1]
    tm, tn, tk = _tile(M, tm), _tile(N, tn), _tile(K, tk)
    nk = K // tk
    a_spec = (pl.BlockSpec((tk, tm), lambda i, j, k: (k, i)) if mode == "tn"
              else pl.BlockSpec((tm, tk), lambda i, j, k: (i, k)))
    if mode == "nt":
        bblk, bidx = (tn, tk), (lambda i, j, k: (j, k))
    else:
        bblk, bidx = (tk, tn), (lambda i, j, k: (k, j))
    if layer is not None:
        b_spec = pl.BlockSpec((None,) + bblk, lambda i, j, k: (layer,) + bidx(i, j, k))
    else:
        b_spec = pl.BlockSpec(bblk, bidx)
    ne, no = len(extras), len(outs)
    nc = len(carry.srcs) if carry is not None else 0
    dot = {"nn": _dot, "nt": _dot_nt, "tn": _dot_tn}[mode]
    grid = (M // tm, N // tn, nk)

    def body(*refs):
        a_ref, b_ref = refs[0], refs[1]
        ex, out_refs = refs[2:2 + ne], refs[2 + ne + nc:2 + ne + nc + no]
        row_tile = pl.program_id(0)

        if nc:
            step = (pl.program_id(0) * grid[1] + pl.program_id(1)) * grid[2] + pl.program_id(2)
            c_src = refs[2 + ne:2 + ne + nc]
            c_dst = refs[2 + ne + nc + no:2 + ne + 2 * nc + no]
            sems = refs[2 + ne + 2 * nc + no:2 + ne + 2 * nc + no + 3]

            @pl.when(step == 0)
            def _():
                for cp in _carry_copies(carry, c_src, c_dst, *sems):
                    cp.start()

        def finish(acc):
            res = epi(acc, row_tile, *ex) if epi is not None else (acc,)
            for r, o in zip(res, out_refs):
                o[...] = r.astype(o.dtype)

        part = dot(a_ref[...].astype(BF16), b_ref[...].astype(BF16))
        if nk == 1:
            finish(part)
        else:
            acc_ref = refs[-1]
            k = pl.program_id(2)

            @pl.when(k == 0)
            def _():
                acc_ref[...] = part

            @pl.when(k > 0)
            def _():
                acc_ref[...] += part

            @pl.when(k == nk - 1)
            def _():
                finish(acc_ref[...])

        if nc:
            @pl.when(step == grid[0] * grid[1] * grid[2] - 1)
            def _():
                for cp in _carry_copies(carry, c_src, c_dst, *sems):
                    cp.wait()

    scratch = [pltpu.SemaphoreType.DMA((3 * nc,)), pltpu.SemaphoreType.DMA((3 * nc,)),
               pltpu.SemaphoreType.DMA((nc,))] if nc else []
    if nk > 1:
        scratch.append(pltpu.VMEM((tm, tn), F32))
    res = pl.pallas_call(
        body, name=name, grid=grid,
        in_specs=[a_spec, b_spec] + [s for _, s in extras] + [ANY] * nc,
        out_specs=[pl.BlockSpec((tm, tn), lambda i, j, k: (i, j)) for _ in outs] + [ANY] * nc,
        out_shape=[jax.ShapeDtypeStruct((M, N), d) for d in outs] + (_carry_out_shapes(carry) if nc else []),
        scratch_shapes=scratch,
        compiler_params=_params(("arbitrary",) * 3 if nc else ("parallel", "parallel", "arbitrary")),
    )(a, b, *[x for x, _ in extras], *(carry.srcs if nc else ()))
    if nc:
        return (res[0] if no == 1 else res[:no]), res[no:]
    return res[0] if no == 1 else res


def _ff_bwd(name, first, second, weight, kind, carry=None):
    (T, D), FF = first.shape, second.shape[1]
    halves = 2
    tm = _tile(T, 1024)
    ffh = FF // halves if kind == "w2" else FF
    dh_cols = D if kind == "w2" else D // halves
    cw = _tile(ffh, 1024)
    n_steps = T // tm
    nc = len(carry.srcs) if carry is not None else 0

    def body(*refs):
        a_ref, b_ref, w_ref = refs[:3]
        c_src = refs[3:3 + nc]
        x_ref, dw_ref = refs[3 + nc:5 + nc]
        c_dst, sems = refs[5 + nc:5 + 2 * nc], refs[5 + 2 * nc:5 + 2 * nc + 3] if nc else ()
        acc_ref = refs[-1]
        i = pl.program_id(1)
        step = pl.program_id(0) * n_steps + i
        if nc:
            _carry_begin(carry, c_src, c_dst, sems, step)
        a = a_ref[...]
        dh = None
        for c in range(ffh // cw):
            cols = pl.ds(c * cw, cw)
            if kind == "w2":
                r = b_ref[:, cols]
                x_ref[:, cols] = (_dot_nt(a, w_ref[cols, :]) * (2.0 * jnp.sqrt(r.astype(F32)))).astype(BF16)
                part, dst = _dot_tn(r, a), acc_ref.at[cols, :]
            else:
                da = b_ref[:, cols]
                term = _dot_nt(da, w_ref[:, cols])
                dh = term if dh is None else dh + term
                part, dst = _dot_tn(a, da), acc_ref.at[:, cols]

            @pl.when(i == 0)
            def _():
                dst[...] = part

            @pl.when(i > 0)
            def _():
                dst[...] += part

        if kind == "w1":
            x_ref[...] = dh.astype(BF16)

        @pl.when(i == n_steps - 1)
        def _():
            dw_ref[...] = acc_ref[...].astype(BF16)

        if nc:
            _carry_end(carry, c_src, c_dst, sems, step, halves * n_steps)

    if kind == "w2":
        wshape = (ffh, D)
        a_spec = pl.BlockSpec((tm, D), lambda j, i: (i, 0))
        b_spec = pl.BlockSpec((tm, ffh), lambda j, i: (i, j))
        x_spec, x_cols = pl.BlockSpec((tm, ffh), lambda j, i: (i, j)), FF
    else:
        wshape = (dh_cols, FF)
        a_spec = pl.BlockSpec((tm, dh_cols), lambda j, i: (i, j))
        b_spec = pl.BlockSpec((tm, FF), lambda j, i: (i, 0))
        x_spec, x_cols = pl.BlockSpec((tm, dh_cols), lambda j, i: (i, j)), D
    wspec = pl.BlockSpec(wshape, lambda j, i: (j, 0), pipeline_mode=pl.Buffered(1))
    res = pl.pallas_call(
        body, name=name, grid=(halves, n_steps),
        in_specs=[a_spec, b_spec, wspec] + [ANY] * nc,
        out_specs=[x_spec, wspec] + [ANY] * nc,
        out_shape=[jax.ShapeDtypeStruct((T, x_cols), BF16), jax.ShapeDtypeStruct(weight.shape, BF16)]
        + (_carry_out_shapes(carry) if nc else []),
        scratch_shapes=_carry_scratch(nc) + [pltpu.VMEM(wshape, F32)],
        compiler_params=_params(("arbitrary", "arbitrary")),
    )(first, second, weight, *(carry.srcs if nc else ()))
    return res[:2], res[2:]


def _norm_mod_fwd(cfg, name, x, gain, mod, ish, isc):
    T, D, TM = cfg.T, cfg.D, cfg.TM

    def body(x_ref, g_ref, mod_ref, h_ref):
        x = x_ref[...]
        rstd = lax.rsqrt(jnp.mean(x * x, axis=-1, keepdims=True) + EPS)
        n = x * rstd * g_ref[...]
        h = n * (1.0 + mod_ref[pl.ds(isc, 1), :]) + mod_ref[pl.ds(ish, 1), :]
        h_ref[...] = h.astype(BF16)

    return pl.pallas_call(
        body, name=name, grid=(T // TM,),
        in_specs=[pl.BlockSpec((TM, D), lambda i: (i, 0)), pl.BlockSpec((1, D), lambda i: (0, 0)),
                  pl.BlockSpec((None, None, 6, D), _mod_index(cfg))],
        out_specs=pl.BlockSpec((TM, D), lambda i: (i, 0)),
        out_shape=jax.ShapeDtypeStruct((T, D), BF16),
        compiler_params=_params(("parallel",)),
    )(x, gain, mod)


def _norm_mod_bwd(cfg, name, x, gain, mod, ish, isc, dh, dres, gate=None, lat_only=False):
    T, D, TM, TPE = cfg.T, cfg.D, cfg.TM, cfg.TPE
    ng = 2 if gate is not None else 0
    dx_spec = (pl.BlockSpec((TM, D), lambda i: ((i // TPE) * (TPE - 1) + jnp.maximum(i % TPE - 1, 0), 0)) if lat_only
               else pl.BlockSpec((TM, D), lambda i: (i, 0)))
    dx_rows = cfg.B * cfg.SL if lat_only else T

    def body(*refs):
        x_ref, g_ref, mod_ref, dh_ref, dres_ref = refs[:5]
        dx_ref, dmod_ref, dgain_ref = refs[5 + ng:8 + ng]
        i = pl.program_id(0)
        t = i % TPE
        x = x_ref[...]
        g = g_ref[...]
        dh = dh_ref[...].astype(F32)
        rstd = lax.rsqrt(jnp.mean(x * x, axis=-1, keepdims=True) + EPS)
        xhat = x * rstd
        dn = dh * (1.0 + mod_ref[pl.ds(isc, 1), :])
        dsh = jnp.sum(dh, axis=0, keepdims=True)
        dsc = jnp.sum(dh * (xhat * g), axis=0, keepdims=True)
        dgain = jnp.sum(dn * xhat, axis=0, keepdims=True)
        dxh = dn * g
        dx = rstd * (dxh - xhat * jnp.mean(dxh * xhat, axis=-1, keepdims=True)) + dres_ref[...]
        dx_ref[...] = dx
        sums = [(dmod_ref.at[pl.ds(0, 1), :], dsh), (dmod_ref.at[pl.ds(1, 1), :], dsc)]
        if ng:
            y_ref, gmod_ref = refs[5:7]
            dy_ref, dgate_ref = refs[8 + ng:]
            dy_ref[...] = (dx * gmod_ref[pl.ds(gate[2], 1), :]).astype(BF16)
            sums.append((dgate_ref, jnp.sum(dx * y_ref[...].astype(F32), axis=0, keepdims=True)))

        @pl.when(t <= 1)
        def _():
            for ref, val in sums:
                ref[...] = val

        @pl.when(t > 1)
        def _():
            for ref, val in sums:
                ref[...] += val

        @pl.when(i == 0)
        def _():
            dgain_ref[...] = dgain

        @pl.when(i > 0)
        def _():
            dgain_ref[...] += dgain

    tok = pl.BlockSpec((TM, D), lambda i: (i, 0))
    mod_spec = pl.BlockSpec((None, None, 6, D), _mod_index(cfg))
    res = pl.pallas_call(
        body, name=name, grid=(T // TM,),
        in_specs=[tok, pl.BlockSpec((1, D), lambda i: (0, 0)), mod_spec, tok, tok] + ([tok, mod_spec] if ng else []),
        out_specs=[dx_spec, pl.BlockSpec((None, None, 2, D), _mod_index(cfg)), pl.BlockSpec((1, D), lambda i: (0, 0))]
        + ([tok, pl.BlockSpec((None, None, 1, D), _mod_index(cfg))] if ng else []),
        out_shape=[jax.ShapeDtypeStruct((dx_rows, D), F32), jax.ShapeDtypeStruct((cfg.B, 2, 2, D), F32),
                   jax.ShapeDtypeStruct((1, D), F32)]
        + ([jax.ShapeDtypeStruct((T, D), BF16), jax.ShapeDtypeStruct((cfg.B, 2, 1, D), F32)] if ng else []),
        compiler_params=_params(("arbitrary",)),
    )(x, gain, mod, dh, dres, *(gate[:2] if ng else ()))
    return res


def _gate_bwd(cfg, name, dx, y, mod, igate):
    T, D, TM, TPE = cfg.T, cfg.D, cfg.TM, cfg.TPE

    def body(dx_ref, y_ref, mod_ref, dy_ref, dg_ref):
        t = pl.program_id(0) % TPE
        dx = dx_ref[...]
        dy_ref[...] = (dx * mod_ref[pl.ds(igate, 1), :]).astype(BF16)
        dg = jnp.sum(dx * y_ref[...].astype(F32), axis=0, keepdims=True)

        @pl.when(t <= 1)
        def _():
            dg_ref[...] = dg

        @pl.when(t > 1)
        def _():
            dg_ref[...] += dg

    tok = pl.BlockSpec((TM, D), lambda i: (i, 0))
    return pl.pallas_call(
        body, name=name, grid=(T // TM,),
        in_specs=[tok, tok, pl.BlockSpec((None, None, 6, D), _mod_index(cfg))],
        out_specs=[tok, pl.BlockSpec((None, None, 1, D), _mod_index(cfg))],
        out_shape=[jax.ShapeDtypeStruct((T, D), BF16), jax.ShapeDtypeStruct((cfg.B, 2, 1, D), F32)],
        compiler_params=_params(("arbitrary",)),
    )(dx, y, mod)


def _loss_grad(cfg, x, tgt):
    T, D, TM, TPE = cfg.T, cfg.D, cfg.TM, cfg.TPE

    def body(x_ref, t_ref, dx_ref, loss_ref):
        i = pl.program_id(0)
        t = i % TPE

        @pl.when(i == 0)
        def _():
            loss_ref[...] = jnp.zeros_like(loss_ref)

        @pl.when(t == 0)
        def _():
            dx_ref[...] = jnp.zeros_like(dx_ref)

        @pl.when(t > 0)
        def _():
            err = x_ref[...] - t_ref[...]
            dx_ref[...] = err * (1.0 / D)
            loss_ref[...] += 0.5 * jnp.sum(jnp.mean(err * err, axis=-1, keepdims=True), axis=0, keepdims=True)

    tok = pl.BlockSpec((TM, D), lambda i: (i, 0))
    tgt_spec = pl.BlockSpec((TM, D), lambda i: ((i // TPE) * (TPE - 1) + jnp.maximum(i % TPE - 1, 0), 0))
    dx, loss = pl.pallas_call(
        body, name="loss_grad", grid=(T // TM,),
        in_specs=[tok, tgt_spec], out_specs=[tok, pl.BlockSpec((8, LANES), lambda i: (0, 0))],
        out_shape=[jax.ShapeDtypeStruct((T, D), F32), jax.ShapeDtypeStruct((8, LANES), F32)],
        compiler_params=_params(("arbitrary",)),
    )(x, tgt)
    return loss[0, 0], dx


def _rope_tables(cfg):
    rows = cfg.SL // cfg.GRID_W
    row = jnp.repeat(jnp.arange(rows, dtype=F32), cfg.GRID_W)
    col = jnp.tile(jnp.arange(cfg.GRID_W, dtype=F32), rows)
    n_freq = CHUNK // 4
    inv = ROPE_BASE ** (-jnp.arange(n_freq, dtype=F32) / n_freq)
    ang = jnp.concatenate([row[:, None] * inv[None, :], col[:, None] * inv[None, :]], axis=-1)
    cos, sin = jnp.cos(ang), jnp.sin(ang)
    cosf = jnp.concatenate([jnp.ones((cfg.SC, CHUNK), F32), jnp.concatenate([cos, cos], axis=-1)], axis=0)
    sinf = jnp.concatenate([jnp.zeros((cfg.SC, CHUNK), F32), jnp.concatenate([-sin, sin], axis=-1)], axis=0)
    return cosf, sinf


def _rope(x, cosf, sinf):
    return x * cosf + pltpu.roll(x, CHUNK // 2, 1) * sinf


def _irope(dy, cosf, sinf):
    return dy * cosf - pltpu.roll(dy, CHUNK // 2, 1) * sinf


def _prep_fwd(cfg, name, p, cosf, sinf, qg, kg):
    T, TM, TPE, H, KV = cfg.T, cfg.TM, cfg.TPE, cfg.H, cfg.KV
    HW = H * CHUNK
    kscale = CHUNK ** -0.5

    def body(p_ref, c_ref, s_ref, qg_ref, kg_ref, rq_ref, rk_ref, aq_ref, ak_ref):
        cosf, sinf = c_ref[...], s_ref[...]

        def normed(x, g):
            return x * lax.rsqrt(jnp.mean(x * x, axis=-1, keepdims=True) + EPS) * g

        def seg(col):
            return p_ref[:, pl.ds(col, CHUNK)].astype(F32)

        for h in range(H):
            sl = pl.ds(h * CHUNK, CHUNK)
            rq_ref[:, sl] = _rope(seg(h * CHUNK), cosf, sinf)
            rk_ref[:, sl] = _rope(seg(HW + h * CHUNK), cosf, sinf) * kscale
            aq_ref[:, sl] = (_rope(normed(seg(4 * HW + h * CHUNK), qg_ref[...]), cosf, sinf) * ATT_SCALE).astype(BF16)
        for h in range(KV):
            ak_ref[:, pl.ds(h * CHUNK, CHUNK)] = _rope(
                normed(seg(5 * HW + h * CHUNK), kg_ref[...]), cosf, sinf).astype(BF16)

    tab = pl.BlockSpec((TM, CHUNK), lambda i: (i % TPE, 0))
    vec = pl.BlockSpec((1, CHUNK), lambda i: (0, 0))
    return pl.pallas_call(
        body, name=name, grid=(T // TM,),
        in_specs=[pl.BlockSpec((TM, cfg.ABW), lambda i: (i, 0)), tab, tab, vec, vec],
        out_specs=[pl.BlockSpec((TM, HW), lambda i: (i, 0))] * 3 + [pl.BlockSpec((TM, KV * CHUNK), lambda i: (i, 0))],
        out_shape=[jax.ShapeDtypeStruct((T, HW), F32), jax.ShapeDtypeStruct((T, HW), F32),
                   jax.ShapeDtypeStruct((T, HW), BF16), jax.ShapeDtypeStruct((T, KV * CHUNK), BF16)],
        compiler_params=_params(("parallel",)),
    )(p, cosf, sinf, qg, kg)


def _prep_bwd(cfg, name, p, cosf, sinf, qg, kg, d_rq, d_rk, d_rv, d_gate, d_aq, d_ak, d_av):
    T, TM, TPE, H, KV = cfg.T, cfg.TM, cfg.TPE, cfg.H, cfg.KV
    HW = H * CHUNK
    kscale = CHUNK ** -0.5

    def body(p_ref, c_ref, s_ref, qg_ref, kg_ref, drq_ref, drk_ref, drv_ref, dgt_ref, daq_ref, dak_ref, dav_ref,
             dp_ref, dqg_ref, dkg_ref):
        i = pl.program_id(0)
        cosf, sinf = c_ref[...], s_ref[...]

        def norm_bwd(x, g, dn):
            rstd = lax.rsqrt(jnp.mean(x * x, axis=-1, keepdims=True) + EPS)
            xhat = x * rstd
            dg = jnp.sum(dn * xhat, axis=0, keepdims=True)
            dxh = dn * g
            return rstd * (dxh - xhat * jnp.mean(dxh * xhat, axis=-1, keepdims=True)), dg

        dqg = jnp.zeros((1, CHUNK), F32)
        dkg = jnp.zeros((1, CHUNK), F32)
        for h in range(H):
            sl = pl.ds(h * CHUNK, CHUNK)
            dp_ref[:, pl.ds(h * CHUNK, CHUNK)] = _irope(drq_ref[:, sl].astype(F32), cosf, sinf).astype(BF16)
            dp_ref[:, pl.ds(HW + h * CHUNK, CHUNK)] = (_irope(drk_ref[:, sl].astype(F32), cosf, sinf)
                                                       * kscale).astype(BF16)
            dp_ref[:, pl.ds(2 * HW + h * CHUNK, CHUNK)] = drv_ref[:, sl].astype(BF16)
            dp_ref[:, pl.ds(3 * HW + h * CHUNK, CHUNK)] = dgt_ref[:, sl].astype(BF16)
            dx, dg = norm_bwd(p_ref[:, pl.ds(4 * HW + h * CHUNK, CHUNK)].astype(F32), qg_ref[...],
                              _irope(daq_ref[:, sl].astype(F32), cosf, sinf))
            dp_ref[:, pl.ds(4 * HW + h * CHUNK, CHUNK)] = dx.astype(BF16)
            dqg = dqg + dg
        for h in range(KV):
            sl = pl.ds(h * CHUNK, CHUNK)
            dx, dg = norm_bwd(p_ref[:, pl.ds(5 * HW + h * CHUNK, CHUNK)].astype(F32), kg_ref[...],
                              _irope(dak_ref[:, sl], cosf, sinf))
            dp_ref[:, pl.ds(5 * HW + h * CHUNK, CHUNK)] = dx.astype(BF16)
            dp_ref[:, pl.ds(5 * HW + (KV + h) * CHUNK, CHUNK)] = dav_ref[:, sl].astype(BF16)
            dkg = dkg + dg

        @pl.when(i == 0)
        def _():
            dqg_ref[...] = dqg
            dkg_ref[...] = dkg

        @pl.when(i > 0)
        def _():
            dqg_ref[...] += dqg
            dkg_ref[...] += dkg

    tab = pl.BlockSpec((TM, CHUNK), lambda i: (i % TPE, 0))
    vec = pl.BlockSpec((1, CHUNK), lambda i: (0, 0))
    hw = pl.BlockSpec((TM, HW), lambda i: (i, 0))
    kvw = pl.BlockSpec((TM, KV * CHUNK), lambda i: (i, 0))
    return pl.pallas_call(
        body, name=name, grid=(T // TM,),
        in_specs=[pl.BlockSpec((TM, cfg.ABW), lambda i: (i, 0)), tab, tab, vec, vec, hw, hw, hw, hw, hw, kvw, kvw],
        out_specs=[pl.BlockSpec((TM, cfg.ABW), lambda i: (i, 0)), vec, vec],
        out_shape=[jax.ShapeDtypeStruct((T, cfg.ABW), BF16), jax.ShapeDtypeStruct((1, CHUNK), F32),
                   jax.ShapeDtypeStruct((1, CHUNK), F32)],
        compiler_params=_params(("arbitrary",)),
    )(p, cosf, sinf, qg, kg, d_rq, d_rk, d_rv, d_gate, d_aq, d_ak, d_av)


def _ret_consts(direction, lg):
    C = CHUNK
    ii = lax.broadcasted_iota(jnp.int32, (C, C), 0)
    jj = lax.broadcasted_iota(jnp.int32, (C, C), 1)
    col = lax.broadcasted_iota(jnp.int32, (C, 1), 0).astype(F32)
    if direction == 0:
        mask, er, ek, eq = ii >= jj, (ii - jj).astype(F32), (C - 1.0) - col, col + 1.0
    else:
        mask, er, ek, eq = jj >= ii, (jj - ii).astype(F32), col, C - col
    er = jnp.where(mask, er, 0.0)
    dm = jnp.where(mask, jnp.exp(er * lg), 0.0)
    return dm, er, jnp.exp(ek * lg), ek, jnp.exp(eq * lg), eq, jnp.exp(C * lg)


def _ret_order(cfg, direction):
    n_all, n_ctx = cfg.S // CHUNK, cfg.SC // CHUNK
    if direction == 0:
        return list(range(n_all))
    return list(range(n_ctx - 1, -1, -1)) + list(range(n_all - 1, n_ctx - 1, -1))


def _carry_begin(carry, c_src, c_dst, sems, step):
    @pl.when(step == 0)
    def _():
        for cp in _carry_copies(carry, c_src, c_dst, *sems):
            cp.start()


def _carry_end(carry, c_src, c_dst, sems, step, n_steps):
    @pl.when(step == n_steps - 1)
    def _():
        for cp in _carry_copies(carry, c_src, c_dst, *sems):
            cp.wait()


def _carry_scratch(nc):
    return [pltpu.SemaphoreType.DMA((3 * nc,)), pltpu.SemaphoreType.DMA((3 * nc,)),
            pltpu.SemaphoreType.DMA((nc,))] if nc else []


def _head_norm_gate(o, g):
    mu = jnp.mean(o, axis=-1, keepdims=True)
    var = jnp.mean(jnp.square(o - mu), axis=-1, keepdims=True)
    rstd = lax.rsqrt(var + EPS)
    y = (o - mu) * rstd
    sg = jax.nn.sigmoid(g)
    return y, rstd, sg


RET_UNROLL = 3


def _retention_fwd(cfg, name, rq, rk, p, lgb, carry=None):
    B, H, S, T = cfg.B, cfg.H, cfg.S, cfg.T
    n_all = S // CHUNK
    nc = len(carry.srcs) if carry is not None else 0

    def body(*refs):
        q_ref, k_ref, v_ref, g_ref, lg_ref = refs[:5]
        c_src = refs[5:5 + nc]
        o_ref, ret_ref, st_ref = refs[5 + nc:8 + nc]
        c_dst = refs[8 + nc:8 + 2 * nc]
        sems = refs[8 + 2 * nc:8 + 2 * nc + 3] if nc else ()
        kv_ref = refs[-1]
        step = pl.program_id(0) * H + pl.program_id(1)
        if nc:
            _carry_begin(carry, c_src, c_dst, sems, step)

        def rows(n):
            return pl.ds(pl.multiple_of(n * CHUNK, CHUNK), CHUNK)

        (dm0, _, kd0, _, qd0, _, cd0), (dm1, _, kd1, _, qd1, _, cd1) = (
            _ret_consts(d, lg_ref[d, 0:1, 0:1]) for d in (0, 1))
        dm_both = dm0 + dm1

        def kv_step(n, c):
            k = k_ref[rows(n), :]
            v = v_ref[rows(n), :].astype(BF16)
            kv_ref[0, n] = _dot_tn((k * kd0).astype(BF16), v)
            kv_ref[1, n] = _dot_tn((k * kd1).astype(BF16), v)
            return c

        lax.fori_loop(0, n_all, kv_step, 0, unroll=RET_UNROLL)
        for direction, cd in ((0, cd0), (1, cd1)):
            st = jnp.zeros((CHUNK, CHUNK), F32)
            for t, n in enumerate(_ret_order(cfg, direction)):
                st_ref[direction, n] = st
                if t + 1 < n_all:
                    st = cd * st + kv_ref[direction, n]

        def out_step(n, c):
            q = q_ref[rows(n), :].astype(BF16)
            v = v_ref[rows(n), :].astype(BF16)
            s = _dot_nt(q, k_ref[rows(n), :].astype(BF16)) * dm_both
            states = jnp.concatenate([st_ref[0, n].astype(BF16), st_ref[1, n].astype(BF16)], axis=1)
            cross = _dot(q, states)
            o = _dot(s.astype(BF16), v) + cross[:, :CHUNK] * qd0 + cross[:, CHUNK:] * qd1
            o_ref[rows(n), :] = o
            g = g_ref[rows(n), :].astype(F32)
            y, _, sg = _head_norm_gate(o, g)
            ret_ref[rows(n), :] = (y * (g * sg)).astype(BF16)
            return c

        lax.fori_loop(0, n_all, out_step, 0, unroll=RET_UNROLL)
        if nc:
            _carry_end(carry, c_src, c_dst, sems, step, B * H)

    HW = H * CHUNK
    blk = lambda off: pl.BlockSpec((S, CHUNK), lambda b, h: (b, off + h))
    st_spec = pl.BlockSpec((None, None, 2, n_all, CHUNK, CHUNK), lambda b, h: (b, h, 0, 0, 0, 0))
    res = pl.pallas_call(
        body, name=name, grid=(B, H),
        in_specs=[blk(0), blk(0), blk(2 * H), blk(3 * H),
                  pl.BlockSpec((None, 2, 8, LANES), lambda b, h: (h, 0, 0, 0))] + [ANY] * nc,
        out_specs=[blk(0), blk(0), st_spec] + [ANY] * nc,
        out_shape=[jax.ShapeDtypeStruct((T, HW), F32), jax.ShapeDtypeStruct((T, 2 * HW), BF16),
                   jax.ShapeDtypeStruct((B, H, 2, n_all, CHUNK, CHUNK), F32)] + (_carry_out_shapes(carry) if nc else []),
        scratch_shapes=_carry_scratch(nc) + [pltpu.VMEM((2, n_all, CHUNK, CHUNK), F32)],
        compiler_params=_params(("arbitrary", "arbitrary") if nc else ("parallel", "parallel")),
    )(rq, rk, p, p, lgb, *(carry.srcs if nc else ()))
    return res[:3], res[3:]


def _retention_bwd(cfg, name, rq, rk, p, o_sum, states, dcat, lgb, carry=None):
    B, H, S, T = cfg.B, cfg.H, cfg.S, cfg.T
    n_all = S // CHUNK
    C = CHUNK
    nc = len(carry.srcs) if carry is not None else 0

    def body(*refs):
        q_ref, k_ref, v_ref, g_ref, o_ref, st_ref, dr_ref, lg_ref = refs[:8]
        c_src = refs[8:8 + nc]
        dq_ref, dk_ref, dv_ref, dg_ref, dlg_ref = refs[8 + nc:13 + nc]
        c_dst = refs[13 + nc:13 + 2 * nc]
        sems = refs[13 + 2 * nc:13 + 2 * nc + 3] if nc else ()
        do_ref, gq_ref, ds_ref, acc_ref = refs[-4:]
        step = pl.program_id(0) * H + pl.program_id(1)
        if nc:
            _carry_begin(carry, c_src, c_dst, sems, step)

        def rows(n):
            return pl.ds(pl.multiple_of(n * C, C), C)

        (dm0, er0, kd0, ek0, qd0, eq0, cd0), (dm1, er1, kd1, ek1, qd1, eq1, cd1) = (
            _ret_consts(d, lg_ref[d, 0:1, 0:1]) for d in (0, 1))
        dm_both = dm0 + dm1
        wdm0, wdm1 = dm0 * er0, dm1 * er1

        def side(a, b):
            return jnp.concatenate([a.astype(BF16), b.astype(BF16)], axis=1)

        def gq_step(n, c):
            g = g_ref[rows(n), :].astype(F32)
            dr = dr_ref[rows(n), :].astype(F32)
            y, rstd, sg = _head_norm_gate(o_ref[rows(n), :], g)
            dy = dr * (g * sg)
            dg_ref[rows(n), :] = (dr * y * (sg * (1.0 + g * (1.0 - sg)))).astype(BF16)
            do = rstd * (dy - jnp.mean(dy, axis=-1, keepdims=True) - y * jnp.mean(dy * y, axis=-1, keepdims=True))
            do_ref[rows(n), :] = do
            gq = _dot_tn(q_ref[rows(n), :].astype(BF16), side(do * qd0, do * qd1))
            gq_ref[0, n] = gq[:, :C]
            gq_ref[1, n] = gq[:, C:]
            return c

        lax.fori_loop(0, n_all, gq_step, 0, unroll=RET_UNROLL)
        for direction, cd in ((0, cd0), (1, cd1)):
            order = _ret_order(cfg, direction)
            ds = jnp.zeros((C, C), F32)
            for t in reversed(range(n_all)):
                ds_ref[direction, order[t]] = ds
                if t > 0:
                    ds = cd * ds + gq_ref[direction, order[t]]
        acc_ref[...] = jnp.zeros_like(acc_ref)

        def chunk_step(n, c):
            q = q_ref[rows(n), :].astype(BF16)
            kf = k_ref[rows(n), :]
            k = kf.astype(BF16)
            v = v_ref[rows(n), :].astype(BF16)
            do = do_ref[rows(n), :]
            dob = do.astype(BF16)
            sp0, sp1 = st_ref[0, n], st_ref[1, n]
            ds0, ds1 = ds_ref[0, n], ds_ref[1, n]
            states = side(sp0, sp1)
            dstates = jnp.concatenate([ds0.astype(BF16), ds1.astype(BF16)], axis=0)
            doq0, doq1 = do * qd0, do * qd1
            doq = side(doq0, doq1)
            s_raw = _dot_nt(q, k)
            dpm = _dot_nt(dob, v)
            dsr = (dpm * dm_both).astype(BF16)
            dks = _dot_nt(v, dstates)
            dks0, dks1 = dks[:, :C] * kd0, dks[:, C:] * kd1
            qs = _dot(q, states)
            dq_ref[rows(n), :] = (_dot(dsr, k) + _dot_nt(doq, states)).astype(BF16)
            dk_ref[rows(n), :] = (_dot_tn(dsr, q) + dks0 + dks1).astype(BF16)
            dv_ref[rows(n), :] = (_dot_tn((s_raw * dm_both).astype(BF16), dob)
                                  + _dot(side(kf * kd0, kf * kd1), dstates)).astype(BF16)
            inner = dpm * s_raw
            acc_ref[0] += (jnp.sum(inner * wdm0, axis=0, keepdims=True)
                           + jnp.sum(eq0 * doq0 * qs[:, :C], axis=0, keepdims=True)
                           + jnp.sum(ek0 * kf * dks0, axis=0, keepdims=True)
                           + (C * cd0) * jnp.sum(ds0 * sp0, axis=0, keepdims=True))
            acc_ref[1] += (jnp.sum(inner * wdm1, axis=0, keepdims=True)
                           + jnp.sum(eq1 * doq1 * qs[:, C:], axis=0, keepdims=True)
                           + jnp.sum(ek1 * kf * dks1, axis=0, keepdims=True)
                           + (C * cd1) * jnp.sum(ds1 * sp1, axis=0, keepdims=True))
            return c

        lax.fori_loop(0, n_all, chunk_step, 0, unroll=RET_UNROLL)
        for direction in (0, 1):
            dlg_ref[direction] = jnp.broadcast_to(jnp.sum(acc_ref[direction], axis=1, keepdims=True), (8, LANES))
        if nc:
            _carry_end(carry, c_src, c_dst, sems, step, B * H)

    HW = H * CHUNK
    blk = lambda off: pl.BlockSpec((S, CHUNK), lambda b, h: (b, off + h))
    st_spec = pl.BlockSpec((None, None, 2, n_all, C, C), lambda b, h: (b, h, 0, 0, 0, 0))
    res = pl.pallas_call(
        body, name=name, grid=(B, H),
        in_specs=[blk(0), blk(0), blk(2 * H), blk(3 * H), blk(0), st_spec, blk(0),
                  pl.BlockSpec((None, 2, 8, LANES), lambda b, h: (h, 0, 0, 0))] + [ANY] * nc,
        out_specs=[blk(0)] * 4 + [pl.BlockSpec((None, None, 2, 8, LANES), lambda b, h: (b, h, 0, 0, 0))] + [ANY] * nc,
        out_shape=[jax.ShapeDtypeStruct((T, HW), BF16)] * 4 + [jax.ShapeDtypeStruct((B, H, 2, 8, LANES), F32)]
        + (_carry_out_shapes(carry) if nc else []),
        scratch_shapes=_carry_scratch(nc) + [pltpu.VMEM((S, CHUNK), F32), pltpu.VMEM((2, n_all, C, C), F32),
                                             pltpu.VMEM((2, n_all, C, C), F32), pltpu.VMEM((2, 1, C), F32)],
        compiler_params=_params(("arbitrary", "arbitrary") if nc else ("parallel", "parallel")),
    )(rq, rk, p, p, o_sum, states, dcat, lgb, *(carry.srcs if nc else ()))
    return res[:5], res[5:]


ATT_SCALE = CHUNK ** -0.5


def _attn_scores(cfg, q, k, t):
    kcol = lax.broadcasted_iota(jnp.int32, (1, cfg.S), 1)
    bias = jnp.where(jnp.logical_or(t > 0, kcol < cfg.SC), 0.0, -1e30)
    return _dot_nt(q, k) + bias


def _attention_fwd(cfg, name, aq, ak, p, cat, carry=None):
    B, H, KV, S, T, TM, TPE = cfg.B, cfg.H, cfg.KV, cfg.S, cfg.T, cfg.TM, cfg.TPE
    G = H // KV
    v_off = (5 * H + KV)
    nc = len(carry.srcs) if carry is not None else 0

    def body(*refs):
        q_ref, k_ref, v_ref = refs[:3]
        o_ref, lse_ref = refs[4 + nc:6 + nc]
        c_src, c_dst, sems = refs[4:4 + nc], refs[6 + nc:6 + 2 * nc], refs[6 + 2 * nc:]
        step = (pl.program_id(0) * KV + pl.program_id(1)) * TPE + pl.program_id(2)
        if nc:
            _carry_begin(carry, c_src, c_dst, sems, step)
        k = k_ref[...]
        v = v_ref[...].astype(BF16)
        for g in range(G):
            cols = pl.ds(g * CHUNK, CHUNK)
            s = _attn_scores(cfg, q_ref[:, cols], k, pl.program_id(2))
            m = jnp.max(s, axis=-1, keepdims=True)
            e = jnp.exp(s - m)
            total = jnp.sum(e, axis=-1, keepdims=True)
            o_ref[:, cols] = (_dot(e.astype(BF16), v) * (1.0 / total)).astype(BF16)
            lse_ref[g] = m + jnp.log(total)
        if nc:
            _carry_end(carry, c_src, c_dst, sems, step, B * KV * TPE)

    res = pl.pallas_call(
        body, name=name, grid=(B, KV, TPE),
        in_specs=[pl.BlockSpec((TM, G * CHUNK), lambda b, kv, t: (b * TPE + t, kv)),
                  pl.BlockSpec((S, CHUNK), lambda b, kv, t: (b, kv)),
                  pl.BlockSpec((S, CHUNK), lambda b, kv, t: (b, v_off + kv)), ANY] + [ANY] * nc,
        out_specs=[pl.BlockSpec((TM, G * CHUNK), lambda b, kv, t: (b * TPE + t, KV + kv)),
                   pl.BlockSpec((G, TM, 1), lambda b, kv, t: (kv, b * TPE + t, 0))] + [ANY] * nc,
        out_shape=[jax.ShapeDtypeStruct(cat.shape, cat.dtype), jax.ShapeDtypeStruct((H, T, 1), F32)]
        + (_carry_out_shapes(carry) if nc else []),
        input_output_aliases={3: 0},
        scratch_shapes=_carry_scratch(nc),
        compiler_params=_params(("arbitrary",) * 3 if nc else ("parallel",) * 3),
    )(aq, ak, p, cat, *(carry.srcs if nc else ()))
    return res[:2], res[2:]


def _attention_bwd(cfg, name, aq, ak, p, cat, lse, dcat, carry=None):
    B, H, KV, S, T, TM, TPE = cfg.B, cfg.H, cfg.KV, cfg.S, cfg.T, cfg.TM, cfg.TPE
    G = H // KV
    v_off = (5 * H + KV)
    nc = len(carry.srcs) if carry is not None else 0

    def body(*refs):
        q_ref, k_ref, v_ref, o_ref, lse_ref, do_ref = refs[:6]
        dq_ref, dk_ref, dv_ref = refs[6 + nc:9 + nc]
        c_src, c_dst, sems = refs[6:6 + nc], refs[9 + nc:9 + 2 * nc], refs[9 + 2 * nc:]
        t = pl.program_id(2)
        step = (pl.program_id(0) * KV + pl.program_id(1)) * TPE + t
        if nc:
            _carry_begin(carry, c_src, c_dst, sems, step)
        k = k_ref[...]
        v = v_ref[...].astype(BF16)
        dk = dv = None
        for g in range(G):
            cols = pl.ds(g * CHUNK, CHUNK)
            q = q_ref[:, cols]
            do = do_ref[:, cols]
            pr = jnp.exp(_attn_scores(cfg, q, k, t) - lse_ref[g])
            delta = jnp.sum(do.astype(F32) * o_ref[:, cols].astype(F32), axis=-1, keepdims=True)
            ds = (pr * (_dot_nt(do, v) - delta)).astype(BF16)
            dq_ref[:, cols] = (_dot(ds, k) * ATT_SCALE).astype(BF16)
            dk_g, dv_g = _dot_tn(ds, q), _dot_tn(pr.astype(BF16), do)
            dk, dv = (dk_g, dv_g) if dk is None else (dk + dk_g, dv + dv_g)

        @pl.when(t == 0)
        def _():
            dk_ref[...] = dk
            dv_ref[...] = dv

        @pl.when(t > 0)
        def _():
            dk_ref[...] += dk
            dv_ref[...] += dv

        if nc:
            _carry_end(carry, c_src, c_dst, sems, step, B * KV * TPE)

    qspec = pl.BlockSpec((TM, G * CHUNK), lambda b, kv, t: (b * TPE + t, kv))
    kvspec = pl.BlockSpec((S, CHUNK), lambda b, kv, t: (b, kv))
    right = pl.BlockSpec((TM, G * CHUNK), lambda b, kv, t: (b * TPE + t, KV + kv))
    res = pl.pallas_call(
        body, name=name, grid=(B, KV, TPE),
        in_specs=[qspec, kvspec, pl.BlockSpec((S, CHUNK), lambda b, kv, t: (b, v_off + kv)), right,
                  pl.BlockSpec((G, TM, 1), lambda b, kv, t: (kv, b * TPE + t, 0)), right] + [ANY] * nc,
        out_specs=[qspec, kvspec, kvspec] + [ANY] * nc,
        out_shape=[jax.ShapeDtypeStruct((T, H * CHUNK), BF16), jax.ShapeDtypeStruct((T, KV * CHUNK), F32),
                   jax.ShapeDtypeStruct((T, KV * CHUNK), F32)] + (_carry_out_shapes(carry) if nc else []),
        scratch_shapes=_carry_scratch(nc),
        compiler_params=_params(("arbitrary",) * 3 if nc else ("parallel", "parallel", "arbitrary")),
    )(aq, ak, p, cat, lse, dcat, *(carry.srcs if nc else ()))
    return res[:3], res[3:]


_GELU_C = math.sqrt(2.0 / math.pi)


def _gelu(x):
    return 0.5 * x * (1.0 + jnp.tanh(_GELU_C * (x + 0.044715 * x * x * x)))


def _gelu_and_grad(x):
    x2 = x * x
    th = jnp.tanh(_GELU_C * (x + 0.044715 * x * x2))
    half = 0.5 * (1.0 + th)
    return x * half, half + 0.5 * x * (1.0 - th * th) * _GELU_C * (1.0 + 3.0 * 0.044715 * x2)


def _cm_fwd(cfg, name, a, vg, ws, bs):
    T, TM, W, NG = cfg.T, cfg.TM, cfg.CMW, cfg.CMG

    def body(a_ref, vg_ref, ws_ref, bs_ref, m_ref):
        v = _gelu(a_ref[:, pl.ds(W, W)].astype(F32))
        vn = (v * lax.rsqrt(jnp.mean(v * v, axis=-1, keepdims=True) + EPS) * vg_ref[...]).astype(BF16)
        for c in range(TM // CHUNK):
            for g in range(NG):
                rows, cols = slice(c * CHUNK, (c + 1) * CHUNK), slice(g * CHUNK, (g + 1) * CHUNK)
                sv = _dot(ws_ref[g].astype(BF16), vn[rows, cols]) + bs_ref[g]
                u = _gelu(a_ref[pl.ds(c * CHUNK, CHUNK), pl.ds(g * CHUNK, CHUNK)].astype(F32))
                m_ref[pl.ds(c * CHUNK, CHUNK), pl.ds(g * CHUNK, CHUNK)] = (u * sv).astype(BF16)

    return pl.pallas_call(
        body, name=name, grid=(T // TM,),
        in_specs=[pl.BlockSpec((TM, 2 * W), lambda i: (i, 0)), pl.BlockSpec((1, W), lambda i: (0, 0)),
                  pl.BlockSpec((NG, CHUNK, CHUNK), lambda i: (0, 0, 0)),
                  pl.BlockSpec((NG, CHUNK, 1), lambda i: (0, 0, 0))],
        out_specs=pl.BlockSpec((TM, W), lambda i: (i, 0)),
        out_shape=jax.ShapeDtypeStruct((T, W), BF16),
        compiler_params=_params(("parallel",)),
    )(a, vg, ws, bs)


def _cm_bwd(cfg, name, a, vg, ws, bs, dm):
    T, TM, W, NG = cfg.T, cfg.TM, cfg.CMW, cfg.CMG

    def body(a_ref, vg_ref, ws_ref, bs_ref, dm_ref, da_ref, dws_ref, dbs_ref, dvg_ref, dvn_ref):
        i = pl.program_id(0)

        @pl.when(i == 0)
        def _():
            dws_ref[...] = jnp.zeros_like(dws_ref)
            dbs_ref[...] = jnp.zeros_like(dbs_ref)
            dvg_ref[...] = jnp.zeros_like(dvg_ref)

        v, v_grad = _gelu_and_grad(a_ref[:, pl.ds(W, W)].astype(F32))
        rstd = lax.rsqrt(jnp.mean(v * v, axis=-1, keepdims=True) + EPS)
        xhat = v * rstd
        vg = vg_ref[...]
        vn = (xhat * vg).astype(BF16)
        for c in range(TM // CHUNK):
            for g in range(NG):
                rows, cols = slice(c * CHUNK, (c + 1) * CHUNK), slice(g * CHUNK, (g + 1) * CHUNK)
                rs, cs = pl.ds(c * CHUNK, CHUNK), pl.ds(g * CHUNK, CHUNK)
                wsb = ws_ref[g].astype(BF16)
                blk = vn[rows, cols]
                sv = _dot(wsb, blk) + bs_ref[g]
                u, u_grad = _gelu_and_grad(a_ref[rs, cs].astype(F32))
                dmb = dm_ref[rs, cs].astype(F32)
                da_ref[rs, cs] = (dmb * sv * u_grad).astype(BF16)
                dsv = dmb * u
                dsvb = dsv.astype(BF16)
                dbs_ref[g] += jnp.sum(dsv, axis=1, keepdims=True)
                dws_ref[g] += _dot_nt(dsvb, blk)
                dvn_ref[rs, cs] = _dot_tn(wsb, dsvb)
        dvn = dvn_ref[...]
        dvg_ref[...] += jnp.sum(dvn * xhat, axis=0, keepdims=True)
        dxh = dvn * vg
        dv = rstd * (dxh - xhat * jnp.mean(dxh * xhat, axis=-1, keepdims=True))
        da_ref[:, pl.ds(W, W)] = (dv * v_grad).astype(BF16)

    return pl.pallas_call(
        body, name=name, grid=(T // TM,),
        in_specs=[pl.BlockSpec((TM, 2 * W), lambda i: (i, 0)), pl.BlockSpec((1, W), lambda i: (0, 0)),
                  pl.BlockSpec((NG, CHUNK, CHUNK), lambda i: (0, 0, 0)),
                  pl.BlockSpec((NG, CHUNK, 1), lambda i: (0, 0, 0)), pl.BlockSpec((TM, W), lambda i: (i, 0))],
        out_specs=[pl.BlockSpec((TM, 2 * W), lambda i: (i, 0)), pl.BlockSpec((NG, CHUNK, CHUNK), lambda i: (0, 0, 0)),
                   pl.BlockSpec((NG, CHUNK, 1), lambda i: (0, 0, 0)), pl.BlockSpec((1, W), lambda i: (0, 0))],
        out_shape=[jax.ShapeDtypeStruct((T, 2 * W), BF16), jax.ShapeDtypeStruct((NG, CHUNK, CHUNK), F32),
                   jax.ShapeDtypeStruct((NG, CHUNK, 1), F32), jax.ShapeDtypeStruct((1, W), F32)],
        scratch_shapes=[pltpu.VMEM((TM, W), F32)],
        compiler_params=_params(("arbitrary",)),
    )(a, vg, ws, bs, dm)


def _layer_weights(l):
    mixer = ("ab_w_in", "ab_w_out") if l % 2 == 0 else ("cm_w_in", "cm_w_out")
    return [(mixer[0], l // 2), (mixer[1], l // 2), ("ff_w1", l), ("ff_w2", l)]


def _local_step(cfg, xcat, tgt, mods, shards, w):
    D, TM, H = cfg.D, cfg.TM, cfg.H
    cosf, sinf = _rope_tables(cfg)
    full, big, recv = {}, {}, {}

    def gather_of(keys):
        return _Carry("gather", tuple(shards[n][i] for n, i in keys), tuple(BIG[n] for n, _ in keys))

    def exchange_of(keys):
        return _Carry("exchange", tuple(big[k] for k in keys), tuple(BIG[n] for n, _ in keys))

    def mm(pending, name, a, b, **kw):
        if not pending:
            return _mm(name, a, b, **kw)
        key, carry, sink = pending.pop(0)
        out, (got,) = _mm(name, a, b, carry=carry, **kw)
        sink[key] = got
        return out

    def with_carry(call, keys, make, sink):
        out, got = call(carry=make(keys) if keys else None)
        sink.update(zip(keys, got))
        return out

    keys0 = _layer_weights(0)
    full[keys0[0]], = _comm_call("gather_weights_0", gather_of(keys0[:1]))
    TG = 3 * TM if cfg.TPE % 3 == 0 else TM
    tiles_per_ex = cfg.S // TG
    gate_spec = pl.BlockSpec((None, 2, 6, D), lambda i, j, k: (i // tiles_per_ex, 0, 0, 0))

    def resid_epi(igate, nxt):
        def epi(acc, row_tile, x_ref, mod_ref, *nxt_refs):
            row = lax.broadcasted_iota(jnp.int32, (TG, 1), 0)
            is_ctx = jnp.logical_and(row_tile % tiles_per_ex == 0, row < cfg.SC)

            def pick(ref, idx):
                return jnp.where(is_ctx, ref[0, pl.ds(idx, 1), :], ref[1, pl.ds(idx, 1), :])

            x = x_ref[...] + pick(mod_ref, igate) * acc
            if nxt is None:
                return x, acc
            gain_ref, modn_ref = nxt_refs
            n = x * lax.rsqrt(jnp.mean(x * x, axis=-1, keepdims=True) + EPS) * gain_ref[...]
            return x, acc, n * (1.0 + pick(modn_ref, nxt[3])) + pick(modn_ref, nxt[2])
        return epi

    def gated_out(pending, name, a, key, x, mod, igate, nxt=None):
        extras = [(x, pl.BlockSpec((TG, D), lambda i, j, k: (i, j))), (mod, gate_spec)]
        if nxt is not None:
            extras += [(nxt[0], pl.BlockSpec((1, D), lambda i, j, k: (0, 0))), (nxt[1], gate_spec)]
        return mm(pending, name, a, full[key], mode="nn", tm=TG, tn=D, outs=[F32, BF16] + [BF16] * (nxt is not None),
                  epi=resid_epi(igate, nxt), extras=extras)

    saved = []
    x = xcat
    h = _norm_mod_fwd(cfg, "norm1_fwd_0", x, w["norm1_g"][0][None], mods[0], 0, 1)
    for l in range(N_LAYERS):
        li = l // 2
        mod = mods[l]
        k_in, k_out, k_ff1, k_ff2 = _layer_weights(l)
        pend = [(k, gather_of([k]), full) for k in _layer_weights(l + 1)] if l + 1 < N_LAYERS else []
        norm2 = (w["norm2_g"][l][None], mod, 3, 4)
        s = {"x0": x, "h": h}
        if l % 2 == 0:
            lgb = jnp.broadcast_to(jax.nn.log_sigmoid(w["ret_decay"][li]).T[:, :, None, None], (H, 2, 8, LANES))
            qg, kg = w["att_q_norm_g"][li][None], w["att_k_norm_g"][li][None]
            s["p"] = mm(pend, f"ab_in_{l}", s["h"], full[k_in], mode="nn", outs=[BF16], tn=768)
            s["rq"], s["rk"], s["aq"], s["ak"] = _prep_fwd(cfg, f"prep_fwd_{l}", s["p"], cosf, sinf, qg, kg)
            s["o"], ret, s["st"] = with_carry(
                functools.partial(_retention_fwd, cfg, f"ret_fwd_{l}", s["rq"], s["rk"], s["p"], lgb),
                keys0[1:3] if l == 0 else [], gather_of, full)
            s["cat"], s["lse"] = with_carry(
                functools.partial(_attention_fwd, cfg, f"att_fwd_{l}", s["aq"], s["ak"], s["p"], ret),
                keys0[3:] if l == 0 else [], gather_of, full)
            s["lgb"], s["qg"], s["kg"] = lgb, qg, kg
            x, s["y1"], s["h2"] = gated_out(pend, f"ab_out_{l}", s["cat"], k_out, x, mod, 2, norm2)
        else:
            s["a"] = mm(pend, f"cm_in_{l}", s["h"], full[k_in], mode="nn", outs=[BF16])
            s["vg"], s["ws"], s["bs"] = w["cm_v_norm_g"][li][None], w["cm_w_s"][li], w["cm_b_s"][li][:, :, None]
            s["m"] = _cm_fwd(cfg, f"cm_fwd_{l}", s["a"], s["vg"], s["ws"], s["bs"])
            x, s["y1"], s["h2"] = gated_out(pend, f"cm_out_{l}", s["m"], k_out, x, mod, 2, norm2)
        s["x1"] = x
        s["r"] = mm(pend, f"ff1_{l}", s["h2"], full[k_ff1], mode="nn", outs=[BF16],
                    epi=lambda acc, row_tile: (jnp.square(jnp.maximum(acc, 0.0)),))
        if l + 1 < N_LAYERS:
            x, s["y2"], h = gated_out(pend, f"ff2_{l}", s["r"], k_ff2, x, mod, 5,
                                      (w["norm1_g"][l + 1][None], mods[l + 1], 0, 1))
        else:
            x, s["y2"] = gated_out(pend, f"ff2_{l}", s["r"], k_ff2, x, mod, 5)
        saved.append(s)

    loss, dx = _loss_grad(cfg, x, tgt)

    small = {k: [None] * n for k, n in (("norm1_g", 4), ("norm2_g", 4), ("ret_lg", 2), ("att_q_norm_g", 2),
                                        ("att_k_norm_g", 2), ("cm_v_norm_g", 2), ("cm_w_s", 2), ("cm_b_s", 2))}
    dmods = [None] * N_LAYERS

    for l in reversed(range(N_LAYERS)):
        li = l // 2
        s, mod = saved[l], mods[l]
        k_in, k_out, k_ff1, k_ff2 = _layer_weights(l)
        above = _layer_weights(l + 1) if l + 1 < N_LAYERS else [None] * 4
        if l == N_LAYERS - 1:
            dy2, dg2 = _gate_bwd(cfg, f"gate2_bwd_{l}", dx, s["y2"], mod, 5)
        da2, big[k_ff2] = with_carry(functools.partial(_ff_bwd, f"ff2_bwd_{l}", dy2, s["r"], full[k_ff2], "w2"),
                                     [above[3], above[1]] if above[0] else [], exchange_of, recv)
        dh2, big[k_ff1] = with_carry(functools.partial(_ff_bwd, f"ff1_bwd_{l}", s["h2"], da2, full[k_ff1], "w1"),
                                     [above[2], above[0]] if above[0] else [], exchange_of, recv)
        dx, dm2, small["norm2_g"][l], do, dg1 = _norm_mod_bwd(
            cfg, f"norm2_bwd_{l}", s["x1"], w["norm2_g"][l][None], mod, 3, 4, dh2, dx, gate=(s["y1"], mod, 2))
        if l % 2 == 0:
            big[k_out] = _mm(f"ab_out_dw_{l}", s["cat"], do, mode="tn", outs=[BF16])
            dcat = _mm(f"ab_out_dx_{l}", do, full[k_out], mode="nt", outs=[BF16])
            d_rq, d_rk, d_rv, d_gt, dlg = with_carry(
                functools.partial(_retention_bwd, cfg, f"ret_bwd_{l}", s["rq"], s["rk"], s["p"], s["o"], s["st"], dcat,
                                  s["lgb"]), [k_ff2, k_ff1] if l == 0 else [], exchange_of, recv)
            d_aq, d_ak, d_av = with_carry(
                functools.partial(_attention_bwd, cfg, f"att_bwd_{l}", s["aq"], s["ak"], s["p"], s["cat"], s["lse"], dcat),
                [k_out] if l == 0 else [], exchange_of, recv)
            dp, dqg, dkg = _prep_bwd(cfg, f"prep_bwd_{l}", s["p"], cosf, sinf, s["qg"], s["kg"],
                                     d_rq, d_rk, d_rv, d_gt, d_aq, d_ak, d_av)
            small["ret_lg"][li] = jnp.sum(dlg[:, :, :, 0, 0], axis=0).T
            small["att_q_norm_g"][li], small["att_k_norm_g"][li] = dqg[0], dkg[0]
            big[k_in] = _mm(f"ab_in_dw_{l}", s["h"], dp, mode="tn", outs=[BF16])
            last = [(k_in, exchange_of([k_in]), recv)] if l == 0 else []
            dh = mm(last, f"ab_in_dx_{l}", dp, full[k_in], mode="nt", outs=[BF16], tk=768)
        else:
            big[k_out] = _mm(f"cm_out_dw_{l}", s["m"], do, mode="tn", outs=[BF16])
            dm = _mm(f"cm_out_dx_{l}", do, full[k_out], mode="nt", outs=[BF16])
            da, dws, dbs, dvg = _cm_bwd(cfg, f"cm_bwd_{l}", s["a"], s["vg"], s["ws"], s["bs"], dm)
            small["cm_w_s"][li], small["cm_b_s"][li], small["cm_v_norm_g"][li] = dws, dbs[:, :, 0], dvg[0]
            big[k_in] = _mm(f"cm_in_dw_{l}", s["h"], da, mode="tn", outs=[BF16])
            dh = _mm(f"cm_in_dx_{l}", da, full[k_in], mode="nt", outs=[BF16])
        below = (saved[l - 1]["y2"], mods[l - 1], 5) if l > 0 else None
        dx, dm1, small["norm1_g"][l], *rest = _norm_mod_bwd(
            cfg, f"norm1_bwd_{l}", s["x0"], w["norm1_g"][l][None], mod, 0, 1, dh, dx, gate=below, lat_only=l == 0)
        dmods[l] = jnp.concatenate([dm1, dg1, dm2, dg2], axis=2)
        if l > 0:
            dy2, dg2 = rest
    return loss, dx, recv, small, dmods


N_DEV = 8
N_CHIP = 4
MESH = pl.DeviceIdType.MESH
ANY = pl.BlockSpec(memory_space=pl.ANY)
BIG = {"ab_w_in": 1, "ab_w_out": 0, "cm_w_in": 1, "cm_w_out": 0, "ff_w1": 1, "ff_w2": 0}


class _Carry(NamedTuple):
    kind: str
    srcs: tuple
    axes: tuple


def _place():
    x, y, c = lax.axis_index("x"), lax.axis_index("y"), lax.axis_index("c")
    return x, y, c, [(1 - x, y), (x, 1 - y), (1 - x, 1 - y)]


def _shard_of(ref, axis, s, width):
    start = pl.multiple_of(s * width, LANES)
    if axis == 0:
        return ref.at[pl.ds(start, width), :]
    return ref.at[:, pl.ds(start, width)]


def _carry_out_shapes(carry):
    shapes = []
    for src, axis in zip(carry.srcs, carry.axes):
        shape = list(src.shape)
        if carry.kind == "swap":
            pass
        elif carry.kind == "gather":
            shape[axis] *= N_CHIP
        else:
            shape[axis] //= N_CHIP
            shape = [N_CHIP] + shape
        shapes.append(jax.ShapeDtypeStruct(tuple(shape), src.dtype))
    return shapes


def _carry_copies(carry, srcs, dsts, send_sems, recv_sems, local_sems):
    x, y, c, chips = _place()
    me = 2 * x + y
    copies = []
    if carry.kind == "swap":
        return [pltpu.make_async_remote_copy(
            src_ref=srcs[t], dst_ref=dsts[t], send_sem=send_sems.at[3 * t], recv_sem=recv_sems.at[3 * t],
            device_id=(x, y, 1 - c), device_id_type=MESH) for t in range(len(srcs))]
    for t, axis in enumerate(carry.axes):
        if carry.kind == "gather":
            own = _shard_of(dsts[t], axis, me, srcs[t].shape[axis])
            copies.append(pltpu.make_async_copy(srcs[t], own, local_sems.at[t]))
            parts = [(srcs[t], own)] * 3
        else:
            width = dsts[t].shape[1 + axis]
            copies.append(pltpu.make_async_copy(_shard_of(srcs[t], axis, me, width), dsts[t].at[3], local_sems.at[t]))
            parts = [(_shard_of(srcs[t], axis, 2 * px + py, width), dsts[t].at[j]) for j, (px, py) in enumerate(chips)]
        for j, (px, py) in enumerate(chips):
            copies.append(pltpu.make_async_remote_copy(
                src_ref=parts[j][0], dst_ref=parts[j][1], send_sem=send_sems.at[3 * t + j],
                recv_sem=recv_sems.at[3 * t + j], device_id=(px, py, c), device_id_type=MESH))
    return copies


def _comm_call(name, carry):
    nc = len(carry.srcs)

    def body(*refs):
        copies = _carry_copies(carry, refs[:nc], refs[nc:2 * nc], *refs[2 * nc:])
        for cp in copies:
            cp.start()
        for cp in copies:
            cp.wait()

    return pl.pallas_call(
        body, name=name, out_shape=_carry_out_shapes(carry), in_specs=[ANY] * nc, out_specs=[ANY] * nc,
        scratch_shapes=[pltpu.SemaphoreType.DMA((3 * nc,)), pltpu.SemaphoreType.DMA((3 * nc,)),
                        pltpu.SemaphoreType.DMA((nc,))],
    )(*carry.srcs)


def _allgather8(name, block, carry=None):
    m_per, n = block.shape
    nc = len(carry.srcs) if carry is not None else 0

    def body(*refs):
        x_ref, out_ref = refs[0], refs[1 + nc]
        send_sems, recv_sems, local_sem = refs[2 + 2 * nc:5 + 2 * nc]
        carried = _carry_copies(carry, refs[1:1 + nc], refs[2 + nc:2 + 2 * nc], *refs[5 + 2 * nc:]) if nc else []
        for cp in carried:
            cp.start()
        x, y, c, chips = _place()
        me, sibling = (x, y, c), (x, y, 1 - c)

        def rows(px, py, pc):
            return out_ref.at[pl.ds((4 * px + 2 * py + pc) * m_per, m_per), :]

        def copy(k, blk, to, src=None):
            return pltpu.make_async_remote_copy(
                src_ref=rows(*blk) if src is None else src, dst_ref=rows(*blk),
                send_sem=send_sems.at[k], recv_sem=recv_sems.at[k], device_id=to, device_id_type=MESH)

        mine = pltpu.make_async_copy(x_ref, rows(*me), local_sem)
        mine.start()
        first = [copy(0, me, sibling, src=x_ref)]
        first += [copy(1 + j, me, (*chip, c), src=x_ref) for j, chip in enumerate(chips)]
        for cp in first:
            cp.start()
        passed = [copy(4 + j, (*chip, c), sibling) for j, chip in enumerate(chips)]
        for j, chip in enumerate(chips):
            copy(1 + j, (*chip, c), me).wait_recv()
            passed[j].start()
        copy(0, sibling, me).wait_recv()
        for j, chip in enumerate(chips):
            copy(4 + j, (*chip, 1 - c), me).wait_recv()
        for cp in first + passed:
            cp.wait_send()
        mine.wait()
        for cp in carried:
            cp.wait()

    res = pl.pallas_call(
        body, name=name,
        out_shape=[jax.ShapeDtypeStruct((N_DEV * m_per, n), block.dtype)] + (_carry_out_shapes(carry) if nc else []),
        in_specs=[pl.BlockSpec(memory_space=pltpu.VMEM)] + [ANY] * nc,
        out_specs=[pl.BlockSpec(memory_space=pltpu.VMEM)] + [ANY] * nc,
        scratch_shapes=[pltpu.SemaphoreType.DMA((7,)), pltpu.SemaphoreType.DMA((7,)), pltpu.SemaphoreType.DMA]
        + _carry_scratch(nc),
        compiler_params=pltpu.CompilerParams(vmem_limit_bytes=VMEM_LIMIT),
    )(block, *(carry.srcs if nc else ()))
    return (res[0], res[1:]) if nc else res[0]


def _rows_view(a):
    if a.ndim == 1:
        return a.reshape(1, a.shape[0])
    return a.reshape(-1, a.shape[-1])


def _row_tile(rows, cols, target_elems=1 << 17):
    tr = rows
    while tr % 16 == 0 and tr * cols > target_elems:
        tr //= 2
    return tr


def _sum_leading(name, a):
    n, rows, cols = a.shape
    tr = _row_tile(rows, cols * n, target_elems=1 << 20)

    def body(a_ref, o_ref):
        acc = a_ref[0].astype(F32)
        for i in range(1, n):
            acc = acc + a_ref[i].astype(F32)
        o_ref[...] = acc

    return pl.pallas_call(
        body, name=name, grid=(rows // tr,),
        in_specs=[pl.BlockSpec((n, tr, cols), lambda i: (0, i, 0))],
        out_specs=pl.BlockSpec((tr, cols), lambda i: (i, 0)),
        out_shape=jax.ShapeDtypeStruct((rows, cols), F32),
        compiler_params=_params(("parallel",)),
    )(a)


def _sum_parts_layers(name, parts):
    n_layers = len(parts)
    n, rows, cols = parts[0].shape
    tr = _row_tile(rows, cols * n, target_elems=1 << 20)

    def body(*refs):
        o_ref = refs[n_layers]
        layer = pl.program_id(0)
        for k in range(n_layers):
            @pl.when(layer == k)
            def _():
                acc = refs[k][0].astype(F32)
                for i in range(1, n):
                    acc = acc + refs[k][i].astype(F32)
                o_ref[...] = acc

    def in_spec(k):
        return pl.BlockSpec((n, tr, cols), lambda l, i: (0, jnp.where(l == k, i, 0), 0))

    return pl.pallas_call(
        body, name=name, grid=(n_layers, rows // tr),
        in_specs=[in_spec(k) for k in range(n_layers)],
        out_specs=pl.BlockSpec((None, tr, cols), lambda l, i: (l, i, 0)),
        out_shape=jax.ShapeDtypeStruct((n_layers, rows, cols), F32),
        compiler_params=_params(("arbitrary", "arbitrary")),
    )(*parts)


def _silu_rows(name, x):
    def body(x_ref, o_ref):
        v = x_ref[...]
        o_ref[...] = v * jax.nn.sigmoid(v)

    return pl.pallas_call(body, name=name, out_shape=jax.ShapeDtypeStruct(x.shape, F32))(x)


def _silu_bwd_rows(name, x, dy):
    def body(x_ref, dy_ref, o_ref):
        v = x_ref[...]
        sg = jax.nn.sigmoid(v)
        o_ref[...] = dy_ref[...] * (sg * (1.0 + v * (1.0 - sg)))

    return pl.pallas_call(body, name=name, out_shape=jax.ShapeDtypeStruct(x.shape, F32))(x, dy)


def _adamw(name, w, g_parts, m, v):
    shape = w.shape
    w2, m2, v2 = _rows_view(w), _rows_view(m), _rows_view(v)
    gs = [_rows_view(g) for g in g_parts]
    rows, cols = w2.shape
    tr = _row_tile(rows, cols)
    ng = len(gs)

    def body(*refs):
        w_ref, m_ref, v_ref = refs[0], refs[1], refs[2]
        g_refs = refs[3:3 + ng]
        g_out, d_out, m_out, v_out = refs[3 + ng:]
        g = g_refs[0][...]
        for r in g_refs[1:]:
            g = g + r[...]
        m1 = ADAM_B1 * m_ref[...] + (1.0 - ADAM_B1) * g
        v1 = ADAM_B2 * v_ref[...] + (1.0 - ADAM_B2) * jnp.square(g)
        m_hat = m1 / (1.0 - ADAM_B1 ** ADAM_STEP)
        v_hat = v1 / (1.0 - ADAM_B2 ** ADAM_STEP)
        g_out[...] = g
        d_out[...] = -ADAM_LR * (m_hat / (jnp.sqrt(v_hat) + ADAM_EPS) + ADAM_WD * w_ref[...])
        m_out[...] = m1
        v_out[...] = v1

    spec = pl.BlockSpec((tr, cols), lambda i: (i, 0))
    res = pl.pallas_call(
        body, name=name, grid=(rows // tr,), in_specs=[spec] * (3 + ng), out_specs=[spec] * 4,
        out_shape=[jax.ShapeDtypeStruct((rows, cols), F32)] * 4,
        compiler_params=_params(("parallel",)),
    )(w2, m2, v2, *gs)
    return tuple(r.reshape(shape) for r in res)


MOD_ROWS = 48


def kernel(x, c, ctx, c_ctx, mod_w, mod_b, norm1_g, norm2_g, ab_w_in, ab_w_out, ret_decay, att_q_norm_g, att_k_norm_g, cm_w_in, cm_v_norm_g, cm_w_s, cm_b_s, cm_w_out, ff_w1, ff_w2, loss_target, m_c_ctx, m_mod_w, m_mod_b, m_norm1_g, m_norm2_g, m_ab_w_in, m_ab_w_out, m_ret_decay, m_att_q_norm_g, m_att_k_norm_g, m_cm_w_in, m_cm_v_norm_g, m_cm_w_s, m_cm_b_s, m_cm_w_out, m_ff_w1, m_ff_w2, v_c_ctx, v_mod_w, v_mod_b, v_norm1_g, v_norm2_g, v_ab_w_in, v_ab_w_out, v_ret_decay, v_att_q_norm_g, v_att_k_norm_g, v_cm_w_in, v_cm_v_norm_g, v_cm_w_s, v_cm_b_s, v_cm_w_out, v_ff_w1, v_ff_w2):
    B, SL, D = x.shape
    cfg = Cfg(B=B, SC=ctx.shape[1], SL=SL, D=D, FF=ff_w1.shape[2] * N_CHIP)
    L = N_LAYERS
    n_ex = B * N_DEV
    mcols = mod_w.shape[2]
    weights = dict(c_ctx=c_ctx, mod_w=mod_w, mod_b=mod_b, norm1_g=norm1_g, norm2_g=norm2_g, ab_w_in=ab_w_in,
                   ab_w_out=ab_w_out, ret_decay=ret_decay, att_q_norm_g=att_q_norm_g, att_k_norm_g=att_k_norm_g,
                   cm_w_in=cm_w_in, cm_v_norm_g=cm_v_norm_g, cm_w_s=cm_w_s, cm_b_s=cm_b_s, cm_w_out=cm_w_out,
                   ff_w1=ff_w1, ff_w2=ff_w2)
    m_in = dict(c_ctx=m_c_ctx, mod_w=m_mod_w, mod_b=m_mod_b, norm1_g=m_norm1_g, norm2_g=m_norm2_g, ab_w_in=m_ab_w_in,
                ab_w_out=m_ab_w_out, ret_decay=m_ret_decay, att_q_norm_g=m_att_q_norm_g, att_k_norm_g=m_att_k_norm_g,
                cm_w_in=m_cm_w_in, cm_v_norm_g=m_cm_v_norm_g, cm_w_s=m_cm_w_s, cm_b_s=m_cm_b_s, cm_w_out=m_cm_w_out,
                ff_w1=m_ff_w1, ff_w2=m_ff_w2)
    v_in = dict(c_ctx=v_c_ctx, mod_w=v_mod_w, mod_b=v_mod_b, norm1_g=v_norm1_g, norm2_g=v_norm2_g, ab_w_in=v_ab_w_in,
                ab_w_out=v_ab_w_out, ret_decay=v_ret_decay, att_q_norm_g=v_att_q_norm_g, att_k_norm_g=v_att_k_norm_g,
                cm_w_in=v_cm_w_in, cm_v_norm_g=v_cm_v_norm_g, cm_w_s=v_cm_w_s, cm_b_s=v_cm_b_s, cm_w_out=v_cm_w_out,
                ff_w1=v_ff_w1, ff_w2=v_ff_w2)
    xi, yi, ci = lax.axis_index("x"), lax.axis_index("y"), lax.axis_index("c")
    chip = 2 * xi + yi
    dev = 2 * chip + ci

    shards = {n: [weights[n][i].astype(BF16) for i in range(weights[n].shape[0])] for n in BIG}
    vgw = cm_v_norm_g.shape[1]
    blk = jnp.zeros((8, D), F32).at[:B].set(c).at[B:B + 2, :vgw].set(cm_v_norm_g)
    g0 = _allgather8("gather_c", blk).reshape(N_DEV, 8, D)
    c_all = g0[:, :B].reshape(n_ex, D)
    vg_full = jnp.concatenate([g0[2 * s, B:B + 2, :vgw] for s in range(N_CHIP)], axis=-1)

    pre = jnp.zeros((MOD_ROWS, D), F32).at[:n_ex].set(c_all).at[n_ex].set(c_ctx)
    act = _silu_rows("silu_c", pre)
    mpart = jnp.stack([_mm(f"mod_fwd_{l}", act, mod_w, mode="nn", layer=l, outs=[F32], tn=mcols) for l in range(L)])
    g1 = _allgather8("gather_mod", mpart.reshape(L * MOD_ROWS, mcols)).reshape(N_DEV, L, MOD_ROWS, mcols)
    mod_all = jnp.concatenate([g1[2 * s] for s in range(N_CHIP)], axis=-1) + mod_b[:, None, :]
    mod_lat = lax.dynamic_slice_in_dim(mod_all, dev * B, B, axis=1)
    mod_ctx = jnp.broadcast_to(mod_all[:, n_ex][:, None], mod_lat.shape)
    mods = jnp.stack([mod_ctx, mod_lat], axis=2).reshape(L, B, 2, 6, D)

    w = dict(norm1_g=norm1_g, norm2_g=norm2_g, ret_decay=ret_decay, att_q_norm_g=att_q_norm_g,
             att_k_norm_g=att_k_norm_g, cm_v_norm_g=vg_full, cm_w_s=cm_w_s, cm_b_s=cm_b_s)
    xcat = jnp.concatenate([ctx, x], axis=1).reshape(cfg.T, D)
    loss_local, dx_lat, recv, small, dmods = _local_step(cfg, xcat, loss_target.reshape(B * SL, D), mods, shards, w)
    loss = lax.psum(loss_local, ("x", "y", "c"))
    grad_x = dx_lat.reshape(B, SL, D)

    part = [_sum_parts_layers(f"sum_{n}", [recv[(n, i)] for i in range(weights[n].shape[0])]) for n in BIG]

    dmod = jnp.stack(dmods).reshape(L, B, 2, 6 * D)
    dmod_lat = dmod[:, :, 1]
    dmod_ctx = jnp.sum(dmod[:, :, 0], axis=1)
    d_ret = jnp.stack(small["ret_lg"]) * jax.nn.sigmoid(-ret_decay)
    summed = [dmod_ctx.reshape(-1), jnp.stack(small["norm1_g"]).reshape(-1), jnp.stack(small["norm2_g"]).reshape(-1),
              jnp.stack(small["cm_v_norm_g"]).reshape(-1), jnp.stack(small["cm_w_s"]).reshape(-1),
              jnp.stack(small["cm_b_s"]).reshape(-1), jnp.stack(small["att_q_norm_g"]).reshape(-1),
              jnp.stack(small["att_k_norm_g"]).reshape(-1), d_ret.reshape(-1)]
    sizes = [int(a.shape[0]) for a in summed]
    flat = jnp.concatenate(summed + [dmod_lat.reshape(-1)])
    n_sum = sum(sizes)
    n_sum_rows = -(-n_sum // D)
    lat_rows = (L * B * 6 * D) // D
    pack_rows = -(-(n_sum_rows + lat_rows) // 8) * 8
    packed = jnp.zeros((pack_rows * D,), F32).at[:n_sum].set(flat[:n_sum])
    packed = packed.at[n_sum_rows * D:(n_sum_rows + lat_rows) * D].set(flat[n_sum:]).reshape(pack_rows, D)
    g2, other = _allgather8("gather_small", packed, carry=_Carry("swap", tuple(part), (0,) * len(part)))
    g2 = g2.reshape(N_DEV, pack_rows, D)
    tot = _sum_leading("sum_small", g2[:, :n_sum_rows]).reshape(-1)
    pieces, off = [], 0
    for sz in sizes:
        pieces.append(tot[off:off + sz])
        off += sz
    dmod_ctx_t, g_n1, g_n2, g_vg, g_ws, g_bs, g_qg, g_kg, g_rd = pieces
    dmod_ctx_t = dmod_ctx_t.reshape(L, 6 * D)
    dmod_lat_all = g2[:, n_sum_rows:n_sum_rows + lat_rows].reshape(N_DEV, L, B, 6 * D)
    dmod_rows = jnp.zeros((L, MOD_ROWS, 6 * D), F32)
    dmod_rows = dmod_rows.at[:, :n_ex].set(jnp.transpose(dmod_lat_all, (1, 0, 2, 3)).reshape(L, n_ex, 6 * D))
    dmod_rows = dmod_rows.at[:, n_ex].set(dmod_ctx_t)
    g_mod_b = _sum_leading("sum_mod_b", jnp.transpose(dmod_rows, (1, 0, 2)))
    dmod_mine = lax.dynamic_slice_in_dim(dmod_rows, chip * mcols, mcols, axis=2)
    g_mod_w = jnp.stack([_mm(f"mod_dw_{l}", act, dmod_mine[l], mode="tn", outs=[F32], tn=mcols) for l in range(L)])
    ctx8 = jnp.zeros((L, 8, mcols), F32).at[:, 0].set(dmod_mine[:, n_ex])
    dcc = [_mm(f"mod_dctx_{l}", ctx8[l], mod_w, mode="nt", layer=l, outs=[F32], tk=mcols) for l in range(L)]
    dcc = _sum_leading("sum_dctx_layers", jnp.stack(dcc))
    g3 = _allgather8("gather_dctx", dcc).reshape(N_DEV, 8, D)
    dcc_t = _sum_leading("sum_dctx_chips", g3[0::2])[0:1]
    g_c_ctx = _silu_bwd_rows("silu_bwd_cctx", c_ctx[None], dcc_t)[0]

    vg_mine = lax.dynamic_slice_in_dim(g_vg.reshape(2, -1), chip * vgw, vgw, axis=1)
    small_g = dict(c_ctx=g_c_ctx, mod_w=g_mod_w, mod_b=g_mod_b, norm1_g=g_n1.reshape(norm1_g.shape),
                   norm2_g=g_n2.reshape(norm2_g.shape), ret_decay=g_rd.reshape(ret_decay.shape),
                   att_q_norm_g=g_qg.reshape(att_q_norm_g.shape), att_k_norm_g=g_kg.reshape(att_k_norm_g.shape),
                   cm_v_norm_g=vg_mine, cm_w_s=g_ws.reshape(cm_w_s.shape), cm_b_s=g_bs.reshape(cm_b_s.shape))
    out = {}
    for n, g in small_g.items():
        out[n] = _adamw(f"adamw_{n}", weights[n], [g], m_in[n], v_in[n])
    for n, p_mine, p_other in zip(BIG, part, other):
        out[n] = _adamw(f"adamw_{n}", weights[n], [p_mine, p_other], m_in[n], v_in[n])

    order = list(weights)
    return (loss, grad_x, *[out[n][0] for n in order], *[out[n][1] for n in order],
            *[out[n][2] for n in order], *[out[n][3] for n in order])
```

```python
import functools
import math
from typing import NamedTuple

import jax
import jax.numpy as jnp
from jax import lax
from jax.experimental import pallas as pl
from jax.experimental.pallas import tpu as pltpu

F32 = jnp.float32
BF16 = jnp.bfloat16
EPS = 1e-6
ROPE_BASE = 10000.0
LANES = 128
CHUNK = 128
N_LAYERS = 4
VMEM_LIMIT = 56 * 1024 * 1024

ADAM_LR = 0.001
ADAM_B1 = 0.9
ADAM_B2 = 0.999
ADAM_EPS = 1e-08
ADAM_WD = 0.01
ADAM_STEP = 10


class Cfg(NamedTuple):
    B: int = 4
    SC: int = 256
    SL: int = 2048
    D: int = 1024
    FF: int = 4096
    GRID_W: int = 64
    H: int = 4
    KV: int = 2
    CMW: int = 1024
    CMG: int = 8

    @property
    def S(self):
        return self.SC + self.SL

    @property
    def T(self):
        return self.B * self.S

    @property
    def TM(self):
        return self.SC

    @property
    def TPE(self):
        return self.S // self.SC

    @property
    def ABW(self):
        return (5 * self.H + 2 * self.KV) * CHUNK


def _tile(dim, pref):
    t = min(dim, pref)
    while dim % t:
        t -= LANES
    return t


def _dot(a, b):
    return lax.dot_general(a, b, (((1,), (0,)), ((), ())), preferred_element_type=F32)


def _dot_nt(a, b):
    return lax.dot_general(a, b, (((1,), (1,)), ((), ())), preferred_element_type=F32)


def _dot_tn(a, b):
    return lax.dot_general(a, b, (((0,), (0,)), ((), ())), preferred_element_type=F32)


def _params(sem, vmem=VMEM_LIMIT):
    return pltpu.CompilerParams(dimension_semantics=sem, vmem_limit_bytes=vmem)


def _mod_index(cfg):
    tpe = cfg.TPE
    return lambda i: (i // tpe, jnp.minimum(i % tpe, 1), 0, 0)


def _mm(name, a, b, *, mode, outs, tm=1024, tn=1024, tk=1024, layer=None, epi=None, extras=(), carry=None,
        resident=False):
    bshape = b.shape[1:] if layer is not None else b.shape
    if mode == "nn":
        (M, K), N = a.shape, bshape[1]
    elif mode == "nt":
        (M, K), N = a.shape, bshape[0]
    else:
        (K, M), N = a.shape, bshape[1]
    tm, tn, tk = _tile(M, tm), _tile(N, tn), _tile(K, tk)
    kchunk = tk
    if resident:
        tk = K
    nk = K // tk
    a_spec = (pl.BlockSpec((tk, tm), lambda i, j, k: (k, i)) if mode == "tn"
              else pl.BlockSpec((tm, tk), lambda i, j, k: (i, k)))
    if mode == "nt":
        bblk, bidx = (tn, tk), (lambda i, j, k: (j, k))
    else:
        bblk, bidx = (tk, tn), (lambda i, j, k: (k, j))
    if layer is not None:
        b_spec = pl.BlockSpec((None,) + bblk, lambda i, j, k: (layer,) + bidx(i, j, k))
    elif resident:
        b_spec = pl.BlockSpec(bblk, bidx, pipeline_mode=pl.Buffered(1))
    else:
        b_spec = pl.BlockSpec(bblk, bidx)
    ne, no = len(extras), len(outs)
    nc = len(carry.srcs) if carry is not None else 0
    dot = {"nn": _dot, "nt": _dot_nt, "tn": _dot_tn}[mode]
    grid = (M // tm, N // tn, nk)

    def body(*refs):
        a_ref, b_ref = refs[0], refs[1]
        ex, out_refs = refs[2:2 + ne], refs[2 + ne + nc:2 + ne + nc + no]
        row_tile = pl.program_id(0)

        if nc:
            step = (pl.program_id(0) * grid[1] + pl.program_id(1)) * grid[2] + pl.program_id(2)
            c_src = refs[2 + ne:2 + ne + nc]
            c_dst = refs[2 + ne + nc + no:2 + ne + 2 * nc + no]
            sems = refs[2 + ne + 2 * nc + no:2 + ne + 2 * nc + no + 3]

            @pl.when(step == 0)
            def _():
                for cp in _carry_copies(carry, c_src, c_dst, *sems):
                    cp.start()

        def finish(acc):
            res = epi(acc, row_tile, *ex) if epi is not None else (acc,)
            for r, o in zip(res, out_refs):
                o[...] = r.astype(o.dtype)

        if resident and K > kchunk:
            part = None
            for c in range(K // kchunk):
                rows = pl.ds(c * kchunk, kchunk)
                term = dot(a_ref[:, rows].astype(BF16), b_ref[rows, :].astype(BF16))
                part = term if part is None else part + term
        else:
            part = dot(a_ref[...].astype(BF16), b_ref[...].astype(BF16))
        if nk == 1:
            finish(part)
        else:
            acc_ref = refs[-1]
            k = pl.program_id(2)

            @pl.when(k == 0)
            def _():
                acc_ref[...] = part

            @pl.when(k > 0)
            def _():
                acc_ref[...] += part

            @pl.when(k == nk - 1)
            def _():
                finish(acc_ref[...])

        if nc:
            @pl.when(step == grid[0] * grid[1] * grid[2] - 1)
            def _():
                for cp in _carry_copies(carry, c_src, c_dst, *sems):
                    cp.wait()

    scratch = [pltpu.SemaphoreType.DMA((3 * nc,)), pltpu.SemaphoreType.DMA((3 * nc,)),
               pltpu.SemaphoreType.DMA((nc,))] if nc else []
    if nk > 1:
        scratch.append(pltpu.VMEM((tm, tn), F32))
    res = pl.pallas_call(
        body, name=name, grid=grid,
        in_specs=[a_spec, b_spec] + [s for _, s in extras] + [ANY] * nc,
        out_specs=[pl.BlockSpec((tm, tn), lambda i, j, k: (i, j)) for _ in outs] + [ANY] * nc,
        out_shape=[jax.ShapeDtypeStruct((M, N), d) for d in outs] + (_carry_out_shapes(carry) if nc else []),
        scratch_shapes=scratch,
        compiler_params=_params(("arbitrary",) * 3 if nc else ("parallel", "parallel", "arbitrary")),
    )(a, b, *[x for x, _ in extras], *(carry.srcs if nc else ()))
    if nc:
        return (res[0] if no == 1 else res[:no]), res[no:]
    return res[0] if no == 1 else res


def _ff_bwd(name, first, second, weight, kind, carry=None):
    (T, D), FF = first.shape, second.shape[1]
    halves = 2
    tm = _tile(T, 1024)
    ffh = FF // halves if kind == "w2" else FF
    dh_cols = D if kind == "w2" else D // halves
    cw = _tile(ffh, 1024)
    n_steps = T // tm
    nc = len(carry.srcs) if carry is not None else 0

    def body(*refs):
        a_ref, b_ref, w_ref = refs[:3]
        c_src = refs[3:3 + nc]
        x_ref, dw_ref = refs[3 + nc:5 + nc]
        c_dst, sems = refs[5 + nc:5 + 2 * nc], refs[5 + 2 * nc:5 + 2 * nc + 3] if nc else ()
        acc_ref = refs[-1]
        i = pl.program_id(1)
        step = pl.program_id(0) * n_steps + i
        if nc:
            _carry_begin(carry, c_src, c_dst, sems, step)
        a = a_ref[...]
        dh = None
        for c in range(ffh // cw):
            cols = pl.ds(c * cw, cw)
            if kind == "w2":
                r = b_ref[:, cols]
                x_ref[:, cols] = (_dot_nt(a, w_ref[cols, :]) * (2.0 * jnp.sqrt(r.astype(F32)))).astype(BF16)
                part, dst = _dot_tn(r, a), acc_ref.at[cols, :]
            else:
                da = b_ref[:, cols]
                term = _dot_nt(da, w_ref[:, cols])
                dh = term if dh is None else dh + term
                part, dst = _dot_tn(a, da), acc_ref.at[:, cols]

            @pl.when(i == 0)
            def _():
                dst[...] = part

            @pl.when(i > 0)
            def _():
                dst[...] += part

        if kind == "w1":
            x_ref[...] = dh.astype(BF16)

        @pl.when(i == n_steps - 1)
        def _():
            dw_ref[...] = acc_ref[...].astype(BF16)

        if nc:
            _carry_end(carry, c_src, c_dst, sems, step, halves * n_steps)

    if kind == "w2":
        wshape = (ffh, D)
        a_spec = pl.BlockSpec((tm, D), lambda j, i: (i, 0))
        b_spec = pl.BlockSpec((tm, ffh), lambda j, i: (i, j))
        x_spec, x_cols = pl.BlockSpec((tm, ffh), lambda j, i: (i, j)), FF
    else:
        wshape = (dh_cols, FF)
        a_spec = pl.BlockSpec((tm, dh_cols), lambda j, i: (i, j))
        b_spec = pl.BlockSpec((tm, FF), lambda j, i: (i, 0))
        x_spec, x_cols = pl.BlockSpec((tm, dh_cols), lambda j, i: (i, j)), D
    wspec = pl.BlockSpec(wshape, lambda j, i: (j, 0), pipeline_mode=pl.Buffered(1))
    res = pl.pallas_call(
        body, name=name, grid=(halves, n_steps),
        in_specs=[a_spec, b_spec, wspec] + [ANY] * nc,
        out_specs=[x_spec, wspec] + [ANY] * nc,
        out_shape=[jax.ShapeDtypeStruct((T, x_cols), BF16), jax.ShapeDtypeStruct(weight.shape, BF16)]
        + (_carry_out_shapes(carry) if nc else []),
        scratch_shapes=_carry_scratch(nc) + [pltpu.VMEM(wshape, F32)],
        compiler_params=_params(("arbitrary", "arbitrary")),
    )(first, second, weight, *(carry.srcs if nc else ()))
    return res[:2], res[2:]


def _norm_mod_fwd(cfg, name, x, gain, mod, ish, isc):
    T, D, TM = cfg.T, cfg.D, cfg.TM

    def body(x_ref, g_ref, mod_ref, h_ref):
        x = x_ref[...]
        rstd = lax.rsqrt(jnp.mean(x * x, axis=-1, keepdims=True) + EPS)
        n = x * rstd * g_ref[...]
        h = n * (1.0 + mod_ref[pl.ds(isc, 1), :]) + mod_ref[pl.ds(ish, 1), :]
        h_ref[...] = h.astype(BF16)

    return pl.pallas_call(
        body, name=name, grid=(T // TM,),
        in_specs=[pl.BlockSpec((TM, D), lambda i: (i, 0)), pl.BlockSpec((1, D), lambda i: (0, 0)),
                  pl.BlockSpec((None, None, 6, D), _mod_index(cfg))],
        out_specs=pl.BlockSpec((TM, D), lambda i: (i, 0)),
        out_shape=jax.ShapeDtypeStruct((T, D), BF16),
        compiler_params=_params(("parallel",)),
    )(x, gain, mod)


def _norm_mod_bwd(cfg, name, x, gain, mod, ish, isc, dh, dres, gate=None, lat_only=False):
    T, D, TM, TPE = cfg.T, cfg.D, cfg.TM, cfg.TPE
    ng = 2 if gate is not None else 0
    dx_spec = (pl.BlockSpec((TM, D), lambda i: ((i // TPE) * (TPE - 1) + jnp.maximum(i % TPE - 1, 0), 0)) if lat_only
               else pl.BlockSpec((TM, D), lambda i: (i, 0)))
    dx_rows = cfg.B * cfg.SL if lat_only else T

    def body(*refs):
        x_ref, g_ref, mod_ref, dh_ref, dres_ref = refs[:5]
        dx_ref, dmod_ref, dgain_ref = refs[5 + ng:8 + ng]
        i = pl.program_id(0)
        t = i % TPE
        x = x_ref[...]
        g = g_ref[...]
        dh = dh_ref[...].astype(F32)
        rstd = lax.rsqrt(jnp.mean(x * x, axis=-1, keepdims=True) + EPS)
        xhat = x * rstd
        dn = dh * (1.0 + mod_ref[pl.ds(isc, 1), :])
        dsh = jnp.sum(dh, axis=0, keepdims=True)
        dsc = jnp.sum(dh * (xhat * g), axis=0, keepdims=True)
        dgain = jnp.sum(dn * xhat, axis=0, keepdims=True)
        dxh = dn * g
        dx = rstd * (dxh - xhat * jnp.mean(dxh * xhat, axis=-1, keepdims=True)) + dres_ref[...]
        dx_ref[...] = dx
        sums = [(dmod_ref.at[pl.ds(0, 1), :], dsh), (dmod_ref.at[pl.ds(1, 1), :], dsc)]
        if ng:
            y_ref, gmod_ref = refs[5:7]
            dy_ref, dgate_ref = refs[8 + ng:]
            dy_ref[...] = (dx * gmod_ref[pl.ds(gate[2], 1), :]).astype(BF16)
            sums.append((dgate_ref, jnp.sum(dx * y_ref[...].astype(F32), axis=0, keepdims=True)))

        @pl.when(t <= 1)
        def _():
            for ref, val in sums:
                ref[...] = val

        @pl.when(t > 1)
        def _():
            for ref, val in sums:
                ref[...] += val

        @pl.when(i == 0)
        def _():
            dgain_ref[...] = dgain

        @pl.when(i > 0)
        def _():
            dgain_ref[...] += dgain

    tok = pl.BlockSpec((TM, D), lambda i: (i, 0))
    mod_spec = pl.BlockSpec((None, None, 6, D), _mod_index(cfg))
    res = pl.pallas_call(
        body, name=name, grid=(T // TM,),
        in_specs=[tok, pl.BlockSpec((1, D), lambda i: (0, 0)), mod_spec, tok, tok] + ([tok, mod_spec] if ng else []),
        out_specs=[dx_spec, pl.BlockSpec((None, None, 2, D), _mod_index(cfg)), pl.BlockSpec((1, D), lambda i: (0, 0))]
        + ([tok, pl.BlockSpec((None, None, 1, D), _mod_index(cfg))] if ng else []),
        out_shape=[jax.ShapeDtypeStruct((dx_rows, D), F32), jax.ShapeDtypeStruct((cfg.B, 2, 2, D), F32),
                   jax.ShapeDtypeStruct((1, D), F32)]
        + ([jax.ShapeDtypeStruct((T, D), BF16), jax.ShapeDtypeStruct((cfg.B, 2, 1, D), F32)] if ng else []),
        compiler_params=_params(("arbitrary",)),
    )(x, gain, mod, dh, dres, *(gate[:2] if ng else ()))
    return res


def _gate_bwd(cfg, name, dx, y, mod, igate):
    T, D, TM, TPE = cfg.T, cfg.D, cfg.TM, cfg.TPE

    def body(dx_ref, y_ref, mod_ref, dy_ref, dg_ref):
        t = pl.program_id(0) % TPE
        dx = dx_ref[...]
        dy_ref[...] = (dx * mod_ref[pl.ds(igate, 1), :]).astype(BF16)
        dg = jnp.sum(dx * y_ref[...].astype(F32), axis=0, keepdims=True)

        @pl.when(t <= 1)
        def _():
            dg_ref[...] = dg

        @pl.when(t > 1)
        def _():
            dg_ref[...] += dg

    tok = pl.BlockSpec((TM, D), lambda i: (i, 0))
    return pl.pallas_call(
        body, name=name, grid=(T // TM,),
        in_specs=[tok, tok, pl.BlockSpec((None, None, 6, D), _mod_index(cfg))],
        out_specs=[tok, pl.BlockSpec((None, None, 1, D), _mod_index(cfg))],
        out_shape=[jax.ShapeDtypeStruct((T, D), BF16), jax.ShapeDtypeStruct((cfg.B, 2, 1, D), F32)],
        compiler_params=_params(("arbitrary",)),
    )(dx, y, mod)


def _loss_grad(cfg, x, tgt):
    T, D, TM, TPE = cfg.T, cfg.D, cfg.TM, cfg.TPE

    def body(x_ref, t_ref, dx_ref, loss_ref):
        i = pl.program_id(0)
        t = i % TPE

        @pl.when(i == 0)
        def _():
            loss_ref[...] = jnp.zeros_like(loss_ref)

        @pl.when(t == 0)
        def _():
            dx_ref[...] = jnp.zeros_like(dx_ref)

        @pl.when(t > 0)
        def _():
            err = x_ref[...] - t_ref[...]
            dx_ref[...] = err * (1.0 / D)
            loss_ref[...] += 0.5 * jnp.sum(jnp.mean(err * err, axis=-1, keepdims=True), axis=0, keepdims=True)

    tok = pl.BlockSpec((TM, D), lambda i: (i, 0))
    tgt_spec = pl.BlockSpec((TM, D), lambda i: ((i // TPE) * (TPE - 1) + jnp.maximum(i % TPE - 1, 0), 0))
    dx, loss = pl.pallas_call(
        body, name="loss_grad", grid=(T // TM,),
        in_specs=[tok, tgt_spec], out_specs=[tok, pl.BlockSpec((8, LANES), lambda i: (0, 0))],
        out_shape=[jax.ShapeDtypeStruct((T, D), F32), jax.ShapeDtypeStruct((8, LANES), F32)],
        compiler_params=_params(("arbitrary",)),
    )(x, tgt)
    return loss[0, 0], dx


def _rope_tables(cfg):
    rows = cfg.SL // cfg.GRID_W
    row = jnp.repeat(jnp.arange(rows, dtype=F32), cfg.GRID_W)
    col = jnp.tile(jnp.arange(cfg.GRID_W, dtype=F32), rows)
    n_freq = CHUNK // 4
    inv = ROPE_BASE ** (-jnp.arange(n_freq, dtype=F32) / n_freq)
    ang = jnp.concatenate([row[:, None] * inv[None, :], col[:, None] * inv[None, :]], axis=-1)
    cos, sin = jnp.cos(ang), jnp.sin(ang)
    cosf = jnp.concatenate([jnp.ones((cfg.SC, CHUNK), F32), jnp.concatenate([cos, cos], axis=-1)], axis=0)
    sinf = jnp.concatenate([jnp.zeros((cfg.SC, CHUNK), F32), jnp.concatenate([-sin, sin], axis=-1)], axis=0)
    return cosf, sinf


def _rope(x, cosf, sinf):
    return x * cosf + pltpu.roll(x, CHUNK // 2, 1) * sinf


def _irope(dy, cosf, sinf):
    return dy * cosf - pltpu.roll(dy, CHUNK // 2, 1) * sinf


def _prep_fwd(cfg, name, p, cosf, sinf, qg, kg):
    T, TM, TPE, H, KV = cfg.T, cfg.TM, cfg.TPE, cfg.H, cfg.KV
    HW = H * CHUNK
    kscale = CHUNK ** -0.5

    def body(p_ref, c_ref, s_ref, qg_ref, kg_ref, rq_ref, rk_ref, aq_ref, ak_ref):
        cosf, sinf = c_ref[...], s_ref[...]

        def normed(x, g):
            return x * lax.rsqrt(jnp.mean(x * x, axis=-1, keepdims=True) + EPS) * g

        def seg(col):
            return p_ref[:, pl.ds(col, CHUNK)].astype(F32)

        for h in range(H):
            sl = pl.ds(h * CHUNK, CHUNK)
            rq_ref[:, sl] = _rope(seg(h * CHUNK), cosf, sinf)
            rk_ref[:, sl] = _rope(seg(HW + h * CHUNK), cosf, sinf) * kscale
            aq_ref[:, sl] = (_rope(normed(seg(4 * HW + h * CHUNK), qg_ref[...]), cosf, sinf) * ATT_SCALE).astype(BF16)
        for h in range(KV):
            ak_ref[:, pl.ds(h * CHUNK, CHUNK)] = _rope(
                normed(seg(5 * HW + h * CHUNK), kg_ref[...]), cosf, sinf).astype(BF16)

    tab = pl.BlockSpec((TM, CHUNK), lambda i: (i % TPE, 0))
    vec = pl.BlockSpec((1, CHUNK), lambda i: (0, 0))
    return pl.pallas_call(
        body, name=name, grid=(T // TM,),
        in_specs=[pl.BlockSpec((TM, cfg.ABW), lambda i: (i, 0)), tab, tab, vec, vec],
        out_specs=[pl.BlockSpec((TM, HW), lambda i: (i, 0))] * 3 + [pl.BlockSpec((TM, KV * CHUNK), lambda i: (i, 0))],
        out_shape=[jax.ShapeDtypeStruct((T, HW), F32), jax.ShapeDtypeStruct((T, HW), F32),
                   jax.ShapeDtypeStruct((T, HW), BF16), jax.ShapeDtypeStruct((T, KV * CHUNK), BF16)],
        compiler_params=_params(("parallel",)),
    )(p, cosf, sinf, qg, kg)


def _prep_bwd(cfg, name, p, cosf, sinf, qg, kg, d_rq, d_rk, d_rv, d_gate, d_aq, d_ak, d_av):
    T, TM, TPE, H, KV = cfg.T, cfg.TM, cfg.TPE, cfg.H, cfg.KV
    HW = H * CHUNK
    kscale = CHUNK ** -0.5

    def body(p_ref, c_ref, s_ref, qg_ref, kg_ref, drq_ref, drk_ref, drv_ref, dgt_ref, daq_ref, dak_ref, dav_ref,
             dp_ref, dqg_ref, dkg_ref):
        i = pl.program_id(0)
        cosf, sinf = c_ref[...], s_ref[...]

        def norm_bwd(x, g, dn):
            rstd = lax.rsqrt(jnp.mean(x * x, axis=-1, keepdims=True) + EPS)
            xhat = x * rstd
            dg = jnp.sum(dn * xhat, axis=0, keepdims=True)
            dxh = dn * g
            return rstd * (dxh - xhat * jnp.mean(dxh * xhat, axis=-1, keepdims=True)), dg

        dqg = jnp.zeros((1, CHUNK), F32)
        dkg = jnp.zeros((1, CHUNK), F32)
        for h in range(H):
            sl = pl.ds(h * CHUNK, CHUNK)
            dp_ref[:, pl.ds(h * CHUNK, CHUNK)] = _irope(drq_ref[:, sl].astype(F32), cosf, sinf).astype(BF16)
            dp_ref[:, pl.ds(HW + h * CHUNK, CHUNK)] = (_irope(drk_ref[:, sl].astype(F32), cosf, sinf)
                                                       * kscale).astype(BF16)
            dp_ref[:, pl.ds(2 * HW + h * CHUNK, CHUNK)] = drv_ref[:, sl].astype(BF16)
            dp_ref[:, pl.ds(3 * HW + h * CHUNK, CHUNK)] = dgt_ref[:, sl].astype(BF16)
            dx, dg = norm_bwd(p_ref[:, pl.ds(4 * HW + h * CHUNK, CHUNK)].astype(F32), qg_ref[...],
                              _irope(daq_ref[:, sl].astype(F32), cosf, sinf))
            dp_ref[:, pl.ds(4 * HW + h * CHUNK, CHUNK)] = dx.astype(BF16)
            dqg = dqg + dg
        for h in range(KV):
            sl = pl.ds(h * CHUNK, CHUNK)
            dx, dg = norm_bwd(p_ref[:, pl.ds(5 * HW + h * CHUNK, CHUNK)].astype(F32), kg_ref[...],
                              _irope(dak_ref[:, sl], cosf, sinf))
            dp_ref[:, pl.ds(5 * HW + h * CHUNK, CHUNK)] = dx.astype(BF16)
            dp_ref[:, pl.ds(5 * HW + (KV + h) * CHUNK, CHUNK)] = dav_ref[:, sl].astype(BF16)
            dkg = dkg + dg

        @pl.when(i == 0)
        def _():
            dqg_ref[...] = dqg
            dkg_ref[...] = dkg

        @pl.when(i > 0)
        def _():
            dqg_ref[...] += dqg
            dkg_ref[...] += dkg

    tab = pl.BlockSpec((TM, CHUNK), lambda i: (i % TPE, 0))
    vec = pl.BlockSpec((1, CHUNK), lambda i: (0, 0))
    hw = pl.BlockSpec((TM, HW), lambda i: (i, 0))
    kvw = pl.BlockSpec((TM, KV * CHUNK), lambda i: (i, 0))
    return pl.pallas_call(
        body, name=name, grid=(T // TM,),
        in_specs=[pl.BlockSpec((TM, cfg.ABW), lambda i: (i, 0)), tab, tab, vec, vec, hw, hw, hw, hw, hw, kvw, kvw],
        out_specs=[pl.BlockSpec((TM, cfg.ABW), lambda i: (i, 0)), vec, vec],
        out_shape=[jax.ShapeDtypeStruct((T, cfg.ABW), BF16), jax.ShapeDtypeStruct((1, CHUNK), F32),
                   jax.ShapeDtypeStruct((1, CHUNK), F32)],
        compiler_params=_params(("arbitrary",)),
    )(p, cosf, sinf, qg, kg, d_rq, d_rk, d_rv, d_gate, d_aq, d_ak, d_av)


def _ret_consts(direction, lg):
    C = CHUNK
    ii = lax.broadcasted_iota(jnp.int32, (C, C), 0)
    jj = lax.broadcasted_iota(jnp.int32, (C, C), 1)
    col = lax.broadcasted_iota(jnp.int32, (C, 1), 0).astype(F32)
    if direction == 0:
        mask, er, ek, eq = ii >= jj, (ii - jj).astype(F32), (C - 1.0) - col, col + 1.0
    else:
        mask, er, ek, eq = jj >= ii, (jj - ii).astype(F32), col, C - col
    er = jnp.where(mask, er, 0.0)
    dm = jnp.where(mask, jnp.exp(er * lg), 0.0)
    return dm, er, jnp.exp(ek * lg), ek, jnp.exp(eq * lg), eq, jnp.exp(C * lg)


def _ret_order(cfg, direction):
    n_all, n_ctx = cfg.S // CHUNK, cfg.SC // CHUNK
    if direction == 0:
        return list(range(n_all))
    return list(range(n_ctx - 1, -1, -1)) + list(range(n_all - 1, n_ctx - 1, -1))


def _carry_begin(carry, c_src, c_dst, sems, step):
    @pl.when(step == 0)
    def _():
        for cp in _carry_copies(carry, c_src, c_dst, *sems):
            cp.start()


def _carry_end(carry, c_src, c_dst, sems, step, n_steps):
    @pl.when(step == n_steps - 1)
    def _():
        for cp in _carry_copies(carry, c_src, c_dst, *sems):
            cp.wait()


def _carry_scratch(nc):
    return [pltpu.SemaphoreType.DMA((3 * nc,)), pltpu.SemaphoreType.DMA((3 * nc,)),
            pltpu.SemaphoreType.DMA((nc,))] if nc else []


def _head_norm_gate(o, g):
    mu = jnp.mean(o, axis=-1, keepdims=True)
    var = jnp.mean(jnp.square(o - mu), axis=-1, keepdims=True)
    rstd = lax.rsqrt(var + EPS)
    y = (o - mu) * rstd
    sg = jax.nn.sigmoid(g)
    return y, rstd, sg


RET_UNROLL = 3


def _retention_fwd(cfg, name, rq, rk, p, lgb, carry=None):
    B, H, S, T = cfg.B, cfg.H, cfg.S, cfg.T
    n_all = S // CHUNK
    nc = len(carry.srcs) if carry is not None else 0

    def body(*refs):
        q_ref, k_ref, v_ref, g_ref, lg_ref = refs[:5]
        c_src = refs[5:5 + nc]
        o_ref, ret_ref, st_ref = refs[5 + nc:8 + nc]
        c_dst = refs[8 + nc:8 + 2 * nc]
        sems = refs[8 + 2 * nc:8 + 2 * nc + 3] if nc else ()
        kv_ref = refs[-1]
        step = pl.program_id(0) * H + pl.program_id(1)
        if nc:
            _carry_begin(carry, c_src, c_dst, sems, step)

        def rows(n):
            return pl.ds(pl.multiple_of(n * CHUNK, CHUNK), CHUNK)

        (dm0, _, kd0, _, qd0, _, cd0), (dm1, _, kd1, _, qd1, _, cd1) = (
            _ret_consts(d, lg_ref[d, 0:1, 0:1]) for d in (0, 1))
        dm_both = dm0 + dm1

        def kv_step(n, c):
            k = k_ref[rows(n), :]
            v = v_ref[rows(n), :].astype(BF16)
            kv_ref[0, n] = _dot_tn((k * kd0).astype(BF16), v)
            kv_ref[1, n] = _dot_tn((k * kd1).astype(BF16), v)
            return c

        lax.fori_loop(0, n_all, kv_step, 0, unroll=RET_UNROLL)
        for direction, cd in ((0, cd0), (1, cd1)):
            st = jnp.zeros((CHUNK, CHUNK), F32)
            for t, n in enumerate(_ret_order(cfg, direction)):
                st_ref[direction, n] = st
                if t + 1 < n_all:
                    st = cd * st + kv_ref[direction, n]

        def out_step(n, c):
            q = q_ref[rows(n), :].astype(BF16)
            v = v_ref[rows(n), :].astype(BF16)
            s = _dot_nt(q, k_ref[rows(n), :].astype(BF16)) * dm_both
            states = jnp.concatenate([st_ref[0, n].astype(BF16), st_ref[1, n].astype(BF16)], axis=1)
            cross = _dot(q, states)
            o = _dot(s.astype(BF16), v) + cross[:, :CHUNK] * qd0 + cross[:, CHUNK:] * qd1
            o_ref[rows(n), :] = o
            g = g_ref[rows(n), :].astype(F32)
            y, _, sg = _head_norm_gate(o, g)
            ret_ref[rows(n), :] = (y * (g * sg)).astype(BF16)
            return c

        lax.fori_loop(0, n_all, out_step, 0, unroll=RET_UNROLL)
        if nc:
            _carry_end(carry, c_src, c_dst, sems, step, B * H)

    HW = H * CHUNK
    blk = lambda off: pl.BlockSpec((S, CHUNK), lambda b, h: (b, off + h))
    st_spec = pl.BlockSpec((None, None, 2, n_all, CHUNK, CHUNK), lambda b, h: (b, h, 0, 0, 0, 0))
    res = pl.pallas_call(
        body, name=name, grid=(B, H),
        in_specs=[blk(0), blk(0), blk(2 * H), blk(3 * H),
                  pl.BlockSpec((None, 2, 8, LANES), lambda b, h: (h, 0, 0, 0))] + [ANY] * nc,
        out_specs=[blk(0), blk(0), st_spec] + [ANY] * nc,
        out_shape=[jax.ShapeDtypeStruct((T, HW), F32), jax.ShapeDtypeStruct((T, 2 * HW), BF16),
                   jax.ShapeDtypeStruct((B, H, 2, n_all, CHUNK, CHUNK), F32)] + (_carry_out_shapes(carry) if nc else []),
        scratch_shapes=_carry_scratch(nc) + [pltpu.VMEM((2, n_all, CHUNK, CHUNK), F32)],
        compiler_params=_params(("arbitrary", "arbitrary") if nc else ("parallel", "parallel")),
    )(rq, rk, p, p, lgb, *(carry.srcs if nc else ()))
    return res[:3], res[3:]


def _retention_bwd(cfg, name, rq, rk, p, o_sum, states, dcat, lgb, carry=None):
    B, H, S, T = cfg.B, cfg.H, cfg.S, cfg.T
    n_all = S // CHUNK
    C = CHUNK
    nc = len(carry.srcs) if carry is not None else 0

    def body(*refs):
        q_ref, k_ref, v_ref, g_ref, o_ref, st_ref, dr_ref, lg_ref = refs[:8]
        c_src = refs[8:8 + nc]
        dq_ref, dk_ref, dv_ref, dg_ref, dlg_ref = refs[8 + nc:13 + nc]
        c_dst = refs[13 + nc:13 + 2 * nc]
        sems = refs[13 + 2 * nc:13 + 2 * nc + 3] if nc else ()
        do_ref, gq_ref, ds_ref, acc_ref = refs[-4:]
        step = pl.program_id(0) * H + pl.program_id(1)
        if nc:
            _carry_begin(carry, c_src, c_dst, sems, step)

        def rows(n):
            return pl.ds(pl.multiple_of(n * C, C), C)

        (dm0, er0, kd0, ek0, qd0, eq0, cd0), (dm1, er1, kd1, ek1, qd1, eq1, cd1) = (
            _ret_consts(d, lg_ref[d, 0:1, 0:1]) for d in (0, 1))
        dm_both = dm0 + dm1
        wdm0, wdm1 = dm0 * er0, dm1 * er1

        def side(a, b):
            return jnp.concatenate([a.astype(BF16), b.astype(BF16)], axis=1)

        def gq_step(n, c):
            g = g_ref[rows(n), :].astype(F32)
            dr = dr_ref[rows(n), :].astype(F32)
            y, rstd, sg = _head_norm_gate(o_ref[rows(n), :], g)
            dy = dr * (g * sg)
            dg_ref[rows(n), :] = (dr * y * (sg * (1.0 + g * (1.0 - sg)))).astype(BF16)
            do = rstd * (dy - jnp.mean(dy, axis=-1, keepdims=True) - y * jnp.mean(dy * y, axis=-1, keepdims=True))
            do_ref[rows(n), :] = do
            gq = _dot_tn(q_ref[rows(n), :].astype(BF16), side(do * qd0, do * qd1))
            gq_ref[0, n] = gq[:, :C]
            gq_ref[1, n] = gq[:, C:]
            return c

        lax.fori_loop(0, n_all, gq_step, 0, unroll=RET_UNROLL)
        for direction, cd in ((0, cd0), (1, cd1)):
            order = _ret_order(cfg, direction)
            ds = jnp.zeros((C, C), F32)
            for t in reversed(range(n_all)):
                ds_ref[direction, order[t]] = ds
                if t > 0:
                    ds = cd * ds + gq_ref[direction, order[t]]
        acc_ref[...] = jnp.zeros_like(acc_ref)

        def chunk_step(n, c):
            q = q_ref[rows(n), :].astype(BF16)
            kf = k_ref[rows(n), :]
            k = kf.astype(BF16)
            v = v_ref[rows(n), :].astype(BF16)
            do = do_ref[rows(n), :]
            dob = do.astype(BF16)
            sp0, sp1 = st_ref[0, n], st_ref[1, n]
            ds0, ds1 = ds_ref[0, n], ds_ref[1, n]
            states = side(sp0, sp1)
            dstates = jnp.concatenate([ds0.astype(BF16), ds1.astype(BF16)], axis=0)
            doq0, doq1 = do * qd0, do * qd1
            doq = side(doq0, doq1)
            s_raw = _dot_nt(q, k)
            dpm = _dot_nt(dob, v)
            dsr = (dpm * dm_both).astype(BF16)
            dks = _dot_nt(v, dstates)
            dks0, dks1 = dks[:, :C] * kd0, dks[:, C:] * kd1
            qs = _dot(q, states)
            dq_ref[rows(n), :] = (_dot(dsr, k) + _dot_nt(doq, states)).astype(BF16)
            dk_ref[rows(n), :] = (_dot_tn(dsr, q) + dks0 + dks1).astype(BF16)
            dv_ref[rows(n), :] = (_dot_tn((s_raw * dm_both).astype(BF16), dob)
                                  + _dot(side(kf * kd0, kf * kd1), dstates)).astype(BF16)
            inner = dpm * s_raw
            acc_ref[0] += (jnp.sum(inner * wdm0, axis=0, keepdims=True)
                           + jnp.sum(eq0 * doq0 * qs[:, :C], axis=0, keepdims=True)
                           + jnp.sum(ek0 * kf * dks0, axis=0, keepdims=True)
                           + (C * cd0) * jnp.sum(ds0 * sp0, axis=0, keepdims=True))
            acc_ref[1] += (jnp.sum(inner * wdm1, axis=0, keepdims=True)
                           + jnp.sum(eq1 * doq1 * qs[:, C:], axis=0, keepdims=True)
                           + jnp.sum(ek1 * kf * dks1, axis=0, keepdims=True)
                           + (C * cd1) * jnp.sum(ds1 * sp1, axis=0, keepdims=True))
            return c

        lax.fori_loop(0, n_all, chunk_step, 0, unroll=RET_UNROLL)
        for direction in (0, 1):
            dlg_ref[direction] = jnp.broadcast_to(jnp.sum(acc_ref[direction], axis=1, keepdims=True), (8, LANES))
        if nc:
            _carry_end(carry, c_src, c_dst, sems, step, B * H)

    HW = H * CHUNK
    blk = lambda off: pl.BlockSpec((S, CHUNK), lambda b, h: (b, off + h))
    st_spec = pl.BlockSpec((None, None, 2, n_all, C, C), lambda b, h: (b, h, 0, 0, 0, 0))
    res = pl.pallas_call(
        body, name=name, grid=(B, H),
        in_specs=[blk(0), blk(0), blk(2 * H), blk(3 * H), blk(0), st_spec, blk(0),
                  pl.BlockSpec((None, 2, 8, LANES), lambda b, h: (h, 0, 0, 0))] + [ANY] * nc,
        out_specs=[blk(0)] * 4 + [pl.BlockSpec((None, None, 2, 8, LANES), lambda b, h: (b, h, 0, 0, 0))] + [ANY] * nc,
        out_shape=[jax.ShapeDtypeStruct((T, HW), BF16)] * 4 + [jax.ShapeDtypeStruct((B, H, 2, 8, LANES), F32)]
        + (_carry_out_shapes(carry) if nc else []),
        scratch_shapes=_carry_scratch(nc) + [pltpu.VMEM((S, CHUNK), F32), pltpu.VMEM((2, n_all, C, C), F32),
                                             pltpu.VMEM((2, n_all, C, C), F32), pltpu.VMEM((2, 1, C), F32)],
        compiler_params=_params(("arbitrary", "arbitrary") if nc else ("parallel", "parallel")),
    )(rq, rk, p, p, o_sum, states, dcat, lgb, *(carry.srcs if nc else ()))
    return res[:5], res[5:]


ATT_SCALE = CHUNK ** -0.5


def _attn_scores(cfg, q, k, t):
    kcol = lax.broadcasted_iota(jnp.int32, (1, cfg.S), 1)
    bias = jnp.where(jnp.logical_or(t > 0, kcol < cfg.SC), 0.0, -1e30)
    return _dot_nt(q, k) + bias


def _attention_fwd(cfg, name, aq, ak, p, cat, carry=None):
    B, H, KV, S, T, TM, TPE = cfg.B, cfg.H, cfg.KV, cfg.S, cfg.T, cfg.TM, cfg.TPE
    G = H // KV
    v_off = (5 * H + KV)
    nc = len(carry.srcs) if carry is not None else 0

    def body(*refs):
        q_ref, k_ref, v_ref = refs[:3]
        o_ref, lse_ref = refs[4 + nc:6 + nc]
        c_src, c_dst, sems = refs[4:4 + nc], refs[6 + nc:6 + 2 * nc], refs[6 + 2 * nc:]
        step = (pl.program_id(0) * KV + pl.program_id(1)) * TPE + pl.program_id(2)
        if nc:
            _carry_begin(carry, c_src, c_dst, sems, step)
        k = k_ref[...]
        v = v_ref[...].astype(BF16)
        for g in range(G):
            cols = pl.ds(g * CHUNK, CHUNK)
            s = _attn_scores(cfg, q_ref[:, cols], k, pl.program_id(2))
            m = jnp.max(s, axis=-1, keepdims=True)
            e = jnp.exp(s - m)
            total = jnp.sum(e, axis=-1, keepdims=True)
            o_ref[:, cols] = (_dot(e.astype(BF16), v) * (1.0 / total)).astype(BF16)
            lse_ref[g] = m + jnp.log(total)
        if nc:
            _carry_end(carry, c_src, c_dst, sems, step, B * KV * TPE)

    res = pl.pallas_call(
        body, name=name, grid=(B, KV, TPE),
        in_specs=[pl.BlockSpec((TM, G * CHUNK), lambda b, kv, t: (b * TPE + t, kv)),
                  pl.BlockSpec((S, CHUNK), lambda b, kv, t: (b, kv)),
                  pl.BlockSpec((S, CHUNK), lambda b, kv, t: (b, v_off + kv)), ANY] + [ANY] * nc,
        out_specs=[pl.BlockSpec((TM, G * CHUNK), lambda b, kv, t: (b * TPE + t, KV + kv)),
                   pl.BlockSpec((G, TM, 1), lambda b, kv, t: (kv, b * TPE + t, 0))] + [ANY] * nc,
        out_shape=[jax.ShapeDtypeStruct(cat.shape, cat.dtype), jax.ShapeDtypeStruct((H, T, 1), F32)]
        + (_carry_out_shapes(carry) if nc else []),
        input_output_aliases={3: 0},
        scratch_shapes=_carry_scratch(nc),
        compiler_params=_params(("arbitrary",) * 3 if nc else ("parallel",) * 3),
    )(aq, ak, p, cat, *(carry.srcs if nc else ()))
    return res[:2], res[2:]


def _attention_bwd(cfg, name, aq, ak, p, cat, lse, dcat, carry=None):
    B, H, KV, S, T, TM, TPE = cfg.B, cfg.H, cfg.KV, cfg.S, cfg.T, cfg.TM, cfg.TPE
    G = H // KV
    v_off = (5 * H + KV)
    nc = len(carry.srcs) if carry is not None else 0

    def body(*refs):
        q_ref, k_ref, v_ref, o_ref, lse_ref, do_ref = refs[:6]
        dq_ref, dk_ref, dv_ref = refs[6 + nc:9 + nc]
        c_src, c_dst, sems = refs[6:6 + nc], refs[9 + nc:9 + 2 * nc], refs[9 + 2 * nc:]
        t = pl.program_id(2)
        step = (pl.program_id(0) * KV + pl.program_id(1)) * TPE + t
        if nc:
            _carry_begin(carry, c_src, c_dst, sems, step)
        k = k_ref[...]
        v = v_ref[...].astype(BF16)
        dk = dv = None
        for g in range(G):
            cols = pl.ds(g * CHUNK, CHUNK)
            q = q_ref[:, cols]
            do = do_ref[:, cols]
            pr = jnp.exp(_attn_scores(cfg, q, k, t) - lse_ref[g])
            delta = jnp.sum(do.astype(F32) * o_ref[:, cols].astype(F32), axis=-1, keepdims=True)
            ds = (pr * (_dot_nt(do, v) - delta)).astype(BF16)
            dq_ref[:, cols] = (_dot(ds, k) * ATT_SCALE).astype(BF16)
            dk_g, dv_g = _dot_tn(ds, q), _dot_tn(pr.astype(BF16), do)
            dk, dv = (dk_g, dv_g) if dk is None else (dk + dk_g, dv + dv_g)

        @pl.when(t == 0)
        def _():
            dk_ref[...] = dk
            dv_ref[...] = dv

        @pl.when(t > 0)
        def _():
            dk_ref[...] += dk
            dv_ref[...] += dv

        if nc:
            _carry_end(carry, c_src, c_dst, sems, step, B * KV * TPE)

    qspec = pl.BlockSpec((TM, G * CHUNK), lambda b, kv, t: (b * TPE + t, kv))
    kvspec = pl.BlockSpec((S, CHUNK), lambda b, kv, t: (b, kv))
    right = pl.BlockSpec((TM, G * CHUNK), lambda b, kv, t: (b * TPE + t, KV + kv))
    res = pl.pallas_call(
        body, name=name, grid=(B, KV, TPE),
        in_specs=[qspec, kvspec, pl.BlockSpec((S, CHUNK), lambda b, kv, t: (b, v_off + kv)), right,
                  pl.BlockSpec((G, TM, 1), lambda b, kv, t: (kv, b * TPE + t, 0)), right] + [ANY] * nc,
        out_specs=[qspec, kvspec, kvspec] + [ANY] * nc,
        out_shape=[jax.ShapeDtypeStruct((T, H * CHUNK), BF16), jax.ShapeDtypeStruct((T, KV * CHUNK), F32),
                   jax.ShapeDtypeStruct((T, KV * CHUNK), F32)] + (_carry_out_shapes(carry) if nc else []),
        scratch_shapes=_carry_scratch(nc),
        compiler_params=_params(("arbitrary",) * 3 if nc else ("parallel", "parallel", "arbitrary")),
    )(aq, ak, p, cat, lse, dcat, *(carry.srcs if nc else ()))
    return res[:3], res[3:]


_GELU_C = math.sqrt(2.0 / math.pi)


def _gelu(x):
    return 0.5 * x * (1.0 + jnp.tanh(_GELU_C * (x + 0.044715 * x * x * x)))


def _gelu_and_grad(x):
    x2 = x * x
    th = jnp.tanh(_GELU_C * (x + 0.044715 * x * x2))
    half = 0.5 * (1.0 + th)
    return x * half, half + 0.5 * x * (1.0 - th * th) * _GELU_C * (1.0 + 3.0 * 0.044715 * x2)


def _cm_fwd(cfg, name, a, vg, ws, bs):
    T, TM, W, NG = cfg.T, cfg.TM, cfg.CMW, cfg.CMG

    def body(a_ref, vg_ref, ws_ref, bs_ref, m_ref):
        v = _gelu(a_ref[:, pl.ds(W, W)].astype(F32))
        vn = (v * lax.rsqrt(jnp.mean(v * v, axis=-1, keepdims=True) + EPS) * vg_ref[...]).astype(BF16)
        for c in range(TM // CHUNK):
            for g in range(NG):
                rows, cols = slice(c * CHUNK, (c + 1) * CHUNK), slice(g * CHUNK, (g + 1) * CHUNK)
                sv = _dot(ws_ref[g].astype(BF16), vn[rows, cols]) + bs_ref[g]
                u = _gelu(a_ref[pl.ds(c * CHUNK, CHUNK), pl.ds(g * CHUNK, CHUNK)].astype(F32))
                m_ref[pl.ds(c * CHUNK, CHUNK), pl.ds(g * CHUNK, CHUNK)] = (u * sv).astype(BF16)

    return pl.pallas_call(
        body, name=name, grid=(T // TM,),
        in_specs=[pl.BlockSpec((TM, 2 * W), lambda i: (i, 0)), pl.BlockSpec((1, W), lambda i: (0, 0)),
                  pl.BlockSpec((NG, CHUNK, CHUNK), lambda i: (0, 0, 0)),
                  pl.BlockSpec((NG, CHUNK, 1), lambda i: (0, 0, 0))],
        out_specs=pl.BlockSpec((TM, W), lambda i: (i, 0)),
        out_shape=jax.ShapeDtypeStruct((T, W), BF16),
        compiler_params=_params(("parallel",)),
    )(a, vg, ws, bs)


def _cm_bwd(cfg, name, a, vg, ws, bs, dm):
    T, TM, W, NG = cfg.T, cfg.TM, cfg.CMW, cfg.CMG

    def body(a_ref, vg_ref, ws_ref, bs_ref, dm_ref, da_ref, dws_ref, dbs_ref, dvg_ref, dvn_ref):
        i = pl.program_id(0)

        @pl.when(i == 0)
        def _():
            dws_ref[...] = jnp.zeros_like(dws_ref)
            dbs_ref[...] = jnp.zeros_like(dbs_ref)
            dvg_ref[...] = jnp.zeros_like(dvg_ref)

        v, v_grad = _gelu_and_grad(a_ref[:, pl.ds(W, W)].astype(F32))
        rstd = lax.rsqrt(jnp.mean(v * v, axis=-1, keepdims=True) + EPS)
        xhat = v * rstd
        vg = vg_ref[...]
        vn = (xhat * vg).astype(BF16)
        for c in range(TM // CHUNK):
            for g in range(NG):
                rows, cols = slice(c * CHUNK, (c + 1) * CHUNK), slice(g * CHUNK, (g + 1) * CHUNK)
                rs, cs = pl.ds(c * CHUNK, CHUNK), pl.ds(g * CHUNK, CHUNK)
                wsb = ws_ref[g].astype(BF16)
                blk = vn[rows, cols]
                sv = _dot(wsb, blk) + bs_ref[g]
                u, u_grad = _gelu_and_grad(a_ref[rs, cs].astype(F32))
                dmb = dm_ref[rs, cs].astype(F32)
                da_ref[rs, cs] = (dmb * sv * u_grad).astype(BF16)
                dsv = dmb * u
                dsvb = dsv.astype(BF16)
                dbs_ref[g] += jnp.sum(dsv, axis=1, keepdims=True)
                dws_ref[g] += _dot_nt(dsvb, blk)
                dvn_ref[rs, cs] = _dot_tn(wsb, dsvb)
        dvn = dvn_ref[...]
        dvg_ref[...] += jnp.sum(dvn * xhat, axis=0, keepdims=True)
        dxh = dvn * vg
        dv = rstd * (dxh - xhat * jnp.mean(dxh * xhat, axis=-1, keepdims=True))
        da_ref[:, pl.ds(W, W)] = (dv * v_grad).astype(BF16)

    return pl.pallas_call(
        body, name=name, grid=(T // TM,),
        in_specs=[pl.BlockSpec((TM, 2 * W), lambda i: (i, 0)), pl.BlockSpec((1, W), lambda i: (0, 0)),
                  pl.BlockSpec((NG, CHUNK, CHUNK), lambda i: (0, 0, 0)),
                  pl.BlockSpec((NG, CHUNK, 1), lambda i: (0, 0, 0)), pl.BlockSpec((TM, W), lambda i: (i, 0))],
        out_specs=[pl.BlockSpec((TM, 2 * W), lambda i: (i, 0)), pl.BlockSpec((NG, CHUNK, CHUNK), lambda i: (0, 0, 0)),
                   pl.BlockSpec((NG, CHUNK, 1), lambda i: (0, 0, 0)), pl.BlockSpec((1, W), lambda i: (0, 0))],
        out_shape=[jax.ShapeDtypeStruct((T, 2 * W), BF16), jax.ShapeDtypeStruct((NG, CHUNK, CHUNK), F32),
                   jax.ShapeDtypeStruct((NG, CHUNK, 1), F32), jax.ShapeDtypeStruct((1, W), F32)],
        scratch_shapes=[pltpu.VMEM((TM, W), F32)],
        compiler_params=_params(("arbitrary",)),
    )(a, vg, ws, bs, dm)


def _layer_weights(l):
    mixer = ("ab_w_in", "ab_w_out") if l % 2 == 0 else ("cm_w_in", "cm_w_out")
    return [(mixer[0], l // 2), (mixer[1], l // 2), ("ff_w1", l), ("ff_w2", l)]


def _local_step(cfg, xcat, tgt, mods, shards, w):
    D, TM, H = cfg.D, cfg.TM, cfg.H
    cosf, sinf = _rope_tables(cfg)
    full, big, recv = {}, {}, {}

    def gather_of(keys):
        return _Carry("gather", tuple(shards[n][i] for n, i in keys), tuple(BIG[n] for n, _ in keys))

    def exchange_of(keys):
        return _Carry("exchange", tuple(big[k] for k in keys), tuple(BIG[n] for n, _ in keys))

    def mm(pending, name, a, b, **kw):
        if not pending:
            return _mm(name, a, b, **kw)
        key, carry, sink = pending.pop(0)
        out, (got,) = _mm(name, a, b, carry=carry, **kw)
        sink[key] = got
        return out

    def with_carry(call, keys, make, sink):
        out, got = call(carry=make(keys) if keys else None)
        sink.update(zip(keys, got))
        return out

    keys0 = _layer_weights(0)
    full[keys0[0]], = _comm_call("gather_weights_0", gather_of(keys0[:1]))
    TG = 3 * TM if cfg.TPE % 3 == 0 else TM
    tiles_per_ex = cfg.S // TG
    gate_spec = pl.BlockSpec((None, 2, 6, D), lambda i, j, k: (i // tiles_per_ex, 0, 0, 0))

    def resid_epi(igate, nxt):
        def epi(acc, row_tile, x_ref, mod_ref, *nxt_refs):
            row = lax.broadcasted_iota(jnp.int32, (TG, 1), 0)
            is_ctx = jnp.logical_and(row_tile % tiles_per_ex == 0, row < cfg.SC)

            def pick(ref, idx):
                return jnp.where(is_ctx, ref[0, pl.ds(idx, 1), :], ref[1, pl.ds(idx, 1), :])

            x = x_ref[...] + pick(mod_ref, igate) * acc
            if nxt is None:
                return x, acc
            gain_ref, modn_ref = nxt_refs
            n = x * lax.rsqrt(jnp.mean(x * x, axis=-1, keepdims=True) + EPS) * gain_ref[...]
            return x, acc, n * (1.0 + pick(modn_ref, nxt[3])) + pick(modn_ref, nxt[2])
        return epi

    def gated_out(pending, name, a, key, x, mod, igate, nxt=None):
        extras = [(x, pl.BlockSpec((TG, D), lambda i, j, k: (i, j))), (mod, gate_spec)]
        if nxt is not None:
            extras += [(nxt[0], pl.BlockSpec((1, D), lambda i, j, k: (0, 0))), (nxt[1], gate_spec)]
        return mm(pending, name, a, full[key], mode="nn", tm=TG, tn=D, outs=[F32, BF16] + [BF16] * (nxt is not None),
                  epi=resid_epi(igate, nxt), extras=extras, resident=True)

    saved = []
    x = xcat
    h = _norm_mod_fwd(cfg, "norm1_fwd_0", x, w["norm1_g"][0][None], mods[0], 0, 1)
    for l in range(N_LAYERS):
        li = l // 2
        mod = mods[l]
        k_in, k_out, k_ff1, k_ff2 = _layer_weights(l)
        pend = [(k, gather_of([k]), full) for k in _layer_weights(l + 1)] if l + 1 < N_LAYERS else []
        norm2 = (w["norm2_g"][l][None], mod, 3, 4)
        s = {"x0": x, "h": h}
        if l % 2 == 0:
            lgb = jnp.broadcast_to(jax.nn.log_sigmoid(w["ret_decay"][li]).T[:, :, None, None], (H, 2, 8, LANES))
            qg, kg = w["att_q_norm_g"][li][None], w["att_k_norm_g"][li][None]
            s["p"] = mm(pend, f"ab_in_{l}", s["h"], full[k_in], mode="nn", outs=[BF16], tn=768)
            s["rq"], s["rk"], s["aq"], s["ak"] = _prep_fwd(cfg, f"prep_fwd_{l}", s["p"], cosf, sinf, qg, kg)
            s["o"], ret, s["st"] = with_carry(
                functools.partial(_retention_fwd, cfg, f"ret_fwd_{l}", s["rq"], s["rk"], s["p"], lgb),
                keys0[1:3] if l == 0 else [], gather_of, full)
            s["cat"], s["lse"] = with_carry(
                functools.partial(_attention_fwd, cfg, f"att_fwd_{l}", s["aq"], s["ak"], s["p"], ret),
                keys0[3:] if l == 0 else [], gather_of, full)
            s["lgb"], s["qg"], s["kg"] = lgb, qg, kg
            x, s["y1"], s["h2"] = gated_out(pend, f"ab_out_{l}", s["cat"], k_out, x, mod, 2, norm2)
        else:
            s["a"] = mm(pend, f"cm_in_{l}", s["h"], full[k_in], mode="nn", outs=[BF16])
            s["vg"], s["ws"], s["bs"] = w["cm_v_norm_g"][li][None], w["cm_w_s"][li], w["cm_b_s"][li][:, :, None]
            s["m"] = _cm_fwd(cfg, f"cm_fwd_{l}", s["a"], s["vg"], s["ws"], s["bs"])
            x, s["y1"], s["h2"] = gated_out(pend, f"cm_out_{l}", s["m"], k_out, x, mod, 2, norm2)
        s["x1"] = x
        s["r"] = mm(pend, f"ff1_{l}", s["h2"], full[k_ff1], mode="nn", outs=[BF16],
                    epi=lambda acc, row_tile: (jnp.square(jnp.maximum(acc, 0.0)),))
        if l + 1 < N_LAYERS:
            x, s["y2"], h = gated_out(pend, f"ff2_{l}", s["r"], k_ff2, x, mod, 5,
                                      (w["norm1_g"][l + 1][None], mods[l + 1], 0, 1))
        else:
            x, s["y2"] = gated_out(pend, f"ff2_{l}", s["r"], k_ff2, x, mod, 5)
        saved.append(s)

    loss, dx = _loss_grad(cfg, x, tgt)

    small = {k: [None] * n for k, n in (("norm1_g", 4), ("norm2_g", 4), ("ret_lg", 2), ("att_q_norm_g", 2),
                                        ("att_k_norm_g", 2), ("cm_v_norm_g", 2), ("cm_w_s", 2), ("cm_b_s", 2))}
    dmods = [None] * N_LAYERS

    for l in reversed(range(N_LAYERS)):
        li = l // 2
        s, mod = saved[l], mods[l]
        k_in, k_out, k_ff1, k_ff2 = _layer_weights(l)
        above = _layer_weights(l + 1) if l + 1 < N_LAYERS else [None] * 4
        if l == N_LAYERS - 1:
            dy2, dg2 = _gate_bwd(cfg, f"gate2_bwd_{l}", dx, s["y2"], mod, 5)
        da2, big[k_ff2] = with_carry(functools.partial(_ff_bwd, f"ff2_bwd_{l}", dy2, s["r"], full[k_ff2], "w2"),
                                     [above[3], above[1]] if above[0] else [], exchange_of, recv)
        dh2, big[k_ff1] = with_carry(functools.partial(_ff_bwd, f"ff1_bwd_{l}", s["h2"], da2, full[k_ff1], "w1"),
                                     [above[2], above[0]] if above[0] else [], exchange_of, recv)
        dx, dm2, small["norm2_g"][l], do, dg1 = _norm_mod_bwd(
            cfg, f"norm2_bwd_{l}", s["x1"], w["norm2_g"][l][None], mod, 3, 4, dh2, dx, gate=(s["y1"], mod, 2))
        if l % 2 == 0:
            big[k_out] = _mm(f"ab_out_dw_{l}", s["cat"], do, mode="tn", outs=[BF16])
            dcat = _mm(f"ab_out_dx_{l}", do, full[k_out], mode="nt", outs=[BF16])
            d_rq, d_rk, d_rv, d_gt, dlg = with_carry(
                functools.partial(_retention_bwd, cfg, f"ret_bwd_{l}", s["rq"], s["rk"], s["p"], s["o"], s["st"], dcat,
                                  s["lgb"]), [k_ff2, k_ff1] if l == 0 else [], exchange_of, recv)
            d_aq, d_ak, d_av = with_carry(
                functools.partial(_attention_bwd, cfg, f"att_bwd_{l}", s["aq"], s["ak"], s["p"], s["cat"], s["lse"], dcat),
                [k_out] if l == 0 else [], exchange_of, recv)
            dp, dqg, dkg = _prep_bwd(cfg, f"prep_bwd_{l}", s["p"], cosf, sinf, s["qg"], s["kg"],
                                     d_rq, d_rk, d_rv, d_gt, d_aq, d_ak, d_av)
            small["ret_lg"][li] = jnp.sum(dlg[:, :, :, 0, 0], axis=0).T
            small["att_q_norm_g"][li], small["att_k_norm_g"][li] = dqg[0], dkg[0]
            big[k_in] = _mm(f"ab_in_dw_{l}", s["h"], dp, mode="tn", outs=[BF16])
            last = [(k_in, exchange_of([k_in]), recv)] if l == 0 else []
            dh = mm(last, f"ab_in_dx_{l}", dp, full[k_in], mode="nt", outs=[BF16], tk=768)
        else:
            big[k_out] = _mm(f"cm_out_dw_{l}", s["m"], do, mode="tn", outs=[BF16])
            dm = _mm(f"cm_out_dx_{l}", do, full[k_out], mode="nt", outs=[BF16])
            da, dws, dbs, dvg = _cm_bwd(cfg, f"cm_bwd_{l}", s["a"], s["vg"], s["ws"], s["bs"], dm)
            small["cm_w_s"][li], small["cm_b_s"][li], small["cm_v_norm_g"][li] = dws, dbs[:, :, 0], dvg[0]
            big[k_in] = _mm(f"cm_in_dw_{l}", s["h"], da, mode="tn", outs=[BF16])
            dh = _mm(f"cm_in_dx_{l}", da, full[k_in], mode="nt", outs=[BF16])
        below = (saved[l - 1]["y2"], mods[l - 1], 5) if l > 0 else None
        dx, dm1, small["norm1_g"][l], *rest = _norm_mod_bwd(
            cfg, f"norm1_bwd_{l}", s["x0"], w["norm1_g"][l][None], mod, 0, 1, dh, dx, gate=below, lat_only=l == 0)
        dmods[l] = jnp.concatenate([dm1, dg1, dm2, dg2], axis=2)
        if l > 0:
            dy2, dg2 = rest
    return loss, dx, recv, small, dmods


N_DEV = 8
N_CHIP = 4
MESH = pl.DeviceIdType.MESH
ANY = pl.BlockSpec(memory_space=pl.ANY)
BIG = {"ab_w_in": 1, "ab_w_out": 0, "cm_w_in": 1, "cm_w_out": 0, "ff_w1": 1, "ff_w2": 0}


class _Carry(NamedTuple):
    kind: str
    srcs: tuple
    axes: tuple


def _place():
    x, y, c = lax.axis_index("x"), lax.axis_index("y"), lax.axis_index("c")
    return x, y, c, [(1 - x, y), (x, 1 - y), (1 - x, 1 - y)]


def _shard_of(ref, axis, s, width):
    start = pl.multiple_of(s * width, LANES)
    if axis == 0:
        return ref.at[pl.ds(start, width), :]
    return ref.at[:, pl.ds(start, width)]


def _carry_out_shapes(carry):
    shapes = []
    for src, axis in zip(carry.srcs, carry.axes):
        shape = list(src.shape)
        if carry.kind == "swap":
            pass
        elif carry.kind == "gather":
            shape[axis] *= N_CHIP
        else:
            shape[axis] //= N_CHIP
            shape = [N_CHIP] + shape
        shapes.append(jax.ShapeDtypeStruct(tuple(shape), src.dtype))
    return shapes


def _carry_copies(carry, srcs, dsts, send_sems, recv_sems, local_sems):
    x, y, c, chips = _place()
    me = 2 * x + y
    copies = []
    if carry.kind == "swap":
        return [pltpu.make_async_remote_copy(
            src_ref=srcs[t], dst_ref=dsts[t], send_sem=send_sems.at[3 * t], recv_sem=recv_sems.at[3 * t],
            device_id=(x, y, 1 - c), device_id_type=MESH) for t in range(len(srcs))]
    for t, axis in enumerate(carry.axes):
        if carry.kind == "gather":
            own = _shard_of(dsts[t], axis, me, srcs[t].shape[axis])
            copies.append(pltpu.make_async_copy(srcs[t], own, local_sems.at[t]))
            parts = [(srcs[t], own)] * 3
        else:
            width = dsts[t].shape[1 + axis]
            copies.append(pltpu.make_async_copy(_shard_of(srcs[t], axis, me, width), dsts[t].at[3], local_sems.at[t]))
            parts = [(_shard_of(srcs[t], axis, 2 * px + py, width), dsts[t].at[j]) for j, (px, py) in enumerate(chips)]
        for j, (px, py) in enumerate(chips):
            copies.append(pltpu.make_async_remote_copy(
                src_ref=parts[j][0], dst_ref=parts[j][1], send_sem=send_sems.at[3 * t + j],
                recv_sem=recv_sems.at[3 * t + j], device_id=(px, py, c), device_id_type=MESH))
    return copies


def _comm_call(name, carry):
    nc = len(carry.srcs)

    def body(*refs):
        copies = _carry_copies(carry, refs[:nc], refs[nc:2 * nc], *refs[2 * nc:])
        for cp in copies:
            cp.start()
        for cp in copies:
            cp.wait()

    return pl.pallas_call(
        body, name=name, out_shape=_carry_out_shapes(carry), in_specs=[ANY] * nc, out_specs=[ANY] * nc,
        scratch_shapes=[pltpu.SemaphoreType.DMA((3 * nc,)), pltpu.SemaphoreType.DMA((3 * nc,)),
                        pltpu.SemaphoreType.DMA((nc,))],
    )(*carry.srcs)


def _allgather8(name, block, carry=None):
    m_per, n = block.shape
    nc = len(carry.srcs) if carry is not None else 0

    def body(*refs):
        x_ref, out_ref = refs[0], refs[1 + nc]
        send_sems, recv_sems, local_sem = refs[2 + 2 * nc:5 + 2 * nc]
        carried = _carry_copies(carry, refs[1:1 + nc], refs[2 + nc:2 + 2 * nc], *refs[5 + 2 * nc:]) if nc else []
        for cp in carried:
            cp.start()
        x, y, c, chips = _place()
        me, sibling = (x, y, c), (x, y, 1 - c)

        def rows(px, py, pc):
            return out_ref.at[pl.ds((4 * px + 2 * py + pc) * m_per, m_per), :]

        def copy(k, blk, to, src=None):
            return pltpu.make_async_remote_copy(
                src_ref=rows(*blk) if src is None else src, dst_ref=rows(*blk),
                send_sem=send_sems.at[k], recv_sem=recv_sems.at[k], device_id=to, device_id_type=MESH)

        mine = pltpu.make_async_copy(x_ref, rows(*me), local_sem)
        mine.start()
        first = [copy(0, me, sibling, src=x_ref)]
        first += [copy(1 + j, me, (*chip, c), src=x_ref) for j, chip in enumerate(chips)]
        for cp in first:
            cp.start()
        passed = [copy(4 + j, (*chip, c), sibling) for j, chip in enumerate(chips)]
        for j, chip in enumerate(chips):
            copy(1 + j, (*chip, c), me).wait_recv()
            passed[j].start()
        copy(0, sibling, me).wait_recv()
        for j, chip in enumerate(chips):
            copy(4 + j, (*chip, 1 - c), me).wait_recv()
        for cp in first + passed:
            cp.wait_send()
        mine.wait()
        for cp in carried:
            cp.wait()

    res = pl.pallas_call(
        body, name=name,
        out_shape=[jax.ShapeDtypeStruct((N_DEV * m_per, n), block.dtype)] + (_carry_out_shapes(carry) if nc else []),
        in_specs=[pl.BlockSpec(memory_space=pltpu.VMEM)] + [ANY] * nc,
        out_specs=[pl.BlockSpec(memory_space=pltpu.VMEM)] + [ANY] * nc,
        scratch_shapes=[pltpu.SemaphoreType.DMA((7,)), pltpu.SemaphoreType.DMA((7,)), pltpu.SemaphoreType.DMA]
        + _carry_scratch(nc),
        compiler_params=pltpu.CompilerParams(vmem_limit_bytes=VMEM_LIMIT),
    )(block, *(carry.srcs if nc else ()))
    return (res[0], res[1:]) if nc else res[0]


def _rows_view(a):
    if a.ndim == 1:
        return a.reshape(1, a.shape[0])
    return a.reshape(-1, a.shape[-1])


def _row_tile(rows, cols, target_elems=1 << 17):
    tr = rows
    while tr % 16 == 0 and tr * cols > target_elems:
        tr //= 2
    return tr


def _sum_leading(name, a):
    n, rows, cols = a.shape
    tr = _row_tile(rows, cols * n, target_elems=1 << 20)

    def body(a_ref, o_ref):
        acc = a_ref[0].astype(F32)
        for i in range(1, n):
            acc = acc + a_ref[i].astype(F32)
        o_ref[...] = acc

    return pl.pallas_call(
        body, name=name, grid=(rows // tr,),
        in_specs=[pl.BlockSpec((n, tr, cols), lambda i: (0, i, 0))],
        out_specs=pl.BlockSpec((tr, cols), lambda i: (i, 0)),
        out_shape=jax.ShapeDtypeStruct((rows, cols), F32),
        compiler_params=_params(("parallel",)),
    )(a)


def _sum_parts_layers(name, parts):
    n_layers = len(parts)
    n, rows, cols = parts[0].shape
    tr = _row_tile(rows, cols * n, target_elems=1 << 20)

    def body(*refs):
        o_ref = refs[n_layers]
        layer = pl.program_id(0)
        for k in range(n_layers):
            @pl.when(layer == k)
            def _():
                acc = refs[k][0].astype(F32)
                for i in range(1, n):
                    acc = acc + refs[k][i].astype(F32)
                o_ref[...] = acc

    def in_spec(k):
        return pl.BlockSpec((n, tr, cols), lambda l, i: (0, jnp.where(l == k, i, 0), 0))

    return pl.pallas_call(
        body, name=name, grid=(n_layers, rows // tr),
        in_specs=[in_spec(k) for k in range(n_layers)],
        out_specs=pl.BlockSpec((None, tr, cols), lambda l, i: (l, i, 0)),
        out_shape=jax.ShapeDtypeStruct((n_layers, rows, cols), F32),
        compiler_params=_params(("arbitrary", "arbitrary")),
    )(*parts)


def _silu_rows(name, x):
    def body(x_ref, o_ref):
        v = x_ref[...]
        o_ref[...] = v * jax.nn.sigmoid(v)

    return pl.pallas_call(body, name=name, out_shape=jax.ShapeDtypeStruct(x.shape, F32))(x)


def _silu_bwd_rows(name, x, dy):
    def body(x_ref, dy_ref, o_ref):
        v = x_ref[...]
        sg = jax.nn.sigmoid(v)
        o_ref[...] = dy_ref[...] * (sg * (1.0 + v * (1.0 - sg)))

    return pl.pallas_call(body, name=name, out_shape=jax.ShapeDtypeStruct(x.shape, F32))(x, dy)


def _adamw(name, w, g_parts, m, v):
    shape = w.shape
    w2, m2, v2 = _rows_view(w), _rows_view(m), _rows_view(v)
    gs = [_rows_view(g) for g in g_parts]
    rows, cols = w2.shape
    tr = _row_tile(rows, cols)
    ng = len(gs)

    def body(*refs):
        w_ref, m_ref, v_ref = refs[0], refs[1], refs[2]
        g_refs = refs[3:3 + ng]
        g_out, d_out, m_out, v_out = refs[3 + ng:]
        g = g_refs[0][...]
        for r in g_refs[1:]:
            g = g + r[...]
        m1 = ADAM_B1 * m_ref[...] + (1.0 - ADAM_B1) * g
        v1 = ADAM_B2 * v_ref[...] + (1.0 - ADAM_B2) * jnp.square(g)
        m_hat = m1 / (1.0 - ADAM_B1 ** ADAM_STEP)
        v_hat = v1 / (1.0 - ADAM_B2 ** ADAM_STEP)
        g_out[...] = g
        d_out[...] = -ADAM_LR * (m_hat / (jnp.sqrt(v_hat) + ADAM_EPS) + ADAM_WD * w_ref[...])
        m_out[...] = m1
        v_out[...] = v1

    spec = pl.BlockSpec((tr, cols), lambda i: (i, 0))
    res = pl.pallas_call(
        body, name=name, grid=(rows // tr,), in_specs=[spec] * (3 + ng), out_specs=[spec] * 4,
        out_shape=[jax.ShapeDtypeStruct((rows, cols), F32)] * 4,
        compiler_params=_params(("parallel",)),
    )(w2, m2, v2, *gs)
    return tuple(r.reshape(shape) for r in res)


MOD_ROWS = 48


def kernel(x, c, ctx, c_ctx, mod_w, mod_b, norm1_g, norm2_g, ab_w_in, ab_w_out, ret_decay, att_q_norm_g, att_k_norm_g, cm_w_in, cm_v_norm_g, cm_w_s, cm_b_s, cm_w_out, ff_w1, ff_w2, loss_target, m_c_ctx, m_mod_w, m_mod_b, m_norm1_g, m_norm2_g, m_ab_w_in, m_ab_w_out, m_ret_decay, m_att_q_norm_g, m_att_k_norm_g, m_cm_w_in, m_cm_v_norm_g, m_cm_w_s, m_cm_b_s, m_cm_w_out, m_ff_w1, m_ff_w2, v_c_ctx, v_mod_w, v_mod_b, v_norm1_g, v_norm2_g, v_ab_w_in, v_ab_w_out, v_ret_decay, v_att_q_norm_g, v_att_k_norm_g, v_cm_w_in, v_cm_v_norm_g, v_cm_w_s, v_cm_b_s, v_cm_w_out, v_ff_w1, v_ff_w2):
    B, SL, D = x.shape
    cfg = Cfg(B=B, SC=ctx.shape[1], SL=SL, D=D, FF=ff_w1.shape[2] * N_CHIP)
    L = N_LAYERS
    n_ex = B * N_DEV
    mcols = mod_w.shape[2]
    weights = dict(c_ctx=c_ctx, mod_w=mod_w, mod_b=mod_b, norm1_g=norm1_g, norm2_g=norm2_g, ab_w_in=ab_w_in,
                   ab_w_out=ab_w_out, ret_decay=ret_decay, att_q_norm_g=att_q_norm_g, att_k_norm_g=att_k_norm_g,
                   cm_w_in=cm_w_in, cm_v_norm_g=cm_v_norm_g, cm_w_s=cm_w_s, cm_b_s=cm_b_s, cm_w_out=cm_w_out,
                   ff_w1=ff_w1, ff_w2=ff_w2)
    m_in = dict(c_ctx=m_c_ctx, mod_w=m_mod_w, mod_b=m_mod_b, norm1_g=m_norm1_g, norm2_g=m_norm2_g, ab_w_in=m_ab_w_in,
                ab_w_out=m_ab_w_out, ret_decay=m_ret_decay, att_q_norm_g=m_att_q_norm_g, att_k_norm_g=m_att_k_norm_g,
                cm_w_in=m_cm_w_in, cm_v_norm_g=m_cm_v_norm_g, cm_w_s=m_cm_w_s, cm_b_s=m_cm_b_s, cm_w_out=m_cm_w_out,
                ff_w1=m_ff_w1, ff_w2=m_ff_w2)
    v_in = dict(c_ctx=v_c_ctx, mod_w=v_mod_w, mod_b=v_mod_b, norm1_g=v_norm1_g, norm2_g=v_norm2_g, ab_w_in=v_ab_w_in,
                ab_w_out=v_ab_w_out, ret_decay=v_ret_decay, att_q_norm_g=v_att_q_norm_g, att_k_norm_g=v_att_k_norm_g,
                cm_w_in=v_cm_w_in, cm_v_norm_g=v_cm_v_norm_g, cm_w_s=v_cm_w_s, cm_b_s=v_cm_b_s, cm_w_out=v_cm_w_out,
                ff_w1=v_ff_w1, ff_w2=v_ff_w2)
    xi, yi, ci = lax.axis_index("x"), lax.axis_index("y"), lax.axis_index("c")
    chip = 2 * xi + yi
    dev = 2 * chip + ci

    shards = {n: [weights[n][i].astype(BF16) for i in range(weights[n].shape[0])] for n in BIG}
    vgw = cm_v_norm_g.shape[1]
    blk = jnp.zeros((8, D), F32).at[:B].set(c).at[B:B + 2, :vgw].set(cm_v_norm_g)
    g0 = _allgather8("gather_c", blk).reshape(N_DEV, 8, D)
    c_all = g0[:, :B].reshape(n_ex, D)
    vg_full = jnp.concatenate([g0[2 * s, B:B + 2, :vgw] for s in range(N_CHIP)], axis=-1)

    pre = jnp.zeros((MOD_ROWS, D), F32).at[:n_ex].set(c_all).at[n_ex].set(c_ctx)
    act = _silu_rows("silu_c", pre)
    mpart = jnp.stack([_mm(f"mod_fwd_{l}", act, mod_w, mode="nn", layer=l, outs=[F32], tn=mcols) for l in range(L)])
    g1 = _allgather8("gather_mod", mpart.reshape(L * MOD_ROWS, mcols)).reshape(N_DEV, L, MOD_ROWS, mcols)
    mod_all = jnp.concatenate([g1[2 * s] for s in range(N_CHIP)], axis=-1) + mod_b[:, None, :]
    mod_lat = lax.dynamic_slice_in_dim(mod_all, dev * B, B, axis=1)
    mod_ctx = jnp.broadcast_to(mod_all[:, n_ex][:, None], mod_lat.shape)
    mods = jnp.stack([mod_ctx, mod_lat], axis=2).reshape(L, B, 2, 6, D)

    w = dict(norm1_g=norm1_g, norm2_g=norm2_g, ret_decay=ret_decay, att_q_norm_g=att_q_norm_g,
             att_k_norm_g=att_k_norm_g, cm_v_norm_g=vg_full, cm_w_s=cm_w_s, cm_b_s=cm_b_s)
    xcat = jnp.concatenate([ctx, x], axis=1).reshape(cfg.T, D)
    loss_local, dx_lat, recv, small, dmods = _local_step(cfg, xcat, loss_target.reshape(B * SL, D), mods, shards, w)
    loss = lax.psum(loss_local, ("x", "y", "c"))
    grad_x = dx_lat.reshape(B, SL, D)

    part = [_sum_parts_layers(f"sum_{n}", [recv[(n, i)] for i in range(weights[n].shape[0])]) for n in BIG]

    dmod = jnp.stack(dmods).reshape(L, B, 2, 6 * D)
    dmod_lat = dmod[:, :, 1]
    dmod_ctx = jnp.sum(dmod[:, :, 0], axis=1)
    d_ret = jnp.stack(small["ret_lg"]) * jax.nn.sigmoid(-ret_decay)
    summed = [dmod_ctx.reshape(-1), jnp.stack(small["norm1_g"]).reshape(-1), jnp.stack(small["norm2_g"]).reshape(-1),
              jnp.stack(small["cm_v_norm_g"]).reshape(-1), jnp.stack(small["cm_w_s"]).reshape(-1),
              jnp.stack(small["cm_b_s"]).reshape(-1), jnp.stack(small["att_q_norm_g"]).reshape(-1),
              jnp.stack(small["att_k_norm_g"]).reshape(-1), d_ret.reshape(-1)]
    sizes = [int(a.shape[0]) for a in summed]
    flat = jnp.concatenate(summed + [dmod_lat.reshape(-1)])
    n_sum = sum(sizes)
    n_sum_rows = -(-n_sum // D)
    lat_rows = (L * B * 6 * D) // D
    pack_rows = -(-(n_sum_rows + lat_rows) // 8) * 8
    packed = jnp.zeros((pack_rows * D,), F32).at[:n_sum].set(flat[:n_sum])
    packed = packed.at[n_sum_rows * D:(n_sum_rows + lat_rows) * D].set(flat[n_sum:]).reshape(pack_rows, D)
    g2, other = _allgather8("gather_small", packed, carry=_Carry("swap", tuple(part), (0,) * len(part)))
    g2 = g2.reshape(N_DEV, pack_rows, D)
    tot = _sum_leading("sum_small", g2[:, :n_sum_rows]).reshape(-1)
    pieces, off = [], 0
    for sz in sizes:
        pieces.append(tot[off:off + sz])
        off += sz
    dmod_ctx_t, g_n1, g_n2, g_vg, g_ws, g_bs, g_qg, g_kg, g_rd = pieces
    dmod_ctx_t = dmod_ctx_t.reshape(L, 6 * D)
    dmod_lat_all = g2[:, n_sum_rows:n_sum_rows + lat_rows].reshape(N_DEV, L, B, 6 * D)
    dmod_rows = jnp.zeros((L, MOD_ROWS, 6 * D), F32)
    dmod_rows = dmod_rows.at[:, :n_ex].set(jnp.transpose(dmod_lat_all, (1, 0, 2, 3)).reshape(L, n_ex, 6 * D))
    dmod_rows = dmod_rows.at[:, n_ex].set(dmod_ctx_t)
    g_mod_b = _sum_leading("sum_mod_b", jnp.transpose(dmod_rows, (1, 0, 2)))
    dmod_mine = lax.dynamic_slice_in_dim(dmod_rows, chip * mcols, mcols, axis=2)
    g_mod_w = jnp.stack([_mm(f"mod_dw_{l}", act, dmod_mine[l], mode="tn", outs=[F32], tn=mcols) for l in range(L)])
    ctx8 = jnp.zeros((L, 8, mcols), F32).at[:, 0].set(dmod_mine[:, n_ex])
    dcc = [_mm(f"mod_dctx_{l}", ctx8[l], mod_w, mode="nt", layer=l, outs=[F32], tk=mcols) for l in range(L)]
    dcc = _sum_leading("sum_dctx_layers", jnp.stack(dcc))
    g3 = _allgather8("gather_dctx", dcc).reshape(N_DEV, 8, D)
    dcc_t = _sum_leading("sum_dctx_chips", g3[0::2])[0:1]
    g_c_ctx = _silu_bwd_rows("silu_bwd_cctx", c_ctx[None], dcc_t)[0]

    vg_mine = lax.dynamic_slice_in_dim(g_vg.reshape(2, -1), chip * vgw, vgw, axis=1)
    small_g = dict(c_ctx=g_c_ctx, mod_w=g_mod_w, mod_b=g_mod_b, norm1_g=g_n1.reshape(norm1_g.shape),
                   norm2_g=g_n2.reshape(norm2_g.shape), ret_decay=g_rd.reshape(ret_decay.shape),
                   att_q_norm_g=g_qg.reshape(att_q_norm_g.shape), att_k_norm_g=g_kg.reshape(att_k_norm_g.shape),
                   cm_v_norm_g=vg_mine, cm_w_s=g_ws.reshape(cm_w_s.shape), cm_b_s=g_bs.reshape(cm_b_s.shape))
    out = {}
    for n, g in small_g.items():
        out[n] = _adamw(f"adamw_{n}", weights[n], [g], m_in[n], v_in[n])
    for n, p_mine, p_other in zip(BIG, part, other):
        out[n] = _adamw(f"adamw_{n}", weights[n], [p_mine, p_other], m_in[n], v_in[n])

    order = list(weights)
    return (loss, grad_x, *[out[n][0] for n in order], *[out[n][1] for n in order],
            *[out[n][2] for n in order], *[out[n][3] for n in order])
```

```python
import functools
import math
from typing import NamedTuple

import jax
import jax.numpy as jnp
from jax import lax
from jax.experimental import pallas as pl
from jax.experimental.pallas import tpu as pltpu

F32 = jnp.float32
BF16 = jnp.bfloat16
EPS = 1e-6
ROPE_BASE = 10000.0
LANES = 128
CHUNK = 128
N_LAYERS = 4
VMEM_LIMIT = 56 * 1024 * 1024

ADAM_LR = 0.001
ADAM_B1 = 0.9
ADAM_B2 = 0.999
ADAM_EPS = 1e-08
ADAM_WD = 0.01
ADAM_STEP = 10


class Cfg(NamedTuple):
    B: int = 4
    SC: int = 256
    SL: int = 2048
    D: int = 1024
    FF: int = 4096
    GRID_W: int = 64
    H: int = 4
    KV: int = 2
    CMW: int = 1024
    CMG: int = 8

    @property
    def S(self):
        return self.SC + self.SL

    @property
    def T(self):
        return self.B * self.S

    @property
    def TM(self):
        return self.SC

    @property
    def TPE(self):
        return self.S // self.SC

    @property
    def ABW(self):
        return (5 * self.H + 2 * self.KV) * CHUNK


def _tile(dim, pref):
    t = min(dim, pref)
    while dim % t:
        t -= LANES
    return t


def _dot(a, b):
    return lax.dot_general(a, b, (((1,), (0,)), ((), ())), preferred_element_type=F32)


def _dot_nt(a, b):
    return lax.dot_general(a, b, (((1,), (1,)), ((), ())), preferred_element_type=F32)


def _dot_tn(a, b):
    return lax.dot_general(a, b, (((0,), (0,)), ((), ())), preferred_element_type=F32)


def _params(sem, vmem=VMEM_LIMIT):
    return pltpu.CompilerParams(dimension_semantics=sem, vmem_limit_bytes=vmem)


def _mod_index(cfg):
    tpe = cfg.TPE
    return lambda i: (i // tpe, jnp.minimum(i % tpe, 1), 0, 0)


def _mm(name, a, b, *, mode, outs, tm=1024, tn=1024, tk=1024, layer=None, epi=None, extras=(), carry=None,
        resident=False):
    bshape = b.shape[1:] if layer is not None else b.shape
    if mode == "nn":
        (M, K), N = a.shape, bshape[1]
    elif mode == "nt":
        (M, K), N = a.shape, bshape[0]
    else:
        (K, M), N = a.shape, bshape[1]
    tm, tn, tk = _tile(M, tm), _tile(N, tn), _tile(K, tk)
    kchunk = tk
    if resident:
        tk = K
    nk = K // tk
    a_spec = (pl.BlockSpec((tk, tm), lambda i, j, k: (k, i)) if mode == "tn"
              else pl.BlockSpec((tm, tk), lambda i, j, k: (i, k)))
    if mode == "nt":
        bblk, bidx = (tn, tk), (lambda i, j, k: (j, k))
    else:
        bblk, bidx = (tk, tn), (lambda i, j, k: (k, j))
    if layer is not None:
        b_spec = pl.BlockSpec((None,) + bblk, lambda i, j, k: (layer,) + bidx(i, j, k))
    elif resident:
        b_spec = pl.BlockSpec(bblk, bidx, pipeline_mode=pl.Buffered(1))
    else:
        b_spec = pl.BlockSpec(bblk, bidx)
    ne, no = len(extras), len(outs)
    nc = len(carry.srcs) if carry is not None else 0
    dot = {"nn": _dot, "nt": _dot_nt, "tn": _dot_tn}[mode]
    grid = (M // tm, N // tn, nk)

    def body(*refs):
        a_ref, b_ref = refs[0], refs[1]
        ex, out_refs = refs[2:2 + ne], refs[2 + ne + nc:2 + ne + nc + no]
        row_tile = pl.program_id(0)

        if nc:
            step = (pl.program_id(0) * grid[1] + pl.program_id(1)) * grid[2] + pl.program_id(2)
            c_src = refs[2 + ne:2 + ne + nc]
            c_dst = refs[2 + ne + nc + no:2 + ne + 2 * nc + no]
            sems = refs[2 + ne + 2 * nc + no:2 + ne + 2 * nc + no + 3]

            @pl.when(step == 0)
            def _():
                for cp in _carry_copies(carry, c_src, c_dst, *sems):
                    cp.start()

        def finish(acc):
            res = epi(acc, row_tile, *ex) if epi is not None else (acc,)
            for r, o in zip(res, out_refs):
                o[...] = r.astype(o.dtype)

        if resident and K > kchunk:
            part = None
            for c in range(K // kchunk):
                ks = pl.ds(c * kchunk, kchunk)
                b_chunk = b_ref[:, ks] if mode == "nt" else b_ref[ks, :]
                term = dot(a_ref[:, ks].astype(BF16), b_chunk.astype(BF16))
                part = term if part is None else part + term
        else:
            part = dot(a_ref[...].astype(BF16), b_ref[...].astype(BF16))
        if nk == 1:
            finish(part)
        else:
            acc_ref = refs[-1]
            k = pl.program_id(2)

            @pl.when(k == 0)
            def _():
                acc_ref[...] = part

            @pl.when(k > 0)
            def _():
                acc_ref[...] += part

            @pl.when(k == nk - 1)
            def _():
                finish(acc_ref[...])

        if nc:
            @pl.when(step == grid[0] * grid[1] * grid[2] - 1)
            def _():
                for cp in _carry_copies(carry, c_src, c_dst, *sems):
                    cp.wait()

    scratch = [pltpu.SemaphoreType.DMA((3 * nc,)), pltpu.SemaphoreType.DMA((3 * nc,)),
               pltpu.SemaphoreType.DMA((nc,))] if nc else []
    if nk > 1:
        scratch.append(pltpu.VMEM((tm, tn), F32))
    res = pl.pallas_call(
        body, name=name, grid=grid,
        in_specs=[a_spec, b_spec] + [s for _, s in extras] + [ANY] * nc,
        out_specs=[pl.BlockSpec((tm, tn), lambda i, j, k: (i, j)) for _ in outs] + [ANY] * nc,
        out_shape=[jax.ShapeDtypeStruct((M, N), d) for d in outs] + (_carry_out_shapes(carry) if nc else []),
        scratch_shapes=scratch,
        compiler_params=_params(("arbitrary",) * 3 if nc else ("parallel", "parallel", "arbitrary")),
    )(a, b, *[x for x, _ in extras], *(carry.srcs if nc else ()))
    if nc:
        return (res[0] if no == 1 else res[:no]), res[no:]
    return res[0] if no == 1 else res


def _ff_bwd(name, first, second, weight, kind, carry=None):
    (T, D), FF = first.shape, second.shape[1]
    halves = 2
    tm = _tile(T, 1024)
    ffh = FF // halves if kind == "w2" else FF
    dh_cols = D if kind == "w2" else D // halves
    cw = _tile(ffh, 1024)
    n_steps = T // tm
    nc = len(carry.srcs) if carry is not None else 0

    def body(*refs):
        a_ref, b_ref, w_ref = refs[:3]
        c_src = refs[3:3 + nc]
        x_ref, dw_ref = refs[3 + nc:5 + nc]
        c_dst, sems = refs[5 + nc:5 + 2 * nc], refs[5 + 2 * nc:5 + 2 * nc + 3] if nc else ()
        acc_ref = refs[-1]
        i = pl.program_id(1)
        step = pl.program_id(0) * n_steps + i
        if nc:
            _carry_begin(carry, c_src, c_dst, sems, step)
        a = a_ref[...]
        dh = None
        for c in range(ffh // cw):
            cols = pl.ds(c * cw, cw)
            if kind == "w2":
                r = b_ref[:, cols]
                x_ref[:, cols] = (_dot_nt(a, w_ref[cols, :]) * (2.0 * jnp.sqrt(r.astype(F32)))).astype(BF16)
                part, dst = _dot_tn(r, a), acc_ref.at[cols, :]
            else:
                da = b_ref[:, cols]
                term = _dot_nt(da, w_ref[:, cols])
                dh = term if dh is None else dh + term
                part, dst = _dot_tn(a, da), acc_ref.at[:, cols]

            @pl.when(i == 0)
            def _():
                dst[...] = part

            @pl.when(i > 0)
            def _():
                dst[...] += part

        if kind == "w1":
            x_ref[...] = dh.astype(BF16)

        @pl.when(i == n_steps - 1)
        def _():
            dw_ref[...] = acc_ref[...].astype(BF16)

        if nc:
            _carry_end(carry, c_src, c_dst, sems, step, halves * n_steps)

    if kind == "w2":
        wshape = (ffh, D)
        a_spec = pl.BlockSpec((tm, D), lambda j, i: (i, 0))
        b_spec = pl.BlockSpec((tm, ffh), lambda j, i: (i, j))
        x_spec, x_cols = pl.BlockSpec((tm, ffh), lambda j, i: (i, j)), FF
    else:
        wshape = (dh_cols, FF)
        a_spec = pl.BlockSpec((tm, dh_cols), lambda j, i: (i, j))
        b_spec = pl.BlockSpec((tm, FF), lambda j, i: (i, 0))
        x_spec, x_cols = pl.BlockSpec((tm, dh_cols), lambda j, i: (i, j)), D
    wspec = pl.BlockSpec(wshape, lambda j, i: (j, 0), pipeline_mode=pl.Buffered(1))
    res = pl.pallas_call(
        body, name=name, grid=(halves, n_steps),
        in_specs=[a_spec, b_spec, wspec] + [ANY] * nc,
        out_specs=[x_spec, wspec] + [ANY] * nc,
        out_shape=[jax.ShapeDtypeStruct((T, x_cols), BF16), jax.ShapeDtypeStruct(weight.shape, BF16)]
        + (_carry_out_shapes(carry) if nc else []),
        scratch_shapes=_carry_scratch(nc) + [pltpu.VMEM(wshape, F32)],
        compiler_params=_params(("arbitrary", "arbitrary")),
    )(first, second, weight, *(carry.srcs if nc else ()))
    return res[:2], res[2:]


def _norm_mod_fwd(cfg, name, x, gain, mod, ish, isc):
    T, D, TM = cfg.T, cfg.D, cfg.TM

    def body(x_ref, g_ref, mod_ref, h_ref):
        x = x_ref[...]
        rstd = lax.rsqrt(jnp.mean(x * x, axis=-1, keepdims=True) + EPS)
        n = x * rstd * g_ref[...]
        h = n * (1.0 + mod_ref[pl.ds(isc, 1), :]) + mod_ref[pl.ds(ish, 1), :]
        h_ref[...] = h.astype(BF16)

    return pl.pallas_call(
        body, name=name, grid=(T // TM,),
        in_specs=[pl.BlockSpec((TM, D), lambda i: (i, 0)), pl.BlockSpec((1, D), lambda i: (0, 0)),
                  pl.BlockSpec((None, None, 6, D), _mod_index(cfg))],
        out_specs=pl.BlockSpec((TM, D), lambda i: (i, 0)),
        out_shape=jax.ShapeDtypeStruct((T, D), BF16),
        compiler_params=_params(("parallel",)),
    )(x, gain, mod)


def _norm_mod_bwd(cfg, name, x, gain, mod, ish, isc, dh, dres, gate=None, lat_only=False):
    T, D, TM, TPE = cfg.T, cfg.D, cfg.TM, cfg.TPE
    ng = 2 if gate is not None else 0
    dx_spec = (pl.BlockSpec((TM, D), lambda i: ((i // TPE) * (TPE - 1) + jnp.maximum(i % TPE - 1, 0), 0)) if lat_only
               else pl.BlockSpec((TM, D), lambda i: (i, 0)))
    dx_rows = cfg.B * cfg.SL if lat_only else T

    def body(*refs):
        x_ref, g_ref, mod_ref, dh_ref, dres_ref = refs[:5]
        dx_ref, dmod_ref, dgain_ref = refs[5 + ng:8 + ng]
        i = pl.program_id(0)
        t = i % TPE
        x = x_ref[...]
        g = g_ref[...]
        dh = dh_ref[...].astype(F32)
        rstd = lax.rsqrt(jnp.mean(x * x, axis=-1, keepdims=True) + EPS)
        xhat = x * rstd
        dn = dh * (1.0 + mod_ref[pl.ds(isc, 1), :])
        dsh = jnp.sum(dh, axis=0, keepdims=True)
        dsc = jnp.sum(dh * (xhat * g), axis=0, keepdims=True)
        dgain = jnp.sum(dn * xhat, axis=0, keepdims=True)
        dxh = dn * g
        dx = rstd * (dxh - xhat * jnp.mean(dxh * xhat, axis=-1, keepdims=True)) + dres_ref[...]
        dx_ref[...] = dx
        sums = [(dmod_ref.at[pl.ds(0, 1), :], dsh), (dmod_ref.at[pl.ds(1, 1), :], dsc)]
        if ng:
            y_ref, gmod_ref = refs[5:7]
            dy_ref, dgate_ref = refs[8 + ng:]
            dy_ref[...] = (dx * gmod_ref[pl.ds(gate[2], 1), :]).astype(BF16)
            sums.append((dgate_ref, jnp.sum(dx * y_ref[...].astype(F32), axis=0, keepdims=True)))

        @pl.when(t <= 1)
        def _():
            for ref, val in sums:
                ref[...] = val

        @pl.when(t > 1)
        def _():
            for ref, val in sums:
                ref[...] += val

        @pl.when(i == 0)
        def _():
            dgain_ref[...] = dgain

        @pl.when(i > 0)
        def _():
            dgain_ref[...] += dgain

    tok = pl.BlockSpec((TM, D), lambda i: (i, 0))
    mod_spec = pl.BlockSpec((None, None, 6, D), _mod_index(cfg))
    res = pl.pallas_call(
        body, name=name, grid=(T // TM,),
        in_specs=[tok, pl.BlockSpec((1, D), lambda i: (0, 0)), mod_spec, tok, tok] + ([tok, mod_spec] if ng else []),
        out_specs=[dx_spec, pl.BlockSpec((None, None, 2, D), _mod_index(cfg)), pl.BlockSpec((1, D), lambda i: (0, 0))]
        + ([tok, pl.BlockSpec((None, None, 1, D), _mod_index(cfg))] if ng else []),
        out_shape=[jax.ShapeDtypeStruct((dx_rows, D), F32), jax.ShapeDtypeStruct((cfg.B, 2, 2, D), F32),
                   jax.ShapeDtypeStruct((1, D), F32)]
        + ([jax.ShapeDtypeStruct((T, D), BF16), jax.ShapeDtypeStruct((cfg.B, 2, 1, D), F32)] if ng else []),
        compiler_params=_params(("arbitrary",)),
    )(x, gain, mod, dh, dres, *(gate[:2] if ng else ()))
    return res


def _gate_bwd(cfg, name, dx, y, mod, igate):
    T, D, TM, TPE = cfg.T, cfg.D, cfg.TM, cfg.TPE

    def body(dx_ref, y_ref, mod_ref, dy_ref, dg_ref):
        t = pl.program_id(0) % TPE
        dx = dx_ref[...]
        dy_ref[...] = (dx * mod_ref[pl.ds(igate, 1), :]).astype(BF16)
        dg = jnp.sum(dx * y_ref[...].astype(F32), axis=0, keepdims=True)

        @pl.when(t <= 1)
        def _():
            dg_ref[...] = dg

        @pl.when(t > 1)
        def _():
            dg_ref[...] += dg

    tok = pl.BlockSpec((TM, D), lambda i: (i, 0))
    return pl.pallas_call(
        body, name=name, grid=(T // TM,),
        in_specs=[tok, tok, pl.BlockSpec((None, None, 6, D), _mod_index(cfg))],
        out_specs=[tok, pl.BlockSpec((None, None, 1, D), _mod_index(cfg))],
        out_shape=[jax.ShapeDtypeStruct((T, D), BF16), jax.ShapeDtypeStruct((cfg.B, 2, 1, D), F32)],
        compiler_params=_params(("arbitrary",)),
    )(dx, y, mod)


def _loss_grad(cfg, x, tgt):
    T, D, TM, TPE = cfg.T, cfg.D, cfg.TM, cfg.TPE

    def body(x_ref, t_ref, dx_ref, loss_ref):
        i = pl.program_id(0)
        t = i % TPE

        @pl.when(i == 0)
        def _():
            loss_ref[...] = jnp.zeros_like(loss_ref)

        @pl.when(t == 0)
        def _():
            dx_ref[...] = jnp.zeros_like(dx_ref)

        @pl.when(t > 0)
        def _():
            err = x_ref[...] - t_ref[...]
            dx_ref[...] = err * (1.0 / D)
            loss_ref[...] += 0.5 * jnp.sum(jnp.mean(err * err, axis=-1, keepdims=True), axis=0, keepdims=True)

    tok = pl.BlockSpec((TM, D), lambda i: (i, 0))
    tgt_spec = pl.BlockSpec((TM, D), lambda i: ((i // TPE) * (TPE - 1) + jnp.maximum(i % TPE - 1, 0), 0))
    dx, loss = pl.pallas_call(
        body, name="loss_grad", grid=(T // TM,),
        in_specs=[tok, tgt_spec], out_specs=[tok, pl.BlockSpec((8, LANES), lambda i: (0, 0))],
        out_shape=[jax.ShapeDtypeStruct((T, D), F32), jax.ShapeDtypeStruct((8, LANES), F32)],
        compiler_params=_params(("arbitrary",)),
    )(x, tgt)
    return loss[0, 0], dx


def _rope_tables(cfg):
    rows = cfg.SL // cfg.GRID_W
    row = jnp.repeat(jnp.arange(rows, dtype=F32), cfg.GRID_W)
    col = jnp.tile(jnp.arange(cfg.GRID_W, dtype=F32), rows)
    n_freq = CHUNK // 4
    inv = ROPE_BASE ** (-jnp.arange(n_freq, dtype=F32) / n_freq)
    ang = jnp.concatenate([row[:, None] * inv[None, :], col[:, None] * inv[None, :]], axis=-1)
    cos, sin = jnp.cos(ang), jnp.sin(ang)
    cosf = jnp.concatenate([jnp.ones((cfg.SC, CHUNK), F32), jnp.concatenate([cos, cos], axis=-1)], axis=0)
    sinf = jnp.concatenate([jnp.zeros((cfg.SC, CHUNK), F32), jnp.concatenate([-sin, sin], axis=-1)], axis=0)
    return cosf, sinf


def _rope(x, cosf, sinf):
    return x * cosf + pltpu.roll(x, CHUNK // 2, 1) * sinf


def _irope(dy, cosf, sinf):
    return dy * cosf - pltpu.roll(dy, CHUNK // 2, 1) * sinf


def _prep_fwd(cfg, name, p, cosf, sinf, qg, kg):
    T, TM, TPE, H, KV = cfg.T, cfg.TM, cfg.TPE, cfg.H, cfg.KV
    HW = H * CHUNK
    kscale = CHUNK ** -0.5

    def body(p_ref, c_ref, s_ref, qg_ref, kg_ref, rq_ref, rk_ref, aq_ref, ak_ref):
        cosf, sinf = c_ref[...], s_ref[...]

        def normed(x, g):
            return x * lax.rsqrt(jnp.mean(x * x, axis=-1, keepdims=True) + EPS) * g

        def seg(col):
            return p_ref[:, pl.ds(col, CHUNK)].astype(F32)

        for h in range(H):
            sl = pl.ds(h * CHUNK, CHUNK)
            rq_ref[:, sl] = _rope(seg(h * CHUNK), cosf, sinf)
            rk_ref[:, sl] = _rope(seg(HW + h * CHUNK), cosf, sinf) * kscale
            aq_ref[:, sl] = (_rope(normed(seg(4 * HW + h * CHUNK), qg_ref[...]), cosf, sinf) * ATT_SCALE).astype(BF16)
        for h in range(KV):
            ak_ref[:, pl.ds(h * CHUNK, CHUNK)] = _rope(
                normed(seg(5 * HW + h * CHUNK), kg_ref[...]), cosf, sinf).astype(BF16)

    tab = pl.BlockSpec((TM, CHUNK), lambda i: (i % TPE, 0))
    vec = pl.BlockSpec((1, CHUNK), lambda i: (0, 0))
    return pl.pallas_call(
        body, name=name, grid=(T // TM,),
        in_specs=[pl.BlockSpec((TM, cfg.ABW), lambda i: (i, 0)), tab, tab, vec, vec],
        out_specs=[pl.BlockSpec((TM, HW), lambda i: (i, 0))] * 3 + [pl.BlockSpec((TM, KV * CHUNK), lambda i: (i, 0))],
        out_shape=[jax.ShapeDtypeStruct((T, HW), F32), jax.ShapeDtypeStruct((T, HW), F32),
                   jax.ShapeDtypeStruct((T, HW), BF16), jax.ShapeDtypeStruct((T, KV * CHUNK), BF16)],
        compiler_params=_params(("parallel",)),
    )(p, cosf, sinf, qg, kg)


def _prep_bwd(cfg, name, p, cosf, sinf, qg, kg, d_rq, d_rk, d_rv, d_gate, d_aq, d_ak, d_av):
    T, TM, TPE, H, KV = cfg.T, cfg.TM, cfg.TPE, cfg.H, cfg.KV
    HW = H * CHUNK
    kscale = CHUNK ** -0.5

    def body(p_ref, c_ref, s_ref, qg_ref, kg_ref, drq_ref, drk_ref, drv_ref, dgt_ref, daq_ref, dak_ref, dav_ref,
             dp_ref, dqg_ref, dkg_ref):
        i = pl.program_id(0)
        cosf, sinf = c_ref[...], s_ref[...]

        def norm_bwd(x, g, dn):
            rstd = lax.rsqrt(jnp.mean(x * x, axis=-1, keepdims=True) + EPS)
            xhat = x * rstd
            dg = jnp.sum(dn * xhat, axis=0, keepdims=True)
            dxh = dn * g
            return rstd * (dxh - xhat * jnp.mean(dxh * xhat, axis=-1, keepdims=True)), dg

        dqg = jnp.zeros((1, CHUNK), F32)
        dkg = jnp.zeros((1, CHUNK), F32)
        for h in range(H):
            sl = pl.ds(h * CHUNK, CHUNK)
            dp_ref[:, pl.ds(h * CHUNK, CHUNK)] = _irope(drq_ref[:, sl].astype(F32), cosf, sinf).astype(BF16)
            dp_ref[:, pl.ds(HW + h * CHUNK, CHUNK)] = (_irope(drk_ref[:, sl].astype(F32), cosf, sinf)
                                                       * kscale).astype(BF16)
            dp_ref[:, pl.ds(2 * HW + h * CHUNK, CHUNK)] = drv_ref[:, sl].astype(BF16)
            dp_ref[:, pl.ds(3 * HW + h * CHUNK, CHUNK)] = dgt_ref[:, sl].astype(BF16)
            dx, dg = norm_bwd(p_ref[:, pl.ds(4 * HW + h * CHUNK, CHUNK)].astype(F32), qg_ref[...],
                              _irope(daq_ref[:, sl].astype(F32), cosf, sinf))
            dp_ref[:, pl.ds(4 * HW + h * CHUNK, CHUNK)] = dx.astype(BF16)
            dqg = dqg + dg
        for h in range(KV):
            sl = pl.ds(h * CHUNK, CHUNK)
            dx, dg = norm_bwd(p_ref[:, pl.ds(5 * HW + h * CHUNK, CHUNK)].astype(F32), kg_ref[...],
                              _irope(dak_ref[:, sl], cosf, sinf))
            dp_ref[:, pl.ds(5 * HW + h * CHUNK, CHUNK)] = dx.astype(BF16)
            dp_ref[:, pl.ds(5 * HW + (KV + h) * CHUNK, CHUNK)] = dav_ref[:, sl].astype(BF16)
            dkg = dkg + dg

        @pl.when(i == 0)
        def _():
            dqg_ref[...] = dqg
            dkg_ref[...] = dkg

        @pl.when(i > 0)
        def _():
            dqg_ref[...] += dqg
            dkg_ref[...] += dkg

    tab = pl.BlockSpec((TM, CHUNK), lambda i: (i % TPE, 0))
    vec = pl.BlockSpec((1, CHUNK), lambda i: (0, 0))
    hw = pl.BlockSpec((TM, HW), lambda i: (i, 0))
    kvw = pl.BlockSpec((TM, KV * CHUNK), lambda i: (i, 0))
    return pl.pallas_call(
        body, name=name, grid=(T // TM,),
        in_specs=[pl.BlockSpec((TM, cfg.ABW), lambda i: (i, 0)), tab, tab, vec, vec, hw, hw, hw, hw, hw, kvw, kvw],
        out_specs=[pl.BlockSpec((TM, cfg.ABW), lambda i: (i, 0)), vec, vec],
        out_shape=[jax.ShapeDtypeStruct((T, cfg.ABW), BF16), jax.ShapeDtypeStruct((1, CHUNK), F32),
                   jax.ShapeDtypeStruct((1, CHUNK), F32)],
        compiler_params=_params(("arbitrary",)),
    )(p, cosf, sinf, qg, kg, d_rq, d_rk, d_rv, d_gate, d_aq, d_ak, d_av)


def _ret_consts(direction, lg):
    C = CHUNK
    ii = lax.broadcasted_iota(jnp.int32, (C, C), 0)
    jj = lax.broadcasted_iota(jnp.int32, (C, C), 1)
    col = lax.broadcasted_iota(jnp.int32, (C, 1), 0).astype(F32)
    if direction == 0:
        mask, er, ek, eq = ii >= jj, (ii - jj).astype(F32), (C - 1.0) - col, col + 1.0
    else:
        mask, er, ek, eq = jj >= ii, (jj - ii).astype(F32), col, C - col
    er = jnp.where(mask, er, 0.0)
    dm = jnp.where(mask, jnp.exp(er * lg), 0.0)
    return dm, er, jnp.exp(ek * lg), ek, jnp.exp(eq * lg), eq, jnp.exp(C * lg)


def _ret_order(cfg, direction):
    n_all, n_ctx = cfg.S // CHUNK, cfg.SC // CHUNK
    if direction == 0:
        return list(range(n_all))
    return list(range(n_ctx - 1, -1, -1)) + list(range(n_all - 1, n_ctx - 1, -1))


def _carry_begin(carry, c_src, c_dst, sems, step):
    @pl.when(step == 0)
    def _():
        for cp in _carry_copies(carry, c_src, c_dst, *sems):
            cp.start()


def _carry_end(carry, c_src, c_dst, sems, step, n_steps):
    @pl.when(step == n_steps - 1)
    def _():
        for cp in _carry_copies(carry, c_src, c_dst, *sems):
            cp.wait()


def _carry_scratch(nc):
    return [pltpu.SemaphoreType.DMA((3 * nc,)), pltpu.SemaphoreType.DMA((3 * nc,)),
            pltpu.SemaphoreType.DMA((nc,))] if nc else []


def _head_norm_gate(o, g):
    mu = jnp.mean(o, axis=-1, keepdims=True)
    var = jnp.mean(jnp.square(o - mu), axis=-1, keepdims=True)
    rstd = lax.rsqrt(var + EPS)
    y = (o - mu) * rstd
    sg = jax.nn.sigmoid(g)
    return y, rstd, sg


RET_UNROLL = 3


def _retention_fwd(cfg, name, rq, rk, p, lgb, carry=None):
    B, H, S, T = cfg.B, cfg.H, cfg.S, cfg.T
    n_all = S // CHUNK
    nc = len(carry.srcs) if carry is not None else 0

    def body(*refs):
        q_ref, k_ref, v_ref, g_ref, lg_ref = refs[:5]
        c_src = refs[5:5 + nc]
        o_ref, ret_ref, st_ref = refs[5 + nc:8 + nc]
        c_dst = refs[8 + nc:8 + 2 * nc]
        sems = refs[8 + 2 * nc:8 + 2 * nc + 3] if nc else ()
        kv_ref = refs[-1]
        step = pl.program_id(0) * H + pl.program_id(1)
        if nc:
            _carry_begin(carry, c_src, c_dst, sems, step)

        def rows(n):
            return pl.ds(pl.multiple_of(n * CHUNK, CHUNK), CHUNK)

        (dm0, _, kd0, _, qd0, _, cd0), (dm1, _, kd1, _, qd1, _, cd1) = (
            _ret_consts(d, lg_ref[d, 0:1, 0:1]) for d in (0, 1))
        dm_both = dm0 + dm1

        def kv_step(n, c):
            k = k_ref[rows(n), :]
            v = v_ref[rows(n), :].astype(BF16)
            kv_ref[0, n] = _dot_tn((k * kd0).astype(BF16), v)
            kv_ref[1, n] = _dot_tn((k * kd1).astype(BF16), v)
            return c

        lax.fori_loop(0, n_all, kv_step, 0, unroll=RET_UNROLL)
        for direction, cd in ((0, cd0), (1, cd1)):
            st = jnp.zeros((CHUNK, CHUNK), F32)
            for t, n in enumerate(_ret_order(cfg, direction)):
                st_ref[direction, n] = st
                if t + 1 < n_all:
                    st = cd * st + kv_ref[direction, n]

        def out_step(n, c):
            q = q_ref[rows(n), :].astype(BF16)
            v = v_ref[rows(n), :].astype(BF16)
            s = _dot_nt(q, k_ref[rows(n), :].astype(BF16)) * dm_both
            states = jnp.concatenate([st_ref[0, n].astype(BF16), st_ref[1, n].astype(BF16)], axis=1)
            cross = _dot(q, states)
            o = _dot(s.astype(BF16), v) + cross[:, :CHUNK] * qd0 + cross[:, CHUNK:] * qd1
            o_ref[rows(n), :] = o
            g = g_ref[rows(n), :].astype(F32)
            y, _, sg = _head_norm_gate(o, g)
            ret_ref[rows(n), :] = (y * (g * sg)).astype(BF16)
            return c

        lax.fori_loop(0, n_all, out_step, 0, unroll=RET_UNROLL)
        if nc:
            _carry_end(carry, c_src, c_dst, sems, step, B * H)

    HW = H * CHUNK
    blk = lambda off: pl.BlockSpec((S, CHUNK), lambda b, h: (b, off + h))
    st_spec = pl.BlockSpec((None, None, 2, n_all, CHUNK, CHUNK), lambda b, h: (b, h, 0, 0, 0, 0))
    res = pl.pallas_call(
        body, name=name, grid=(B, H),
        in_specs=[blk(0), blk(0), blk(2 * H), blk(3 * H),
                  pl.BlockSpec((None, 2, 8, LANES), lambda b, h: (h, 0, 0, 0))] + [ANY] * nc,
        out_specs=[blk(0), blk(0), st_spec] + [ANY] * nc,
        out_shape=[jax.ShapeDtypeStruct((T, HW), F32), jax.ShapeDtypeStruct((T, 2 * HW), BF16),
                   jax.ShapeDtypeStruct((B, H, 2, n_all, CHUNK, CHUNK), F32)] + (_carry_out_shapes(carry) if nc else []),
        scratch_shapes=_carry_scratch(nc) + [pltpu.VMEM((2, n_all, CHUNK, CHUNK), F32)],
        compiler_params=_params(("arbitrary", "arbitrary") if nc else ("parallel", "parallel")),
    )(rq, rk, p, p, lgb, *(carry.srcs if nc else ()))
    return res[:3], res[3:]


def _retention_bwd(cfg, name, rq, rk, p, o_sum, states, dcat, lgb, carry=None):
    B, H, S, T = cfg.B, cfg.H, cfg.S, cfg.T
    n_all = S // CHUNK
    C = CHUNK
    nc = len(carry.srcs) if carry is not None else 0

    def body(*refs):
        q_ref, k_ref, v_ref, g_ref, o_ref, st_ref, dr_ref, lg_ref = refs[:8]
        c_src = refs[8:8 + nc]
        dq_ref, dk_ref, dv_ref, dg_ref, dlg_ref = refs[8 + nc:13 + nc]
        c_dst = refs[13 + nc:13 + 2 * nc]
        sems = refs[13 + 2 * nc:13 + 2 * nc + 3] if nc else ()
        do_ref, gq_ref, ds_ref, acc_ref = refs[-4:]
        step = pl.program_id(0) * H + pl.program_id(1)
        if nc:
            _carry_begin(carry, c_src, c_dst, sems, step)

        def rows(n):
            return pl.ds(pl.multiple_of(n * C, C), C)

        (dm0, er0, kd0, ek0, qd0, eq0, cd0), (dm1, er1, kd1, ek1, qd1, eq1, cd1) = (
            _ret_consts(d, lg_ref[d, 0:1, 0:1]) for d in (0, 1))
        dm_both = dm0 + dm1
        wdm0, wdm1 = dm0 * er0, dm1 * er1

        def side(a, b):
            return jnp.concatenate([a.astype(BF16), b.astype(BF16)], axis=1)

        def gq_step(n, c):
            g = g_ref[rows(n), :].astype(F32)
            dr = dr_ref[rows(n), :].astype(F32)
            y, rstd, sg = _head_norm_gate(o_ref[rows(n), :], g)
            dy = dr * (g * sg)
            dg_ref[rows(n), :] = (dr * y * (sg * (1.0 + g * (1.0 - sg)))).astype(BF16)
            do = rstd * (dy - jnp.mean(dy, axis=-1, keepdims=True) - y * jnp.mean(dy * y, axis=-1, keepdims=True))
            do_ref[rows(n), :] = do
            gq = _dot_tn(q_ref[rows(n), :].astype(BF16), side(do * qd0, do * qd1))
            gq_ref[0, n] = gq[:, :C]
            gq_ref[1, n] = gq[:, C:]
            return c

        lax.fori_loop(0, n_all, gq_step, 0, unroll=RET_UNROLL)
        for direction, cd in ((0, cd0), (1, cd1)):
            order = _ret_order(cfg, direction)
            ds = jnp.zeros((C, C), F32)
            for t in reversed(range(n_all)):
                ds_ref[direction, order[t]] = ds
                if t > 0:
                    ds = cd * ds + gq_ref[direction, order[t]]
        acc_ref[...] = jnp.zeros_like(acc_ref)

        def chunk_step(n, c):
            q = q_ref[rows(n), :].astype(BF16)
            kf = k_ref[rows(n), :]
            k = kf.astype(BF16)
            v = v_ref[rows(n), :].astype(BF16)
            do = do_ref[rows(n), :]
            dob = do.astype(BF16)
            sp0, sp1 = st_ref[0, n], st_ref[1, n]
            ds0, ds1 = ds_ref[0, n], ds_ref[1, n]
            states = side(sp0, sp1)
            dstates = jnp.concatenate([ds0.astype(BF16), ds1.astype(BF16)], axis=0)
            doq0, doq1 = do * qd0, do * qd1
            doq = side(doq0, doq1)
            s_raw = _dot_nt(q, k)
            dpm = _dot_nt(dob, v)
            dsr = (dpm * dm_both).astype(BF16)
            dks = _dot_nt(v, dstates)
            dks0, dks1 = dks[:, :C] * kd0, dks[:, C:] * kd1
            qs = _dot(q, states)
            dq_ref[rows(n), :] = (_dot(dsr, k) + _dot_nt(doq, states)).astype(BF16)
            dk_ref[rows(n), :] = (_dot_tn(dsr, q) + dks0 + dks1).astype(BF16)
            dv_ref[rows(n), :] = (_dot_tn((s_raw * dm_both).astype(BF16), dob)
                                  + _dot(side(kf * kd0, kf * kd1), dstates)).astype(BF16)
            inner = dpm * s_raw
            acc_ref[0] += (jnp.sum(inner * wdm0, axis=0, keepdims=True)
                           + jnp.sum(eq0 * doq0 * qs[:, :C], axis=0, keepdims=True)
                           + jnp.sum(ek0 * kf * dks0, axis=0, keepdims=True)
                           + (C * cd0) * jnp.sum(ds0 * sp0, axis=0, keepdims=True))
            acc_ref[1] += (jnp.sum(inner * wdm1, axis=0, keepdims=True)
                           + jnp.sum(eq1 * doq1 * qs[:, C:], axis=0, keepdims=True)
                           + jnp.sum(ek1 * kf * dks1, axis=0, keepdims=True)
                           + (C * cd1) * jnp.sum(ds1 * sp1, axis=0, keepdims=True))
            return c

        lax.fori_loop(0, n_all, chunk_step, 0, unroll=RET_UNROLL)
        for direction in (0, 1):
            dlg_ref[direction] = jnp.broadcast_to(jnp.sum(acc_ref[direction], axis=1, keepdims=True), (8, LANES))
        if nc:
            _carry_end(carry, c_src, c_dst, sems, step, B * H)

    HW = H * CHUNK
    blk = lambda off: pl.BlockSpec((S, CHUNK), lambda b, h: (b, off + h))
    st_spec = pl.BlockSpec((None, None, 2, n_all, C, C), lambda b, h: (b, h, 0, 0, 0, 0))
    res = pl.pallas_call(
        body, name=name, grid=(B, H),
        in_specs=[blk(0), blk(0), blk(2 * H), blk(3 * H), blk(0), st_spec, blk(0),
                  pl.BlockSpec((None, 2, 8, LANES), lambda b, h: (h, 0, 0, 0))] + [ANY] * nc,
        out_specs=[blk(0)] * 4 + [pl.BlockSpec((None, None, 2, 8, LANES), lambda b, h: (b, h, 0, 0, 0))] + [ANY] * nc,
        out_shape=[jax.ShapeDtypeStruct((T, HW), BF16)] * 4 + [jax.ShapeDtypeStruct((B, H, 2, 8, LANES), F32)]
        + (_carry_out_shapes(carry) if nc else []),
        scratch_shapes=_carry_scratch(nc) + [pltpu.VMEM((S, CHUNK), F32), pltpu.VMEM((2, n_all, C, C), F32),
                                             pltpu.VMEM((2, n_all, C, C), F32), pltpu.VMEM((2, 1, C), F32)],
        compiler_params=_params(("arbitrary", "arbitrary") if nc else ("parallel", "parallel")),
    )(rq, rk, p, p, o_sum, states, dcat, lgb, *(carry.srcs if nc else ()))
    return res[:5], res[5:]


ATT_SCALE = CHUNK ** -0.5


def _attn_scores(cfg, q, k, t):
    kcol = lax.broadcasted_iota(jnp.int32, (1, cfg.S), 1)
    bias = jnp.where(jnp.logical_or(t > 0, kcol < cfg.SC), 0.0, -1e30)
    return _dot_nt(q, k) + bias


def _attention_fwd(cfg, name, aq, ak, p, cat, carry=None):
    B, H, KV, S, T, TM, TPE = cfg.B, cfg.H, cfg.KV, cfg.S, cfg.T, cfg.TM, cfg.TPE
    G = H // KV
    v_off = (5 * H + KV)
    nc = len(carry.srcs) if carry is not None else 0

    def body(*refs):
        q_ref, k_ref, v_ref = refs[:3]
        o_ref, lse_ref = refs[4 + nc:6 + nc]
        c_src, c_dst, sems = refs[4:4 + nc], refs[6 + nc:6 + 2 * nc], refs[6 + 2 * nc:]
        step = (pl.program_id(0) * KV + pl.program_id(1)) * TPE + pl.program_id(2)
        if nc:
            _carry_begin(carry, c_src, c_dst, sems, step)
        k = k_ref[...]
        v = v_ref[...].astype(BF16)
        for g in range(G):
            cols = pl.ds(g * CHUNK, CHUNK)
            s = _attn_scores(cfg, q_ref[:, cols], k, pl.program_id(2))
            m = jnp.max(s, axis=-1, keepdims=True)
            e = jnp.exp(s - m)
            total = jnp.sum(e, axis=-1, keepdims=True)
            o_ref[:, cols] = (_dot(e.astype(BF16), v) * (1.0 / total)).astype(BF16)
            lse_ref[g] = m + jnp.log(total)
        if nc:
            _carry_end(carry, c_src, c_dst, sems, step, B * KV * TPE)

    res = pl.pallas_call(
        body, name=name, grid=(B, KV, TPE),
        in_specs=[pl.BlockSpec((TM, G * CHUNK), lambda b, kv, t: (b * TPE + t, kv)),
                  pl.BlockSpec((S, CHUNK), lambda b, kv, t: (b, kv)),
                  pl.BlockSpec((S, CHUNK), lambda b, kv, t: (b, v_off + kv)), ANY] + [ANY] * nc,
        out_specs=[pl.BlockSpec((TM, G * CHUNK), lambda b, kv, t: (b * TPE + t, KV + kv)),
                   pl.BlockSpec((G, TM, 1), lambda b, kv, t: (kv, b * TPE + t, 0))] + [ANY] * nc,
        out_shape=[jax.ShapeDtypeStruct(cat.shape, cat.dtype), jax.ShapeDtypeStruct((H, T, 1), F32)]
        + (_carry_out_shapes(carry) if nc else []),
        input_output_aliases={3: 0},
        scratch_shapes=_carry_scratch(nc),
        compiler_params=_params(("arbitrary",) * 3 if nc else ("parallel",) * 3),
    )(aq, ak, p, cat, *(carry.srcs if nc else ()))
    return res[:2], res[2:]


def _attention_bwd(cfg, name, aq, ak, p, cat, lse, dcat, carry=None):
    B, H, KV, S, T, TM, TPE = cfg.B, cfg.H, cfg.KV, cfg.S, cfg.T, cfg.TM, cfg.TPE
    G = H // KV
    v_off = (5 * H + KV)
    nc = len(carry.srcs) if carry is not None else 0

    def body(*refs):
        q_ref, k_ref, v_ref, o_ref, lse_ref, do_ref = refs[:6]
        dq_ref, dk_ref, dv_ref = refs[6 + nc:9 + nc]
        c_src, c_dst, sems = refs[6:6 + nc], refs[9 + nc:9 + 2 * nc], refs[9 + 2 * nc:]
        t = pl.program_id(2)
        step = (pl.program_id(0) * KV + pl.program_id(1)) * TPE + t
        if nc:
            _carry_begin(carry, c_src, c_dst, sems, step)
        k = k_ref[...]
        v = v_ref[...].astype(BF16)
        dk = dv = None
        for g in range(G):
            cols = pl.ds(g * CHUNK, CHUNK)
            q = q_ref[:, cols]
            do = do_ref[:, cols]
            pr = jnp.exp(_attn_scores(cfg, q, k, t) - lse_ref[g])
            delta = jnp.sum(do.astype(F32) * o_ref[:, cols].astype(F32), axis=-1, keepdims=True)
            ds = (pr * (_dot_nt(do, v) - delta)).astype(BF16)
            dq_ref[:, cols] = (_dot(ds, k) * ATT_SCALE).astype(BF16)
            dk_g, dv_g = _dot_tn(ds, q), _dot_tn(pr.astype(BF16), do)
            dk, dv = (dk_g, dv_g) if dk is None else (dk + dk_g, dv + dv_g)

        @pl.when(t == 0)
        def _():
            dk_ref[...] = dk
            dv_ref[...] = dv

        @pl.when(t > 0)
        def _():
            dk_ref[...] += dk
            dv_ref[...] += dv

        if nc:
            _carry_end(carry, c_src, c_dst, sems, step, B * KV * TPE)

    qspec = pl.BlockSpec((TM, G * CHUNK), lambda b, kv, t: (b * TPE + t, kv))
    kvspec = pl.BlockSpec((S, CHUNK), lambda b, kv, t: (b, kv))
    right = pl.BlockSpec((TM, G * CHUNK), lambda b, kv, t: (b * TPE + t, KV + kv))
    res = pl.pallas_call(
        body, name=name, grid=(B, KV, TPE),
        in_specs=[qspec, kvspec, pl.BlockSpec((S, CHUNK), lambda b, kv, t: (b, v_off + kv)), right,
                  pl.BlockSpec((G, TM, 1), lambda b, kv, t: (kv, b * TPE + t, 0)), right] + [ANY] * nc,
        out_specs=[qspec, kvspec, kvspec] + [ANY] * nc,
        out_shape=[jax.ShapeDtypeStruct((T, H * CHUNK), BF16), jax.ShapeDtypeStruct((T, KV * CHUNK), F32),
                   jax.ShapeDtypeStruct((T, KV * CHUNK), F32)] + (_carry_out_shapes(carry) if nc else []),
        scratch_shapes=_carry_scratch(nc),
        compiler_params=_params(("arbitrary",) * 3 if nc else ("parallel", "parallel", "arbitrary")),
    )(aq, ak, p, cat, lse, dcat, *(carry.srcs if nc else ()))
    return res[:3], res[3:]


_GELU_C = math.sqrt(2.0 / math.pi)


def _gelu(x):
    return 0.5 * x * (1.0 + jnp.tanh(_GELU_C * (x + 0.044715 * x * x * x)))


def _gelu_and_grad(x):
    x2 = x * x
    th = jnp.tanh(_GELU_C * (x + 0.044715 * x * x2))
    half = 0.5 * (1.0 + th)
    return x * half, half + 0.5 * x * (1.0 - th * th) * _GELU_C * (1.0 + 3.0 * 0.044715 * x2)


def _cm_fwd(cfg, name, a, vg, ws, bs):
    T, TM, W, NG = cfg.T, cfg.TM, cfg.CMW, cfg.CMG

    def body(a_ref, vg_ref, ws_ref, bs_ref, m_ref):
        v = _gelu(a_ref[:, pl.ds(W, W)].astype(F32))
        vn = (v * lax.rsqrt(jnp.mean(v * v, axis=-1, keepdims=True) + EPS) * vg_ref[...]).astype(BF16)
        for c in range(TM // CHUNK):
            for g in range(NG):
                rows, cols = slice(c * CHUNK, (c + 1) * CHUNK), slice(g * CHUNK, (g + 1) * CHUNK)
                sv = _dot(ws_ref[g].astype(BF16), vn[rows, cols]) + bs_ref[g]
                u = _gelu(a_ref[pl.ds(c * CHUNK, CHUNK), pl.ds(g * CHUNK, CHUNK)].astype(F32))
                m_ref[pl.ds(c * CHUNK, CHUNK), pl.ds(g * CHUNK, CHUNK)] = (u * sv).astype(BF16)

    return pl.pallas_call(
        body, name=name, grid=(T // TM,),
        in_specs=[pl.BlockSpec((TM, 2 * W), lambda i: (i, 0)), pl.BlockSpec((1, W), lambda i: (0, 0)),
                  pl.BlockSpec((NG, CHUNK, CHUNK), lambda i: (0, 0, 0)),
                  pl.BlockSpec((NG, CHUNK, 1), lambda i: (0, 0, 0))],
        out_specs=pl.BlockSpec((TM, W), lambda i: (i, 0)),
        out_shape=jax.ShapeDtypeStruct((T, W), BF16),
        compiler_params=_params(("parallel",)),
    )(a, vg, ws, bs)


def _cm_bwd(cfg, name, a, vg, ws, bs, dm):
    T, TM, W, NG = cfg.T, cfg.TM, cfg.CMW, cfg.CMG

    def body(a_ref, vg_ref, ws_ref, bs_ref, dm_ref, da_ref, dws_ref, dbs_ref, dvg_ref, dvn_ref):
        i = pl.program_id(0)

        @pl.when(i == 0)
        def _():
            dws_ref[...] = jnp.zeros_like(dws_ref)
            dbs_ref[...] = jnp.zeros_like(dbs_ref)
            dvg_ref[...] = jnp.zeros_like(dvg_ref)

        v, v_grad = _gelu_and_grad(a_ref[:, pl.ds(W, W)].astype(F32))
        rstd = lax.rsqrt(jnp.mean(v * v, axis=-1, keepdims=True) + EPS)
        xhat = v * rstd
        vg = vg_ref[...]
        vn = (xhat * vg).astype(BF16)
        for c in range(TM // CHUNK):
            for g in range(NG):
                rows, cols = slice(c * CHUNK, (c + 1) * CHUNK), slice(g * CHUNK, (g + 1) * CHUNK)
                rs, cs = pl.ds(c * CHUNK, CHUNK), pl.ds(g * CHUNK, CHUNK)
                wsb = ws_ref[g].astype(BF16)
                blk = vn[rows, cols]
                sv = _dot(wsb, blk) + bs_ref[g]
                u, u_grad = _gelu_and_grad(a_ref[rs, cs].astype(F32))
                dmb = dm_ref[rs, cs].astype(F32)
                da_ref[rs, cs] = (dmb * sv * u_grad).astype(BF16)
                dsv = dmb * u
                dsvb = dsv.astype(BF16)
                dbs_ref[g] += jnp.sum(dsv, axis=1, keepdims=True)
                dws_ref[g] += _dot_nt(dsvb, blk)
                dvn_ref[rs, cs] = _dot_tn(wsb, dsvb)
        dvn = dvn_ref[...]
        dvg_ref[...] += jnp.sum(dvn * xhat, axis=0, keepdims=True)
        dxh = dvn * vg
        dv = rstd * (dxh - xhat * jnp.mean(dxh * xhat, axis=-1, keepdims=True))
        da_ref[:, pl.ds(W, W)] = (dv * v_grad).astype(BF16)

    return pl.pallas_call(
        body, name=name, grid=(T // TM,),
        in_specs=[pl.BlockSpec((TM, 2 * W), lambda i: (i, 0)), pl.BlockSpec((1, W), lambda i: (0, 0)),
                  pl.BlockSpec((NG, CHUNK, CHUNK), lambda i: (0, 0, 0)),
                  pl.BlockSpec((NG, CHUNK, 1), lambda i: (0, 0, 0)), pl.BlockSpec((TM, W), lambda i: (i, 0))],
        out_specs=[pl.BlockSpec((TM, 2 * W), lambda i: (i, 0)), pl.BlockSpec((NG, CHUNK, CHUNK), lambda i: (0, 0, 0)),
                   pl.BlockSpec((NG, CHUNK, 1), lambda i: (0, 0, 0)), pl.BlockSpec((1, W), lambda i: (0, 0))],
        out_shape=[jax.ShapeDtypeStruct((T, 2 * W), BF16), jax.ShapeDtypeStruct((NG, CHUNK, CHUNK), F32),
                   jax.ShapeDtypeStruct((NG, CHUNK, 1), F32), jax.ShapeDtypeStruct((1, W), F32)],
        scratch_shapes=[pltpu.VMEM((TM, W), F32)],
        compiler_params=_params(("arbitrary",)),
    )(a, vg, ws, bs, dm)


def _layer_weights(l):
    mixer = ("ab_w_in", "ab_w_out") if l % 2 == 0 else ("cm_w_in", "cm_w_out")
    return [(mixer[0], l // 2), (mixer[1], l // 2), ("ff_w1", l), ("ff_w2", l)]


def _local_step(cfg, xcat, tgt, mods, shards, w, first_weight):
    D, TM, H = cfg.D, cfg.TM, cfg.H
    cosf, sinf = _rope_tables(cfg)
    full, big, recv = {}, {}, {}

    def gather_of(keys):
        return _Carry("gather", tuple(shards[n][i] for n, i in keys), tuple(BIG[n] for n, _ in keys))

    def exchange_of(keys):
        return _Carry("exchange", tuple(big[k] for k in keys), tuple(BIG[n] for n, _ in keys))

    def mm(pending, name, a, b, **kw):
        if not pending:
            return _mm(name, a, b, **kw)
        key, carry, sink = pending.pop(0)
        out, (got,) = _mm(name, a, b, carry=carry, **kw)
        sink[key] = got
        return out

    def with_carry(call, keys, make, sink):
        out, got = call(carry=make(keys) if keys else None)
        sink.update(zip(keys, got))
        return out

    keys0 = _layer_weights(0)
    full[keys0[0]] = first_weight
    TG = 3 * TM if cfg.TPE % 3 == 0 else TM
    tiles_per_ex = cfg.S // TG
    gate_spec = pl.BlockSpec((None, 2, 6, D), lambda i, j, k: (i // tiles_per_ex, 0, 0, 0))

    def resid_epi(igate, nxt):
        def epi(acc, row_tile, x_ref, mod_ref, *nxt_refs):
            row = lax.broadcasted_iota(jnp.int32, (TG, 1), 0)
            is_ctx = jnp.logical_and(row_tile % tiles_per_ex == 0, row < cfg.SC)

            def pick(ref, idx):
                return jnp.where(is_ctx, ref[0, pl.ds(idx, 1), :], ref[1, pl.ds(idx, 1), :])

            x = x_ref[...] + pick(mod_ref, igate) * acc
            if nxt is None:
                return x, acc
            gain_ref, modn_ref = nxt_refs
            n = x * lax.rsqrt(jnp.mean(x * x, axis=-1, keepdims=True) + EPS) * gain_ref[...]
            return x, acc, n * (1.0 + pick(modn_ref, nxt[3])) + pick(modn_ref, nxt[2])
        return epi

    def gated_out(pending, name, a, key, x, mod, igate, nxt=None):
        extras = [(x, pl.BlockSpec((TG, D), lambda i, j, k: (i, j))), (mod, gate_spec)]
        if nxt is not None:
            extras += [(nxt[0], pl.BlockSpec((1, D), lambda i, j, k: (0, 0))), (nxt[1], gate_spec)]
        return mm(pending, name, a, full[key], mode="nn", tm=TG, tn=D, outs=[F32, BF16] + [BF16] * (nxt is not None),
                  epi=resid_epi(igate, nxt), extras=extras, resident=True)

    saved = []
    x = xcat
    h = _norm_mod_fwd(cfg, "norm1_fwd_0", x, w["norm1_g"][0][None], mods[0], 0, 1)
    for l in range(N_LAYERS):
        li = l // 2
        mod = mods[l]
        k_in, k_out, k_ff1, k_ff2 = _layer_weights(l)
        pend = [(k, gather_of([k]), full) for k in _layer_weights(l + 1)] if l + 1 < N_LAYERS else []
        norm2 = (w["norm2_g"][l][None], mod, 3, 4)
        s = {"x0": x, "h": h}
        if l % 2 == 0:
            lgb = jnp.broadcast_to(jax.nn.log_sigmoid(w["ret_decay"][li]).T[:, :, None, None], (H, 2, 8, LANES))
            qg, kg = w["att_q_norm_g"][li][None], w["att_k_norm_g"][li][None]
            s["p"] = mm(pend, f"ab_in_{l}", s["h"], full[k_in], mode="nn", outs=[BF16], tn=768)
            s["rq"], s["rk"], s["aq"], s["ak"] = _prep_fwd(cfg, f"prep_fwd_{l}", s["p"], cosf, sinf, qg, kg)
            s["o"], ret, s["st"] = with_carry(
                functools.partial(_retention_fwd, cfg, f"ret_fwd_{l}", s["rq"], s["rk"], s["p"], lgb),
                keys0[1:3] if l == 0 else [], gather_of, full)
            s["cat"], s["lse"] = with_carry(
                functools.partial(_attention_fwd, cfg, f"att_fwd_{l}", s["aq"], s["ak"], s["p"], ret),
                keys0[3:] if l == 0 else [], gather_of, full)
            s["lgb"], s["qg"], s["kg"] = lgb, qg, kg
            x, s["y1"], s["h2"] = gated_out(pend, f"ab_out_{l}", s["cat"], k_out, x, mod, 2, norm2)
        else:
            s["a"] = mm(pend, f"cm_in_{l}", s["h"], full[k_in], mode="nn", outs=[BF16])
            s["vg"], s["ws"], s["bs"] = w["cm_v_norm_g"][li][None], w["cm_w_s"][li], w["cm_b_s"][li][:, :, None]
            s["m"] = _cm_fwd(cfg, f"cm_fwd_{l}", s["a"], s["vg"], s["ws"], s["bs"])
            x, s["y1"], s["h2"] = gated_out(pend, f"cm_out_{l}", s["m"], k_out, x, mod, 2, norm2)
        s["x1"] = x
        s["r"] = mm(pend, f"ff1_{l}", s["h2"], full[k_ff1], mode="nn", outs=[BF16],
                    epi=lambda acc, row_tile: (jnp.square(jnp.maximum(acc, 0.0)),))
        if l + 1 < N_LAYERS:
            x, s["y2"], h = gated_out(pend, f"ff2_{l}", s["r"], k_ff2, x, mod, 5,
                                      (w["norm1_g"][l + 1][None], mods[l + 1], 0, 1))
        else:
            x, s["y2"] = gated_out(pend, f"ff2_{l}", s["r"], k_ff2, x, mod, 5)
        saved.append(s)

    loss, dx = _loss_grad(cfg, x, tgt)

    small = {k: [None] * n for k, n in (("norm1_g", 4), ("norm2_g", 4), ("ret_lg", 2), ("att_q_norm_g", 2),
                                        ("att_k_norm_g", 2), ("cm_v_norm_g", 2), ("cm_w_s", 2), ("cm_b_s", 2))}
    dmods = [None] * N_LAYERS

    for l in reversed(range(N_LAYERS)):
        li = l // 2
        s, mod = saved[l], mods[l]
        k_in, k_out, k_ff1, k_ff2 = _layer_weights(l)
        above = _layer_weights(l + 1) if l + 1 < N_LAYERS else [None] * 4
        if l == N_LAYERS - 1:
            dy2, dg2 = _gate_bwd(cfg, f"gate2_bwd_{l}", dx, s["y2"], mod, 5)
        da2, big[k_ff2] = with_carry(functools.partial(_ff_bwd, f"ff2_bwd_{l}", dy2, s["r"], full[k_ff2], "w2"),
                                     [above[3], above[1]] if above[0] else [], exchange_of, recv)
        dh2, big[k_ff1] = with_carry(functools.partial(_ff_bwd, f"ff1_bwd_{l}", s["h2"], da2, full[k_ff1], "w1"),
                                     [above[2], above[0]] if above[0] else [], exchange_of, recv)
        dx, dm2, small["norm2_g"][l], do, dg1 = _norm_mod_bwd(
            cfg, f"norm2_bwd_{l}", s["x1"], w["norm2_g"][l][None], mod, 3, 4, dh2, dx, gate=(s["y1"], mod, 2))
        if l % 2 == 0:
            big[k_out] = _mm(f"ab_out_dw_{l}", s["cat"], do, mode="tn", outs=[BF16])
            dcat = _mm(f"ab_out_dx_{l}", do, full[k_out], mode="nt", outs=[BF16])
            d_rq, d_rk, d_rv, d_gt, dlg = with_carry(
                functools.partial(_retention_bwd, cfg, f"ret_bwd_{l}", s["rq"], s["rk"], s["p"], s["o"], s["st"], dcat,
                                  s["lgb"]), [k_ff2, k_ff1] if l == 0 else [], exchange_of, recv)
            d_aq, d_ak, d_av = with_carry(
                functools.partial(_attention_bwd, cfg, f"att_bwd_{l}", s["aq"], s["ak"], s["p"], s["cat"], s["lse"], dcat),
                [k_out] if l == 0 else [], exchange_of, recv)
            dp, dqg, dkg = _prep_bwd(cfg, f"prep_bwd_{l}", s["p"], cosf, sinf, s["qg"], s["kg"],
                                     d_rq, d_rk, d_rv, d_gt, d_aq, d_ak, d_av)
            small["ret_lg"][li] = jnp.sum(dlg[:, :, :, 0, 0], axis=0).T
            small["att_q_norm_g"][li], small["att_k_norm_g"][li] = dqg[0], dkg[0]
            big[k_in] = _mm(f"ab_in_dw_{l}", s["h"], dp, mode="tn", outs=[BF16])
            last = [(k_in, exchange_of([k_in]), recv)] if l == 0 else []
            dh = mm(last, f"ab_in_dx_{l}", dp, full[k_in], mode="nt", outs=[BF16], resident=True)
        else:
            big[k_out] = _mm(f"cm_out_dw_{l}", s["m"], do, mode="tn", outs=[BF16])
            dm = _mm(f"cm_out_dx_{l}", do, full[k_out], mode="nt", outs=[BF16])
            da, dws, dbs, dvg = _cm_bwd(cfg, f"cm_bwd_{l}", s["a"], s["vg"], s["ws"], s["bs"], dm)
            small["cm_w_s"][li], small["cm_b_s"][li], small["cm_v_norm_g"][li] = dws, dbs[:, :, 0], dvg[0]
            big[k_in] = _mm(f"cm_in_dw_{l}", s["h"], da, mode="tn", outs=[BF16])
            dh = _mm(f"cm_in_dx_{l}", da, full[k_in], mode="nt", outs=[BF16], resident=True)
        below = (saved[l - 1]["y2"], mods[l - 1], 5) if l > 0 else None
        dx, dm1, small["norm1_g"][l], *rest = _norm_mod_bwd(
            cfg, f"norm1_bwd_{l}", s["x0"], w["norm1_g"][l][None], mod, 0, 1, dh, dx, gate=below, lat_only=l == 0)
        dmods[l] = jnp.concatenate([dm1, dg1, dm2, dg2], axis=2)
        if l > 0:
            dy2, dg2 = rest
    return loss, dx, recv, small, dmods


N_DEV = 8
N_CHIP = 4
MESH = pl.DeviceIdType.MESH
ANY = pl.BlockSpec(memory_space=pl.ANY)
BIG = {"ab_w_in": 1, "ab_w_out": 0, "cm_w_in": 1, "cm_w_out": 0, "ff_w1": 1, "ff_w2": 0}


class _Carry(NamedTuple):
    kind: str
    srcs: tuple
    axes: tuple


def _place():
    x, y, c = lax.axis_index("x"), lax.axis_index("y"), lax.axis_index("c")
    return x, y, c, [(1 - x, y), (x, 1 - y), (1 - x, 1 - y)]


def _shard_of(ref, axis, s, width):
    start = pl.multiple_of(s * width, LANES)
    if axis == 0:
        return ref.at[pl.ds(start, width), :]
    return ref.at[:, pl.ds(start, width)]


def _carry_out_shapes(carry):
    shapes = []
    for src, axis in zip(carry.srcs, carry.axes):
        shape = list(src.shape)
        if carry.kind == "swap":
            pass
        elif carry.kind == "gather":
            shape[axis] *= N_CHIP
        else:
            shape[axis] //= N_CHIP
            shape = [N_CHIP] + shape
        shapes.append(jax.ShapeDtypeStruct(tuple(shape), src.dtype))
    return shapes


def _carry_copies(carry, srcs, dsts, send_sems, recv_sems, local_sems):
    x, y, c, chips = _place()
    me = 2 * x + y
    copies = []
    if carry.kind == "swap":
        return [pltpu.make_async_remote_copy(
            src_ref=srcs[t], dst_ref=dsts[t], send_sem=send_sems.at[3 * t], recv_sem=recv_sems.at[3 * t],
            device_id=(x, y, 1 - c), device_id_type=MESH) for t in range(len(srcs))]
    for t, axis in enumerate(carry.axes):
        if carry.kind == "gather":
            own = _shard_of(dsts[t], axis, me, srcs[t].shape[axis])
            copies.append(pltpu.make_async_copy(srcs[t], own, local_sems.at[t]))
            parts = [(srcs[t], own)] * 3
        else:
            width = dsts[t].shape[1 + axis]
            copies.append(pltpu.make_async_copy(_shard_of(srcs[t], axis, me, width), dsts[t].at[3], local_sems.at[t]))
            parts = [(_shard_of(srcs[t], axis, 2 * px + py, width), dsts[t].at[j]) for j, (px, py) in enumerate(chips)]
        for j, (px, py) in enumerate(chips):
            copies.append(pltpu.make_async_remote_copy(
                src_ref=parts[j][0], dst_ref=parts[j][1], send_sem=send_sems.at[3 * t + j],
                recv_sem=recv_sems.at[3 * t + j], device_id=(px, py, c), device_id_type=MESH))
    return copies


def _allgather8(name, block, carry=None):
    m_per, n = block.shape
    nc = len(carry.srcs) if carry is not None else 0

    def body(*refs):
        x_ref, out_ref = refs[0], refs[1 + nc]
        send_sems, recv_sems, local_sem = refs[2 + 2 * nc:5 + 2 * nc]
        carried = _carry_copies(carry, refs[1:1 + nc], refs[2 + nc:2 + 2 * nc], *refs[5 + 2 * nc:]) if nc else []
        for cp in carried:
            cp.start()
        x, y, c, chips = _place()
        me, sibling = (x, y, c), (x, y, 1 - c)

        def rows(px, py, pc):
            return out_ref.at[pl.ds((4 * px + 2 * py + pc) * m_per, m_per), :]

        def copy(k, blk, to, src=None):
            return pltpu.make_async_remote_copy(
                src_ref=rows(*blk) if src is None else src, dst_ref=rows(*blk),
                send_sem=send_sems.at[k], recv_sem=recv_sems.at[k], device_id=to, device_id_type=MESH)

        mine = pltpu.make_async_copy(x_ref, rows(*me), local_sem)
        mine.start()
        first = [copy(0, me, sibling, src=x_ref)]
        first += [copy(1 + j, me, (*chip, c), src=x_ref) for j, chip in enumerate(chips)]
        for cp in first:
            cp.start()
        passed = [copy(4 + j, (*chip, c), sibling) for j, chip in enumerate(chips)]
        for j, chip in enumerate(chips):
            copy(1 + j, (*chip, c), me).wait_recv()
            passed[j].start()
        copy(0, sibling, me).wait_recv()
        for j, chip in enumerate(chips):
            copy(4 + j, (*chip, 1 - c), me).wait_recv()
        for cp in first + passed:
            cp.wait_send()
        mine.wait()
        for cp in carried:
            cp.wait()

    res = pl.pallas_call(
        body, name=name,
        out_shape=[jax.ShapeDtypeStruct((N_DEV * m_per, n), block.dtype)] + (_carry_out_shapes(carry) if nc else []),
        in_specs=[pl.BlockSpec(memory_space=pltpu.VMEM)] + [ANY] * nc,
        out_specs=[pl.BlockSpec(memory_space=pltpu.VMEM)] + [ANY] * nc,
        scratch_shapes=[pltpu.SemaphoreType.DMA((7,)), pltpu.SemaphoreType.DMA((7,)), pltpu.SemaphoreType.DMA]
        + _carry_scratch(nc),
        compiler_params=pltpu.CompilerParams(vmem_limit_bytes=VMEM_LIMIT),
    )(block, *(carry.srcs if nc else ()))
    return (res[0], res[1:]) if nc else res[0]


def _rows_view(a):
    if a.ndim == 1:
        return a.reshape(1, a.shape[0])
    return a.reshape(-1, a.shape[-1])


def _row_tile(rows, cols, target_elems=1 << 17):
    tr = rows
    while tr % 16 == 0 and tr * cols > target_elems:
        tr //= 2
    return tr


def _sum_leading(name, a):
    n, rows, cols = a.shape
    tr = _row_tile(rows, cols * n, target_elems=1 << 20)

    def body(a_ref, o_ref):
        acc = a_ref[0].astype(F32)
        for i in range(1, n):
            acc = acc + a_ref[i].astype(F32)
        o_ref[...] = acc

    return pl.pallas_call(
        body, name=name, grid=(rows // tr,),
        in_specs=[pl.BlockSpec((n, tr, cols), lambda i: (0, i, 0))],
        out_specs=pl.BlockSpec((tr, cols), lambda i: (i, 0)),
        out_shape=jax.ShapeDtypeStruct((rows, cols), F32),
        compiler_params=_params(("parallel",)),
    )(a)


def _sum_parts_layers(name, parts):
    n_layers = len(parts)
    n, rows, cols = parts[0].shape
    tr = _row_tile(rows, cols * n, target_elems=1 << 20)

    def body(*refs):
        o_ref = refs[n_layers]
        layer = pl.program_id(0)
        for k in range(n_layers):
            @pl.when(layer == k)
            def _():
                acc = refs[k][0].astype(F32)
                for i in range(1, n):
                    acc = acc + refs[k][i].astype(F32)
                o_ref[...] = acc

    def in_spec(k):
        return pl.BlockSpec((n, tr, cols), lambda l, i: (0, jnp.where(l == k, i, 0), 0))

    return pl.pallas_call(
        body, name=name, grid=(n_layers, rows // tr),
        in_specs=[in_spec(k) for k in range(n_layers)],
        out_specs=pl.BlockSpec((None, tr, cols), lambda l, i: (l, i, 0)),
        out_shape=jax.ShapeDtypeStruct((n_layers, rows, cols), F32),
        compiler_params=_params(("arbitrary", "arbitrary")),
    )(*parts)


def _silu_rows(name, x):
    def body(x_ref, o_ref):
        v = x_ref[...]
        o_ref[...] = v * jax.nn.sigmoid(v)

    return pl.pallas_call(body, name=name, out_shape=jax.ShapeDtypeStruct(x.shape, F32))(x)


def _silu_bwd_rows(name, x, dy):
    def body(x_ref, dy_ref, o_ref):
        v = x_ref[...]
        sg = jax.nn.sigmoid(v)
        o_ref[...] = dy_ref[...] * (sg * (1.0 + v * (1.0 - sg)))

    return pl.pallas_call(body, name=name, out_shape=jax.ShapeDtypeStruct(x.shape, F32))(x, dy)


def _adamw(name, w, g_parts, m, v):
    shape = w.shape
    w2, m2, v2 = _rows_view(w), _rows_view(m), _rows_view(v)
    gs = [_rows_view(g) for g in g_parts]
    rows, cols = w2.shape
    tr = _row_tile(rows, cols)
    ng = len(gs)

    def body(*refs):
        w_ref, m_ref, v_ref = refs[0], refs[1], refs[2]
        g_refs = refs[3:3 + ng]
        g_out, d_out, m_out, v_out = refs[3 + ng:]
        g = g_refs[0][...]
        for r in g_refs[1:]:
            g = g + r[...]
        m1 = ADAM_B1 * m_ref[...] + (1.0 - ADAM_B1) * g
        v1 = ADAM_B2 * v_ref[...] + (1.0 - ADAM_B2) * jnp.square(g)
        m_hat = m1 / (1.0 - ADAM_B1 ** ADAM_STEP)
        v_hat = v1 / (1.0 - ADAM_B2 ** ADAM_STEP)
        g_out[...] = g
        d_out[...] = -ADAM_LR * (m_hat / (jnp.sqrt(v_hat) + ADAM_EPS) + ADAM_WD * w_ref[...])
        m_out[...] = m1
        v_out[...] = v1

    spec = pl.BlockSpec((tr, cols), lambda i: (i, 0))
    res = pl.pallas_call(
        body, name=name, grid=(rows // tr,), in_specs=[spec] * (3 + ng), out_specs=[spec] * 4,
        out_shape=[jax.ShapeDtypeStruct((rows, cols), F32)] * 4,
        compiler_params=_params(("parallel",)),
    )(w2, m2, v2, *gs)
    return tuple(r.reshape(shape) for r in res)


MOD_ROWS = 48


def kernel(x, c, ctx, c_ctx, mod_w, mod_b, norm1_g, norm2_g, ab_w_in, ab_w_out, ret_decay, att_q_norm_g, att_k_norm_g, cm_w_in, cm_v_norm_g, cm_w_s, cm_b_s, cm_w_out, ff_w1, ff_w2, loss_target, m_c_ctx, m_mod_w, m_mod_b, m_norm1_g, m_norm2_g, m_ab_w_in, m_ab_w_out, m_ret_decay, m_att_q_norm_g, m_att_k_norm_g, m_cm_w_in, m_cm_v_norm_g, m_cm_w_s, m_cm_b_s, m_cm_w_out, m_ff_w1, m_ff_w2, v_c_ctx, v_mod_w, v_mod_b, v_norm1_g, v_norm2_g, v_ab_w_in, v_ab_w_out, v_ret_decay, v_att_q_norm_g, v_att_k_norm_g, v_cm_w_in, v_cm_v_norm_g, v_cm_w_s, v_cm_b_s, v_cm_w_out, v_ff_w1, v_ff_w2):
    B, SL, D = x.shape
    cfg = Cfg(B=B, SC=ctx.shape[1], SL=SL, D=D, FF=ff_w1.shape[2] * N_CHIP)
    L = N_LAYERS
    n_ex = B * N_DEV
    mcols = mod_w.shape[2]
    weights = dict(c_ctx=c_ctx, mod_w=mod_w, mod_b=mod_b, norm1_g=norm1_g, norm2_g=norm2_g, ab_w_in=ab_w_in,
                   ab_w_out=ab_w_out, ret_decay=ret_decay, att_q_norm_g=att_q_norm_g, att_k_norm_g=att_k_norm_g,
                   cm_w_in=cm_w_in, cm_v_norm_g=cm_v_norm_g, cm_w_s=cm_w_s, cm_b_s=cm_b_s, cm_w_out=cm_w_out,
                   ff_w1=ff_w1, ff_w2=ff_w2)
    m_in = dict(c_ctx=m_c_ctx, mod_w=m_mod_w, mod_b=m_mod_b, norm1_g=m_norm1_g, norm2_g=m_norm2_g, ab_w_in=m_ab_w_in,
                ab_w_out=m_ab_w_out, ret_decay=m_ret_decay, att_q_norm_g=m_att_q_norm_g, att_k_norm_g=m_att_k_norm_g,
                cm_w_in=m_cm_w_in, cm_v_norm_g=m_cm_v_norm_g, cm_w_s=m_cm_w_s, cm_b_s=m_cm_b_s, cm_w_out=m_cm_w_out,
                ff_w1=m_ff_w1, ff_w2=m_ff_w2)
    v_in = dict(c_ctx=v_c_ctx, mod_w=v_mod_w, mod_b=v_mod_b, norm1_g=v_norm1_g, norm2_g=v_norm2_g, ab_w_in=v_ab_w_in,
                ab_w_out=v_ab_w_out, ret_decay=v_ret_decay, att_q_norm_g=v_att_q_norm_g, att_k_norm_g=v_att_k_norm_g,
                cm_w_in=v_cm_w_in, cm_v_norm_g=v_cm_v_norm_g, cm_w_s=v_cm_w_s, cm_b_s=v_cm_b_s, cm_w_out=v_cm_w_out,
                ff_w1=v_ff_w1, ff_w2=v_ff_w2)
    xi, yi, ci = lax.axis_index("x"), lax.axis_index("y"), lax.axis_index("c")
    chip = 2 * xi + yi
    dev = 2 * chip + ci

    shards = {n: [weights[n][i].astype(BF16) for i in range(weights[n].shape[0])] for n in BIG}
    vgw = cm_v_norm_g.shape[1]
    blk = jnp.zeros((8, D), F32).at[:B].set(c).at[B:B + 2, :vgw].set(cm_v_norm_g)
    g0 = _allgather8("gather_c", blk).reshape(N_DEV, 8, D)
    c_all = g0[:, :B].reshape(n_ex, D)
    vg_full = jnp.concatenate([g0[2 * s, B:B + 2, :vgw] for s in range(N_CHIP)], axis=-1)

    pre = jnp.zeros((MOD_ROWS, D), F32).at[:n_ex].set(c_all).at[n_ex].set(c_ctx)
    act = _silu_rows("silu_c", pre)
    mpart = jnp.stack([_mm(f"mod_fwd_{l}", act, mod_w, mode="nn", layer=l, outs=[F32], tn=mcols) for l in range(L)])
    g1, (first_weight,) = _allgather8("gather_mod", mpart.reshape(L * MOD_ROWS, mcols),
                                      carry=_Carry("gather", (shards["ab_w_in"][0],), (BIG["ab_w_in"],)))
    g1 = g1.reshape(N_DEV, L, MOD_ROWS, mcols)
    mod_all = jnp.concatenate([g1[2 * s] for s in range(N_CHIP)], axis=-1) + mod_b[:, None, :]
    mod_lat = lax.dynamic_slice_in_dim(mod_all, dev * B, B, axis=1)
    mod_ctx = jnp.broadcast_to(mod_all[:, n_ex][:, None], mod_lat.shape)
    mods = jnp.stack([mod_ctx, mod_lat], axis=2).reshape(L, B, 2, 6, D)

    w = dict(norm1_g=norm1_g, norm2_g=norm2_g, ret_decay=ret_decay, att_q_norm_g=att_q_norm_g,
             att_k_norm_g=att_k_norm_g, cm_v_norm_g=vg_full, cm_w_s=cm_w_s, cm_b_s=cm_b_s)
    xcat = jnp.concatenate([ctx, x], axis=1).reshape(cfg.T, D)
    loss_local, dx_lat, recv, small, dmods = _local_step(cfg, xcat, loss_target.reshape(B * SL, D), mods, shards, w,
                                                         first_weight)
    loss = lax.psum(loss_local, ("x", "y", "c"))
    grad_x = dx_lat.reshape(B, SL, D)

    part = [_sum_parts_layers(f"sum_{n}", [recv[(n, i)] for i in range(weights[n].shape[0])]) for n in BIG]

    dmod = jnp.stack(dmods).reshape(L, B, 2, 6 * D)
    dmod_lat = dmod[:, :, 1]
    dmod_ctx = jnp.sum(dmod[:, :, 0], axis=1)
    d_ret = jnp.stack(small["ret_lg"]) * jax.nn.sigmoid(-ret_decay)
    summed = [dmod_ctx.reshape(-1), jnp.stack(small["norm1_g"]).reshape(-1), jnp.stack(small["norm2_g"]).reshape(-1),
              jnp.stack(small["cm_v_norm_g"]).reshape(-1), jnp.stack(small["cm_w_s"]).reshape(-1),
              jnp.stack(small["cm_b_s"]).reshape(-1), jnp.stack(small["att_q_norm_g"]).reshape(-1),
              jnp.stack(small["att_k_norm_g"]).reshape(-1), d_ret.reshape(-1)]
    sizes = [int(a.shape[0]) for a in summed]
    flat = jnp.concatenate(summed + [dmod_lat.reshape(-1)])
    n_sum = sum(sizes)
    n_sum_rows = -(-n_sum // D)
    lat_rows = (L * B * 6 * D) // D
    pack_rows = -(-(n_sum_rows + lat_rows) // 8) * 8
    packed = jnp.zeros((pack_rows * D,), F32).at[:n_sum].set(flat[:n_sum])
    packed = packed.at[n_sum_rows * D:(n_sum_rows + lat_rows) * D].set(flat[n_sum:]).reshape(pack_rows, D)
    g2, other = _allgather8("gather_small", packed, carry=_Carry("swap", tuple(part), (0,) * len(part)))
    g2 = g2.reshape(N_DEV, pack_rows, D)
    tot = _sum_leading("sum_small", g2[:, :n_sum_rows]).reshape(-1)
    pieces, off = [], 0
    for sz in sizes:
        pieces.append(tot[off:off + sz])
        off += sz
    dmod_ctx_t, g_n1, g_n2, g_vg, g_ws, g_bs, g_qg, g_kg, g_rd = pieces
    dmod_ctx_t = dmod_ctx_t.reshape(L, 6 * D)
    dmod_lat_all = g2[:, n_sum_rows:n_sum_rows + lat_rows].reshape(N_DEV, L, B, 6 * D)
    dmod_rows = jnp.zeros((L, MOD_ROWS, 6 * D), F32)
    dmod_rows = dmod_rows.at[:, :n_ex].set(jnp.transpose(dmod_lat_all, (1, 0, 2, 3)).reshape(L, n_ex, 6 * D))
    dmod_rows = dmod_rows.at[:, n_ex].set(dmod_ctx_t)
    g_mod_b = _sum_leading("sum_mod_b", jnp.transpose(dmod_rows, (1, 0, 2)))
    dmod_mine = lax.dynamic_slice_in_dim(dmod_rows, chip * mcols, mcols, axis=2)
    g_mod_w = jnp.stack([_mm(f"mod_dw_{l}", act, dmod_mine[l], mode="tn", outs=[F32], tn=mcols) for l in range(L)])
    ctx8 = jnp.zeros((L, 8, mcols), F32).at[:, 0].set(dmod_mine[:, n_ex])
    dcc = [_mm(f"mod_dctx_{l}", ctx8[l], mod_w, mode="nt", layer=l, outs=[F32], tk=mcols) for l in range(L)]
    dcc = _sum_leading("sum_dctx_layers", jnp.stack(dcc))
    g3 = _allgather8("gather_dctx", dcc).reshape(N_DEV, 8, D)
    dcc_t = _sum_leading("sum_dctx_chips", g3[0::2])[0:1]
    g_c_ctx = _silu_bwd_rows("silu_bwd_cctx", c_ctx[None], dcc_t)[0]

    vg_mine = lax.dynamic_slice_in_dim(g_vg.reshape(2, -1), chip * vgw, vgw, axis=1)
    small_g = dict(c_ctx=g_c_ctx, mod_w=g_mod_w, mod_b=g_mod_b, norm1_g=g_n1.reshape(norm1_g.shape),
                   norm2_g=g_n2.reshape(norm2_g.shape), ret_decay=g_rd.reshape(ret_decay.shape),
                   att_q_norm_g=g_qg.reshape(att_q_norm_g.shape), att_k_norm_g=g_kg.reshape(att_k_norm_g.shape),
                   cm_v_norm_g=vg_mine, cm_w_s=g_ws.reshape(cm_w_s.shape), cm_b_s=g_bs.reshape(cm_b_s.shape))
    out = {}
    for n, g in small_g.items():
        out[n] = _adamw(f"adamw_{n}", weights[n], [g], m_in[n], v_in[n])
    for n, p_mine, p_other in zip(BIG, part, other):
        out[n] = _adamw(f"adamw_{n}", weights[n], [p_mine, p_other], m_in[n], v_in[n])

    order = list(weights)
    return (loss, grad_x, *[out[n][0] for n in order], *[out[n][1] for n in order],
            *[out[n][2] for n in order], *[out[n][3] for n in order])
```

```python
import functools
import math
from typing import NamedTuple

import jax
import jax.numpy as jnp
from jax import lax
from jax.experimental import pallas as pl
from jax.experimental.pallas import tpu as pltpu

F32 = jnp.float32
BF16 = jnp.bfloat16
EPS = 1e-6
ROPE_BASE = 10000.0
LANES = 128
CHUNK = 128
N_LAYERS = 4
VMEM_LIMIT = 56 * 1024 * 1024

ADAM_LR = 0.001
ADAM_B1 = 0.9
ADAM_B2 = 0.999
ADAM_EPS = 1e-08
ADAM_WD = 0.01
ADAM_STEP = 10


class Cfg(NamedTuple):
    B: int = 4
    SC: int = 256
    SL: int = 2048
    D: int = 1024
    FF: int = 4096
    GRID_W: int = 64
    H: int = 4
    KV: int = 2
    CMW: int = 1024
    CMG: int = 8

    @property
    def S(self):
        return self.SC + self.SL

    @property
    def T(self):
        return self.B * self.S

    @property
    def TM(self):
        return self.SC

    @property
    def TPE(self):
        return self.S // self.SC

    @property
    def ABW(self):
        return (5 * self.H + 2 * self.KV) * CHUNK


def _tile(dim, pref):
    t = min(dim, pref)
    while dim % t:
        t -= LANES
    return t


def _dot(a, b):
    return lax.dot_general(a, b, (((1,), (0,)), ((), ())), preferred_element_type=F32)


def _dot_nt(a, b):
    return lax.dot_general(a, b, (((1,), (1,)), ((), ())), preferred_element_type=F32)


def _dot_tn(a, b):
    return lax.dot_general(a, b, (((0,), (0,)), ((), ())), preferred_element_type=F32)


def _params(sem, vmem=VMEM_LIMIT):
    return pltpu.CompilerParams(dimension_semantics=sem, vmem_limit_bytes=vmem)


def _mod_index(cfg):
    tpe = cfg.TPE
    return lambda i: (i // tpe, jnp.minimum(i % tpe, 1), 0, 0)


def _mm(name, a, b, *, mode, outs, tm=1024, tn=1024, tk=1024, layer=None, epi=None, extras=(), carry=None,
        resident=False):
    bshape = b.shape[1:] if layer is not None else b.shape
    if mode == "nn":
        (M, K), N = a.shape, bshape[1]
    elif mode == "nt":
        (M, K), N = a.shape, bshape[0]
    else:
        (K, M), N = a.shape, bshape[1]
    tm, tn, tk = _tile(M, tm), _tile(N, tn), _tile(K, tk)
    kchunk = tk
    if resident:
        tk = K
    nk = K // tk
    a_spec = (pl.BlockSpec((tk, tm), lambda i, j, k: (k, i)) if mode == "tn"
              else pl.BlockSpec((tm, tk), lambda i, j, k: (i, k)))
    if mode == "nt":
        bblk, bidx = (tn, tk), (lambda i, j, k: (j, k))
    else:
        bblk, bidx = (tk, tn), (lambda i, j, k: (k, j))
    if layer is not None:
        b_spec = pl.BlockSpec((None,) + bblk, lambda i, j, k: (layer,) + bidx(i, j, k))
    elif resident:
        b_spec = pl.BlockSpec(bblk, bidx, pipeline_mode=pl.Buffered(1))
    else:
        b_spec = pl.BlockSpec(bblk, bidx)
    ne, no = len(extras), len(outs)
    nc = len(carry.srcs) if carry is not None else 0
    dot = {"nn": _dot, "nt": _dot_nt, "tn": _dot_tn}[mode]
    grid = (M // tm, N // tn, nk)

    def body(*refs):
        a_ref, b_ref = refs[0], refs[1]
        ex, out_refs = refs[2:2 + ne], refs[2 + ne + nc:2 + ne + nc + no]
        row_tile = pl.program_id(0)

        if nc:
            step = (pl.program_id(0) * grid[1] + pl.program_id(1)) * grid[2] + pl.program_id(2)
            c_src = refs[2 + ne:2 + ne + nc]
            c_dst = refs[2 + ne + nc + no:2 + ne + 2 * nc + no]
            sems = refs[2 + ne + 2 * nc + no:2 + ne + 2 * nc + no + 3]

            @pl.when(step == 0)
            def _():
                for cp in _carry_copies(carry, c_src, c_dst, *sems):
                    cp.start()

        def finish(acc):
            res = epi(acc, row_tile, *ex) if epi is not None else (acc,)
            for r, o in zip(res, out_refs):
                o[...] = r.astype(o.dtype)

        if resident and K > kchunk:
            part = None
            for c in range(K // kchunk):
                ks = pl.ds(c * kchunk, kchunk)
                b_chunk = b_ref[:, ks] if mode == "nt" else b_ref[ks, :]
                term = dot(a_ref[:, ks].astype(BF16), b_chunk.astype(BF16))
                part = term if part is None else part + term
        else:
            part = dot(a_ref[...].astype(BF16), b_ref[...].astype(BF16))
        if nk == 1:
            finish(part)
        else:
            acc_ref = refs[-1]
            k = pl.program_id(2)

            @pl.when(k == 0)
            def _():
                acc_ref[...] = part

            @pl.when(k > 0)
            def _():
                acc_ref[...] += part

            @pl.when(k == nk - 1)
            def _():
                finish(acc_ref[...])

        if nc:
            @pl.when(step == grid[0] * grid[1] * grid[2] - 1)
            def _():
                for cp in _carry_copies(carry, c_src, c_dst, *sems):
                    cp.wait()

    scratch = [pltpu.SemaphoreType.DMA((3 * nc,)), pltpu.SemaphoreType.DMA((3 * nc,)),
               pltpu.SemaphoreType.DMA((nc,))] if nc else []
    if nk > 1:
        scratch.append(pltpu.VMEM((tm, tn), F32))
    res = pl.pallas_call(
        body, name=name, grid=grid,
        in_specs=[a_spec, b_spec] + [s for _, s in extras] + [ANY] * nc,
        out_specs=[pl.BlockSpec((tm, tn), lambda i, j, k: (i, j)) for _ in outs] + [ANY] * nc,
        out_shape=[jax.ShapeDtypeStruct((M, N), d) for d in outs] + (_carry_out_shapes(carry) if nc else []),
        scratch_shapes=scratch,
        compiler_params=_params(("arbitrary",) * 3 if nc else ("parallel", "parallel", "arbitrary")),
    )(a, b, *[x for x, _ in extras], *(carry.srcs if nc else ()))
    if nc:
        return (res[0] if no == 1 else res[:no]), res[no:]
    return res[0] if no == 1 else res


def _ff_bwd(name, first, second, weight, kind, carry=None):
    (T, D), FF = first.shape, second.shape[1]
    halves = 2
    tm = _tile(T, 1024)
    ffh = FF // halves if kind == "w2" else FF
    dh_cols = D if kind == "w2" else D // halves
    cw = _tile(ffh, 1024)
    n_steps = T // tm
    nc = len(carry.srcs) if carry is not None else 0

    def body(*refs):
        a_ref, b_ref, w_ref = refs[:3]
        c_src = refs[3:3 + nc]
        x_ref, dw_ref = refs[3 + nc:5 + nc]
        c_dst, sems = refs[5 + nc:5 + 2 * nc], refs[5 + 2 * nc:5 + 2 * nc + 3] if nc else ()
        acc_ref = refs[-1]
        i = pl.program_id(1)
        step = pl.program_id(0) * n_steps + i
        if nc:
            _carry_begin(carry, c_src, c_dst, sems, step)
        a = a_ref[...]
        dh = None
        for c in range(ffh // cw):
            cols = pl.ds(c * cw, cw)
            if kind == "w2":
                r = b_ref[:, cols]
                x_ref[:, cols] = (_dot_nt(a, w_ref[cols, :]) * (2.0 * jnp.sqrt(r.astype(F32)))).astype(BF16)
                part, dst = _dot_tn(r, a), acc_ref.at[cols, :]
            else:
                da = b_ref[:, cols]
                term = _dot_nt(da, w_ref[:, cols])
                dh = term if dh is None else dh + term
                part, dst = _dot_tn(a, da), acc_ref.at[:, cols]

            @pl.when(i == 0)
            def _():
                dst[...] = part

            @pl.when(i > 0)
            def _():
                dst[...] += part

        if kind == "w1":
            x_ref[...] = dh.astype(BF16)

        @pl.when(i == n_steps - 1)
        def _():
            dw_ref[...] = acc_ref[...].astype(BF16)

        if nc:
            _carry_end(carry, c_src, c_dst, sems, step, halves * n_steps)

    if kind == "w2":
        wshape = (ffh, D)
        a_spec = pl.BlockSpec((tm, D), lambda j, i: (i, 0))
        b_spec = pl.BlockSpec((tm, ffh), lambda j, i: (i, j))
        x_spec, x_cols = pl.BlockSpec((tm, ffh), lambda j, i: (i, j)), FF
    else:
        wshape = (dh_cols, FF)
        a_spec = pl.BlockSpec((tm, dh_cols), lambda j, i: (i, j))
        b_spec = pl.BlockSpec((tm, FF), lambda j, i: (i, 0))
        x_spec, x_cols = pl.BlockSpec((tm, dh_cols), lambda j, i: (i, j)), D
    wspec = pl.BlockSpec(wshape, lambda j, i: (j, 0), pipeline_mode=pl.Buffered(1))
    res = pl.pallas_call(
        body, name=name, grid=(halves, n_steps),
        in_specs=[a_spec, b_spec, wspec] + [ANY] * nc,
        out_specs=[x_spec, wspec] + [ANY] * nc,
        out_shape=[jax.ShapeDtypeStruct((T, x_cols), BF16), jax.ShapeDtypeStruct(weight.shape, BF16)]
        + (_carry_out_shapes(carry) if nc else []),
        scratch_shapes=_carry_scratch(nc) + [pltpu.VMEM(wshape, F32)],
        compiler_params=_params(("arbitrary", "arbitrary")),
    )(first, second, weight, *(carry.srcs if nc else ()))
    return res[:2], res[2:]


def _norm_mod_fwd(cfg, name, x, gain, mod, ish, isc):
    T, D, TM = cfg.T, cfg.D, cfg.TM

    def body(x_ref, g_ref, mod_ref, h_ref):
        x = x_ref[...]
        rstd = lax.rsqrt(jnp.mean(x * x, axis=-1, keepdims=True) + EPS)
        n = x * rstd * g_ref[...]
        h = n * (1.0 + mod_ref[pl.ds(isc, 1), :]) + mod_ref[pl.ds(ish, 1), :]
        h_ref[...] = h.astype(BF16)

    return pl.pallas_call(
        body, name=name, grid=(T // TM,),
        in_specs=[pl.BlockSpec((TM, D), lambda i: (i, 0)), pl.BlockSpec((1, D), lambda i: (0, 0)),
                  pl.BlockSpec((None, None, 6, D), _mod_index(cfg))],
        out_specs=pl.BlockSpec((TM, D), lambda i: (i, 0)),
        out_shape=jax.ShapeDtypeStruct((T, D), BF16),
        compiler_params=_params(("parallel",)),
    )(x, gain, mod)


def _norm_mod_bwd(cfg, name, x, gain, mod, ish, isc, dh, dres, gate=None, lat_only=False):
    T, D, TM, TPE = cfg.T, cfg.D, cfg.TM, cfg.TPE
    ng = 2 if gate is not None else 0
    dx_spec = (pl.BlockSpec((TM, D), lambda i: ((i // TPE) * (TPE - 1) + jnp.maximum(i % TPE - 1, 0), 0)) if lat_only
               else pl.BlockSpec((TM, D), lambda i: (i, 0)))
    dx_rows = cfg.B * cfg.SL if lat_only else T

    def body(*refs):
        x_ref, g_ref, mod_ref, dh_ref, dres_ref = refs[:5]
        dx_ref, dmod_ref, dgain_ref = refs[5 + ng:8 + ng]
        i = pl.program_id(0)
        t = i % TPE
        x = x_ref[...]
        g = g_ref[...]
        dh = dh_ref[...].astype(F32)
        rstd = lax.rsqrt(jnp.mean(x * x, axis=-1, keepdims=True) + EPS)
        xhat = x * rstd
        dn = dh * (1.0 + mod_ref[pl.ds(isc, 1), :])
        dsh = jnp.sum(dh, axis=0, keepdims=True)
        dsc = jnp.sum(dh * (xhat * g), axis=0, keepdims=True)
        dgain = jnp.sum(dn * xhat, axis=0, keepdims=True)
        dxh = dn * g
        dx = rstd * (dxh - xhat * jnp.mean(dxh * xhat, axis=-1, keepdims=True)) + dres_ref[...]
        dx_ref[...] = dx
        sums = [(dmod_ref.at[pl.ds(0, 1), :], dsh), (dmod_ref.at[pl.ds(1, 1), :], dsc)]
        if ng:
            y_ref, gmod_ref = refs[5:7]
            dy_ref, dgate_ref = refs[8 + ng:]
            dy_ref[...] = (dx * gmod_ref[pl.ds(gate[2], 1), :]).astype(BF16)
            sums.append((dgate_ref, jnp.sum(dx * y_ref[...].astype(F32), axis=0, keepdims=True)))

        @pl.when(t <= 1)
        def _():
            for ref, val in sums:
                ref[...] = val

        @pl.when(t > 1)
        def _():
            for ref, val in sums:
                ref[...] += val

        @pl.when(i == 0)
        def _():
            dgain_ref[...] = dgain

        @pl.when(i > 0)
        def _():
            dgain_ref[...] += dgain

    tok = pl.BlockSpec((TM, D), lambda i: (i, 0))
    mod_spec = pl.BlockSpec((None, None, 6, D), _mod_index(cfg))
    res = pl.pallas_call(
        body, name=name, grid=(T // TM,),
        in_specs=[tok, pl.BlockSpec((1, D), lambda i: (0, 0)), mod_spec, tok, tok] + ([tok, mod_spec] if ng else []),
        out_specs=[dx_spec, pl.BlockSpec((None, None, 2, D), _mod_index(cfg)), pl.BlockSpec((1, D), lambda i: (0, 0))]
        + ([tok, pl.BlockSpec((None, None, 1, D), _mod_index(cfg))] if ng else []),
        out_shape=[jax.ShapeDtypeStruct((dx_rows, D), F32), jax.ShapeDtypeStruct((cfg.B, 2, 2, D), F32),
                   jax.ShapeDtypeStruct((1, D), F32)]
        + ([jax.ShapeDtypeStruct((T, D), BF16), jax.ShapeDtypeStruct((cfg.B, 2, 1, D), F32)] if ng else []),
        compiler_params=_params(("arbitrary",)),
    )(x, gain, mod, dh, dres, *(gate[:2] if ng else ()))
    return res


def _loss_grad(cfg, x, tgt, y, mod, igate):
    T, D, TM, TPE = cfg.T, cfg.D, cfg.TM, cfg.TPE

    def body(x_ref, t_ref, y_ref, mod_ref, dx_ref, loss_ref, dy_ref, dg_ref):
        i = pl.program_id(0)
        t = i % TPE

        @pl.when(i == 0)
        def _():
            loss_ref[...] = jnp.zeros_like(loss_ref)

        @pl.when(t == 0)
        def _():
            dx_ref[...] = jnp.zeros_like(dx_ref)
            dy_ref[...] = jnp.zeros_like(dy_ref)
            dg_ref[...] = jnp.zeros_like(dg_ref)

        @pl.when(t > 0)
        def _():
            err = x_ref[...] - t_ref[...]
            dx = err * (1.0 / D)
            dx_ref[...] = dx
            loss_ref[...] += 0.5 * jnp.sum(jnp.mean(err * err, axis=-1, keepdims=True), axis=0, keepdims=True)
            dy_ref[...] = (dx * mod_ref[pl.ds(igate, 1), :]).astype(BF16)
            dg = jnp.sum(dx * y_ref[...].astype(F32), axis=0, keepdims=True)

            @pl.when(t == 1)
            def _():
                dg_ref[...] = dg

            @pl.when(t > 1)
            def _():
                dg_ref[...] += dg

    tok = pl.BlockSpec((TM, D), lambda i: (i, 0))
    tgt_spec = pl.BlockSpec((TM, D), lambda i: ((i // TPE) * (TPE - 1) + jnp.maximum(i % TPE - 1, 0), 0))
    dx, loss, dy, dg = pl.pallas_call(
        body, name="loss_grad", grid=(T // TM,),
        in_specs=[tok, tgt_spec, tok, pl.BlockSpec((None, None, 6, D), _mod_index(cfg))],
        out_specs=[tok, pl.BlockSpec((8, LANES), lambda i: (0, 0)), tok,
                   pl.BlockSpec((None, None, 1, D), _mod_index(cfg))],
        out_shape=[jax.ShapeDtypeStruct((T, D), F32), jax.ShapeDtypeStruct((8, LANES), F32),
                   jax.ShapeDtypeStruct((T, D), BF16), jax.ShapeDtypeStruct((cfg.B, 2, 1, D), F32)],
        compiler_params=_params(("arbitrary",)),
    )(x, tgt, y, mod)
    return loss[0, 0], dx, dy, dg


def _rope_tables(cfg):
    rows = cfg.SL // cfg.GRID_W
    row = jnp.repeat(jnp.arange(rows, dtype=F32), cfg.GRID_W)
    col = jnp.tile(jnp.arange(cfg.GRID_W, dtype=F32), rows)
    n_freq = CHUNK // 4
    inv = ROPE_BASE ** (-jnp.arange(n_freq, dtype=F32) / n_freq)
    ang = jnp.concatenate([row[:, None] * inv[None, :], col[:, None] * inv[None, :]], axis=-1)
    cos, sin = jnp.cos(ang), jnp.sin(ang)
    cosf = jnp.concatenate([jnp.ones((cfg.SC, CHUNK), F32), jnp.concatenate([cos, cos], axis=-1)], axis=0)
    sinf = jnp.concatenate([jnp.zeros((cfg.SC, CHUNK), F32), jnp.concatenate([-sin, sin], axis=-1)], axis=0)
    return cosf, sinf


def _rope(x, cosf, sinf):
    return x * cosf + pltpu.roll(x, CHUNK // 2, 1) * sinf


def _irope(dy, cosf, sinf):
    return dy * cosf - pltpu.roll(dy, CHUNK // 2, 1) * sinf


def _prep_fwd(cfg, name, p, cosf, sinf, qg, kg):
    T, TM, TPE, H, KV = cfg.T, cfg.TM, cfg.TPE, cfg.H, cfg.KV
    HW = H * CHUNK
    kscale = CHUNK ** -0.5

    def body(p_ref, c_ref, s_ref, qg_ref, kg_ref, rq_ref, rk_ref, aq_ref, ak_ref):
        cosf, sinf = c_ref[...], s_ref[...]

        def normed(x, g):
            return x * lax.rsqrt(jnp.mean(x * x, axis=-1, keepdims=True) + EPS) * g

        def seg(col):
            return p_ref[:, pl.ds(col, CHUNK)].astype(F32)

        for h in range(H):
            sl = pl.ds(h * CHUNK, CHUNK)
            rq_ref[:, sl] = _rope(seg(h * CHUNK), cosf, sinf)
            rk_ref[:, sl] = _rope(seg(HW + h * CHUNK), cosf, sinf) * kscale
            aq_ref[:, sl] = (_rope(normed(seg(4 * HW + h * CHUNK), qg_ref[...]), cosf, sinf) * ATT_SCALE).astype(BF16)
        for h in range(KV):
            ak_ref[:, pl.ds(h * CHUNK, CHUNK)] = _rope(
                normed(seg(5 * HW + h * CHUNK), kg_ref[...]), cosf, sinf).astype(BF16)

    tab = pl.BlockSpec((TM, CHUNK), lambda i: (i % TPE, 0))
    vec = pl.BlockSpec((1, CHUNK), lambda i: (0, 0))
    return pl.pallas_call(
        body, name=name, grid=(T // TM,),
        in_specs=[pl.BlockSpec((TM, cfg.ABW), lambda i: (i, 0)), tab, tab, vec, vec],
        out_specs=[pl.BlockSpec((TM, HW), lambda i: (i, 0))] * 3 + [pl.BlockSpec((TM, KV * CHUNK), lambda i: (i, 0))],
        out_shape=[jax.ShapeDtypeStruct((T, HW), F32), jax.ShapeDtypeStruct((T, HW), F32),
                   jax.ShapeDtypeStruct((T, HW), BF16), jax.ShapeDtypeStruct((T, KV * CHUNK), BF16)],
        compiler_params=_params(("parallel",)),
    )(p, cosf, sinf, qg, kg)


def _prep_bwd(cfg, name, p, cosf, sinf, qg, kg, d_rq, d_rk, d_rv, d_gate, d_aq, d_ak, d_av):
    T, TM, TPE, H, KV = cfg.T, cfg.TM, cfg.TPE, cfg.H, cfg.KV
    HW = H * CHUNK
    kscale = CHUNK ** -0.5

    def body(p_ref, c_ref, s_ref, qg_ref, kg_ref, drq_ref, drk_ref, drv_ref, dgt_ref, daq_ref, dak_ref, dav_ref,
             dp_ref, dqg_ref, dkg_ref):
        i = pl.program_id(0)
        cosf, sinf = c_ref[...], s_ref[...]

        def norm_bwd(x, g, dn):
            rstd = lax.rsqrt(jnp.mean(x * x, axis=-1, keepdims=True) + EPS)
            xhat = x * rstd
            dg = jnp.sum(dn * xhat, axis=0, keepdims=True)
            dxh = dn * g
            return rstd * (dxh - xhat * jnp.mean(dxh * xhat, axis=-1, keepdims=True)), dg

        dqg = jnp.zeros((1, CHUNK), F32)
        dkg = jnp.zeros((1, CHUNK), F32)
        for h in range(H):
            sl = pl.ds(h * CHUNK, CHUNK)
            dp_ref[:, pl.ds(h * CHUNK, CHUNK)] = _irope(drq_ref[:, sl].astype(F32), cosf, sinf).astype(BF16)
            dp_ref[:, pl.ds(HW + h * CHUNK, CHUNK)] = (_irope(drk_ref[:, sl].astype(F32), cosf, sinf)
                                                       * kscale).astype(BF16)
            dp_ref[:, pl.ds(2 * HW + h * CHUNK, CHUNK)] = drv_ref[:, sl].astype(BF16)
            dp_ref[:, pl.ds(3 * HW + h * CHUNK, CHUNK)] = dgt_ref[:, sl].astype(BF16)
            dx, dg = norm_bwd(p_ref[:, pl.ds(4 * HW + h * CHUNK, CHUNK)].astype(F32), qg_ref[...],
                              _irope(daq_ref[:, sl].astype(F32), cosf, sinf))
            dp_ref[:, pl.ds(4 * HW + h * CHUNK, CHUNK)] = dx.astype(BF16)
            dqg = dqg + dg
        for h in range(KV):
            sl = pl.ds(h * CHUNK, CHUNK)
            dx, dg = norm_bwd(p_ref[:, pl.ds(5 * HW + h * CHUNK, CHUNK)].astype(F32), kg_ref[...],
                              _irope(dak_ref[:, sl], cosf, sinf))
            dp_ref[:, pl.ds(5 * HW + h * CHUNK, CHUNK)] = dx.astype(BF16)
            dp_ref[:, pl.ds(5 * HW + (KV + h) * CHUNK, CHUNK)] = dav_ref[:, sl].astype(BF16)
            dkg = dkg + dg

        @pl.when(i == 0)
        def _():
            dqg_ref[...] = dqg
            dkg_ref[...] = dkg

        @pl.when(i > 0)
        def _():
            dqg_ref[...] += dqg
            dkg_ref[...] += dkg

    tab = pl.BlockSpec((TM, CHUNK), lambda i: (i % TPE, 0))
    vec = pl.BlockSpec((1, CHUNK), lambda i: (0, 0))
    hw = pl.BlockSpec((TM, HW), lambda i: (i, 0))
    kvw = pl.BlockSpec((TM, KV * CHUNK), lambda i: (i, 0))
    return pl.pallas_call(
        body, name=name, grid=(T // TM,),
        in_specs=[pl.BlockSpec((TM, cfg.ABW), lambda i: (i, 0)), tab, tab, vec, vec, hw, hw, hw, hw, hw, kvw, kvw],
        out_specs=[pl.BlockSpec((TM, cfg.ABW), lambda i: (i, 0)), vec, vec],
        out_shape=[jax.ShapeDtypeStruct((T, cfg.ABW), BF16), jax.ShapeDtypeStruct((1, CHUNK), F32),
                   jax.ShapeDtypeStruct((1, CHUNK), F32)],
        compiler_params=_params(("arbitrary",)),
    )(p, cosf, sinf, qg, kg, d_rq, d_rk, d_rv, d_gate, d_aq, d_ak, d_av)


def _ret_consts(direction, lg):
    C = CHUNK
    ii = lax.broadcasted_iota(jnp.int32, (C, C), 0)
    jj = lax.broadcasted_iota(jnp.int32, (C, C), 1)
    col = lax.broadcasted_iota(jnp.int32, (C, 1), 0).astype(F32)
    if direction == 0:
        mask, er, ek, eq = ii >= jj, (ii - jj).astype(F32), (C - 1.0) - col, col + 1.0
    else:
        mask, er, ek, eq = jj >= ii, (jj - ii).astype(F32), col, C - col
    er = jnp.where(mask, er, 0.0)
    dm = jnp.where(mask, jnp.exp(er * lg), 0.0)
    return dm, er, jnp.exp(ek * lg), ek, jnp.exp(eq * lg), eq, jnp.exp(C * lg)


def _ret_order(cfg, direction):
    n_all, n_ctx = cfg.S // CHUNK, cfg.SC // CHUNK
    if direction == 0:
        return list(range(n_all))
    return list(range(n_ctx - 1, -1, -1)) + list(range(n_all - 1, n_ctx - 1, -1))


def _carry_begin(carry, c_src, c_dst, sems, step):
    @pl.when(step == 0)
    def _():
        for cp in _carry_copies(carry, c_src, c_dst, *sems):
            cp.start()


def _carry_end(carry, c_src, c_dst, sems, step, n_steps):
    @pl.when(step == n_steps - 1)
    def _():
        for cp in _carry_copies(carry, c_src, c_dst, *sems):
            cp.wait()


def _carry_scratch(nc):
    return [pltpu.SemaphoreType.DMA((3 * nc,)), pltpu.SemaphoreType.DMA((3 * nc,)),
            pltpu.SemaphoreType.DMA((nc,))] if nc else []


def _head_norm_gate(o, g):
    mu = jnp.mean(o, axis=-1, keepdims=True)
    var = jnp.mean(jnp.square(o - mu), axis=-1, keepdims=True)
    rstd = lax.rsqrt(var + EPS)
    y = (o - mu) * rstd
    sg = jax.nn.sigmoid(g)
    return y, rstd, sg


RET_UNROLL = 3


def _retention_fwd(cfg, name, rq, rk, p, lgb, carry=None):
    B, H, S, T = cfg.B, cfg.H, cfg.S, cfg.T
    n_all = S // CHUNK
    nc = len(carry.srcs) if carry is not None else 0

    def body(*refs):
        q_ref, k_ref, v_ref, g_ref, lg_ref = refs[:5]
        c_src = refs[5:5 + nc]
        o_ref, ret_ref, st_ref = refs[5 + nc:8 + nc]
        c_dst = refs[8 + nc:8 + 2 * nc]
        sems = refs[8 + 2 * nc:8 + 2 * nc + 3] if nc else ()
        kv_ref = refs[-1]
        step = pl.program_id(0) * H + pl.program_id(1)
        if nc:
            _carry_begin(carry, c_src, c_dst, sems, step)

        def rows(n):
            return pl.ds(pl.multiple_of(n * CHUNK, CHUNK), CHUNK)

        (dm0, _, kd0, _, qd0, _, cd0), (dm1, _, kd1, _, qd1, _, cd1) = (
            _ret_consts(d, lg_ref[d, 0:1, 0:1]) for d in (0, 1))
        dm_both = dm0 + dm1

        def kv_step(n, c):
            k = k_ref[rows(n), :]
            v = v_ref[rows(n), :].astype(BF16)
            kv_ref[0, n] = _dot_tn((k * kd0).astype(BF16), v)
            kv_ref[1, n] = _dot_tn((k * kd1).astype(BF16), v)
            return c

        lax.fori_loop(0, n_all, kv_step, 0, unroll=RET_UNROLL)
        for direction, cd in ((0, cd0), (1, cd1)):
            st = jnp.zeros((CHUNK, CHUNK), F32)
            for t, n in enumerate(_ret_order(cfg, direction)):
                st_ref[direction, n] = st
                if t + 1 < n_all:
                    st = cd * st + kv_ref[direction, n]

        def out_step(n, c):
            q = q_ref[rows(n), :].astype(BF16)
            v = v_ref[rows(n), :].astype(BF16)
            s = _dot_nt(q, k_ref[rows(n), :].astype(BF16)) * dm_both
            states = jnp.concatenate([st_ref[0, n].astype(BF16), st_ref[1, n].astype(BF16)], axis=1)
            cross = _dot(q, states)
            o = _dot(s.astype(BF16), v) + cross[:, :CHUNK] * qd0 + cross[:, CHUNK:] * qd1
            o_ref[rows(n), :] = o
            g = g_ref[rows(n), :].astype(F32)
            y, _, sg = _head_norm_gate(o, g)
            ret_ref[rows(n), :] = (y * (g * sg)).astype(BF16)
            return c

        lax.fori_loop(0, n_all, out_step, 0, unroll=RET_UNROLL)
        if nc:
            _carry_end(carry, c_src, c_dst, sems, step, B * H)

    HW = H * CHUNK
    blk = lambda off: pl.BlockSpec((S, CHUNK), lambda b, h: (b, off + h))
    st_spec = pl.BlockSpec((None, None, 2, n_all, CHUNK, CHUNK), lambda b, h: (b, h, 0, 0, 0, 0))
    res = pl.pallas_call(
        body, name=name, grid=(B, H),
        in_specs=[blk(0), blk(0), blk(2 * H), blk(3 * H),
                  pl.BlockSpec((None, 2, 8, LANES), lambda b, h: (h, 0, 0, 0))] + [ANY] * nc,
        out_specs=[blk(0), blk(0), st_spec] + [ANY] * nc,
        out_shape=[jax.ShapeDtypeStruct((T, HW), F32), jax.ShapeDtypeStruct((T, 2 * HW), BF16),
                   jax.ShapeDtypeStruct((B, H, 2, n_all, CHUNK, CHUNK), F32)] + (_carry_out_shapes(carry) if nc else []),
        scratch_shapes=_carry_scratch(nc) + [pltpu.VMEM((2, n_all, CHUNK, CHUNK), F32)],
        compiler_params=_params(("arbitrary", "arbitrary") if nc else ("parallel", "parallel")),
    )(rq, rk, p, p, lgb, *(carry.srcs if nc else ()))
    return res[:3], res[3:]


def _retention_bwd(cfg, name, rq, rk, p, o_sum, states, dcat, lgb, carry=None):
    B, H, S, T = cfg.B, cfg.H, cfg.S, cfg.T
    n_all = S // CHUNK
    C = CHUNK
    nc = len(carry.srcs) if carry is not None else 0

    def body(*refs):
        q_ref, k_ref, v_ref, g_ref, o_ref, st_ref, dr_ref, lg_ref = refs[:8]
        c_src = refs[8:8 + nc]
        dq_ref, dk_ref, dv_ref, dg_ref, dlg_ref = refs[8 + nc:13 + nc]
        c_dst = refs[13 + nc:13 + 2 * nc]
        sems = refs[13 + 2 * nc:13 + 2 * nc + 3] if nc else ()
        do_ref, gq_ref, ds_ref, acc_ref = refs[-4:]
        step = pl.program_id(0) * H + pl.program_id(1)
        if nc:
            _carry_begin(carry, c_src, c_dst, sems, step)

        def rows(n):
            return pl.ds(pl.multiple_of(n * C, C), C)

        (dm0, er0, kd0, ek0, qd0, eq0, cd0), (dm1, er1, kd1, ek1, qd1, eq1, cd1) = (
            _ret_consts(d, lg_ref[d, 0:1, 0:1]) for d in (0, 1))
        dm_both = dm0 + dm1
        wdm0, wdm1 = dm0 * er0, dm1 * er1

        def side(a, b):
            return jnp.concatenate([a.astype(BF16), b.astype(BF16)], axis=1)

        def gq_step(n, c):
            g = g_ref[rows(n), :].astype(F32)
            dr = dr_ref[rows(n), :].astype(F32)
            y, rstd, sg = _head_norm_gate(o_ref[rows(n), :], g)
            dy = dr * (g * sg)
            dg_ref[rows(n), :] = (dr * y * (sg * (1.0 + g * (1.0 - sg)))).astype(BF16)
            do = rstd * (dy - jnp.mean(dy, axis=-1, keepdims=True) - y * jnp.mean(dy * y, axis=-1, keepdims=True))
            do_ref[rows(n), :] = do
            gq = _dot_tn(q_ref[rows(n), :].astype(BF16), side(do * qd0, do * qd1))
            gq_ref[0, n] = gq[:, :C]
            gq_ref[1, n] = gq[:, C:]
            return c

        lax.fori_loop(0, n_all, gq_step, 0, unroll=RET_UNROLL)
        for direction, cd in ((0, cd0), (1, cd1)):
            order = _ret_order(cfg, direction)
            ds = jnp.zeros((C, C), F32)
            for t in reversed(range(n_all)):
                ds_ref[direction, order[t]] = ds
                if t > 0:
                    ds = cd * ds + gq_ref[direction, order[t]]
        acc_ref[...] = jnp.zeros_like(acc_ref)

        def chunk_step(n, c):
            q = q_ref[rows(n), :].astype(BF16)
            kf = k_ref[rows(n), :]
            k = kf.astype(BF16)
            v = v_ref[rows(n), :].astype(BF16)
            do = do_ref[rows(n), :]
            dob = do.astype(BF16)
            sp0, sp1 = st_ref[0, n], st_ref[1, n]
            ds0, ds1 = ds_ref[0, n], ds_ref[1, n]
            states = side(sp0, sp1)
            dstates = jnp.concatenate([ds0.astype(BF16), ds1.astype(BF16)], axis=0)
            doq0, doq1 = do * qd0, do * qd1
            doq = side(doq0, doq1)
            s_raw = _dot_nt(q, k)
            dpm = _dot_nt(dob, v)
            dsr = (dpm * dm_both).astype(BF16)
            dks = _dot_nt(v, dstates)
            dks0, dks1 = dks[:, :C] * kd0, dks[:, C:] * kd1
            qs = _dot(q, states)
            dq_ref[rows(n), :] = (_dot(dsr, k) + _dot_nt(doq, states)).astype(BF16)
            dk_ref[rows(n), :] = (_dot_tn(dsr, q) + dks0 + dks1).astype(BF16)
            dv_ref[rows(n), :] = (_dot_tn((s_raw * dm_both).astype(BF16), dob)
                                  + _dot(side(kf * kd0, kf * kd1), dstates)).astype(BF16)
            inner = dpm * s_raw
            acc_ref[0] += (jnp.sum(inner * wdm0, axis=0, keepdims=True)
                           + jnp.sum(eq0 * doq0 * qs[:, :C], axis=0, keepdims=True)
                           + jnp.sum(ek0 * kf * dks0, axis=0, keepdims=True)
                           + (C * cd0) * jnp.sum(ds0 * sp0, axis=0, keepdims=True))
            acc_ref[1] += (jnp.sum(inner * wdm1, axis=0, keepdims=True)
                           + jnp.sum(eq1 * doq1 * qs[:, C:], axis=0, keepdims=True)
                           + jnp.sum(ek1 * kf * dks1, axis=0, keepdims=True)
                           + (C * cd1) * jnp.sum(ds1 * sp1, axis=0, keepdims=True))
            return c

        lax.fori_loop(0, n_all, chunk_step, 0, unroll=RET_UNROLL)
        for direction in (0, 1):
            dlg_ref[direction] = jnp.broadcast_to(jnp.sum(acc_ref[direction], axis=1, keepdims=True), (8, LANES))
        if nc:
            _carry_end(carry, c_src, c_dst, sems, step, B * H)

    HW = H * CHUNK
    blk = lambda off: pl.BlockSpec((S, CHUNK), lambda b, h: (b, off + h))
    st_spec = pl.BlockSpec((None, None, 2, n_all, C, C), lambda b, h: (b, h, 0, 0, 0, 0))
    res = pl.pallas_call(
        body, name=name, grid=(B, H),
        in_specs=[blk(0), blk(0), blk(2 * H), blk(3 * H), blk(0), st_spec, blk(0),
                  pl.BlockSpec((None, 2, 8, LANES), lambda b, h: (h, 0, 0, 0))] + [ANY] * nc,
        out_specs=[blk(0)] * 4 + [pl.BlockSpec((None, None, 2, 8, LANES), lambda b, h: (b, h, 0, 0, 0))] + [ANY] * nc,
        out_shape=[jax.ShapeDtypeStruct((T, HW), BF16)] * 4 + [jax.ShapeDtypeStruct((B, H, 2, 8, LANES), F32)]
        + (_carry_out_shapes(carry) if nc else []),
        scratch_shapes=_carry_scratch(nc) + [pltpu.VMEM((S, CHUNK), F32), pltpu.VMEM((2, n_all, C, C), F32),
                                             pltpu.VMEM((2, n_all, C, C), F32), pltpu.VMEM((2, 1, C), F32)],
        compiler_params=_params(("arbitrary", "arbitrary") if nc else ("parallel", "parallel")),
    )(rq, rk, p, p, o_sum, states, dcat, lgb, *(carry.srcs if nc else ()))
    return res[:5], res[5:]


ATT_SCALE = CHUNK ** -0.5


def _attn_scores(cfg, q, k, t):
    kcol = lax.broadcasted_iota(jnp.int32, (1, cfg.S), 1)
    bias = jnp.where(jnp.logical_or(t > 0, kcol < cfg.SC), 0.0, -1e30)
    return _dot_nt(q, k) + bias


def _attention_fwd(cfg, name, aq, ak, p, cat, carry=None):
    B, H, KV, S, T, TM, TPE = cfg.B, cfg.H, cfg.KV, cfg.S, cfg.T, cfg.TM, cfg.TPE
    G = H // KV
    v_off = (5 * H + KV)
    nc = len(carry.srcs) if carry is not None else 0

    def body(*refs):
        q_ref, k_ref, v_ref = refs[:3]
        o_ref, lse_ref = refs[4 + nc:6 + nc]
        c_src, c_dst, sems = refs[4:4 + nc], refs[6 + nc:6 + 2 * nc], refs[6 + 2 * nc:]
        step = (pl.program_id(0) * KV + pl.program_id(1)) * TPE + pl.program_id(2)
        if nc:
            _carry_begin(carry, c_src, c_dst, sems, step)
        k = k_ref[...]
        v = v_ref[...].astype(BF16)
        for g in range(G):
            cols = pl.ds(g * CHUNK, CHUNK)
            s = _attn_scores(cfg, q_ref[:, cols], k, pl.program_id(2))
            m = jnp.max(s, axis=-1, keepdims=True)
            e = jnp.exp(s - m)
            total = jnp.sum(e, axis=-1, keepdims=True)
            o_ref[:, cols] = (_dot(e.astype(BF16), v) * (1.0 / total)).astype(BF16)
            lse_ref[g] = m + jnp.log(total)
        if nc:
            _carry_end(carry, c_src, c_dst, sems, step, B * KV * TPE)

    res = pl.pallas_call(
        body, name=name, grid=(B, KV, TPE),
        in_specs=[pl.BlockSpec((TM, G * CHUNK), lambda b, kv, t: (b * TPE + t, kv)),
                  pl.BlockSpec((S, CHUNK), lambda b, kv, t: (b, kv)),
                  pl.BlockSpec((S, CHUNK), lambda b, kv, t: (b, v_off + kv)), ANY] + [ANY] * nc,
        out_specs=[pl.BlockSpec((TM, G * CHUNK), lambda b, kv, t: (b * TPE + t, KV + kv)),
                   pl.BlockSpec((G, TM, 1), lambda b, kv, t: (kv, b * TPE + t, 0))] + [ANY] * nc,
        out_shape=[jax.ShapeDtypeStruct(cat.shape, cat.dtype), jax.ShapeDtypeStruct((H, T, 1), F32)]
        + (_carry_out_shapes(carry) if nc else []),
        input_output_aliases={3: 0},
        scratch_shapes=_carry_scratch(nc),
        compiler_params=_params(("arbitrary",) * 3 if nc else ("parallel",) * 3),
    )(aq, ak, p, cat, *(carry.srcs if nc else ()))
    return res[:2], res[2:]


def _attention_bwd(cfg, name, aq, ak, p, cat, lse, dcat, carry=None):
    B, H, KV, S, T, TM, TPE = cfg.B, cfg.H, cfg.KV, cfg.S, cfg.T, cfg.TM, cfg.TPE
    G = H // KV
    v_off = (5 * H + KV)
    nc = len(carry.srcs) if carry is not None else 0

    def body(*refs):
        q_ref, k_ref, v_ref, o_ref, lse_ref, do_ref = refs[:6]
        dq_ref, dk_ref, dv_ref = refs[6 + nc:9 + nc]
        c_src, c_dst, sems = refs[6:6 + nc], refs[9 + nc:9 + 2 * nc], refs[9 + 2 * nc:]
        t = pl.program_id(2)
        step = (pl.program_id(0) * KV + pl.program_id(1)) * TPE + t
        if nc:
            _carry_begin(carry, c_src, c_dst, sems, step)
        k = k_ref[...]
        v = v_ref[...].astype(BF16)
        dk = dv = None
        for g in range(G):
            cols = pl.ds(g * CHUNK, CHUNK)
            q = q_ref[:, cols]
            do = do_ref[:, cols]
            pr = jnp.exp(_attn_scores(cfg, q, k, t) - lse_ref[g])
            delta = jnp.sum(do.astype(F32) * o_ref[:, cols].astype(F32), axis=-1, keepdims=True)
            ds = (pr * (_dot_nt(do, v) - delta)).astype(BF16)
            dq_ref[:, cols] = (_dot(ds, k) * ATT_SCALE).astype(BF16)
            dk_g, dv_g = _dot_tn(ds, q), _dot_tn(pr.astype(BF16), do)
            dk, dv = (dk_g, dv_g) if dk is None else (dk + dk_g, dv + dv_g)

        @pl.when(t == 0)
        def _():
            dk_ref[...] = dk
            dv_ref[...] = dv

        @pl.when(t > 0)
        def _():
            dk_ref[...] += dk
            dv_ref[...] += dv

        if nc:
            _carry_end(carry, c_src, c_dst, sems, step, B * KV * TPE)

    qspec = pl.BlockSpec((TM, G * CHUNK), lambda b, kv, t: (b * TPE + t, kv))
    kvspec = pl.BlockSpec((S, CHUNK), lambda b, kv, t: (b, kv))
    right = pl.BlockSpec((TM, G * CHUNK), lambda b, kv, t: (b * TPE + t, KV + kv))
    res = pl.pallas_call(
        body, name=name, grid=(B, KV, TPE),
        in_specs=[qspec, kvspec, pl.BlockSpec((S, CHUNK), lambda b, kv, t: (b, v_off + kv)), right,
                  pl.BlockSpec((G, TM, 1), lambda b, kv, t: (kv, b * TPE + t, 0)), right] + [ANY] * nc,
        out_specs=[qspec, kvspec, kvspec] + [ANY] * nc,
        out_shape=[jax.ShapeDtypeStruct((T, H * CHUNK), BF16), jax.ShapeDtypeStruct((T, KV * CHUNK), F32),
                   jax.ShapeDtypeStruct((T, KV * CHUNK), F32)] + (_carry_out_shapes(carry) if nc else []),
        scratch_shapes=_carry_scratch(nc),
        compiler_params=_params(("arbitrary",) * 3 if nc else ("parallel", "parallel", "arbitrary")),
    )(aq, ak, p, cat, lse, dcat, *(carry.srcs if nc else ()))
    return res[:3], res[3:]


_GELU_C = math.sqrt(2.0 / math.pi)


def _gelu(x):
    return 0.5 * x * (1.0 + jnp.tanh(_GELU_C * (x + 0.044715 * x * x * x)))


def _gelu_and_grad(x):
    x2 = x * x
    th = jnp.tanh(_GELU_C * (x + 0.044715 * x * x2))
    half = 0.5 * (1.0 + th)
    return x * half, half + 0.5 * x * (1.0 - th * th) * _GELU_C * (1.0 + 3.0 * 0.044715 * x2)


def _cm_fwd(cfg, name, a, vg, ws, bs):
    T, TM, W, NG = cfg.T, cfg.TM, cfg.CMW, cfg.CMG

    def body(a_ref, vg_ref, ws_ref, bs_ref, m_ref):
        v = _gelu(a_ref[:, pl.ds(W, W)].astype(F32))
        vn = (v * lax.rsqrt(jnp.mean(v * v, axis=-1, keepdims=True) + EPS) * vg_ref[...]).astype(BF16)
        for c in range(TM // CHUNK):
            for g in range(NG):
                rows, cols = slice(c * CHUNK, (c + 1) * CHUNK), slice(g * CHUNK, (g + 1) * CHUNK)
                sv = _dot(ws_ref[g].astype(BF16), vn[rows, cols]) + bs_ref[g]
                u = _gelu(a_ref[pl.ds(c * CHUNK, CHUNK), pl.ds(g * CHUNK, CHUNK)].astype(F32))
                m_ref[pl.ds(c * CHUNK, CHUNK), pl.ds(g * CHUNK, CHUNK)] = (u * sv).astype(BF16)

    return pl.pallas_call(
        body, name=name, grid=(T // TM,),
        in_specs=[pl.BlockSpec((TM, 2 * W), lambda i: (i, 0)), pl.BlockSpec((1, W), lambda i: (0, 0)),
                  pl.BlockSpec((NG, CHUNK, CHUNK), lambda i: (0, 0, 0)),
                  pl.BlockSpec((NG, CHUNK, 1), lambda i: (0, 0, 0))],
        out_specs=pl.BlockSpec((TM, W), lambda i: (i, 0)),
        out_shape=jax.ShapeDtypeStruct((T, W), BF16),
        compiler_params=_params(("parallel",)),
    )(a, vg, ws, bs)


def _cm_bwd(cfg, name, a, vg, ws, bs, dm):
    T, TM, W, NG = cfg.T, cfg.TM, cfg.CMW, cfg.CMG

    def body(a_ref, vg_ref, ws_ref, bs_ref, dm_ref, da_ref, dws_ref, dbs_ref, dvg_ref, dvn_ref):
        i = pl.program_id(0)

        @pl.when(i == 0)
        def _():
            dws_ref[...] = jnp.zeros_like(dws_ref)
            dbs_ref[...] = jnp.zeros_like(dbs_ref)
            dvg_ref[...] = jnp.zeros_like(dvg_ref)

        v, v_grad = _gelu_and_grad(a_ref[:, pl.ds(W, W)].astype(F32))
        rstd = lax.rsqrt(jnp.mean(v * v, axis=-1, keepdims=True) + EPS)
        xhat = v * rstd
        vg = vg_ref[...]
        vn = (xhat * vg).astype(BF16)
        for c in range(TM // CHUNK):
            for g in range(NG):
                rows, cols = slice(c * CHUNK, (c + 1) * CHUNK), slice(g * CHUNK, (g + 1) * CHUNK)
                rs, cs = pl.ds(c * CHUNK, CHUNK), pl.ds(g * CHUNK, CHUNK)
                wsb = ws_ref[g].astype(BF16)
                blk = vn[rows, cols]
                sv = _dot(wsb, blk) + bs_ref[g]
                u, u_grad = _gelu_and_grad(a_ref[rs, cs].astype(F32))
                dmb = dm_ref[rs, cs].astype(F32)
                da_ref[rs, cs] = (dmb * sv * u_grad).astype(BF16)
                dsv = dmb * u
                dsvb = dsv.astype(BF16)
                dbs_ref[g] += jnp.sum(dsv, axis=1, keepdims=True)
                dws_ref[g] += _dot_nt(dsvb, blk)
                dvn_ref[rs, cs] = _dot_tn(wsb, dsvb)
        dvn = dvn_ref[...]
        dvg_ref[...] += jnp.sum(dvn * xhat, axis=0, keepdims=True)
        dxh = dvn * vg
        dv = rstd * (dxh - xhat * jnp.mean(dxh * xhat, axis=-1, keepdims=True))
        da_ref[:, pl.ds(W, W)] = (dv * v_grad).astype(BF16)

    return pl.pallas_call(
        body, name=name, grid=(T // TM,),
        in_specs=[pl.BlockSpec((TM, 2 * W), lambda i: (i, 0)), pl.BlockSpec((1, W), lambda i: (0, 0)),
                  pl.BlockSpec((NG, CHUNK, CHUNK), lambda i: (0, 0, 0)),
                  pl.BlockSpec((NG, CHUNK, 1), lambda i: (0, 0, 0)), pl.BlockSpec((TM, W), lambda i: (i, 0))],
        out_specs=[pl.BlockSpec((TM, 2 * W), lambda i: (i, 0)), pl.BlockSpec((NG, CHUNK, CHUNK), lambda i: (0, 0, 0)),
                   pl.BlockSpec((NG, CHUNK, 1), lambda i: (0, 0, 0)), pl.BlockSpec((1, W), lambda i: (0, 0))],
        out_shape=[jax.ShapeDtypeStruct((T, 2 * W), BF16), jax.ShapeDtypeStruct((NG, CHUNK, CHUNK), F32),
                   jax.ShapeDtypeStruct((NG, CHUNK, 1), F32), jax.ShapeDtypeStruct((1, W), F32)],
        scratch_shapes=[pltpu.VMEM((TM, W), F32)],
        compiler_params=_params(("arbitrary",)),
    )(a, vg, ws, bs, dm)


def _layer_weights(l):
    mixer = ("ab_w_in", "ab_w_out") if l % 2 == 0 else ("cm_w_in", "cm_w_out")
    return [(mixer[0], l // 2), (mixer[1], l // 2), ("ff_w1", l), ("ff_w2", l)]


def _local_step(cfg, xcat, tgt, mods, shards, w, first_weight):
    D, TM, H = cfg.D, cfg.TM, cfg.H
    cosf, sinf = _rope_tables(cfg)
    full, big, recv = {}, {}, {}

    def gather_of(keys):
        return _Carry("gather", tuple(shards[n][i] for n, i in keys), tuple(BIG[n] for n, _ in keys))

    def exchange_of(keys):
        return _Carry("exchange", tuple(big[k] for k in keys), tuple(BIG[n] for n, _ in keys))

    def mm(pending, name, a, b, **kw):
        if not pending:
            return _mm(name, a, b, **kw)
        key, carry, sink = pending.pop(0)
        out, (got,) = _mm(name, a, b, carry=carry, **kw)
        sink[key] = got
        return out

    def with_carry(call, keys, make, sink):
        out, got = call(carry=make(keys) if keys else None)
        sink.update(zip(keys, got))
        return out

    keys0 = _layer_weights(0)
    full[keys0[0]] = first_weight
    TG = 3 * TM if cfg.TPE % 3 == 0 else TM
    tiles_per_ex = cfg.S // TG
    gate_spec = pl.BlockSpec((None, 2, 6, D), lambda i, j, k: (i // tiles_per_ex, 0, 0, 0))

    def resid_epi(igate, nxt):
        def epi(acc, row_tile, x_ref, mod_ref, *nxt_refs):
            row = lax.broadcasted_iota(jnp.int32, (TG, 1), 0)
            is_ctx = jnp.logical_and(row_tile % tiles_per_ex == 0, row < cfg.SC)

            def pick(ref, idx):
                return jnp.where(is_ctx, ref[0, pl.ds(idx, 1), :], ref[1, pl.ds(idx, 1), :])

            x = x_ref[...] + pick(mod_ref, igate) * acc
            if nxt is None:
                return x, acc
            gain_ref, modn_ref = nxt_refs
            n = x * lax.rsqrt(jnp.mean(x * x, axis=-1, keepdims=True) + EPS) * gain_ref[...]
            return x, acc, n * (1.0 + pick(modn_ref, nxt[3])) + pick(modn_ref, nxt[2])
        return epi

    def gated_out(pending, name, a, key, x, mod, igate, nxt=None):
        extras = [(x, pl.BlockSpec((TG, D), lambda i, j, k: (i, j))), (mod, gate_spec)]
        if nxt is not None:
            extras += [(nxt[0], pl.BlockSpec((1, D), lambda i, j, k: (0, 0))), (nxt[1], gate_spec)]
        return mm(pending, name, a, full[key], mode="nn", tm=TG, tn=D, outs=[F32, BF16] + [BF16] * (nxt is not None),
                  epi=resid_epi(igate, nxt), extras=extras, resident=True)

    saved = []
    x = xcat
    h = _norm_mod_fwd(cfg, "norm1_fwd_0", x, w["norm1_g"][0][None], mods[0], 0, 1)
    for l in range(N_LAYERS):
        li = l // 2
        mod = mods[l]
        k_in, k_out, k_ff1, k_ff2 = _layer_weights(l)
        pend = [(k, gather_of([k]), full) for k in _layer_weights(l + 1)] if l + 1 < N_LAYERS else []
        norm2 = (w["norm2_g"][l][None], mod, 3, 4)
        s = {"x0": x, "h": h}
        if l % 2 == 0:
            lgb = jnp.broadcast_to(jax.nn.log_sigmoid(w["ret_decay"][li]).T[:, :, None, None], (H, 2, 8, LANES))
            qg, kg = w["att_q_norm_g"][li][None], w["att_k_norm_g"][li][None]
            s["p"] = mm(pend, f"ab_in_{l}", s["h"], full[k_in], mode="nn", outs=[BF16], tn=768)
            s["rq"], s["rk"], s["aq"], s["ak"] = _prep_fwd(cfg, f"prep_fwd_{l}", s["p"], cosf, sinf, qg, kg)
            s["o"], ret, s["st"] = with_carry(
                functools.partial(_retention_fwd, cfg, f"ret_fwd_{l}", s["rq"], s["rk"], s["p"], lgb),
                keys0[1:3] if l == 0 else [], gather_of, full)
            s["cat"], s["lse"] = with_carry(
                functools.partial(_attention_fwd, cfg, f"att_fwd_{l}", s["aq"], s["ak"], s["p"], ret),
                keys0[3:] if l == 0 else [], gather_of, full)
            s["lgb"], s["qg"], s["kg"] = lgb, qg, kg
            x, s["y1"], s["h2"] = gated_out(pend, f"ab_out_{l}", s["cat"], k_out, x, mod, 2, norm2)
        else:
            s["a"] = mm(pend, f"cm_in_{l}", s["h"], full[k_in], mode="nn", outs=[BF16])
            s["vg"], s["ws"], s["bs"] = w["cm_v_norm_g"][li][None], w["cm_w_s"][li], w["cm_b_s"][li][:, :, None]
            s["m"] = _cm_fwd(cfg, f"cm_fwd_{l}", s["a"], s["vg"], s["ws"], s["bs"])
            x, s["y1"], s["h2"] = gated_out(pend, f"cm_out_{l}", s["m"], k_out, x, mod, 2, norm2)
        s["x1"] = x
        s["r"] = mm(pend, f"ff1_{l}", s["h2"], full[k_ff1], mode="nn", outs=[BF16],
                    epi=lambda acc, row_tile: (jnp.square(jnp.maximum(acc, 0.0)),))
        if l + 1 < N_LAYERS:
            x, s["y2"], h = gated_out(pend, f"ff2_{l}", s["r"], k_ff2, x, mod, 5,
                                      (w["norm1_g"][l + 1][None], mods[l + 1], 0, 1))
        else:
            x, s["y2"] = gated_out(pend, f"ff2_{l}", s["r"], k_ff2, x, mod, 5)
        saved.append(s)

    loss, dx, dy2, dg2 = _loss_grad(cfg, x, tgt, saved[-1]["y2"], mods[-1], 5)

    small = {k: [None] * n for k, n in (("norm1_g", 4), ("norm2_g", 4), ("ret_lg", 2), ("att_q_norm_g", 2),
                                        ("att_k_norm_g", 2), ("cm_v_norm_g", 2), ("cm_w_s", 2), ("cm_b_s", 2))}
    dmods = [None] * N_LAYERS

    for l in reversed(range(N_LAYERS)):
        li = l // 2
        s, mod = saved[l], mods[l]
        k_in, k_out, k_ff1, k_ff2 = _layer_weights(l)
        above = _layer_weights(l + 1) if l + 1 < N_LAYERS else [None] * 4
        da2, big[k_ff2] = with_carry(functools.partial(_ff_bwd, f"ff2_bwd_{l}", dy2, s["r"], full[k_ff2], "w2"),
                                     [above[3], above[1]] if above[0] else [], exchange_of, recv)
        dh2, big[k_ff1] = with_carry(functools.partial(_ff_bwd, f"ff1_bwd_{l}", s["h2"], da2, full[k_ff1], "w1"),
                                     [above[2], above[0]] if above[0] else [], exchange_of, recv)
        dx, dm2, small["norm2_g"][l], do, dg1 = _norm_mod_bwd(
            cfg, f"norm2_bwd_{l}", s["x1"], w["norm2_g"][l][None], mod, 3, 4, dh2, dx, gate=(s["y1"], mod, 2))
        if l % 2 == 0:
            big[k_out] = _mm(f"ab_out_dw_{l}", s["cat"], do, mode="tn", outs=[BF16])
            dcat = _mm(f"ab_out_dx_{l}", do, full[k_out], mode="nt", outs=[BF16])
            d_rq, d_rk, d_rv, d_gt, dlg = with_carry(
                functools.partial(_retention_bwd, cfg, f"ret_bwd_{l}", s["rq"], s["rk"], s["p"], s["o"], s["st"], dcat,
                                  s["lgb"]), [k_ff2, k_ff1] if l == 0 else [], exchange_of, recv)
            d_aq, d_ak, d_av = with_carry(
                functools.partial(_attention_bwd, cfg, f"att_bwd_{l}", s["aq"], s["ak"], s["p"], s["cat"], s["lse"], dcat),
                [k_out] if l == 0 else [], exchange_of, recv)
            dp, dqg, dkg = _prep_bwd(cfg, f"prep_bwd_{l}", s["p"], cosf, sinf, s["qg"], s["kg"],
                                     d_rq, d_rk, d_rv, d_gt, d_aq, d_ak, d_av)
            small["ret_lg"][li] = jnp.sum(dlg[:, :, :, 0, 0], axis=0).T
            small["att_q_norm_g"][li], small["att_k_norm_g"][li] = dqg[0], dkg[0]
            big[k_in] = _mm(f"ab_in_dw_{l}", s["h"], dp, mode="tn", outs=[BF16])
            last = [(k_in, exchange_of([k_in]), recv)] if l == 0 else []
            dh = mm(last, f"ab_in_dx_{l}", dp, full[k_in], mode="nt", outs=[BF16], resident=True)
        else:
            big[k_out] = _mm(f"cm_out_dw_{l}", s["m"], do, mode="tn", outs=[BF16])
            dm = _mm(f"cm_out_dx_{l}", do, full[k_out], mode="nt", outs=[BF16])
            da, dws, dbs, dvg = _cm_bwd(cfg, f"cm_bwd_{l}", s["a"], s["vg"], s["ws"], s["bs"], dm)
            small["cm_w_s"][li], small["cm_b_s"][li], small["cm_v_norm_g"][li] = dws, dbs[:, :, 0], dvg[0]
            big[k_in] = _mm(f"cm_in_dw_{l}", s["h"], da, mode="tn", outs=[BF16])
            dh = _mm(f"cm_in_dx_{l}", da, full[k_in], mode="nt", outs=[BF16], resident=True)
        below = (saved[l - 1]["y2"], mods[l - 1], 5) if l > 0 else None
        dx, dm1, small["norm1_g"][l], *rest = _norm_mod_bwd(
            cfg, f"norm1_bwd_{l}", s["x0"], w["norm1_g"][l][None], mod, 0, 1, dh, dx, gate=below, lat_only=l == 0)
        dmods[l] = jnp.concatenate([dm1, dg1, dm2, dg2], axis=2)
        if l > 0:
            dy2, dg2 = rest
    return loss, dx, recv, small, dmods


N_DEV = 8
N_CHIP = 4
MESH = pl.DeviceIdType.MESH
ANY = pl.BlockSpec(memory_space=pl.ANY)
BIG = {"ab_w_in": 1, "ab_w_out": 0, "cm_w_in": 1, "cm_w_out": 0, "ff_w1": 1, "ff_w2": 0}


class _Carry(NamedTuple):
    kind: str
    srcs: tuple
    axes: tuple


def _place():
    x, y, c = lax.axis_index("x"), lax.axis_index("y"), lax.axis_index("c")
    return x, y, c, [(1 - x, y), (x, 1 - y), (1 - x, 1 - y)]


def _shard_of(ref, axis, s, width):
    start = pl.multiple_of(s * width, LANES)
    if axis == 0:
        return ref.at[pl.ds(start, width), :]
    return ref.at[:, pl.ds(start, width)]


def _carry_out_shapes(carry):
    shapes = []
    for src, axis in zip(carry.srcs, carry.axes):
        shape = list(src.shape)
        if carry.kind == "swap":
            pass
        elif carry.kind == "gather":
            shape[axis] *= N_CHIP
        else:
            shape[axis] //= N_CHIP
            shape = [N_CHIP] + shape
        shapes.append(jax.ShapeDtypeStruct(tuple(shape), src.dtype))
    return shapes


def _carry_copies(carry, srcs, dsts, send_sems, recv_sems, local_sems):
    x, y, c, chips = _place()
    me = 2 * x + y
    copies = []
    if carry.kind == "swap":
        return [pltpu.make_async_remote_copy(
            src_ref=srcs[t], dst_ref=dsts[t], send_sem=send_sems.at[3 * t], recv_sem=recv_sems.at[3 * t],
            device_id=(x, y, 1 - c), device_id_type=MESH) for t in range(len(srcs))]
    for t, axis in enumerate(carry.axes):
        if carry.kind == "gather":
            own = _shard_of(dsts[t], axis, me, srcs[t].shape[axis])
            copies.append(pltpu.make_async_copy(srcs[t], own, local_sems.at[t]))
            parts = [(srcs[t], own)] * 3
        else:
            width = dsts[t].shape[1 + axis]
            copies.append(pltpu.make_async_copy(_shard_of(srcs[t], axis, me, width), dsts[t].at[3], local_sems.at[t]))
            parts = [(_shard_of(srcs[t], axis, 2 * px + py, width), dsts[t].at[j]) for j, (px, py) in enumerate(chips)]
        for j, (px, py) in enumerate(chips):
            copies.append(pltpu.make_async_remote_copy(
                src_ref=parts[j][0], dst_ref=parts[j][1], send_sem=send_sems.at[3 * t + j],
                recv_sem=recv_sems.at[3 * t + j], device_id=(px, py, c), device_id_type=MESH))
    return copies


def _allgather8(name, block, carry=None):
    m_per, n = block.shape
    nc = len(carry.srcs) if carry is not None else 0

    def body(*refs):
        x_ref, out_ref = refs[0], refs[1 + nc]
        send_sems, recv_sems, local_sem = refs[2 + 2 * nc:5 + 2 * nc]
        carried = _carry_copies(carry, refs[1:1 + nc], refs[2 + nc:2 + 2 * nc], *refs[5 + 2 * nc:]) if nc else []
        for cp in carried:
            cp.start()
        x, y, c, chips = _place()
        me, sibling = (x, y, c), (x, y, 1 - c)

        def rows(px, py, pc):
            return out_ref.at[pl.ds((4 * px + 2 * py + pc) * m_per, m_per), :]

        def copy(k, blk, to, src=None):
            return pltpu.make_async_remote_copy(
                src_ref=rows(*blk) if src is None else src, dst_ref=rows(*blk),
                send_sem=send_sems.at[k], recv_sem=recv_sems.at[k], device_id=to, device_id_type=MESH)

        mine = pltpu.make_async_copy(x_ref, rows(*me), local_sem)
        mine.start()
        first = [copy(0, me, sibling, src=x_ref)]
        first += [copy(1 + j, me, (*chip, c), src=x_ref) for j, chip in enumerate(chips)]
        for cp in first:
            cp.start()
        passed = [copy(4 + j, (*chip, c), sibling) for j, chip in enumerate(chips)]
        for j, chip in enumerate(chips):
            copy(1 + j, (*chip, c), me).wait_recv()
            passed[j].start()
        copy(0, sibling, me).wait_recv()
        for j, chip in enumerate(chips):
            copy(4 + j, (*chip, 1 - c), me).wait_recv()
        for cp in first + passed:
            cp.wait_send()
        mine.wait()
        for cp in carried:
            cp.wait()

    res = pl.pallas_call(
        body, name=name,
        out_shape=[jax.ShapeDtypeStruct((N_DEV * m_per, n), block.dtype)] + (_carry_out_shapes(carry) if nc else []),
        in_specs=[pl.BlockSpec(memory_space=pltpu.VMEM)] + [ANY] * nc,
        out_specs=[pl.BlockSpec(memory_space=pltpu.VMEM)] + [ANY] * nc,
        scratch_shapes=[pltpu.SemaphoreType.DMA((7,)), pltpu.SemaphoreType.DMA((7,)), pltpu.SemaphoreType.DMA]
        + _carry_scratch(nc),
        compiler_params=pltpu.CompilerParams(vmem_limit_bytes=VMEM_LIMIT),
    )(block, *(carry.srcs if nc else ()))
    return (res[0], res[1:]) if nc else res[0]


def _rows_view(a):
    if a.ndim == 1:
        return a.reshape(1, a.shape[0])
    return a.reshape(-1, a.shape[-1])


def _row_tile(rows, cols, target_elems=1 << 17):
    tr = rows
    while tr % 16 == 0 and tr * cols > target_elems:
        tr //= 2
    return tr


def _sum_leading(name, a):
    n, rows, cols = a.shape
    tr = _row_tile(rows, cols * n, target_elems=1 << 20)

    def body(a_ref, o_ref):
        acc = a_ref[0].astype(F32)
        for i in range(1, n):
            acc = acc + a_ref[i].astype(F32)
        o_ref[...] = acc

    return pl.pallas_call(
        body, name=name, grid=(rows // tr,),
        in_specs=[pl.BlockSpec((n, tr, cols), lambda i: (0, i, 0))],
        out_specs=pl.BlockSpec((tr, cols), lambda i: (i, 0)),
        out_shape=jax.ShapeDtypeStruct((rows, cols), F32),
        compiler_params=_params(("parallel",)),
    )(a)


def _sum_parts_layers(name, parts):
    n_layers = len(parts)
    n, rows, cols = parts[0].shape
    tr = _row_tile(rows, cols * n, target_elems=1 << 20)

    def body(*refs):
        o_ref = refs[n_layers]
        layer = pl.program_id(0)
        for k in range(n_layers):
            @pl.when(layer == k)
            def _():
                acc = refs[k][0].astype(F32)
                for i in range(1, n):
                    acc = acc + refs[k][i].astype(F32)
                o_ref[...] = acc

    def in_spec(k):
        return pl.BlockSpec((n, tr, cols), lambda l, i: (0, jnp.where(l == k, i, 0), 0))

    return pl.pallas_call(
        body, name=name, grid=(n_layers, rows // tr),
        in_specs=[in_spec(k) for k in range(n_layers)],
        out_specs=pl.BlockSpec((None, tr, cols), lambda l, i: (l, i, 0)),
        out_shape=jax.ShapeDtypeStruct((n_layers, rows, cols), F32),
        compiler_params=_params(("arbitrary", "arbitrary")),
    )(*parts)


def _silu_rows(name, x):
    def body(x_ref, o_ref):
        v = x_ref[...]
        o_ref[...] = v * jax.nn.sigmoid(v)

    return pl.pallas_call(body, name=name, out_shape=jax.ShapeDtypeStruct(x.shape, F32))(x)


def _silu_bwd_rows(name, x, dy):
    def body(x_ref, dy_ref, o_ref):
        v = x_ref[...]
        sg = jax.nn.sigmoid(v)
        o_ref[...] = dy_ref[...] * (sg * (1.0 + v * (1.0 - sg)))

    return pl.pallas_call(body, name=name, out_shape=jax.ShapeDtypeStruct(x.shape, F32))(x, dy)


def _adamw(name, w, g_parts, m, v):
    shape = w.shape
    w2, m2, v2 = _rows_view(w), _rows_view(m), _rows_view(v)
    gs = [_rows_view(g) for g in g_parts]
    rows, cols = w2.shape
    tr = _row_tile(rows, cols)
    ng = len(gs)

    def body(*refs):
        w_ref, m_ref, v_ref = refs[0], refs[1], refs[2]
        g_refs = refs[3:3 + ng]
        g_out, d_out, m_out, v_out = refs[3 + ng:]
        g = g_refs[0][...]
        for r in g_refs[1:]:
            g = g + r[...]
        m1 = ADAM_B1 * m_ref[...] + (1.0 - ADAM_B1) * g
        v1 = ADAM_B2 * v_ref[...] + (1.0 - ADAM_B2) * jnp.square(g)
        m_hat = m1 / (1.0 - ADAM_B1 ** ADAM_STEP)
        v_hat = v1 / (1.0 - ADAM_B2 ** ADAM_STEP)
        g_out[...] = g
        d_out[...] = -ADAM_LR * (m_hat / (jnp.sqrt(v_hat) + ADAM_EPS) + ADAM_WD * w_ref[...])
        m_out[...] = m1
        v_out[...] = v1

    spec = pl.BlockSpec((tr, cols), lambda i: (i, 0))
    res = pl.pallas_call(
        body, name=name, grid=(rows // tr,), in_specs=[spec] * (3 + ng), out_specs=[spec] * 4,
        out_shape=[jax.ShapeDtypeStruct((rows, cols), F32)] * 4,
        compiler_params=_params(("parallel",)),
    )(w2, m2, v2, *gs)
    return tuple(r.reshape(shape) for r in res)


MOD_ROWS = 48


def kernel(x, c, ctx, c_ctx, mod_w, mod_b, norm1_g, norm2_g, ab_w_in, ab_w_out, ret_decay, att_q_norm_g, att_k_norm_g, cm_w_in, cm_v_norm_g, cm_w_s, cm_b_s, cm_w_out, ff_w1, ff_w2, loss_target, m_c_ctx, m_mod_w, m_mod_b, m_norm1_g, m_norm2_g, m_ab_w_in, m_ab_w_out, m_ret_decay, m_att_q_norm_g, m_att_k_norm_g, m_cm_w_in, m_cm_v_norm_g, m_cm_w_s, m_cm_b_s, m_cm_w_out, m_ff_w1, m_ff_w2, v_c_ctx, v_mod_w, v_mod_b, v_norm1_g, v_norm2_g, v_ab_w_in, v_ab_w_out, v_ret_decay, v_att_q_norm_g, v_att_k_norm_g, v_cm_w_in, v_cm_v_norm_g, v_cm_w_s, v_cm_b_s, v_cm_w_out, v_ff_w1, v_ff_w2):
    B, SL, D = x.shape
    cfg = Cfg(B=B, SC=ctx.shape[1], SL=SL, D=D, FF=ff_w1.shape[2] * N_CHIP)
    L = N_LAYERS
    n_ex = B * N_DEV
    mcols = mod_w.shape[2]
    weights = dict(c_ctx=c_ctx, mod_w=mod_w, mod_b=mod_b, norm1_g=norm1_g, norm2_g=norm2_g, ab_w_in=ab_w_in,
                   ab_w_out=ab_w_out, ret_decay=ret_decay, att_q_norm_g=att_q_norm_g, att_k_norm_g=att_k_norm_g,
                   cm_w_in=cm_w_in, cm_v_norm_g=cm_v_norm_g, cm_w_s=cm_w_s, cm_b_s=cm_b_s, cm_w_out=cm_w_out,
                   ff_w1=ff_w1, ff_w2=ff_w2)
    m_in = dict(c_ctx=m_c_ctx, mod_w=m_mod_w, mod_b=m_mod_b, norm1_g=m_norm1_g, norm2_g=m_norm2_g, ab_w_in=m_ab_w_in,
                ab_w_out=m_ab_w_out, ret_decay=m_ret_decay, att_q_norm_g=m_att_q_norm_g, att_k_norm_g=m_att_k_norm_g,
                cm_w_in=m_cm_w_in, cm_v_norm_g=m_cm_v_norm_g, cm_w_s=m_cm_w_s, cm_b_s=m_cm_b_s, cm_w_out=m_cm_w_out,
                ff_w1=m_ff_w1, ff_w2=m_ff_w2)
    v_in = dict(c_ctx=v_c_ctx, mod_w=v_mod_w, mod_b=v_mod_b, norm1_g=v_norm1_g, norm2_g=v_norm2_g, ab_w_in=v_ab_w_in,
                ab_w_out=v_ab_w_out, ret_decay=v_ret_decay, att_q_norm_g=v_att_q_norm_g, att_k_norm_g=v_att_k_norm_g,
                cm_w_in=v_cm_w_in, cm_v_norm_g=v_cm_v_norm_g, cm_w_s=v_cm_w_s, cm_b_s=v_cm_b_s, cm_w_out=v_cm_w_out,
                ff_w1=v_ff_w1, ff_w2=v_ff_w2)
    xi, yi, ci = lax.axis_index("x"), lax.axis_index("y"), lax.axis_index("c")
    chip = 2 * xi + yi
    dev = 2 * chip + ci

    shards = {n: [weights[n][i].astype(BF16) for i in range(weights[n].shape[0])] for n in BIG}
    vgw = cm_v_norm_g.shape[1]
    blk = jnp.zeros((8, D), F32).at[:B].set(c).at[B:B + 2, :vgw].set(cm_v_norm_g)
    g0 = _allgather8("gather_c", blk).reshape(N_DEV, 8, D)
    c_all = g0[:, :B].reshape(n_ex, D)
    vg_full = jnp.concatenate([g0[2 * s, B:B + 2, :vgw] for s in range(N_CHIP)], axis=-1)

    pre = jnp.zeros((MOD_ROWS, D), F32).at[:n_ex].set(c_all).at[n_ex].set(c_ctx)
    act = _silu_rows("silu_c", pre)
    mpart = jnp.stack([_mm(f"mod_fwd_{l}", act, mod_w, mode="nn", layer=l, outs=[F32], tn=mcols) for l in range(L)])
    g1, (first_weight,) = _allgather8("gather_mod", mpart.reshape(L * MOD_ROWS, mcols),
                                      carry=_Carry("gather", (shards["ab_w_in"][0],), (BIG["ab_w_in"],)))
    g1 = g1.reshape(N_DEV, L, MOD_ROWS, mcols)
    mod_all = jnp.concatenate([g1[2 * s] for s in range(N_CHIP)], axis=-1) + mod_b[:, None, :]
    mod_lat = lax.dynamic_slice_in_dim(mod_all, dev * B, B, axis=1)
    mod_ctx = jnp.broadcast_to(mod_all[:, n_ex][:, None], mod_lat.shape)
    mods = jnp.stack([mod_ctx, mod_lat], axis=2).reshape(L, B, 2, 6, D)

    w = dict(norm1_g=norm1_g, norm2_g=norm2_g, ret_decay=ret_decay, att_q_norm_g=att_q_norm_g,
             att_k_norm_g=att_k_norm_g, cm_v_norm_g=vg_full, cm_w_s=cm_w_s, cm_b_s=cm_b_s)
    xcat = jnp.concatenate([ctx, x], axis=1).reshape(cfg.T, D)
    loss_local, dx_lat, recv, small, dmods = _local_step(cfg, xcat, loss_target.reshape(B * SL, D), mods, shards, w,
                                                         first_weight)
    loss = lax.psum(loss_local, ("x", "y", "c"))
    grad_x = dx_lat.reshape(B, SL, D)

    part = [_sum_parts_layers(f"sum_{n}", [recv[(n, i)] for i in range(weights[n].shape[0])]) for n in BIG]

    dmod = jnp.stack(dmods).reshape(L, B, 2, 6 * D)
    dmod_lat = dmod[:, :, 1]
    dmod_ctx = jnp.sum(dmod[:, :, 0], axis=1)
    d_ret = jnp.stack(small["ret_lg"]) * jax.nn.sigmoid(-ret_decay)
    summed = [dmod_ctx.reshape(-1), jnp.stack(small["norm1_g"]).reshape(-1), jnp.stack(small["norm2_g"]).reshape(-1),
              jnp.stack(small["cm_v_norm_g"]).reshape(-1), jnp.stack(small["cm_w_s"]).reshape(-1),
              jnp.stack(small["cm_b_s"]).reshape(-1), jnp.stack(small["att_q_norm_g"]).reshape(-1),
              jnp.stack(small["att_k_norm_g"]).reshape(-1), d_ret.reshape(-1)]
    sizes = [int(a.shape[0]) for a in summed]
    flat = jnp.concatenate(summed + [dmod_lat.reshape(-1)])
    n_sum = sum(sizes)
    n_sum_rows = -(-n_sum // D)
    lat_rows = (L * B * 6 * D) // D
    pack_rows = -(-(n_sum_rows + lat_rows) // 8) * 8
    packed = jnp.zeros((pack_rows * D,), F32).at[:n_sum].set(flat[:n_sum])
    packed = packed.at[n_sum_rows * D:(n_sum_rows + lat_rows) * D].set(flat[n_sum:]).reshape(pack_rows, D)
    g2, other = _allgather8("gather_small", packed, carry=_Carry("swap", tuple(part), (0,) * len(part)))
    g2 = g2.reshape(N_DEV, pack_rows, D)
    tot = _sum_leading("sum_small", g2[:, :n_sum_rows]).reshape(-1)
    pieces, off = [], 0
    for sz in sizes:
        pieces.append(tot[off:off + sz])
        off += sz
    dmod_ctx_t, g_n1, g_n2, g_vg, g_ws, g_bs, g_qg, g_kg, g_rd = pieces
    dmod_ctx_t = dmod_ctx_t.reshape(L, 6 * D)
    dmod_lat_all = g2[:, n_sum_rows:n_sum_rows + lat_rows].reshape(N_DEV, L, B, 6 * D)
    dmod_rows = jnp.zeros((L, MOD_ROWS, 6 * D), F32)
    dmod_rows = dmod_rows.at[:, :n_ex].set(jnp.transpose(dmod_lat_all, (1, 0, 2, 3)).reshape(L, n_ex, 6 * D))
    dmod_rows = dmod_rows.at[:, n_ex].set(dmod_ctx_t)
    g_mod_b = _sum_leading("sum_mod_b", jnp.transpose(dmod_rows, (1, 0, 2)))
    dmod_mine = lax.dynamic_slice_in_dim(dmod_rows, chip * mcols, mcols, axis=2)
    g_mod_w = jnp.stack([_mm(f"mod_dw_{l}", act, dmod_mine[l], mode="tn", outs=[F32], tn=mcols) for l in range(L)])
    ctx8 = jnp.zeros((L, 8, mcols), F32).at[:, 0].set(dmod_mine[:, n_ex])
    dcc = [_mm(f"mod_dctx_{l}", ctx8[l], mod_w, mode="nt", layer=l, outs=[F32], tk=mcols) for l in range(L)]
    dcc = _sum_leading("sum_dctx_layers", jnp.stack(dcc))
    g3 = _allgather8("gather_dctx", dcc).reshape(N_DEV, 8, D)
    dcc_t = _sum_leading("sum_dctx_chips", g3[0::2])[0:1]
    g_c_ctx = _silu_bwd_rows("silu_bwd_cctx", c_ctx[None], dcc_t)[0]

    vg_mine = lax.dynamic_slice_in_dim(g_vg.reshape(2, -1), chip * vgw, vgw, axis=1)
    small_g = dict(c_ctx=g_c_ctx, mod_w=g_mod_w, mod_b=g_mod_b, norm1_g=g_n1.reshape(norm1_g.shape),
                   norm2_g=g_n2.reshape(norm2_g.shape), ret_decay=g_rd.reshape(ret_decay.shape),
                   att_q_norm_g=g_qg.reshape(att_q_norm_g.shape), att_k_norm_g=g_kg.reshape(att_k_norm_g.shape),
                   cm_v_norm_g=vg_mine, cm_w_s=g_ws.reshape(cm_w_s.shape), cm_b_s=g_bs.reshape(cm_b_s.shape))
    out = {}
    for n, g in small_g.items():
        out[n] = _adamw(f"adamw_{n}", weights[n], [g], m_in[n], v_in[n])
    for n, p_mine, p_other in zip(BIG, part, other):
        out[n] = _adamw(f"adamw_{n}", weights[n], [p_mine, p_other], m_in[n], v_in[n])

    order = list(weights)
    return (loss, grad_x, *[out[n][0] for n in order], *[out[n][1] for n in order],
            *[out[n][2] for n in order], *[out[n][3] for n in order])
```

```python
import functools
import math
from typing import NamedTuple

import jax
import jax.numpy as jnp
from jax import lax
from jax.experimental import pallas as pl
from jax.experimental.pallas import tpu as pltpu

F32 = jnp.float32
BF16 = jnp.bfloat16
EPS = 1e-6
ROPE_BASE = 10000.0
LANES = 128
CHUNK = 128
N_LAYERS = 4
VMEM_LIMIT = 56 * 1024 * 1024

ADAM_LR = 0.001
ADAM_B1 = 0.9
ADAM_B2 = 0.999
ADAM_EPS = 1e-08
ADAM_WD = 0.01
ADAM_STEP = 10


class Cfg(NamedTuple):
    B: int = 4
    SC: int = 256
    SL: int = 2048
    D: int = 1024
    FF: int = 4096
    GRID_W: int = 64
    H: int = 4
    KV: int = 2
    CMW: int = 1024
    CMG: int = 8

    @property
    def S(self):
        return self.SC + self.SL

    @property
    def T(self):
        return self.B * self.S

    @property
    def TM(self):
        return self.SC

    @property
    def TPE(self):
        return self.S // self.SC

    @property
    def ABW(self):
        return (5 * self.H + 2 * self.KV) * CHUNK


def _tile(dim, pref):
    t = min(dim, pref)
    while dim % t:
        t -= LANES
    return t


def _dot(a, b):
    return lax.dot_general(a, b, (((1,), (0,)), ((), ())), preferred_element_type=F32)


def _dot_nt(a, b):
    return lax.dot_general(a, b, (((1,), (1,)), ((), ())), preferred_element_type=F32)


def _dot_tn(a, b):
    return lax.dot_general(a, b, (((0,), (0,)), ((), ())), preferred_element_type=F32)


def _params(sem, vmem=VMEM_LIMIT):
    return pltpu.CompilerParams(dimension_semantics=sem, vmem_limit_bytes=vmem)


def _mod_index(cfg):
    tpe = cfg.TPE
    return lambda i: (i // tpe, jnp.minimum(i % tpe, 1), 0, 0)


def _mm(name, a, b, *, mode, outs, tm=1024, tn=1024, tk=1024, layer=None, epi=None, extras=(), carry=None,
        resident=False):
    bshape = b.shape[1:] if layer is not None else b.shape
    if mode == "nn":
        (M, K), N = a.shape, bshape[1]
    elif mode == "nt":
        (M, K), N = a.shape, bshape[0]
    else:
        (K, M), N = a.shape, bshape[1]
    tm, tn, tk = _tile(M, tm), _tile(N, tn), _tile(K, tk)
    kchunk = tk
    if resident:
        tk = K
    nk = K // tk
    a_spec = (pl.BlockSpec((tk, tm), lambda i, j, k: (k, i)) if mode == "tn"
              else pl.BlockSpec((tm, tk), lambda i, j, k: (i, k)))
    if mode == "nt":
        bblk, bidx = (tn, tk), (lambda i, j, k: (j, k))
    else:
        bblk, bidx = (tk, tn), (lambda i, j, k: (k, j))
    if layer is not None:
        b_spec = pl.BlockSpec((None,) + bblk, lambda i, j, k: (layer,) + bidx(i, j, k))
    elif resident:
        b_spec = pl.BlockSpec(bblk, bidx, pipeline_mode=pl.Buffered(1))
    else:
        b_spec = pl.BlockSpec(bblk, bidx)
    ne, no = len(extras), len(outs)
    nc = len(carry.srcs) if carry is not None else 0
    dot = {"nn": _dot, "nt": _dot_nt, "tn": _dot_tn}[mode]
    grid = (M // tm, N // tn, nk)

    def body(*refs):
        a_ref, b_ref = refs[0], refs[1]
        ex, out_refs = refs[2:2 + ne], refs[2 + ne + nc:2 + ne + nc + no]
        row_tile = pl.program_id(0)

        if nc:
            step = (pl.program_id(0) * grid[1] + pl.program_id(1)) * grid[2] + pl.program_id(2)
            c_src = refs[2 + ne:2 + ne + nc]
            c_dst = refs[2 + ne + nc + no:2 + ne + 2 * nc + no]
            sems = refs[2 + ne + 2 * nc + no:2 + ne + 2 * nc + no + 3]

            @pl.when(step == 0)
            def _():
                for cp in _carry_copies(carry, c_src, c_dst, *sems):
                    cp.start()

        def finish(acc):
            res = epi(acc, row_tile, *ex) if epi is not None else (acc,)
            for r, o in zip(res, out_refs):
                o[...] = r.astype(o.dtype)

        if resident and K > kchunk:
            part = None
            for c in range(K // kchunk):
                ks = pl.ds(c * kchunk, kchunk)
                b_chunk = b_ref[:, ks] if mode == "nt" else b_ref[ks, :]
                term = dot(a_ref[:, ks].astype(BF16), b_chunk.astype(BF16))
                part = term if part is None else part + term
        else:
            part = dot(a_ref[...].astype(BF16), b_ref[...].astype(BF16))
        if nk == 1:
            finish(part)
        else:
            acc_ref = refs[-1]
            k = pl.program_id(2)

            @pl.when(k == 0)
            def _():
                acc_ref[...] = part

            @pl.when(k > 0)
            def _():
                acc_ref[...] += part

            @pl.when(k == nk - 1)
            def _():
                finish(acc_ref[...])

        if nc:
            @pl.when(step == grid[0] * grid[1] * grid[2] - 1)
            def _():
                for cp in _carry_copies(carry, c_src, c_dst, *sems):
                    cp.wait()

    scratch = [pltpu.SemaphoreType.DMA((3 * nc,)), pltpu.SemaphoreType.DMA((3 * nc,)),
               pltpu.SemaphoreType.DMA((nc,))] if nc else []
    if nk > 1:
        scratch.append(pltpu.VMEM((tm, tn), F32))
    res = pl.pallas_call(
        body, name=name, grid=grid,
        in_specs=[a_spec, b_spec] + [s for _, s in extras] + [ANY] * nc,
        out_specs=[pl.BlockSpec((tm, tn), lambda i, j, k: (i, j)) for _ in outs] + [ANY] * nc,
        out_shape=[jax.ShapeDtypeStruct((M, N), d) for d in outs] + (_carry_out_shapes(carry) if nc else []),
        scratch_shapes=scratch,
        compiler_params=_params(("arbitrary",) * 3 if nc else ("parallel", "parallel", "arbitrary")),
    )(a, b, *[x for x, _ in extras], *(carry.srcs if nc else ()))
    if nc:
        return (res[0] if no == 1 else res[:no]), res[no:]
    return res[0] if no == 1 else res


def _ff_bwd(name, first, second, weight, kind, carry=None):
    (T, D), FF = first.shape, second.shape[1]
    halves = 2
    tm = _tile(T, 1024)
    ffh = FF // halves if kind == "w2" else FF
    dh_cols = D if kind == "w2" else D // halves
    cw = _tile(ffh, 1024)
    n_steps = T // tm
    nc = len(carry.srcs) if carry is not None else 0

    def body(*refs):
        a_ref, b_ref, w_ref = refs[:3]
        c_src = refs[3:3 + nc]
        x_ref, dw_ref = refs[3 + nc:5 + nc]
        c_dst, sems = refs[5 + nc:5 + 2 * nc], refs[5 + 2 * nc:5 + 2 * nc + 3] if nc else ()
        acc_ref = refs[-1]
        i = pl.program_id(1)
        step = pl.program_id(0) * n_steps + i
        if nc:
            _carry_begin(carry, c_src, c_dst, sems, step)
        a = a_ref[...]
        dh = None
        for c in range(ffh // cw):
            cols = pl.ds(c * cw, cw)
            if kind == "w2":
                r = b_ref[:, cols]
                x_ref[:, cols] = (_dot_nt(a, w_ref[cols, :]) * (2.0 * jnp.sqrt(r.astype(F32)))).astype(BF16)
                part, dst = _dot_tn(r, a), acc_ref.at[cols, :]
            else:
                da = b_ref[:, cols]
                term = _dot_nt(da, w_ref[:, cols])
                dh = term if dh is None else dh + term
                part, dst = _dot_tn(a, da), acc_ref.at[:, cols]

            @pl.when(i == 0)
            def _():
                dst[...] = part

            @pl.when(i > 0)
            def _():
                dst[...] += part

        if kind == "w1":
            x_ref[...] = dh.astype(BF16)

        @pl.when(i == n_steps - 1)
        def _():
            dw_ref[...] = acc_ref[...].astype(BF16)

        if nc:
            _carry_end(carry, c_src, c_dst, sems, step, halves * n_steps)

    if kind == "w2":
        wshape = (ffh, D)
        a_spec = pl.BlockSpec((tm, D), lambda j, i: (i, 0))
        b_spec = pl.BlockSpec((tm, ffh), lambda j, i: (i, j))
        x_spec, x_cols = pl.BlockSpec((tm, ffh), lambda j, i: (i, j)), FF
    else:
        wshape = (dh_cols, FF)
        a_spec = pl.BlockSpec((tm, dh_cols), lambda j, i: (i, j))
        b_spec = pl.BlockSpec((tm, FF), lambda j, i: (i, 0))
        x_spec, x_cols = pl.BlockSpec((tm, dh_cols), lambda j, i: (i, j)), D
    wspec = pl.BlockSpec(wshape, lambda j, i: (j, 0), pipeline_mode=pl.Buffered(1))
    res = pl.pallas_call(
        body, name=name, grid=(halves, n_steps),
        in_specs=[a_spec, b_spec, wspec] + [ANY] * nc,
        out_specs=[x_spec, wspec] + [ANY] * nc,
        out_shape=[jax.ShapeDtypeStruct((T, x_cols), BF16), jax.ShapeDtypeStruct(weight.shape, BF16)]
        + (_carry_out_shapes(carry) if nc else []),
        scratch_shapes=_carry_scratch(nc) + [pltpu.VMEM(wshape, F32)],
        compiler_params=_params(("arbitrary", "arbitrary")),
    )(first, second, weight, *(carry.srcs if nc else ()))
    return res[:2], res[2:]


def _norm_mod_fwd(cfg, name, x, gain, mod, ish, isc):
    T, D, TM = cfg.T, cfg.D, cfg.TM

    def body(x_ref, g_ref, mod_ref, h_ref):
        x = x_ref[...]
        rstd = lax.rsqrt(jnp.mean(x * x, axis=-1, keepdims=True) + EPS)
        n = x * rstd * g_ref[...]
        h = n * (1.0 + mod_ref[pl.ds(isc, 1), :]) + mod_ref[pl.ds(ish, 1), :]
        h_ref[...] = h.astype(BF16)

    return pl.pallas_call(
        body, name=name, grid=(T // TM,),
        in_specs=[pl.BlockSpec((TM, D), lambda i: (i, 0)), pl.BlockSpec((1, D), lambda i: (0, 0)),
                  pl.BlockSpec((None, None, 6, D), _mod_index(cfg))],
        out_specs=pl.BlockSpec((TM, D), lambda i: (i, 0)),
        out_shape=jax.ShapeDtypeStruct((T, D), BF16),
        compiler_params=_params(("parallel",)),
    )(x, gain, mod)


def _norm_mod_bwd(cfg, name, x, gain, mod, ish, isc, dh, dres, gate=None, lat_only=False):
    T, D, TM, TPE = cfg.T, cfg.D, cfg.TM, cfg.TPE
    ng = 2 if gate is not None else 0
    dx_spec = (pl.BlockSpec((TM, D), lambda i: ((i // TPE) * (TPE - 1) + jnp.maximum(i % TPE - 1, 0), 0)) if lat_only
               else pl.BlockSpec((TM, D), lambda i: (i, 0)))
    dx_rows = cfg.B * cfg.SL if lat_only else T

    def body(*refs):
        x_ref, g_ref, mod_ref, dh_ref, dres_ref = refs[:5]
        dx_ref, dmod_ref, dgain_ref = refs[5 + ng:8 + ng]
        i = pl.program_id(0)
        t = i % TPE
        x = x_ref[...]
        g = g_ref[...]
        dh = dh_ref[...].astype(F32)
        rstd = lax.rsqrt(jnp.mean(x * x, axis=-1, keepdims=True) + EPS)
        xhat = x * rstd
        dn = dh * (1.0 + mod_ref[pl.ds(isc, 1), :])
        dsh = jnp.sum(dh, axis=0, keepdims=True)
        dsc = jnp.sum(dh * (xhat * g), axis=0, keepdims=True)
        dgain = jnp.sum(dn * xhat, axis=0, keepdims=True)
        dxh = dn * g
        dx = rstd * (dxh - xhat * jnp.mean(dxh * xhat, axis=-1, keepdims=True)) + dres_ref[...]
        dx_ref[...] = dx
        sums = [(dmod_ref.at[pl.ds(0, 1), :], dsh), (dmod_ref.at[pl.ds(1, 1), :], dsc)]
        if ng:
            y_ref, gmod_ref = refs[5:7]
            dy_ref, dgate_ref = refs[8 + ng:]
            dy_ref[...] = (dx * gmod_ref[pl.ds(gate[2], 1), :]).astype(BF16)
            sums.append((dgate_ref, jnp.sum(dx * y_ref[...].astype(F32), axis=0, keepdims=True)))

        @pl.when(t <= 1)
        def _():
            for ref, val in sums:
                ref[...] = val

        @pl.when(t > 1)
        def _():
            for ref, val in sums:
                ref[...] += val

        @pl.when(i == 0)
        def _():
            dgain_ref[...] = dgain

        @pl.when(i > 0)
        def _():
            dgain_ref[...] += dgain

    tok = pl.BlockSpec((TM, D), lambda i: (i, 0))
    mod_spec = pl.BlockSpec((None, None, 6, D), _mod_index(cfg))
    res = pl.pallas_call(
        body, name=name, grid=(T // TM,),
        in_specs=[tok, pl.BlockSpec((1, D), lambda i: (0, 0)), mod_spec, tok, tok] + ([tok, mod_spec] if ng else []),
        out_specs=[dx_spec, pl.BlockSpec((None, None, 2, D), _mod_index(cfg)), pl.BlockSpec((1, D), lambda i: (0, 0))]
        + ([tok, pl.BlockSpec((None, None, 1, D), _mod_index(cfg))] if ng else []),
        out_shape=[jax.ShapeDtypeStruct((dx_rows, D), F32), jax.ShapeDtypeStruct((cfg.B, 2, 2, D), F32),
                   jax.ShapeDtypeStruct((1, D), F32)]
        + ([jax.ShapeDtypeStruct((T, D), BF16), jax.ShapeDtypeStruct((cfg.B, 2, 1, D), F32)] if ng else []),
        compiler_params=_params(("arbitrary",)),
    )(x, gain, mod, dh, dres, *(gate[:2] if ng else ()))
    return res


def _loss_grad(cfg, x, tgt, y, mod, igate):
    T, D, TM, TPE = cfg.T, cfg.D, cfg.TM, cfg.TPE

    def body(x_ref, t_ref, y_ref, mod_ref, dx_ref, loss_ref, dy_ref, dg_ref):
        i = pl.program_id(0)
        t = i % TPE

        @pl.when(i == 0)
        def _():
            loss_ref[...] = jnp.zeros_like(loss_ref)

        @pl.when(t == 0)
        def _():
            dx_ref[...] = jnp.zeros_like(dx_ref)
            dy_ref[...] = jnp.zeros_like(dy_ref)
            dg_ref[...] = jnp.zeros_like(dg_ref)

        @pl.when(t > 0)
        def _():
            err = x_ref[...] - t_ref[...]
            dx = err * (1.0 / D)
            dx_ref[...] = dx
            loss_ref[...] += 0.5 * jnp.sum(jnp.mean(err * err, axis=-1, keepdims=True), axis=0, keepdims=True)
            dy_ref[...] = (dx * mod_ref[pl.ds(igate, 1), :]).astype(BF16)
            dg = jnp.sum(dx * y_ref[...].astype(F32), axis=0, keepdims=True)

            @pl.when(t == 1)
            def _():
                dg_ref[...] = dg

            @pl.when(t > 1)
            def _():
                dg_ref[...] += dg

    tok = pl.BlockSpec((TM, D), lambda i: (i, 0))
    tgt_spec = pl.BlockSpec((TM, D), lambda i: ((i // TPE) * (TPE - 1) + jnp.maximum(i % TPE - 1, 0), 0))
    dx, loss, dy, dg = pl.pallas_call(
        body, name="loss_grad", grid=(T // TM,),
        in_specs=[tok, tgt_spec, tok, pl.BlockSpec((None, None, 6, D), _mod_index(cfg))],
        out_specs=[tok, pl.BlockSpec((8, LANES), lambda i: (0, 0)), tok,
                   pl.BlockSpec((None, None, 1, D), _mod_index(cfg))],
        out_shape=[jax.ShapeDtypeStruct((T, D), F32), jax.ShapeDtypeStruct((8, LANES), F32),
                   jax.ShapeDtypeStruct((T, D), BF16), jax.ShapeDtypeStruct((cfg.B, 2, 1, D), F32)],
        compiler_params=_params(("arbitrary",)),
    )(x, tgt, y, mod)
    return loss[0, 0], dx, dy, dg


def _rope_tables(cfg):
    rows = cfg.SL // cfg.GRID_W
    row = jnp.repeat(jnp.arange(rows, dtype=F32), cfg.GRID_W)
    col = jnp.tile(jnp.arange(cfg.GRID_W, dtype=F32), rows)
    n_freq = CHUNK // 4
    inv = ROPE_BASE ** (-jnp.arange(n_freq, dtype=F32) / n_freq)
    ang = jnp.concatenate([row[:, None] * inv[None, :], col[:, None] * inv[None, :]], axis=-1)
    cos, sin = jnp.cos(ang), jnp.sin(ang)
    cosf = jnp.concatenate([jnp.ones((cfg.SC, CHUNK), F32), jnp.concatenate([cos, cos], axis=-1)], axis=0)
    sinf = jnp.concatenate([jnp.zeros((cfg.SC, CHUNK), F32), jnp.concatenate([-sin, sin], axis=-1)], axis=0)
    return cosf, sinf


def _rope(x, cosf, sinf):
    return x * cosf + pltpu.roll(x, CHUNK // 2, 1) * sinf


def _irope(dy, cosf, sinf):
    return dy * cosf - pltpu.roll(dy, CHUNK // 2, 1) * sinf


def _prep_fwd(cfg, name, p, cosf, sinf, qg, kg):
    T, TM, TPE, H, KV = cfg.T, cfg.TM, cfg.TPE, cfg.H, cfg.KV
    HW = H * CHUNK
    kscale = CHUNK ** -0.5

    def body(p_ref, c_ref, s_ref, qg_ref, kg_ref, rq_ref, rk_ref, aq_ref, ak_ref):
        cosf, sinf = c_ref[...], s_ref[...]

        def normed(x, g):
            return x * lax.rsqrt(jnp.mean(x * x, axis=-1, keepdims=True) + EPS) * g

        def seg(col):
            return p_ref[:, pl.ds(col, CHUNK)].astype(F32)

        for h in range(H):
            sl = pl.ds(h * CHUNK, CHUNK)
            rq_ref[:, sl] = _rope(seg(h * CHUNK), cosf, sinf)
            rk_ref[:, sl] = _rope(seg(HW + h * CHUNK), cosf, sinf) * kscale
            aq_ref[:, sl] = (_rope(normed(seg(4 * HW + h * CHUNK), qg_ref[...]), cosf, sinf) * ATT_SCALE).astype(BF16)
        for h in range(KV):
            ak_ref[:, pl.ds(h * CHUNK, CHUNK)] = _rope(
                normed(seg(5 * HW + h * CHUNK), kg_ref[...]), cosf, sinf).astype(BF16)

    tab = pl.BlockSpec((TM, CHUNK), lambda i: (i % TPE, 0))
    vec = pl.BlockSpec((1, CHUNK), lambda i: (0, 0))
    return pl.pallas_call(
        body, name=name, grid=(T // TM,),
        in_specs=[pl.BlockSpec((TM, cfg.ABW), lambda i: (i, 0)), tab, tab, vec, vec],
        out_specs=[pl.BlockSpec((TM, HW), lambda i: (i, 0))] * 3 + [pl.BlockSpec((TM, KV * CHUNK), lambda i: (i, 0))],
        out_shape=[jax.ShapeDtypeStruct((T, HW), F32), jax.ShapeDtypeStruct((T, HW), F32),
                   jax.ShapeDtypeStruct((T, HW), BF16), jax.ShapeDtypeStruct((T, KV * CHUNK), BF16)],
        compiler_params=_params(("parallel",)),
    )(p, cosf, sinf, qg, kg)


def _prep_bwd(cfg, name, p, cosf, sinf, qg, kg, d_rq, d_rk, d_rv, d_gate, d_aq, d_ak, d_av):
    T, TM, TPE, H, KV = cfg.T, cfg.TM, cfg.TPE, cfg.H, cfg.KV
    HW = H * CHUNK
    kscale = CHUNK ** -0.5

    def body(p_ref, c_ref, s_ref, qg_ref, kg_ref, drq_ref, drk_ref, drv_ref, dgt_ref, daq_ref, dak_ref, dav_ref,
             dp_ref, dqg_ref, dkg_ref):
        i = pl.program_id(0)
        cosf, sinf = c_ref[...], s_ref[...]

        def norm_bwd(x, g, dn):
            rstd = lax.rsqrt(jnp.mean(x * x, axis=-1, keepdims=True) + EPS)
            xhat = x * rstd
            dg = jnp.sum(dn * xhat, axis=0, keepdims=True)
            dxh = dn * g
            return rstd * (dxh - xhat * jnp.mean(dxh * xhat, axis=-1, keepdims=True)), dg

        dqg = jnp.zeros((1, CHUNK), F32)
        dkg = jnp.zeros((1, CHUNK), F32)
        for h in range(H):
            sl = pl.ds(h * CHUNK, CHUNK)
            dp_ref[:, pl.ds(h * CHUNK, CHUNK)] = _irope(drq_ref[:, sl].astype(F32), cosf, sinf).astype(BF16)
            dp_ref[:, pl.ds(HW + h * CHUNK, CHUNK)] = (_irope(drk_ref[:, sl].astype(F32), cosf, sinf)
                                                       * kscale).astype(BF16)
            dp_ref[:, pl.ds(2 * HW + h * CHUNK, CHUNK)] = drv_ref[:, sl].astype(BF16)
            dp_ref[:, pl.ds(3 * HW + h * CHUNK, CHUNK)] = dgt_ref[:, sl].astype(BF16)
            dx, dg = norm_bwd(p_ref[:, pl.ds(4 * HW + h * CHUNK, CHUNK)].astype(F32), qg_ref[...],
                              _irope(daq_ref[:, sl].astype(F32), cosf, sinf))
            dp_ref[:, pl.ds(4 * HW + h * CHUNK, CHUNK)] = dx.astype(BF16)
            dqg = dqg + dg
        for h in range(KV):
            sl = pl.ds(h * CHUNK, CHUNK)
            dx, dg = norm_bwd(p_ref[:, pl.ds(5 * HW + h * CHUNK, CHUNK)].astype(F32), kg_ref[...],
                              _irope(dak_ref[:, sl], cosf, sinf))
            dp_ref[:, pl.ds(5 * HW + h * CHUNK, CHUNK)] = dx.astype(BF16)
            dp_ref[:, pl.ds(5 * HW + (KV + h) * CHUNK, CHUNK)] = dav_ref[:, sl].astype(BF16)
            dkg = dkg + dg

        @pl.when(i == 0)
        def _():
            dqg_ref[...] = dqg
            dkg_ref[...] = dkg

        @pl.when(i > 0)
        def _():
            dqg_ref[...] += dqg
            dkg_ref[...] += dkg

    tab = pl.BlockSpec((TM, CHUNK), lambda i: (i % TPE, 0))
    vec = pl.BlockSpec((1, CHUNK), lambda i: (0, 0))
    hw = pl.BlockSpec((TM, HW), lambda i: (i, 0))
    kvw = pl.BlockSpec((TM, KV * CHUNK), lambda i: (i, 0))
    return pl.pallas_call(
        body, name=name, grid=(T // TM,),
        in_specs=[pl.BlockSpec((TM, cfg.ABW), lambda i: (i, 0)), tab, tab, vec, vec, hw, hw, hw, hw, hw, kvw, kvw],
        out_specs=[pl.BlockSpec((TM, cfg.ABW), lambda i: (i, 0)), vec, vec],
        out_shape=[jax.ShapeDtypeStruct((T, cfg.ABW), BF16), jax.ShapeDtypeStruct((1, CHUNK), F32),
                   jax.ShapeDtypeStruct((1, CHUNK), F32)],
        compiler_params=_params(("arbitrary",)),
    )(p, cosf, sinf, qg, kg, d_rq, d_rk, d_rv, d_gate, d_aq, d_ak, d_av)


def _ret_consts(direction, lg):
    C = CHUNK
    ii = lax.broadcasted_iota(jnp.int32, (C, C), 0)
    jj = lax.broadcasted_iota(jnp.int32, (C, C), 1)
    col = lax.broadcasted_iota(jnp.int32, (C, 1), 0).astype(F32)
    if direction == 0:
        mask, er, ek, eq = ii >= jj, (ii - jj).astype(F32), (C - 1.0) - col, col + 1.0
    else:
        mask, er, ek, eq = jj >= ii, (jj - ii).astype(F32), col, C - col
    er = jnp.where(mask, er, 0.0)
    dm = jnp.where(mask, jnp.exp(er * lg), 0.0)
    return dm, er, jnp.exp(ek * lg), ek, jnp.exp(eq * lg), eq, jnp.exp(C * lg)


def _ret_order(cfg, direction):
    n_all, n_ctx = cfg.S // CHUNK, cfg.SC // CHUNK
    if direction == 0:
        return list(range(n_all))
    return list(range(n_ctx - 1, -1, -1)) + list(range(n_all - 1, n_ctx - 1, -1))


def _carry_begin(carry, c_src, c_dst, sems, step):
    @pl.when(step == 0)
    def _():
        for cp in _carry_copies(carry, c_src, c_dst, *sems):
            cp.start()


def _carry_end(carry, c_src, c_dst, sems, step, n_steps):
    @pl.when(step == n_steps - 1)
    def _():
        for cp in _carry_copies(carry, c_src, c_dst, *sems):
            cp.wait()


def _carry_scratch(nc):
    return [pltpu.SemaphoreType.DMA((3 * nc,)), pltpu.SemaphoreType.DMA((3 * nc,)),
            pltpu.SemaphoreType.DMA((nc,))] if nc else []


def _head_norm_gate(o, g):
    mu = jnp.mean(o, axis=-1, keepdims=True)
    var = jnp.mean(jnp.square(o - mu), axis=-1, keepdims=True)
    rstd = lax.rsqrt(var + EPS)
    y = (o - mu) * rstd
    sg = jax.nn.sigmoid(g)
    return y, rstd, sg


RET_UNROLL = 3


def _retention_fwd(cfg, name, rq, rk, p, lgb, carry=None):
    B, H, S, T = cfg.B, cfg.H, cfg.S, cfg.T
    n_all = S // CHUNK
    nc = len(carry.srcs) if carry is not None else 0

    def body(*refs):
        q_ref, k_ref, v_ref, g_ref, lg_ref = refs[:5]
        c_src = refs[5:5 + nc]
        o_ref, ret_ref, st_ref = refs[5 + nc:8 + nc]
        c_dst = refs[8 + nc:8 + 2 * nc]
        sems = refs[8 + 2 * nc:8 + 2 * nc + 3] if nc else ()
        kv_ref = refs[-1]
        step = pl.program_id(0) * H + pl.program_id(1)
        if nc:
            _carry_begin(carry, c_src, c_dst, sems, step)

        def rows(n):
            return pl.ds(pl.multiple_of(n * CHUNK, CHUNK), CHUNK)

        (dm0, _, kd0, _, qd0, _, cd0), (dm1, _, kd1, _, qd1, _, cd1) = (
            _ret_consts(d, lg_ref[d, 0:1, 0:1]) for d in (0, 1))
        dm_both = dm0 + dm1

        def kv_step(n, c):
            k = k_ref[rows(n), :]
            v = v_ref[rows(n), :].astype(BF16)
            kv_ref[0, n] = _dot_tn((k * kd0).astype(BF16), v)
            kv_ref[1, n] = _dot_tn((k * kd1).astype(BF16), v)
            return c

        lax.fori_loop(0, n_all, kv_step, 0, unroll=RET_UNROLL)
        for direction, cd in ((0, cd0), (1, cd1)):
            st = jnp.zeros((CHUNK, CHUNK), F32)
            for t, n in enumerate(_ret_order(cfg, direction)):
                st_ref[direction, n] = st
                if t + 1 < n_all:
                    st = cd * st + kv_ref[direction, n]

        def out_step(n, c):
            q = q_ref[rows(n), :].astype(BF16)
            v = v_ref[rows(n), :].astype(BF16)
            s = _dot_nt(q, k_ref[rows(n), :].astype(BF16)) * dm_both
            states = jnp.concatenate([st_ref[0, n].astype(BF16), st_ref[1, n].astype(BF16)], axis=1)
            cross = _dot(q, states)
            o = _dot(s.astype(BF16), v) + cross[:, :CHUNK] * qd0 + cross[:, CHUNK:] * qd1
            o_ref[rows(n), :] = o
            g = g_ref[rows(n), :].astype(F32)
            y, _, sg = _head_norm_gate(o, g)
            ret_ref[rows(n), :] = (y * (g * sg)).astype(BF16)
            return c

        lax.fori_loop(0, n_all, out_step, 0, unroll=RET_UNROLL)
        if nc:
            _carry_end(carry, c_src, c_dst, sems, step, B * H)

    HW = H * CHUNK
    blk = lambda off: pl.BlockSpec((S, CHUNK), lambda b, h: (b, off + h))
    st_spec = pl.BlockSpec((None, None, 2, n_all, CHUNK, CHUNK), lambda b, h: (b, h, 0, 0, 0, 0))
    res = pl.pallas_call(
        body, name=name, grid=(B, H),
        in_specs=[blk(0), blk(0), blk(2 * H), blk(3 * H),
                  pl.BlockSpec((None, 2, 8, LANES), lambda b, h: (h, 0, 0, 0))] + [ANY] * nc,
        out_specs=[blk(0), blk(0), st_spec] + [ANY] * nc,
        out_shape=[jax.ShapeDtypeStruct((T, HW), F32), jax.ShapeDtypeStruct((T, 2 * HW), BF16),
                   jax.ShapeDtypeStruct((B, H, 2, n_all, CHUNK, CHUNK), F32)] + (_carry_out_shapes(carry) if nc else []),
        scratch_shapes=_carry_scratch(nc) + [pltpu.VMEM((2, n_all, CHUNK, CHUNK), F32)],
        compiler_params=_params(("arbitrary", "arbitrary") if nc else ("parallel", "parallel")),
    )(rq, rk, p, p, lgb, *(carry.srcs if nc else ()))
    return res[:3], res[3:]


def _retention_bwd(cfg, name, rq, rk, p, o_sum, states, dcat, lgb, carry=None):
    B, H, S, T = cfg.B, cfg.H, cfg.S, cfg.T
    n_all = S // CHUNK
    C = CHUNK
    nc = len(carry.srcs) if carry is not None else 0

    def body(*refs):
        q_ref, k_ref, v_ref, g_ref, o_ref, st_ref, dr_ref, lg_ref = refs[:8]
        c_src = refs[8:8 + nc]
        dq_ref, dk_ref, dv_ref, dg_ref, dlg_ref = refs[8 + nc:13 + nc]
        c_dst = refs[13 + nc:13 + 2 * nc]
        sems = refs[13 + 2 * nc:13 + 2 * nc + 3] if nc else ()
        do_ref, gq_ref, ds_ref, acc_ref = refs[-4:]
        step = pl.program_id(0) * H + pl.program_id(1)
        if nc:
            _carry_begin(carry, c_src, c_dst, sems, step)

        def rows(n):
            return pl.ds(pl.multiple_of(n * C, C), C)

        (dm0, er0, kd0, ek0, qd0, eq0, cd0), (dm1, er1, kd1, ek1, qd1, eq1, cd1) = (
            _ret_consts(d, lg_ref[d, 0:1, 0:1]) for d in (0, 1))
        dm_both = dm0 + dm1
        wdm0, wdm1 = dm0 * er0, dm1 * er1

        def side(a, b):
            return jnp.concatenate([a.astype(BF16), b.astype(BF16)], axis=1)

        def gq_step(n, c):
            g = g_ref[rows(n), :].astype(F32)
            dr = dr_ref[rows(n), :].astype(F32)
            y, rstd, sg = _head_norm_gate(o_ref[rows(n), :], g)
            dy = dr * (g * sg)
            dg_ref[rows(n), :] = (dr * y * (sg * (1.0 + g * (1.0 - sg)))).astype(BF16)
            do = rstd * (dy - jnp.mean(dy, axis=-1, keepdims=True) - y * jnp.mean(dy * y, axis=-1, keepdims=True))
            do_ref[rows(n), :] = do
            gq = _dot_tn(q_ref[rows(n), :].astype(BF16), side(do * qd0, do * qd1))
            gq_ref[0, n] = gq[:, :C]
            gq_ref[1, n] = gq[:, C:]
            return c

        lax.fori_loop(0, n_all, gq_step, 0, unroll=RET_UNROLL)
        for direction, cd in ((0, cd0), (1, cd1)):
            order = _ret_order(cfg, direction)
            ds = jnp.zeros((C, C), F32)
            for t in reversed(range(n_all)):
                ds_ref[direction, order[t]] = ds
                if t > 0:
                    ds = cd * ds + gq_ref[direction, order[t]]
        acc_ref[...] = jnp.zeros_like(acc_ref)

        def chunk_step(n, c):
            q = q_ref[rows(n), :].astype(BF16)
            kf = k_ref[rows(n), :]
            k = kf.astype(BF16)
            v = v_ref[rows(n), :].astype(BF16)
            do = do_ref[rows(n), :]
            dob = do.astype(BF16)
            sp0, sp1 = st_ref[0, n], st_ref[1, n]
            ds0, ds1 = ds_ref[0, n], ds_ref[1, n]
            states = side(sp0, sp1)
            dstates = jnp.concatenate([ds0.astype(BF16), ds1.astype(BF16)], axis=0)
            doq0, doq1 = do * qd0, do * qd1
            doq = side(doq0, doq1)
            s_raw = _dot_nt(q, k)
            dpm = _dot_nt(dob, v)
            dsr = (dpm * dm_both).astype(BF16)
            dks = _dot_nt(v, dstates)
            dks0, dks1 = dks[:, :C] * kd0, dks[:, C:] * kd1
            qs = _dot(q, states)
            dq_ref[rows(n), :] = (_dot(dsr, k) + _dot_nt(doq, states)).astype(BF16)
            dk_ref[rows(n), :] = (_dot_tn(dsr, q) + dks0 + dks1).astype(BF16)
            dv_ref[rows(n), :] = (_dot_tn((s_raw * dm_both).astype(BF16), dob)
                                  + _dot(side(kf * kd0, kf * kd1), dstates)).astype(BF16)
            inner = dpm * s_raw
            acc_ref[0] += (jnp.sum(inner * wdm0, axis=0, keepdims=True)
                           + jnp.sum(eq0 * doq0 * qs[:, :C], axis=0, keepdims=True)
                           + jnp.sum(ek0 * kf * dks0, axis=0, keepdims=True)
                           + (C * cd0) * jnp.sum(ds0 * sp0, axis=0, keepdims=True))
            acc_ref[1] += (jnp.sum(inner * wdm1, axis=0, keepdims=True)
                           + jnp.sum(eq1 * doq1 * qs[:, C:], axis=0, keepdims=True)
                           + jnp.sum(ek1 * kf * dks1, axis=0, keepdims=True)
                           + (C * cd1) * jnp.sum(ds1 * sp1, axis=0, keepdims=True))
            return c

        lax.fori_loop(0, n_all, chunk_step, 0, unroll=RET_UNROLL)
        for direction in (0, 1):
            dlg_ref[direction] = jnp.broadcast_to(jnp.sum(acc_ref[direction], axis=1, keepdims=True), (8, LANES))
        if nc:
            _carry_end(carry, c_src, c_dst, sems, step, B * H)

    HW = H * CHUNK
    blk = lambda off: pl.BlockSpec((S, CHUNK), lambda b, h: (b, off + h))
    st_spec = pl.BlockSpec((None, None, 2, n_all, C, C), lambda b, h: (b, h, 0, 0, 0, 0))
    res = pl.pallas_call(
        body, name=name, grid=(B, H),
        in_specs=[blk(0), blk(0), blk(2 * H), blk(3 * H), blk(0), st_spec, blk(0),
                  pl.BlockSpec((None, 2, 8, LANES), lambda b, h: (h, 0, 0, 0))] + [ANY] * nc,
        out_specs=[blk(0)] * 4 + [pl.BlockSpec((None, None, 2, 8, LANES), lambda b, h: (b, h, 0, 0, 0))] + [ANY] * nc,
        out_shape=[jax.ShapeDtypeStruct((T, HW), BF16)] * 4 + [jax.ShapeDtypeStruct((B, H, 2, 8, LANES), F32)]
        + (_carry_out_shapes(carry) if nc else []),
        scratch_shapes=_carry_scratch(nc) + [pltpu.VMEM((S, CHUNK), F32), pltpu.VMEM((2, n_all, C, C), F32),
                                             pltpu.VMEM((2, n_all, C, C), F32), pltpu.VMEM((2, 1, C), F32)],
        compiler_params=_params(("arbitrary", "arbitrary") if nc else ("parallel", "parallel")),
    )(rq, rk, p, p, o_sum, states, dcat, lgb, *(carry.srcs if nc else ()))
    return res[:5], res[5:]


ATT_SCALE = CHUNK ** -0.5


def _attn_scores(cfg, q, k, t):
    kcol = lax.broadcasted_iota(jnp.int32, (1, cfg.S), 1)
    bias = jnp.where(jnp.logical_or(t > 0, kcol < cfg.SC), 0.0, -1e30)
    return _dot_nt(q, k) + bias


def _attention_fwd(cfg, name, aq, ak, p, cat, carry=None):
    B, H, KV, S, T, TM, TPE = cfg.B, cfg.H, cfg.KV, cfg.S, cfg.T, cfg.TM, cfg.TPE
    G = H // KV
    v_off = (5 * H + KV)
    nc = len(carry.srcs) if carry is not None else 0

    def body(*refs):
        q_ref, k_ref, v_ref = refs[:3]
        o_ref, lse_ref = refs[4 + nc:6 + nc]
        c_src, c_dst, sems = refs[4:4 + nc], refs[6 + nc:6 + 2 * nc], refs[6 + 2 * nc:]
        step = (pl.program_id(0) * KV + pl.program_id(1)) * TPE + pl.program_id(2)
        if nc:
            _carry_begin(carry, c_src, c_dst, sems, step)
        k = k_ref[...]
        v = v_ref[...].astype(BF16)
        for g in range(G):
            cols = pl.ds(g * CHUNK, CHUNK)
            s = _attn_scores(cfg, q_ref[:, cols], k, pl.program_id(2))
            m = jnp.max(s, axis=-1, keepdims=True)
            e = jnp.exp(s - m)
            total = jnp.sum(e, axis=-1, keepdims=True)
            o_ref[:, cols] = (_dot(e.astype(BF16), v) * (1.0 / total)).astype(BF16)
            lse_ref[g] = m + jnp.log(total)
        if nc:
            _carry_end(carry, c_src, c_dst, sems, step, B * KV * TPE)

    res = pl.pallas_call(
        body, name=name, grid=(B, KV, TPE),
        in_specs=[pl.BlockSpec((TM, G * CHUNK), lambda b, kv, t: (b * TPE + t, kv)),
                  pl.BlockSpec((S, CHUNK), lambda b, kv, t: (b, kv)),
                  pl.BlockSpec((S, CHUNK), lambda b, kv, t: (b, v_off + kv)), ANY] + [ANY] * nc,
        out_specs=[pl.BlockSpec((TM, G * CHUNK), lambda b, kv, t: (b * TPE + t, KV + kv)),
                   pl.BlockSpec((G, TM, 1), lambda b, kv, t: (kv, b * TPE + t, 0))] + [ANY] * nc,
        out_shape=[jax.ShapeDtypeStruct(cat.shape, cat.dtype), jax.ShapeDtypeStruct((H, T, 1), F32)]
        + (_carry_out_shapes(carry) if nc else []),
        input_output_aliases={3: 0},
        scratch_shapes=_carry_scratch(nc),
        compiler_params=_params(("arbitrary",) * 3 if nc else ("parallel",) * 3),
    )(aq, ak, p, cat, *(carry.srcs if nc else ()))
    return res[:2], res[2:]


def _attention_bwd(cfg, name, aq, ak, p, cat, lse, dcat, carry=None):
    B, H, KV, S, T, TM, TPE = cfg.B, cfg.H, cfg.KV, cfg.S, cfg.T, cfg.TM, cfg.TPE
    G = H // KV
    v_off = (5 * H + KV)
    nc = len(carry.srcs) if carry is not None else 0

    def body(*refs):
        q_ref, k_ref, v_ref, o_ref, lse_ref, do_ref = refs[:6]
        dq_ref, dk_ref, dv_ref = refs[6 + nc:9 + nc]
        c_src, c_dst, sems = refs[6:6 + nc], refs[9 + nc:9 + 2 * nc], refs[9 + 2 * nc:]
        t = pl.program_id(2)
        step = (pl.program_id(0) * KV + pl.program_id(1)) * TPE + t
        if nc:
            _carry_begin(carry, c_src, c_dst, sems, step)
        k = k_ref[...]
        v = v_ref[...].astype(BF16)
        dk = dv = None
        for g in range(G):
            cols = pl.ds(g * CHUNK, CHUNK)
            q = q_ref[:, cols]
            do = do_ref[:, cols]
            pr = jnp.exp(_attn_scores(cfg, q, k, t) - lse_ref[g])
            delta = jnp.sum(do.astype(F32) * o_ref[:, cols].astype(F32), axis=-1, keepdims=True)
            ds = (pr * (_dot_nt(do, v) - delta)).astype(BF16)
            dq_ref[:, cols] = (_dot(ds, k) * ATT_SCALE).astype(BF16)
            dk_g, dv_g = _dot_tn(ds, q), _dot_tn(pr.astype(BF16), do)
            dk, dv = (dk_g, dv_g) if dk is None else (dk + dk_g, dv + dv_g)

        @pl.when(t == 0)
        def _():
            dk_ref[...] = dk
            dv_ref[...] = dv

        @pl.when(t > 0)
        def _():
            dk_ref[...] += dk
            dv_ref[...] += dv

        if nc:
            _carry_end(carry, c_src, c_dst, sems, step, B * KV * TPE)

    qspec = pl.BlockSpec((TM, G * CHUNK), lambda b, kv, t: (b * TPE + t, kv))
    kvspec = pl.BlockSpec((S, CHUNK), lambda b, kv, t: (b, kv))
    right = pl.BlockSpec((TM, G * CHUNK), lambda b, kv, t: (b * TPE + t, KV + kv))
    res = pl.pallas_call(
        body, name=name, grid=(B, KV, TPE),
        in_specs=[qspec, kvspec, pl.BlockSpec((S, CHUNK), lambda b, kv, t: (b, v_off + kv)), right,
                  pl.BlockSpec((G, TM, 1), lambda b, kv, t: (kv, b * TPE + t, 0)), right] + [ANY] * nc,
        out_specs=[qspec, kvspec, kvspec] + [ANY] * nc,
        out_shape=[jax.ShapeDtypeStruct((T, H * CHUNK), BF16), jax.ShapeDtypeStruct((T, KV * CHUNK), F32),
                   jax.ShapeDtypeStruct((T, KV * CHUNK), F32)] + (_carry_out_shapes(carry) if nc else []),
        scratch_shapes=_carry_scratch(nc),
        compiler_params=_params(("arbitrary",) * 3 if nc else ("parallel", "parallel", "arbitrary")),
    )(aq, ak, p, cat, lse, dcat, *(carry.srcs if nc else ()))
    return res[:3], res[3:]


_GELU_C = math.sqrt(2.0 / math.pi)


def _gelu(x):
    return 0.5 * x * (1.0 + jnp.tanh(_GELU_C * (x + 0.044715 * x * x * x)))


def _gelu_and_grad(x):
    x2 = x * x
    th = jnp.tanh(_GELU_C * (x + 0.044715 * x * x2))
    half = 0.5 * (1.0 + th)
    return x * half, half + 0.5 * x * (1.0 - th * th) * _GELU_C * (1.0 + 3.0 * 0.044715 * x2)


def _cm_fwd(cfg, name, a, vg, ws, bs):
    T, TM, W, NG = cfg.T, cfg.TM, cfg.CMW, cfg.CMG

    def body(a_ref, vg_ref, ws_ref, bs_ref, m_ref):
        v = _gelu(a_ref[:, pl.ds(W, W)].astype(F32))
        vn = (v * lax.rsqrt(jnp.mean(v * v, axis=-1, keepdims=True) + EPS) * vg_ref[...]).astype(BF16)
        for c in range(TM // CHUNK):
            for g in range(NG):
                rows, cols = slice(c * CHUNK, (c + 1) * CHUNK), slice(g * CHUNK, (g + 1) * CHUNK)
                sv = _dot(ws_ref[g].astype(BF16), vn[rows, cols]) + bs_ref[g]
                u = _gelu(a_ref[pl.ds(c * CHUNK, CHUNK), pl.ds(g * CHUNK, CHUNK)].astype(F32))
                m_ref[pl.ds(c * CHUNK, CHUNK), pl.ds(g * CHUNK, CHUNK)] = (u * sv).astype(BF16)

    return pl.pallas_call(
        body, name=name, grid=(T // TM,),
        in_specs=[pl.BlockSpec((TM, 2 * W), lambda i: (i, 0)), pl.BlockSpec((1, W), lambda i: (0, 0)),
                  pl.BlockSpec((NG, CHUNK, CHUNK), lambda i: (0, 0, 0)),
                  pl.BlockSpec((NG, CHUNK, 1), lambda i: (0, 0, 0))],
        out_specs=pl.BlockSpec((TM, W), lambda i: (i, 0)),
        out_shape=jax.ShapeDtypeStruct((T, W), BF16),
        compiler_params=_params(("parallel",)),
    )(a, vg, ws, bs)


def _cm_bwd(cfg, name, a, vg, ws, bs, dm):
    T, TM, W, NG = cfg.T, cfg.TM, cfg.CMW, cfg.CMG

    def body(a_ref, vg_ref, ws_ref, bs_ref, dm_ref, da_ref, dws_ref, dbs_ref, dvg_ref, dvn_ref):
        i = pl.program_id(0)

        @pl.when(i == 0)
        def _():
            dws_ref[...] = jnp.zeros_like(dws_ref)
            dbs_ref[...] = jnp.zeros_like(dbs_ref)
            dvg_ref[...] = jnp.zeros_like(dvg_ref)

        v, v_grad = _gelu_and_grad(a_ref[:, pl.ds(W, W)].astype(F32))
        rstd = lax.rsqrt(jnp.mean(v * v, axis=-1, keepdims=True) + EPS)
        xhat = v * rstd
        vg = vg_ref[...]
        vn = (xhat * vg).astype(BF16)
        for c in range(TM // CHUNK):
            for g in range(NG):
                rows, cols = slice(c * CHUNK, (c + 1) * CHUNK), slice(g * CHUNK, (g + 1) * CHUNK)
                rs, cs = pl.ds(c * CHUNK, CHUNK), pl.ds(g * CHUNK, CHUNK)
                wsb = ws_ref[g].astype(BF16)
                blk = vn[rows, cols]
                sv = _dot(wsb, blk) + bs_ref[g]
                u, u_grad = _gelu_and_grad(a_ref[rs, cs].astype(F32))
                dmb = dm_ref[rs, cs].astype(F32)
                da_ref[rs, cs] = (dmb * sv * u_grad).astype(BF16)
                dsv = dmb * u
                dsvb = dsv.astype(BF16)
                dbs_ref[g] += jnp.sum(dsv, axis=1, keepdims=True)
                dws_ref[g] += _dot_nt(dsvb, blk)
                dvn_ref[rs, cs] = _dot_tn(wsb, dsvb)
        dvn = dvn_ref[...]
        dvg_ref[...] += jnp.sum(dvn * xhat, axis=0, keepdims=True)
        dxh = dvn * vg
        dv = rstd * (dxh - xhat * jnp.mean(dxh * xhat, axis=-1, keepdims=True))
        da_ref[:, pl.ds(W, W)] = (dv * v_grad).astype(BF16)

    return pl.pallas_call(
        body, name=name, grid=(T // TM,),
        in_specs=[pl.BlockSpec((TM, 2 * W), lambda i: (i, 0)), pl.BlockSpec((1, W), lambda i: (0, 0)),
                  pl.BlockSpec((NG, CHUNK, CHUNK), lambda i: (0, 0, 0)),
                  pl.BlockSpec((NG, CHUNK, 1), lambda i: (0, 0, 0)), pl.BlockSpec((TM, W), lambda i: (i, 0))],
        out_specs=[pl.BlockSpec((TM, 2 * W), lambda i: (i, 0)), pl.BlockSpec((NG, CHUNK, CHUNK), lambda i: (0, 0, 0)),
                   pl.BlockSpec((NG, CHUNK, 1), lambda i: (0, 0, 0)), pl.BlockSpec((1, W), lambda i: (0, 0))],
        out_shape=[jax.ShapeDtypeStruct((T, 2 * W), BF16), jax.ShapeDtypeStruct((NG, CHUNK, CHUNK), F32),
                   jax.ShapeDtypeStruct((NG, CHUNK, 1), F32), jax.ShapeDtypeStruct((1, W), F32)],
        scratch_shapes=[pltpu.VMEM((TM, W), F32)],
        compiler_params=_params(("arbitrary",)),
    )(a, vg, ws, bs, dm)


def _layer_weights(l):
    mixer = ("ab_w_in", "ab_w_out") if l % 2 == 0 else ("cm_w_in", "cm_w_out")
    return [(mixer[0], l // 2), (mixer[1], l // 2), ("ff_w1", l), ("ff_w2", l)]


def _local_step(cfg, xcat, tgt, mods, shards, w, first_weight):
    D, TM, H = cfg.D, cfg.TM, cfg.H
    cosf, sinf = _rope_tables(cfg)
    full, big, recv = {}, {}, {}

    def gather_of(keys):
        return _Carry("gather", tuple(shards[n][i] for n, i in keys), tuple(BIG[n] for n, _ in keys))

    def exchange_of(keys):
        return _Carry("exchange", tuple(big[k] for k in keys), tuple(BIG[n] for n, _ in keys))

    def mm(pending, name, a, b, **kw):
        if not pending:
            return _mm(name, a, b, **kw)
        key, carry, sink = pending.pop(0)
        out, (got,) = _mm(name, a, b, carry=carry, **kw)
        sink[key] = got
        return out

    def with_carry(call, keys, make, sink):
        out, got = call(carry=make(keys) if keys else None)
        sink.update(zip(keys, got))
        return out

    keys0 = _layer_weights(0)
    full[keys0[0]] = first_weight
    TG = 3 * TM if cfg.TPE % 3 == 0 else TM
    tiles_per_ex = cfg.S // TG
    gate_spec = pl.BlockSpec((None, 2, 6, D), lambda i, j, k: (i // tiles_per_ex, 0, 0, 0))

    def resid_epi(igate, nxt):
        def epi(acc, row_tile, x_ref, mod_ref, *nxt_refs):
            row = lax.broadcasted_iota(jnp.int32, (TG, 1), 0)
            is_ctx = jnp.logical_and(row_tile % tiles_per_ex == 0, row < cfg.SC)

            def pick(ref, idx):
                return jnp.where(is_ctx, ref[0, pl.ds(idx, 1), :], ref[1, pl.ds(idx, 1), :])

            x = x_ref[...] + pick(mod_ref, igate) * acc
            if nxt is None:
                return x, acc
            gain_ref, modn_ref = nxt_refs
            n = x * lax.rsqrt(jnp.mean(x * x, axis=-1, keepdims=True) + EPS) * gain_ref[...]
            return x, acc, n * (1.0 + pick(modn_ref, nxt[3])) + pick(modn_ref, nxt[2])
        return epi

    def gated_out(pending, name, a, key, x, mod, igate, nxt=None):
        extras = [(x, pl.BlockSpec((TG, D), lambda i, j, k: (i, j))), (mod, gate_spec)]
        if nxt is not None:
            extras += [(nxt[0], pl.BlockSpec((1, D), lambda i, j, k: (0, 0))), (nxt[1], gate_spec)]
        return mm(pending, name, a, full[key], mode="nn", tm=TG, tn=D, outs=[F32, BF16] + [BF16] * (nxt is not None),
                  epi=resid_epi(igate, nxt), extras=extras, resident=True)

    saved = []
    x = xcat
    h = _norm_mod_fwd(cfg, "norm1_fwd_0", x, w["norm1_g"][0][None], mods[0], 0, 1)
    for l in range(N_LAYERS):
        li = l // 2
        mod = mods[l]
        k_in, k_out, k_ff1, k_ff2 = _layer_weights(l)
        pend = [(k, gather_of([k]), full) for k in _layer_weights(l + 1)] if l + 1 < N_LAYERS else []
        norm2 = (w["norm2_g"][l][None], mod, 3, 4)
        s = {"x0": x, "h": h}
        if l % 2 == 0:
            lgb = jnp.broadcast_to(jax.nn.log_sigmoid(w["ret_decay"][li]).T[:, :, None, None], (H, 2, 8, LANES))
            qg, kg = w["att_q_norm_g"][li][None], w["att_k_norm_g"][li][None]
            s["p"] = mm(pend, f"ab_in_{l}", s["h"], full[k_in], mode="nn", outs=[BF16], tn=768)
            s["rq"], s["rk"], s["aq"], s["ak"] = _prep_fwd(cfg, f"prep_fwd_{l}", s["p"], cosf, sinf, qg, kg)
            s["o"], ret, s["st"] = with_carry(
                functools.partial(_retention_fwd, cfg, f"ret_fwd_{l}", s["rq"], s["rk"], s["p"], lgb),
                keys0[1:3] if l == 0 else [], gather_of, full)
            s["cat"], s["lse"] = with_carry(
                functools.partial(_attention_fwd, cfg, f"att_fwd_{l}", s["aq"], s["ak"], s["p"], ret),
                keys0[3:] if l == 0 else [], gather_of, full)
            s["lgb"], s["qg"], s["kg"] = lgb, qg, kg
            x, s["y1"], s["h2"] = gated_out(pend, f"ab_out_{l}", s["cat"], k_out, x, mod, 2, norm2)
        else:
            s["a"] = mm(pend, f"cm_in_{l}", s["h"], full[k_in], mode="nn", outs=[BF16])
            s["vg"], s["ws"], s["bs"] = w["cm_v_norm_g"][li][None], w["cm_w_s"][li], w["cm_b_s"][li][:, :, None]
            s["m"] = _cm_fwd(cfg, f"cm_fwd_{l}", s["a"], s["vg"], s["ws"], s["bs"])
            x, s["y1"], s["h2"] = gated_out(pend, f"cm_out_{l}", s["m"], k_out, x, mod, 2, norm2)
        s["x1"] = x
        s["r"] = mm(pend, f"ff1_{l}", s["h2"], full[k_ff1], mode="nn", outs=[BF16],
                    epi=lambda acc, row_tile: (jnp.square(jnp.maximum(acc, 0.0)),))
        if l + 1 < N_LAYERS:
            x, s["y2"], h = gated_out(pend, f"ff2_{l}", s["r"], k_ff2, x, mod, 5,
                                      (w["norm1_g"][l + 1][None], mods[l + 1], 0, 1))
        else:
            x, s["y2"] = gated_out(pend, f"ff2_{l}", s["r"], k_ff2, x, mod, 5)
        saved.append(s)

    loss, dx, dy2, dg2 = _loss_grad(cfg, x, tgt, saved[-1]["y2"], mods[-1], 5)

    small = {k: [None] * n for k, n in (("norm1_g", 4), ("norm2_g", 4), ("ret_lg", 2), ("att_q_norm_g", 2),
                                        ("att_k_norm_g", 2), ("cm_v_norm_g", 2), ("cm_w_s", 2), ("cm_b_s", 2))}
    dmods = [None] * N_LAYERS

    for l in reversed(range(N_LAYERS)):
        li = l // 2
        s, mod = saved[l], mods[l]
        k_in, k_out, k_ff1, k_ff2 = _layer_weights(l)
        above = _layer_weights(l + 1) if l + 1 < N_LAYERS else [None] * 4
        da2, big[k_ff2] = with_carry(functools.partial(_ff_bwd, f"ff2_bwd_{l}", dy2, s["r"], full[k_ff2], "w2"),
                                     [above[3], above[1]] if above[0] else [], exchange_of, recv)
        dh2, big[k_ff1] = with_carry(functools.partial(_ff_bwd, f"ff1_bwd_{l}", s["h2"], da2, full[k_ff1], "w1"),
                                     [above[2], above[0]] if above[0] else [], exchange_of, recv)
        dx, dm2, small["norm2_g"][l], do, dg1 = _norm_mod_bwd(
            cfg, f"norm2_bwd_{l}", s["x1"], w["norm2_g"][l][None], mod, 3, 4, dh2, dx, gate=(s["y1"], mod, 2))
        if l % 2 == 0:
            big[k_out] = _mm(f"ab_out_dw_{l}", s["cat"], do, mode="tn", outs=[BF16])
            dcat = _mm(f"ab_out_dx_{l}", do, full[k_out], mode="nt", outs=[BF16])
            d_rq, d_rk, d_rv, d_gt, dlg = with_carry(
                functools.partial(_retention_bwd, cfg, f"ret_bwd_{l}", s["rq"], s["rk"], s["p"], s["o"], s["st"], dcat,
                                  s["lgb"]), [k_ff2, k_ff1] if l == 0 else [], exchange_of, recv)
            d_aq, d_ak, d_av = with_carry(
                functools.partial(_attention_bwd, cfg, f"att_bwd_{l}", s["aq"], s["ak"], s["p"], s["cat"], s["lse"], dcat),
                [k_out] if l == 0 else [], exchange_of, recv)
            dp, dqg, dkg = _prep_bwd(cfg, f"prep_bwd_{l}", s["p"], cosf, sinf, s["qg"], s["kg"],
                                     d_rq, d_rk, d_rv, d_gt, d_aq, d_ak, d_av)
            small["ret_lg"][li] = jnp.sum(dlg[:, :, :, 0, 0], axis=0).T
            small["att_q_norm_g"][li], small["att_k_norm_g"][li] = dqg[0], dkg[0]
            big[k_in] = _mm(f"ab_in_dw_{l}", s["h"], dp, mode="tn", outs=[BF16])
            last = [(k_in, exchange_of([k_in]), recv)] if l == 0 else []
            dh = mm(last, f"ab_in_dx_{l}", dp, full[k_in], mode="nt", outs=[BF16], resident=True)
        else:
            big[k_out] = _mm(f"cm_out_dw_{l}", s["m"], do, mode="tn", outs=[BF16])
            dm = _mm(f"cm_out_dx_{l}", do, full[k_out], mode="nt", outs=[BF16])
            da, dws, dbs, dvg = _cm_bwd(cfg, f"cm_bwd_{l}", s["a"], s["vg"], s["ws"], s["bs"], dm)
            small["cm_w_s"][li], small["cm_b_s"][li], small["cm_v_norm_g"][li] = dws, dbs[:, :, 0], dvg[0]
            (dh, big[k_in]), _ = _ff_bwd(f"cm_in_bwd_{l}", s["h"], da, full[k_in], "w1")
        below = (saved[l - 1]["y2"], mods[l - 1], 5) if l > 0 else None
        dx, dm1, small["norm1_g"][l], *rest = _norm_mod_bwd(
            cfg, f"norm1_bwd_{l}", s["x0"], w["norm1_g"][l][None], mod, 0, 1, dh, dx, gate=below, lat_only=l == 0)
        dmods[l] = jnp.concatenate([dm1, dg1, dm2, dg2], axis=2)
        if l > 0:
            dy2, dg2 = rest
    return loss, dx, recv, small, dmods


N_DEV = 8
N_CHIP = 4
MESH = pl.DeviceIdType.MESH
ANY = pl.BlockSpec(memory_space=pl.ANY)
BIG = {"ab_w_in": 1, "ab_w_out": 0, "cm_w_in": 1, "cm_w_out": 0, "ff_w1": 1, "ff_w2": 0}


class _Carry(NamedTuple):
    kind: str
    srcs: tuple
    axes: tuple


def _place():
    x, y, c = lax.axis_index("x"), lax.axis_index("y"), lax.axis_index("c")
    return x, y, c, [(1 - x, y), (x, 1 - y), (1 - x, 1 - y)]


def _shard_of(ref, axis, s, width):
    start = pl.multiple_of(s * width, LANES)
    if axis == 0:
        return ref.at[pl.ds(start, width), :]
    return ref.at[:, pl.ds(start, width)]


def _carry_out_shapes(carry):
    shapes = []
    for src, axis in zip(carry.srcs, carry.axes):
        shape = list(src.shape)
        if carry.kind == "swap":
            pass
        elif carry.kind == "gather":
            shape[axis] *= N_CHIP
        else:
            shape[axis] //= N_CHIP
            shape = [N_CHIP] + shape
        shapes.append(jax.ShapeDtypeStruct(tuple(shape), src.dtype))
    return shapes


def _carry_copies(carry, srcs, dsts, send_sems, recv_sems, local_sems):
    x, y, c, chips = _place()
    me = 2 * x + y
    copies = []
    if carry.kind == "swap":
        return [pltpu.make_async_remote_copy(
            src_ref=srcs[t], dst_ref=dsts[t], send_sem=send_sems.at[3 * t], recv_sem=recv_sems.at[3 * t],
            device_id=(x, y, 1 - c), device_id_type=MESH) for t in range(len(srcs))]
    for t, axis in enumerate(carry.axes):
        if carry.kind == "gather":
            own = _shard_of(dsts[t], axis, me, srcs[t].shape[axis])
            copies.append(pltpu.make_async_copy(srcs[t], own, local_sems.at[t]))
            parts = [(srcs[t], own)] * 3
        else:
            width = dsts[t].shape[1 + axis]
            copies.append(pltpu.make_async_copy(_shard_of(srcs[t], axis, me, width), dsts[t].at[3], local_sems.at[t]))
            parts = [(_shard_of(srcs[t], axis, 2 * px + py, width), dsts[t].at[j]) for j, (px, py) in enumerate(chips)]
        for j, (px, py) in enumerate(chips):
            copies.append(pltpu.make_async_remote_copy(
                src_ref=parts[j][0], dst_ref=parts[j][1], send_sem=send_sems.at[3 * t + j],
                recv_sem=recv_sems.at[3 * t + j], device_id=(px, py, c), device_id_type=MESH))
    return copies


def _allgather8(name, block, carry=None):
    m_per, n = block.shape
    nc = len(carry.srcs) if carry is not None else 0

    def body(*refs):
        x_ref, out_ref = refs[0], refs[1 + nc]
        send_sems, recv_sems, local_sem = refs[2 + 2 * nc:5 + 2 * nc]
        carried = _carry_copies(carry, refs[1:1 + nc], refs[2 + nc:2 + 2 * nc], *refs[5 + 2 * nc:]) if nc else []
        for cp in carried:
            cp.start()
        x, y, c, chips = _place()
        me, sibling = (x, y, c), (x, y, 1 - c)

        def rows(px, py, pc):
            return out_ref.at[pl.ds((4 * px + 2 * py + pc) * m_per, m_per), :]

        def copy(k, blk, to, src=None):
            return pltpu.make_async_remote_copy(
                src_ref=rows(*blk) if src is None else src, dst_ref=rows(*blk),
                send_sem=send_sems.at[k], recv_sem=recv_sems.at[k], device_id=to, device_id_type=MESH)

        mine = pltpu.make_async_copy(x_ref, rows(*me), local_sem)
        mine.start()
        first = [copy(0, me, sibling, src=x_ref)]
        first += [copy(1 + j, me, (*chip, c), src=x_ref) for j, chip in enumerate(chips)]
        for cp in first:
            cp.start()
        passed = [copy(4 + j, (*chip, c), sibling) for j, chip in enumerate(chips)]
        for j, chip in enumerate(chips):
            copy(1 + j, (*chip, c), me).wait_recv()
            passed[j].start()
        copy(0, sibling, me).wait_recv()
        for j, chip in enumerate(chips):
            copy(4 + j, (*chip, 1 - c), me).wait_recv()
        for cp in first + passed:
            cp.wait_send()
        mine.wait()
        for cp in carried:
            cp.wait()

    res = pl.pallas_call(
        body, name=name,
        out_shape=[jax.ShapeDtypeStruct((N_DEV * m_per, n), block.dtype)] + (_carry_out_shapes(carry) if nc else []),
        in_specs=[pl.BlockSpec(memory_space=pltpu.VMEM)] + [ANY] * nc,
        out_specs=[pl.BlockSpec(memory_space=pltpu.VMEM)] + [ANY] * nc,
        scratch_shapes=[pltpu.SemaphoreType.DMA((7,)), pltpu.SemaphoreType.DMA((7,)), pltpu.SemaphoreType.DMA]
        + _carry_scratch(nc),
        compiler_params=pltpu.CompilerParams(vmem_limit_bytes=VMEM_LIMIT),
    )(block, *(carry.srcs if nc else ()))
    return (res[0], res[1:]) if nc else res[0]


def _rows_view(a):
    if a.ndim == 1:
        return a.reshape(1, a.shape[0])
    return a.reshape(-1, a.shape[-1])


def _row_tile(rows, cols, target_elems=1 << 17):
    tr = rows
    while tr % 16 == 0 and tr * cols > target_elems:
        tr //= 2
    return tr


def _sum_leading(name, a):
    n, rows, cols = a.shape
    tr = _row_tile(rows, cols * n, target_elems=1 << 20)

    def body(a_ref, o_ref):
        acc = a_ref[0].astype(F32)
        for i in range(1, n):
            acc = acc + a_ref[i].astype(F32)
        o_ref[...] = acc

    return pl.pallas_call(
        body, name=name, grid=(rows // tr,),
        in_specs=[pl.BlockSpec((n, tr, cols), lambda i: (0, i, 0))],
        out_specs=pl.BlockSpec((tr, cols), lambda i: (i, 0)),
        out_shape=jax.ShapeDtypeStruct((rows, cols), F32),
        compiler_params=_params(("parallel",)),
    )(a)


def _sum_parts_layers(name, parts):
    n_layers = len(parts)
    n, rows, cols = parts[0].shape
    tr = _row_tile(rows, cols * n, target_elems=1 << 20)

    def body(*refs):
        o_ref = refs[n_layers]
        layer = pl.program_id(0)
        for k in range(n_layers):
            @pl.when(layer == k)
            def _():
                acc = refs[k][0].astype(F32)
                for i in range(1, n):
                    acc = acc + refs[k][i].astype(F32)
                o_ref[...] = acc

    def in_spec(k):
        return pl.BlockSpec((n, tr, cols), lambda l, i: (0, jnp.where(l == k, i, 0), 0))

    return pl.pallas_call(
        body, name=name, grid=(n_layers, rows // tr),
        in_specs=[in_spec(k) for k in range(n_layers)],
        out_specs=pl.BlockSpec((None, tr, cols), lambda l, i: (l, i, 0)),
        out_shape=jax.ShapeDtypeStruct((n_layers, rows, cols), F32),
        compiler_params=_params(("arbitrary", "arbitrary")),
    )(*parts)


def _silu_rows(name, x):
    def body(x_ref, o_ref):
        v = x_ref[...]
        o_ref[...] = v * jax.nn.sigmoid(v)

    return pl.pallas_call(body, name=name, out_shape=jax.ShapeDtypeStruct(x.shape, F32))(x)


def _silu_bwd_rows(name, x, dy):
    def body(x_ref, dy_ref, o_ref):
        v = x_ref[...]
        sg = jax.nn.sigmoid(v)
        o_ref[...] = dy_ref[...] * (sg * (1.0 + v * (1.0 - sg)))

    return pl.pallas_call(body, name=name, out_shape=jax.ShapeDtypeStruct(x.shape, F32))(x, dy)


def _adamw(name, w, g_parts, m, v):
    shape = w.shape
    w2, m2, v2 = _rows_view(w), _rows_view(m), _rows_view(v)
    gs = [_rows_view(g) for g in g_parts]
    rows, cols = w2.shape
    tr = _row_tile(rows, cols)
    ng = len(gs)

    def body(*refs):
        w_ref, m_ref, v_ref = refs[0], refs[1], refs[2]
        g_refs = refs[3:3 + ng]
        g_out, d_out, m_out, v_out = refs[3 + ng:]
        g = g_refs[0][...]
        for r in g_refs[1:]:
            g = g + r[...]
        m1 = ADAM_B1 * m_ref[...] + (1.0 - ADAM_B1) * g
        v1 = ADAM_B2 * v_ref[...] + (1.0 - ADAM_B2) * jnp.square(g)
        m_hat = m1 / (1.0 - ADAM_B1 ** ADAM_STEP)
        v_hat = v1 / (1.0 - ADAM_B2 ** ADAM_STEP)
        g_out[...] = g
        d_out[...] = -ADAM_LR * (m_hat / (jnp.sqrt(v_hat) + ADAM_EPS) + ADAM_WD * w_ref[...])
        m_out[...] = m1
        v_out[...] = v1

    spec = pl.BlockSpec((tr, cols), lambda i: (i, 0))
    res = pl.pallas_call(
        body, name=name, grid=(rows // tr,), in_specs=[spec] * (3 + ng), out_specs=[spec] * 4,
        out_shape=[jax.ShapeDtypeStruct((rows, cols), F32)] * 4,
        compiler_params=_params(("parallel",)),
    )(w2, m2, v2, *gs)
    return tuple(r.reshape(shape) for r in res)


MOD_ROWS = 48


def kernel(x, c, ctx, c_ctx, mod_w, mod_b, norm1_g, norm2_g, ab_w_in, ab_w_out, ret_decay, att_q_norm_g, att_k_norm_g, cm_w_in, cm_v_norm_g, cm_w_s, cm_b_s, cm_w_out, ff_w1, ff_w2, loss_target, m_c_ctx, m_mod_w, m_mod_b, m_norm1_g, m_norm2_g, m_ab_w_in, m_ab_w_out, m_ret_decay, m_att_q_norm_g, m_att_k_norm_g, m_cm_w_in, m_cm_v_norm_g, m_cm_w_s, m_cm_b_s, m_cm_w_out, m_ff_w1, m_ff_w2, v_c_ctx, v_mod_w, v_mod_b, v_norm1_g, v_norm2_g, v_ab_w_in, v_ab_w_out, v_ret_decay, v_att_q_norm_g, v_att_k_norm_g, v_cm_w_in, v_cm_v_norm_g, v_cm_w_s, v_cm_b_s, v_cm_w_out, v_ff_w1, v_ff_w2):
    B, SL, D = x.shape
    cfg = Cfg(B=B, SC=ctx.shape[1], SL=SL, D=D, FF=ff_w1.shape[2] * N_CHIP)
    L = N_LAYERS
    n_ex = B * N_DEV
    mcols = mod_w.shape[2]
    weights = dict(c_ctx=c_ctx, mod_w=mod_w, mod_b=mod_b, norm1_g=norm1_g, norm2_g=norm2_g, ab_w_in=ab_w_in,
                   ab_w_out=ab_w_out, ret_decay=ret_decay, att_q_norm_g=att_q_norm_g, att_k_norm_g=att_k_norm_g,
                   cm_w_in=cm_w_in, cm_v_norm_g=cm_v_norm_g, cm_w_s=cm_w_s, cm_b_s=cm_b_s, cm_w_out=cm_w_out,
                   ff_w1=ff_w1, ff_w2=ff_w2)
    m_in = dict(c_ctx=m_c_ctx, mod_w=m_mod_w, mod_b=m_mod_b, norm1_g=m_norm1_g, norm2_g=m_norm2_g, ab_w_in=m_ab_w_in,
                ab_w_out=m_ab_w_out, ret_decay=m_ret_decay, att_q_norm_g=m_att_q_norm_g, att_k_norm_g=m_att_k_norm_g,
                cm_w_in=m_cm_w_in, cm_v_norm_g=m_cm_v_norm_g, cm_w_s=m_cm_w_s, cm_b_s=m_cm_b_s, cm_w_out=m_cm_w_out,
                ff_w1=m_ff_w1, ff_w2=m_ff_w2)
    v_in = dict(c_ctx=v_c_ctx, mod_w=v_mod_w, mod_b=v_mod_b, norm1_g=v_norm1_g, norm2_g=v_norm2_g, ab_w_in=v_ab_w_in,
                ab_w_out=v_ab_w_out, ret_decay=v_ret_decay, att_q_norm_g=v_att_q_norm_g, att_k_norm_g=v_att_k_norm_g,
                cm_w_in=v_cm_w_in, cm_v_norm_g=v_cm_v_norm_g, cm_w_s=v_cm_w_s, cm_b_s=v_cm_b_s, cm_w_out=v_cm_w_out,
                ff_w1=v_ff_w1, ff_w2=v_ff_w2)
    xi, yi, ci = lax.axis_index("x"), lax.axis_index("y"), lax.axis_index("c")
    chip = 2 * xi + yi
    dev = 2 * chip + ci

    shards = {n: [weights[n][i].astype(BF16) for i in range(weights[n].shape[0])] for n in BIG}
    vgw = cm_v_norm_g.shape[1]
    blk = jnp.zeros((8, D), F32).at[:B].set(c).at[B:B + 2, :vgw].set(cm_v_norm_g)
    g0 = _allgather8("gather_c", blk).reshape(N_DEV, 8, D)
    c_all = g0[:, :B].reshape(n_ex, D)
    vg_full = jnp.concatenate([g0[2 * s, B:B + 2, :vgw] for s in range(N_CHIP)], axis=-1)

    pre = jnp.zeros((MOD_ROWS, D), F32).at[:n_ex].set(c_all).at[n_ex].set(c_ctx)
    act = _silu_rows("silu_c", pre)
    mpart = jnp.stack([_mm(f"mod_fwd_{l}", act, mod_w, mode="nn", layer=l, outs=[F32], tn=mcols) for l in range(L)])
    g1, (first_weight,) = _allgather8("gather_mod", mpart.reshape(L * MOD_ROWS, mcols),
                                      carry=_Carry("gather", (shards["ab_w_in"][0],), (BIG["ab_w_in"],)))
    g1 = g1.reshape(N_DEV, L, MOD_ROWS, mcols)
    mod_all = jnp.concatenate([g1[2 * s] for s in range(N_CHIP)], axis=-1) + mod_b[:, None, :]
    mod_lat = lax.dynamic_slice_in_dim(mod_all, dev * B, B, axis=1)
    mod_ctx = jnp.broadcast_to(mod_all[:, n_ex][:, None], mod_lat.shape)
    mods = jnp.stack([mod_ctx, mod_lat], axis=2).reshape(L, B, 2, 6, D)

    w = dict(norm1_g=norm1_g, norm2_g=norm2_g, ret_decay=ret_decay, att_q_norm_g=att_q_norm_g,
             att_k_norm_g=att_k_norm_g, cm_v_norm_g=vg_full, cm_w_s=cm_w_s, cm_b_s=cm_b_s)
    xcat = jnp.concatenate([ctx, x], axis=1).reshape(cfg.T, D)
    loss_local, dx_lat, recv, small, dmods = _local_step(cfg, xcat, loss_target.reshape(B * SL, D), mods, shards, w,
                                                         first_weight)
    loss = lax.psum(loss_local, ("x", "y", "c"))
    grad_x = dx_lat.reshape(B, SL, D)

    part = [_sum_parts_layers(f"sum_{n}", [recv[(n, i)] for i in range(weights[n].shape[0])]) for n in BIG]

    dmod = jnp.stack(dmods).reshape(L, B, 2, 6 * D)
    dmod_lat = dmod[:, :, 1]
    dmod_ctx = jnp.sum(dmod[:, :, 0], axis=1)
    d_ret = jnp.stack(small["ret_lg"]) * jax.nn.sigmoid(-ret_decay)
    summed = [dmod_ctx.reshape(-1), jnp.stack(small["norm1_g"]).reshape(-1), jnp.stack(small["norm2_g"]).reshape(-1),
              jnp.stack(small["cm_v_norm_g"]).reshape(-1), jnp.stack(small["cm_w_s"]).reshape(-1),
              jnp.stack(small["cm_b_s"]).reshape(-1), jnp.stack(small["att_q_norm_g"]).reshape(-1),
              jnp.stack(small["att_k_norm_g"]).reshape(-1), d_ret.reshape(-1)]
    sizes = [int(a.shape[0]) for a in summed]
    flat = jnp.concatenate(summed + [dmod_lat.reshape(-1)])
    n_sum = sum(sizes)
    n_sum_rows = -(-n_sum // D)
    lat_rows = (L * B * 6 * D) // D
    pack_rows = -(-(n_sum_rows + lat_rows) // 8) * 8
    packed = jnp.zeros((pack_rows * D,), F32).at[:n_sum].set(flat[:n_sum])
    packed = packed.at[n_sum_rows * D:(n_sum_rows + lat_rows) * D].set(flat[n_sum:]).reshape(pack_rows, D)
    g2, other = _allgather8("gather_small", packed, carry=_Carry("swap", tuple(part), (0,) * len(part)))
    g2 = g2.reshape(N_DEV, pack_rows, D)
    tot = _sum_leading("sum_small", g2[:, :n_sum_rows]).reshape(-1)
    pieces, off = [], 0
    for sz in sizes:
        pieces.append(tot[off:off + sz])
        off += sz
    dmod_ctx_t, g_n1, g_n2, g_vg, g_ws, g_bs, g_qg, g_kg, g_rd = pieces
    dmod_ctx_t = dmod_ctx_t.reshape(L, 6 * D)
    dmod_lat_all = g2[:, n_sum_rows:n_sum_rows + lat_rows].reshape(N_DEV, L, B, 6 * D)
    dmod_rows = jnp.zeros((L, MOD_ROWS, 6 * D), F32)
    dmod_rows = dmod_rows.at[:, :n_ex].set(jnp.transpose(dmod_lat_all, (1, 0, 2, 3)).reshape(L, n_ex, 6 * D))
    dmod_rows = dmod_rows.at[:, n_ex].set(dmod_ctx_t)
    g_mod_b = _sum_leading("sum_mod_b", jnp.transpose(dmod_rows, (1, 0, 2)))
    dmod_mine = lax.dynamic_slice_in_dim(dmod_rows, chip * mcols, mcols, axis=2)
    g_mod_w = jnp.stack([_mm(f"mod_dw_{l}", act, dmod_mine[l], mode="tn", outs=[F32], tn=mcols) for l in range(L)])
    ctx8 = jnp.zeros((L, 8, mcols), F32).at[:, 0].set(dmod_mine[:, n_ex])
    dcc = [_mm(f"mod_dctx_{l}", ctx8[l], mod_w, mode="nt", layer=l, outs=[F32], tk=mcols) for l in range(L)]
    dcc = _sum_leading("sum_dctx_layers", jnp.stack(dcc))
    g3 = _allgather8("gather_dctx", dcc).reshape(N_DEV, 8, D)
    dcc_t = _sum_leading("sum_dctx_chips", g3[0::2])[0:1]
    g_c_ctx = _silu_bwd_rows("silu_bwd_cctx", c_ctx[None], dcc_t)[0]

    vg_mine = lax.dynamic_slice_in_dim(g_vg.reshape(2, -1), chip * vgw, vgw, axis=1)
    small_g = dict(c_ctx=g_c_ctx, mod_w=g_mod_w, mod_b=g_mod_b, norm1_g=g_n1.reshape(norm1_g.shape),
                   norm2_g=g_n2.reshape(norm2_g.shape), ret_decay=g_rd.reshape(ret_decay.shape),
                   att_q_norm_g=g_qg.reshape(att_q_norm_g.shape), att_k_norm_g=g_kg.reshape(att_k_norm_g.shape),
                   cm_v_norm_g=vg_mine, cm_w_s=g_ws.reshape(cm_w_s.shape), cm_b_s=g_bs.reshape(cm_b_s.shape))
    out = {}
    for n, g in small_g.items():
        out[n] = _adamw(f"adamw_{n}", weights[n], [g], m_in[n], v_in[n])
    for n, p_mine, p_other in zip(BIG, part, other):
        out[n] = _adamw(f"adamw_{n}", weights[n], [p_mine, p_other], m_in[n], v_in[n])

    order = list(weights)
    return (loss, grad_x, *[out[n][0] for n in order], *[out[n][1] for n in order],
            *[out[n][2] for n in order], *[out[n][3] for n in order])
```

```python
import functools
import math
from typing import NamedTuple

import jax
import jax.numpy as jnp
from jax import lax
from jax.experimental import pallas as pl
from jax.experimental.pallas import tpu as pltpu

F32 = jnp.float32
BF16 = jnp.bfloat16
EPS = 1e-6
ROPE_BASE = 10000.0
LANES = 128
CHUNK = 128
N_LAYERS = 4
VMEM_LIMIT = 56 * 1024 * 1024

ADAM_LR = 0.001
ADAM_B1 = 0.9
ADAM_B2 = 0.999
ADAM_EPS = 1e-08
ADAM_WD = 0.01
ADAM_STEP = 10


class Cfg(NamedTuple):
    B: int = 4
    SC: int = 256
    SL: int = 2048
    D: int = 1024
    FF: int = 4096
    GRID_W: int = 64
    H: int = 4
    KV: int = 2
    CMW: int = 1024
    CMG: int = 8

    @property
    def S(self):
        return self.SC + self.SL

    @property
    def T(self):
        return self.B * self.S

    @property
    def TM(self):
        return self.SC

    @property
    def TPE(self):
        return self.S // self.SC

    @property
    def ABW(self):
        return (5 * self.H + 2 * self.KV) * CHUNK


def _tile(dim, pref):
    t = min(dim, pref)
    while dim % t:
        t -= LANES
    return t


def _dot(a, b):
    return lax.dot_general(a, b, (((1,), (0,)), ((), ())), preferred_element_type=F32)


def _dot_nt(a, b):
    return lax.dot_general(a, b, (((1,), (1,)), ((), ())), preferred_element_type=F32)


def _dot_tn(a, b):
    return lax.dot_general(a, b, (((0,), (0,)), ((), ())), preferred_element_type=F32)


def _params(sem, vmem=VMEM_LIMIT):
    return pltpu.CompilerParams(dimension_semantics=sem, vmem_limit_bytes=vmem)


def _mod_index(cfg):
    tpe = cfg.TPE
    return lambda i: (i // tpe, jnp.minimum(i % tpe, 1), 0, 0)


def _mm(name, a, b, *, mode, outs, tm=1024, tn=1024, tk=1024, layer=None, epi=None, extras=(), carry=None,
        resident=False):
    bshape = b.shape[1:] if layer is not None else b.shape
    if mode == "nn":
        (M, K), N = a.shape, bshape[1]
    elif mode == "nt":
        (M, K), N = a.shape, bshape[0]
    else:
        (K, M), N = a.shape, bshape[1]
    tm, tn, tk = _tile(M, tm), _tile(N, tn), _tile(K, tk)
    kchunk = tk
    if resident:
        tk = K
    nk = K // tk
    a_spec = (pl.BlockSpec((tk, tm), lambda i, j, k: (k, i)) if mode == "tn"
              else pl.BlockSpec((tm, tk), lambda i, j, k: (i, k)))
    if mode == "nt":
        bblk, bidx = (tn, tk), (lambda i, j, k: (j, k))
    else:
        bblk, bidx = (tk, tn), (lambda i, j, k: (k, j))
    if layer is not None:
        b_spec = pl.BlockSpec((None,) + bblk, lambda i, j, k: (layer,) + bidx(i, j, k))
    elif resident:
        b_spec = pl.BlockSpec(bblk, bidx, pipeline_mode=pl.Buffered(1))
    else:
        b_spec = pl.BlockSpec(bblk, bidx)
    ne, no = len(extras), len(outs)
    nc = len(carry.srcs) if carry is not None else 0
    dot = {"nn": _dot, "nt": _dot_nt, "tn": _dot_tn}[mode]
    grid = (M // tm, N // tn, nk)

    def body(*refs):
        a_ref, b_ref = refs[0], refs[1]
        ex, out_refs = refs[2:2 + ne], refs[2 + ne + nc:2 + ne + nc + no]
        row_tile = pl.program_id(0)

        if nc:
            step = (pl.program_id(0) * grid[1] + pl.program_id(1)) * grid[2] + pl.program_id(2)
            c_src = refs[2 + ne:2 + ne + nc]
            c_dst = refs[2 + ne + nc + no:2 + ne + 2 * nc + no]
            sems = refs[2 + ne + 2 * nc + no:2 + ne + 2 * nc + no + 3]

            @pl.when(step == 0)
            def _():
                for cp in _carry_copies(carry, c_src, c_dst, *sems):
                    cp.start()

        def finish(acc):
            res = epi(acc, row_tile, *ex) if epi is not None else (acc,)
            for r, o in zip(res, out_refs):
                o[...] = r.astype(o.dtype)

        if resident and K > kchunk:
            part = None
            for c in range(K // kchunk):
                ks = pl.ds(c * kchunk, kchunk)
                b_chunk = b_ref[:, ks] if mode == "nt" else b_ref[ks, :]
                term = dot(a_ref[:, ks].astype(BF16), b_chunk.astype(BF16))
                part = term if part is None else part + term
        else:
            part = dot(a_ref[...].astype(BF16), b_ref[...].astype(BF16))
        if nk == 1:
            finish(part)
        else:
            acc_ref = refs[-1]
            k = pl.program_id(2)

            @pl.when(k == 0)
            def _():
                acc_ref[...] = part

            @pl.when(k > 0)
            def _():
                acc_ref[...] += part

            @pl.when(k == nk - 1)
            def _():
                finish(acc_ref[...])

        if nc:
            @pl.when(step == grid[0] * grid[1] * grid[2] - 1)
            def _():
                for cp in _carry_copies(carry, c_src, c_dst, *sems):
                    cp.wait()

    scratch = [pltpu.SemaphoreType.DMA((3 * nc,)), pltpu.SemaphoreType.DMA((3 * nc,)),
               pltpu.SemaphoreType.DMA((nc,))] if nc else []
    if nk > 1:
        scratch.append(pltpu.VMEM((tm, tn), F32))
    res = pl.pallas_call(
        body, name=name, grid=grid,
        in_specs=[a_spec, b_spec] + [s for _, s in extras] + [ANY] * nc,
        out_specs=[pl.BlockSpec((tm, tn), lambda i, j, k: (i, j)) for _ in outs] + [ANY] * nc,
        out_shape=[jax.ShapeDtypeStruct((M, N), d) for d in outs] + (_carry_out_shapes(carry) if nc else []),
        scratch_shapes=scratch,
        compiler_params=_params(("arbitrary",) * 3 if nc else ("parallel", "parallel", "arbitrary")),
    )(a, b, *[x for x, _ in extras], *(carry.srcs if nc else ()))
    if nc:
        return (res[0] if no == 1 else res[:no]), res[no:]
    return res[0] if no == 1 else res


def _ff_bwd(name, first, second, weight, kind, carry=None):
    (T, D), FF = first.shape, second.shape[1]
    halves = 2
    tm = _tile(T, 1024)
    ffh = FF // halves if kind == "w2" else FF
    dh_cols = D if kind == "w2" else D // halves
    cw = _tile(ffh, 1024)
    n_steps = T // tm
    nc = len(carry.srcs) if carry is not None else 0

    def body(*refs):
        a_ref, b_ref, w_ref = refs[:3]
        c_src = refs[3:3 + nc]
        x_ref, dw_ref = refs[3 + nc:5 + nc]
        c_dst, sems = refs[5 + nc:5 + 2 * nc], refs[5 + 2 * nc:5 + 2 * nc + 3] if nc else ()
        acc_ref = refs[-1]
        i = pl.program_id(1)
        step = pl.program_id(0) * n_steps + i
        if nc:
            _carry_begin(carry, c_src, c_dst, sems, step)
        a = a_ref[...]
        dh = None
        for c in range(ffh // cw):
            cols = pl.ds(c * cw, cw)
            if kind == "w2":
                r = b_ref[:, cols]
                x_ref[:, cols] = (_dot_nt(a, w_ref[cols, :]) * (2.0 * jnp.sqrt(r.astype(F32)))).astype(BF16)
                part, dst = _dot_tn(r, a), acc_ref.at[cols, :]
            else:
                da = b_ref[:, cols]
                term = _dot_nt(da, w_ref[:, cols])
                dh = term if dh is None else dh + term
                part, dst = _dot_tn(a, da), acc_ref.at[:, cols]

            @pl.when(i == 0)
            def _():
                dst[...] = part

            @pl.when(i > 0)
            def _():
                dst[...] += part

        if kind == "w1":
            x_ref[...] = dh.astype(BF16)

        @pl.when(i == n_steps - 1)
        def _():
            dw_ref[...] = acc_ref[...].astype(BF16)

        if nc:
            _carry_end(carry, c_src, c_dst, sems, step, halves * n_steps)

    if kind == "w2":
        wshape = (ffh, D)
        a_spec = pl.BlockSpec((tm, D), lambda j, i: (i, 0))
        b_spec = pl.BlockSpec((tm, ffh), lambda j, i: (i, j))
        x_spec, x_cols = pl.BlockSpec((tm, ffh), lambda j, i: (i, j)), FF
    else:
        wshape = (dh_cols, FF)
        a_spec = pl.BlockSpec((tm, dh_cols), lambda j, i: (i, j))
        b_spec = pl.BlockSpec((tm, FF), lambda j, i: (i, 0))
        x_spec, x_cols = pl.BlockSpec((tm, dh_cols), lambda j, i: (i, j)), D
    wspec = pl.BlockSpec(wshape, lambda j, i: (j, 0), pipeline_mode=pl.Buffered(1))
    res = pl.pallas_call(
        body, name=name, grid=(halves, n_steps),
        in_specs=[a_spec, b_spec, wspec] + [ANY] * nc,
        out_specs=[x_spec, wspec] + [ANY] * nc,
        out_shape=[jax.ShapeDtypeStruct((T, x_cols), BF16), jax.ShapeDtypeStruct(weight.shape, BF16)]
        + (_carry_out_shapes(carry) if nc else []),
        scratch_shapes=_carry_scratch(nc) + [pltpu.VMEM(wshape, F32)],
        compiler_params=_params(("arbitrary", "arbitrary")),
    )(first, second, weight, *(carry.srcs if nc else ()))
    return res[:2], res[2:]


def _norm_mod_fwd(cfg, name, x, gain, mod, ish, isc):
    T, D, TM = cfg.T, cfg.D, cfg.TM

    def body(x_ref, g_ref, mod_ref, h_ref):
        x = x_ref[...]
        rstd = lax.rsqrt(jnp.mean(x * x, axis=-1, keepdims=True) + EPS)
        n = x * rstd * g_ref[...]
        h = n * (1.0 + mod_ref[pl.ds(isc, 1), :]) + mod_ref[pl.ds(ish, 1), :]
        h_ref[...] = h.astype(BF16)

    return pl.pallas_call(
        body, name=name, grid=(T // TM,),
        in_specs=[pl.BlockSpec((TM, D), lambda i: (i, 0)), pl.BlockSpec((1, D), lambda i: (0, 0)),
                  pl.BlockSpec((None, None, 6, D), _mod_index(cfg))],
        out_specs=pl.BlockSpec((TM, D), lambda i: (i, 0)),
        out_shape=jax.ShapeDtypeStruct((T, D), BF16),
        compiler_params=_params(("parallel",)),
    )(x, gain, mod)


def _norm_mod_bwd(cfg, name, x, gain, mod, ish, isc, dh, dres, gate=None, lat_only=False):
    T, D, TM, TPE = cfg.T, cfg.D, cfg.TM, cfg.TPE
    ng = 2 if gate is not None else 0
    dx_spec = (pl.BlockSpec((TM, D), lambda i: ((i // TPE) * (TPE - 1) + jnp.maximum(i % TPE - 1, 0), 0)) if lat_only
               else pl.BlockSpec((TM, D), lambda i: (i, 0)))
    dx_rows = cfg.B * cfg.SL if lat_only else T

    def body(*refs):
        x_ref, g_ref, mod_ref, dh_ref, dres_ref = refs[:5]
        dx_ref, dmod_ref, dgain_ref = refs[5 + ng:8 + ng]
        i = pl.program_id(0)
        t = i % TPE
        x = x_ref[...]
        g = g_ref[...]
        dh = dh_ref[...].astype(F32)
        rstd = lax.rsqrt(jnp.mean(x * x, axis=-1, keepdims=True) + EPS)
        xhat = x * rstd
        dn = dh * (1.0 + mod_ref[pl.ds(isc, 1), :])
        dsh = jnp.sum(dh, axis=0, keepdims=True)
        dsc = jnp.sum(dh * (xhat * g), axis=0, keepdims=True)
        dgain = jnp.sum(dn * xhat, axis=0, keepdims=True)
        dxh = dn * g
        dx = rstd * (dxh - xhat * jnp.mean(dxh * xhat, axis=-1, keepdims=True)) + dres_ref[...]
        dx_ref[...] = dx
        sums = [(dmod_ref.at[pl.ds(0, 1), :], dsh), (dmod_ref.at[pl.ds(1, 1), :], dsc)]
        if ng:
            y_ref, gmod_ref = refs[5:7]
            dy_ref, dgate_ref = refs[8 + ng:]
            dy_ref[...] = (dx * gmod_ref[pl.ds(gate[2], 1), :]).astype(BF16)
            sums.append((dgate_ref, jnp.sum(dx * y_ref[...].astype(F32), axis=0, keepdims=True)))

        @pl.when(t <= 1)
        def _():
            for ref, val in sums:
                ref[...] = val

        @pl.when(t > 1)
        def _():
            for ref, val in sums:
                ref[...] += val

        @pl.when(i == 0)
        def _():
            dgain_ref[...] = dgain

        @pl.when(i > 0)
        def _():
            dgain_ref[...] += dgain

    tok = pl.BlockSpec((TM, D), lambda i: (i, 0))
    mod_spec = pl.BlockSpec((None, None, 6, D), _mod_index(cfg))
    res = pl.pallas_call(
        body, name=name, grid=(T // TM,),
        in_specs=[tok, pl.BlockSpec((1, D), lambda i: (0, 0)), mod_spec, tok, tok] + ([tok, mod_spec] if ng else []),
        out_specs=[dx_spec, pl.BlockSpec((None, None, 2, D), _mod_index(cfg)), pl.BlockSpec((1, D), lambda i: (0, 0))]
        + ([tok, pl.BlockSpec((None, None, 1, D), _mod_index(cfg))] if ng else []),
        out_shape=[jax.ShapeDtypeStruct((dx_rows, D), F32), jax.ShapeDtypeStruct((cfg.B, 2, 2, D), F32),
                   jax.ShapeDtypeStruct((1, D), F32)]
        + ([jax.ShapeDtypeStruct((T, D), BF16), jax.ShapeDtypeStruct((cfg.B, 2, 1, D), F32)] if ng else []),
        compiler_params=_params(("arbitrary",)),
    )(x, gain, mod, dh, dres, *(gate[:2] if ng else ()))
    return res


def _loss_grad(cfg, x, tgt, y, mod, igate):
    T, D, TM, TPE = cfg.T, cfg.D, cfg.TM, cfg.TPE

    def body(x_ref, t_ref, y_ref, mod_ref, dx_ref, loss_ref, dy_ref, dg_ref):
        i = pl.program_id(0)
        t = i % TPE

        @pl.when(i == 0)
        def _():
            loss_ref[...] = jnp.zeros_like(loss_ref)

        @pl.when(t == 0)
        def _():
            dx_ref[...] = jnp.zeros_like(dx_ref)
            dy_ref[...] = jnp.zeros_like(dy_ref)
            dg_ref[...] = jnp.zeros_like(dg_ref)

        @pl.when(t > 0)
        def _():
            err = x_ref[...] - t_ref[...]
            dx = err * (1.0 / D)
            dx_ref[...] = dx
            loss_ref[...] += 0.5 * jnp.sum(jnp.mean(err * err, axis=-1, keepdims=True), axis=0, keepdims=True)
            dy_ref[...] = (dx * mod_ref[pl.ds(igate, 1), :]).astype(BF16)
            dg = jnp.sum(dx * y_ref[...].astype(F32), axis=0, keepdims=True)

            @pl.when(t == 1)
            def _():
                dg_ref[...] = dg

            @pl.when(t > 1)
            def _():
                dg_ref[...] += dg

    tok = pl.BlockSpec((TM, D), lambda i: (i, 0))
    tgt_spec = pl.BlockSpec((TM, D), lambda i: ((i // TPE) * (TPE - 1) + jnp.maximum(i % TPE - 1, 0), 0))
    dx, loss, dy, dg = pl.pallas_call(
        body, name="loss_grad", grid=(T // TM,),
        in_specs=[tok, tgt_spec, tok, pl.BlockSpec((None, None, 6, D), _mod_index(cfg))],
        out_specs=[tok, pl.BlockSpec((8, LANES), lambda i: (0, 0)), tok,
                   pl.BlockSpec((None, None, 1, D), _mod_index(cfg))],
        out_shape=[jax.ShapeDtypeStruct((T, D), F32), jax.ShapeDtypeStruct((8, LANES), F32),
                   jax.ShapeDtypeStruct((T, D), BF16), jax.ShapeDtypeStruct((cfg.B, 2, 1, D), F32)],
        compiler_params=_params(("arbitrary",)),
    )(x, tgt, y, mod)
    return loss[0, 0], dx, dy, dg


def _rope_tables(cfg):
    rows = cfg.SL // cfg.GRID_W
    row = jnp.repeat(jnp.arange(rows, dtype=F32), cfg.GRID_W)
    col = jnp.tile(jnp.arange(cfg.GRID_W, dtype=F32), rows)
    n_freq = CHUNK // 4
    inv = ROPE_BASE ** (-jnp.arange(n_freq, dtype=F32) / n_freq)
    ang = jnp.concatenate([row[:, None] * inv[None, :], col[:, None] * inv[None, :]], axis=-1)
    cos, sin = jnp.cos(ang), jnp.sin(ang)
    cosf = jnp.concatenate([jnp.ones((cfg.SC, CHUNK), F32), jnp.concatenate([cos, cos], axis=-1)], axis=0)
    sinf = jnp.concatenate([jnp.zeros((cfg.SC, CHUNK), F32), jnp.concatenate([-sin, sin], axis=-1)], axis=0)
    return cosf, sinf


def _rope(x, cosf, sinf):
    return x * cosf + pltpu.roll(x, CHUNK // 2, 1) * sinf


def _irope(dy, cosf, sinf):
    return dy * cosf - pltpu.roll(dy, CHUNK // 2, 1) * sinf


def _prep_fwd(cfg, name, p, cosf, sinf, qg, kg):
    T, TM, TPE, H, KV = cfg.T, cfg.TM, cfg.TPE, cfg.H, cfg.KV
    HW = H * CHUNK
    kscale = CHUNK ** -0.5

    def body(p_ref, c_ref, s_ref, qg_ref, kg_ref, rq_ref, rk_ref, aq_ref, ak_ref):
        cosf, sinf = c_ref[...], s_ref[...]

        def normed(x, g):
            return x * lax.rsqrt(jnp.mean(x * x, axis=-1, keepdims=True) + EPS) * g

        def seg(col):
            return p_ref[:, pl.ds(col, CHUNK)].astype(F32)

        for h in range(H):
            sl = pl.ds(h * CHUNK, CHUNK)
            rq_ref[:, sl] = _rope(seg(h * CHUNK), cosf, sinf)
            rk_ref[:, sl] = _rope(seg(HW + h * CHUNK), cosf, sinf) * kscale
            aq_ref[:, sl] = (_rope(normed(seg(4 * HW + h * CHUNK), qg_ref[...]), cosf, sinf) * ATT_SCALE).astype(BF16)
        for h in range(KV):
            ak_ref[:, pl.ds(h * CHUNK, CHUNK)] = _rope(
                normed(seg(5 * HW + h * CHUNK), kg_ref[...]), cosf, sinf).astype(BF16)

    tab = pl.BlockSpec((TM, CHUNK), lambda i: (i % TPE, 0))
    vec = pl.BlockSpec((1, CHUNK), lambda i: (0, 0))
    return pl.pallas_call(
        body, name=name, grid=(T // TM,),
        in_specs=[pl.BlockSpec((TM, cfg.ABW), lambda i: (i, 0)), tab, tab, vec, vec],
        out_specs=[pl.BlockSpec((TM, HW), lambda i: (i, 0))] * 3 + [pl.BlockSpec((TM, KV * CHUNK), lambda i: (i, 0))],
        out_shape=[jax.ShapeDtypeStruct((T, HW), F32), jax.ShapeDtypeStruct((T, HW), F32),
                   jax.ShapeDtypeStruct((T, HW), BF16), jax.ShapeDtypeStruct((T, KV * CHUNK), BF16)],
        compiler_params=_params(("parallel",)),
    )(p, cosf, sinf, qg, kg)


def _prep_bwd(cfg, name, p, cosf, sinf, qg, kg, d_rq, d_rk, d_rv, d_gate, d_aq, d_ak, d_av):
    T, TM, TPE, H, KV = cfg.T, cfg.TM, cfg.TPE, cfg.H, cfg.KV
    HW = H * CHUNK
    kscale = CHUNK ** -0.5

    def body(p_ref, c_ref, s_ref, qg_ref, kg_ref, drq_ref, drk_ref, drv_ref, dgt_ref, daq_ref, dak_ref, dav_ref,
             dp_ref, dqg_ref, dkg_ref):
        i = pl.program_id(0)
        cosf, sinf = c_ref[...], s_ref[...]

        def norm_bwd(x, g, dn):
            rstd = lax.rsqrt(jnp.mean(x * x, axis=-1, keepdims=True) + EPS)
            xhat = x * rstd
            dg = jnp.sum(dn * xhat, axis=0, keepdims=True)
            dxh = dn * g
            return rstd * (dxh - xhat * jnp.mean(dxh * xhat, axis=-1, keepdims=True)), dg

        dqg = jnp.zeros((1, CHUNK), F32)
        dkg = jnp.zeros((1, CHUNK), F32)
        for h in range(H):
            sl = pl.ds(h * CHUNK, CHUNK)
            dp_ref[:, pl.ds(h * CHUNK, CHUNK)] = _irope(drq_ref[:, sl].astype(F32), cosf, sinf).astype(BF16)
            dp_ref[:, pl.ds(HW + h * CHUNK, CHUNK)] = (_irope(drk_ref[:, sl].astype(F32), cosf, sinf)
                                                       * kscale).astype(BF16)
            dp_ref[:, pl.ds(2 * HW + h * CHUNK, CHUNK)] = drv_ref[:, sl].astype(BF16)
            dp_ref[:, pl.ds(3 * HW + h * CHUNK, CHUNK)] = dgt_ref[:, sl].astype(BF16)
            dx, dg = norm_bwd(p_ref[:, pl.ds(4 * HW + h * CHUNK, CHUNK)].astype(F32), qg_ref[...],
                              _irope(daq_ref[:, sl].astype(F32), cosf, sinf))
            dp_ref[:, pl.ds(4 * HW + h * CHUNK, CHUNK)] = dx.astype(BF16)
            dqg = dqg + dg
        for h in range(KV):
            sl = pl.ds(h * CHUNK, CHUNK)
            dx, dg = norm_bwd(p_ref[:, pl.ds(5 * HW + h * CHUNK, CHUNK)].astype(F32), kg_ref[...],
                              _irope(dak_ref[:, sl], cosf, sinf))
            dp_ref[:, pl.ds(5 * HW + h * CHUNK, CHUNK)] = dx.astype(BF16)
            dp_ref[:, pl.ds(5 * HW + (KV + h) * CHUNK, CHUNK)] = dav_ref[:, sl].astype(BF16)
            dkg = dkg + dg

        @pl.when(i == 0)
        def _():
            dqg_ref[...] = dqg
            dkg_ref[...] = dkg

        @pl.when(i > 0)
        def _():
            dqg_ref[...] += dqg
            dkg_ref[...] += dkg

    tab = pl.BlockSpec((TM, CHUNK), lambda i: (i % TPE, 0))
    vec = pl.BlockSpec((1, CHUNK), lambda i: (0, 0))
    hw = pl.BlockSpec((TM, HW), lambda i: (i, 0))
    kvw = pl.BlockSpec((TM, KV * CHUNK), lambda i: (i, 0))
    return pl.pallas_call(
        body, name=name, grid=(T // TM,),
        in_specs=[pl.BlockSpec((TM, cfg.ABW), lambda i: (i, 0)), tab, tab, vec, vec, hw, hw, hw, hw, hw, kvw, kvw],
        out_specs=[pl.BlockSpec((TM, cfg.ABW), lambda i: (i, 0)), vec, vec],
        out_shape=[jax.ShapeDtypeStruct((T, cfg.ABW), BF16), jax.ShapeDtypeStruct((1, CHUNK), F32),
                   jax.ShapeDtypeStruct((1, CHUNK), F32)],
        compiler_params=_params(("arbitrary",)),
    )(p, cosf, sinf, qg, kg, d_rq, d_rk, d_rv, d_gate, d_aq, d_ak, d_av)


def _ret_consts(direction, lg):
    C = CHUNK
    ii = lax.broadcasted_iota(jnp.int32, (C, C), 0)
    jj = lax.broadcasted_iota(jnp.int32, (C, C), 1)
    col = lax.broadcasted_iota(jnp.int32, (C, 1), 0).astype(F32)
    if direction == 0:
        mask, er, ek, eq = ii >= jj, (ii - jj).astype(F32), (C - 1.0) - col, col + 1.0
    else:
        mask, er, ek, eq = jj >= ii, (jj - ii).astype(F32), col, C - col
    er = jnp.where(mask, er, 0.0)
    dm = jnp.where(mask, jnp.exp(er * lg), 0.0)
    return dm, er, jnp.exp(ek * lg), ek, jnp.exp(eq * lg), eq, jnp.exp(C * lg)


def _ret_order(cfg, direction):
    n_all, n_ctx = cfg.S // CHUNK, cfg.SC // CHUNK
    if direction == 0:
        return list(range(n_all))
    return list(range(n_ctx - 1, -1, -1)) + list(range(n_all - 1, n_ctx - 1, -1))


def _carry_begin(carry, c_src, c_dst, sems, step):
    @pl.when(step == 0)
    def _():
        for cp in _carry_copies(carry, c_src, c_dst, *sems):
            cp.start()


def _carry_end(carry, c_src, c_dst, sems, step, n_steps):
    @pl.when(step == n_steps - 1)
    def _():
        for cp in _carry_copies(carry, c_src, c_dst, *sems):
            cp.wait()


def _carry_scratch(nc):
    return [pltpu.SemaphoreType.DMA((3 * nc,)), pltpu.SemaphoreType.DMA((3 * nc,)),
            pltpu.SemaphoreType.DMA((nc,))] if nc else []


def _head_norm_gate(o, g):
    mu = jnp.mean(o, axis=-1, keepdims=True)
    var = jnp.mean(jnp.square(o - mu), axis=-1, keepdims=True)
    rstd = lax.rsqrt(var + EPS)
    y = (o - mu) * rstd
    sg = jax.nn.sigmoid(g)
    return y, rstd, sg


RET_UNROLL = 3


def _retention_fwd(cfg, name, rq, rk, p, lgb, carry=None):
    B, H, S, T = cfg.B, cfg.H, cfg.S, cfg.T
    n_all = S // CHUNK
    nc = len(carry.srcs) if carry is not None else 0

    def body(*refs):
        q_ref, k_ref, v_ref, g_ref, lg_ref = refs[:5]
        c_src = refs[5:5 + nc]
        o_ref, ret_ref, st_ref = refs[5 + nc:8 + nc]
        c_dst = refs[8 + nc:8 + 2 * nc]
        sems = refs[8 + 2 * nc:8 + 2 * nc + 3] if nc else ()
        kv_ref = refs[-1]
        step = pl.program_id(0) * H + pl.program_id(1)
        if nc:
            _carry_begin(carry, c_src, c_dst, sems, step)

        def rows(n):
            return pl.ds(pl.multiple_of(n * CHUNK, CHUNK), CHUNK)

        (dm0, _, kd0, _, qd0, _, cd0), (dm1, _, kd1, _, qd1, _, cd1) = (
            _ret_consts(d, lg_ref[d, 0:1, 0:1]) for d in (0, 1))
        dm_both = dm0 + dm1

        def kv_step(n, c):
            k = k_ref[rows(n), :]
            v = v_ref[rows(n), :].astype(BF16)
            kv_ref[0, n] = _dot_tn((k * kd0).astype(BF16), v)
            kv_ref[1, n] = _dot_tn((k * kd1).astype(BF16), v)
            return c

        lax.fori_loop(0, n_all, kv_step, 0, unroll=RET_UNROLL)
        for direction, cd in ((0, cd0), (1, cd1)):
            st = jnp.zeros((CHUNK, CHUNK), F32)
            for t, n in enumerate(_ret_order(cfg, direction)):
                st_ref[direction, n] = st
                if t + 1 < n_all:
                    st = cd * st + kv_ref[direction, n]

        def out_step(n, c):
            q = q_ref[rows(n), :].astype(BF16)
            v = v_ref[rows(n), :].astype(BF16)
            s = _dot_nt(q, k_ref[rows(n), :].astype(BF16)) * dm_both
            states = jnp.concatenate([st_ref[0, n].astype(BF16), st_ref[1, n].astype(BF16)], axis=1)
            cross = _dot(q, states)
            o = _dot(s.astype(BF16), v) + cross[:, :CHUNK] * qd0 + cross[:, CHUNK:] * qd1
            o_ref[rows(n), :] = o
            g = g_ref[rows(n), :].astype(F32)
            y, _, sg = _head_norm_gate(o, g)
            ret_ref[rows(n), :] = (y * (g * sg)).astype(BF16)
            return c

        lax.fori_loop(0, n_all, out_step, 0, unroll=RET_UNROLL)
        if nc:
            _carry_end(carry, c_src, c_dst, sems, step, B * H)

    HW = H * CHUNK
    blk = lambda off: pl.BlockSpec((S, CHUNK), lambda b, h: (b, off + h))
    st_spec = pl.BlockSpec((None, None, 2, n_all, CHUNK, CHUNK), lambda b, h: (b, h, 0, 0, 0, 0))
    res = pl.pallas_call(
        body, name=name, grid=(B, H),
        in_specs=[blk(0), blk(0), blk(2 * H), blk(3 * H),
                  pl.BlockSpec((None, 2, 8, LANES), lambda b, h: (h, 0, 0, 0))] + [ANY] * nc,
        out_specs=[blk(0), blk(0), st_spec] + [ANY] * nc,
        out_shape=[jax.ShapeDtypeStruct((T, HW), F32), jax.ShapeDtypeStruct((T, 2 * HW), BF16),
                   jax.ShapeDtypeStruct((B, H, 2, n_all, CHUNK, CHUNK), F32)] + (_carry_out_shapes(carry) if nc else []),
        scratch_shapes=_carry_scratch(nc) + [pltpu.VMEM((2, n_all, CHUNK, CHUNK), F32)],
        compiler_params=_params(("arbitrary", "arbitrary") if nc else ("parallel", "parallel")),
    )(rq, rk, p, p, lgb, *(carry.srcs if nc else ()))
    return res[:3], res[3:]


def _retention_bwd(cfg, name, rq, rk, p, o_sum, states, dcat, lgb, carry=None):
    B, H, S, T = cfg.B, cfg.H, cfg.S, cfg.T
    n_all = S // CHUNK
    C = CHUNK
    nc = len(carry.srcs) if carry is not None else 0

    def body(*refs):
        q_ref, k_ref, v_ref, g_ref, o_ref, st_ref, dr_ref, lg_ref = refs[:8]
        c_src = refs[8:8 + nc]
        dq_ref, dk_ref, dv_ref, dg_ref, dlg_ref = refs[8 + nc:13 + nc]
        c_dst = refs[13 + nc:13 + 2 * nc]
        sems = refs[13 + 2 * nc:13 + 2 * nc + 3] if nc else ()
        do_ref, gq_ref, ds_ref, acc_ref = refs[-4:]
        step = pl.program_id(0) * H + pl.program_id(1)
        if nc:
            _carry_begin(carry, c_src, c_dst, sems, step)

        def rows(n):
            return pl.ds(pl.multiple_of(n * C, C), C)

        (dm0, er0, kd0, ek0, qd0, eq0, cd0), (dm1, er1, kd1, ek1, qd1, eq1, cd1) = (
            _ret_consts(d, lg_ref[d, 0:1, 0:1]) for d in (0, 1))
        dm_both = dm0 + dm1
        wdm0, wdm1 = dm0 * er0, dm1 * er1

        def side(a, b):
            return jnp.concatenate([a.astype(BF16), b.astype(BF16)], axis=1)

        def gq_step(n, c):
            g = g_ref[rows(n), :].astype(F32)
            dr = dr_ref[rows(n), :].astype(F32)
            y, rstd, sg = _head_norm_gate(o_ref[rows(n), :], g)
            dy = dr * (g * sg)
            dg_ref[rows(n), :] = (dr * y * (sg * (1.0 + g * (1.0 - sg)))).astype(BF16)
            do = rstd * (dy - jnp.mean(dy, axis=-1, keepdims=True) - y * jnp.mean(dy * y, axis=-1, keepdims=True))
            do_ref[rows(n), :] = do
            gq = _dot_tn(q_ref[rows(n), :].astype(BF16), side(do * qd0, do * qd1))
            gq_ref[0, n] = gq[:, :C]
            gq_ref[1, n] = gq[:, C:]
            return c

        lax.fori_loop(0, n_all, gq_step, 0, unroll=RET_UNROLL)
        for direction, cd in ((0, cd0), (1, cd1)):
            order = _ret_order(cfg, direction)
            ds = jnp.zeros((C, C), F32)
            for t in reversed(range(n_all)):
                ds_ref[direction, order[t]] = ds
                if t > 0:
                    ds = cd * ds + gq_ref[direction, order[t]]
        acc_ref[...] = jnp.zeros_like(acc_ref)

        def chunk_step(n, c):
            q = q_ref[rows(n), :].astype(BF16)
            kf = k_ref[rows(n), :]
            k = kf.astype(BF16)
            v = v_ref[rows(n), :].astype(BF16)
            do = do_ref[rows(n), :]
            dob = do.astype(BF16)
            sp0, sp1 = st_ref[0, n], st_ref[1, n]
            ds0, ds1 = ds_ref[0, n], ds_ref[1, n]
            states = side(sp0, sp1)
            dstates = jnp.concatenate([ds0.astype(BF16), ds1.astype(BF16)], axis=0)
            doq0, doq1 = do * qd0, do * qd1
            doq = side(doq0, doq1)
            s_raw = _dot_nt(q, k)
            dpm = _dot_nt(dob, v)
            dsr = (dpm * dm_both).astype(BF16)
            dks = _dot_nt(v, dstates)
            dks0, dks1 = dks[:, :C] * kd0, dks[:, C:] * kd1
            qs = _dot(q, states)
            dq_ref[rows(n), :] = (_dot(dsr, k) + _dot_nt(doq, states)).astype(BF16)
            dk_ref[rows(n), :] = (_dot_tn(dsr, q) + dks0 + dks1).astype(BF16)
            dv_ref[rows(n), :] = (_dot_tn((s_raw * dm_both).astype(BF16), dob)
                                  + _dot(side(kf * kd0, kf * kd1), dstates)).astype(BF16)
            inner = dpm * s_raw
            acc_ref[0] += (jnp.sum(inner * wdm0, axis=0, keepdims=True)
                           + jnp.sum(eq0 * doq0 * qs[:, :C], axis=0, keepdims=True)
                           + jnp.sum(ek0 * kf * dks0, axis=0, keepdims=True)
                           + (C * cd0) * jnp.sum(ds0 * sp0, axis=0, keepdims=True))
            acc_ref[1] += (jnp.sum(inner * wdm1, axis=0, keepdims=True)
                           + jnp.sum(eq1 * doq1 * qs[:, C:], axis=0, keepdims=True)
                           + jnp.sum(ek1 * kf * dks1, axis=0, keepdims=True)
                           + (C * cd1) * jnp.sum(ds1 * sp1, axis=0, keepdims=True))
            return c

        lax.fori_loop(0, n_all, chunk_step, 0, unroll=RET_UNROLL)
        for direction in (0, 1):
            dlg_ref[direction] = jnp.broadcast_to(jnp.sum(acc_ref[direction], axis=1, keepdims=True), (8, LANES))
        if nc:
            _carry_end(carry, c_src, c_dst, sems, step, B * H)

    HW = H * CHUNK
    blk = lambda off: pl.BlockSpec((S, CHUNK), lambda b, h: (b, off + h))
    st_spec = pl.BlockSpec((None, None, 2, n_all, C, C), lambda b, h: (b, h, 0, 0, 0, 0))
    res = pl.pallas_call(
        body, name=name, grid=(B, H),
        in_specs=[blk(0), blk(0), blk(2 * H), blk(3 * H), blk(0), st_spec, blk(0),
                  pl.BlockSpec((None, 2, 8, LANES), lambda b, h: (h, 0, 0, 0))] + [ANY] * nc,
        out_specs=[blk(0)] * 4 + [pl.BlockSpec((None, None, 2, 8, LANES), lambda b, h: (b, h, 0, 0, 0))] + [ANY] * nc,
        out_shape=[jax.ShapeDtypeStruct((T, HW), BF16)] * 4 + [jax.ShapeDtypeStruct((B, H, 2, 8, LANES), F32)]
        + (_carry_out_shapes(carry) if nc else []),
        scratch_shapes=_carry_scratch(nc) + [pltpu.VMEM((S, CHUNK), F32), pltpu.VMEM((2, n_all, C, C), F32),
                                             pltpu.VMEM((2, n_all, C, C), F32), pltpu.VMEM((2, 1, C), F32)],
        compiler_params=_params(("arbitrary", "arbitrary") if nc else ("parallel", "parallel")),
    )(rq, rk, p, p, o_sum, states, dcat, lgb, *(carry.srcs if nc else ()))
    return res[:5], res[5:]


ATT_SCALE = CHUNK ** -0.5


def _attn_scores(cfg, q, k, t):
    kcol = lax.broadcasted_iota(jnp.int32, (1, cfg.S), 1)
    bias = jnp.where(jnp.logical_or(t > 0, kcol < cfg.SC), 0.0, -1e30)
    return _dot_nt(q, k) + bias


def _attention_fwd(cfg, name, aq, ak, p, cat, carry=None):
    B, H, KV, S, T, TM, TPE = cfg.B, cfg.H, cfg.KV, cfg.S, cfg.T, cfg.TM, cfg.TPE
    G = H // KV
    v_off = (5 * H + KV)
    nc = len(carry.srcs) if carry is not None else 0

    def body(*refs):
        q_ref, k_ref, v_ref = refs[:3]
        o_ref, lse_ref = refs[4 + nc:6 + nc]
        c_src, c_dst, sems = refs[4:4 + nc], refs[6 + nc:6 + 2 * nc], refs[6 + 2 * nc:]
        step = (pl.program_id(0) * KV + pl.program_id(1)) * TPE + pl.program_id(2)
        if nc:
            _carry_begin(carry, c_src, c_dst, sems, step)
        t = pl.program_id(2)
        kb = S // 3 if (S // 3) % CHUNK == 0 else S
        for g in range(G):
            cols = pl.ds(g * CHUNK, CHUNK)
            q = q_ref[:, cols]
            m = jnp.full((TM, 1), -1e30, F32)
            total = jnp.zeros((TM, 1), F32)
            acc = jnp.zeros((TM, CHUNK), F32)
            for b0 in range(0, S, kb):
                ks = pl.ds(b0, kb)
                kcol = lax.broadcasted_iota(jnp.int32, (1, kb), 1) + b0
                bias = jnp.where(jnp.logical_or(t > 0, kcol < cfg.SC), 0.0, -1e30)
                s = _dot_nt(q, k_ref[ks, :]) + bias
                m_new = jnp.maximum(m, jnp.max(s, axis=-1, keepdims=True))
                alpha = jnp.exp(m - m_new)
                e = jnp.exp(s - m_new)
                total = alpha * total + jnp.sum(e, axis=-1, keepdims=True)
                acc = alpha * acc + _dot(e.astype(BF16), v_ref[ks, :].astype(BF16))
                m = m_new
            o_ref[:, cols] = (acc * (1.0 / total)).astype(BF16)
            lse_ref[g] = m + jnp.log(total)
        if nc:
            _carry_end(carry, c_src, c_dst, sems, step, B * KV * TPE)

    res = pl.pallas_call(
        body, name=name, grid=(B, KV, TPE),
        in_specs=[pl.BlockSpec((TM, G * CHUNK), lambda b, kv, t: (b * TPE + t, kv)),
                  pl.BlockSpec((S, CHUNK), lambda b, kv, t: (b, kv)),
                  pl.BlockSpec((S, CHUNK), lambda b, kv, t: (b, v_off + kv)), ANY] + [ANY] * nc,
        out_specs=[pl.BlockSpec((TM, G * CHUNK), lambda b, kv, t: (b * TPE + t, KV + kv)),
                   pl.BlockSpec((G, TM, 1), lambda b, kv, t: (kv, b * TPE + t, 0))] + [ANY] * nc,
        out_shape=[jax.ShapeDtypeStruct(cat.shape, cat.dtype), jax.ShapeDtypeStruct((H, T, 1), F32)]
        + (_carry_out_shapes(carry) if nc else []),
        input_output_aliases={3: 0},
        scratch_shapes=_carry_scratch(nc),
        compiler_params=_params(("arbitrary",) * 3 if nc else ("parallel",) * 3),
    )(aq, ak, p, cat, *(carry.srcs if nc else ()))
    return res[:2], res[2:]


def _attention_bwd(cfg, name, aq, ak, p, cat, lse, dcat, carry=None):
    B, H, KV, S, T, TM, TPE = cfg.B, cfg.H, cfg.KV, cfg.S, cfg.T, cfg.TM, cfg.TPE
    G = H // KV
    v_off = (5 * H + KV)
    nc = len(carry.srcs) if carry is not None else 0

    def body(*refs):
        q_ref, k_ref, v_ref, o_ref, lse_ref, do_ref = refs[:6]
        dq_ref, dk_ref, dv_ref = refs[6 + nc:9 + nc]
        c_src, c_dst, sems = refs[6:6 + nc], refs[9 + nc:9 + 2 * nc], refs[9 + 2 * nc:]
        t = pl.program_id(2)
        step = (pl.program_id(0) * KV + pl.program_id(1)) * TPE + t
        if nc:
            _carry_begin(carry, c_src, c_dst, sems, step)
        k = k_ref[...]
        v = v_ref[...].astype(BF16)
        dk = dv = None
        for g in range(G):
            cols = pl.ds(g * CHUNK, CHUNK)
            q = q_ref[:, cols]
            do = do_ref[:, cols]
            pr = jnp.exp(_attn_scores(cfg, q, k, t) - lse_ref[g])
            delta = jnp.sum(do.astype(F32) * o_ref[:, cols].astype(F32), axis=-1, keepdims=True)
            ds = (pr * (_dot_nt(do, v) - delta)).astype(BF16)
            dq_ref[:, cols] = (_dot(ds, k) * ATT_SCALE).astype(BF16)
            dk_g, dv_g = _dot_tn(ds, q), _dot_tn(pr.astype(BF16), do)
            dk, dv = (dk_g, dv_g) if dk is None else (dk + dk_g, dv + dv_g)

        @pl.when(t == 0)
        def _():
            dk_ref[...] = dk
            dv_ref[...] = dv

        @pl.when(t > 0)
        def _():
            dk_ref[...] += dk
            dv_ref[...] += dv

        if nc:
            _carry_end(carry, c_src, c_dst, sems, step, B * KV * TPE)

    qspec = pl.BlockSpec((TM, G * CHUNK), lambda b, kv, t: (b * TPE + t, kv))
    kvspec = pl.BlockSpec((S, CHUNK), lambda b, kv, t: (b, kv))
    right = pl.BlockSpec((TM, G * CHUNK), lambda b, kv, t: (b * TPE + t, KV + kv))
    res = pl.pallas_call(
        body, name=name, grid=(B, KV, TPE),
        in_specs=[qspec, kvspec, pl.BlockSpec((S, CHUNK), lambda b, kv, t: (b, v_off + kv)), right,
                  pl.BlockSpec((G, TM, 1), lambda b, kv, t: (kv, b * TPE + t, 0)), right] + [ANY] * nc,
        out_specs=[qspec, kvspec, kvspec] + [ANY] * nc,
        out_shape=[jax.ShapeDtypeStruct((T, H * CHUNK), BF16), jax.ShapeDtypeStruct((T, KV * CHUNK), F32),
                   jax.ShapeDtypeStruct((T, KV * CHUNK), F32)] + (_carry_out_shapes(carry) if nc else []),
        scratch_shapes=_carry_scratch(nc),
        compiler_params=_params(("arbitrary",) * 3 if nc else ("parallel", "parallel", "arbitrary")),
    )(aq, ak, p, cat, lse, dcat, *(carry.srcs if nc else ()))
    return res[:3], res[3:]


_GELU_C = math.sqrt(2.0 / math.pi)


def _gelu(x):
    return 0.5 * x * (1.0 + jnp.tanh(_GELU_C * (x + 0.044715 * x * x * x)))


def _gelu_and_grad(x):
    x2 = x * x
    th = jnp.tanh(_GELU_C * (x + 0.044715 * x * x2))
    half = 0.5 * (1.0 + th)
    return x * half, half + 0.5 * x * (1.0 - th * th) * _GELU_C * (1.0 + 3.0 * 0.044715 * x2)


def _cm_fwd(cfg, name, a, vg, ws, bs):
    T, TM, W, NG = cfg.T, cfg.TM, cfg.CMW, cfg.CMG

    def body(a_ref, vg_ref, ws_ref, bs_ref, m_ref):
        v = _gelu(a_ref[:, pl.ds(W, W)].astype(F32))
        vn = (v * lax.rsqrt(jnp.mean(v * v, axis=-1, keepdims=True) + EPS) * vg_ref[...]).astype(BF16)
        for c in range(TM // CHUNK):
            for g in range(NG):
                rows, cols = slice(c * CHUNK, (c + 1) * CHUNK), slice(g * CHUNK, (g + 1) * CHUNK)
                sv = _dot(ws_ref[g].astype(BF16), vn[rows, cols]) + bs_ref[g]
                u = _gelu(a_ref[pl.ds(c * CHUNK, CHUNK), pl.ds(g * CHUNK, CHUNK)].astype(F32))
                m_ref[pl.ds(c * CHUNK, CHUNK), pl.ds(g * CHUNK, CHUNK)] = (u * sv).astype(BF16)

    return pl.pallas_call(
        body, name=name, grid=(T // TM,),
        in_specs=[pl.BlockSpec((TM, 2 * W), lambda i: (i, 0)), pl.BlockSpec((1, W), lambda i: (0, 0)),
                  pl.BlockSpec((NG, CHUNK, CHUNK), lambda i: (0, 0, 0)),
                  pl.BlockSpec((NG, CHUNK, 1), lambda i: (0, 0, 0))],
        out_specs=pl.BlockSpec((TM, W), lambda i: (i, 0)),
        out_shape=jax.ShapeDtypeStruct((T, W), BF16),
        compiler_params=_params(("parallel",)),
    )(a, vg, ws, bs)


def _cm_bwd(cfg, name, a, vg, ws, bs, dm):
    T, TM, W, NG = cfg.T, cfg.TM, cfg.CMW, cfg.CMG

    def body(a_ref, vg_ref, ws_ref, bs_ref, dm_ref, da_ref, dws_ref, dbs_ref, dvg_ref, dvn_ref):
        i = pl.program_id(0)

        @pl.when(i == 0)
        def _():
            dws_ref[...] = jnp.zeros_like(dws_ref)
            dbs_ref[...] = jnp.zeros_like(dbs_ref)
            dvg_ref[...] = jnp.zeros_like(dvg_ref)

        v, v_grad = _gelu_and_grad(a_ref[:, pl.ds(W, W)].astype(F32))
        rstd = lax.rsqrt(jnp.mean(v * v, axis=-1, keepdims=True) + EPS)
        xhat = v * rstd
        vg = vg_ref[...]
        vn = (xhat * vg).astype(BF16)
        for c in range(TM // CHUNK):
            for g in range(NG):
                rows, cols = slice(c * CHUNK, (c + 1) * CHUNK), slice(g * CHUNK, (g + 1) * CHUNK)
                rs, cs = pl.ds(c * CHUNK, CHUNK), pl.ds(g * CHUNK, CHUNK)
                wsb = ws_ref[g].astype(BF16)
                blk = vn[rows, cols]
                sv = _dot(wsb, blk) + bs_ref[g]
                u, u_grad = _gelu_and_grad(a_ref[rs, cs].astype(F32))
                dmb = dm_ref[rs, cs].astype(F32)
                da_ref[rs, cs] = (dmb * sv * u_grad).astype(BF16)
                dsv = dmb * u
                dsvb = dsv.astype(BF16)
                dbs_ref[g] += jnp.sum(dsv, axis=1, keepdims=True)
                dws_ref[g] += _dot_nt(dsvb, blk)
                dvn_ref[rs, cs] = _dot_tn(wsb, dsvb)
        dvn = dvn_ref[...]
        dvg_ref[...] += jnp.sum(dvn * xhat, axis=0, keepdims=True)
        dxh = dvn * vg
        dv = rstd * (dxh - xhat * jnp.mean(dxh * xhat, axis=-1, keepdims=True))
        da_ref[:, pl.ds(W, W)] = (dv * v_grad).astype(BF16)

    return pl.pallas_call(
        body, name=name, grid=(T // TM,),
        in_specs=[pl.BlockSpec((TM, 2 * W), lambda i: (i, 0)), pl.BlockSpec((1, W), lambda i: (0, 0)),
                  pl.BlockSpec((NG, CHUNK, CHUNK), lambda i: (0, 0, 0)),
                  pl.BlockSpec((NG, CHUNK, 1), lambda i: (0, 0, 0)), pl.BlockSpec((TM, W), lambda i: (i, 0))],
        out_specs=[pl.BlockSpec((TM, 2 * W), lambda i: (i, 0)), pl.BlockSpec((NG, CHUNK, CHUNK), lambda i: (0, 0, 0)),
                   pl.BlockSpec((NG, CHUNK, 1), lambda i: (0, 0, 0)), pl.BlockSpec((1, W), lambda i: (0, 0))],
        out_shape=[jax.ShapeDtypeStruct((T, 2 * W), BF16), jax.ShapeDtypeStruct((NG, CHUNK, CHUNK), F32),
                   jax.ShapeDtypeStruct((NG, CHUNK, 1), F32), jax.ShapeDtypeStruct((1, W), F32)],
        scratch_shapes=[pltpu.VMEM((TM, W), F32)],
        compiler_params=_params(("arbitrary",)),
    )(a, vg, ws, bs, dm)


def _layer_weights(l):
    mixer = ("ab_w_in", "ab_w_out") if l % 2 == 0 else ("cm_w_in", "cm_w_out")
    return [(mixer[0], l // 2), (mixer[1], l // 2), ("ff_w1", l), ("ff_w2", l)]


def _local_step(cfg, xcat, tgt, mods, shards, w, first_weight):
    D, TM, H = cfg.D, cfg.TM, cfg.H
    cosf, sinf = _rope_tables(cfg)
    full, big, recv = {}, {}, {}

    def gather_of(keys):
        return _Carry("gather", tuple(shards[n][i] for n, i in keys), tuple(BIG[n] for n, _ in keys))

    def exchange_of(keys):
        return _Carry("exchange", tuple(big[k] for k in keys), tuple(BIG[n] for n, _ in keys))

    def mm(pending, name, a, b, **kw):
        if not pending:
            return _mm(name, a, b, **kw)
        key, carry, sink = pending.pop(0)
        out, (got,) = _mm(name, a, b, carry=carry, **kw)
        sink[key] = got
        return out

    def with_carry(call, keys, make, sink):
        out, got = call(carry=make(keys) if keys else None)
        sink.update(zip(keys, got))
        return out

    keys0 = _layer_weights(0)
    full[keys0[0]] = first_weight
    TG = 3 * TM if cfg.TPE % 3 == 0 else TM
    tiles_per_ex = cfg.S // TG
    gate_spec = pl.BlockSpec((None, 2, 6, D), lambda i, j, k: (i // tiles_per_ex, 0, 0, 0))

    def resid_epi(igate, nxt):
        def epi(acc, row_tile, x_ref, mod_ref, *nxt_refs):
            row = lax.broadcasted_iota(jnp.int32, (TG, 1), 0)
            is_ctx = jnp.logical_and(row_tile % tiles_per_ex == 0, row < cfg.SC)

            def pick(ref, idx):
                return jnp.where(is_ctx, ref[0, pl.ds(idx, 1), :], ref[1, pl.ds(idx, 1), :])

            x = x_ref[...] + pick(mod_ref, igate) * acc
            if nxt is None:
                return x, acc
            gain_ref, modn_ref = nxt_refs
            n = x * lax.rsqrt(jnp.mean(x * x, axis=-1, keepdims=True) + EPS) * gain_ref[...]
            return x, acc, n * (1.0 + pick(modn_ref, nxt[3])) + pick(modn_ref, nxt[2])
        return epi

    def gated_out(pending, name, a, key, x, mod, igate, nxt=None):
        extras = [(x, pl.BlockSpec((TG, D), lambda i, j, k: (i, j))), (mod, gate_spec)]
        if nxt is not None:
            extras += [(nxt[0], pl.BlockSpec((1, D), lambda i, j, k: (0, 0))), (nxt[1], gate_spec)]
        return mm(pending, name, a, full[key], mode="nn", tm=TG, tn=D, outs=[F32, BF16] + [BF16] * (nxt is not None),
                  epi=resid_epi(igate, nxt), extras=extras, resident=True)

    saved = []
    x = xcat
    h = _norm_mod_fwd(cfg, "norm1_fwd_0", x, w["norm1_g"][0][None], mods[0], 0, 1)
    for l in range(N_LAYERS):
        li = l // 2
        mod = mods[l]
        k_in, k_out, k_ff1, k_ff2 = _layer_weights(l)
        pend = [(k, gather_of([k]), full) for k in _layer_weights(l + 1)] if l + 1 < N_LAYERS else []
        norm2 = (w["norm2_g"][l][None], mod, 3, 4)
        s = {"x0": x, "h": h}
        if l % 2 == 0:
            lgb = jnp.broadcast_to(jax.nn.log_sigmoid(w["ret_decay"][li]).T[:, :, None, None], (H, 2, 8, LANES))
            qg, kg = w["att_q_norm_g"][li][None], w["att_k_norm_g"][li][None]
            s["p"] = mm(pend, f"ab_in_{l}", s["h"], full[k_in], mode="nn", outs=[BF16], tn=768)
            s["rq"], s["rk"], s["aq"], s["ak"] = _prep_fwd(cfg, f"prep_fwd_{l}", s["p"], cosf, sinf, qg, kg)
            s["o"], ret, s["st"] = with_carry(
                functools.partial(_retention_fwd, cfg, f"ret_fwd_{l}", s["rq"], s["rk"], s["p"], lgb),
                keys0[1:3] if l == 0 else [], gather_of, full)
            s["cat"], s["lse"] = with_carry(
                functools.partial(_attention_fwd, cfg, f"att_fwd_{l}", s["aq"], s["ak"], s["p"], ret),
                keys0[3:] if l == 0 else [], gather_of, full)
            s["lgb"], s["qg"], s["kg"] = lgb, qg, kg
            x, s["y1"], s["h2"] = gated_out(pend, f"ab_out_{l}", s["cat"], k_out, x, mod, 2, norm2)
        else:
            s["a"] = mm(pend, f"cm_in_{l}", s["h"], full[k_in], mode="nn", outs=[BF16])
            s["vg"], s["ws"], s["bs"] = w["cm_v_norm_g"][li][None], w["cm_w_s"][li], w["cm_b_s"][li][:, :, None]
            s["m"] = _cm_fwd(cfg, f"cm_fwd_{l}", s["a"], s["vg"], s["ws"], s["bs"])
            x, s["y1"], s["h2"] = gated_out(pend, f"cm_out_{l}", s["m"], k_out, x, mod, 2, norm2)
        s["x1"] = x
        s["r"] = mm(pend, f"ff1_{l}", s["h2"], full[k_ff1], mode="nn", outs=[BF16],
                    epi=lambda acc, row_tile: (jnp.square(jnp.maximum(acc, 0.0)),))
        if l + 1 < N_LAYERS:
            x, s["y2"], h = gated_out(pend, f"ff2_{l}", s["r"], k_ff2, x, mod, 5,
                                      (w["norm1_g"][l + 1][None], mods[l + 1], 0, 1))
        else:
            x, s["y2"] = gated_out(pend, f"ff2_{l}", s["r"], k_ff2, x, mod, 5)
        saved.append(s)

    loss, dx, dy2, dg2 = _loss_grad(cfg, x, tgt, saved[-1]["y2"], mods[-1], 5)

    small = {k: [None] * n for k, n in (("norm1_g", 4), ("norm2_g", 4), ("ret_lg", 2), ("att_q_norm_g", 2),
                                        ("att_k_norm_g", 2), ("cm_v_norm_g", 2), ("cm_w_s", 2), ("cm_b_s", 2))}
    dmods = [None] * N_LAYERS

    for l in reversed(range(N_LAYERS)):
        li = l // 2
        s, mod = saved[l], mods[l]
        k_in, k_out, k_ff1, k_ff2 = _layer_weights(l)
        above = _layer_weights(l + 1) if l + 1 < N_LAYERS else [None] * 4
        da2, big[k_ff2] = with_carry(functools.partial(_ff_bwd, f"ff2_bwd_{l}", dy2, s["r"], full[k_ff2], "w2"),
                                     [above[3], above[1]] if above[0] else [], exchange_of, recv)
        dh2, big[k_ff1] = with_carry(functools.partial(_ff_bwd, f"ff1_bwd_{l}", s["h2"], da2, full[k_ff1], "w1"),
                                     [above[2], above[0]] if above[0] else [], exchange_of, recv)
        dx, dm2, small["norm2_g"][l], do, dg1 = _norm_mod_bwd(
            cfg, f"norm2_bwd_{l}", s["x1"], w["norm2_g"][l][None], mod, 3, 4, dh2, dx, gate=(s["y1"], mod, 2))
        if l % 2 == 0:
            big[k_out] = _mm(f"ab_out_dw_{l}", s["cat"], do, mode="tn", outs=[BF16])
            dcat = _mm(f"ab_out_dx_{l}", do, full[k_out], mode="nt", outs=[BF16])
            d_rq, d_rk, d_rv, d_gt, dlg = with_carry(
                functools.partial(_retention_bwd, cfg, f"ret_bwd_{l}", s["rq"], s["rk"], s["p"], s["o"], s["st"], dcat,
                                  s["lgb"]), [k_ff2, k_ff1] if l == 0 else [], exchange_of, recv)
            d_aq, d_ak, d_av = with_carry(
                functools.partial(_attention_bwd, cfg, f"att_bwd_{l}", s["aq"], s["ak"], s["p"], s["cat"], s["lse"], dcat),
                [k_out] if l == 0 else [], exchange_of, recv)
            dp, dqg, dkg = _prep_bwd(cfg, f"prep_bwd_{l}", s["p"], cosf, sinf, s["qg"], s["kg"],
                                     d_rq, d_rk, d_rv, d_gt, d_aq, d_ak, d_av)
            small["ret_lg"][li] = jnp.sum(dlg[:, :, :, 0, 0], axis=0).T
            small["att_q_norm_g"][li], small["att_k_norm_g"][li] = dqg[0], dkg[0]
            big[k_in] = _mm(f"ab_in_dw_{l}", s["h"], dp, mode="tn", outs=[BF16])
            last = [(k_in, exchange_of([k_in]), recv)] if l == 0 else []
            dh = mm(last, f"ab_in_dx_{l}", dp, full[k_in], mode="nt", outs=[BF16], resident=True)
        else:
            big[k_out] = _mm(f"cm_out_dw_{l}", s["m"], do, mode="tn", outs=[BF16])
            dm = _mm(f"cm_out_dx_{l}", do, full[k_out], mode="nt", outs=[BF16])
            da, dws, dbs, dvg = _cm_bwd(cfg, f"cm_bwd_{l}", s["a"], s["vg"], s["ws"], s["bs"], dm)
            small["cm_w_s"][li], small["cm_b_s"][li], small["cm_v_norm_g"][li] = dws, dbs[:, :, 0], dvg[0]
            (dh, big[k_in]), _ = _ff_bwd(f"cm_in_bwd_{l}", s["h"], da, full[k_in], "w1")
        below = (saved[l - 1]["y2"], mods[l - 1], 5) if l > 0 else None
        dx, dm1, small["norm1_g"][l], *rest = _norm_mod_bwd(
            cfg, f"norm1_bwd_{l}", s["x0"], w["norm1_g"][l][None], mod, 0, 1, dh, dx, gate=below, lat_only=l == 0)
        dmods[l] = jnp.concatenate([dm1, dg1, dm2, dg2], axis=2)
        if l > 0:
            dy2, dg2 = rest
    return loss, dx, recv, small, dmods


N_DEV = 8
N_CHIP = 4
MESH = pl.DeviceIdType.MESH
ANY = pl.BlockSpec(memory_space=pl.ANY)
BIG = {"ab_w_in": 1, "ab_w_out": 0, "cm_w_in": 1, "cm_w_out": 0, "ff_w1": 1, "ff_w2": 0}


class _Carry(NamedTuple):
    kind: str
    srcs: tuple
    axes: tuple


def _place():
    x, y, c = lax.axis_index("x"), lax.axis_index("y"), lax.axis_index("c")
    return x, y, c, [(1 - x, y), (x, 1 - y), (1 - x, 1 - y)]


def _shard_of(ref, axis, s, width):
    start = pl.multiple_of(s * width, LANES)
    if axis == 0:
        return ref.at[pl.ds(start, width), :]
    return ref.at[:, pl.ds(start, width)]


def _carry_out_shapes(carry):
    shapes = []
    for src, axis in zip(carry.srcs, carry.axes):
        shape = list(src.shape)
        if carry.kind == "swap":
            pass
        elif carry.kind == "gather":
            shape[axis] *= N_CHIP
        else:
            shape[axis] //= N_CHIP
            shape = [N_CHIP] + shape
        shapes.append(jax.ShapeDtypeStruct(tuple(shape), src.dtype))
    return shapes


def _carry_copies(carry, srcs, dsts, send_sems, recv_sems, local_sems):
    x, y, c, chips = _place()
    me = 2 * x + y
    copies = []
    if carry.kind == "swap":
        return [pltpu.make_async_remote_copy(
            src_ref=srcs[t], dst_ref=dsts[t], send_sem=send_sems.at[3 * t], recv_sem=recv_sems.at[3 * t],
            device_id=(x, y, 1 - c), device_id_type=MESH) for t in range(len(srcs))]
    for t, axis in enumerate(carry.axes):
        if carry.kind == "gather":
            own = _shard_of(dsts[t], axis, me, srcs[t].shape[axis])
            copies.append(pltpu.make_async_copy(srcs[t], own, local_sems.at[t]))
            parts = [(srcs[t], own)] * 3
        else:
            width = dsts[t].shape[1 + axis]
            copies.append(pltpu.make_async_copy(_shard_of(srcs[t], axis, me, width), dsts[t].at[3], local_sems.at[t]))
            parts = [(_shard_of(srcs[t], axis, 2 * px + py, width), dsts[t].at[j]) for j, (px, py) in enumerate(chips)]
        for j, (px, py) in enumerate(chips):
            copies.append(pltpu.make_async_remote_copy(
                src_ref=parts[j][0], dst_ref=parts[j][1], send_sem=send_sems.at[3 * t + j],
                recv_sem=recv_sems.at[3 * t + j], device_id=(px, py, c), device_id_type=MESH))
    return copies


def _allgather8(name, block, carry=None):
    m_per, n = block.shape
    nc = len(carry.srcs) if carry is not None else 0

    def body(*refs):
        x_ref, out_ref = refs[0], refs[1 + nc]
        send_sems, recv_sems, local_sem = refs[2 + 2 * nc:5 + 2 * nc]
        carried = _carry_copies(carry, refs[1:1 + nc], refs[2 + nc:2 + 2 * nc], *refs[5 + 2 * nc:]) if nc else []
        for cp in carried:
            cp.start()
        x, y, c, chips = _place()
        me, sibling = (x, y, c), (x, y, 1 - c)

        def rows(px, py, pc):
            return out_ref.at[pl.ds((4 * px + 2 * py + pc) * m_per, m_per), :]

        def copy(k, blk, to, src=None):
            return pltpu.make_async_remote_copy(
                src_ref=rows(*blk) if src is None else src, dst_ref=rows(*blk),
                send_sem=send_sems.at[k], recv_sem=recv_sems.at[k], device_id=to, device_id_type=MESH)

        mine = pltpu.make_async_copy(x_ref, rows(*me), local_sem)
        mine.start()
        first = [copy(0, me, sibling, src=x_ref)]
        first += [copy(1 + j, me, (*chip, c), src=x_ref) for j, chip in enumerate(chips)]
        for cp in first:
            cp.start()
        passed = [copy(4 + j, (*chip, c), sibling) for j, chip in enumerate(chips)]
        for j, chip in enumerate(chips):
            copy(1 + j, (*chip, c), me).wait_recv()
            passed[j].start()
        copy(0, sibling, me).wait_recv()
        for j, chip in enumerate(chips):
            copy(4 + j, (*chip, 1 - c), me).wait_recv()
        for cp in first + passed:
            cp.wait_send()
        mine.wait()
        for cp in carried:
            cp.wait()

    res = pl.pallas_call(
        body, name=name,
        out_shape=[jax.ShapeDtypeStruct((N_DEV * m_per, n), block.dtype)] + (_carry_out_shapes(carry) if nc else []),
        in_specs=[pl.BlockSpec(memory_space=pltpu.VMEM)] + [ANY] * nc,
        out_specs=[pl.BlockSpec(memory_space=pltpu.VMEM)] + [ANY] * nc,
        scratch_shapes=[pltpu.SemaphoreType.DMA((7,)), pltpu.SemaphoreType.DMA((7,)), pltpu.SemaphoreType.DMA]
        + _carry_scratch(nc),
        compiler_params=pltpu.CompilerParams(vmem_limit_bytes=VMEM_LIMIT),
    )(block, *(carry.srcs if nc else ()))
    return (res[0], res[1:]) if nc else res[0]


def _rows_view(a):
    if a.ndim == 1:
        return a.reshape(1, a.shape[0])
    return a.reshape(-1, a.shape[-1])


def _row_tile(rows, cols, target_elems=1 << 17):
    tr = rows
    while tr % 16 == 0 and tr * cols > target_elems:
        tr //= 2
    return tr


def _sum_leading(name, a):
    n, rows, cols = a.shape
    tr = _row_tile(rows, cols * n, target_elems=1 << 20)

    def body(a_ref, o_ref):
        acc = a_ref[0].astype(F32)
        for i in range(1, n):
            acc = acc + a_ref[i].astype(F32)
        o_ref[...] = acc

    return pl.pallas_call(
        body, name=name, grid=(rows // tr,),
        in_specs=[pl.BlockSpec((n, tr, cols), lambda i: (0, i, 0))],
        out_specs=pl.BlockSpec((tr, cols), lambda i: (i, 0)),
        out_shape=jax.ShapeDtypeStruct((rows, cols), F32),
        compiler_params=_params(("parallel",)),
    )(a)


def _sum_parts_layers(name, parts):
    n_layers = len(parts)
    n, rows, cols = parts[0].shape
    tr = _row_tile(rows, cols * n, target_elems=1 << 20)

    def body(*refs):
        o_ref = refs[n_layers]
        layer = pl.program_id(0)
        for k in range(n_layers):
            @pl.when(layer == k)
            def _():
                acc = refs[k][0].astype(F32)
                for i in range(1, n):
                    acc = acc + refs[k][i].astype(F32)
                o_ref[...] = acc

    def in_spec(k):
        return pl.BlockSpec((n, tr, cols), lambda l, i: (0, jnp.where(l == k, i, 0), 0))

    return pl.pallas_call(
        body, name=name, grid=(n_layers, rows // tr),
        in_specs=[in_spec(k) for k in range(n_layers)],
        out_specs=pl.BlockSpec((None, tr, cols), lambda l, i: (l, i, 0)),
        out_shape=jax.ShapeDtypeStruct((n_layers, rows, cols), F32),
        compiler_params=_params(("arbitrary", "arbitrary")),
    )(*parts)


def _silu_rows(name, x):
    def body(x_ref, o_ref):
        v = x_ref[...]
        o_ref[...] = v * jax.nn.sigmoid(v)

    return pl.pallas_call(body, name=name, out_shape=jax.ShapeDtypeStruct(x.shape, F32))(x)


def _silu_bwd_rows(name, x, dy):
    def body(x_ref, dy_ref, o_ref):
        v = x_ref[...]
        sg = jax.nn.sigmoid(v)
        o_ref[...] = dy_ref[...] * (sg * (1.0 + v * (1.0 - sg)))

    return pl.pallas_call(body, name=name, out_shape=jax.ShapeDtypeStruct(x.shape, F32))(x, dy)


def _adamw(name, w, g_parts, m, v):
    shape = w.shape
    w2, m2, v2 = _rows_view(w), _rows_view(m), _rows_view(v)
    gs = [_rows_view(g) for g in g_parts]
    rows, cols = w2.shape
    tr = _row_tile(rows, cols)
    ng = len(gs)

    def body(*refs):
        w_ref, m_ref, v_ref = refs[0], refs[1], refs[2]
        g_refs = refs[3:3 + ng]
        g_out, d_out, m_out, v_out = refs[3 + ng:]
        g = g_refs[0][...]
        for r in g_refs[1:]:
            g = g + r[...]
        m1 = ADAM_B1 * m_ref[...] + (1.0 - ADAM_B1) * g
        v1 = ADAM_B2 * v_ref[...] + (1.0 - ADAM_B2) * jnp.square(g)
        m_hat = m1 / (1.0 - ADAM_B1 ** ADAM_STEP)
        v_hat = v1 / (1.0 - ADAM_B2 ** ADAM_STEP)
        g_out[...] = g
        d_out[...] = -ADAM_LR * (m_hat / (jnp.sqrt(v_hat) + ADAM_EPS) + ADAM_WD * w_ref[...])
        m_out[...] = m1
        v_out[...] = v1

    spec = pl.BlockSpec((tr, cols), lambda i: (i, 0))
    res = pl.pallas_call(
        body, name=name, grid=(rows // tr,), in_specs=[spec] * (3 + ng), out_specs=[spec] * 4,
        out_shape=[jax.ShapeDtypeStruct((rows, cols), F32)] * 4,
        compiler_params=_params(("parallel",)),
    )(w2, m2, v2, *gs)
    return tuple(r.reshape(shape) for r in res)


MOD_ROWS = 48


def kernel(x, c, ctx, c_ctx, mod_w, mod_b, norm1_g, norm2_g, ab_w_in, ab_w_out, ret_decay, att_q_norm_g, att_k_norm_g, cm_w_in, cm_v_norm_g, cm_w_s, cm_b_s, cm_w_out, ff_w1, ff_w2, loss_target, m_c_ctx, m_mod_w, m_mod_b, m_norm1_g, m_norm2_g, m_ab_w_in, m_ab_w_out, m_ret_decay, m_att_q_norm_g, m_att_k_norm_g, m_cm_w_in, m_cm_v_norm_g, m_cm_w_s, m_cm_b_s, m_cm_w_out, m_ff_w1, m_ff_w2, v_c_ctx, v_mod_w, v_mod_b, v_norm1_g, v_norm2_g, v_ab_w_in, v_ab_w_out, v_ret_decay, v_att_q_norm_g, v_att_k_norm_g, v_cm_w_in, v_cm_v_norm_g, v_cm_w_s, v_cm_b_s, v_cm_w_out, v_ff_w1, v_ff_w2):
    B, SL, D = x.shape
    cfg = Cfg(B=B, SC=ctx.shape[1], SL=SL, D=D, FF=ff_w1.shape[2] * N_CHIP)
    L = N_LAYERS
    n_ex = B * N_DEV
    mcols = mod_w.shape[2]
    weights = dict(c_ctx=c_ctx, mod_w=mod_w, mod_b=mod_b, norm1_g=norm1_g, norm2_g=norm2_g, ab_w_in=ab_w_in,
                   ab_w_out=ab_w_out, ret_decay=ret_decay, att_q_norm_g=att_q_norm_g, att_k_norm_g=att_k_norm_g,
                   cm_w_in=cm_w_in, cm_v_norm_g=cm_v_norm_g, cm_w_s=cm_w_s, cm_b_s=cm_b_s, cm_w_out=cm_w_out,
                   ff_w1=ff_w1, ff_w2=ff_w2)
    m_in = dict(c_ctx=m_c_ctx, mod_w=m_mod_w, mod_b=m_mod_b, norm1_g=m_norm1_g, norm2_g=m_norm2_g, ab_w_in=m_ab_w_in,
                ab_w_out=m_ab_w_out, ret_decay=m_ret_decay, att_q_norm_g=m_att_q_norm_g, att_k_norm_g=m_att_k_norm_g,
                cm_w_in=m_cm_w_in, cm_v_norm_g=m_cm_v_norm_g, cm_w_s=m_cm_w_s, cm_b_s=m_cm_b_s, cm_w_out=m_cm_w_out,
                ff_w1=m_ff_w1, ff_w2=m_ff_w2)
    v_in = dict(c_ctx=v_c_ctx, mod_w=v_mod_w, mod_b=v_mod_b, norm1_g=v_norm1_g, norm2_g=v_norm2_g, ab_w_in=v_ab_w_in,
                ab_w_out=v_ab_w_out, ret_decay=v_ret_decay, att_q_norm_g=v_att_q_norm_g, att_k_norm_g=v_att_k_norm_g,
                cm_w_in=v_cm_w_in, cm_v_norm_g=v_cm_v_norm_g, cm_w_s=v_cm_w_s, cm_b_s=v_cm_b_s, cm_w_out=v_cm_w_out,
                ff_w1=v_ff_w1, ff_w2=v_ff_w2)
    xi, yi, ci = lax.axis_index("x"), lax.axis_index("y"), lax.axis_index("c")
    chip = 2 * xi + yi
    dev = 2 * chip + ci

    shards = {n: [weights[n][i].astype(BF16) for i in range(weights[n].shape[0])] for n in BIG}
    vgw = cm_v_norm_g.shape[1]
    blk = jnp.zeros((8, D), F32).at[:B].set(c).at[B:B + 2, :vgw].set(cm_v_norm_g)
    g0 = _allgather8("gather_c", blk).reshape(N_DEV, 8, D)
    c_all = g0[:, :B].reshape(n_ex, D)
    vg_full = jnp.concatenate([g0[2 * s, B:B + 2, :vgw] for s in range(N_CHIP)], axis=-1)

    pre = jnp.zeros((MOD_ROWS, D), F32).at[:n_ex].set(c_all).at[n_ex].set(c_ctx)
    act = _silu_rows("silu_c", pre)
    mpart = jnp.stack([_mm(f"mod_fwd_{l}", act, mod_w, mode="nn", layer=l, outs=[F32], tn=mcols) for l in range(L)])
    g1, (first_weight,) = _allgather8("gather_mod", mpart.reshape(L * MOD_ROWS, mcols),
                                      carry=_Carry("gather", (shards["ab_w_in"][0],), (BIG["ab_w_in"],)))
    g1 = g1.reshape(N_DEV, L, MOD_ROWS, mcols)
    mod_all = jnp.concatenate([g1[2 * s] for s in range(N_CHIP)], axis=-1) + mod_b[:, None, :]
    mod_lat = lax.dynamic_slice_in_dim(mod_all, dev * B, B, axis=1)
    mod_ctx = jnp.broadcast_to(mod_all[:, n_ex][:, None], mod_lat.shape)
    mods = jnp.stack([mod_ctx, mod_lat], axis=2).reshape(L, B, 2, 6, D)

    w = dict(norm1_g=norm1_g, norm2_g=norm2_g, ret_decay=ret_decay, att_q_norm_g=att_q_norm_g,
             att_k_norm_g=att_k_norm_g, cm_v_norm_g=vg_full, cm_w_s=cm_w_s, cm_b_s=cm_b_s)
    xcat = jnp.concatenate([ctx, x], axis=1).reshape(cfg.T, D)
    loss_local, dx_lat, recv, small, dmods = _local_step(cfg, xcat, loss_target.reshape(B * SL, D), mods, shards, w,
                                                         first_weight)
    loss = lax.psum(loss_local, ("x", "y", "c"))
    grad_x = dx_lat.reshape(B, SL, D)

    part = [_sum_parts_layers(f"sum_{n}", [recv[(n, i)] for i in range(weights[n].shape[0])]) for n in BIG]

    dmod = jnp.stack(dmods).reshape(L, B, 2, 6 * D)
    dmod_lat = dmod[:, :, 1]
    dmod_ctx = jnp.sum(dmod[:, :, 0], axis=1)
    d_ret = jnp.stack(small["ret_lg"]) * jax.nn.sigmoid(-ret_decay)
    summed = [dmod_ctx.reshape(-1), jnp.stack(small["norm1_g"]).reshape(-1), jnp.stack(small["norm2_g"]).reshape(-1),
              jnp.stack(small["cm_v_norm_g"]).reshape(-1), jnp.stack(small["cm_w_s"]).reshape(-1),
              jnp.stack(small["cm_b_s"]).reshape(-1), jnp.stack(small["att_q_norm_g"]).reshape(-1),
              jnp.stack(small["att_k_norm_g"]).reshape(-1), d_ret.reshape(-1)]
    sizes = [int(a.shape[0]) for a in summed]
    flat = jnp.concatenate(summed + [dmod_lat.reshape(-1)])
    n_sum = sum(sizes)
    n_sum_rows = -(-n_sum // D)
    lat_rows = (L * B * 6 * D) // D
    pack_rows = -(-(n_sum_rows + lat_rows) // 8) * 8
    packed = jnp.zeros((pack_rows * D,), F32).at[:n_sum].set(flat[:n_sum])
    packed = packed.at[n_sum_rows * D:(n_sum_rows + lat_rows) * D].set(flat[n_sum:]).reshape(pack_rows, D)
    g2, other = _allgather8("gather_small", packed, carry=_Carry("swap", tuple(part), (0,) * len(part)))
    g2 = g2.reshape(N_DEV, pack_rows, D)
    tot = _sum_leading("sum_small", g2[:, :n_sum_rows]).reshape(-1)
    pieces, off = [], 0
    for sz in sizes:
        pieces.append(tot[off:off + sz])
        off += sz
    dmod_ctx_t, g_n1, g_n2, g_vg, g_ws, g_bs, g_qg, g_kg, g_rd = pieces
    dmod_ctx_t = dmod_ctx_t.reshape(L, 6 * D)
    dmod_lat_all = g2[:, n_sum_rows:n_sum_rows + lat_rows].reshape(N_DEV, L, B, 6 * D)
    dmod_rows = jnp.zeros((L, MOD_ROWS, 6 * D), F32)
    dmod_rows = dmod_rows.at[:, :n_ex].set(jnp.transpose(dmod_lat_all, (1, 0, 2, 3)).reshape(L, n_ex, 6 * D))
    dmod_rows = dmod_rows.at[:, n_ex].set(dmod_ctx_t)
    g_mod_b = _sum_leading("sum_mod_b", jnp.transpose(dmod_rows, (1, 0, 2)))
    dmod_mine = lax.dynamic_slice_in_dim(dmod_rows, chip * mcols, mcols, axis=2)
    g_mod_w = jnp.stack([_mm(f"mod_dw_{l}", act, dmod_mine[l], mode="tn", outs=[F32], tn=mcols) for l in range(L)])
    ctx8 = jnp.zeros((L, 8, mcols), F32).at[:, 0].set(dmod_mine[:, n_ex])
    dcc = [_mm(f"mod_dctx_{l}", ctx8[l], mod_w, mode="nt", layer=l, outs=[F32], tk=mcols) for l in range(L)]
    dcc = _sum_leading("sum_dctx_layers", jnp.stack(dcc))
    g3 = _allgather8("gather_dctx", dcc).reshape(N_DEV, 8, D)
    dcc_t = _sum_leading("sum_dctx_chips", g3[0::2])[0:1]
    g_c_ctx = _silu_bwd_rows("silu_bwd_cctx", c_ctx[None], dcc_t)[0]

    vg_mine = lax.dynamic_slice_in_dim(g_vg.reshape(2, -1), chip * vgw, vgw, axis=1)
    small_g = dict(c_ctx=g_c_ctx, mod_w=g_mod_w, mod_b=g_mod_b, norm1_g=g_n1.reshape(norm1_g.shape),
                   norm2_g=g_n2.reshape(norm2_g.shape), ret_decay=g_rd.reshape(ret_decay.shape),
                   att_q_norm_g=g_qg.reshape(att_q_norm_g.shape), att_k_norm_g=g_kg.reshape(att_k_norm_g.shape),
                   cm_v_norm_g=vg_mine, cm_w_s=g_ws.reshape(cm_w_s.shape), cm_b_s=g_bs.reshape(cm_b_s.shape))
    out = {}
    for n, g in small_g.items():
        out[n] = _adamw(f"adamw_{n}", weights[n], [g], m_in[n], v_in[n])
    for n, p_mine, p_other in zip(BIG, part, other):
        out[n] = _adamw(f"adamw_{n}", weights[n], [p_mine, p_other], m_in[n], v_in[n])

    order = list(weights)
    return (loss, grad_x, *[out[n][0] for n in order], *[out[n][1] for n in order],
            *[out[n][2] for n in order], *[out[n][3] for n in order])
```
